```python
import math
import jax, jax.numpy as jnp
from jax import lax
import numpy as np

D_MODEL = 1024
BATCH = 4
SEQ = 4096
DEPTH = 1

GLA_HEADS = 4
GLA_DK = D_MODEL // 2 // GLA_HEADS
GLA_DV = D_MODEL // GLA_HEADS
GLA_LOWRANK = 16
GLA_TAU = 16.0
GLA_CHUNK = 64
FNET_GROUPS = 4
FNET_GW = D_MODEL // 8
MEM_LEN = 256
MEM_HEADS = 4
MEM_HD = D_MODEL // 8
N_BRANCH = 3
N_EXPERTS = 32
TOP_K = 4
D_FF = D_MODEL
SWIGLU_LIMIT = 7.0
SWIGLU_ALPHA = 1.702
MOE_BLOCK = 256
LN_EPS = 1e-5
RMS_EPS = 1e-6
DN_ALPHA = (2.0 * DEPTH) ** 0.25
DN_BETA = (8.0 * DEPTH) ** -0.25

QK_W = GLA_HEADS * GLA_DK
V_W = GLA_HEADS * GLA_DV
R_W = V_W
FN_W = FNET_GROUPS * FNET_GW
MQ_W = MEM_HEADS * MEM_HD
GATE_W = N_BRANCH * D_MODEL
IN_SIZES = (QK_W, QK_W, V_W, R_W, GLA_LOWRANK, GLA_LOWRANK, FN_W, MQ_W, GATE_W)
IN_WIDTH = sum(IN_SIZES)

kernel_name = "hybrid_gla_fnet_memxattn_moe_encoder"


def layer_norm(x, g, b):
    xf = x.astype(jnp.float32)
    mu = jnp.mean(xf, axis=-1, keepdims=True)
    var = jnp.mean(jnp.square(xf - mu), axis=-1, keepdims=True)
    return ((xf - mu) * lax.rsqrt(var + LN_EPS) * g + b).astype(x.dtype)


def gla_chunked(q, k, v, log_a):
    B, S, H, DK = q.shape
    DV = v.shape[-1]
    C = GLA_CHUNK
    N = S // C

    def chunk(t):
        return t.astype(jnp.float32).reshape(B, N, C, H, t.shape[-1]).transpose(1, 0, 3, 2, 4)

    qc, kc, vc, ac = chunk(q), chunk(k), chunk(v), chunk(log_a)
    bcum = jnp.cumsum(ac, axis=-2)
    blast = bcum[..., -1:, :]
    q_in = qc * jnp.exp(bcum)
    k_in = kc * jnp.exp(-bcum)
    k_st = kc * jnp.exp(blast - bcum)
    mask = jnp.tril(jnp.ones((C, C), dtype=bool))
    att = jnp.where(mask, jnp.einsum('nbhik,nbhjk->nbhij', q_in, k_in), 0.0)
    o_intra = jnp.einsum('nbhij,nbhjv->nbhiv', att, vc)

    def step(state, inp):
        q_i, k_s, v_i, dec = inp
        o = jnp.einsum('bhik,bhkv->bhiv', q_i, state)
        state = state * jnp.exp(dec[..., 0, :])[..., None] + jnp.einsum('bhjk,bhjv->bhkv', k_s, v_i)
        return state, o

    state0 = jnp.zeros((B, H, DK, DV), jnp.float32)
    _, o_inter = lax.scan(step, state0, (q_in, k_st, vc, blast))
    o = o_intra + o_inter
    return o.transpose(1, 0, 3, 2, 4).reshape(B, S, H, DV)


def mixer_branches(u, mem_n, w_in, b_in, w_decay_f, b_decay_f, w_decay_b, b_decay_b,
                   gla_norm_g, w_br_gla, w_br_fnet, w_br_mem, w_mem_kv, w_out, b_out):
    B, S, D = u.shape
    dt = u.dtype
    proj = u @ w_in + b_in
    splits = np.cumsum(IN_SIZES)[:-1].tolist()
    q, k, v, r, lr_f, lr_b, fn, mq, gates = jnp.split(proj, splits, axis=-1)

    def heads(t, d):
        return t.reshape(B, S, GLA_HEADS, d)
    qh = heads(q, GLA_DK) * (GLA_DK ** -0.5)
    kh = heads(k, GLA_DK)
    vh = heads(v, GLA_DV)
    la_f = heads(jax.nn.log_sigmoid((lr_f @ w_decay_f + b_decay_f).astype(jnp.float32)) / GLA_TAU, GLA_DK)
    la_b = heads(jax.nn.log_sigmoid((lr_b @ w_decay_b + b_decay_b).astype(jnp.float32)) / GLA_TAU, GLA_DK)
    flip = lambda t: jnp.flip(t, axis=1)
    o_f = gla_chunked(qh, kh, vh, la_f)
    o_b = flip(gla_chunked(flip(qh), flip(kh), flip(vh), flip(la_b)))
    o = o_f + o_b
    o = o * lax.rsqrt(jnp.mean(jnp.square(o), axis=-1, keepdims=True) + RMS_EPS) * gla_norm_g
    o = (o * jax.nn.silu(heads(r, GLA_DV).astype(jnp.float32))).astype(dt).reshape(B, S, V_W)
    y_gla = o @ w_br_gla

    f = fn.reshape(B, S, FNET_GROUPS, FNET_GW).astype(jnp.float32)
    f = jnp.fft.fft2(f, axes=(1, 3), norm='ortho').real
    y_fnet = f.astype(dt).reshape(B, S, FN_W) @ w_br_fnet

    kv = mem_n @ w_mem_kv
    mk, mv = jnp.split(kv, 2, axis=-1)
    M = mem_n.shape[1]
    qm = mq.reshape(B, S, MEM_HEADS, MEM_HD)
    mk = mk.reshape(B, M, MEM_HEADS, MEM_HD)
    mv = mv.reshape(B, M, MEM_HEADS, MEM_HD)
    s = jnp.einsum('bshd,bmhd->bhsm', qm, mk).astype(jnp.float32) * (MEM_HD ** -0.5)
    p = jax.nn.softmax(s, axis=-1).astype(dt)
    om = jnp.einsum('bhsm,bmhd->bshd', p, mv).reshape(B, S, MQ_W)
    y_mem = om @ w_br_mem

    g = jax.nn.sigmoid(gates.astype(jnp.float32)).reshape(B, S, N_BRANCH, D)
    merged = (g[:, :, 0] * y_gla.astype(jnp.float32)
              + g[:, :, 1] * y_fnet.astype(jnp.float32)
              + g[:, :, 2] * y_mem.astype(jnp.float32)).astype(dt)
    return merged @ w_out + b_out


def moe_ffn(h, w_router, b_router, w_gu, b_gu, w_down, b_down):
    B, S, D = h.shape
    T = B * S
    A = T * TOP_K
    hf = h.reshape(T, D)
    logits = (hf @ w_router + b_router).astype(jnp.float32)
    top_val, top_idx = lax.top_k(logits, TOP_K)
    top_w = jax.nn.softmax(top_val, axis=-1)
    flat_e = top_idx.reshape(A).astype(jnp.int32)
    flat_tok = jnp.repeat(jnp.arange(T, dtype=jnp.int32), TOP_K)
    flat_w = top_w.reshape(A)
    order = jnp.argsort(flat_e)
    sorted_e = flat_e[order]
    counts = jnp.bincount(flat_e, length=N_EXPERTS)
    start = jnp.cumsum(counts) - counts
    padded = (counts + MOE_BLOCK - 1) // MOE_BLOCK * MOE_BLOCK
    pad_end = jnp.cumsum(padded)
    pad_start = pad_end - padded
    dest = pad_start[sorted_e] + (jnp.arange(A, dtype=jnp.int32) - start[sorted_e])
    n_blocks = (A + N_EXPERTS * (MOE_BLOCK - 1) + MOE_BLOCK - 1) // MOE_BLOCK
    P = n_blocks * MOE_BLOCK
    buf_tok = jnp.zeros((P,), jnp.int32).at[dest].set(flat_tok[order])
    buf_w = jnp.zeros((P,), jnp.float32).at[dest].set(flat_w[order])
    block_e = jnp.minimum(
        jnp.searchsorted(pad_end, jnp.arange(n_blocks, dtype=pad_end.dtype) * MOE_BLOCK, side='right'),
        N_EXPERTS - 1)

    def block_fn(inp):
        tok, wgt, e = inp
        xb = hf[tok]
        gu = xb @ w_gu[e] + b_gu[e]
        gate, up = jnp.split(gu, 2, axis=-1)
        gate = jnp.minimum(gate, SWIGLU_LIMIT)
        up = jnp.clip(up, -SWIGLU_LIMIT, SWIGLU_LIMIT)
        act = (up + 1.0) * (gate * jax.nn.sigmoid(SWIGLU_ALPHA * gate))
        out = act @ w_down[e] + b_down[e]
        return out * wgt[:, None].astype(out.dtype)

    outs = lax.map(block_fn, (buf_tok.reshape(n_blocks, MOE_BLOCK),
                              buf_w.reshape(n_blocks, MOE_BLOCK), block_e))
    y = jnp.zeros((T, D), h.dtype).at[buf_tok].add(outs.reshape(P, D).astype(h.dtype))
    return y.reshape(B, S, D)


def setup_inputs(seed: int = 0) -> dict:
    key = jax.random.key(seed)
    keys = jax.random.split(key, 40)
    ctr = [0]

    def nrm(shape, scale):
        k_ = keys[ctr[0]]
        ctr[0] += 1
        return jax.random.normal(k_, shape, jnp.float32) * scale

    L, D = DEPTH, D_MODEL
    x = nrm((BATCH, SEQ, D), 1.0)
    mem = nrm((BATCH, MEM_LEN, D), 1.0)
    ln_in_g = 1.0 + nrm((D,), 0.02)
    ln_in_b = nrm((D,), 0.01)
    ln_mem_g = 1.0 + nrm((D,), 0.02)
    ln_mem_b = nrm((D,), 0.01)
    col_scale = jnp.concatenate([jnp.ones((2 * QK_W,), jnp.float32),
                                 jnp.full((V_W,), DN_BETA, jnp.float32),
                                 jnp.ones((IN_WIDTH - 2 * QK_W - V_W,), jnp.float32)])
    w_in = nrm((L, D, IN_WIDTH), D ** -0.5) * col_scale
    b_in = nrm((L, IN_WIDTH), 0.01)
    w_decay_f = nrm((L, GLA_LOWRANK, QK_W), GLA_LOWRANK ** -0.5)
    b_decay_f = nrm((L, QK_W), 0.1)
    w_decay_b = nrm((L, GLA_LOWRANK, QK_W), GLA_LOWRANK ** -0.5)
    b_decay_b = nrm((L, QK_W), 0.1)
    gla_norm_g = 1.0 + nrm((L, GLA_DV), 0.02)
    w_br_gla = nrm((L, V_W, D), V_W ** -0.5)
    w_br_fnet = nrm((L, FN_W, D), FN_W ** -0.5)
    w_br_mem = nrm((L, MQ_W, D), MQ_W ** -0.5)
    kv_scale = jnp.concatenate([jnp.ones((MQ_W,), jnp.float32), jnp.full((MQ_W,), DN_BETA, jnp.float32)])
    w_mem_kv = nrm((L, D, 2 * MQ_W), D ** -0.5) * kv_scale
    w_out = nrm((L, D, D), D ** -0.5) * DN_BETA
    b_out = nrm((L, D), 0.01)
    ln1_g = 1.0 + nrm((L, D), 0.02)
    ln1_b = nrm((L, D), 0.01)
    w_router = nrm((L, D, N_EXPERTS), D ** -0.5)
    b_router = nrm((L, N_EXPERTS), 0.01)
    w_gu = nrm((L, N_EXPERTS, D, 2 * D_FF), D ** -0.5)
    b_gu = nrm((L, N_EXPERTS, 2 * D_FF), 0.01)
    w_down = nrm((L, N_EXPERTS, D_FF, D), D_FF ** -0.5) * DN_BETA
    b_down = nrm((L, N_EXPERTS, D), 0.01)
    ln2_g = 1.0 + nrm((L, D), 0.02)
    ln2_b = nrm((L, D), 0.01)
    return {"x": x, "mem": mem, "ln_in_g": ln_in_g, "ln_in_b": ln_in_b,
            "ln_mem_g": ln_mem_g, "ln_mem_b": ln_mem_b, "w_in": w_in, "b_in": b_in,
            "w_decay_f": w_decay_f, "b_decay_f": b_decay_f, "w_decay_b": w_decay_b,
            "b_decay_b": b_decay_b, "gla_norm_g": gla_norm_g, "w_br_gla": w_br_gla,
            "w_br_fnet": w_br_fnet, "w_br_mem": w_br_mem, "w_mem_kv": w_mem_kv,
            "w_out": w_out, "b_out": b_out, "ln1_g": ln1_g, "ln1_b": ln1_b,
            "w_router": w_router, "b_router": b_router, "w_gu": w_gu, "b_gu": b_gu,
            "w_down": w_down, "b_down": b_down, "ln2_g": ln2_g, "ln2_b": ln2_b}


def reference(x, mem, ln_in_g, ln_in_b, ln_mem_g, ln_mem_b, w_in, b_in,
              w_decay_f, b_decay_f, w_decay_b, b_decay_b, gla_norm_g, w_br_gla,
              w_br_fnet, w_br_mem, w_mem_kv, w_out, b_out, ln1_g, ln1_b,
              w_router, b_router, w_gu, b_gu, w_down, b_down, ln2_g, ln2_b):
    h = layer_norm(x, ln_in_g, ln_in_b)
    mem_n = layer_norm(mem, ln_mem_g, ln_mem_b)
    for l in range(DEPTH):
        mix = mixer_branches(h, mem_n, w_in[l], b_in[l], w_decay_f[l], b_decay_f[l],
                             w_decay_b[l], b_decay_b[l], gla_norm_g[l], w_br_gla[l],
                             w_br_fnet[l], w_br_mem[l], w_mem_kv[l], w_out[l], b_out[l])
        h = layer_norm(DN_ALPHA * h + mix, ln1_g[l], ln1_b[l])
        ff = moe_ffn(h, w_router[l], b_router[l], w_gu[l], b_gu[l], w_down[l], b_down[l])
        h = layer_norm(DN_ALPHA * h + ff, ln2_g[l], ln2_b[l])
    return h
```

```python
import functools
import math

import numpy as np
import jax
import jax.numpy as jnp
from jax import lax
from jax.experimental import pallas as pl
from jax.experimental.pallas import tpu as pltpu

F32 = jnp.float32
BF16 = jnp.bfloat16
I32 = jnp.int32
U32 = jnp.uint32

D = 1024
BATCH = 4
SEQ = 4096
T = BATCH * SEQ
GLA_H = 4
GLA_DK = 128
GLA_DV = 256
GLA_LR = 16
GLA_TAU = 16.0
GLA_C = 64
FN_G = 4
FN_GW = 128
FN_W = 512
MEM_LEN = 256
MEM_H = 4
MEM_HD = 128
MQ_W = 512
N_EXP = 32
TOP_K = 4
D_FF = 1024
SW_LIMIT = 7.0
SW_ALPHA = 1.702
LN_EPS = 1e-5
RMS_EPS = 1e-6
DN_ALPHA = 2.0 ** 0.25
A_ROWS = T * TOP_K

FFT_N1 = 128
FFT_N2 = 32

LANES = 128
NEG_BIG = -1e30


def _ln(x, g, b):
    mu = jnp.mean(x, axis=-1, keepdims=True)
    xc = x - mu
    var = jnp.mean(xc * xc, axis=-1, keepdims=True)
    return xc * lax.rsqrt(var + LN_EPS) * g + b


def _dot(a, b):
    return jnp.dot(a, b, preferred_element_type=F32)


def _dot_nt(a, b):
    return lax.dot_general(a, b, (((1,), (1,)), ((), ())), preferred_element_type=F32)


def _dot_tn(a, b):
    return lax.dot_general(a, b, (((0,), (0,)), ((), ())), preferred_element_type=F32)


INPROJ_TM = 1024
INPROJ_TN = 1024


def _inproj_kernel(x_ref, g_ref, b_ref, w_ref, bias_ref, wlr_ref, blr_ref,
                   qk_ref, v_ref, r_ref, fm_ref, gates_ref, lr_ref, hb_ref):
    j = pl.program_id(1)

    @pl.when(j == 0)
    def _():
        hb = _ln(x_ref[...], g_ref[...], b_ref[...]).astype(BF16)
        hb_ref[...] = hb
        lr_ref[...] = _dot(hb, wlr_ref[...]) + blr_ref[...]

    y = (_dot(hb_ref[...], w_ref[...]) + bias_ref[...]).astype(BF16)
    for idx, ref in enumerate((qk_ref, v_ref, r_ref, fm_ref)):
        @pl.when(j == idx)
        def _(ref=ref):
            ref[...] = y

    @pl.when(j >= 4)
    def _():
        gates_ref[...] = y


def _inproj(x2, ln_g, ln_b, w_main, b_main, w_lr, b_lr):
    tm, tn = INPROJ_TM, INPROJ_TN
    nj = w_main.shape[1] // tn
    row = lambda i, j: (i, 0)
    outs = (
        jax.ShapeDtypeStruct((T, 1024), BF16),
        jax.ShapeDtypeStruct((T, 1024), BF16),
        jax.ShapeDtypeStruct((T, 1024), BF16),
        jax.ShapeDtypeStruct((T, 1024), BF16),
        jax.ShapeDtypeStruct((T, 3072), BF16),
        jax.ShapeDtypeStruct((T, LANES), F32),
    )
    return pl.pallas_call(
        _inproj_kernel,
        grid=(T // tm, nj),
        in_specs=[
            pl.BlockSpec((tm, D), row),
            pl.BlockSpec((1, D), lambda i, j: (0, 0)),
            pl.BlockSpec((1, D), lambda i, j: (0, 0)),
            pl.BlockSpec((D, tn), lambda i, j: (0, j)),
            pl.BlockSpec((1, tn), lambda i, j: (0, j)),
            pl.BlockSpec((D, LANES), lambda i, j: (0, 0)),
            pl.BlockSpec((1, LANES), lambda i, j: (0, 0)),
        ],
        out_specs=[
            pl.BlockSpec((tm, tn), row),
            pl.BlockSpec((tm, tn), row),
            pl.BlockSpec((tm, tn), row),
            pl.BlockSpec((tm, tn), row),
            pl.BlockSpec((tm, tn), lambda i, j: (i, jnp.maximum(j - 4, 0))),
            pl.BlockSpec((tm, LANES), row),
        ],
        out_shape=outs,
        scratch_shapes=[pltpu.VMEM((tm, D), BF16)],
        compiler_params=pltpu.CompilerParams(
            dimension_semantics=("arbitrary", "arbitrary"),
            vmem_limit_bytes=48 * 1024 * 1024),
        name="ln_inproj",
    )(x2, ln_g, ln_b, w_main, b_main, w_lr, b_lr)


GLA_BULK = 512
GLA_NCH = SEQ // GLA_C


def _gla_kernel(q_ref, k_ref, v_ref, r_ref, lr_ref, wdf_ref, bdf_ref, wdb_ref, bdb_ref,
                g_ref, o_ref, acc_ref, qin_ref, kin_ref, kst_ref, dec_ref):
    C = GLA_C
    G = GLA_BULK
    scale = GLA_DK ** -0.5
    rowi = lax.broadcasted_iota(I32, (G, GLA_DK), 0) & (C - 1)
    ii = lax.broadcasted_iota(I32, (C, C), 0)
    jj = lax.broadcasted_iota(I32, (C, C), 1)

    def run_direction(rev, wd_ref, bd_ref):
        def bulk(gi, carry):
            r0 = pl.multiple_of(gi * G, G)
            rows = pl.ds(r0, G)
            z = _dot(lr_ref[rows, :], wd_ref[...]) + bd_ref[...]
            la = -(jnp.maximum(-z, 0.0) + jnp.log(1.0 + jnp.exp(-jnp.abs(z)))) * (1.0 / GLA_TAU)
            c = la
            for s in (1, 2, 4, 8, 16, 32):
                c = c + jnp.where(rowi >= s, pltpu.roll(c, s, 0), 0.0)
            qf = q_ref[rows, :].astype(F32)
            kf = k_ref[rows, :].astype(F32)
            for ci in range(G // C):
                sl = slice(ci * C, (ci + 1) * C)
                cc = c[sl]
                blast = cc[C - 1:C, :]
                b = (blast - cc + la[sl]) if rev else cc
                rr = pl.ds(r0 + ci * C, C)
                qin_ref[rr, :] = (qf[sl] * (scale * jnp.exp(b))).astype(BF16)
                kin_ref[rr, :] = (kf[sl] * jnp.exp(-b)).astype(BF16)
                kst_ref[rr, :] = (kf[sl] * jnp.exp(blast - b)).astype(BF16)
                dec_ref[pl.ds(gi * (G // C) + ci, 1), :] = jnp.exp(blast)
            return carry

        lax.fori_loop(0, SEQ // G, bulk, 0)

        mask = (ii <= jj) if rev else (ii >= jj)

        def step(i, st):
            n = (GLA_NCH - 1 - i) if rev else i
            rows = pl.ds(pl.multiple_of(n * C, C), C)
            qi = qin_ref[rows, :]
            ki = kin_ref[rows, :]
            ks = kst_ref[rows, :]
            vi = v_ref[rows, :]
            att = jnp.where(mask, _dot_nt(qi, ki), 0.0)
            o = _dot(att.astype(BF16), vi) + _dot_nt(qi, st.astype(BF16))
            if rev:
                acc_ref[rows, :] += o
            else:
                acc_ref[rows, :] = o
            return st * dec_ref[pl.ds(n, 1), :] + _dot_tn(vi, ks)

        lax.fori_loop(0, GLA_NCH, step, jnp.zeros((GLA_DV, GLA_DK), F32), unroll=2)

    run_direction(False, wdf_ref, bdf_ref)
    run_direction(True, wdb_ref, bdb_ref)

    def fin(gi, carry):
        rows = pl.ds(pl.multiple_of(gi * G, G), G)
        o = acc_ref[rows, :]
        o = o * lax.rsqrt(jnp.mean(o * o, axis=-1, keepdims=True) + RMS_EPS) * g_ref[...]
        rg = r_ref[rows, :].astype(F32)
        o_ref[rows, :] = (o * (rg * jax.nn.sigmoid(rg))).astype(BF16)
        return carry

    lax.fori_loop(0, SEQ // G, fin, 0)


def _gla(qk, v, r, lr, wdf, bdf, wdb, bdb, g):
    return pl.pallas_call(
        _gla_kernel,
        grid=(BATCH, GLA_H),
        in_specs=[
            pl.BlockSpec((SEQ, GLA_DK), lambda b, h: (b, h)),
            pl.BlockSpec((SEQ, GLA_DK), lambda b, h: (b, GLA_H + h)),
            pl.BlockSpec((SEQ, GLA_DV), lambda b, h: (b, h)),
            pl.BlockSpec((SEQ, GLA_DV), lambda b, h: (b, h)),
            pl.BlockSpec((SEQ, LANES), lambda b, h: (b, 0)),
            pl.BlockSpec((LANES, GLA_DK), lambda b, h: (0, h)),
            pl.BlockSpec((1, GLA_DK), lambda b, h: (0, h)),
            pl.BlockSpec((LANES, GLA_DK), lambda b, h: (0, h)),
            pl.BlockSpec((1, GLA_DK), lambda b, h: (0, h)),
            pl.BlockSpec((1, GLA_DV), lambda b, h: (0, 0)),
        ],
        out_specs=pl.BlockSpec((SEQ, GLA_DV), lambda b, h: (b, h)),
        out_shape=jax.ShapeDtypeStruct((T, GLA_H * GLA_DV), BF16),
        scratch_shapes=[
            pltpu.VMEM((SEQ, GLA_DV), F32),
            pltpu.VMEM((SEQ, GLA_DK), BF16),
            pltpu.VMEM((SEQ, GLA_DK), BF16),
            pltpu.VMEM((SEQ, GLA_DK), BF16),
            pltpu.VMEM((GLA_NCH, GLA_DK), F32),
        ],
        compiler_params=pltpu.CompilerParams(
            dimension_semantics=("arbitrary", "arbitrary"),
            vmem_limit_bytes=48 * 1024 * 1024),
        name="gla",
    )(qk, qk, v, r, lr, wdf, bdf, wdb, bdb, g)


FFT1_M = 16
FFT2_TN = 2048


def _fft1_kernel(x_ref, f1_ref, cw_ref, sw_ref, o_ref):
    f1 = f1_ref[...]
    for l in range(FFT1_M):
        xl = x_ref[:, l * 1024:l * 1024 + FN_W]
        a = _dot(f1, xl)
        ar = a[:FFT_N2]
        ai = a[FFT_N2:]
        cw = jnp.concatenate([cw_ref[l]] * (FN_W // LANES), axis=1)
        sw = jnp.concatenate([sw_ref[l]] * (FN_W // LANES), axis=1)
        o_ref[0, l] = (ar * cw + ai * sw).astype(BF16)
        o_ref[1, l] = (ai * cw - ar * sw).astype(BF16)


def _fft1(fm, f1, cwb, swb):
    xv = fm.reshape(BATCH * FFT_N2, FFT_N1 * 1024)
    m = FFT1_M
    return pl.pallas_call(
        _fft1_kernel,
        grid=(BATCH, FFT_N1 // m),
        in_specs=[
            pl.BlockSpec((FFT_N2, m * 1024), lambda b, j: (b, j)),
            pl.BlockSpec((2 * FFT_N2, FFT_N2), lambda b, j: (0, 0)),
            pl.BlockSpec((m, FFT_N2, LANES), lambda b, j: (j, 0, 0)),
            pl.BlockSpec((m, FFT_N2, LANES), lambda b, j: (j, 0, 0)),
        ],
        out_specs=pl.BlockSpec((None, 2, m, FFT_N2, FN_W), lambda b, j: (b, 0, j, 0, 0)),
        out_shape=jax.ShapeDtypeStruct((BATCH, 2, FFT_N1, FFT_N2, FN_W), BF16),
        compiler_params=pltpu.CompilerParams(
            dimension_semantics=("arbitrary", "arbitrary")),
        name="fft_stage1",
    )(xv, f1, cwb, swb)


def _fft2_kernel(d_ref, f2_ref, o_ref):
    z = _dot(f2_ref[...], d_ref[...])
    o_ref[0] = z[:FFT_N1].astype(BF16)
    o_ref[1] = z[FFT_N1:].astype(BF16)


def _fft2(bout, f2):
    dv = bout.reshape(BATCH, 2 * FFT_N1, FFT_N2 * FN_W)
    tn = FFT2_TN
    z = pl.pallas_call(
        _fft2_kernel,
        grid=(BATCH, FFT_N2 * FN_W // tn),
        in_specs=[
            pl.BlockSpec((None, 2 * FFT_N1, tn), lambda b, j: (b, 0, j)),
            pl.BlockSpec((2 * FFT_N1, 2 * FFT_N1), lambda b, j: (0, 0)),
        ],
        out_specs=pl.BlockSpec((None, 2, FFT_N1, tn), lambda b, j: (b, 0, 0, j)),
        out_shape=jax.ShapeDtypeStruct((BATCH, 2, FFT_N1, FFT_N2 * FN_W), BF16),
        compiler_params=pltpu.CompilerParams(
            dimension_semantics=("arbitrary", "arbitrary")),
        name="fft_stage2",
    )(dv, f2)
    return z.reshape(BATCH, 2, SEQ, FN_W)


def _dft_tables():
    n2 = np.arange(FFT_N2, dtype=np.float64)
    n1 = np.arange(FFT_N1, dtype=np.float64)
    th = 2.0 * np.pi * np.outer(n2, n2) / FFT_N2
    f1 = np.concatenate([np.cos(th), -np.sin(th)], axis=0) / math.sqrt(SEQ)
    tw = 2.0 * np.pi * np.outer(n1, n2) / SEQ
    cwb = np.broadcast_to(np.cos(tw)[:, :, None], (FFT_N1, FFT_N2, LANES))
    swb = np.broadcast_to(np.sin(tw)[:, :, None], (FFT_N1, FFT_N2, LANES))
    th1 = 2.0 * np.pi * np.outer(n1, n1) / FFT_N1
    c1, s1 = np.cos(th1), np.sin(th1)
    f2 = np.block([[c1, s1], [-s1, c1]])
    cc = np.arange(FN_GW, dtype=np.float64)
    thc = 2.0 * np.pi * np.outer(cc, cc) / FN_GW
    ccs = np.concatenate([np.cos(thc), np.sin(thc)], axis=0) / math.sqrt(FN_GW)
    as32 = lambda a: jnp.asarray(np.ascontiguousarray(a), dtype=F32)
    return as32(f1).astype(BF16), as32(cwb), as32(swb), as32(f2).astype(BF16), as32(ccs).astype(BF16)


def _memkv_kernel(m_ref, g_ref, b_ref, w_ref, o_ref):
    mn = _ln(m_ref[...], g_ref[...], b_ref[...]).astype(BF16)
    o_ref[...] = _dot(mn, w_ref[...]).astype(BF16)


def _memkv(mem2, g, b, w):
    return pl.pallas_call(
        _memkv_kernel,
        grid=(BATCH,),
        in_specs=[
            pl.BlockSpec((MEM_LEN, D), lambda i: (i, 0)),
            pl.BlockSpec((1, D), lambda i: (0, 0)),
            pl.BlockSpec((1, D), lambda i: (0, 0)),
            pl.BlockSpec((D, 2 * MQ_W), lambda i: (0, 0)),
        ],
        out_specs=pl.BlockSpec((MEM_LEN, 2 * MQ_W), lambda i: (i, 0)),
        out_shape=jax.ShapeDtypeStruct((BATCH * MEM_LEN, 2 * MQ_W), BF16),
        compiler_params=pltpu.CompilerParams(dimension_semantics=("arbitrary",)),
        name="mem_kv",
    )(mem2, g, b, w)


MERGE_TM = 256


def _pack_bf16_pair(v):
    n = v.shape[1] // 2
    bits = lax.bitcast_convert_type(v.astype(BF16).astype(F32), U32)
    return (bits[:, n:] & jnp.uint32(0xFFFF0000)) | (bits[:, :n] >> 16)


def _unpack_bf16_pair(p):
    lo = lax.bitcast_convert_type(p << 16, F32)
    hi = lax.bitcast_convert_type(p & jnp.uint32(0xFFFF0000), F32)
    return lo, hi


def _merge_kernel(x_ref, og_ref, zr_ref, zi_ref, fm_ref, gt_ref, kv_ref,
                  lng_ref, lnb_ref, wg_ref, ccs_ref, wf_ref, wm_ref, wo_ref, bo_ref,
                  l1g_ref, l1b_ref, wr_ref, br_ref,
                  h1_ref, h1p_ref, eidx_ref, topw_ref):
    tm = MERGE_TM
    h = _ln(x_ref[...], lng_ref[...], lnb_ref[...])
    y_gla = _dot(og_ref[...], wg_ref[...])

    ys = []
    for g in range(FN_G):
        sl = slice(g * FN_GW, (g + 1) * FN_GW)
        zz = jnp.concatenate([zr_ref[:, sl], zi_ref[:, sl]], axis=1)
        ys.append(_dot(zz, ccs_ref[...]))
    y_fn = _dot(jnp.concatenate(ys, axis=1).astype(BF16), wf_ref[...])

    oms = []
    for hd in range(MEM_H):
        sl = slice(hd * MEM_HD, (hd + 1) * MEM_HD)
        qh = fm_ref[:, FN_W + hd * MEM_HD:FN_W + (hd + 1) * MEM_HD]
        s = _dot_nt(qh, kv_ref[:, sl]) * (MEM_HD ** -0.5)
        s = s - jnp.max(s, axis=-1, keepdims=True)
        p = jnp.exp(s)
        p = p / jnp.sum(p, axis=-1, keepdims=True)
        oms.append(_dot(p.astype(BF16), kv_ref[:, MQ_W + hd * MEM_HD:MQ_W + (hd + 1) * MEM_HD]))
    y_mem = _dot(jnp.concatenate(oms, axis=1).astype(BF16), wm_ref[...])

    merged = (jax.nn.sigmoid(gt_ref[:, 0:D].astype(F32)) * y_gla
              + jax.nn.sigmoid(gt_ref[:, D:2 * D].astype(F32)) * y_fn
              + jax.nn.sigmoid(gt_ref[:, 2 * D:3 * D].astype(F32)) * y_mem)
    mix = _dot(merged.astype(BF16), wo_ref[...]) + bo_ref[...]
    h1 = _ln(DN_ALPHA * h + mix, l1g_ref[...], l1b_ref[...])
    h1_ref[...] = h1
    h1p_ref[...] = _pack_bf16_pair(h1)

    logits = jnp.dot(h1, wr_ref[...], preferred_element_type=F32,
                     precision=lax.Precision.HIGHEST) + br_ref[...]
    lane = lax.broadcasted_iota(I32, (tm, LANES), 1)
    l = logits
    vals, idxs = [], []
    for _ in range(TOP_K):
        m = jnp.max(l, axis=-1, keepdims=True)
        idx = jnp.min(jnp.where(l == m, lane, LANES), axis=-1, keepdims=True)
        vals.append(m)
        idxs.append(idx)
        l = jnp.where(lane == idx, -jnp.inf, l)
    es = [jnp.exp(v - vals[0]) for v in vals]
    den = es[0] + es[1] + es[2] + es[3]
    eo = jnp.zeros((tm, LANES), I32)
    wo = jnp.zeros((tm, LANES), F32)
    for k in range(TOP_K):
        eo = jnp.where(lane == k, idxs[k], eo)
        wo = jnp.where(lane == k, es[k] / den, wo)
    eidx_ref[...] = eo
    topw_ref[...] = wo


def _merge(x2, og, z, fm, gates, kv, lng, lnb, wg, ccs, wf, wm, wo, bo, l1g, l1b, wr, br):
    tm = MERGE_TM
    per_b = SEQ // tm
    row = lambda i: (i, 0)
    const = lambda i: (0, 0)
    outs = (
        jax.ShapeDtypeStruct((T, D), F32),
        jax.ShapeDtypeStruct((T, D // 2), U32),
        jax.ShapeDtypeStruct((T, LANES), I32),
        jax.ShapeDtypeStruct((T, LANES), F32),
    )
    return pl.pallas_call(
        _merge_kernel,
        grid=(T // tm,),
        in_specs=[
            pl.BlockSpec((tm, D), row),
            pl.BlockSpec((tm, D), row),
            pl.BlockSpec((None, None, tm, FN_W), lambda i: (i // per_b, 0, i % per_b, 0)),
            pl.BlockSpec((None, None, tm, FN_W), lambda i: (i // per_b, 1, i % per_b, 0)),
            pl.BlockSpec((tm, D), row),
            pl.BlockSpec((tm, 3 * D), row),
            pl.BlockSpec((MEM_LEN, 2 * MQ_W), lambda i: (i // per_b, 0)),
            pl.BlockSpec((1, D), const), pl.BlockSpec((1, D), const),
            pl.BlockSpec((D, D), const),
            pl.BlockSpec((2 * FN_GW, FN_GW), const),
            pl.BlockSpec((FN_W, D), const),
            pl.BlockSpec((MQ_W, D), const),
            pl.BlockSpec((D, D), const),
            pl.BlockSpec((1, D), const),
            pl.BlockSpec((1, D), const), pl.BlockSpec((1, D), const),
            pl.BlockSpec((D, LANES), const),
            pl.BlockSpec((1, LANES), const),
        ],
        out_specs=[
            pl.BlockSpec((tm, D), row),
            pl.BlockSpec((tm, D // 2), row),
            pl.BlockSpec((tm, LANES), row),
            pl.BlockSpec((tm, LANES), row),
        ],
        out_shape=outs,
        compiler_params=pltpu.CompilerParams(
            dimension_semantics=("arbitrary",),
            vmem_limit_bytes=48 * 1024 * 1024),
        name="merge_ln1_router",
    )(x2, og, z, z, fm, gates, kv, lng, lnb, wg, ccs, wf, wm, wo, bo, l1g, l1b, wr, br)


PLAN_TP = 512


def _plan_kernel(e_ref, dest_ref, cnt_out_ref, cnt_ref, off_ref):
    p = pl.program_id(0)
    i = pl.program_id(1)
    tp = PLAN_TP
    lane = lax.broadcasted_iota(I32, (tp, LANES), 1)
    e = e_ref[...]
    onehots = [lane == e[:, k:k + 1] for k in range(TOP_K)]
    mf = jnp.zeros((tp, LANES), F32)
    for oh in onehots:
        mf = mf + jnp.where(oh, 1.0, 0.0)
    colsum = jnp.sum(mf, axis=0, keepdims=True)

    @pl.when(jnp.logical_and(p == 0, i == 0))
    def _():
        cnt_ref[...] = jnp.zeros_like(cnt_ref)

    @pl.when(jnp.logical_and(p == 1, i == 0))
    def _():
        tot = cnt_ref[...]
        cnt_out_ref[...] = jnp.broadcast_to(tot, cnt_out_ref.shape)
        lane1 = lax.broadcasted_iota(I32, (1, LANES), 1)
        inc = tot
        for s in (1, 2, 4, 8, 16, 32, 64):
            inc = inc + jnp.where(lane1 >= s, pltpu.roll(inc, s, 1), 0.0)
        off_ref[...] = inc - tot
        cnt_ref[...] = jnp.zeros_like(cnt_ref)

    @pl.when(p == 0)
    def _():
        dest_ref[...] = jnp.zeros_like(dest_ref)

    @pl.when(p == 1)
    def _():
        ri = lax.broadcasted_iota(I32, (tp, tp), 0)
        ci = lax.broadcasted_iota(I32, (tp, tp), 1)
        ltri = jnp.where(ri > ci, 1.0, 0.0).astype(BF16)
        rank = _dot(ltri, mf.astype(BF16)) + cnt_ref[...] + off_ref[...]
        out = jnp.zeros((tp, LANES), I32)
        for k in range(TOP_K):
            dk = jnp.sum(jnp.where(onehots[k], rank, 0.0), axis=-1, keepdims=True)
            out = jnp.where(lane == k, dk.astype(I32), out)
        dest_ref[...] = out

    cnt_ref[...] += colsum


def _plan(eidx):
    tp = PLAN_TP
    return pl.pallas_call(
        _plan_kernel,
        grid=(2, T // tp),
        in_specs=[pl.BlockSpec((tp, LANES), lambda p, i: (i, 0))],
        out_specs=[
            pl.BlockSpec((tp, LANES), lambda p, i: (i * p, 0)),
            pl.BlockSpec((8, LANES), lambda p, i: (0, 0)),
        ],
        out_shape=(jax.ShapeDtypeStruct((T, LANES), I32),
                   jax.ShapeDtypeStruct((8, LANES), F32)),
        scratch_shapes=[pltpu.VMEM((1, LANES), F32), pltpu.VMEM((1, LANES), F32)],
        compiler_params=pltpu.CompilerParams(
            dimension_semantics=("arbitrary", "arbitrary")),
        name="route_plan",
    )(eidx)


DISP_TM = 256
DISP_N = DISP_TM * TOP_K


def _dispatch_kernel(dest_hbm, h_ref, xs_hbm, idx_smem, isem, sem):
    i = pl.program_id(0)
    cp = pltpu.make_async_copy(dest_hbm.at[pl.ds(i, 1)], idx_smem, isem)
    cp.start()
    cp.wait()

    def row_copy(a):
        r = a // TOP_K
        return pltpu.make_async_copy(h_ref.at[pl.ds(r, 1)],
                                     xs_hbm.at[pl.ds(idx_smem[0, a], 1)], sem)

    def issue(a, c):
        row_copy(a).start()
        return c

    lax.fori_loop(0, DISP_N, issue, 0, unroll=8)

    def drain(a, c):
        row_copy(a).wait()
        return c

    lax.fori_loop(0, DISP_N, drain, 0, unroll=8)


def _dispatch(dest2, h1p):
    return pl.pallas_call(
        _dispatch_kernel,
        grid=(T // DISP_TM,),
        in_specs=[
            pl.BlockSpec(memory_space=pl.ANY),
            pl.BlockSpec((DISP_TM, D // 2), lambda i: (i, 0)),
        ],
        out_specs=pl.BlockSpec(memory_space=pl.ANY),
        out_shape=jax.ShapeDtypeStruct((A_ROWS, D // 2), U32),
        scratch_shapes=[
            pltpu.SMEM((1, DISP_N), I32),
            pltpu.SemaphoreType.DMA,
            pltpu.SemaphoreType.DMA,
        ],
        compiler_params=pltpu.CompilerParams(dimension_semantics=("arbitrary",)),
        name="moe_dispatch",
    )(dest2, h1p)


MOE_BM = 512
MOE_NBLK = A_ROWS // MOE_BM
MOE_NW = MOE_NBLK + N_EXP


def _expert_kernel(we_ref, wb_ref, wlo_ref, whi_ref, wfe_ref, wfb_ref,
                   x_ref, wgu_ref, bgu_ref, wdn_ref, bdn_ref, o_ref, wgu_bf, wdn_bf):
    w = pl.program_id(0)
    e = we_ref[w]
    lo = wlo_ref[w]
    hi = whi_ref[w]

    @pl.when(wfe_ref[w] == 1)
    def _():
        wgu_bf[...] = wgu_ref[...].astype(BF16)
        wdn_bf[...] = wdn_ref[...].astype(BF16)

    @pl.when(wfb_ref[w] == 1)
    def _():
        o_ref[...] = jnp.zeros_like(o_ref)

    @pl.when(hi > lo)
    def _():
        xlo, xhi = _unpack_bf16_pair(x_ref[...])
        half = D // 2
        gu = (_dot(xlo.astype(BF16), wgu_bf[:half, :]) + _dot(xhi.astype(BF16), wgu_bf[half:, :])
              + bgu_ref[pl.ds(e, 1), :])
        gate = jnp.minimum(gu[:, :D_FF], SW_LIMIT)
        up = jnp.clip(gu[:, D_FF:], -SW_LIMIT, SW_LIMIT)
        act = (up + 1.0) * (gate * jax.nn.sigmoid(SW_ALPHA * gate))
        out = _dot(act.astype(BF16), wdn_bf[...]) + bdn_ref[pl.ds(e, 1), :]
        rid = lax.broadcasted_iota(I32, (MOE_BM, 1), 0)
        keep = jnp.logical_and(rid >= lo, rid < hi)
        o_ref[...] = jnp.where(keep, _pack_bf16_pair(out), o_ref[...])


def _experts(meta, xs, w_gu, b_gu, w_down, b_down):
    we, wb, wlo, whi, wfe, wfb = meta
    return pl.pallas_call(
        _expert_kernel,
        grid_spec=pltpu.PrefetchScalarGridSpec(
            num_scalar_prefetch=6,
            grid=(MOE_NW,),
            in_specs=[
                pl.BlockSpec((MOE_BM, D // 2), lambda w, we, wb, *_: (wb[w], 0)),
                pl.BlockSpec((None, D, 2 * D_FF), lambda w, we, *_: (we[w], 0, 0)),
                pl.BlockSpec((N_EXP, 2 * D_FF), lambda w, *_: (0, 0)),
                pl.BlockSpec((None, D_FF, D), lambda w, we, *_: (we[w], 0, 0)),
                pl.BlockSpec((N_EXP, D), lambda w, *_: (0, 0)),
            ],
            out_specs=pl.BlockSpec((MOE_BM, D // 2), lambda w, we, wb, *_: (wb[w], 0)),
            scratch_shapes=[
                pltpu.VMEM((D, 2 * D_FF), BF16),
                pltpu.VMEM((D_FF, D), BF16),
            ],
        ),
        out_shape=jax.ShapeDtypeStruct((A_ROWS, D // 2), U32),
        compiler_params=pltpu.CompilerParams(
            dimension_semantics=("arbitrary",),
            vmem_limit_bytes=56 * 1024 * 1024),
        name="moe_experts",
    )(we, wb, wlo, whi, wfe, wfb, xs, w_gu, b_gu, w_down, b_down)


def _work_items(counts):
    bm = MOE_BM
    end = jnp.cumsum(counts)
    start = end - counts
    first = start // bm
    last = jnp.maximum(end - 1, 0) // bm
    n_e = jnp.where(counts > 0, last - first + 1, 0)
    item_end = jnp.cumsum(n_e)
    item_start = item_end - n_e
    total = item_end[-1]
    w = jnp.arange(MOE_NW, dtype=I32)
    wc = jnp.minimum(w, total - 1)
    e_w = jnp.minimum(jnp.searchsorted(item_end, wc, side="right"), N_EXP - 1).astype(I32)
    blk = first[e_w] + (wc - item_start[e_w])
    lo = jnp.maximum(start[e_w], blk * bm) - blk * bm
    hi = jnp.minimum(end[e_w], (blk + 1) * bm) - blk * bm
    valid = w < total
    lo = jnp.where(valid, lo, 0)
    hi = jnp.where(valid, hi, 0)
    prev_e = jnp.concatenate([jnp.full((1,), -1, I32), e_w[:-1]])
    prev_b = jnp.concatenate([jnp.full((1,), -1, I32), blk[:-1]])
    fe = (e_w != prev_e).astype(I32)
    fb = (blk != prev_b).astype(I32)
    return tuple(a.astype(I32) for a in (e_w, blk, lo, hi, fe, fb))


COMB_TM = 256
COMB_N = COMB_TM * TOP_K


def _combine_kernel(dest_hbm, ys_hbm, h1_ref, tw_ref, g_ref, b_ref, o_ref,
                    idx_smem, gbuf, isem, sem):
    i = pl.program_id(0)
    cp = pltpu.make_async_copy(dest_hbm.at[pl.ds(i, 1)], idx_smem, isem)
    cp.start()
    cp.wait()

    def row_copy(a):
        r = a // TOP_K
        k = a % TOP_K
        return pltpu.make_async_copy(ys_hbm.at[pl.ds(idx_smem[0, a], 1)],
                                     gbuf.at[k, pl.ds(r, 1)], sem)

    def issue(a, c):
        row_copy(a).start()
        return c

    lax.fori_loop(0, COMB_N, issue, 0, unroll=8)

    def drain(a, c):
        row_copy(a).wait()
        return c

    lax.fori_loop(0, COMB_N, drain, 0, unroll=8)

    tw = tw_ref[...]
    ylo = jnp.zeros((COMB_TM, D // 2), F32)
    yhi = jnp.zeros((COMB_TM, D // 2), F32)
    for k in range(TOP_K):
        lo, hi = _unpack_bf16_pair(gbuf[k])
        wk = tw[:, k:k + 1]
        ylo = ylo + lo * wk
        yhi = yhi + hi * wk
    ff = jnp.concatenate([ylo, yhi], axis=1)
    o_ref[...] = _ln(DN_ALPHA * h1_ref[...] + ff, g_ref[...], b_ref[...])


def _combine(dest2, ys, h1, topw, g, b):
    tm = COMB_TM
    return pl.pallas_call(
        _combine_kernel,
        grid=(T // tm,),
        in_specs=[
            pl.BlockSpec(memory_space=pl.ANY),
            pl.BlockSpec(memory_space=pl.ANY),
            pl.BlockSpec((tm, D), lambda i: (i, 0)),
            pl.BlockSpec((tm, LANES), lambda i: (i, 0)),
            pl.BlockSpec((1, D), lambda i: (0, 0)),
            pl.BlockSpec((1, D), lambda i: (0, 0)),
        ],
        out_specs=pl.BlockSpec((tm, D), lambda i: (i, 0)),
        out_shape=jax.ShapeDtypeStruct((T, D), F32),
        scratch_shapes=[
            pltpu.SMEM((1, COMB_N), I32),
            pltpu.VMEM((TOP_K, tm, D // 2), U32),
            pltpu.SemaphoreType.DMA,
            pltpu.SemaphoreType.DMA,
        ],
        compiler_params=pltpu.CompilerParams(dimension_semantics=("arbitrary",)),
        name="moe_combine_ln2",
    )(dest2, ys, h1, topw, g, b)


def _pad_cols(a, n):
    return jnp.pad(a, ((0, 0), (0, n - a.shape[1])))


def kernel(x, mem, ln_in_g, ln_in_b, ln_mem_g, ln_mem_b, w_in, b_in, w_decay_f, b_decay_f,
           w_decay_b, b_decay_b, gla_norm_g, w_br_gla, w_br_fnet, w_br_mem, w_mem_kv, w_out,
           b_out, ln1_g, ln1_b, w_router, b_router, w_gu, b_gu, w_down, b_down, ln2_g, ln2_b):
    assert x.shape == (BATCH, SEQ, D) and w_in.shape[0] == 1
    row = lambda a: a.reshape(1, -1)
    x2 = x.reshape(T, D)
    w_in0, b_in0 = w_in[0], b_in[0]
    lr0, lr1 = 3072, 3072 + 2 * GLA_LR
    w_main = jnp.concatenate([w_in0[:, :lr0], w_in0[:, lr1:]], axis=1).astype(BF16)
    b_main = row(jnp.concatenate([b_in0[:lr0], b_in0[lr1:]]))
    w_lr = _pad_cols(w_in0[:, lr0:lr1], LANES).astype(BF16)
    b_lr = _pad_cols(row(b_in0[lr0:lr1]), LANES)

    qk, v, r, fm, gates, lr = _inproj(x2, row(ln_in_g), row(ln_in_b), w_main, b_main, w_lr, b_lr)

    zpad = jnp.zeros((LANES - 2 * GLA_LR, GLA_H * GLA_DK), F32)
    zlr = jnp.zeros((GLA_LR, GLA_H * GLA_DK), F32)
    wdf = jnp.concatenate([w_decay_f[0], zlr, zpad], axis=0)
    wdb = jnp.concatenate([zlr, w_decay_b[0], zpad], axis=0)
    og = _gla(qk, v, r, lr, wdf, row(b_decay_f[0]), wdb, row(b_decay_b[0]), row(gla_norm_g[0]))

    f1, cwb, swb, f2, ccs = _dft_tables()
    z = _fft2(_fft1(fm, f1, cwb, swb), f2)

    kv = _memkv(mem.reshape(BATCH * MEM_LEN, D), row(ln_mem_g), row(ln_mem_b),
                w_mem_kv[0].astype(BF16))

    w_r = _pad_cols(w_router[0], LANES)
    b_r = jnp.concatenate([row(b_router[0]),
                           jnp.full((1, LANES - N_EXP), NEG_BIG, F32)], axis=1)
    h1, h1p, eidx, topw = _merge(
        x2, og, z, fm, gates, kv, row(ln_in_g), row(ln_in_b),
        w_br_gla[0].astype(BF16), ccs, w_br_fnet[0].astype(BF16), w_br_mem[0].astype(BF16),
        w_out[0].astype(BF16), row(b_out[0]), row(ln1_g[0]), row(ln1_b[0]), w_r, b_r)

    dest, cnt = _plan(eidx)
    counts = cnt[0, :N_EXP].astype(I32)
    dest2 = dest[:, :TOP_K].reshape(T // DISP_TM, DISP_N)
    xs = _dispatch(dest2, h1p)
    ys = _experts(_work_items(counts), xs, w_gu[0], b_gu[0], w_down[0], b_down[0])
    out = _combine(dest2, ys, h1, topw, row(ln2_g[0]), row(ln2_b[0]))
    return out.reshape(BATCH, SEQ, D)
```

```python
import functools
import math

import numpy as np
import jax
import jax.numpy as jnp
from jax import lax
from jax.experimental import pallas as pl
from jax.experimental.pallas import tpu as pltpu

F32 = jnp.float32
BF16 = jnp.bfloat16
I32 = jnp.int32
U32 = jnp.uint32

D = 1024
BATCH = 4
SEQ = 4096
T = BATCH * SEQ
GLA_H = 4
GLA_DK = 128
GLA_DV = 256
GLA_LR = 16
GLA_TAU = 16.0
GLA_C = 64
FN_G = 4
FN_GW = 128
FN_W = 512
MEM_LEN = 256
MEM_H = 4
MEM_HD = 128
MQ_W = 512
N_EXP = 32
TOP_K = 4
D_FF = 1024
SW_LIMIT = 7.0
SW_ALPHA = 1.702
LN_EPS = 1e-5
RMS_EPS = 1e-6
DN_ALPHA = 2.0 ** 0.25
A_ROWS = T * TOP_K

FFT_N1 = 128
FFT_N2 = 32

LANES = 128
NEG_BIG = -1e30


def _ln(x, g, b):
    mu = jnp.mean(x, axis=-1, keepdims=True)
    xc = x - mu
    var = jnp.mean(xc * xc, axis=-1, keepdims=True)
    return xc * lax.rsqrt(var + LN_EPS) * g + b


def _dot(a, b):
    return jnp.dot(a, b, preferred_element_type=F32)


def _dot_nt(a, b):
    return lax.dot_general(a, b, (((1,), (1,)), ((), ())), preferred_element_type=F32)


def _dot_tn(a, b):
    return lax.dot_general(a, b, (((0,), (0,)), ((), ())), preferred_element_type=F32)


INPROJ_TM = 1024
INPROJ_TN = 1024


def _inproj_kernel(x_ref, g_ref, b_ref, w_ref, bias_ref, wlr_ref, blr_ref,
                   qk_ref, v_ref, r_ref, fm_ref, gates_ref, lr_ref, hb_ref):
    j = pl.program_id(1)

    @pl.when(j == 0)
    def _():
        hb = _ln(x_ref[...], g_ref[...], b_ref[...]).astype(BF16)
        hb_ref[...] = hb
        lr_ref[...] = _dot(hb, wlr_ref[...]) + blr_ref[...]

    y = (_dot(hb_ref[...], w_ref[...]) + bias_ref[...]).astype(BF16)
    for idx, ref in enumerate((qk_ref, v_ref, r_ref, fm_ref)):
        @pl.when(j == idx)
        def _(ref=ref):
            ref[...] = y

    @pl.when(j >= 4)
    def _():
        gates_ref[...] = y


def _inproj(x2, ln_g, ln_b, w_main, b_main, w_lr, b_lr):
    tm, tn = INPROJ_TM, INPROJ_TN
    nj = w_main.shape[1] // tn
    row = lambda i, j: (i, 0)
    outs = (
        jax.ShapeDtypeStruct((T, 1024), BF16),
        jax.ShapeDtypeStruct((T, 1024), BF16),
        jax.ShapeDtypeStruct((T, 1024), BF16),
        jax.ShapeDtypeStruct((T, 1024), BF16),
        jax.ShapeDtypeStruct((T, 3072), BF16),
        jax.ShapeDtypeStruct((T, LANES), F32),
    )
    return pl.pallas_call(
        _inproj_kernel,
        grid=(T // tm, nj),
        in_specs=[
            pl.BlockSpec((tm, D), row),
            pl.BlockSpec((1, D), lambda i, j: (0, 0)),
            pl.BlockSpec((1, D), lambda i, j: (0, 0)),
            pl.BlockSpec((D, tn), lambda i, j: (0, j)),
            pl.BlockSpec((1, tn), lambda i, j: (0, j)),
            pl.BlockSpec((D, LANES), lambda i, j: (0, 0)),
            pl.BlockSpec((1, LANES), lambda i, j: (0, 0)),
        ],
        out_specs=[
            pl.BlockSpec((tm, tn), row),
            pl.BlockSpec((tm, tn), row),
            pl.BlockSpec((tm, tn), row),
            pl.BlockSpec((tm, tn), row),
            pl.BlockSpec((tm, tn), lambda i, j: (i, jnp.maximum(j - 4, 0))),
            pl.BlockSpec((tm, LANES), row),
        ],
        out_shape=outs,
        scratch_shapes=[pltpu.VMEM((tm, D), BF16)],
        compiler_params=pltpu.CompilerParams(
            dimension_semantics=("arbitrary", "arbitrary"),
            vmem_limit_bytes=48 * 1024 * 1024),
        name="ln_inproj",
    )(x2, ln_g, ln_b, w_main, b_main, w_lr, b_lr)


GLA_BULK = 512
GLA_NCH = SEQ // GLA_C


def _gla_kernel(q_ref, k_ref, v_ref, r_ref, lr_ref, wdf_ref, bdf_ref, wdb_ref, bdb_ref,
                g_ref, o_ref, acc_ref, qin_ref, kin_ref, kst_ref, dec_ref):
    C = GLA_C
    G = GLA_BULK
    scale = GLA_DK ** -0.5
    rowi = lax.broadcasted_iota(I32, (G, GLA_DK), 0) & (C - 1)
    ii = lax.broadcasted_iota(I32, (C, C), 0)
    jj = lax.broadcasted_iota(I32, (C, C), 1)

    def run_direction(rev, wd_ref, bd_ref):
        def bulk(gi, carry):
            r0 = pl.multiple_of(gi * G, G)
            rows = pl.ds(r0, G)
            z = _dot(lr_ref[rows, :], wd_ref[...]) + bd_ref[...]
            la = -(jnp.maximum(-z, 0.0) + jnp.log(1.0 + jnp.exp(-jnp.abs(z)))) * (1.0 / GLA_TAU)
            c = la
            for s in (1, 2, 4, 8, 16, 32):
                c = c + jnp.where(rowi >= s, pltpu.roll(c, s, 0), 0.0)
            qf = q_ref[rows, :].astype(F32)
            kf = k_ref[rows, :].astype(F32)
            for ci in range(G // C):
                sl = slice(ci * C, (ci + 1) * C)
                cc = c[sl]
                blast = cc[C - 1:C, :]
                b = (blast - cc + la[sl]) if rev else cc
                rr = pl.ds(r0 + ci * C, C)
                qin_ref[rr, :] = (qf[sl] * (scale * jnp.exp(b))).astype(BF16)
                kin_ref[rr, :] = (kf[sl] * jnp.exp(-b)).astype(BF16)
                kst_ref[rr, :] = (kf[sl] * jnp.exp(blast - b)).astype(BF16)
                dec_ref[pl.ds(gi * (G // C) + ci, 1), :] = jnp.exp(blast)
            return carry

        lax.fori_loop(0, SEQ // G, bulk, 0)

        mask = (ii <= jj) if rev else (ii >= jj)

        def step(i, st):
            n = (GLA_NCH - 1 - i) if rev else i
            rows = pl.ds(pl.multiple_of(n * C, C), C)
            qi = qin_ref[rows, :]
            ki = kin_ref[rows, :]
            ks = kst_ref[rows, :]
            vi = v_ref[rows, :]
            att = jnp.where(mask, _dot_nt(qi, ki), 0.0)
            o = _dot(att.astype(BF16), vi) + _dot_nt(qi, st.astype(BF16))
            if rev:
                acc_ref[rows, :] += o
            else:
                acc_ref[rows, :] = o
            return st * dec_ref[pl.ds(n, 1), :] + _dot_tn(vi, ks)

        lax.fori_loop(0, GLA_NCH, step, jnp.zeros((GLA_DV, GLA_DK), F32), unroll=2)

    run_direction(False, wdf_ref, bdf_ref)
    run_direction(True, wdb_ref, bdb_ref)

    def fin(gi, carry):
        rows = pl.ds(pl.multiple_of(gi * G, G), G)
        o = acc_ref[rows, :]
        o = o * lax.rsqrt(jnp.mean(o * o, axis=-1, keepdims=True) + RMS_EPS) * g_ref[...]
        rg = r_ref[rows, :].astype(F32)
        o_ref[rows, :] = (o * (rg * jax.nn.sigmoid(rg))).astype(BF16)
        return carry

    lax.fori_loop(0, SEQ // G, fin, 0)


def _gla(qk, v, r, lr, wdf, bdf, wdb, bdb, g):
    return pl.pallas_call(
        _gla_kernel,
        grid=(BATCH, GLA_H),
        in_specs=[
            pl.BlockSpec((SEQ, GLA_DK), lambda b, h: (b, h)),
            pl.BlockSpec((SEQ, GLA_DK), lambda b, h: (b, GLA_H + h)),
            pl.BlockSpec((SEQ, GLA_DV), lambda b, h: (b, h)),
            pl.BlockSpec((SEQ, GLA_DV), lambda b, h: (b, h)),
            pl.BlockSpec((SEQ, LANES), lambda b, h: (b, 0)),
            pl.BlockSpec((LANES, GLA_DK), lambda b, h: (0, h)),
            pl.BlockSpec((1, GLA_DK), lambda b, h: (0, h)),
            pl.BlockSpec((LANES, GLA_DK), lambda b, h: (0, h)),
            pl.BlockSpec((1, GLA_DK), lambda b, h: (0, h)),
            pl.BlockSpec((1, GLA_DV), lambda b, h: (0, 0)),
        ],
        out_specs=pl.BlockSpec((SEQ, GLA_DV), lambda b, h: (b, h)),
        out_shape=jax.ShapeDtypeStruct((T, GLA_H * GLA_DV), BF16),
        scratch_shapes=[
            pltpu.VMEM((SEQ, GLA_DV), F32),
            pltpu.VMEM((SEQ, GLA_DK), BF16),
            pltpu.VMEM((SEQ, GLA_DK), BF16),
            pltpu.VMEM((SEQ, GLA_DK), BF16),
            pltpu.VMEM((GLA_NCH, GLA_DK), F32),
        ],
        compiler_params=pltpu.CompilerParams(
            dimension_semantics=("arbitrary", "arbitrary"),
            vmem_limit_bytes=48 * 1024 * 1024),
        name="gla",
    )(qk, qk, v, r, lr, wdf, bdf, wdb, bdb, g)


FFT1_M = 16
FFT2_TN = 2048


def _fft1_kernel(x_ref, f1_ref, cw_ref, sw_ref, o_ref):
    f1 = f1_ref[...]
    for l in range(FFT1_M):
        xl = x_ref[:, l * 1024:l * 1024 + FN_W]
        a = _dot(f1, xl)
        ar = a[:FFT_N2]
        ai = a[FFT_N2:]
        cw = jnp.concatenate([cw_ref[l]] * (FN_W // LANES), axis=1)
        sw = jnp.concatenate([sw_ref[l]] * (FN_W // LANES), axis=1)
        o_ref[0, l] = (ar * cw + ai * sw).astype(BF16)
        o_ref[1, l] = (ai * cw - ar * sw).astype(BF16)


def _fft1(fm, f1, cwb, swb):
    xv = fm.reshape(BATCH * FFT_N2, FFT_N1 * 1024)
    m = FFT1_M
    return pl.pallas_call(
        _fft1_kernel,
        grid=(BATCH, FFT_N1 // m),
        in_specs=[
            pl.BlockSpec((FFT_N2, m * 1024), lambda b, j: (b, j)),
            pl.BlockSpec((2 * FFT_N2, FFT_N2), lambda b, j: (0, 0)),
            pl.BlockSpec((m, FFT_N2, LANES), lambda b, j: (j, 0, 0)),
            pl.BlockSpec((m, FFT_N2, LANES), lambda b, j: (j, 0, 0)),
        ],
        out_specs=pl.BlockSpec((None, 2, m, FFT_N2, FN_W), lambda b, j: (b, 0, j, 0, 0)),
        out_shape=jax.ShapeDtypeStruct((BATCH, 2, FFT_N1, FFT_N2, FN_W), BF16),
        compiler_params=pltpu.CompilerParams(
            dimension_semantics=("arbitrary", "arbitrary")),
        name="fft_stage1",
    )(xv, f1, cwb, swb)


def _fft2_kernel(d_ref, f2_ref, o_ref):
    z = _dot(f2_ref[...], d_ref[...])
    o_ref[0] = z[:FFT_N1].astype(BF16)
    o_ref[1] = z[FFT_N1:].astype(BF16)


def _fft2(bout, f2):
    dv = bout.reshape(BATCH, 2 * FFT_N1, FFT_N2 * FN_W)
    tn = FFT2_TN
    z = pl.pallas_call(
        _fft2_kernel,
        grid=(BATCH, FFT_N2 * FN_W // tn),
        in_specs=[
            pl.BlockSpec((None, 2 * FFT_N1, tn), lambda b, j: (b, 0, j)),
            pl.BlockSpec((2 * FFT_N1, 2 * FFT_N1), lambda b, j: (0, 0)),
        ],
        out_specs=pl.BlockSpec((None, 2, FFT_N1, tn), lambda b, j: (b, 0, 0, j)),
        out_shape=jax.ShapeDtypeStruct((BATCH, 2, FFT_N1, FFT_N2 * FN_W), BF16),
        compiler_params=pltpu.CompilerParams(
            dimension_semantics=("arbitrary", "arbitrary")),
        name="fft_stage2",
    )(dv, f2)
    return z.reshape(BATCH, 2, SEQ, FN_W)


def _dft_tables():
    n2 = np.arange(FFT_N2, dtype=np.float64)
    n1 = np.arange(FFT_N1, dtype=np.float64)
    th = 2.0 * np.pi * np.outer(n2, n2) / FFT_N2
    f1 = np.concatenate([np.cos(th), -np.sin(th)], axis=0) / math.sqrt(SEQ)
    tw = 2.0 * np.pi * np.outer(n1, n2) / SEQ
    cwb = np.broadcast_to(np.cos(tw)[:, :, None], (FFT_N1, FFT_N2, LANES))
    swb = np.broadcast_to(np.sin(tw)[:, :, None], (FFT_N1, FFT_N2, LANES))
    th1 = 2.0 * np.pi * np.outer(n1, n1) / FFT_N1
    c1, s1 = np.cos(th1), np.sin(th1)
    f2 = np.block([[c1, s1], [-s1, c1]])
    cc = np.arange(FN_GW, dtype=np.float64)
    thc = 2.0 * np.pi * np.outer(cc, cc) / FN_GW
    ccs = np.concatenate([np.cos(thc), np.sin(thc)], axis=0) / math.sqrt(FN_GW)
    as32 = lambda a: jnp.asarray(np.ascontiguousarray(a), dtype=F32)
    return as32(f1).astype(BF16), as32(cwb), as32(swb), as32(f2).astype(BF16), as32(ccs).astype(BF16)


def _memkv_kernel(m_ref, g_ref, b_ref, w_ref, o_ref):
    mn = _ln(m_ref[...], g_ref[...], b_ref[...]).astype(BF16)
    o_ref[...] = _dot(mn, w_ref[...]).astype(BF16)


def _memkv(mem2, g, b, w):
    return pl.pallas_call(
        _memkv_kernel,
        grid=(BATCH,),
        in_specs=[
            pl.BlockSpec((MEM_LEN, D), lambda i: (i, 0)),
            pl.BlockSpec((1, D), lambda i: (0, 0)),
            pl.BlockSpec((1, D), lambda i: (0, 0)),
            pl.BlockSpec((D, 2 * MQ_W), lambda i: (0, 0)),
        ],
        out_specs=pl.BlockSpec((MEM_LEN, 2 * MQ_W), lambda i: (i, 0)),
        out_shape=jax.ShapeDtypeStruct((BATCH * MEM_LEN, 2 * MQ_W), BF16),
        compiler_params=pltpu.CompilerParams(dimension_semantics=("arbitrary",)),
        name="mem_kv",
    )(mem2, g, b, w)


MERGE_TM = 256


def _pack_bf16_pair(v):
    n = v.shape[1] // 2
    bits = lax.bitcast_convert_type(v.astype(BF16).astype(F32), U32)
    return (bits[:, n:] & jnp.uint32(0xFFFF0000)) | (bits[:, :n] >> 16)


def _unpack_bf16_pair(p):
    lo = lax.bitcast_convert_type(p << 16, F32)
    hi = lax.bitcast_convert_type(p & jnp.uint32(0xFFFF0000), F32)
    return lo, hi


def _merge_kernel(x_ref, og_ref, zr_ref, zi_ref, fm_ref, gt_ref, kv_ref,
                  lng_ref, lnb_ref, wg_ref, ccs_ref, wf_ref, wm_ref, wo_ref, bo_ref,
                  l1g_ref, l1b_ref, wr_ref, br_ref,
                  h1_ref, h1p_ref, eidx_ref, topw_ref):
    tm = MERGE_TM
    h = _ln(x_ref[...], lng_ref[...], lnb_ref[...])
    y_gla = _dot(og_ref[...], wg_ref[...])

    ys = []
    for g in range(FN_G):
        sl = slice(g * FN_GW, (g + 1) * FN_GW)
        zz = jnp.concatenate([zr_ref[:, sl], zi_ref[:, sl]], axis=1)
        ys.append(_dot(zz, ccs_ref[...]))
    y_fn = _dot(jnp.concatenate(ys, axis=1).astype(BF16), wf_ref[...])

    oms = []
    for hd in range(MEM_H):
        sl = slice(hd * MEM_HD, (hd + 1) * MEM_HD)
        qh = fm_ref[:, FN_W + hd * MEM_HD:FN_W + (hd + 1) * MEM_HD]
        s = _dot_nt(qh, kv_ref[:, sl]) * (MEM_HD ** -0.5)
        s = s - jnp.max(s, axis=-1, keepdims=True)
        p = jnp.exp(s)
        p = p / jnp.sum(p, axis=-1, keepdims=True)
        oms.append(_dot(p.astype(BF16), kv_ref[:, MQ_W + hd * MEM_HD:MQ_W + (hd + 1) * MEM_HD]))
    y_mem = _dot(jnp.concatenate(oms, axis=1).astype(BF16), wm_ref[...])

    merged = (jax.nn.sigmoid(gt_ref[:, 0:D].astype(F32)) * y_gla
              + jax.nn.sigmoid(gt_ref[:, D:2 * D].astype(F32)) * y_fn
              + jax.nn.sigmoid(gt_ref[:, 2 * D:3 * D].astype(F32)) * y_mem)
    mix = _dot(merged.astype(BF16), wo_ref[...]) + bo_ref[...]
    h1 = _ln(DN_ALPHA * h + mix, l1g_ref[...], l1b_ref[...])
    h1_ref[...] = h1
    h1p_ref[...] = _pack_bf16_pair(h1)

    logits = jnp.dot(h1, wr_ref[...], preferred_element_type=F32,
                     precision=lax.Precision.HIGHEST) + br_ref[...]
    lane = lax.broadcasted_iota(I32, (tm, LANES), 1)
    l = logits
    vals, idxs = [], []
    for _ in range(TOP_K):
        m = jnp.max(l, axis=-1, keepdims=True)
        idx = jnp.min(jnp.where(l == m, lane, LANES), axis=-1, keepdims=True)
        vals.append(m)
        idxs.append(idx)
        l = jnp.where(lane == idx, -jnp.inf, l)
    es = [jnp.exp(v - vals[0]) for v in vals]
    den = es[0] + es[1] + es[2] + es[3]
    eo = jnp.zeros((tm, LANES), I32)
    wo = jnp.zeros((tm, LANES), F32)
    for k in range(TOP_K):
        eo = jnp.where(lane == k, idxs[k], eo)
        wo = jnp.where(lane == k, es[k] / den, wo)
    eidx_ref[...] = eo
    topw_ref[...] = wo


def _merge(x2, og, z, fm, gates, kv, lng, lnb, wg, ccs, wf, wm, wo, bo, l1g, l1b, wr, br):
    tm = MERGE_TM
    per_b = SEQ // tm
    row = lambda i: (i, 0)
    const = lambda i: (0, 0)
    outs = (
        jax.ShapeDtypeStruct((T, D), F32),
        jax.ShapeDtypeStruct((T, D // 2), U32),
        jax.ShapeDtypeStruct((T, LANES), I32),
        jax.ShapeDtypeStruct((T, LANES), F32),
    )
    return pl.pallas_call(
        _merge_kernel,
        grid=(T // tm,),
        in_specs=[
            pl.BlockSpec((tm, D), row),
            pl.BlockSpec((tm, D), row),
            pl.BlockSpec((None, None, tm, FN_W), lambda i: (i // per_b, 0, i % per_b, 0)),
            pl.BlockSpec((None, None, tm, FN_W), lambda i: (i // per_b, 1, i % per_b, 0)),
            pl.BlockSpec((tm, D), row),
            pl.BlockSpec((tm, 3 * D), row),
            pl.BlockSpec((MEM_LEN, 2 * MQ_W), lambda i: (i // per_b, 0)),
            pl.BlockSpec((1, D), const), pl.BlockSpec((1, D), const),
            pl.BlockSpec((D, D), const),
            pl.BlockSpec((2 * FN_GW, FN_GW), const),
            pl.BlockSpec((FN_W, D), const),
            pl.BlockSpec((MQ_W, D), const),
            pl.BlockSpec((D, D), const),
            pl.BlockSpec((1, D), const),
            pl.BlockSpec((1, D), const), pl.BlockSpec((1, D), const),
            pl.BlockSpec((D, LANES), const),
            pl.BlockSpec((1, LANES), const),
        ],
        out_specs=[
            pl.BlockSpec((tm, D), row),
            pl.BlockSpec((tm, D // 2), row),
            pl.BlockSpec((tm, LANES), row),
            pl.BlockSpec((tm, LANES), row),
        ],
        out_shape=outs,
        compiler_params=pltpu.CompilerParams(
            dimension_semantics=("arbitrary",),
            vmem_limit_bytes=48 * 1024 * 1024),
        name="merge_ln1_router",
    )(x2, og, z, z, fm, gates, kv, lng, lnb, wg, ccs, wf, wm, wo, bo, l1g, l1b, wr, br)


PLAN_TP = 512


def _plan_kernel(e_ref, dest_ref, cnt_out_ref, cnt_ref, off_ref):
    p = pl.program_id(0)
    i = pl.program_id(1)
    tp = PLAN_TP
    lane = lax.broadcasted_iota(I32, (tp, LANES), 1)
    e = e_ref[...]
    onehots = [lane == e[:, k:k + 1] for k in range(TOP_K)]
    mf = jnp.zeros((tp, LANES), F32)
    for oh in onehots:
        mf = mf + jnp.where(oh, 1.0, 0.0)
    colsum = jnp.sum(mf, axis=0, keepdims=True)

    @pl.when(jnp.logical_and(p == 0, i == 0))
    def _():
        cnt_ref[...] = jnp.zeros_like(cnt_ref)

    @pl.when(jnp.logical_and(p == 1, i == 0))
    def _():
        tot = cnt_ref[...]
        cnt_out_ref[...] = jnp.broadcast_to(tot, cnt_out_ref.shape)
        lane1 = lax.broadcasted_iota(I32, (1, LANES), 1)
        inc = tot
        for s in (1, 2, 4, 8, 16, 32, 64):
            inc = inc + jnp.where(lane1 >= s, pltpu.roll(inc, s, 1), 0.0)
        off_ref[...] = inc - tot
        cnt_ref[...] = jnp.zeros_like(cnt_ref)

    @pl.when(p == 0)
    def _():
        dest_ref[...] = jnp.zeros_like(dest_ref)

    @pl.when(p == 1)
    def _():
        ri = lax.broadcasted_iota(I32, (tp, tp), 0)
        ci = lax.broadcasted_iota(I32, (tp, tp), 1)
        ltri = jnp.where(ri > ci, 1.0, 0.0).astype(BF16)
        rank = _dot(ltri, mf.astype(BF16)) + cnt_ref[...] + off_ref[...]
        out = jnp.zeros((tp, LANES), I32)
        for k in range(TOP_K):
            dk = jnp.sum(jnp.where(onehots[k], rank, 0.0), axis=-1, keepdims=True)
            out = jnp.where(lane == k, dk.astype(I32), out)
        dest_ref[...] = out

    cnt_ref[...] += colsum


def _plan(eidx):
    tp = PLAN_TP
    return pl.pallas_call(
        _plan_kernel,
        grid=(2, T // tp),
        in_specs=[pl.BlockSpec((tp, LANES), lambda p, i: (i, 0))],
        out_specs=[
            pl.BlockSpec((tp, LANES), lambda p, i: (i * p, 0)),
            pl.BlockSpec((8, LANES), lambda p, i: (0, 0)),
        ],
        out_shape=(jax.ShapeDtypeStruct((T, LANES), I32),
                   jax.ShapeDtypeStruct((8, LANES), F32)),
        scratch_shapes=[pltpu.VMEM((1, LANES), F32), pltpu.VMEM((1, LANES), F32)],
        compiler_params=pltpu.CompilerParams(
            dimension_semantics=("arbitrary", "arbitrary")),
        name="route_plan",
    )(eidx)


DISP_TM = 256
DISP_N = DISP_TM * TOP_K
ROWS_PER_ISSUE = 8


def _dispatch_kernel(dest_hbm, h_ref, xs_hbm, idx_smem, isem, sem):
    i = pl.program_id(0)
    cp = pltpu.make_async_copy(dest_hbm.at[pl.ds(i, 1)], idx_smem, isem)
    cp.start()
    cp.wait()

    def row_copies(ro):
        r0 = pl.multiple_of(ro * ROWS_PER_ISSUE, ROWS_PER_ISSUE)
        a0 = ro * (ROWS_PER_ISSUE * TOP_K)
        return [pltpu.make_async_copy(h_ref.at[pl.ds(r0 + ri, 1)],
                                      xs_hbm.at[pl.ds(idx_smem[0, a0 + ri * TOP_K + k], 1)], sem)
                for ri in range(ROWS_PER_ISSUE) for k in range(TOP_K)]

    def issue(ro, c):
        for n, cp_ in enumerate(row_copies(ro)):
            cp_.start(priority=n % 2)
        return c

    lax.fori_loop(0, DISP_TM // ROWS_PER_ISSUE, issue, 0)

    def drain(ro, c):
        for cp_ in row_copies(ro):
            cp_.wait()
        return c

    lax.fori_loop(0, DISP_TM // ROWS_PER_ISSUE, drain, 0)


def _dispatch(dest2, h1p):
    return pl.pallas_call(
        _dispatch_kernel,
        grid=(T // DISP_TM,),
        in_specs=[
            pl.BlockSpec(memory_space=pl.ANY),
            pl.BlockSpec((DISP_TM, D // 2), lambda i: (i, 0)),
        ],
        out_specs=pl.BlockSpec(memory_space=pl.ANY),
        out_shape=jax.ShapeDtypeStruct((A_ROWS, D // 2), U32),
        scratch_shapes=[
            pltpu.SMEM((1, DISP_N), I32),
            pltpu.SemaphoreType.DMA,
            pltpu.SemaphoreType.DMA,
        ],
        compiler_params=pltpu.CompilerParams(dimension_semantics=("arbitrary",)),
        name="moe_dispatch",
    )(dest2, h1p)


MOE_BM = 512
MOE_NBLK = A_ROWS // MOE_BM
MOE_NW = MOE_NBLK + N_EXP


def _expert_kernel(we_ref, wb_ref, wlo_ref, whi_ref, wfe_ref, wfb_ref,
                   x_ref, wgu_ref, bgu_ref, wdn_ref, bdn_ref, o_ref, wgu_bf, wdn_bf):
    w = pl.program_id(0)
    e = we_ref[w]
    lo = wlo_ref[w]
    hi = whi_ref[w]

    @pl.when(wfe_ref[w] == 1)
    def _():
        wgu_bf[...] = wgu_ref[...].astype(BF16)
        wdn_bf[...] = wdn_ref[...].astype(BF16)

    @pl.when(wfb_ref[w] == 1)
    def _():
        o_ref[...] = jnp.zeros_like(o_ref)

    @pl.when(hi > lo)
    def _():
        xlo, xhi = _unpack_bf16_pair(x_ref[...])
        half = D // 2
        gu = (_dot(xlo.astype(BF16), wgu_bf[:half, :]) + _dot(xhi.astype(BF16), wgu_bf[half:, :])
              + bgu_ref[pl.ds(e, 1), :])
        gate = jnp.minimum(gu[:, :D_FF], SW_LIMIT)
        up = jnp.clip(gu[:, D_FF:], -SW_LIMIT, SW_LIMIT)
        act = (up + 1.0) * (gate * jax.nn.sigmoid(SW_ALPHA * gate))
        out = _dot(act.astype(BF16), wdn_bf[...]) + bdn_ref[pl.ds(e, 1), :]
        rid = lax.broadcasted_iota(I32, (MOE_BM, 1), 0)
        keep = jnp.logical_and(rid >= lo, rid < hi)
        o_ref[...] = jnp.where(keep, _pack_bf16_pair(out), o_ref[...])


def _experts(meta, xs, w_gu, b_gu, w_down, b_down):
    we, wb, wlo, whi, wfe, wfb = meta
    return pl.pallas_call(
        _expert_kernel,
        grid_spec=pltpu.PrefetchScalarGridSpec(
            num_scalar_prefetch=6,
            grid=(MOE_NW,),
            in_specs=[
                pl.BlockSpec((MOE_BM, D // 2), lambda w, we, wb, *_: (wb[w], 0)),
                pl.BlockSpec((None, D, 2 * D_FF), lambda w, we, *_: (we[w], 0, 0)),
                pl.BlockSpec((N_EXP, 2 * D_FF), lambda w, *_: (0, 0)),
                pl.BlockSpec((None, D_FF, D), lambda w, we, *_: (we[w], 0, 0)),
                pl.BlockSpec((N_EXP, D), lambda w, *_: (0, 0)),
            ],
            out_specs=pl.BlockSpec((MOE_BM, D // 2), lambda w, we, wb, *_: (wb[w], 0)),
            scratch_shapes=[
                pltpu.VMEM((D, 2 * D_FF), BF16),
                pltpu.VMEM((D_FF, D), BF16),
            ],
        ),
        out_shape=jax.ShapeDtypeStruct((A_ROWS, D // 2), U32),
        compiler_params=pltpu.CompilerParams(
            dimension_semantics=("arbitrary",),
            vmem_limit_bytes=56 * 1024 * 1024),
        name="moe_experts",
    )(we, wb, wlo, whi, wfe, wfb, xs, w_gu, b_gu, w_down, b_down)


def _work_items(counts):
    bm = MOE_BM
    end = jnp.cumsum(counts)
    start = end - counts
    first = start // bm
    last = jnp.maximum(end - 1, 0) // bm
    n_e = jnp.where(counts > 0, last - first + 1, 0)
    item_end = jnp.cumsum(n_e)
    item_start = item_end - n_e
    total = item_end[-1]
    w = jnp.arange(MOE_NW, dtype=I32)
    wc = jnp.minimum(w, total - 1)
    e_w = jnp.sum((item_end[None, :] <= wc[:, None]).astype(I32), axis=1)
    e_w = jnp.minimum(e_w, N_EXP - 1)
    blk = first[e_w] + (wc - item_start[e_w])
    lo = jnp.maximum(start[e_w], blk * bm) - blk * bm
    hi = jnp.minimum(end[e_w], (blk + 1) * bm) - blk * bm
    valid = w < total
    lo = jnp.where(valid, lo, 0)
    hi = jnp.where(valid, hi, 0)
    prev_e = jnp.concatenate([jnp.full((1,), -1, I32), e_w[:-1]])
    prev_b = jnp.concatenate([jnp.full((1,), -1, I32), blk[:-1]])
    fe = (e_w != prev_e).astype(I32)
    fb = (blk != prev_b).astype(I32)
    return tuple(a.astype(I32) for a in (e_w, blk, lo, hi, fe, fb))


COMB_TM = 256
COMB_N = COMB_TM * TOP_K


def _combine_kernel(dest_hbm, ys_hbm, h1_ref, tw_ref, g_ref, b_ref, o_ref,
                    idx_smem, gbuf, isem, sem):
    i = pl.program_id(0)
    cp = pltpu.make_async_copy(dest_hbm.at[pl.ds(i, 1)], idx_smem, isem)
    cp.start()
    cp.wait()

    def row_copies(ro):
        r0 = pl.multiple_of(ro * ROWS_PER_ISSUE, ROWS_PER_ISSUE)
        a0 = ro * (ROWS_PER_ISSUE * TOP_K)
        return [pltpu.make_async_copy(ys_hbm.at[pl.ds(idx_smem[0, a0 + ri * TOP_K + k], 1)],
                                      gbuf.at[k, pl.ds(r0 + ri, 1)], sem)
                for ri in range(ROWS_PER_ISSUE) for k in range(TOP_K)]

    def issue(ro, c):
        for n, cp_ in enumerate(row_copies(ro)):
            cp_.start(priority=n % 2)
        return c

    lax.fori_loop(0, COMB_TM // ROWS_PER_ISSUE, issue, 0)

    def drain(ro, c):
        for cp_ in row_copies(ro):
            cp_.wait()
        return c

    lax.fori_loop(0, COMB_TM // ROWS_PER_ISSUE, drain, 0)

    tw = tw_ref[...]
    ylo = jnp.zeros((COMB_TM, D // 2), F32)
    yhi = jnp.zeros((COMB_TM, D // 2), F32)
    for k in range(TOP_K):
        lo, hi = _unpack_bf16_pair(gbuf[k])
        wk = tw[:, k:k + 1]
        ylo = ylo + lo * wk
        yhi = yhi + hi * wk
    ff = jnp.concatenate([ylo, yhi], axis=1)
    o_ref[...] = _ln(DN_ALPHA * h1_ref[...] + ff, g_ref[...], b_ref[...])


def _combine(dest2, ys, h1, topw, g, b):
    tm = COMB_TM
    return pl.pallas_call(
        _combine_kernel,
        grid=(T // tm,),
        in_specs=[
            pl.BlockSpec(memory_space=pl.ANY),
            pl.BlockSpec(memory_space=pl.ANY),
            pl.BlockSpec((tm, D), lambda i: (i, 0)),
            pl.BlockSpec((tm, LANES), lambda i: (i, 0)),
            pl.BlockSpec((1, D), lambda i: (0, 0)),
            pl.BlockSpec((1, D), lambda i: (0, 0)),
        ],
        out_specs=pl.BlockSpec((tm, D), lambda i: (i, 0)),
        out_shape=jax.ShapeDtypeStruct((T, D), F32),
        scratch_shapes=[
            pltpu.SMEM((1, COMB_N), I32),
            pltpu.VMEM((TOP_K, tm, D // 2), U32),
            pltpu.SemaphoreType.DMA,
            pltpu.SemaphoreType.DMA,
        ],
        compiler_params=pltpu.CompilerParams(dimension_semantics=("arbitrary",)),
        name="moe_combine_ln2",
    )(dest2, ys, h1, topw, g, b)


def _pad_cols(a, n):
    return jnp.pad(a, ((0, 0), (0, n - a.shape[1])))


def kernel(x, mem, ln_in_g, ln_in_b, ln_mem_g, ln_mem_b, w_in, b_in, w_decay_f, b_decay_f,
           w_decay_b, b_decay_b, gla_norm_g, w_br_gla, w_br_fnet, w_br_mem, w_mem_kv, w_out,
           b_out, ln1_g, ln1_b, w_router, b_router, w_gu, b_gu, w_down, b_down, ln2_g, ln2_b):
    assert x.shape == (BATCH, SEQ, D) and w_in.shape[0] == 1
    row = lambda a: a.reshape(1, -1)
    x2 = x.reshape(T, D)
    w_in0, b_in0 = w_in[0], b_in[0]
    lr0, lr1 = 3072, 3072 + 2 * GLA_LR
    w_main = jnp.concatenate([w_in0[:, :lr0], w_in0[:, lr1:]], axis=1).astype(BF16)
    b_main = row(jnp.concatenate([b_in0[:lr0], b_in0[lr1:]]))
    w_lr = _pad_cols(w_in0[:, lr0:lr1], LANES).astype(BF16)
    b_lr = _pad_cols(row(b_in0[lr0:lr1]), LANES)

    qk, v, r, fm, gates, lr = _inproj(x2, row(ln_in_g), row(ln_in_b), w_main, b_main, w_lr, b_lr)

    zpad = jnp.zeros((LANES - 2 * GLA_LR, GLA_H * GLA_DK), F32)
    zlr = jnp.zeros((GLA_LR, GLA_H * GLA_DK), F32)
    wdf = jnp.concatenate([w_decay_f[0], zlr, zpad], axis=0)
    wdb = jnp.concatenate([zlr, w_decay_b[0], zpad], axis=0)
    og = _gla(qk, v, r, lr, wdf, row(b_decay_f[0]), wdb, row(b_decay_b[0]), row(gla_norm_g[0]))

    f1, cwb, swb, f2, ccs = _dft_tables()
    z = _fft2(_fft1(fm, f1, cwb, swb), f2)

    kv = _memkv(mem.reshape(BATCH * MEM_LEN, D), row(ln_mem_g), row(ln_mem_b),
                w_mem_kv[0].astype(BF16))

    w_r = _pad_cols(w_router[0], LANES)
    b_r = jnp.concatenate([row(b_router[0]),
                           jnp.full((1, LANES - N_EXP), NEG_BIG, F32)], axis=1)
    h1, h1p, eidx, topw = _merge(
        x2, og, z, fm, gates, kv, row(ln_in_g), row(ln_in_b),
        w_br_gla[0].astype(BF16), ccs, w_br_fnet[0].astype(BF16), w_br_mem[0].astype(BF16),
        w_out[0].astype(BF16), row(b_out[0]), row(ln1_g[0]), row(ln1_b[0]), w_r, b_r)

    dest, cnt = _plan(eidx)
    counts = cnt[0, :N_EXP].astype(I32)
    dest2 = dest[:, :TOP_K].reshape(T // DISP_TM, DISP_N)
    xs = _dispatch(dest2, h1p)
    ys = _experts(_work_items(counts), xs, w_gu[0], b_gu[0], w_down[0], b_down[0])
    out = _combine(dest2, ys, h1, topw, row(ln2_g[0]), row(ln2_b[0]))
    return out.reshape(BATCH, SEQ, D)
```

```python
import functools
import math

import numpy as np
import jax
import jax.numpy as jnp
from jax import lax
from jax.experimental import pallas as pl
from jax.experimental.pallas import tpu as pltpu

F32 = jnp.float32
BF16 = jnp.bfloat16
I32 = jnp.int32
U32 = jnp.uint32

D = 1024
BATCH = 4
SEQ = 4096
T = BATCH * SEQ
GLA_H = 4
GLA_DK = 128
GLA_DV = 256
GLA_LR = 16
GLA_TAU = 16.0
GLA_C = 64
FN_G = 4
FN_GW = 128
FN_W = 512
MEM_LEN = 256
MEM_H = 4
MEM_HD = 128
MQ_W = 512
N_EXP = 32
TOP_K = 4
D_FF = 1024
SW_LIMIT = 7.0
SW_ALPHA = 1.702
LN_EPS = 1e-5
RMS_EPS = 1e-6
DN_ALPHA = 2.0 ** 0.25
A_ROWS = T * TOP_K

FFT_N1 = 128
FFT_N2 = 32

LANES = 128
NEG_BIG = -1e30
MIB = 1024 * 1024


def _ln(x, g, b):
    mu = jnp.mean(x, axis=-1, keepdims=True)
    xc = x - mu
    var = jnp.mean(xc * xc, axis=-1, keepdims=True)
    return xc * lax.rsqrt(var + LN_EPS) * g + b


def _dot(a, b):
    return jnp.dot(a, b, preferred_element_type=F32)


def _dot_nt(a, b):
    return lax.dot_general(a, b, (((1,), (1,)), ((), ())), preferred_element_type=F32)


def _dot_tn(a, b):
    return lax.dot_general(a, b, (((0,), (0,)), ((), ())), preferred_element_type=F32)


def _split_bf16(a):
    hi = a.astype(BF16)
    return hi, (a - hi.astype(F32)).astype(BF16)


INPROJ_TM = 1024
INPROJ_TN = 1024


def _inproj_kernel(x_ref, g_ref, b_ref, w_ref, bias_ref, wmq_ref, bmq_ref, wlr_ref, blr_ref,
                   qk_ref, v_ref, r_ref, gates_ref, mq_ref, lr_ref, hb_ref):
    j = pl.program_id(1)

    @pl.when(j == 0)
    def _():
        hb = _ln(x_ref[...], g_ref[...], b_ref[...]).astype(BF16)
        hb_ref[...] = hb
        lr_ref[...] = _dot(hb, wlr_ref[...]) + blr_ref[...]
        mq_ref[...] = (_dot(hb, wmq_ref[...]) + bmq_ref[...]).astype(BF16)

    y = (_dot(hb_ref[...], w_ref[...]) + bias_ref[...]).astype(BF16)
    for idx, ref in enumerate((qk_ref, v_ref, r_ref)):
        @pl.when(j == idx)
        def _(ref=ref):
            ref[...] = y

    @pl.when(j >= 3)
    def _():
        gates_ref[...] = y


def _inproj(x2, ln_g, ln_b, w_main, b_main, w_mq, b_mq, w_lr, b_lr):
    tm, tn = INPROJ_TM, INPROJ_TN
    nj = w_main.shape[1] // tn
    row = lambda i, j: (i, 0)
    const = lambda i, j: (0, 0)
    outs = (
        jax.ShapeDtypeStruct((T, 1024), BF16),
        jax.ShapeDtypeStruct((T, 1024), BF16),
        jax.ShapeDtypeStruct((T, 1024), BF16),
        jax.ShapeDtypeStruct((T, 3072), BF16),
        jax.ShapeDtypeStruct((T, MQ_W), BF16),
        jax.ShapeDtypeStruct((T, LANES), F32),
    )
    return pl.pallas_call(
        _inproj_kernel,
        grid=(T // tm, nj),
        in_specs=[
            pl.BlockSpec((tm, D), row),
            pl.BlockSpec((1, D), const),
            pl.BlockSpec((1, D), const),
            pl.BlockSpec((D, tn), lambda i, j: (0, j)),
            pl.BlockSpec((1, tn), lambda i, j: (0, j)),
            pl.BlockSpec((D, MQ_W), const),
            pl.BlockSpec((1, MQ_W), const),
            pl.BlockSpec((D, LANES), const),
            pl.BlockSpec((1, LANES), const),
        ],
        out_specs=[
            pl.BlockSpec((tm, tn), row),
            pl.BlockSpec((tm, tn), row),
            pl.BlockSpec((tm, tn), row),
            pl.BlockSpec((tm, tn), lambda i, j: (i, jnp.maximum(j - 3, 0))),
            pl.BlockSpec((tm, MQ_W), row),
            pl.BlockSpec((tm, LANES), row),
        ],
        out_shape=outs,
        scratch_shapes=[pltpu.VMEM((tm, D), BF16)],
        compiler_params=pltpu.CompilerParams(
            dimension_semantics=("arbitrary", "arbitrary"),
            vmem_limit_bytes=48 * MIB),
        name="ln_inproj",
    )(x2, ln_g, ln_b, w_main, b_main, w_mq, b_mq, w_lr, b_lr)


GLA_BULK = 256
GLA_FIN = 512
GLA_NCH = SEQ // GLA_C
GLA_CPB = GLA_BULK // GLA_C


def _gla_kernel(q_ref, k_ref, v_ref, r_ref, lr_ref, wd_ref, bd_ref, g_ref, cs_ref, o_ref,
                acc_ref, qin_ref, kst_ref, dec_ref, stf_ref, stb_ref):
    C = GLA_C
    G = GLA_BULK
    DK = GLA_DK
    scale = DK ** -0.5
    ii = lax.broadcasted_iota(I32, (G, G), 0)
    jj = lax.broadcasted_iota(I32, (G, G), 1)
    same = (ii // C) == (jj // C)
    lower = jnp.logical_and(same, ii >= jj)
    upper = jnp.logical_and(same, ii <= jj)
    is_fwd = lax.broadcasted_iota(I32, (G, 2 * DK), 1) < DK

    def bulk(gi, carry):
        r0 = pl.multiple_of(gi * G, G)
        rows = pl.ds(r0, G)
        z = _dot(lr_ref[rows, :], wd_ref[...]) + bd_ref[...]
        la = -(jnp.maximum(-z, 0.0) + jnp.log(1.0 + jnp.exp(-jnp.abs(z)))) * (1.0 / GLA_TAU)
        la_hi, la_lo = _split_bf16(la)
        pre = _dot(cs_ref[...], la_hi) + _dot(cs_ref[...], la_lo)
        blast = jnp.concatenate(
            [jnp.broadcast_to(pre[ci * C + C - 1:ci * C + C, :], (C, 2 * DK))
             for ci in range(GLA_CPB)], axis=0)
        b = jnp.where(is_fwd, pre, blast - pre + la)
        qf32 = q_ref[rows, :].astype(F32)
        kf32 = k_ref[rows, :].astype(F32)
        q2 = jnp.concatenate([qf32, qf32], axis=1)
        k2 = jnp.concatenate([kf32, kf32], axis=1)
        qi = (q2 * (scale * jnp.exp(b))).astype(BF16)
        ki = (k2 * jnp.exp(-b)).astype(BF16)
        qin_ref[rows, :] = qi
        kst_ref[rows, :] = (k2 * jnp.exp(blast - b)).astype(BF16)
        dec = jnp.exp(blast)
        for ci in range(GLA_CPB):
            dec_ref[pl.ds(gi * GLA_CPB + ci, 1), :] = dec[ci * C:ci * C + 1, :]
        att = (jnp.where(lower, _dot_nt(qi[:, :DK], ki[:, :DK]), 0.0)
               + jnp.where(upper, _dot_nt(qi[:, DK:], ki[:, DK:]), 0.0))
        acc_ref[rows, :] = _dot(att.astype(BF16), v_ref[rows, :])
        return carry

    lax.fori_loop(0, SEQ // G, bulk, 0)

    stf_ref[...] = jnp.zeros_like(stf_ref)
    stb_ref[...] = jnp.zeros_like(stb_ref)

    def one(n, lanes, st_ref):
        rows = pl.ds(pl.multiple_of(n * C, C), C)
        st = st_ref[...]
        acc_ref[rows, :] += _dot_nt(qin_ref[rows, lanes], st.astype(BF16))
        st_ref[...] = (st * dec_ref[pl.ds(n, 1), :][:, lanes]
                       + _dot_tn(v_ref[rows, :], kst_ref[rows, lanes]))

    def step(i, carry):
        one(i, slice(0, DK), stf_ref)
        one(GLA_NCH - 1 - i, slice(DK, 2 * DK), stb_ref)
        return carry

    lax.fori_loop(0, GLA_NCH, step, 0, unroll=4)

    def fin(gi, carry):
        rows = pl.ds(pl.multiple_of(gi * GLA_FIN, GLA_FIN), GLA_FIN)
        o = acc_ref[rows, :]
        o = o * lax.rsqrt(jnp.mean(o * o, axis=-1, keepdims=True) + RMS_EPS) * g_ref[...]
        rg = r_ref[rows, :].astype(F32)
        o_ref[rows, :] = (o * (rg * jax.nn.sigmoid(rg))).astype(BF16)
        return carry

    lax.fori_loop(0, SEQ // GLA_FIN, fin, 0)


def _gla(qk, v, r, lr, wd, bd, g):
    i = np.arange(GLA_BULK)
    cs = ((i[:, None] // GLA_C) == (i[None, :] // GLA_C)) & (i[:, None] >= i[None, :])
    cs = jnp.asarray(cs, dtype=F32).astype(BF16)
    return pl.pallas_call(
        _gla_kernel,
        grid=(BATCH, GLA_H),
        in_specs=[
            pl.BlockSpec((SEQ, GLA_DK), lambda b, h: (b, h)),
            pl.BlockSpec((SEQ, GLA_DK), lambda b, h: (b, GLA_H + h)),
            pl.BlockSpec((SEQ, GLA_DV), lambda b, h: (b, h)),
            pl.BlockSpec((SEQ, GLA_DV), lambda b, h: (b, h)),
            pl.BlockSpec((SEQ, LANES), lambda b, h: (b, 0)),
            pl.BlockSpec((LANES, 2 * GLA_DK), lambda b, h: (0, h)),
            pl.BlockSpec((1, 2 * GLA_DK), lambda b, h: (0, h)),
            pl.BlockSpec((1, GLA_DV), lambda b, h: (0, 0)),
            pl.BlockSpec((GLA_BULK, GLA_BULK), lambda b, h: (0, 0)),
        ],
        out_specs=pl.BlockSpec((SEQ, GLA_DV), lambda b, h: (b, h)),
        out_shape=jax.ShapeDtypeStruct((T, GLA_H * GLA_DV), BF16),
        scratch_shapes=[
            pltpu.VMEM((SEQ, GLA_DV), F32),
            pltpu.VMEM((SEQ, 2 * GLA_DK), BF16),
            pltpu.VMEM((SEQ, 2 * GLA_DK), BF16),
            pltpu.VMEM((GLA_NCH, 2 * GLA_DK), F32),
            pltpu.VMEM((GLA_DV, GLA_DK), F32),
            pltpu.VMEM((GLA_DV, GLA_DK), F32),
        ],
        compiler_params=pltpu.CompilerParams(
            dimension_semantics=("arbitrary", "arbitrary"),
            vmem_limit_bytes=48 * MIB),
        name="gla",
    )(qk, qk, v, r, lr, wd, bd, g, cs)


FFT1_S = 16
FFT1_ROWS = FFT_N2 * FFT1_S
FFT2_KB = 4


def _fft1_kernel(x_ref, g_ref, b_ref, w_ref, bias_ref, fbig_ref, cw_ref, sw_ref, o_ref):
    xv = x_ref[...].reshape(FFT1_ROWS, D)
    hb = _ln(xv, g_ref[...], b_ref[...]).astype(BF16)
    fn = (_dot(hb, w_ref[...]) + bias_ref[...]).astype(BF16)
    a = _dot(fbig_ref[...], fn)
    ar = a[:FFT1_ROWS]
    ai = a[FFT1_ROWS:]
    cw = jnp.concatenate([cw_ref[...]] * (FN_W // LANES), axis=1)
    sw = jnp.concatenate([sw_ref[...]] * (FN_W // LANES), axis=1)
    o_ref[0] = (ar * cw + ai * sw).reshape(FFT_N2, FFT1_S, FN_W).astype(BF16)
    o_ref[1] = (ai * cw - ar * sw).reshape(FFT_N2, FFT1_S, FN_W).astype(BF16)


def _fft1(x4, ln_g, ln_b, w_fn, b_fn, fbig, cwt, swt):
    s = FFT1_S
    const = lambda b, j: (0, 0)
    return pl.pallas_call(
        _fft1_kernel,
        grid=(BATCH, FFT_N1 // s),
        in_specs=[
            pl.BlockSpec((None, FFT_N2, s, D), lambda b, j: (b, 0, j, 0)),
            pl.BlockSpec((1, D), const),
            pl.BlockSpec((1, D), const),
            pl.BlockSpec((D, FN_W), const),
            pl.BlockSpec((1, FN_W), const),
            pl.BlockSpec((2 * FFT1_ROWS, FFT1_ROWS), const),
            pl.BlockSpec((None, FFT1_ROWS, LANES), lambda b, j: (j, 0, 0)),
            pl.BlockSpec((None, FFT1_ROWS, LANES), lambda b, j: (j, 0, 0)),
        ],
        out_specs=pl.BlockSpec((None, 2, FFT_N2, s, FN_W), lambda b, j: (b, 0, 0, j, 0)),
        out_shape=jax.ShapeDtypeStruct((BATCH, 2, FFT_N2, FFT_N1, FN_W), BF16),
        compiler_params=pltpu.CompilerParams(
            dimension_semantics=("arbitrary", "arbitrary"),
            vmem_limit_bytes=40 * MIB),
        name="fft_stage1",
    )(x4, ln_g, ln_b, w_fn, b_fn, fbig, cwt, swt)


def _fft2_kernel(d_ref, f2_ref, o_ref):
    f2 = f2_ref[...]
    for kk in range(FFT2_KB):
        z = _dot(f2, jnp.concatenate([d_ref[0, kk], d_ref[1, kk]], axis=0))
        o_ref[0, kk] = z[:FFT_N1].astype(BF16)
        o_ref[1, kk] = z[FFT_N1:].astype(BF16)


def _fft2(dmat, f2):
    kb = FFT2_KB
    blk = (None, 2, kb, FFT_N1, FN_W)
    return pl.pallas_call(
        _fft2_kernel,
        grid=(BATCH, FFT_N2 // kb),
        in_specs=[
            pl.BlockSpec(blk, lambda b, j: (b, 0, j, 0, 0)),
            pl.BlockSpec((2 * FFT_N1, 2 * FFT_N1), lambda b, j: (0, 0)),
        ],
        out_specs=pl.BlockSpec(blk, lambda b, j: (b, 0, j, 0, 0)),
        out_shape=jax.ShapeDtypeStruct((BATCH, 2, FFT_N2, FFT_N1, FN_W), BF16),
        compiler_params=pltpu.CompilerParams(
            dimension_semantics=("arbitrary", "arbitrary")),
        name="fft_stage2",
    )(dmat, f2)


def _dft_tables(merge_tm):
    s = FFT1_S
    n2 = np.arange(FFT_N2, dtype=np.float64)
    n1 = np.arange(FFT_N1, dtype=np.float64)
    th = 2.0 * np.pi * np.outer(n2, n2) / FFT_N2
    f1 = np.stack([np.cos(th), -np.sin(th)]) / math.sqrt(SEQ)
    fbig = np.einsum("rkn,st->rksnt", f1, np.eye(s)).reshape(2 * FFT_N2 * s, FFT_N2 * s)
    tw = 2.0 * np.pi * np.outer(n2, n1) / SEQ
    tw = tw.reshape(FFT_N2, FFT_N1 // s, s).transpose(1, 0, 2).reshape(FFT_N1 // s, FFT_N2 * s)
    cwt = np.broadcast_to(np.cos(tw)[:, :, None], tw.shape + (LANES,))
    swt = np.broadcast_to(np.sin(tw)[:, :, None], tw.shape + (LANES,))
    th1 = 2.0 * np.pi * np.outer(n1, n1) / FFT_N1
    c1, s1 = np.cos(th1), np.sin(th1)
    f2 = np.block([[c1, s1], [-s1, c1]])
    cc = np.arange(FN_GW, dtype=np.float64)
    thc = 2.0 * np.pi * np.outer(cc, cc) / FN_GW
    ccs = np.concatenate([np.cos(thc), np.sin(thc)], axis=0) / math.sqrt(FN_GW)
    k1n = merge_tm // FFT_N2
    r = np.arange(merge_tm)
    perm = np.zeros((merge_tm, merge_tm))
    perm[r, (r % FFT_N2) * k1n + r // FFT_N2] = 1.0
    as32 = lambda a: jnp.asarray(np.ascontiguousarray(a), dtype=F32)
    return (as32(fbig).astype(BF16), as32(cwt), as32(swt), as32(f2).astype(BF16),
            as32(ccs).astype(BF16), as32(perm).astype(BF16))


def _memkv_kernel(m_ref, g_ref, b_ref, w_ref, o_ref):
    mn = _ln(m_ref[...], g_ref[...], b_ref[...]).astype(BF16)
    o_ref[...] = _dot(mn, w_ref[...]).astype(BF16)


def _memkv(mem2, g, b, w):
    return pl.pallas_call(
        _memkv_kernel,
        grid=(BATCH,),
        in_specs=[
            pl.BlockSpec((MEM_LEN, D), lambda i: (i, 0)),
            pl.BlockSpec((1, D), lambda i: (0, 0)),
            pl.BlockSpec((1, D), lambda i: (0, 0)),
            pl.BlockSpec((D, 2 * MQ_W), lambda i: (0, 0)),
        ],
        out_specs=pl.BlockSpec((MEM_LEN, 2 * MQ_W), lambda i: (i, 0)),
        out_shape=jax.ShapeDtypeStruct((BATCH * MEM_LEN, 2 * MQ_W), BF16),
        compiler_params=pltpu.CompilerParams(dimension_semantics=("arbitrary",)),
        name="mem_kv",
    )(mem2, g, b, w)


MERGE_TM = 512
MERGE_K1 = MERGE_TM // FFT_N2


def _pack_bf16_pair(v):
    n = v.shape[1] // 2
    bits = lax.bitcast_convert_type(v.astype(BF16).astype(F32), U32)
    return (bits[:, n:] & jnp.uint32(0xFFFF0000)) | (bits[:, :n] >> 16)


def _unpack_bf16_pair(p):
    lo = lax.bitcast_convert_type(p << 16, F32)
    hi = lax.bitcast_convert_type(p & jnp.uint32(0xFFFF0000), F32)
    return lo, hi


def _merge_kernel(x_ref, og_ref, zr_ref, zi_ref, mq_ref, gt_ref, kv_ref,
                  lng_ref, lnb_ref, wg_ref, ccs_ref, perm_ref, wf_ref, wm_ref, wo_ref, bo_ref,
                  l1g_ref, l1b_ref, wr2_ref, wrh_ref, br_ref,
                  h1_ref, h1p_ref, eidx_ref, topw_ref):
    tm = MERGE_TM
    y_gla = _dot(og_ref[...], wg_ref[...])

    zr = zr_ref[...].reshape(tm, FN_W)
    zi = zi_ref[...].reshape(tm, FN_W)
    ys = []
    for g in range(FN_G):
        sl = slice(g * FN_GW, (g + 1) * FN_GW)
        ys.append(_dot(jnp.concatenate([zr[:, sl], zi[:, sl]], axis=1), ccs_ref[...]))
    yp = jnp.concatenate(ys, axis=1).astype(BF16)
    y_fn = _dot(_dot(perm_ref[...], yp).astype(BF16), wf_ref[...])

    oms = []
    for hd in range(MEM_H):
        sl = slice(hd * MEM_HD, (hd + 1) * MEM_HD)
        s = _dot_nt(mq_ref[:, sl], kv_ref[:, sl]) * (MEM_HD ** -0.5)
        s = s - jnp.max(s, axis=-1, keepdims=True)
        p = jnp.exp(s)
        p = p / jnp.sum(p, axis=-1, keepdims=True)
        oms.append(_dot(p.astype(BF16), kv_ref[:, MQ_W + hd * MEM_HD:MQ_W + (hd + 1) * MEM_HD]))
    y_mem = _dot(jnp.concatenate(oms, axis=1).astype(BF16), wm_ref[...])

    merged = (jax.nn.sigmoid(gt_ref[:, 0:D].astype(F32)) * y_gla
              + jax.nn.sigmoid(gt_ref[:, D:2 * D].astype(F32)) * y_fn
              + jax.nn.sigmoid(gt_ref[:, 2 * D:3 * D].astype(F32)) * y_mem)
    mix = _dot(merged.astype(BF16), wo_ref[...]) + bo_ref[...]
    h = _ln(x_ref[...], lng_ref[...], lnb_ref[...])
    h1 = _ln(DN_ALPHA * h + mix, l1g_ref[...], l1b_ref[...])
    h1_ref[...] = h1
    h1p_ref[...] = _pack_bf16_pair(h1)

    h_hi, h_lo = _split_bf16(h1)
    d2 = _dot(h_hi, wr2_ref[...])
    logits = d2[:, :LANES] + d2[:, LANES:] + _dot(h_lo, wrh_ref[...]) + br_ref[...]
    lane = lax.broadcasted_iota(I32, (tm, LANES), 1)
    l = logits
    vals, idxs = [], []
    for _ in range(TOP_K):
        m = jnp.max(l, axis=-1, keepdims=True)
        idx = jnp.min(jnp.where(l == m, lane, LANES), axis=-1, keepdims=True)
        vals.append(m)
        idxs.append(idx)
        l = jnp.where(lane == idx, -jnp.inf, l)
    es = [jnp.exp(v - vals[0]) for v in vals]
    den = es[0] + es[1] + es[2] + es[3]
    eo = jnp.zeros((tm, LANES), I32)
    wo = jnp.zeros((tm, LANES), F32)
    for k in range(TOP_K):
        eo = jnp.where(lane == k, idxs[k], eo)
        wo = jnp.where(lane == k, es[k] / den, wo)
    eidx_ref[...] = eo
    topw_ref[...] = wo


def _merge(x2, og, z, mq, gates, kv, lng, lnb, wg, ccs, perm, wf, wm, wo, bo, l1g, l1b,
           wr2, wrh, br):
    tm = MERGE_TM
    per_b = SEQ // tm
    row = lambda i: (i, 0)
    const = lambda i: (0, 0)
    zblk = (None, None, FFT_N2, MERGE_K1, FN_W)
    outs = (
        jax.ShapeDtypeStruct((T, D), F32),
        jax.ShapeDtypeStruct((T, D // 2), U32),
        jax.ShapeDtypeStruct((T, LANES), I32),
        jax.ShapeDtypeStruct((T, LANES), F32),
    )
    return pl.pallas_call(
        _merge_kernel,
        grid=(T // tm,),
        in_specs=[
            pl.BlockSpec((tm, D), row),
            pl.BlockSpec((tm, D), row),
            pl.BlockSpec(zblk, lambda i: (i // per_b, 0, 0, i % per_b, 0)),
            pl.BlockSpec(zblk, lambda i: (i // per_b, 1, 0, i % per_b, 0)),
            pl.BlockSpec((tm, MQ_W), row),
            pl.BlockSpec((tm, 3 * D), row),
            pl.BlockSpec((MEM_LEN, 2 * MQ_W), lambda i: (i // per_b, 0)),
            pl.BlockSpec((1, D), const), pl.BlockSpec((1, D), const),
            pl.BlockSpec((D, D), const),
            pl.BlockSpec((2 * FN_GW, FN_GW), const),
            pl.BlockSpec((tm, tm), const),
            pl.BlockSpec((FN_W, D), const),
            pl.BlockSpec((MQ_W, D), const),
            pl.BlockSpec((D, D), const),
            pl.BlockSpec((1, D), const),
            pl.BlockSpec((1, D), const), pl.BlockSpec((1, D), const),
            pl.BlockSpec((D, 2 * LANES), const),
            pl.BlockSpec((D, LANES), const),
            pl.BlockSpec((1, LANES), const),
        ],
        out_specs=[
            pl.BlockSpec((tm, D), row),
            pl.BlockSpec((tm, D // 2), row),
            pl.BlockSpec((tm, LANES), row),
            pl.BlockSpec((tm, LANES), row),
        ],
        out_shape=outs,
        compiler_params=pltpu.CompilerParams(
            dimension_semantics=("arbitrary",),
            vmem_limit_bytes=58 * MIB),
        name="merge_ln1_router",
    )(x2, og, z, z, mq, gates, kv, lng, lnb, wg, ccs, perm, wf, wm, wo, bo, l1g, l1b,
      wr2, wrh, br)


PLAN_TP = 512


def _plan_kernel(e_ref, dest_ref, cnt_out_ref, cnt_ref, off_ref):
    p = pl.program_id(0)
    i = pl.program_id(1)
    tp = PLAN_TP
    lane = lax.broadcasted_iota(I32, (tp, LANES), 1)
    e = e_ref[...]
    onehots = [lane == e[:, k:k + 1] for k in range(TOP_K)]
    mf = jnp.zeros((tp, LANES), F32)
    for oh in onehots:
        mf = mf + jnp.where(oh, 1.0, 0.0)
    colsum = jnp.sum(mf, axis=0, keepdims=True)

    @pl.when(jnp.logical_and(p == 0, i == 0))
    def _():
        cnt_ref[...] = jnp.zeros_like(cnt_ref)

    @pl.when(jnp.logical_and(p == 1, i == 0))
    def _():
        tot = cnt_ref[...]
        cnt_out_ref[...] = jnp.broadcast_to(tot, cnt_out_ref.shape)
        lane1 = lax.broadcasted_iota(I32, (1, LANES), 1)
        inc = tot
        for s in (1, 2, 4, 8, 16, 32, 64):
            inc = inc + jnp.where(lane1 >= s, pltpu.roll(inc, s, 1), 0.0)
        off_ref[...] = inc - tot
        cnt_ref[...] = jnp.zeros_like(cnt_ref)

    @pl.when(p == 0)
    def _():
        dest_ref[...] = jnp.zeros_like(dest_ref)

    @pl.when(p == 1)
    def _():
        ri = lax.broadcasted_iota(I32, (tp, tp), 0)
        ci = lax.broadcasted_iota(I32, (tp, tp), 1)
        ltri = jnp.where(ri > ci, 1.0, 0.0).astype(BF16)
        rank = _dot(ltri, mf.astype(BF16)) + cnt_ref[...] + off_ref[...]
        out = jnp.zeros((tp, LANES), I32)
        for k in range(TOP_K):
            dk = jnp.sum(jnp.where(onehots[k], rank, 0.0), axis=-1, keepdims=True)
            out = jnp.where(lane == k, dk.astype(I32), out)
        dest_ref[...] = out

    cnt_ref[...] += colsum


def _plan(eidx):
    tp = PLAN_TP
    return pl.pallas_call(
        _plan_kernel,
        grid=(2, T // tp),
        in_specs=[pl.BlockSpec((tp, LANES), lambda p, i: (i, 0))],
        out_specs=[
            pl.BlockSpec((tp, LANES), lambda p, i: (i * p, 0)),
            pl.BlockSpec((8, LANES), lambda p, i: (0, 0)),
        ],
        out_shape=(jax.ShapeDtypeStruct((T, LANES), I32),
                   jax.ShapeDtypeStruct((8, LANES), F32)),
        scratch_shapes=[pltpu.VMEM((1, LANES), F32), pltpu.VMEM((1, LANES), F32)],
        compiler_params=pltpu.CompilerParams(
            dimension_semantics=("arbitrary", "arbitrary")),
        name="route_plan",
    )(eidx)


DISP_TM = 256
DISP_N = DISP_TM * TOP_K
ROWS_PER_ISSUE = 8


def _dispatch_kernel(dest_hbm, h_ref, xs_hbm, idx_smem, isem, sem):
    i = pl.program_id(0)
    slot = lax.rem(i, 2)

    def idx_copy(t, s):
        return pltpu.make_async_copy(dest_hbm.at[pl.ds(t, 1)], idx_smem.at[pl.ds(s, 1)],
                                     isem.at[s])

    @pl.when(i == 0)
    def _():
        idx_copy(0, 0).start()

    idx_copy(i, slot).wait()

    @pl.when(i + 1 < pl.num_programs(0))
    def _():
        idx_copy(i + 1, 1 - slot).start()

    def row_copies(ro):
        r0 = pl.multiple_of(ro * ROWS_PER_ISSUE, ROWS_PER_ISSUE)
        a0 = ro * (ROWS_PER_ISSUE * TOP_K)
        return [pltpu.make_async_copy(h_ref.at[pl.ds(r0 + ri, 1)],
                                      xs_hbm.at[pl.ds(idx_smem[slot, a0 + ri * TOP_K + k], 1)],
                                      sem)
                for ri in range(ROWS_PER_ISSUE) for k in range(TOP_K)]

    def issue(ro, c):
        for n, cp_ in enumerate(row_copies(ro)):
            cp_.start(priority=n % 2)
        return c

    lax.fori_loop(0, DISP_TM // ROWS_PER_ISSUE, issue, 0)

    def drain(ro, c):
        for cp_ in row_copies(ro):
            cp_.wait()
        return c

    lax.fori_loop(0, DISP_TM // ROWS_PER_ISSUE, drain, 0)


def _dispatch(dest2, h1p):
    return pl.pallas_call(
        _dispatch_kernel,
        grid=(T // DISP_TM,),
        in_specs=[
            pl.BlockSpec(memory_space=pl.ANY),
            pl.BlockSpec((DISP_TM, D // 2), lambda i: (i, 0)),
        ],
        out_specs=pl.BlockSpec(memory_space=pl.ANY),
        out_shape=jax.ShapeDtypeStruct((A_ROWS, D // 2), U32),
        scratch_shapes=[
            pltpu.SMEM((2, DISP_N), I32),
            pltpu.SemaphoreType.DMA((2,)),
            pltpu.SemaphoreType.DMA,
        ],
        compiler_params=pltpu.CompilerParams(dimension_semantics=("arbitrary",)),
        name="moe_dispatch",
    )(dest2, h1p)


MOE_BM = 512
MOE_NBLK = A_ROWS // MOE_BM
MOE_NW = MOE_NBLK + N_EXP


def _expert_kernel(we_ref, wb_ref, wlo_ref, whi_ref, wfe_ref, wfb_ref, wsl_ref, wnx_ref,
                   x_ref, wgu_hbm, bgu_ref, wdn_hbm, bdn_ref, o_ref,
                   wgu_f32, wdn_f32, wgu_bf, wdn_bf, sems):
    w = pl.program_id(0)
    e = we_ref[w]
    lo = wlo_ref[w]
    hi = whi_ref[w]

    def weight_copies(expert, slot):
        return (pltpu.make_async_copy(wgu_hbm.at[expert], wgu_f32.at[slot], sems.at[slot, 0]),
                pltpu.make_async_copy(wdn_hbm.at[expert], wdn_f32.at[slot], sems.at[slot, 1]))

    @pl.when(w == 0)
    def _():
        for cp_ in weight_copies(e, 0):
            cp_.start()

    @pl.when(wfe_ref[w] == 1)
    def _():
        slot = wsl_ref[w]
        for cp_ in weight_copies(e, slot):
            cp_.wait()
        wgu_bf[...] = wgu_f32[slot].astype(BF16)
        wdn_bf[...] = wdn_f32[slot].astype(BF16)
        nxt = wnx_ref[w]

        @pl.when(nxt >= 0)
        def _():
            for cp_ in weight_copies(nxt, 1 - slot):
                cp_.start()

    @pl.when(wfb_ref[w] == 1)
    def _():
        o_ref[...] = jnp.zeros_like(o_ref)

    @pl.when(hi > lo)
    def _():
        xlo, xhi = _unpack_bf16_pair(x_ref[...])
        half = D // 2
        gu = (_dot(xlo.astype(BF16), wgu_bf[:half, :]) + _dot(xhi.astype(BF16), wgu_bf[half:, :])
              + bgu_ref[pl.ds(e, 1), :])
        gate = jnp.minimum(gu[:, :D_FF], SW_LIMIT)
        up = jnp.clip(gu[:, D_FF:], -SW_LIMIT, SW_LIMIT)
        act = (up + 1.0) * (gate * jax.nn.sigmoid(SW_ALPHA * gate))
        out = _dot(act.astype(BF16), wdn_bf[...]) + bdn_ref[pl.ds(e, 1), :]
        rid = lax.broadcasted_iota(I32, (MOE_BM, 1), 0)
        keep = jnp.logical_and(rid >= lo, rid < hi)
        o_ref[...] = jnp.where(keep, _pack_bf16_pair(out), o_ref[...])


def _experts(meta, xs, w_gu, b_gu, w_down, b_down):
    return pl.pallas_call(
        _expert_kernel,
        grid_spec=pltpu.PrefetchScalarGridSpec(
            num_scalar_prefetch=len(meta),
            grid=(MOE_NW,),
            in_specs=[
                pl.BlockSpec((MOE_BM, D // 2), lambda w, we, wb, *_: (wb[w], 0)),
                pl.BlockSpec(memory_space=pl.ANY),
                pl.BlockSpec((N_EXP, 2 * D_FF), lambda w, *_: (0, 0)),
                pl.BlockSpec(memory_space=pl.ANY),
                pl.BlockSpec((N_EXP, D), lambda w, *_: (0, 0)),
            ],
            out_specs=pl.BlockSpec((MOE_BM, D // 2), lambda w, we, wb, *_: (wb[w], 0)),
            scratch_shapes=[
                pltpu.VMEM((2, D, 2 * D_FF), F32),
                pltpu.VMEM((2, D_FF, D), F32),
                pltpu.VMEM((D, 2 * D_FF), BF16),
                pltpu.VMEM((D_FF, D), BF16),
                pltpu.SemaphoreType.DMA((2, 2)),
            ],
        ),
        out_shape=jax.ShapeDtypeStruct((A_ROWS, D // 2), U32),
        compiler_params=pltpu.CompilerParams(
            dimension_semantics=("arbitrary",),
            vmem_limit_bytes=56 * MIB),
        name="moe_experts",
    )(*meta, xs, w_gu, b_gu, w_down, b_down)


def _work_items(counts):
    bm = MOE_BM
    end = jnp.cumsum(counts)
    start = end - counts
    first = start // bm
    last = jnp.maximum(end - 1, 0) // bm
    n_e = jnp.where(counts > 0, last - first + 1, 0)
    item_end = jnp.cumsum(n_e)
    item_start = item_end - n_e
    total = item_end[-1]
    w = jnp.arange(MOE_NW, dtype=I32)
    wc = jnp.minimum(w, total - 1)
    e_w = jnp.sum((item_end[None, :] <= wc[:, None]).astype(I32), axis=1)
    e_w = jnp.minimum(e_w, N_EXP - 1)
    blk = first[e_w] + (wc - item_start[e_w])
    lo = jnp.maximum(start[e_w], blk * bm) - blk * bm
    hi = jnp.minimum(end[e_w], (blk + 1) * bm) - blk * bm
    valid = w < total
    lo = jnp.where(valid, lo, 0)
    hi = jnp.where(valid, hi, 0)
    prev_e = jnp.concatenate([jnp.full((1,), -1, I32), e_w[:-1]])
    prev_b = jnp.concatenate([jnp.full((1,), -1, I32), blk[:-1]])
    fe = (e_w != prev_e).astype(I32)
    fb = (blk != prev_b).astype(I32)
    slot = (jnp.cumsum(fe) - 1) % 2
    first_at = jnp.where(fe == 1, w, MOE_NW)
    next_first = jnp.concatenate([lax.cummin(first_at, reverse=True)[1:],
                                  jnp.full((1,), MOE_NW, I32)])
    nxt = jnp.where(next_first < MOE_NW, e_w[jnp.minimum(next_first, MOE_NW - 1)], -1)
    return tuple(a.astype(I32) for a in (e_w, blk, lo, hi, fe, fb, slot, nxt))


COMB_TM = 256
COMB_N = COMB_TM * TOP_K


def _combine_kernel(dest_hbm, ys_hbm, h1_ref, tw_ref, g_ref, b_ref, o_ref,
                    idx_smem, gbuf, isem, sem):
    i = pl.program_id(0)
    n_tiles = pl.num_programs(0)

    def idx_copy(t):
        s = lax.rem(t, 3)
        return pltpu.make_async_copy(dest_hbm.at[pl.ds(t, 1)], idx_smem.at[pl.ds(s, 1)],
                                     isem.at[s])

    def row_copies(t, ro):
        si = lax.rem(t, 3)
        sb = lax.rem(t, 2)
        r0 = pl.multiple_of(ro * ROWS_PER_ISSUE, ROWS_PER_ISSUE)
        a0 = ro * (ROWS_PER_ISSUE * TOP_K)
        return [pltpu.make_async_copy(ys_hbm.at[pl.ds(idx_smem[si, a0 + ri * TOP_K + k], 1)],
                                      gbuf.at[sb, k, pl.ds(r0 + ri, 1)], sem.at[sb])
                for ri in range(ROWS_PER_ISSUE) for k in range(TOP_K)]

    def issue_tile(t):
        def issue(ro, c):
            for n, cp_ in enumerate(row_copies(t, ro)):
                cp_.start(priority=n % 2)
            return c
        lax.fori_loop(0, COMB_TM // ROWS_PER_ISSUE, issue, 0)

    @pl.when(i == 0)
    def _():
        idx_copy(0).start()
        idx_copy(0).wait()
        issue_tile(0)

        @pl.when(n_tiles > 1)
        def _():
            idx_copy(1).start()

    @pl.when(i + 2 < n_tiles)
    def _():
        idx_copy(i + 2).start()

    @pl.when(i + 1 < n_tiles)
    def _():
        idx_copy(i + 1).wait()
        issue_tile(i + 1)

    def drain(ro, c):
        for cp_ in row_copies(i, ro):
            cp_.wait()
        return c

    lax.fori_loop(0, COMB_TM // ROWS_PER_ISSUE, drain, 0)

    gcur = gbuf.at[lax.rem(i, 2)]
    tw = tw_ref[...]
    ylo = jnp.zeros((COMB_TM, D // 2), F32)
    yhi = jnp.zeros((COMB_TM, D // 2), F32)
    for k in range(TOP_K):
        lo, hi = _unpack_bf16_pair(gcur[k])
        wk = tw[:, k:k + 1]
        ylo = ylo + lo * wk
        yhi = yhi + hi * wk
    ff = jnp.concatenate([ylo, yhi], axis=1)
    o_ref[...] = _ln(DN_ALPHA * h1_ref[...] + ff, g_ref[...], b_ref[...])


def _combine(dest2, ys, h1, topw, g, b):
    tm = COMB_TM
    return pl.pallas_call(
        _combine_kernel,
        grid=(T // tm,),
        in_specs=[
            pl.BlockSpec(memory_space=pl.ANY),
            pl.BlockSpec(memory_space=pl.ANY),
            pl.BlockSpec((tm, D), lambda i: (i, 0)),
            pl.BlockSpec((tm, LANES), lambda i: (i, 0)),
            pl.BlockSpec((1, D), lambda i: (0, 0)),
            pl.BlockSpec((1, D), lambda i: (0, 0)),
        ],
        out_specs=pl.BlockSpec((tm, D), lambda i: (i, 0)),
        out_shape=jax.ShapeDtypeStruct((T, D), F32),
        scratch_shapes=[
            pltpu.SMEM((3, COMB_N), I32),
            pltpu.VMEM((2, TOP_K, tm, D // 2), U32),
            pltpu.SemaphoreType.DMA((3,)),
            pltpu.SemaphoreType.DMA((2,)),
        ],
        compiler_params=pltpu.CompilerParams(dimension_semantics=("arbitrary",)),
        name="moe_combine_ln2",
    )(dest2, ys, h1, topw, g, b)


def _pad_cols(a, n):
    return jnp.pad(a, ((0, 0), (0, n - a.shape[1])))


def kernel(x, mem, ln_in_g, ln_in_b, ln_mem_g, ln_mem_b, w_in, b_in, w_decay_f, b_decay_f,
           w_decay_b, b_decay_b, gla_norm_g, w_br_gla, w_br_fnet, w_br_mem, w_mem_kv, w_out,
           b_out, ln1_g, ln1_b, w_router, b_router, w_gu, b_gu, w_down, b_down, ln2_g, ln2_b):
    assert x.shape == (BATCH, SEQ, D) and w_in.shape[0] == 1
    row = lambda a: a.reshape(1, -1)
    x2 = x.reshape(T, D)
    w_in0, b_in0 = w_in[0], b_in[0]
    c_lr, c_fn, c_mq, c_gt = 3072, 3072 + 2 * GLA_LR, 3104 + FN_W, 3104 + FN_W + MQ_W
    w_main = jnp.concatenate([w_in0[:, :c_lr], w_in0[:, c_gt:]], axis=1).astype(BF16)
    b_main = row(jnp.concatenate([b_in0[:c_lr], b_in0[c_gt:]]))
    w_lr = _pad_cols(w_in0[:, c_lr:c_fn], LANES).astype(BF16)
    b_lr = _pad_cols(row(b_in0[c_lr:c_fn]), LANES)
    w_mq = w_in0[:, c_mq:c_gt].astype(BF16)
    b_mq = row(b_in0[c_mq:c_gt])
    w_fn = w_in0[:, c_fn:c_mq].astype(BF16)
    b_fn = row(b_in0[c_fn:c_mq])
    lng, lnb = row(ln_in_g), row(ln_in_b)

    qk, v, r, gates, mq, lr = _inproj(x2, lng, lnb, w_main, b_main, w_mq, b_mq, w_lr, b_lr)

    zpad = jnp.zeros((LANES - 2 * GLA_LR, GLA_H * GLA_DK), F32)
    zlr = jnp.zeros((GLA_LR, GLA_H * GLA_DK), F32)
    wdf = jnp.concatenate([w_decay_f[0], zlr, zpad], axis=0).reshape(LANES, GLA_H, GLA_DK)
    wdb = jnp.concatenate([zlr, w_decay_b[0], zpad], axis=0).reshape(LANES, GLA_H, GLA_DK)
    wd = jnp.concatenate([wdf, wdb], axis=2).reshape(LANES, GLA_H * 2 * GLA_DK)
    bd = jnp.concatenate([b_decay_f[0].reshape(GLA_H, GLA_DK),
                          b_decay_b[0].reshape(GLA_H, GLA_DK)], axis=1).reshape(1, -1)
    og = _gla(qk, v, r, lr, wd, bd, row(gla_norm_g[0]))

    fbig, cwt, swt, f2, ccs, perm = _dft_tables(MERGE_TM)
    x4 = x.reshape(BATCH, FFT_N2, FFT_N1, D)
    z = _fft2(_fft1(x4, lng, lnb, w_fn, b_fn, fbig, cwt, swt), f2)

    kv = _memkv(mem.reshape(BATCH * MEM_LEN, D), row(ln_mem_g), row(ln_mem_b),
                w_mem_kv[0].astype(BF16))

    w_r = _pad_cols(w_router[0], LANES)
    wr_hi = w_r.astype(BF16)
    wr_lo = (w_r - wr_hi.astype(F32)).astype(BF16)
    b_r = jnp.concatenate([row(b_router[0]),
                           jnp.full((1, LANES - N_EXP), NEG_BIG, F32)], axis=1)
    h1, h1p, eidx, topw = _merge(
        x2, og, z, mq, gates, kv, lng, lnb,
        w_br_gla[0].astype(BF16), ccs, perm, w_br_fnet[0].astype(BF16),
        w_br_mem[0].astype(BF16), w_out[0].astype(BF16), row(b_out[0]),
        row(ln1_g[0]), row(ln1_b[0]), jnp.concatenate([wr_hi, wr_lo], axis=1), wr_hi, b_r)

    dest, cnt = _plan(eidx)
    counts = cnt[0, :N_EXP].astype(I32)
    dest2 = dest[:, :TOP_K].reshape(T // DISP_TM, DISP_N)
    xs = _dispatch(dest2, h1p)
    ys = _experts(_work_items(counts), xs, w_gu[0], b_gu[0], w_down[0], b_down[0])
    out = _combine(dest2, ys, h1, topw, row(ln2_g[0]), row(ln2_b[0]))
    return out.reshape(BATCH, SEQ, D)
```

```python
import functools
import math

import numpy as np
import jax
import jax.numpy as jnp
from jax import lax
from jax.experimental import pallas as pl
from jax.experimental.pallas import tpu as pltpu
from jax.experimental.pallas import tpu_sc as plsc

F32 = jnp.float32
BF16 = jnp.bfloat16
I32 = jnp.int32
U32 = jnp.uint32

D = 1024
BATCH = 4
SEQ = 4096
T = BATCH * SEQ
GLA_H = 4
GLA_DK = 128
GLA_DV = 256
GLA_LR = 16
GLA_TAU = 16.0
GLA_C = 64
FN_G = 4
FN_GW = 128
FN_W = 512
MEM_LEN = 256
MEM_H = 4
MEM_HD = 128
MQ_W = 512
N_EXP = 32
TOP_K = 4
D_FF = 1024
SW_LIMIT = 7.0
SW_ALPHA = 1.702
LN_EPS = 1e-5
RMS_EPS = 1e-6
DN_ALPHA = 2.0 ** 0.25
A_ROWS = T * TOP_K

FFT_N1 = 128
FFT_N2 = 32

LANES = 128
NEG_BIG = -1e30
MIB = 1024 * 1024


def _ln(x, g, b):
    mu = jnp.mean(x, axis=-1, keepdims=True)
    xc = x - mu
    var = jnp.mean(xc * xc, axis=-1, keepdims=True)
    return xc * lax.rsqrt(var + LN_EPS) * g + b


def _dot(a, b):
    return jnp.dot(a, b, preferred_element_type=F32)


def _dot_nt(a, b):
    return lax.dot_general(a, b, (((1,), (1,)), ((), ())), preferred_element_type=F32)


def _dot_tn(a, b):
    return lax.dot_general(a, b, (((0,), (0,)), ((), ())), preferred_element_type=F32)


def _split_bf16(a):
    hi = a.astype(BF16)
    return hi, (a - hi.astype(F32)).astype(BF16)


INPROJ_TM = 1024
INPROJ_TN = 1024


def _inproj_kernel(x_ref, g_ref, b_ref, w_ref, bias_ref, wmq_ref, bmq_ref, wlr_ref, blr_ref,
                   qk_ref, v_ref, r_ref, gates_ref, mq_ref, lr_ref, hb_ref):
    j = pl.program_id(1)

    @pl.when(j == 0)
    def _():
        hb = _ln(x_ref[...], g_ref[...], b_ref[...]).astype(BF16)
        hb_ref[...] = hb
        lr_ref[...] = _dot(hb, wlr_ref[...]) + blr_ref[...]
        mq_ref[...] = (_dot(hb, wmq_ref[...]) + bmq_ref[...]).astype(BF16)

    y = (_dot(hb_ref[...], w_ref[...]) + bias_ref[...]).astype(BF16)
    for idx, ref in enumerate((qk_ref, v_ref, r_ref)):
        @pl.when(j == idx)
        def _(ref=ref):
            ref[...] = y

    @pl.when(j >= 3)
    def _():
        gates_ref[...] = y


def _inproj(x2, ln_g, ln_b, w_main, b_main, w_mq, b_mq, w_lr, b_lr):
    tm, tn = INPROJ_TM, INPROJ_TN
    nj = w_main.shape[1] // tn
    row = lambda i, j: (i, 0)
    const = lambda i, j: (0, 0)
    outs = (
        jax.ShapeDtypeStruct((T, 1024), BF16),
        jax.ShapeDtypeStruct((T, 1024), BF16),
        jax.ShapeDtypeStruct((T, 1024), BF16),
        jax.ShapeDtypeStruct((T, 3072), BF16),
        jax.ShapeDtypeStruct((T, MQ_W), BF16),
        jax.ShapeDtypeStruct((T, LANES), F32),
    )
    return pl.pallas_call(
        _inproj_kernel,
        grid=(T // tm, nj),
        in_specs=[
            pl.BlockSpec((tm, D), row),
            pl.BlockSpec((1, D), const),
            pl.BlockSpec((1, D), const),
            pl.BlockSpec((D, tn), lambda i, j: (0, j)),
            pl.BlockSpec((1, tn), lambda i, j: (0, j)),
            pl.BlockSpec((D, MQ_W), const),
            pl.BlockSpec((1, MQ_W), const),
            pl.BlockSpec((D, LANES), const),
            pl.BlockSpec((1, LANES), const),
        ],
        out_specs=[
            pl.BlockSpec((tm, tn), row),
            pl.BlockSpec((tm, tn), row),
            pl.BlockSpec((tm, tn), row),
            pl.BlockSpec((tm, tn), lambda i, j: (i, jnp.maximum(j - 3, 0))),
            pl.BlockSpec((tm, MQ_W), row),
            pl.BlockSpec((tm, LANES), row),
        ],
        out_shape=outs,
        scratch_shapes=[pltpu.VMEM((tm, D), BF16)],
        compiler_params=pltpu.CompilerParams(
            dimension_semantics=("arbitrary", "arbitrary"),
            vmem_limit_bytes=48 * MIB),
        name="ln_inproj",
    )(x2, ln_g, ln_b, w_main, b_main, w_mq, b_mq, w_lr, b_lr)


GLA_BULK = 256
GLA_FIN = 512
GLA_NCH = SEQ // GLA_C
GLA_CPB = GLA_BULK // GLA_C


def _gla_kernel(q_ref, k_ref, v_ref, r_ref, lr_ref, wd_ref, bd_ref, g_ref, cs_ref, o_ref,
                acc_ref, qin_ref, kst_ref, dec_ref, stf_ref, stb_ref):
    C = GLA_C
    G = GLA_BULK
    DK = GLA_DK
    scale = DK ** -0.5
    ii = lax.broadcasted_iota(I32, (G, G), 0)
    jj = lax.broadcasted_iota(I32, (G, G), 1)
    same = (ii // C) == (jj // C)
    lower = jnp.logical_and(same, ii >= jj)
    upper = jnp.logical_and(same, ii <= jj)
    is_fwd = lax.broadcasted_iota(I32, (G, 2 * DK), 1) < DK

    def bulk(gi, carry):
        r0 = pl.multiple_of(gi * G, G)
        rows = pl.ds(r0, G)
        z = _dot(lr_ref[rows, :], wd_ref[...]) + bd_ref[...]
        la = -(jnp.maximum(-z, 0.0) + jnp.log(1.0 + jnp.exp(-jnp.abs(z)))) * (1.0 / GLA_TAU)
        la_hi, la_lo = _split_bf16(la)
        pre = _dot(cs_ref[...], la_hi) + _dot(cs_ref[...], la_lo)
        blast = jnp.concatenate(
            [jnp.broadcast_to(pre[ci * C + C - 1:ci * C + C, :], (C, 2 * DK))
             for ci in range(GLA_CPB)], axis=0)
        b = jnp.where(is_fwd, pre, blast - pre + la)
        qf32 = q_ref[rows, :].astype(F32)
        kf32 = k_ref[rows, :].astype(F32)
        q2 = jnp.concatenate([qf32, qf32], axis=1)
        k2 = jnp.concatenate([kf32, kf32], axis=1)
        qi = (q2 * (scale * jnp.exp(b))).astype(BF16)
        ki = (k2 * jnp.exp(-b)).astype(BF16)
        qin_ref[rows, :] = qi
        kst_ref[rows, :] = (k2 * jnp.exp(blast - b)).astype(BF16)
        dec = jnp.exp(blast)
        for ci in range(GLA_CPB):
            dec_ref[pl.ds(gi * GLA_CPB + ci, 1), :] = dec[ci * C:ci * C + 1, :]
        att = (jnp.where(lower, _dot_nt(qi[:, :DK], ki[:, :DK]), 0.0)
               + jnp.where(upper, _dot_nt(qi[:, DK:], ki[:, DK:]), 0.0))
        acc_ref[rows, :] = _dot(att.astype(BF16), v_ref[rows, :])
        return carry

    lax.fori_loop(0, SEQ // G, bulk, 0)

    stf_ref[...] = jnp.zeros_like(stf_ref)
    stb_ref[...] = jnp.zeros_like(stb_ref)

    def one(n, lanes, st_ref):
        rows = pl.ds(pl.multiple_of(n * C, C), C)
        st = st_ref[...]
        acc_ref[rows, :] += _dot_nt(qin_ref[rows, lanes], st.astype(BF16))
        st_ref[...] = (st * dec_ref[pl.ds(n, 1), :][:, lanes]
                       + _dot_tn(v_ref[rows, :], kst_ref[rows, lanes]))

    def step(i, carry):
        one(i, slice(0, DK), stf_ref)
        one(GLA_NCH - 1 - i, slice(DK, 2 * DK), stb_ref)
        return carry

    lax.fori_loop(0, GLA_NCH, step, 0, unroll=4)

    def fin(gi, carry):
        rows = pl.ds(pl.multiple_of(gi * GLA_FIN, GLA_FIN), GLA_FIN)
        o = acc_ref[rows, :]
        o = o * lax.rsqrt(jnp.mean(o * o, axis=-1, keepdims=True) + RMS_EPS) * g_ref[...]
        rg = r_ref[rows, :].astype(F32)
        o_ref[rows, :] = (o * (rg * jax.nn.sigmoid(rg))).astype(BF16)
        return carry

    lax.fori_loop(0, SEQ // GLA_FIN, fin, 0)


def _gla(qk, v, r, lr, wd, bd, g):
    i = np.arange(GLA_BULK)
    cs = ((i[:, None] // GLA_C) == (i[None, :] // GLA_C)) & (i[:, None] >= i[None, :])
    cs = jnp.asarray(cs, dtype=F32).astype(BF16)
    return pl.pallas_call(
        _gla_kernel,
        grid=(BATCH, GLA_H),
        in_specs=[
            pl.BlockSpec((SEQ, GLA_DK), lambda b, h: (b, h)),
            pl.BlockSpec((SEQ, GLA_DK), lambda b, h: (b, GLA_H + h)),
            pl.BlockSpec((SEQ, GLA_DV), lambda b, h: (b, h)),
            pl.BlockSpec((SEQ, GLA_DV), lambda b, h: (b, h)),
            pl.BlockSpec((SEQ, LANES), lambda b, h: (b, 0)),
            pl.BlockSpec((LANES, 2 * GLA_DK), lambda b, h: (0, h)),
            pl.BlockSpec((1, 2 * GLA_DK), lambda b, h: (0, h)),
            pl.BlockSpec((1, GLA_DV), lambda b, h: (0, 0)),
            pl.BlockSpec((GLA_BULK, GLA_BULK), lambda b, h: (0, 0)),
        ],
        out_specs=pl.BlockSpec((SEQ, GLA_DV), lambda b, h: (b, h)),
        out_shape=jax.ShapeDtypeStruct((T, GLA_H * GLA_DV), BF16),
        scratch_shapes=[
            pltpu.VMEM((SEQ, GLA_DV), F32),
            pltpu.VMEM((SEQ, 2 * GLA_DK), BF16),
            pltpu.VMEM((SEQ, 2 * GLA_DK), BF16),
            pltpu.VMEM((GLA_NCH, 2 * GLA_DK), F32),
            pltpu.VMEM((GLA_DV, GLA_DK), F32),
            pltpu.VMEM((GLA_DV, GLA_DK), F32),
        ],
        compiler_params=pltpu.CompilerParams(
            dimension_semantics=("arbitrary", "arbitrary"),
            vmem_limit_bytes=48 * MIB),
        name="gla",
    )(qk, qk, v, r, lr, wd, bd, g, cs)


FFT1_S = 16
FFT1_ROWS = FFT_N2 * FFT1_S
FFT2_KB = 4


def _fft1_kernel(x_ref, g_ref, b_ref, w_ref, bias_ref, fbig_ref, cw_ref, sw_ref, o_ref):
    xv = x_ref[...].reshape(FFT1_ROWS, D)
    hb = _ln(xv, g_ref[...], b_ref[...]).astype(BF16)
    fn = (_dot(hb, w_ref[...]) + bias_ref[...]).astype(BF16)
    a = _dot(fbig_ref[...], fn)
    ar = a[:FFT1_ROWS]
    ai = a[FFT1_ROWS:]
    cw = jnp.concatenate([cw_ref[...]] * (FN_W // LANES), axis=1)
    sw = jnp.concatenate([sw_ref[...]] * (FN_W // LANES), axis=1)
    o_ref[0] = (ar * cw + ai * sw).reshape(FFT_N2, FFT1_S, FN_W).astype(BF16)
    o_ref[1] = (ai * cw - ar * sw).reshape(FFT_N2, FFT1_S, FN_W).astype(BF16)


def _fft1(x4, ln_g, ln_b, w_fn, b_fn, fbig, cwt, swt):
    s = FFT1_S
    const = lambda b, j: (0, 0)
    return pl.pallas_call(
        _fft1_kernel,
        grid=(BATCH, FFT_N1 // s),
        in_specs=[
            pl.BlockSpec((None, FFT_N2, s, D), lambda b, j: (b, 0, j, 0)),
            pl.BlockSpec((1, D), const),
            pl.BlockSpec((1, D), const),
            pl.BlockSpec((D, FN_W), const),
            pl.BlockSpec((1, FN_W), const),
            pl.BlockSpec((2 * FFT1_ROWS, FFT1_ROWS), const),
            pl.BlockSpec((None, FFT1_ROWS, LANES), lambda b, j: (j, 0, 0)),
            pl.BlockSpec((None, FFT1_ROWS, LANES), lambda b, j: (j, 0, 0)),
        ],
        out_specs=pl.BlockSpec((None, 2, FFT_N2, s, FN_W), lambda b, j: (b, 0, 0, j, 0)),
        out_shape=jax.ShapeDtypeStruct((BATCH, 2, FFT_N2, FFT_N1, FN_W), BF16),
        compiler_params=pltpu.CompilerParams(
            dimension_semantics=("arbitrary", "arbitrary"),
            vmem_limit_bytes=40 * MIB),
        name="fft_stage1",
    )(x4, ln_g, ln_b, w_fn, b_fn, fbig, cwt, swt)


def _fft2_kernel(d_ref, f2_ref, o_ref):
    f2 = f2_ref[...]
    for kk in range(FFT2_KB):
        z = _dot(f2, jnp.concatenate([d_ref[0, kk], d_ref[1, kk]], axis=0))
        o_ref[0, kk] = z[:FFT_N1].astype(BF16)
        o_ref[1, kk] = z[FFT_N1:].astype(BF16)


def _fft2(dmat, f2):
    kb = FFT2_KB
    blk = (None, 2, kb, FFT_N1, FN_W)
    return pl.pallas_call(
        _fft2_kernel,
        grid=(BATCH, FFT_N2 // kb),
        in_specs=[
            pl.BlockSpec(blk, lambda b, j: (b, 0, j, 0, 0)),
            pl.BlockSpec((2 * FFT_N1, 2 * FFT_N1), lambda b, j: (0, 0)),
        ],
        out_specs=pl.BlockSpec(blk, lambda b, j: (b, 0, j, 0, 0)),
        out_shape=jax.ShapeDtypeStruct((BATCH, 2, FFT_N2, FFT_N1, FN_W), BF16),
        compiler_params=pltpu.CompilerParams(
            dimension_semantics=("arbitrary", "arbitrary")),
        name="fft_stage2",
    )(dmat, f2)


def _dft_tables(merge_tm):
    s = FFT1_S
    n2 = np.arange(FFT_N2, dtype=np.float64)
    n1 = np.arange(FFT_N1, dtype=np.float64)
    th = 2.0 * np.pi * np.outer(n2, n2) / FFT_N2
    f1 = np.stack([np.cos(th), -np.sin(th)]) / math.sqrt(SEQ)
    fbig = np.einsum("rkn,st->rksnt", f1, np.eye(s)).reshape(2 * FFT_N2 * s, FFT_N2 * s)
    tw = 2.0 * np.pi * np.outer(n2, n1) / SEQ
    tw = tw.reshape(FFT_N2, FFT_N1 // s, s).transpose(1, 0, 2).reshape(FFT_N1 // s, FFT_N2 * s)
    cwt = np.broadcast_to(np.cos(tw)[:, :, None], tw.shape + (LANES,))
    swt = np.broadcast_to(np.sin(tw)[:, :, None], tw.shape + (LANES,))
    th1 = 2.0 * np.pi * np.outer(n1, n1) / FFT_N1
    c1, s1 = np.cos(th1), np.sin(th1)
    f2 = np.block([[c1, s1], [-s1, c1]])
    cc = np.arange(FN_GW, dtype=np.float64)
    thc = 2.0 * np.pi * np.outer(cc, cc) / FN_GW
    ccs = np.concatenate([np.cos(thc), np.sin(thc)], axis=0) / math.sqrt(FN_GW)
    k1n = merge_tm // FFT_N2
    r = np.arange(merge_tm)
    perm = np.zeros((merge_tm, merge_tm))
    perm[r, (r % FFT_N2) * k1n + r // FFT_N2] = 1.0
    as32 = lambda a: jnp.asarray(np.ascontiguousarray(a), dtype=F32)
    return (as32(fbig).astype(BF16), as32(cwt), as32(swt), as32(f2).astype(BF16),
            as32(ccs).astype(BF16), as32(perm).astype(BF16))


def _memkv_kernel(m_ref, g_ref, b_ref, w_ref, o_ref):
    mn = _ln(m_ref[...], g_ref[...], b_ref[...]).astype(BF16)
    o_ref[...] = _dot(mn, w_ref[...]).astype(BF16)


def _memkv(mem2, g, b, w):
    return pl.pallas_call(
        _memkv_kernel,
        grid=(BATCH,),
        in_specs=[
            pl.BlockSpec((MEM_LEN, D), lambda i: (i, 0)),
            pl.BlockSpec((1, D), lambda i: (0, 0)),
            pl.BlockSpec((1, D), lambda i: (0, 0)),
            pl.BlockSpec((D, 2 * MQ_W), lambda i: (0, 0)),
        ],
        out_specs=pl.BlockSpec((MEM_LEN, 2 * MQ_W), lambda i: (i, 0)),
        out_shape=jax.ShapeDtypeStruct((BATCH * MEM_LEN, 2 * MQ_W), BF16),
        compiler_params=pltpu.CompilerParams(dimension_semantics=("arbitrary",)),
        name="mem_kv",
    )(mem2, g, b, w)


MERGE_TM = 512
MERGE_K1 = MERGE_TM // FFT_N2


def _pack_bf16_pair(v):
    n = v.shape[1] // 2
    bits = lax.bitcast_convert_type(v.astype(BF16).astype(F32), U32)
    return (bits[:, n:] & jnp.uint32(0xFFFF0000)) | (bits[:, :n] >> 16)


def _unpack_bf16_pair(p):
    lo = lax.bitcast_convert_type(p << 16, F32)
    hi = lax.bitcast_convert_type(p & jnp.uint32(0xFFFF0000), F32)
    return lo, hi


def _merge_kernel(x_ref, og_ref, zr_ref, zi_ref, mq_ref, gt_ref, kv_ref,
                  lng_ref, lnb_ref, wg_ref, ccs_ref, perm_ref, wf_ref, wm_ref, wo_ref, bo_ref,
                  l1g_ref, l1b_ref, wr2_ref, wrh_ref, br_ref,
                  h1_ref, h1p_ref, eidx_ref, topw_ref):
    tm = MERGE_TM
    y_gla = _dot(og_ref[...], wg_ref[...])

    zr = zr_ref[...].reshape(tm, FN_W)
    zi = zi_ref[...].reshape(tm, FN_W)
    ys = []
    for g in range(FN_G):
        sl = slice(g * FN_GW, (g + 1) * FN_GW)
        ys.append(_dot(jnp.concatenate([zr[:, sl], zi[:, sl]], axis=1), ccs_ref[...]))
    yp = jnp.concatenate(ys, axis=1).astype(BF16)
    y_fn = _dot(_dot(perm_ref[...], yp).astype(BF16), wf_ref[...])

    oms = []
    for hd in range(MEM_H):
        sl = slice(hd * MEM_HD, (hd + 1) * MEM_HD)
        s = _dot_nt(mq_ref[:, sl], kv_ref[:, sl]) * (MEM_HD ** -0.5)
        s = s - jnp.max(s, axis=-1, keepdims=True)
        p = jnp.exp(s)
        p = p / jnp.sum(p, axis=-1, keepdims=True)
        oms.append(_dot(p.astype(BF16), kv_ref[:, MQ_W + hd * MEM_HD:MQ_W + (hd + 1) * MEM_HD]))
    y_mem = _dot(jnp.concatenate(oms, axis=1).astype(BF16), wm_ref[...])

    merged = (jax.nn.sigmoid(gt_ref[:, 0:D].astype(F32)) * y_gla
              + jax.nn.sigmoid(gt_ref[:, D:2 * D].astype(F32)) * y_fn
              + jax.nn.sigmoid(gt_ref[:, 2 * D:3 * D].astype(F32)) * y_mem)
    mix = _dot(merged.astype(BF16), wo_ref[...]) + bo_ref[...]
    h = _ln(x_ref[...], lng_ref[...], lnb_ref[...])
    h1 = _ln(DN_ALPHA * h + mix, l1g_ref[...], l1b_ref[...])
    h1_ref[...] = h1
    h1p_ref[...] = _pack_bf16_pair(h1)

    h_hi, h_lo = _split_bf16(h1)
    d2 = _dot(h_hi, wr2_ref[...])
    logits = d2[:, :LANES] + d2[:, LANES:] + _dot(h_lo, wrh_ref[...]) + br_ref[...]
    lane = lax.broadcasted_iota(I32, (tm, LANES), 1)
    l = logits
    vals, idxs = [], []
    for _ in range(TOP_K):
        m = jnp.max(l, axis=-1, keepdims=True)
        idx = jnp.min(jnp.where(l == m, lane, LANES), axis=-1, keepdims=True)
        vals.append(m)
        idxs.append(idx)
        l = jnp.where(lane == idx, -jnp.inf, l)
    es = [jnp.exp(v - vals[0]) for v in vals]
    den = es[0] + es[1] + es[2] + es[3]
    eo = jnp.zeros((tm, LANES), I32)
    wo = jnp.zeros((tm, LANES), F32)
    for k in range(TOP_K):
        eo = jnp.where(lane == k, idxs[k], eo)
        wo = jnp.where(lane == k, es[k] / den, wo)
    eidx_ref[...] = eo
    topw_ref[...] = wo


def _merge(x2, og, z, mq, gates, kv, lng, lnb, wg, ccs, perm, wf, wm, wo, bo, l1g, l1b,
           wr2, wrh, br):
    tm = MERGE_TM
    per_b = SEQ // tm
    row = lambda i: (i, 0)
    const = lambda i: (0, 0)
    zblk = (None, None, FFT_N2, MERGE_K1, FN_W)
    outs = (
        jax.ShapeDtypeStruct((T, D), F32),
        jax.ShapeDtypeStruct((T, D // 2), U32),
        jax.ShapeDtypeStruct((T, LANES), I32),
        jax.ShapeDtypeStruct((T, LANES), F32),
    )
    return pl.pallas_call(
        _merge_kernel,
        grid=(T // tm,),
        in_specs=[
            pl.BlockSpec((tm, D), row),
            pl.BlockSpec((tm, D), row),
            pl.BlockSpec(zblk, lambda i: (i // per_b, 0, 0, i % per_b, 0)),
            pl.BlockSpec(zblk, lambda i: (i // per_b, 1, 0, i % per_b, 0)),
            pl.BlockSpec((tm, MQ_W), row),
            pl.BlockSpec((tm, 3 * D), row),
            pl.BlockSpec((MEM_LEN, 2 * MQ_W), lambda i: (i // per_b, 0)),
            pl.BlockSpec((1, D), const), pl.BlockSpec((1, D), const),
            pl.BlockSpec((D, D), const),
            pl.BlockSpec((2 * FN_GW, FN_GW), const),
            pl.BlockSpec((tm, tm), const),
            pl.BlockSpec((FN_W, D), const),
            pl.BlockSpec((MQ_W, D), const),
            pl.BlockSpec((D, D), const),
            pl.BlockSpec((1, D), const),
            pl.BlockSpec((1, D), const), pl.BlockSpec((1, D), const),
            pl.BlockSpec((D, 2 * LANES), const),
            pl.BlockSpec((D, LANES), const),
            pl.BlockSpec((1, LANES), const),
        ],
        out_specs=[
            pl.BlockSpec((tm, D), row),
            pl.BlockSpec((tm, D // 2), row),
            pl.BlockSpec((tm, LANES), row),
            pl.BlockSpec((tm, LANES), row),
        ],
        out_shape=outs,
        compiler_params=pltpu.CompilerParams(
            dimension_semantics=("arbitrary",),
            vmem_limit_bytes=58 * MIB),
        name="merge_ln1_router",
    )(x2, og, z, z, mq, gates, kv, lng, lnb, wg, ccs, perm, wf, wm, wo, bo, l1g, l1b,
      wr2, wrh, br)


PLAN_TP = 512


def _plan_kernel(e_ref, dest_ref, cnt_out_ref, cnt_ref, off_ref):
    p = pl.program_id(0)
    i = pl.program_id(1)
    tp = PLAN_TP
    lane = lax.broadcasted_iota(I32, (tp, LANES), 1)
    e = e_ref[...]
    onehots = [lane == e[:, k:k + 1] for k in range(TOP_K)]
    mf = jnp.zeros((tp, LANES), F32)
    for oh in onehots:
        mf = mf + jnp.where(oh, 1.0, 0.0)
    colsum = jnp.sum(mf, axis=0, keepdims=True)

    @pl.when(jnp.logical_and(p == 0, i == 0))
    def _():
        cnt_ref[...] = jnp.zeros_like(cnt_ref)

    @pl.when(jnp.logical_and(p == 1, i == 0))
    def _():
        tot = cnt_ref[...]
        cnt_out_ref[...] = jnp.broadcast_to(tot, cnt_out_ref.shape)
        lane1 = lax.broadcasted_iota(I32, (1, LANES), 1)
        inc = tot
        for s in (1, 2, 4, 8, 16, 32, 64):
            inc = inc + jnp.where(lane1 >= s, pltpu.roll(inc, s, 1), 0.0)
        off_ref[...] = inc - tot
        cnt_ref[...] = jnp.zeros_like(cnt_ref)

    @pl.when(p == 0)
    def _():
        dest_ref[...] = jnp.zeros_like(dest_ref)

    @pl.when(p == 1)
    def _():
        ri = lax.broadcasted_iota(I32, (tp, tp), 0)
        ci = lax.broadcasted_iota(I32, (tp, tp), 1)
        ltri = jnp.where(ri > ci, 1.0, 0.0).astype(BF16)
        rank = _dot(ltri, mf.astype(BF16)) + cnt_ref[...] + off_ref[...]
        out = jnp.zeros((tp, LANES), I32)
        for k in range(TOP_K):
            dk = jnp.sum(jnp.where(onehots[k], rank, 0.0), axis=-1, keepdims=True)
            out = jnp.where(lane == k, dk.astype(I32), out)
        dest_ref[...] = out

    cnt_ref[...] += colsum


def _plan(eidx):
    tp = PLAN_TP
    return pl.pallas_call(
        _plan_kernel,
        grid=(2, T // tp),
        in_specs=[pl.BlockSpec((tp, LANES), lambda p, i: (i, 0))],
        out_specs=[
            pl.BlockSpec((tp, LANES), lambda p, i: (i * p, 0)),
            pl.BlockSpec((8, LANES), lambda p, i: (0, 0)),
        ],
        out_shape=(jax.ShapeDtypeStruct((T, LANES), I32),
                   jax.ShapeDtypeStruct((8, LANES), F32)),
        scratch_shapes=[pltpu.VMEM((1, LANES), F32), pltpu.VMEM((1, LANES), F32)],
        compiler_params=pltpu.CompilerParams(
            dimension_semantics=("arbitrary", "arbitrary")),
        name="route_plan",
    )(eidx)


DISP_TM = 256
DISP_N = DISP_TM * TOP_K
ROWS_PER_ISSUE = 8


def _dispatch_kernel(dest_hbm, h_ref, xs_hbm, idx_smem, isem, sem):
    i = pl.program_id(0)
    slot = lax.rem(i, 2)

    def idx_copy(t, s):
        return pltpu.make_async_copy(dest_hbm.at[pl.ds(t, 1)], idx_smem.at[pl.ds(s, 1)],
                                     isem.at[s])

    @pl.when(i == 0)
    def _():
        idx_copy(0, 0).start()

    idx_copy(i, slot).wait()

    @pl.when(i + 1 < pl.num_programs(0))
    def _():
        idx_copy(i + 1, 1 - slot).start()

    def row_copies(ro):
        r0 = pl.multiple_of(ro * ROWS_PER_ISSUE, ROWS_PER_ISSUE)
        a0 = ro * (ROWS_PER_ISSUE * TOP_K)
        return [pltpu.make_async_copy(h_ref.at[pl.ds(r0 + ri, 1)],
                                      xs_hbm.at[pl.ds(idx_smem[slot, a0 + ri * TOP_K + k], 1)],
                                      sem)
                for ri in range(ROWS_PER_ISSUE) for k in range(TOP_K)]

    def issue(ro, c):
        for n, cp_ in enumerate(row_copies(ro)):
            cp_.start(priority=n % 2)
        return c

    lax.fori_loop(0, DISP_TM // ROWS_PER_ISSUE, issue, 0)

    def drain(ro, c):
        for cp_ in row_copies(ro):
            cp_.wait()
        return c

    lax.fori_loop(0, DISP_TM // ROWS_PER_ISSUE, drain, 0)


def _dispatch(dest2, h1p):
    return pl.pallas_call(
        _dispatch_kernel,
        grid=(T // DISP_TM,),
        in_specs=[
            pl.BlockSpec(memory_space=pl.ANY),
            pl.BlockSpec((DISP_TM, D // 2), lambda i: (i, 0)),
        ],
        out_specs=pl.BlockSpec(memory_space=pl.ANY),
        out_shape=jax.ShapeDtypeStruct((A_ROWS, D // 2), U32),
        scratch_shapes=[
            pltpu.SMEM((2, DISP_N), I32),
            pltpu.SemaphoreType.DMA((2,)),
            pltpu.SemaphoreType.DMA,
        ],
        compiler_params=pltpu.CompilerParams(dimension_semantics=("arbitrary",)),
        name="moe_dispatch",
    )(dest2, h1p)


MOE_BM = 512
MOE_NBLK = A_ROWS // MOE_BM
MOE_NW = MOE_NBLK + N_EXP


def _expert_kernel(we_ref, wb_ref, wlo_ref, whi_ref, wfe_ref, wfb_ref, wsl_ref, wnx_ref,
                   x_ref, wgu_hbm, bgu_ref, wdn_hbm, bdn_ref, o_ref,
                   wgu_f32, wdn_f32, wgu_bf, wdn_bf, sems):
    w = pl.program_id(0)
    e = we_ref[w]
    lo = wlo_ref[w]
    hi = whi_ref[w]

    def weight_copies(expert, slot):
        return (pltpu.make_async_copy(wgu_hbm.at[expert], wgu_f32.at[slot], sems.at[slot, 0]),
                pltpu.make_async_copy(wdn_hbm.at[expert], wdn_f32.at[slot], sems.at[slot, 1]))

    @pl.when(w == 0)
    def _():
        for cp_ in weight_copies(e, 0):
            cp_.start()

    @pl.when(wfe_ref[w] == 1)
    def _():
        slot = wsl_ref[w]
        for cp_ in weight_copies(e, slot):
            cp_.wait()
        wgu_bf[...] = wgu_f32[slot].astype(BF16)
        wdn_bf[...] = wdn_f32[slot].astype(BF16)
        nxt = wnx_ref[w]

        @pl.when(nxt >= 0)
        def _():
            for cp_ in weight_copies(nxt, 1 - slot):
                cp_.start()

    @pl.when(wfb_ref[w] == 1)
    def _():
        o_ref[...] = jnp.zeros_like(o_ref)

    @pl.when(hi > lo)
    def _():
        xlo, xhi = _unpack_bf16_pair(x_ref[...])
        half = D // 2
        gu = (_dot(xlo.astype(BF16), wgu_bf[:half, :]) + _dot(xhi.astype(BF16), wgu_bf[half:, :])
              + bgu_ref[pl.ds(e, 1), :])
        gate = jnp.minimum(gu[:, :D_FF], SW_LIMIT)
        up = jnp.clip(gu[:, D_FF:], -SW_LIMIT, SW_LIMIT)
        act = (up + 1.0) * (gate * jax.nn.sigmoid(SW_ALPHA * gate))
        out = _dot(act.astype(BF16), wdn_bf[...]) + bdn_ref[pl.ds(e, 1), :]
        rid = lax.broadcasted_iota(I32, (MOE_BM, 1), 0)
        keep = jnp.logical_and(rid >= lo, rid < hi)
        o_ref[...] = jnp.where(keep, _pack_bf16_pair(out), o_ref[...])


def _experts(meta, xs, w_gu, b_gu, w_down, b_down):
    return pl.pallas_call(
        _expert_kernel,
        grid_spec=pltpu.PrefetchScalarGridSpec(
            num_scalar_prefetch=len(meta),
            grid=(MOE_NW,),
            in_specs=[
                pl.BlockSpec((MOE_BM, D // 2), lambda w, we, wb, *_: (wb[w], 0)),
                pl.BlockSpec(memory_space=pl.ANY),
                pl.BlockSpec((N_EXP, 2 * D_FF), lambda w, *_: (0, 0)),
                pl.BlockSpec(memory_space=pl.ANY),
                pl.BlockSpec((N_EXP, D), lambda w, *_: (0, 0)),
            ],
            out_specs=pl.BlockSpec((MOE_BM, D // 2), lambda w, we, wb, *_: (wb[w], 0)),
            scratch_shapes=[
                pltpu.VMEM((2, D, 2 * D_FF), F32),
                pltpu.VMEM((2, D_FF, D), F32),
                pltpu.VMEM((D, 2 * D_FF), BF16),
                pltpu.VMEM((D_FF, D), BF16),
                pltpu.SemaphoreType.DMA((2, 2)),
            ],
        ),
        out_shape=jax.ShapeDtypeStruct((A_ROWS, D // 2), U32),
        compiler_params=pltpu.CompilerParams(
            dimension_semantics=("arbitrary",),
            vmem_limit_bytes=56 * MIB),
        name="moe_experts",
    )(*meta, xs, w_gu, b_gu, w_down, b_down)


def _work_items(counts):
    bm = MOE_BM
    end = jnp.cumsum(counts)
    start = end - counts
    first = start // bm
    last = jnp.maximum(end - 1, 0) // bm
    n_e = jnp.where(counts > 0, last - first + 1, 0)
    item_end = jnp.cumsum(n_e)
    item_start = item_end - n_e
    total = item_end[-1]
    w = jnp.arange(MOE_NW, dtype=I32)
    wc = jnp.minimum(w, total - 1)
    e_w = jnp.sum((item_end[None, :] <= wc[:, None]).astype(I32), axis=1)
    e_w = jnp.minimum(e_w, N_EXP - 1)
    blk = first[e_w] + (wc - item_start[e_w])
    lo = jnp.maximum(start[e_w], blk * bm) - blk * bm
    hi = jnp.minimum(end[e_w], (blk + 1) * bm) - blk * bm
    valid = w < total
    lo = jnp.where(valid, lo, 0)
    hi = jnp.where(valid, hi, 0)
    prev_e = jnp.concatenate([jnp.full((1,), -1, I32), e_w[:-1]])
    prev_b = jnp.concatenate([jnp.full((1,), -1, I32), blk[:-1]])
    fe = (e_w != prev_e).astype(I32)
    fb = (blk != prev_b).astype(I32)
    slot = (jnp.cumsum(fe) - 1) % 2
    first_at = jnp.where(fe == 1, w, MOE_NW)
    next_first = jnp.concatenate([lax.cummin(first_at, reverse=True)[1:],
                                  jnp.full((1,), MOE_NW, I32)])
    nxt = jnp.where(next_first < MOE_NW, e_w[jnp.minimum(next_first, MOE_NW - 1)], -1)
    return tuple(a.astype(I32) for a in (e_w, blk, lo, hi, fe, fb, slot, nxt))


COMB_TM = 256
COMB_N = COMB_TM * TOP_K


def _combine_kernel(dest_hbm, ys_hbm, h1_ref, tw_ref, g_ref, b_ref, o_ref,
                    idx_smem, gbuf, isem, sem):
    i = pl.program_id(0)
    n_tiles = pl.num_programs(0)

    def idx_copy(t):
        s = lax.rem(t, 3)
        return pltpu.make_async_copy(dest_hbm.at[pl.ds(t, 1)], idx_smem.at[pl.ds(s, 1)],
                                     isem.at[s])

    def row_copies(t, ro):
        si = lax.rem(t, 3)
        sb = lax.rem(t, 2)
        r0 = pl.multiple_of(ro * ROWS_PER_ISSUE, ROWS_PER_ISSUE)
        a0 = ro * (ROWS_PER_ISSUE * TOP_K)
        return [pltpu.make_async_copy(ys_hbm.at[pl.ds(idx_smem[si, a0 + ri * TOP_K + k], 1)],
                                      gbuf.at[sb, k, pl.ds(r0 + ri, 1)], sem.at[sb])
                for ri in range(ROWS_PER_ISSUE) for k in range(TOP_K)]

    def issue_tile(t):
        def issue(ro, c):
            for n, cp_ in enumerate(row_copies(t, ro)):
                cp_.start(priority=n % 2)
            return c
        lax.fori_loop(0, COMB_TM // ROWS_PER_ISSUE, issue, 0)

    @pl.when(i == 0)
    def _():
        idx_copy(0).start()
        idx_copy(0).wait()
        issue_tile(0)

        @pl.when(n_tiles > 1)
        def _():
            idx_copy(1).start()

    @pl.when(i + 2 < n_tiles)
    def _():
        idx_copy(i + 2).start()

    @pl.when(i + 1 < n_tiles)
    def _():
        idx_copy(i + 1).wait()
        issue_tile(i + 1)

    def drain(ro, c):
        for cp_ in row_copies(i, ro):
            cp_.wait()
        return c

    lax.fori_loop(0, COMB_TM // ROWS_PER_ISSUE, drain, 0)

    gcur = gbuf.at[lax.rem(i, 2)]
    tw = tw_ref[...]
    ylo = jnp.zeros((COMB_TM, D // 2), F32)
    yhi = jnp.zeros((COMB_TM, D // 2), F32)
    for k in range(TOP_K):
        lo, hi = _unpack_bf16_pair(gcur[k])
        wk = tw[:, k:k + 1]
        ylo = ylo + lo * wk
        yhi = yhi + hi * wk
    ff = jnp.concatenate([ylo, yhi], axis=1)
    o_ref[...] = _ln(DN_ALPHA * h1_ref[...] + ff, g_ref[...], b_ref[...])


def _combine(dest2, ys, h1, topw, g, b):
    tm = COMB_TM
    return pl.pallas_call(
        _combine_kernel,
        grid=(T // tm,),
        in_specs=[
            pl.BlockSpec(memory_space=pl.ANY),
            pl.BlockSpec(memory_space=pl.ANY),
            pl.BlockSpec((tm, D), lambda i: (i, 0)),
            pl.BlockSpec((tm, LANES), lambda i: (i, 0)),
            pl.BlockSpec((1, D), lambda i: (0, 0)),
            pl.BlockSpec((1, D), lambda i: (0, 0)),
        ],
        out_specs=pl.BlockSpec((tm, D), lambda i: (i, 0)),
        out_shape=jax.ShapeDtypeStruct((T, D), F32),
        scratch_shapes=[
            pltpu.SMEM((3, COMB_N), I32),
            pltpu.VMEM((2, TOP_K, tm, D // 2), U32),
            pltpu.SemaphoreType.DMA((3,)),
            pltpu.SemaphoreType.DMA((2,)),
        ],
        compiler_params=pltpu.CompilerParams(dimension_semantics=("arbitrary",)),
        name="moe_combine_ln2",
    )(dest2, ys, h1, topw, g, b)


SC_CORES = 2
SC_SUBCORES = 16
SC_WORKERS = SC_CORES * SC_SUBCORES
SC_CH = 64
SC_ROWS_PER_W = A_ROWS // SC_WORKERS
SC_NCH = SC_ROWS_PER_W // SC_CH


def _sc_gather(table, idx3):
    mesh = plsc.VectorSubcoreMesh(core_axis_name="c", subcore_axis_name="s")

    @functools.partial(
        pl.kernel, mesh=mesh,
        out_type=jax.ShapeDtypeStruct((A_ROWS, D // 2), U32),
        scratch_types=[
            pltpu.VMEM((SC_NCH, SC_CH), I32),
            pltpu.VMEM((SC_CH, D // 2), U32),
            pltpu.SemaphoreType.DMA,
        ],
    )
    def k(table_hbm, idx_hbm, out_hbm, idx_v, rows_v, sem):
        wid = lax.axis_index("s") * SC_CORES + lax.axis_index("c")
        base = wid * SC_ROWS_PER_W
        pltpu.sync_copy(idx_hbm.at[wid], idx_v)

        @pl.loop(0, SC_NCH)
        def _(j):
            pltpu.async_copy(table_hbm.at[idx_v.at[j]], rows_v, sem).wait()
            pltpu.sync_copy(rows_v, out_hbm.at[pl.ds(base + j * SC_CH, SC_CH)])

    return k(table, idx3)


def _combine_dense_kernel(g_ref, h1_ref, tw_ref, lg_ref, lb_ref, o_ref):
    tw = tw_ref[...]
    ylo = jnp.zeros((COMB_TM, D // 2), F32)
    yhi = jnp.zeros((COMB_TM, D // 2), F32)
    for k in range(TOP_K):
        lo, hi = _unpack_bf16_pair(g_ref[k])
        wk = tw[:, k:k + 1]
        ylo = ylo + lo * wk
        yhi = yhi + hi * wk
    ff = jnp.concatenate([ylo, yhi], axis=1)
    o_ref[...] = _ln(DN_ALPHA * h1_ref[...] + ff, lg_ref[...], lb_ref[...])


def _combine_dense(g4, h1, topw, g, b):
    tm = COMB_TM
    return pl.pallas_call(
        _combine_dense_kernel,
        grid=(T // tm,),
        in_specs=[
            pl.BlockSpec((TOP_K, tm, D // 2), lambda i: (0, i, 0)),
            pl.BlockSpec((tm, D), lambda i: (i, 0)),
            pl.BlockSpec((tm, LANES), lambda i: (i, 0)),
            pl.BlockSpec((1, D), lambda i: (0, 0)),
            pl.BlockSpec((1, D), lambda i: (0, 0)),
        ],
        out_specs=pl.BlockSpec((tm, D), lambda i: (i, 0)),
        out_shape=jax.ShapeDtypeStruct((T, D), F32),
        compiler_params=pltpu.CompilerParams(dimension_semantics=("arbitrary",)),
        name="moe_combine_dense_ln2",
    )(g4, h1, topw, g, b)


def _pad_cols(a, n):
    return jnp.pad(a, ((0, 0), (0, n - a.shape[1])))


def kernel(x, mem, ln_in_g, ln_in_b, ln_mem_g, ln_mem_b, w_in, b_in, w_decay_f, b_decay_f,
           w_decay_b, b_decay_b, gla_norm_g, w_br_gla, w_br_fnet, w_br_mem, w_mem_kv, w_out,
           b_out, ln1_g, ln1_b, w_router, b_router, w_gu, b_gu, w_down, b_down, ln2_g, ln2_b):
    assert x.shape == (BATCH, SEQ, D) and w_in.shape[0] == 1
    row = lambda a: a.reshape(1, -1)
    x2 = x.reshape(T, D)
    w_in0, b_in0 = w_in[0], b_in[0]
    c_lr, c_fn, c_mq, c_gt = 3072, 3072 + 2 * GLA_LR, 3104 + FN_W, 3104 + FN_W + MQ_W
    w_main = jnp.concatenate([w_in0[:, :c_lr], w_in0[:, c_gt:]], axis=1).astype(BF16)
    b_main = row(jnp.concatenate([b_in0[:c_lr], b_in0[c_gt:]]))
    w_lr = _pad_cols(w_in0[:, c_lr:c_fn], LANES).astype(BF16)
    b_lr = _pad_cols(row(b_in0[c_lr:c_fn]), LANES)
    w_mq = w_in0[:, c_mq:c_gt].astype(BF16)
    b_mq = row(b_in0[c_mq:c_gt])
    w_fn = w_in0[:, c_fn:c_mq].astype(BF16)
    b_fn = row(b_in0[c_fn:c_mq])
    lng, lnb = row(ln_in_g), row(ln_in_b)

    qk, v, r, gates, mq, lr = _inproj(x2, lng, lnb, w_main, b_main, w_mq, b_mq, w_lr, b_lr)

    zpad = jnp.zeros((LANES - 2 * GLA_LR, GLA_H * GLA_DK), F32)
    zlr = jnp.zeros((GLA_LR, GLA_H * GLA_DK), F32)
    wdf = jnp.concatenate([w_decay_f[0], zlr, zpad], axis=0).reshape(LANES, GLA_H, GLA_DK)
    wdb = jnp.concatenate([zlr, w_decay_b[0], zpad], axis=0).reshape(LANES, GLA_H, GLA_DK)
    wd = jnp.concatenate([wdf, wdb], axis=2).reshape(LANES, GLA_H * 2 * GLA_DK)
    bd = jnp.concatenate([b_decay_f[0].reshape(GLA_H, GLA_DK),
                          b_decay_b[0].reshape(GLA_H, GLA_DK)], axis=1).reshape(1, -1)
    og = _gla(qk, v, r, lr, wd, bd, row(gla_norm_g[0]))

    fbig, cwt, swt, f2, ccs, perm = _dft_tables(MERGE_TM)
    x4 = x.reshape(BATCH, FFT_N2, FFT_N1, D)
    z = _fft2(_fft1(x4, lng, lnb, w_fn, b_fn, fbig, cwt, swt), f2)

    kv = _memkv(mem.reshape(BATCH * MEM_LEN, D), row(ln_mem_g), row(ln_mem_b),
                w_mem_kv[0].astype(BF16))

    w_r = _pad_cols(w_router[0], LANES)
    wr_hi = w_r.astype(BF16)
    wr_lo = (w_r - wr_hi.astype(F32)).astype(BF16)
    b_r = jnp.concatenate([row(b_router[0]),
                           jnp.full((1, LANES - N_EXP), NEG_BIG, F32)], axis=1)
    h1, h1p, eidx, topw = _merge(
        x2, og, z, mq, gates, kv, lng, lnb,
        w_br_gla[0].astype(BF16), ccs, perm, w_br_fnet[0].astype(BF16),
        w_br_mem[0].astype(BF16), w_out[0].astype(BF16), row(b_out[0]),
        row(ln1_g[0]), row(ln1_b[0]), jnp.concatenate([wr_hi, wr_lo], axis=1), wr_hi, b_r)

    dest, cnt = _plan(eidx)
    counts = cnt[0, :N_EXP].astype(I32)
    dest2 = dest[:, :TOP_K].reshape(T // DISP_TM, DISP_N)
    xs = _dispatch(dest2, h1p)
    ys = _experts(_work_items(counts), xs, w_gu[0], b_gu[0], w_down[0], b_down[0])
    dest_t = dest[:, :TOP_K].T.reshape(SC_WORKERS, SC_NCH, SC_CH)
    g4 = _sc_gather(ys, dest_t).reshape(TOP_K, T, D // 2)
    out = _combine_dense(g4, h1, topw, row(ln2_g[0]), row(ln2_b[0]))
    return out.reshape(BATCH, SEQ, D)
```

```python
import functools
import math

import numpy as np
import jax
import jax.numpy as jnp
from jax import lax
from jax.experimental import pallas as pl
from jax.experimental.pallas import tpu as pltpu
from jax.experimental.pallas import tpu_sc as plsc

F32 = jnp.float32
BF16 = jnp.bfloat16
I32 = jnp.int32
U32 = jnp.uint32

D = 1024
BATCH = 4
SEQ = 4096
T = BATCH * SEQ
GLA_H = 4
GLA_DK = 128
GLA_DV = 256
GLA_LR = 16
GLA_TAU = 16.0
GLA_C = 64
FN_G = 4
FN_GW = 128
FN_W = 512
MEM_LEN = 256
MEM_H = 4
MEM_HD = 128
MQ_W = 512
N_EXP = 32
TOP_K = 4
D_FF = 1024
SW_LIMIT = 7.0
SW_ALPHA = 1.702
LN_EPS = 1e-5
RMS_EPS = 1e-6
DN_ALPHA = 2.0 ** 0.25
A_ROWS = T * TOP_K

FFT_N1 = 128
FFT_N2 = 32

LANES = 128
NEG_BIG = -1e30
MIB = 1024 * 1024


def _ln(x, g, b):
    mu = jnp.mean(x, axis=-1, keepdims=True)
    xc = x - mu
    var = jnp.mean(xc * xc, axis=-1, keepdims=True)
    return xc * lax.rsqrt(var + LN_EPS) * g + b


def _dot(a, b):
    return jnp.dot(a, b, preferred_element_type=F32)


def _dot_nt(a, b):
    return lax.dot_general(a, b, (((1,), (1,)), ((), ())), preferred_element_type=F32)


def _dot_tn(a, b):
    return lax.dot_general(a, b, (((0,), (0,)), ((), ())), preferred_element_type=F32)


def _split_bf16(a):
    hi = a.astype(BF16)
    return hi, (a - hi.astype(F32)).astype(BF16)


INPROJ_TM = 1024
INPROJ_TN = 1024


def _inproj_kernel(x_ref, g_ref, b_ref, w_ref, bias_ref, wmq_ref, bmq_ref, wlr_ref, blr_ref,
                   qk_ref, v_ref, r_ref, gates_ref, mq_ref, lr_ref, hb_ref):
    j = pl.program_id(1)

    @pl.when(j == 0)
    def _():
        hb = _ln(x_ref[...], g_ref[...], b_ref[...]).astype(BF16)
        hb_ref[...] = hb
        lr_ref[...] = _dot(hb, wlr_ref[...]) + blr_ref[...]
        mq_ref[...] = (_dot(hb, wmq_ref[...]) + bmq_ref[...]).astype(BF16)

    y = (_dot(hb_ref[...], w_ref[...]) + bias_ref[...]).astype(BF16)
    for idx, ref in enumerate((qk_ref, v_ref, r_ref)):
        @pl.when(j == idx)
        def _(ref=ref):
            ref[...] = y

    @pl.when(j >= 3)
    def _():
        gates_ref[...] = y


def _inproj(x2, ln_g, ln_b, w_main, b_main, w_mq, b_mq, w_lr, b_lr):
    tm, tn = INPROJ_TM, INPROJ_TN
    nj = w_main.shape[1] // tn
    row = lambda i, j: (i, 0)
    const = lambda i, j: (0, 0)
    outs = (
        jax.ShapeDtypeStruct((T, 1024), BF16),
        jax.ShapeDtypeStruct((T, 1024), BF16),
        jax.ShapeDtypeStruct((T, 1024), BF16),
        jax.ShapeDtypeStruct((T, 3072), BF16),
        jax.ShapeDtypeStruct((T, MQ_W), BF16),
        jax.ShapeDtypeStruct((T, LANES), F32),
    )
    return pl.pallas_call(
        _inproj_kernel,
        grid=(T // tm, nj),
        in_specs=[
            pl.BlockSpec((tm, D), row),
            pl.BlockSpec((1, D), const),
            pl.BlockSpec((1, D), const),
            pl.BlockSpec((D, tn), lambda i, j: (0, j)),
            pl.BlockSpec((1, tn), lambda i, j: (0, j)),
            pl.BlockSpec((D, MQ_W), const),
            pl.BlockSpec((1, MQ_W), const),
            pl.BlockSpec((D, LANES), const),
            pl.BlockSpec((1, LANES), const),
        ],
        out_specs=[
            pl.BlockSpec((tm, tn), row),
            pl.BlockSpec((tm, tn), row),
            pl.BlockSpec((tm, tn), row),
            pl.BlockSpec((tm, tn), lambda i, j: (i, jnp.maximum(j - 3, 0))),
            pl.BlockSpec((tm, MQ_W), row),
            pl.BlockSpec((tm, LANES), row),
        ],
        out_shape=outs,
        scratch_shapes=[pltpu.VMEM((tm, D), BF16)],
        compiler_params=pltpu.CompilerParams(
            dimension_semantics=("arbitrary", "arbitrary"),
            vmem_limit_bytes=48 * MIB),
        name="ln_inproj",
    )(x2, ln_g, ln_b, w_main, b_main, w_mq, b_mq, w_lr, b_lr)


GLA_BULK = 256
GLA_FIN = 512
GLA_NCH = SEQ // GLA_C
GLA_CPB = GLA_BULK // GLA_C


def _gla_kernel(q_ref, k_ref, v_ref, r_ref, lr_ref, wd_ref, bd_ref, g_ref, cs_ref, o_ref,
                acc_ref, qin_ref, kst_ref, dec_ref, stf_ref, stb_ref):
    C = GLA_C
    G = GLA_BULK
    DK = GLA_DK
    scale = DK ** -0.5
    ii = lax.broadcasted_iota(I32, (G, G), 0)
    jj = lax.broadcasted_iota(I32, (G, G), 1)
    same = (ii // C) == (jj // C)
    lower = jnp.logical_and(same, ii >= jj)
    upper = jnp.logical_and(same, ii <= jj)
    is_fwd = lax.broadcasted_iota(I32, (G, 2 * DK), 1) < DK

    def bulk(gi, carry):
        r0 = pl.multiple_of(gi * G, G)
        rows = pl.ds(r0, G)
        z = _dot(lr_ref[rows, :], wd_ref[...]) + bd_ref[...]
        la = -(jnp.maximum(-z, 0.0) + jnp.log(1.0 + jnp.exp(-jnp.abs(z)))) * (1.0 / GLA_TAU)
        la_hi, la_lo = _split_bf16(la)
        pre = _dot(cs_ref[...], la_hi) + _dot(cs_ref[...], la_lo)
        blast = jnp.concatenate(
            [jnp.broadcast_to(pre[ci * C + C - 1:ci * C + C, :], (C, 2 * DK))
             for ci in range(GLA_CPB)], axis=0)
        b = jnp.where(is_fwd, pre, blast - pre + la)
        qf32 = q_ref[rows, :].astype(F32)
        kf32 = k_ref[rows, :].astype(F32)
        q2 = jnp.concatenate([qf32, qf32], axis=1)
        k2 = jnp.concatenate([kf32, kf32], axis=1)
        qi = (q2 * (scale * jnp.exp(b))).astype(BF16)
        ki = (k2 * jnp.exp(-b)).astype(BF16)
        qin_ref[rows, :] = qi
        kst_ref[rows, :] = (k2 * jnp.exp(blast - b)).astype(BF16)
        dec = jnp.exp(blast)
        for ci in range(GLA_CPB):
            dec_ref[pl.ds(gi * GLA_CPB + ci, 1), :] = dec[ci * C:ci * C + 1, :]
        att = (jnp.where(lower, _dot_nt(qi[:, :DK], ki[:, :DK]), 0.0)
               + jnp.where(upper, _dot_nt(qi[:, DK:], ki[:, DK:]), 0.0))
        acc_ref[rows, :] = _dot(att.astype(BF16), v_ref[rows, :])
        return carry

    lax.fori_loop(0, SEQ // G, bulk, 0)

    stf_ref[...] = jnp.zeros_like(stf_ref)
    stb_ref[...] = jnp.zeros_like(stb_ref)

    def one(n, lanes, st_ref):
        rows = pl.ds(pl.multiple_of(n * C, C), C)
        st = st_ref[...]
        acc_ref[rows, :] += _dot_nt(qin_ref[rows, lanes], st.astype(BF16))
        st_ref[...] = (st * dec_ref[pl.ds(n, 1), :][:, lanes]
                       + _dot_tn(v_ref[rows, :], kst_ref[rows, lanes]))

    def step(i, carry):
        one(i, slice(0, DK), stf_ref)
        one(GLA_NCH - 1 - i, slice(DK, 2 * DK), stb_ref)
        return carry

    lax.fori_loop(0, GLA_NCH, step, 0, unroll=4)

    def fin(gi, carry):
        rows = pl.ds(pl.multiple_of(gi * GLA_FIN, GLA_FIN), GLA_FIN)
        o = acc_ref[rows, :]
        o = o * lax.rsqrt(jnp.mean(o * o, axis=-1, keepdims=True) + RMS_EPS) * g_ref[...]
        rg = r_ref[rows, :].astype(F32)
        o_ref[rows, :] = (o * (rg * jax.nn.sigmoid(rg))).astype(BF16)
        return carry

    lax.fori_loop(0, SEQ // GLA_FIN, fin, 0)


def _gla(qk, v, r, lr, wd, bd, g):
    i = np.arange(GLA_BULK)
    cs = ((i[:, None] // GLA_C) == (i[None, :] // GLA_C)) & (i[:, None] >= i[None, :])
    cs = jnp.asarray(cs, dtype=F32).astype(BF16)
    return pl.pallas_call(
        _gla_kernel,
        grid=(BATCH, GLA_H),
        in_specs=[
            pl.BlockSpec((SEQ, GLA_DK), lambda b, h: (b, h)),
            pl.BlockSpec((SEQ, GLA_DK), lambda b, h: (b, GLA_H + h)),
            pl.BlockSpec((SEQ, GLA_DV), lambda b, h: (b, h)),
            pl.BlockSpec((SEQ, GLA_DV), lambda b, h: (b, h)),
            pl.BlockSpec((SEQ, LANES), lambda b, h: (b, 0)),
            pl.BlockSpec((LANES, 2 * GLA_DK), lambda b, h: (0, h)),
            pl.BlockSpec((1, 2 * GLA_DK), lambda b, h: (0, h)),
            pl.BlockSpec((1, GLA_DV), lambda b, h: (0, 0)),
            pl.BlockSpec((GLA_BULK, GLA_BULK), lambda b, h: (0, 0)),
        ],
        out_specs=pl.BlockSpec((SEQ, GLA_DV), lambda b, h: (b, h)),
        out_shape=jax.ShapeDtypeStruct((T, GLA_H * GLA_DV), BF16),
        scratch_shapes=[
            pltpu.VMEM((SEQ, GLA_DV), F32),
            pltpu.VMEM((SEQ, 2 * GLA_DK), BF16),
            pltpu.VMEM((SEQ, 2 * GLA_DK), BF16),
            pltpu.VMEM((GLA_NCH, 2 * GLA_DK), F32),
            pltpu.VMEM((GLA_DV, GLA_DK), F32),
            pltpu.VMEM((GLA_DV, GLA_DK), F32),
        ],
        compiler_params=pltpu.CompilerParams(
            dimension_semantics=("arbitrary", "arbitrary"),
            vmem_limit_bytes=48 * MIB),
        name="gla",
    )(qk, qk, v, r, lr, wd, bd, g, cs)


FFT1_S = 16
FFT1_ROWS = FFT_N2 * FFT1_S
FFT2_KB = 4


def _fft1_kernel(x_ref, g_ref, b_ref, w_ref, bias_ref, fbig_ref, cw_ref, sw_ref, o_ref):
    xv = x_ref[...].reshape(FFT1_ROWS, D)
    hb = _ln(xv, g_ref[...], b_ref[...]).astype(BF16)
    fn = (_dot(hb, w_ref[...]) + bias_ref[...]).astype(BF16)
    a = _dot(fbig_ref[...], fn)
    ar = a[:FFT1_ROWS]
    ai = a[FFT1_ROWS:]
    cw = jnp.concatenate([cw_ref[...]] * (FN_W // LANES), axis=1)
    sw = jnp.concatenate([sw_ref[...]] * (FN_W // LANES), axis=1)
    o_ref[0] = (ar * cw + ai * sw).reshape(FFT_N2, FFT1_S, FN_W).astype(BF16)
    o_ref[1] = (ai * cw - ar * sw).reshape(FFT_N2, FFT1_S, FN_W).astype(BF16)


def _fft1(x4, ln_g, ln_b, w_fn, b_fn, fbig, cwt, swt):
    s = FFT1_S
    const = lambda b, j: (0, 0)
    return pl.pallas_call(
        _fft1_kernel,
        grid=(BATCH, FFT_N1 // s),
        in_specs=[
            pl.BlockSpec((None, FFT_N2, s, D), lambda b, j: (b, 0, j, 0)),
            pl.BlockSpec((1, D), const),
            pl.BlockSpec((1, D), const),
            pl.BlockSpec((D, FN_W), const),
            pl.BlockSpec((1, FN_W), const),
            pl.BlockSpec((2 * FFT1_ROWS, FFT1_ROWS), const),
            pl.BlockSpec((None, FFT1_ROWS, LANES), lambda b, j: (j, 0, 0)),
            pl.BlockSpec((None, FFT1_ROWS, LANES), lambda b, j: (j, 0, 0)),
        ],
        out_specs=pl.BlockSpec((None, 2, FFT_N2, s, FN_W), lambda b, j: (b, 0, 0, j, 0)),
        out_shape=jax.ShapeDtypeStruct((BATCH, 2, FFT_N2, FFT_N1, FN_W), BF16),
        compiler_params=pltpu.CompilerParams(
            dimension_semantics=("arbitrary", "arbitrary"),
            vmem_limit_bytes=40 * MIB),
        name="fft_stage1",
    )(x4, ln_g, ln_b, w_fn, b_fn, fbig, cwt, swt)


def _fft2_kernel(d_ref, f2_ref, o_ref):
    f2 = f2_ref[...]
    for kk in range(FFT2_KB):
        z = _dot(f2, jnp.concatenate([d_ref[0, kk], d_ref[1, kk]], axis=0))
        o_ref[0, kk] = z[:FFT_N1].astype(BF16)
        o_ref[1, kk] = z[FFT_N1:].astype(BF16)


def _fft2(dmat, f2):
    kb = FFT2_KB
    blk = (None, 2, kb, FFT_N1, FN_W)
    return pl.pallas_call(
        _fft2_kernel,
        grid=(BATCH, FFT_N2 // kb),
        in_specs=[
            pl.BlockSpec(blk, lambda b, j: (b, 0, j, 0, 0)),
            pl.BlockSpec((2 * FFT_N1, 2 * FFT_N1), lambda b, j: (0, 0)),
        ],
        out_specs=pl.BlockSpec(blk, lambda b, j: (b, 0, j, 0, 0)),
        out_shape=jax.ShapeDtypeStruct((BATCH, 2, FFT_N2, FFT_N1, FN_W), BF16),
        compiler_params=pltpu.CompilerParams(
            dimension_semantics=("arbitrary", "arbitrary")),
        name="fft_stage2",
    )(dmat, f2)


def _dft_tables(merge_tm):
    s = FFT1_S
    n2 = np.arange(FFT_N2, dtype=np.float64)
    n1 = np.arange(FFT_N1, dtype=np.float64)
    th = 2.0 * np.pi * np.outer(n2, n2) / FFT_N2
    f1 = np.stack([np.cos(th), -np.sin(th)]) / math.sqrt(SEQ)
    fbig = np.einsum("rkn,st->rksnt", f1, np.eye(s)).reshape(2 * FFT_N2 * s, FFT_N2 * s)
    tw = 2.0 * np.pi * np.outer(n2, n1) / SEQ
    tw = tw.reshape(FFT_N2, FFT_N1 // s, s).transpose(1, 0, 2).reshape(FFT_N1 // s, FFT_N2 * s)
    cwt = np.broadcast_to(np.cos(tw)[:, :, None], tw.shape + (LANES,))
    swt = np.broadcast_to(np.sin(tw)[:, :, None], tw.shape + (LANES,))
    th1 = 2.0 * np.pi * np.outer(n1, n1) / FFT_N1
    c1, s1 = np.cos(th1), np.sin(th1)
    f2 = np.block([[c1, s1], [-s1, c1]])
    cc = np.arange(FN_GW, dtype=np.float64)
    thc = 2.0 * np.pi * np.outer(cc, cc) / FN_GW
    ccs = np.concatenate([np.cos(thc), np.sin(thc)], axis=0) / math.sqrt(FN_GW)
    k1n = merge_tm // FFT_N2
    r = np.arange(merge_tm)
    perm = np.zeros((merge_tm, merge_tm))
    perm[r, (r % FFT_N2) * k1n + r // FFT_N2] = 1.0
    as32 = lambda a: jnp.asarray(np.ascontiguousarray(a), dtype=F32)
    return (as32(fbig).astype(BF16), as32(cwt), as32(swt), as32(f2).astype(BF16),
            as32(ccs).astype(BF16), as32(perm).astype(BF16))


def _memkv_kernel(m_ref, g_ref, b_ref, w_ref, o_ref):
    mn = _ln(m_ref[...], g_ref[...], b_ref[...]).astype(BF16)
    o_ref[...] = _dot(mn, w_ref[...]).astype(BF16)


def _memkv(mem2, g, b, w):
    return pl.pallas_call(
        _memkv_kernel,
        grid=(BATCH,),
        in_specs=[
            pl.BlockSpec((MEM_LEN, D), lambda i: (i, 0)),
            pl.BlockSpec((1, D), lambda i: (0, 0)),
            pl.BlockSpec((1, D), lambda i: (0, 0)),
            pl.BlockSpec((D, 2 * MQ_W), lambda i: (0, 0)),
        ],
        out_specs=pl.BlockSpec((MEM_LEN, 2 * MQ_W), lambda i: (i, 0)),
        out_shape=jax.ShapeDtypeStruct((BATCH * MEM_LEN, 2 * MQ_W), BF16),
        compiler_params=pltpu.CompilerParams(dimension_semantics=("arbitrary",)),
        name="mem_kv",
    )(mem2, g, b, w)


MERGE_TM = 512
MERGE_K1 = MERGE_TM // FFT_N2


def _pack_bf16_pair(v):
    n = v.shape[1] // 2
    bits = lax.bitcast_convert_type(v.astype(BF16).astype(F32), U32)
    return (bits[:, n:] & jnp.uint32(0xFFFF0000)) | (bits[:, :n] >> 16)


def _unpack_bf16_pair(p):
    lo = lax.bitcast_convert_type(p << 16, F32)
    hi = lax.bitcast_convert_type(p & jnp.uint32(0xFFFF0000), F32)
    return lo, hi


def _merge_kernel(x_ref, og_ref, zr_ref, zi_ref, mq_ref, gt_ref, kv_ref,
                  lng_ref, lnb_ref, wg_ref, ccs_ref, perm_ref, wf_ref, wm_ref, wo_ref, bo_ref,
                  l1g_ref, l1b_ref, wr2_ref, wrh_ref, br_ref,
                  h1_ref, h1p_ref, eidx_ref, topw_ref):
    tm = MERGE_TM
    y_gla = _dot(og_ref[...], wg_ref[...])

    zr = zr_ref[...].reshape(tm, FN_W)
    zi = zi_ref[...].reshape(tm, FN_W)
    ys = []
    for g in range(FN_G):
        sl = slice(g * FN_GW, (g + 1) * FN_GW)
        ys.append(_dot(jnp.concatenate([zr[:, sl], zi[:, sl]], axis=1), ccs_ref[...]))
    yp = jnp.concatenate(ys, axis=1).astype(BF16)
    y_fn = _dot(_dot(perm_ref[...], yp).astype(BF16), wf_ref[...])

    oms = []
    for hd in range(MEM_H):
        sl = slice(hd * MEM_HD, (hd + 1) * MEM_HD)
        s = _dot_nt(mq_ref[:, sl], kv_ref[:, sl]) * (MEM_HD ** -0.5)
        s = s - jnp.max(s, axis=-1, keepdims=True)
        p = jnp.exp(s)
        p = p / jnp.sum(p, axis=-1, keepdims=True)
        oms.append(_dot(p.astype(BF16), kv_ref[:, MQ_W + hd * MEM_HD:MQ_W + (hd + 1) * MEM_HD]))
    y_mem = _dot(jnp.concatenate(oms, axis=1).astype(BF16), wm_ref[...])

    merged = (jax.nn.sigmoid(gt_ref[:, 0:D].astype(F32)) * y_gla
              + jax.nn.sigmoid(gt_ref[:, D:2 * D].astype(F32)) * y_fn
              + jax.nn.sigmoid(gt_ref[:, 2 * D:3 * D].astype(F32)) * y_mem)
    mix = _dot(merged.astype(BF16), wo_ref[...]) + bo_ref[...]
    h = _ln(x_ref[...], lng_ref[...], lnb_ref[...])
    h1 = _ln(DN_ALPHA * h + mix, l1g_ref[...], l1b_ref[...])
    h1_ref[...] = h1
    h1p_ref[...] = _pack_bf16_pair(h1)

    h_hi, h_lo = _split_bf16(h1)
    d2 = _dot(h_hi, wr2_ref[...])
    logits = d2[:, :LANES] + d2[:, LANES:] + _dot(h_lo, wrh_ref[...]) + br_ref[...]
    lane = lax.broadcasted_iota(I32, (tm, LANES), 1)
    l = logits
    vals, idxs = [], []
    for _ in range(TOP_K):
        m = jnp.max(l, axis=-1, keepdims=True)
        idx = jnp.min(jnp.where(l == m, lane, LANES), axis=-1, keepdims=True)
        vals.append(m)
        idxs.append(idx)
        l = jnp.where(lane == idx, -jnp.inf, l)
    es = [jnp.exp(v - vals[0]) for v in vals]
    den = es[0] + es[1] + es[2] + es[3]
    eo = jnp.zeros((tm, LANES), I32)
    wo = jnp.zeros((tm, LANES), F32)
    for k in range(TOP_K):
        eo = jnp.where(lane == k, idxs[k], eo)
        wo = jnp.where(lane == k, es[k] / den, wo)
    eidx_ref[...] = eo
    topw_ref[...] = wo


def _merge(x2, og, z, mq, gates, kv, lng, lnb, wg, ccs, perm, wf, wm, wo, bo, l1g, l1b,
           wr2, wrh, br):
    tm = MERGE_TM
    per_b = SEQ // tm
    row = lambda i: (i, 0)
    const = lambda i: (0, 0)
    zblk = (None, None, FFT_N2, MERGE_K1, FN_W)
    outs = (
        jax.ShapeDtypeStruct((T, D), F32),
        jax.ShapeDtypeStruct((T, D // 2), U32),
        jax.ShapeDtypeStruct((T, LANES), I32),
        jax.ShapeDtypeStruct((T, LANES), F32),
    )
    return pl.pallas_call(
        _merge_kernel,
        grid=(T // tm,),
        in_specs=[
            pl.BlockSpec((tm, D), row),
            pl.BlockSpec((tm, D), row),
            pl.BlockSpec(zblk, lambda i: (i // per_b, 0, 0, i % per_b, 0)),
            pl.BlockSpec(zblk, lambda i: (i // per_b, 1, 0, i % per_b, 0)),
            pl.BlockSpec((tm, MQ_W), row),
            pl.BlockSpec((tm, 3 * D), row),
            pl.BlockSpec((MEM_LEN, 2 * MQ_W), lambda i: (i // per_b, 0)),
            pl.BlockSpec((1, D), const), pl.BlockSpec((1, D), const),
            pl.BlockSpec((D, D), const),
            pl.BlockSpec((2 * FN_GW, FN_GW), const),
            pl.BlockSpec((tm, tm), const),
            pl.BlockSpec((FN_W, D), const),
            pl.BlockSpec((MQ_W, D), const),
            pl.BlockSpec((D, D), const),
            pl.BlockSpec((1, D), const),
            pl.BlockSpec((1, D), const), pl.BlockSpec((1, D), const),
            pl.BlockSpec((D, 2 * LANES), const),
            pl.BlockSpec((D, LANES), const),
            pl.BlockSpec((1, LANES), const),
        ],
        out_specs=[
            pl.BlockSpec((tm, D), row),
            pl.BlockSpec((tm, D // 2), row),
            pl.BlockSpec((tm, LANES), row),
            pl.BlockSpec((tm, LANES), row),
        ],
        out_shape=outs,
        compiler_params=pltpu.CompilerParams(
            dimension_semantics=("arbitrary",),
            vmem_limit_bytes=58 * MIB),
        name="merge_ln1_router",
    )(x2, og, z, z, mq, gates, kv, lng, lnb, wg, ccs, perm, wf, wm, wo, bo, l1g, l1b,
      wr2, wrh, br)


PLAN_TP = 512


def _plan_kernel(e_ref, dest_ref, cnt_out_ref, cnt_ref, off_ref):
    p = pl.program_id(0)
    i = pl.program_id(1)
    tp = PLAN_TP
    lane = lax.broadcasted_iota(I32, (tp, LANES), 1)
    e = e_ref[...]
    onehots = [lane == e[:, k:k + 1] for k in range(TOP_K)]
    mf = jnp.zeros((tp, LANES), F32)
    for oh in onehots:
        mf = mf + jnp.where(oh, 1.0, 0.0)
    colsum = jnp.sum(mf, axis=0, keepdims=True)

    @pl.when(jnp.logical_and(p == 0, i == 0))
    def _():
        cnt_ref[...] = jnp.zeros_like(cnt_ref)

    @pl.when(jnp.logical_and(p == 1, i == 0))
    def _():
        tot = cnt_ref[...]
        cnt_out_ref[...] = jnp.broadcast_to(tot, cnt_out_ref.shape)
        lane1 = lax.broadcasted_iota(I32, (1, LANES), 1)
        inc = tot
        for s in (1, 2, 4, 8, 16, 32, 64):
            inc = inc + jnp.where(lane1 >= s, pltpu.roll(inc, s, 1), 0.0)
        off_ref[...] = inc - tot
        cnt_ref[...] = jnp.zeros_like(cnt_ref)

    @pl.when(p == 0)
    def _():
        dest_ref[...] = jnp.zeros_like(dest_ref)

    @pl.when(p == 1)
    def _():
        ri = lax.broadcasted_iota(I32, (tp, tp), 0)
        ci = lax.broadcasted_iota(I32, (tp, tp), 1)
        ltri = jnp.where(ri > ci, 1.0, 0.0).astype(BF16)
        rank = _dot(ltri, mf.astype(BF16)) + cnt_ref[...] + off_ref[...]
        out = jnp.zeros((tp, LANES), I32)
        for k in range(TOP_K):
            dk = jnp.sum(jnp.where(onehots[k], rank, 0.0), axis=-1, keepdims=True)
            out = jnp.where(lane == k, dk.astype(I32), out)
        dest_ref[...] = out

    cnt_ref[...] += colsum


def _plan(eidx):
    tp = PLAN_TP
    return pl.pallas_call(
        _plan_kernel,
        grid=(2, T // tp),
        in_specs=[pl.BlockSpec((tp, LANES), lambda p, i: (i, 0))],
        out_specs=[
            pl.BlockSpec((tp, LANES), lambda p, i: (i * p, 0)),
            pl.BlockSpec((8, LANES), lambda p, i: (0, 0)),
        ],
        out_shape=(jax.ShapeDtypeStruct((T, LANES), I32),
                   jax.ShapeDtypeStruct((8, LANES), F32)),
        scratch_shapes=[pltpu.VMEM((1, LANES), F32), pltpu.VMEM((1, LANES), F32)],
        compiler_params=pltpu.CompilerParams(
            dimension_semantics=("arbitrary", "arbitrary")),
        name="route_plan",
    )(eidx)


DISP_TM = 256
DISP_N = DISP_TM * TOP_K
ROWS_PER_ISSUE = 8


def _dispatch_kernel(dest_hbm, h_ref, xs_hbm, idx_smem, isem, sem):
    i = pl.program_id(0)
    slot = lax.rem(i, 2)

    def idx_copy(t, s):
        return pltpu.make_async_copy(dest_hbm.at[pl.ds(t, 1)], idx_smem.at[pl.ds(s, 1)],
                                     isem.at[s])

    @pl.when(i == 0)
    def _():
        idx_copy(0, 0).start()

    idx_copy(i, slot).wait()

    @pl.when(i + 1 < pl.num_programs(0))
    def _():
        idx_copy(i + 1, 1 - slot).start()

    def row_copies(ro):
        r0 = pl.multiple_of(ro * ROWS_PER_ISSUE, ROWS_PER_ISSUE)
        a0 = ro * (ROWS_PER_ISSUE * TOP_K)
        return [pltpu.make_async_copy(h_ref.at[pl.ds(r0 + ri, 1)],
                                      xs_hbm.at[pl.ds(idx_smem[slot, a0 + ri * TOP_K + k], 1)],
                                      sem)
                for ri in range(ROWS_PER_ISSUE) for k in range(TOP_K)]

    def issue(ro, c):
        for n, cp_ in enumerate(row_copies(ro)):
            cp_.start(priority=n % 2)
        return c

    lax.fori_loop(0, DISP_TM // ROWS_PER_ISSUE, issue, 0)

    def drain(ro, c):
        for cp_ in row_copies(ro):
            cp_.wait()
        return c

    lax.fori_loop(0, DISP_TM // ROWS_PER_ISSUE, drain, 0)


def _dispatch(dest2, h1p):
    return pl.pallas_call(
        _dispatch_kernel,
        grid=(T // DISP_TM,),
        in_specs=[
            pl.BlockSpec(memory_space=pl.ANY),
            pl.BlockSpec((DISP_TM, D // 2), lambda i: (i, 0)),
        ],
        out_specs=pl.BlockSpec(memory_space=pl.ANY),
        out_shape=jax.ShapeDtypeStruct((A_ROWS, D // 2), U32),
        scratch_shapes=[
            pltpu.SMEM((2, DISP_N), I32),
            pltpu.SemaphoreType.DMA((2,)),
            pltpu.SemaphoreType.DMA,
        ],
        compiler_params=pltpu.CompilerParams(dimension_semantics=("arbitrary",)),
        name="moe_dispatch",
    )(dest2, h1p)


MOE_BM = 512
MOE_NBLK = A_ROWS // MOE_BM
MOE_NW = MOE_NBLK + N_EXP


def _expert_kernel(we_ref, wb_ref, wlo_ref, whi_ref, wfe_ref, wfb_ref, wsl_ref, wnx_ref,
                   x_ref, wgu_hbm, bgu_ref, wdn_hbm, bdn_ref, o_ref,
                   wgu_f32, wdn_f32, wgu_bf, wdn_bf, sems):
    w = pl.program_id(0)
    e = we_ref[w]
    lo = wlo_ref[w]
    hi = whi_ref[w]

    def weight_copies(expert, slot):
        return (pltpu.make_async_copy(wgu_hbm.at[expert], wgu_f32.at[slot], sems.at[slot, 0]),
                pltpu.make_async_copy(wdn_hbm.at[expert], wdn_f32.at[slot], sems.at[slot, 1]))

    @pl.when(w == 0)
    def _():
        for cp_ in weight_copies(e, 0):
            cp_.start()

    @pl.when(wfe_ref[w] == 1)
    def _():
        slot = wsl_ref[w]
        for cp_ in weight_copies(e, slot):
            cp_.wait()
        wgu_bf[...] = wgu_f32[slot].astype(BF16)
        wdn_bf[...] = wdn_f32[slot].astype(BF16)
        nxt = wnx_ref[w]

        @pl.when(nxt >= 0)
        def _():
            for cp_ in weight_copies(nxt, 1 - slot):
                cp_.start()

    @pl.when(wfb_ref[w] == 1)
    def _():
        o_ref[...] = jnp.zeros_like(o_ref)

    @pl.when(hi > lo)
    def _():
        xlo, xhi = _unpack_bf16_pair(x_ref[...])
        half = D // 2
        gu = (_dot(xlo.astype(BF16), wgu_bf[:half, :]) + _dot(xhi.astype(BF16), wgu_bf[half:, :])
              + bgu_ref[pl.ds(e, 1), :])
        gate = jnp.minimum(gu[:, :D_FF], SW_LIMIT)
        up = jnp.clip(gu[:, D_FF:], -SW_LIMIT, SW_LIMIT)
        act = (up + 1.0) * (gate * jax.nn.sigmoid(SW_ALPHA * gate))
        out = _dot(act.astype(BF16), wdn_bf[...]) + bdn_ref[pl.ds(e, 1), :]
        rid = lax.broadcasted_iota(I32, (MOE_BM, 1), 0)
        keep = jnp.logical_and(rid >= lo, rid < hi)
        o_ref[...] = jnp.where(keep, _pack_bf16_pair(out), o_ref[...])


def _experts(meta, xs, w_gu, b_gu, w_down, b_down):
    return pl.pallas_call(
        _expert_kernel,
        grid_spec=pltpu.PrefetchScalarGridSpec(
            num_scalar_prefetch=len(meta),
            grid=(MOE_NW,),
            in_specs=[
                pl.BlockSpec((MOE_BM, D // 2), lambda w, we, wb, *_: (wb[w], 0)),
                pl.BlockSpec(memory_space=pl.ANY),
                pl.BlockSpec((N_EXP, 2 * D_FF), lambda w, *_: (0, 0)),
                pl.BlockSpec(memory_space=pl.ANY),
                pl.BlockSpec((N_EXP, D), lambda w, *_: (0, 0)),
            ],
            out_specs=pl.BlockSpec((MOE_BM, D // 2), lambda w, we, wb, *_: (wb[w], 0)),
            scratch_shapes=[
                pltpu.VMEM((2, D, 2 * D_FF), F32),
                pltpu.VMEM((2, D_FF, D), F32),
                pltpu.VMEM((D, 2 * D_FF), BF16),
                pltpu.VMEM((D_FF, D), BF16),
                pltpu.SemaphoreType.DMA((2, 2)),
            ],
        ),
        out_shape=jax.ShapeDtypeStruct((A_ROWS, D // 2), U32),
        compiler_params=pltpu.CompilerParams(
            dimension_semantics=("arbitrary",),
            vmem_limit_bytes=56 * MIB),
        name="moe_experts",
    )(*meta, xs, w_gu, b_gu, w_down, b_down)


def _work_items(counts):
    bm = MOE_BM
    end = jnp.cumsum(counts)
    start = end - counts
    first = start // bm
    last = jnp.maximum(end - 1, 0) // bm
    n_e = jnp.where(counts > 0, last - first + 1, 0)
    item_end = jnp.cumsum(n_e)
    item_start = item_end - n_e
    total = item_end[-1]
    w = jnp.arange(MOE_NW, dtype=I32)
    wc = jnp.minimum(w, total - 1)
    e_w = jnp.sum((item_end[None, :] <= wc[:, None]).astype(I32), axis=1)
    e_w = jnp.minimum(e_w, N_EXP - 1)
    blk = first[e_w] + (wc - item_start[e_w])
    lo = jnp.maximum(start[e_w], blk * bm) - blk * bm
    hi = jnp.minimum(end[e_w], (blk + 1) * bm) - blk * bm
    valid = w < total
    lo = jnp.where(valid, lo, 0)
    hi = jnp.where(valid, hi, 0)
    prev_e = jnp.concatenate([jnp.full((1,), -1, I32), e_w[:-1]])
    prev_b = jnp.concatenate([jnp.full((1,), -1, I32), blk[:-1]])
    fe = (e_w != prev_e).astype(I32)
    fb = (blk != prev_b).astype(I32)
    slot = (jnp.cumsum(fe) - 1) % 2
    first_at = jnp.where(fe == 1, w, MOE_NW)
    next_first = jnp.concatenate([lax.cummin(first_at, reverse=True)[1:],
                                  jnp.full((1,), MOE_NW, I32)])
    nxt = jnp.where(next_first < MOE_NW, e_w[jnp.minimum(next_first, MOE_NW - 1)], -1)
    return tuple(a.astype(I32) for a in (e_w, blk, lo, hi, fe, fb, slot, nxt))


COMB_TM = 256
COMB_N = COMB_TM * TOP_K


def _combine_kernel(dest_hbm, ys_hbm, h1_ref, tw_ref, g_ref, b_ref, o_ref,
                    idx_smem, gbuf, isem, sem):
    i = pl.program_id(0)
    n_tiles = pl.num_programs(0)

    def idx_copy(t):
        s = lax.rem(t, 3)
        return pltpu.make_async_copy(dest_hbm.at[pl.ds(t, 1)], idx_smem.at[pl.ds(s, 1)],
                                     isem.at[s])

    def row_copies(t, ro):
        si = lax.rem(t, 3)
        sb = lax.rem(t, 2)
        r0 = pl.multiple_of(ro * ROWS_PER_ISSUE, ROWS_PER_ISSUE)
        a0 = ro * (ROWS_PER_ISSUE * TOP_K)
        return [pltpu.make_async_copy(ys_hbm.at[pl.ds(idx_smem[si, a0 + ri * TOP_K + k], 1)],
                                      gbuf.at[sb, k, pl.ds(r0 + ri, 1)], sem.at[sb])
                for ri in range(ROWS_PER_ISSUE) for k in range(TOP_K)]

    def issue_tile(t):
        def issue(ro, c):
            for n, cp_ in enumerate(row_copies(t, ro)):
                cp_.start(priority=n % 2)
            return c
        lax.fori_loop(0, COMB_TM // ROWS_PER_ISSUE, issue, 0)

    @pl.when(i == 0)
    def _():
        idx_copy(0).start()
        idx_copy(0).wait()
        issue_tile(0)

        @pl.when(n_tiles > 1)
        def _():
            idx_copy(1).start()

    @pl.when(i + 2 < n_tiles)
    def _():
        idx_copy(i + 2).start()

    @pl.when(i + 1 < n_tiles)
    def _():
        idx_copy(i + 1).wait()
        issue_tile(i + 1)

    def drain(ro, c):
        for cp_ in row_copies(i, ro):
            cp_.wait()
        return c

    lax.fori_loop(0, COMB_TM // ROWS_PER_ISSUE, drain, 0)

    gcur = gbuf.at[lax.rem(i, 2)]
    tw = tw_ref[...]
    ylo = jnp.zeros((COMB_TM, D // 2), F32)
    yhi = jnp.zeros((COMB_TM, D // 2), F32)
    for k in range(TOP_K):
        lo, hi = _unpack_bf16_pair(gcur[k])
        wk = tw[:, k:k + 1]
        ylo = ylo + lo * wk
        yhi = yhi + hi * wk
    ff = jnp.concatenate([ylo, yhi], axis=1)
    o_ref[...] = _ln(DN_ALPHA * h1_ref[...] + ff, g_ref[...], b_ref[...])


def _combine(dest2, ys, h1, topw, g, b):
    tm = COMB_TM
    return pl.pallas_call(
        _combine_kernel,
        grid=(T // tm,),
        in_specs=[
            pl.BlockSpec(memory_space=pl.ANY),
            pl.BlockSpec(memory_space=pl.ANY),
            pl.BlockSpec((tm, D), lambda i: (i, 0)),
            pl.BlockSpec((tm, LANES), lambda i: (i, 0)),
            pl.BlockSpec((1, D), lambda i: (0, 0)),
            pl.BlockSpec((1, D), lambda i: (0, 0)),
        ],
        out_specs=pl.BlockSpec((tm, D), lambda i: (i, 0)),
        out_shape=jax.ShapeDtypeStruct((T, D), F32),
        scratch_shapes=[
            pltpu.SMEM((3, COMB_N), I32),
            pltpu.VMEM((2, TOP_K, tm, D // 2), U32),
            pltpu.SemaphoreType.DMA((3,)),
            pltpu.SemaphoreType.DMA((2,)),
        ],
        compiler_params=pltpu.CompilerParams(dimension_semantics=("arbitrary",)),
        name="moe_combine_ln2",
    )(dest2, ys, h1, topw, g, b)


SC_CORES = 2
SC_SUBCORES = 16
SC_WORKERS = SC_CORES * SC_SUBCORES
SC_CH = 64
SC_ROWS_PER_W = A_ROWS // SC_WORKERS
SC_NCH = SC_ROWS_PER_W // SC_CH


def _sc_gather(table, idx3):
    mesh = plsc.VectorSubcoreMesh(core_axis_name="c", subcore_axis_name="s")

    @functools.partial(
        pl.kernel, mesh=mesh,
        out_type=jax.ShapeDtypeStruct((A_ROWS, D // 2), U32),
        scratch_types=[
            pltpu.VMEM((SC_NCH, SC_CH), I32),
            pltpu.VMEM((2, SC_CH, D // 2), U32),
            pltpu.SemaphoreType.DMA((2,)),
            pltpu.SemaphoreType.DMA((2,)),
        ],
    )
    def k(table_hbm, idx_hbm, out_hbm, idx_v, rows_v, gsem, psem):
        wid = lax.axis_index("s") * SC_CORES + lax.axis_index("c")
        base = wid * SC_ROWS_PER_W
        pltpu.sync_copy(idx_hbm.at[wid], idx_v)

        def gather(j, b):
            return pltpu.make_async_copy(table_hbm.at[idx_v.at[j]], rows_v.at[b], gsem.at[b])

        def put(j, b):
            return pltpu.make_async_copy(rows_v.at[b], out_hbm.at[pl.ds(base + j * SC_CH, SC_CH)],
                                         psem.at[b])

        gather(0, 0).start()

        @pl.loop(0, SC_NCH, step=2)
        def _(j0):
            for b in range(2):
                j = j0 + b

                @pl.when(j + 1 < SC_NCH)
                def _():
                    @pl.when(j >= 1)
                    def _():
                        put(j - 1, 1 - b).wait()
                    gather(j + 1, 1 - b).start()

                gather(j, b).wait()
                put(j, b).start()

        put(SC_NCH - 2, 0).wait()
        put(SC_NCH - 1, 1).wait()

    return k(table, idx3)


SCD_TOK_PER_W = T // SC_WORKERS
SCD_NCH = SCD_TOK_PER_W // SC_CH


def _sc_dispatch(h1p, idx4):
    mesh = plsc.VectorSubcoreMesh(core_axis_name="c", subcore_axis_name="s")

    @functools.partial(
        pl.kernel, mesh=mesh,
        out_type=jax.ShapeDtypeStruct((A_ROWS, D // 2), U32),
        scratch_types=[
            pltpu.VMEM((SCD_NCH * TOP_K, SC_CH), I32),
            pltpu.VMEM((2, SC_CH, D // 2), U32),
            pltpu.SemaphoreType.DMA((2,)),
            pltpu.SemaphoreType.DMA((2,)),
        ],
    )
    def k(h_hbm, idx_hbm, xs_hbm, idx_v, rows_v, gsem, psem):
        wid = lax.axis_index("s") * SC_CORES + lax.axis_index("c")
        base = wid * SCD_TOK_PER_W
        pltpu.sync_copy(idx_hbm.at[wid], idx_v)

        def get(c, b):
            return pltpu.make_async_copy(h_hbm.at[pl.ds(base + c * SC_CH, SC_CH)], rows_v.at[b],
                                         gsem.at[b])

        def puts(c, b):
            return [pltpu.make_async_copy(rows_v.at[b], xs_hbm.at[idx_v.at[c * TOP_K + kk]],
                                          psem.at[b]) for kk in range(TOP_K)]

        get(0, 0).start()

        @pl.loop(0, SCD_NCH, step=2)
        def _(c0):
            for b in range(2):
                c = c0 + b

                @pl.when(c + 1 < SCD_NCH)
                def _():
                    @pl.when(c >= 1)
                    def _():
                        for cp_ in puts(c - 1, 1 - b):
                            cp_.wait()
                    get(c + 1, 1 - b).start()

                get(c, b).wait()
                for cp_ in puts(c, b):
                    cp_.start()

        for cp_ in puts(SCD_NCH - 2, 0) + puts(SCD_NCH - 1, 1):
            cp_.wait()

    return k(h1p, idx4)


def _combine_dense_kernel(g_ref, h1_ref, tw_ref, lg_ref, lb_ref, o_ref):
    tw = tw_ref[...]
    ylo = jnp.zeros((COMB_TM, D // 2), F32)
    yhi = jnp.zeros((COMB_TM, D // 2), F32)
    for k in range(TOP_K):
        lo, hi = _unpack_bf16_pair(g_ref[k])
        wk = tw[:, k:k + 1]
        ylo = ylo + lo * wk
        yhi = yhi + hi * wk
    ff = jnp.concatenate([ylo, yhi], axis=1)
    o_ref[...] = _ln(DN_ALPHA * h1_ref[...] + ff, lg_ref[...], lb_ref[...])


def _combine_dense(g4, h1, topw, g, b):
    tm = COMB_TM
    return pl.pallas_call(
        _combine_dense_kernel,
        grid=(T // tm,),
        in_specs=[
            pl.BlockSpec((TOP_K, tm, D // 2), lambda i: (0, i, 0)),
            pl.BlockSpec((tm, D), lambda i: (i, 0)),
            pl.BlockSpec((tm, LANES), lambda i: (i, 0)),
            pl.BlockSpec((1, D), lambda i: (0, 0)),
            pl.BlockSpec((1, D), lambda i: (0, 0)),
        ],
        out_specs=pl.BlockSpec((tm, D), lambda i: (i, 0)),
        out_shape=jax.ShapeDtypeStruct((T, D), F32),
        compiler_params=pltpu.CompilerParams(dimension_semantics=("arbitrary",)),
        name="moe_combine_dense_ln2",
    )(g4, h1, topw, g, b)


def _pad_cols(a, n):
    return jnp.pad(a, ((0, 0), (0, n - a.shape[1])))


def kernel(x, mem, ln_in_g, ln_in_b, ln_mem_g, ln_mem_b, w_in, b_in, w_decay_f, b_decay_f,
           w_decay_b, b_decay_b, gla_norm_g, w_br_gla, w_br_fnet, w_br_mem, w_mem_kv, w_out,
           b_out, ln1_g, ln1_b, w_router, b_router, w_gu, b_gu, w_down, b_down, ln2_g, ln2_b):
    assert x.shape == (BATCH, SEQ, D) and w_in.shape[0] == 1
    row = lambda a: a.reshape(1, -1)
    x2 = x.reshape(T, D)
    w_in0, b_in0 = w_in[0], b_in[0]
    c_lr, c_fn, c_mq, c_gt = 3072, 3072 + 2 * GLA_LR, 3104 + FN_W, 3104 + FN_W + MQ_W
    w_main = jnp.concatenate([w_in0[:, :c_lr], w_in0[:, c_gt:]], axis=1).astype(BF16)
    b_main = row(jnp.concatenate([b_in0[:c_lr], b_in0[c_gt:]]))
    w_lr = _pad_cols(w_in0[:, c_lr:c_fn], LANES).astype(BF16)
    b_lr = _pad_cols(row(b_in0[c_lr:c_fn]), LANES)
    w_mq = w_in0[:, c_mq:c_gt].astype(BF16)
    b_mq = row(b_in0[c_mq:c_gt])
    w_fn = w_in0[:, c_fn:c_mq].astype(BF16)
    b_fn = row(b_in0[c_fn:c_mq])
    lng, lnb = row(ln_in_g), row(ln_in_b)

    qk, v, r, gates, mq, lr = _inproj(x2, lng, lnb, w_main, b_main, w_mq, b_mq, w_lr, b_lr)

    zpad = jnp.zeros((LANES - 2 * GLA_LR, GLA_H * GLA_DK), F32)
    zlr = jnp.zeros((GLA_LR, GLA_H * GLA_DK), F32)
    wdf = jnp.concatenate([w_decay_f[0], zlr, zpad], axis=0).reshape(LANES, GLA_H, GLA_DK)
    wdb = jnp.concatenate([zlr, w_decay_b[0], zpad], axis=0).reshape(LANES, GLA_H, GLA_DK)
    wd = jnp.concatenate([wdf, wdb], axis=2).reshape(LANES, GLA_H * 2 * GLA_DK)
    bd = jnp.concatenate([b_decay_f[0].reshape(GLA_H, GLA_DK),
                          b_decay_b[0].reshape(GLA_H, GLA_DK)], axis=1).reshape(1, -1)
    og = _gla(qk, v, r, lr, wd, bd, row(gla_norm_g[0]))

    fbig, cwt, swt, f2, ccs, perm = _dft_tables(MERGE_TM)
    x4 = x.reshape(BATCH, FFT_N2, FFT_N1, D)
    z = _fft2(_fft1(x4, lng, lnb, w_fn, b_fn, fbig, cwt, swt), f2)

    kv = _memkv(mem.reshape(BATCH * MEM_LEN, D), row(ln_mem_g), row(ln_mem_b),
                w_mem_kv[0].astype(BF16))

    w_r = _pad_cols(w_router[0], LANES)
    wr_hi = w_r.astype(BF16)
    wr_lo = (w_r - wr_hi.astype(F32)).astype(BF16)
    b_r = jnp.concatenate([row(b_router[0]),
                           jnp.full((1, LANES - N_EXP), NEG_BIG, F32)], axis=1)
    h1, h1p, eidx, topw = _merge(
        x2, og, z, mq, gates, kv, lng, lnb,
        w_br_gla[0].astype(BF16), ccs, perm, w_br_fnet[0].astype(BF16),
        w_br_mem[0].astype(BF16), w_out[0].astype(BF16), row(b_out[0]),
        row(ln1_g[0]), row(ln1_b[0]), jnp.concatenate([wr_hi, wr_lo], axis=1), wr_hi, b_r)

    dest, cnt = _plan(eidx)
    counts = cnt[0, :N_EXP].astype(I32)
    dest_k = dest[:, :TOP_K]
    idx4 = dest_k.reshape(SC_WORKERS, SCD_NCH, SC_CH, TOP_K).transpose(0, 1, 3, 2)
    xs = _sc_dispatch(h1p, idx4.reshape(SC_WORKERS, SCD_NCH * TOP_K, SC_CH))
    ys = _experts(_work_items(counts), xs, w_gu[0], b_gu[0], w_down[0], b_down[0])
    dest_t = dest_k.T.reshape(SC_WORKERS, SC_NCH, SC_CH)
    g4 = _sc_gather(ys, dest_t).reshape(TOP_K, T, D // 2)
    out = _combine_dense(g4, h1, topw, row(ln2_g[0]), row(ln2_b[0]))
    return out.reshape(BATCH, SEQ, D)
```

```python
import functools
import math

import numpy as np
import jax
import jax.numpy as jnp
from jax import lax
from jax.experimental import pallas as pl
from jax.experimental.pallas import tpu as pltpu
from jax.experimental.pallas import tpu_sc as plsc

F32 = jnp.float32
BF16 = jnp.bfloat16
I32 = jnp.int32
U32 = jnp.uint32

D = 1024
BATCH = 4
SEQ = 4096
T = BATCH * SEQ
GLA_H = 4
GLA_DK = 128
GLA_DV = 256
GLA_LR = 16
GLA_TAU = 16.0
GLA_C = 64
FN_G = 4
FN_GW = 128
FN_W = 512
MEM_LEN = 256
MEM_H = 4
MEM_HD = 128
MQ_W = 512
N_EXP = 32
TOP_K = 4
D_FF = 1024
SW_LIMIT = 7.0
SW_ALPHA = 1.702
LN_EPS = 1e-5
RMS_EPS = 1e-6
DN_ALPHA = 2.0 ** 0.25
A_ROWS = T * TOP_K

FFT_N1 = 128
FFT_N2 = 32

LANES = 128
NEG_BIG = -1e30
MIB = 1024 * 1024


def _ln(x, g, b):
    mu = jnp.mean(x, axis=-1, keepdims=True)
    xc = x - mu
    var = jnp.mean(xc * xc, axis=-1, keepdims=True)
    return xc * lax.rsqrt(var + LN_EPS) * g + b


def _dot(a, b):
    return jnp.dot(a, b, preferred_element_type=F32)


def _dot_nt(a, b):
    return lax.dot_general(a, b, (((1,), (1,)), ((), ())), preferred_element_type=F32)


def _dot_tn(a, b):
    return lax.dot_general(a, b, (((0,), (0,)), ((), ())), preferred_element_type=F32)


def _split_bf16(a):
    hi = a.astype(BF16)
    return hi, (a - hi.astype(F32)).astype(BF16)


INPROJ_TM = 1024
INPROJ_TN = 1024


def _inproj_kernel(x_ref, g_ref, b_ref, w_ref, bias_ref, wmq_ref, bmq_ref, wlr_ref, blr_ref,
                   qk_ref, v_ref, r_ref, gates_ref, mq_ref, lr_ref, hb_ref):
    j = pl.program_id(1)

    @pl.when(j == 0)
    def _():
        hb = _ln(x_ref[...], g_ref[...], b_ref[...]).astype(BF16)
        hb_ref[...] = hb
        lr_ref[...] = _dot(hb, wlr_ref[...]) + blr_ref[...]
        mq_ref[...] = (_dot(hb, wmq_ref[...]) + bmq_ref[...]).astype(BF16)

    y = (_dot(hb_ref[...], w_ref[...]) + bias_ref[...]).astype(BF16)
    for idx, ref in enumerate((qk_ref, v_ref, r_ref)):
        @pl.when(j == idx)
        def _(ref=ref):
            ref[...] = y

    @pl.when(j >= 3)
    def _():
        gates_ref[...] = y


def _inproj(x2, ln_g, ln_b, w_main, b_main, w_mq, b_mq, w_lr, b_lr):
    tm, tn = INPROJ_TM, INPROJ_TN
    nj = w_main.shape[1] // tn
    row = lambda i, j: (i, 0)
    const = lambda i, j: (0, 0)
    outs = (
        jax.ShapeDtypeStruct((T, 1024), BF16),
        jax.ShapeDtypeStruct((T, 1024), BF16),
        jax.ShapeDtypeStruct((T, 1024), BF16),
        jax.ShapeDtypeStruct((T, 3072), BF16),
        jax.ShapeDtypeStruct((T, MQ_W), BF16),
        jax.ShapeDtypeStruct((T, LANES), F32),
    )
    return pl.pallas_call(
        _inproj_kernel,
        grid=(T // tm, nj),
        in_specs=[
            pl.BlockSpec((tm, D), row),
            pl.BlockSpec((1, D), const),
            pl.BlockSpec((1, D), const),
            pl.BlockSpec((D, tn), lambda i, j: (0, j)),
            pl.BlockSpec((1, tn), lambda i, j: (0, j)),
            pl.BlockSpec((D, MQ_W), const),
            pl.BlockSpec((1, MQ_W), const),
            pl.BlockSpec((D, LANES), const),
            pl.BlockSpec((1, LANES), const),
        ],
        out_specs=[
            pl.BlockSpec((tm, tn), row),
            pl.BlockSpec((tm, tn), row),
            pl.BlockSpec((tm, tn), row),
            pl.BlockSpec((tm, tn), lambda i, j: (i, jnp.maximum(j - 3, 0))),
            pl.BlockSpec((tm, MQ_W), row),
            pl.BlockSpec((tm, LANES), row),
        ],
        out_shape=outs,
        scratch_shapes=[pltpu.VMEM((tm, D), BF16)],
        compiler_params=pltpu.CompilerParams(
            dimension_semantics=("arbitrary", "arbitrary"),
            vmem_limit_bytes=48 * MIB),
        name="ln_inproj",
    )(x2, ln_g, ln_b, w_main, b_main, w_mq, b_mq, w_lr, b_lr)


GLA_BULK = 256
GLA_FIN = 512
GLA_NCH = SEQ // GLA_C
GLA_CPB = GLA_BULK // GLA_C


def _gla_kernel(q_ref, k_ref, v_ref, r_ref, lr_ref, wd_ref, bd_ref, g_ref, cs_ref, o_ref,
                acc_ref, qin_ref, kst_ref, dec_ref, stf_ref, stb_ref):
    C = GLA_C
    G = GLA_BULK
    DK = GLA_DK
    scale = DK ** -0.5
    ii = lax.broadcasted_iota(I32, (G, G), 0)
    jj = lax.broadcasted_iota(I32, (G, G), 1)
    same = (ii // C) == (jj // C)
    lower = jnp.logical_and(same, ii >= jj)
    upper = jnp.logical_and(same, ii <= jj)
    is_fwd = lax.broadcasted_iota(I32, (G, 2 * DK), 1) < DK

    def bulk(gi, carry):
        r0 = pl.multiple_of(gi * G, G)
        rows = pl.ds(r0, G)
        z = _dot(lr_ref[rows, :], wd_ref[...]) + bd_ref[...]
        la = -(jnp.maximum(-z, 0.0) + jnp.log(1.0 + jnp.exp(-jnp.abs(z)))) * (1.0 / GLA_TAU)
        la_hi, la_lo = _split_bf16(la)
        pre = _dot(cs_ref[...], la_hi) + _dot(cs_ref[...], la_lo)
        blast = jnp.concatenate(
            [jnp.broadcast_to(pre[ci * C + C - 1:ci * C + C, :], (C, 2 * DK))
             for ci in range(GLA_CPB)], axis=0)
        b = jnp.where(is_fwd, pre, blast - pre + la)
        qf32 = q_ref[rows, :].astype(F32)
        kf32 = k_ref[rows, :].astype(F32)
        q2 = jnp.concatenate([qf32, qf32], axis=1)
        k2 = jnp.concatenate([kf32, kf32], axis=1)
        qi = (q2 * (scale * jnp.exp(b))).astype(BF16)
        ki = (k2 * jnp.exp(-b)).astype(BF16)
        qin_ref[rows, :] = qi
        kst_ref[rows, :] = (k2 * jnp.exp(blast - b)).astype(BF16)
        dec = jnp.exp(blast)
        for ci in range(GLA_CPB):
            dec_ref[pl.ds(gi * GLA_CPB + ci, 1), :] = dec[ci * C:ci * C + 1, :]
        att = (jnp.where(lower, _dot_nt(qi[:, :DK], ki[:, :DK]), 0.0)
               + jnp.where(upper, _dot_nt(qi[:, DK:], ki[:, DK:]), 0.0))
        acc_ref[rows, :] = _dot(att.astype(BF16), v_ref[rows, :])
        return carry

    lax.fori_loop(0, SEQ // G, bulk, 0, unroll=2)

    stf_ref[...] = jnp.zeros_like(stf_ref)
    stb_ref[...] = jnp.zeros_like(stb_ref)

    def one(n, lanes, st_ref):
        rows = pl.ds(pl.multiple_of(n * C, C), C)
        st = st_ref[...]
        acc_ref[rows, :] += _dot_nt(qin_ref[rows, lanes], st.astype(BF16))
        st_ref[...] = (st * dec_ref[pl.ds(n, 1), :][:, lanes]
                       + _dot_tn(v_ref[rows, :], kst_ref[rows, lanes]))

    def step(i, carry):
        one(i, slice(0, DK), stf_ref)
        one(GLA_NCH - 1 - i, slice(DK, 2 * DK), stb_ref)
        return carry

    lax.fori_loop(0, GLA_NCH, step, 0, unroll=4)

    def fin(gi, carry):
        rows = pl.ds(pl.multiple_of(gi * GLA_FIN, GLA_FIN), GLA_FIN)
        o = acc_ref[rows, :]
        o = o * lax.rsqrt(jnp.mean(o * o, axis=-1, keepdims=True) + RMS_EPS) * g_ref[...]
        rg = r_ref[rows, :].astype(F32)
        o_ref[rows, :] = (o * (rg * jax.nn.sigmoid(rg))).astype(BF16)
        return carry

    lax.fori_loop(0, SEQ // GLA_FIN, fin, 0)


def _gla(qk, v, r, lr, wd, bd, g):
    i = np.arange(GLA_BULK)
    cs = ((i[:, None] // GLA_C) == (i[None, :] // GLA_C)) & (i[:, None] >= i[None, :])
    cs = jnp.asarray(cs, dtype=F32).astype(BF16)
    return pl.pallas_call(
        _gla_kernel,
        grid=(BATCH, GLA_H),
        in_specs=[
            pl.BlockSpec((SEQ, GLA_DK), lambda b, h: (b, h)),
            pl.BlockSpec((SEQ, GLA_DK), lambda b, h: (b, GLA_H + h)),
            pl.BlockSpec((SEQ, GLA_DV), lambda b, h: (b, h)),
            pl.BlockSpec((SEQ, GLA_DV), lambda b, h: (b, h)),
            pl.BlockSpec((SEQ, LANES), lambda b, h: (b, 0)),
            pl.BlockSpec((LANES, 2 * GLA_DK), lambda b, h: (0, h)),
            pl.BlockSpec((1, 2 * GLA_DK), lambda b, h: (0, h)),
            pl.BlockSpec((1, GLA_DV), lambda b, h: (0, 0)),
            pl.BlockSpec((GLA_BULK, GLA_BULK), lambda b, h: (0, 0)),
        ],
        out_specs=pl.BlockSpec((SEQ, GLA_DV), lambda b, h: (b, h)),
        out_shape=jax.ShapeDtypeStruct((T, GLA_H * GLA_DV), BF16),
        scratch_shapes=[
            pltpu.VMEM((SEQ, GLA_DV), F32),
            pltpu.VMEM((SEQ, 2 * GLA_DK), BF16),
            pltpu.VMEM((SEQ, 2 * GLA_DK), BF16),
            pltpu.VMEM((GLA_NCH, 2 * GLA_DK), F32),
            pltpu.VMEM((GLA_DV, GLA_DK), F32),
            pltpu.VMEM((GLA_DV, GLA_DK), F32),
        ],
        compiler_params=pltpu.CompilerParams(
            dimension_semantics=("arbitrary", "arbitrary"),
            vmem_limit_bytes=48 * MIB),
        name="gla",
    )(qk, qk, v, r, lr, wd, bd, g, cs)


FFT1_S = 16
FFT1_ROWS = FFT_N2 * FFT1_S
FFT2_KB = 4


def _fft1_kernel(x_ref, g_ref, b_ref, w_ref, bias_ref, fbig_ref, cw_ref, sw_ref, o_ref):
    xv = x_ref[...].reshape(FFT1_ROWS, D)
    hb = _ln(xv, g_ref[...], b_ref[...]).astype(BF16)
    fn = (_dot(hb, w_ref[...]) + bias_ref[...]).astype(BF16)
    a = _dot(fbig_ref[...], fn)
    ar = a[:FFT1_ROWS]
    ai = a[FFT1_ROWS:]
    cw = jnp.concatenate([cw_ref[...]] * (FN_W // LANES), axis=1)
    sw = jnp.concatenate([sw_ref[...]] * (FN_W // LANES), axis=1)
    o_ref[0] = (ar * cw + ai * sw).reshape(FFT_N2, FFT1_S, FN_W).astype(BF16)
    o_ref[1] = (ai * cw - ar * sw).reshape(FFT_N2, FFT1_S, FN_W).astype(BF16)


def _fft1(x4, ln_g, ln_b, w_fn, b_fn, fbig, cwt, swt):
    s = FFT1_S
    const = lambda b, j: (0, 0)
    return pl.pallas_call(
        _fft1_kernel,
        grid=(BATCH, FFT_N1 // s),
        in_specs=[
            pl.BlockSpec((None, FFT_N2, s, D), lambda b, j: (b, 0, j, 0)),
            pl.BlockSpec((1, D), const),
            pl.BlockSpec((1, D), const),
            pl.BlockSpec((D, FN_W), const),
            pl.BlockSpec((1, FN_W), const),
            pl.BlockSpec((2 * FFT1_ROWS, FFT1_ROWS), const),
            pl.BlockSpec((None, FFT1_ROWS, LANES), lambda b, j: (j, 0, 0)),
            pl.BlockSpec((None, FFT1_ROWS, LANES), lambda b, j: (j, 0, 0)),
        ],
        out_specs=pl.BlockSpec((None, 2, FFT_N2, s, FN_W), lambda b, j: (b, 0, 0, j, 0)),
        out_shape=jax.ShapeDtypeStruct((BATCH, 2, FFT_N2, FFT_N1, FN_W), BF16),
        compiler_params=pltpu.CompilerParams(
            dimension_semantics=("arbitrary", "arbitrary"),
            vmem_limit_bytes=40 * MIB),
        name="fft_stage1",
    )(x4, ln_g, ln_b, w_fn, b_fn, fbig, cwt, swt)


def _fft2_kernel(d_ref, f2_ref, o_ref):
    f2 = f2_ref[...]
    for kk in range(FFT2_KB):
        z = _dot(f2, jnp.concatenate([d_ref[0, kk], d_ref[1, kk]], axis=0))
        o_ref[0, kk] = z[:FFT_N1].astype(BF16)
        o_ref[1, kk] = z[FFT_N1:].astype(BF16)


def _fft2(dmat, f2):
    kb = FFT2_KB
    blk = (None, 2, kb, FFT_N1, FN_W)
    return pl.pallas_call(
        _fft2_kernel,
        grid=(BATCH, FFT_N2 // kb),
        in_specs=[
            pl.BlockSpec(blk, lambda b, j: (b, 0, j, 0, 0)),
            pl.BlockSpec((2 * FFT_N1, 2 * FFT_N1), lambda b, j: (0, 0)),
        ],
        out_specs=pl.BlockSpec(blk, lambda b, j: (b, 0, j, 0, 0)),
        out_shape=jax.ShapeDtypeStruct((BATCH, 2, FFT_N2, FFT_N1, FN_W), BF16),
        compiler_params=pltpu.CompilerParams(
            dimension_semantics=("arbitrary", "arbitrary")),
        name="fft_stage2",
    )(dmat, f2)


def _dft_tables(merge_tm):
    s = FFT1_S
    n2 = np.arange(FFT_N2, dtype=np.float64)
    n1 = np.arange(FFT_N1, dtype=np.float64)
    th = 2.0 * np.pi * np.outer(n2, n2) / FFT_N2
    f1 = np.stack([np.cos(th), -np.sin(th)]) / math.sqrt(SEQ)
    fbig = np.einsum("rkn,st->rksnt", f1, np.eye(s)).reshape(2 * FFT_N2 * s, FFT_N2 * s)
    tw = 2.0 * np.pi * np.outer(n2, n1) / SEQ
    tw = tw.reshape(FFT_N2, FFT_N1 // s, s).transpose(1, 0, 2).reshape(FFT_N1 // s, FFT_N2 * s)
    cwt = np.broadcast_to(np.cos(tw)[:, :, None], tw.shape + (LANES,))
    swt = np.broadcast_to(np.sin(tw)[:, :, None], tw.shape + (LANES,))
    th1 = 2.0 * np.pi * np.outer(n1, n1) / FFT_N1
    c1, s1 = np.cos(th1), np.sin(th1)
    f2 = np.block([[c1, s1], [-s1, c1]])
    cc = np.arange(FN_GW, dtype=np.float64)
    thc = 2.0 * np.pi * np.outer(cc, cc) / FN_GW
    ccs = np.concatenate([np.cos(thc), np.sin(thc)], axis=0) / math.sqrt(FN_GW)
    k1n = merge_tm // FFT_N2
    r = np.arange(merge_tm)
    perm = np.zeros((merge_tm, merge_tm))
    perm[r, (r % FFT_N2) * k1n + r // FFT_N2] = 1.0
    as32 = lambda a: jnp.asarray(np.ascontiguousarray(a), dtype=F32)
    return (as32(fbig).astype(BF16), as32(cwt), as32(swt), as32(f2).astype(BF16),
            as32(ccs).astype(BF16), as32(perm).astype(BF16))


def _memkv_kernel(m_ref, g_ref, b_ref, w_ref, o_ref):
    mn = _ln(m_ref[...], g_ref[...], b_ref[...]).astype(BF16)
    o_ref[...] = _dot(mn, w_ref[...]).astype(BF16)


def _memkv(mem2, g, b, w):
    return pl.pallas_call(
        _memkv_kernel,
        grid=(BATCH,),
        in_specs=[
            pl.BlockSpec((MEM_LEN, D), lambda i: (i, 0)),
            pl.BlockSpec((1, D), lambda i: (0, 0)),
            pl.BlockSpec((1, D), lambda i: (0, 0)),
            pl.BlockSpec((D, 2 * MQ_W), lambda i: (0, 0)),
        ],
        out_specs=pl.BlockSpec((MEM_LEN, 2 * MQ_W), lambda i: (i, 0)),
        out_shape=jax.ShapeDtypeStruct((BATCH * MEM_LEN, 2 * MQ_W), BF16),
        compiler_params=pltpu.CompilerParams(dimension_semantics=("arbitrary",)),
        name="mem_kv",
    )(mem2, g, b, w)


MERGE_TM = 512
MERGE_K1 = MERGE_TM // FFT_N2


def _pack_bf16_pair(v):
    n = v.shape[1] // 2
    bits = lax.bitcast_convert_type(v.astype(BF16).astype(F32), U32)
    return (bits[:, n:] & jnp.uint32(0xFFFF0000)) | (bits[:, :n] >> 16)


def _unpack_bf16_pair(p):
    lo = lax.bitcast_convert_type(p << 16, F32)
    hi = lax.bitcast_convert_type(p & jnp.uint32(0xFFFF0000), F32)
    return lo, hi


def _merge_kernel(x_ref, og_ref, zr_ref, zi_ref, mq_ref, gt_ref, kv_ref,
                  lng_ref, lnb_ref, wg_ref, ccs_ref, perm_ref, wf_ref, wm_ref, wo_ref, bo_ref,
                  l1g_ref, l1b_ref, wr2_ref, wrh_ref, br_ref,
                  h1_ref, h1p_ref, eidx_ref, topw_ref):
    tm = MERGE_TM
    y_gla = _dot(og_ref[...], wg_ref[...])

    zr = zr_ref[...].reshape(tm, FN_W)
    zi = zi_ref[...].reshape(tm, FN_W)
    ys = []
    for g in range(FN_G):
        sl = slice(g * FN_GW, (g + 1) * FN_GW)
        ys.append(_dot(jnp.concatenate([zr[:, sl], zi[:, sl]], axis=1), ccs_ref[...]))
    yp = jnp.concatenate(ys, axis=1).astype(BF16)
    y_fn = _dot(_dot(perm_ref[...], yp).astype(BF16), wf_ref[...])

    oms = []
    for hd in range(MEM_H):
        sl = slice(hd * MEM_HD, (hd + 1) * MEM_HD)
        s = _dot_nt(mq_ref[:, sl], kv_ref[:, sl]) * (MEM_HD ** -0.5)
        s = s - jnp.max(s, axis=-1, keepdims=True)
        p = jnp.exp(s)
        p = p / jnp.sum(p, axis=-1, keepdims=True)
        oms.append(_dot(p.astype(BF16), kv_ref[:, MQ_W + hd * MEM_HD:MQ_W + (hd + 1) * MEM_HD]))
    y_mem = _dot(jnp.concatenate(oms, axis=1).astype(BF16), wm_ref[...])

    merged = (jax.nn.sigmoid(gt_ref[:, 0:D].astype(F32)) * y_gla
              + jax.nn.sigmoid(gt_ref[:, D:2 * D].astype(F32)) * y_fn
              + jax.nn.sigmoid(gt_ref[:, 2 * D:3 * D].astype(F32)) * y_mem)
    mix = _dot(merged.astype(BF16), wo_ref[...]) + bo_ref[...]
    h = _ln(x_ref[...], lng_ref[...], lnb_ref[...])
    h1 = _ln(DN_ALPHA * h + mix, l1g_ref[...], l1b_ref[...])
    h1_ref[...] = h1
    h1p_ref[...] = _pack_bf16_pair(h1)

    h_hi, h_lo = _split_bf16(h1)
    d2 = _dot(h_hi, wr2_ref[...])
    logits = d2[:, :LANES] + d2[:, LANES:] + _dot(h_lo, wrh_ref[...]) + br_ref[...]
    lane = lax.broadcasted_iota(I32, (tm, LANES), 1)
    l = logits
    vals, idxs = [], []
    for _ in range(TOP_K):
        m = jnp.max(l, axis=-1, keepdims=True)
        idx = jnp.min(jnp.where(l == m, lane, LANES), axis=-1, keepdims=True)
        vals.append(m)
        idxs.append(idx)
        l = jnp.where(lane == idx, -jnp.inf, l)
    es = [jnp.exp(v - vals[0]) for v in vals]
    den = es[0] + es[1] + es[2] + es[3]
    eo = jnp.zeros((tm, LANES), I32)
    wo = jnp.zeros((tm, LANES), F32)
    for k in range(TOP_K):
        eo = jnp.where(lane == k, idxs[k], eo)
        wo = jnp.where(lane == k, es[k] / den, wo)
    eidx_ref[...] = eo
    topw_ref[...] = wo


def _merge(x2, og, z, mq, gates, kv, lng, lnb, wg, ccs, perm, wf, wm, wo, bo, l1g, l1b,
           wr2, wrh, br):
    tm = MERGE_TM
    per_b = SEQ // tm
    row = lambda i: (i, 0)
    const = lambda i: (0, 0)
    zblk = (None, None, FFT_N2, MERGE_K1, FN_W)
    outs = (
        jax.ShapeDtypeStruct((T, D), F32),
        jax.ShapeDtypeStruct((T, D // 2), U32),
        jax.ShapeDtypeStruct((T, LANES), I32),
        jax.ShapeDtypeStruct((T, LANES), F32),
    )
    return pl.pallas_call(
        _merge_kernel,
        grid=(T // tm,),
        in_specs=[
            pl.BlockSpec((tm, D), row),
            pl.BlockSpec((tm, D), row),
            pl.BlockSpec(zblk, lambda i: (i // per_b, 0, 0, i % per_b, 0)),
            pl.BlockSpec(zblk, lambda i: (i // per_b, 1, 0, i % per_b, 0)),
            pl.BlockSpec((tm, MQ_W), row),
            pl.BlockSpec((tm, 3 * D), row),
            pl.BlockSpec((MEM_LEN, 2 * MQ_W), lambda i: (i // per_b, 0)),
            pl.BlockSpec((1, D), const), pl.BlockSpec((1, D), const),
            pl.BlockSpec((D, D), const),
            pl.BlockSpec((2 * FN_GW, FN_GW), const),
            pl.BlockSpec((tm, tm), const),
            pl.BlockSpec((FN_W, D), const),
            pl.BlockSpec((MQ_W, D), const),
            pl.BlockSpec((D, D), const),
            pl.BlockSpec((1, D), const),
            pl.BlockSpec((1, D), const), pl.BlockSpec((1, D), const),
            pl.BlockSpec((D, 2 * LANES), const),
            pl.BlockSpec((D, LANES), const),
            pl.BlockSpec((1, LANES), const),
        ],
        out_specs=[
            pl.BlockSpec((tm, D), row),
            pl.BlockSpec((tm, D // 2), row),
            pl.BlockSpec((tm, LANES), row),
            pl.BlockSpec((tm, LANES), row),
        ],
        out_shape=outs,
        compiler_params=pltpu.CompilerParams(
            dimension_semantics=("arbitrary",),
            vmem_limit_bytes=58 * MIB),
        name="merge_ln1_router",
    )(x2, og, z, z, mq, gates, kv, lng, lnb, wg, ccs, perm, wf, wm, wo, bo, l1g, l1b,
      wr2, wrh, br)


PLAN_TP = 1024


def _plan_kernel(e_ref, dest_ref, cnt_out_ref, cnt_ref, off_ref):
    p = pl.program_id(0)
    i = pl.program_id(1)
    tp = PLAN_TP
    lane = lax.broadcasted_iota(I32, (tp, LANES), 1)
    e = e_ref[...]
    onehots = [lane == e[:, k:k + 1] for k in range(TOP_K)]
    mf = jnp.zeros((tp, LANES), F32)
    for oh in onehots:
        mf = mf + jnp.where(oh, 1.0, 0.0)
    colsum = jnp.sum(mf, axis=0, keepdims=True)

    @pl.when(jnp.logical_and(p == 0, i == 0))
    def _():
        cnt_ref[...] = jnp.zeros_like(cnt_ref)

    @pl.when(jnp.logical_and(p == 1, i == 0))
    def _():
        tot = cnt_ref[...]
        cnt_out_ref[...] = jnp.broadcast_to(tot, cnt_out_ref.shape)
        padded = jnp.floor((tot + (MOE_BM - 1)) * (1.0 / MOE_BM)) * MOE_BM
        lane1 = lax.broadcasted_iota(I32, (1, LANES), 1)
        inc = padded
        for s in (1, 2, 4, 8, 16, 32, 64):
            inc = inc + jnp.where(lane1 >= s, pltpu.roll(inc, s, 1), 0.0)
        off_ref[...] = inc - padded
        cnt_ref[...] = jnp.zeros_like(cnt_ref)

    @pl.when(p == 0)
    def _():
        dest_ref[...] = jnp.zeros_like(dest_ref)

    @pl.when(p == 1)
    def _():
        ri = lax.broadcasted_iota(I32, (tp, tp), 0)
        ci = lax.broadcasted_iota(I32, (tp, tp), 1)
        ltri = jnp.where(ri > ci, 1.0, 0.0).astype(BF16)
        rank = _dot(ltri, mf.astype(BF16)) + cnt_ref[...] + off_ref[...]
        out = jnp.zeros((tp, LANES), I32)
        for k in range(TOP_K):
            dk = jnp.sum(jnp.where(onehots[k], rank, 0.0), axis=-1, keepdims=True)
            out = jnp.where(lane == k, dk.astype(I32), out)
        dest_ref[...] = out

    cnt_ref[...] += colsum


def _plan(eidx):
    tp = PLAN_TP
    return pl.pallas_call(
        _plan_kernel,
        grid=(2, T // tp),
        in_specs=[pl.BlockSpec((tp, LANES), lambda p, i: (i, 0))],
        out_specs=[
            pl.BlockSpec((tp, LANES), lambda p, i: (i * p, 0)),
            pl.BlockSpec((8, LANES), lambda p, i: (0, 0)),
        ],
        out_shape=(jax.ShapeDtypeStruct((T, LANES), I32),
                   jax.ShapeDtypeStruct((8, LANES), F32)),
        scratch_shapes=[pltpu.VMEM((1, LANES), F32), pltpu.VMEM((1, LANES), F32)],
        compiler_params=pltpu.CompilerParams(
            dimension_semantics=("arbitrary", "arbitrary")),
        name="route_plan",
    )(eidx)


MOE_BM = 512
MOE_NW = A_ROWS // MOE_BM + N_EXP
XS_ROWS = MOE_NW * MOE_BM


def _expert_kernel(we_ref, wb_ref, wv_ref, wfe_ref, wsl_ref, wnx_ref,
                   x_ref, wgu_hbm, bgu_ref, wdn_hbm, bdn_ref, o_ref,
                   wgu_f32, wdn_f32, wgu_bf, wdn_bf, sems):
    w = pl.program_id(0)
    e = we_ref[w]

    def weight_copies(expert, slot):
        return (pltpu.make_async_copy(wgu_hbm.at[expert], wgu_f32.at[slot], sems.at[slot, 0]),
                pltpu.make_async_copy(wdn_hbm.at[expert], wdn_f32.at[slot], sems.at[slot, 1]))

    @pl.when(w == 0)
    def _():
        for cp_ in weight_copies(e, 0):
            cp_.start()

    @pl.when(wfe_ref[w] == 1)
    def _():
        slot = wsl_ref[w]
        for cp_ in weight_copies(e, slot):
            cp_.wait()
        wgu_bf[...] = wgu_f32[slot].astype(BF16)
        wdn_bf[...] = wdn_f32[slot].astype(BF16)
        nxt = wnx_ref[w]

        @pl.when(nxt >= 0)
        def _():
            for cp_ in weight_copies(nxt, 1 - slot):
                cp_.start()

    @pl.when(wv_ref[w] == 1)
    def _():
        xlo, xhi = _unpack_bf16_pair(x_ref[...])
        half = D // 2
        gu = (_dot(xlo.astype(BF16), wgu_bf[:half, :]) + _dot(xhi.astype(BF16), wgu_bf[half:, :])
              + bgu_ref[pl.ds(e, 1), :])
        gate = jnp.minimum(gu[:, :D_FF], SW_LIMIT)
        up = jnp.clip(gu[:, D_FF:], -SW_LIMIT, SW_LIMIT)
        act = (up + 1.0) * (gate * jax.nn.sigmoid(SW_ALPHA * gate))
        out = _dot(act.astype(BF16), wdn_bf[...]) + bdn_ref[pl.ds(e, 1), :]
        o_ref[...] = _pack_bf16_pair(out)


def _experts(meta, xs, w_gu, b_gu, w_down, b_down):
    return pl.pallas_call(
        _expert_kernel,
        grid_spec=pltpu.PrefetchScalarGridSpec(
            num_scalar_prefetch=len(meta),
            grid=(MOE_NW,),
            in_specs=[
                pl.BlockSpec((MOE_BM, D // 2), lambda w, we, wb, *_: (wb[w], 0)),
                pl.BlockSpec(memory_space=pl.ANY),
                pl.BlockSpec((N_EXP, 2 * D_FF), lambda w, *_: (0, 0)),
                pl.BlockSpec(memory_space=pl.ANY),
                pl.BlockSpec((N_EXP, D), lambda w, *_: (0, 0)),
            ],
            out_specs=pl.BlockSpec((MOE_BM, D // 2), lambda w, we, wb, *_: (wb[w], 0)),
            scratch_shapes=[
                pltpu.VMEM((2, D, 2 * D_FF), F32),
                pltpu.VMEM((2, D_FF, D), F32),
                pltpu.VMEM((D, 2 * D_FF), BF16),
                pltpu.VMEM((D_FF, D), BF16),
                pltpu.SemaphoreType.DMA((2, 2)),
            ],
        ),
        out_shape=jax.ShapeDtypeStruct((XS_ROWS, D // 2), U32),
        compiler_params=pltpu.CompilerParams(
            dimension_semantics=("arbitrary",),
            vmem_limit_bytes=56 * MIB),
        name="moe_experts",
    )(*meta, xs, w_gu, b_gu, w_down, b_down)


def _work_items(counts):
    n_e = (counts + MOE_BM - 1) // MOE_BM
    item_end = jnp.cumsum(n_e)
    total = item_end[-1]
    w = jnp.arange(MOE_NW, dtype=I32)
    valid = w < total
    wc = jnp.minimum(w, total - 1)
    e_w = jnp.sum((item_end[None, :] <= wc[:, None]).astype(I32), axis=1)
    e_w = jnp.minimum(e_w, N_EXP - 1)
    prev_e = jnp.concatenate([jnp.full((1,), -1, I32), e_w[:-1]])
    fe = (e_w != prev_e).astype(I32)
    slot = (jnp.cumsum(fe) - 1) % 2
    first_at = jnp.where(fe == 1, w, MOE_NW)
    next_first = jnp.concatenate([lax.cummin(first_at, reverse=True)[1:],
                                  jnp.full((1,), MOE_NW, I32)])
    nxt = jnp.where(next_first < MOE_NW, e_w[jnp.minimum(next_first, MOE_NW - 1)], -1)
    return tuple(a.astype(I32) for a in (e_w, wc, valid, fe, slot, nxt))


COMB_TM = 256
SC_CORES = 2
SC_SUBCORES = 16
SC_WORKERS = SC_CORES * SC_SUBCORES
SC_CH = 64
SC_ROWS_PER_W = A_ROWS // SC_WORKERS
SC_NCH = SC_ROWS_PER_W // SC_CH


def _sc_gather(table, idx3):
    mesh = plsc.VectorSubcoreMesh(core_axis_name="c", subcore_axis_name="s")

    @functools.partial(
        pl.kernel, mesh=mesh,
        out_type=jax.ShapeDtypeStruct((A_ROWS, D // 2), U32),
        scratch_types=[
            pltpu.VMEM((SC_NCH, SC_CH), I32),
            pltpu.VMEM((2, SC_CH, D // 2), U32),
            pltpu.SemaphoreType.DMA((2,)),
            pltpu.SemaphoreType.DMA((2,)),
        ],
    )
    def k(table_hbm, idx_hbm, out_hbm, idx_v, rows_v, gsem, psem):
        wid = lax.axis_index("s") * SC_CORES + lax.axis_index("c")
        base = wid * SC_ROWS_PER_W
        pltpu.sync_copy(idx_hbm.at[wid], idx_v)

        def gather(j, b):
            return pltpu.make_async_copy(table_hbm.at[idx_v.at[j]], rows_v.at[b], gsem.at[b])

        def put(j, b):
            return pltpu.make_async_copy(rows_v.at[b], out_hbm.at[pl.ds(base + j * SC_CH, SC_CH)],
                                         psem.at[b])

        gather(0, 0).start()

        @pl.loop(0, SC_NCH, step=2)
        def _(j0):
            for b in range(2):
                j = j0 + b

                @pl.when(j + 1 < SC_NCH)
                def _():
                    @pl.when(j >= 1)
                    def _():
                        put(j - 1, 1 - b).wait()
                    gather(j + 1, 1 - b).start()

                gather(j, b).wait()
                put(j, b).start()

        put(SC_NCH - 2, 0).wait()
        put(SC_NCH - 1, 1).wait()

    return k(table, idx3)


SCD_TOK_PER_W = T // SC_WORKERS
SCD_NCH = SCD_TOK_PER_W // SC_CH


def _sc_dispatch(h1p, idx4):
    mesh = plsc.VectorSubcoreMesh(core_axis_name="c", subcore_axis_name="s")

    @functools.partial(
        pl.kernel, mesh=mesh,
        out_type=jax.ShapeDtypeStruct((XS_ROWS, D // 2), U32),
        scratch_types=[
            pltpu.VMEM((SCD_NCH * TOP_K, SC_CH), I32),
            pltpu.VMEM((2, SC_CH, D // 2), U32),
            pltpu.SemaphoreType.DMA((2,)),
            pltpu.SemaphoreType.DMA((2,)),
        ],
    )
    def k(h_hbm, idx_hbm, xs_hbm, idx_v, rows_v, gsem, psem):
        wid = lax.axis_index("s") * SC_CORES + lax.axis_index("c")
        base = wid * SCD_TOK_PER_W
        pltpu.sync_copy(idx_hbm.at[wid], idx_v)

        def get(c, b):
            return pltpu.make_async_copy(h_hbm.at[pl.ds(base + c * SC_CH, SC_CH)], rows_v.at[b],
                                         gsem.at[b])

        def puts(c, b):
            return [pltpu.make_async_copy(rows_v.at[b], xs_hbm.at[idx_v.at[c * TOP_K + kk]],
                                          psem.at[b]) for kk in range(TOP_K)]

        get(0, 0).start()

        @pl.loop(0, SCD_NCH, step=2)
        def _(c0):
            for b in range(2):
                c = c0 + b

                @pl.when(c + 1 < SCD_NCH)
                def _():
                    @pl.when(c >= 1)
                    def _():
                        for cp_ in puts(c - 1, 1 - b):
                            cp_.wait()
                    get(c + 1, 1 - b).start()

                get(c, b).wait()
                for cp_ in puts(c, b):
                    cp_.start()

        for cp_ in puts(SCD_NCH - 2, 0) + puts(SCD_NCH - 1, 1):
            cp_.wait()

    return k(h1p, idx4)


def _combine_dense_kernel(g_ref, h1_ref, tw_ref, lg_ref, lb_ref, o_ref):
    tw = tw_ref[...]
    ylo = jnp.zeros((COMB_TM, D // 2), F32)
    yhi = jnp.zeros((COMB_TM, D // 2), F32)
    for k in range(TOP_K):
        lo, hi = _unpack_bf16_pair(g_ref[k])
        wk = tw[:, k:k + 1]
        ylo = ylo + lo * wk
        yhi = yhi + hi * wk
    ff = jnp.concatenate([ylo, yhi], axis=1)
    o_ref[...] = _ln(DN_ALPHA * h1_ref[...] + ff, lg_ref[...], lb_ref[...])


def _combine_dense(g4, h1, topw, g, b):
    tm = COMB_TM
    return pl.pallas_call(
        _combine_dense_kernel,
        grid=(T // tm,),
        in_specs=[
            pl.BlockSpec((TOP_K, tm, D // 2), lambda i: (0, i, 0)),
            pl.BlockSpec((tm, D), lambda i: (i, 0)),
            pl.BlockSpec((tm, LANES), lambda i: (i, 0)),
            pl.BlockSpec((1, D), lambda i: (0, 0)),
            pl.BlockSpec((1, D), lambda i: (0, 0)),
        ],
        out_specs=pl.BlockSpec((tm, D), lambda i: (i, 0)),
        out_shape=jax.ShapeDtypeStruct((T, D), F32),
        compiler_params=pltpu.CompilerParams(dimension_semantics=("arbitrary",)),
        name="moe_combine_dense_ln2",
    )(g4, h1, topw, g, b)


def _pad_cols(a, n):
    return jnp.pad(a, ((0, 0), (0, n - a.shape[1])))


def kernel(x, mem, ln_in_g, ln_in_b, ln_mem_g, ln_mem_b, w_in, b_in, w_decay_f, b_decay_f,
           w_decay_b, b_decay_b, gla_norm_g, w_br_gla, w_br_fnet, w_br_mem, w_mem_kv, w_out,
           b_out, ln1_g, ln1_b, w_router, b_router, w_gu, b_gu, w_down, b_down, ln2_g, ln2_b):
    assert x.shape == (BATCH, SEQ, D) and w_in.shape[0] == 1
    row = lambda a: a.reshape(1, -1)
    x2 = x.reshape(T, D)
    w_in0, b_in0 = w_in[0], b_in[0]
    c_lr, c_fn, c_mq, c_gt = 3072, 3072 + 2 * GLA_LR, 3104 + FN_W, 3104 + FN_W + MQ_W
    w_main = jnp.concatenate([w_in0[:, :c_lr], w_in0[:, c_gt:]], axis=1).astype(BF16)
    b_main = row(jnp.concatenate([b_in0[:c_lr], b_in0[c_gt:]]))
    w_lr = _pad_cols(w_in0[:, c_lr:c_fn], LANES).astype(BF16)
    b_lr = _pad_cols(row(b_in0[c_lr:c_fn]), LANES)
    w_mq = w_in0[:, c_mq:c_gt].astype(BF16)
    b_mq = row(b_in0[c_mq:c_gt])
    w_fn = w_in0[:, c_fn:c_mq].astype(BF16)
    b_fn = row(b_in0[c_fn:c_mq])
    lng, lnb = row(ln_in_g), row(ln_in_b)

    qk, v, r, gates, mq, lr = _inproj(x2, lng, lnb, w_main, b_main, w_mq, b_mq, w_lr, b_lr)

    zpad = jnp.zeros((LANES - 2 * GLA_LR, GLA_H * GLA_DK), F32)
    zlr = jnp.zeros((GLA_LR, GLA_H * GLA_DK), F32)
    wdf = jnp.concatenate([w_decay_f[0], zlr, zpad], axis=0).reshape(LANES, GLA_H, GLA_DK)
    wdb = jnp.concatenate([zlr, w_decay_b[0], zpad], axis=0).reshape(LANES, GLA_H, GLA_DK)
    wd = jnp.concatenate([wdf, wdb], axis=2).reshape(LANES, GLA_H * 2 * GLA_DK)
    bd = jnp.concatenate([b_decay_f[0].reshape(GLA_H, GLA_DK),
                          b_decay_b[0].reshape(GLA_H, GLA_DK)], axis=1).reshape(1, -1)
    og = _gla(qk, v, r, lr, wd, bd, row(gla_norm_g[0]))

    fbig, cwt, swt, f2, ccs, perm = _dft_tables(MERGE_TM)
    x4 = x.reshape(BATCH, FFT_N2, FFT_N1, D)
    z = _fft2(_fft1(x4, lng, lnb, w_fn, b_fn, fbig, cwt, swt), f2)

    kv = _memkv(mem.reshape(BATCH * MEM_LEN, D), row(ln_mem_g), row(ln_mem_b),
                w_mem_kv[0].astype(BF16))

    w_r = _pad_cols(w_router[0], LANES)
    wr_hi = w_r.astype(BF16)
    wr_lo = (w_r - wr_hi.astype(F32)).astype(BF16)
    b_r = jnp.concatenate([row(b_router[0]),
                           jnp.full((1, LANES - N_EXP), NEG_BIG, F32)], axis=1)
    h1, h1p, eidx, topw = _merge(
        x2, og, z, mq, gates, kv, lng, lnb,
        w_br_gla[0].astype(BF16), ccs, perm, w_br_fnet[0].astype(BF16),
        w_br_mem[0].astype(BF16), w_out[0].astype(BF16), row(b_out[0]),
        row(ln1_g[0]), row(ln1_b[0]), jnp.concatenate([wr_hi, wr_lo], axis=1), wr_hi, b_r)

    dest, cnt = _plan(eidx)
    counts = cnt[0, :N_EXP].astype(I32)
    dest_k = dest[:, :TOP_K]
    idx4 = dest_k.reshape(SC_WORKERS, SCD_NCH, SC_CH, TOP_K).transpose(0, 1, 3, 2)
    xs = _sc_dispatch(h1p, idx4.reshape(SC_WORKERS, SCD_NCH * TOP_K, SC_CH))
    ys = _experts(_work_items(counts), xs, w_gu[0], b_gu[0], w_down[0], b_down[0])
    dest_t = dest_k.T.reshape(SC_WORKERS, SC_NCH, SC_CH)
    g4 = _sc_gather(ys, dest_t).reshape(TOP_K, T, D // 2)
    out = _combine_dense(g4, h1, topw, row(ln2_g[0]), row(ln2_b[0]))
    return out.reshape(BATCH, SEQ, D)
```

```python
import functools
import math

import numpy as np
import jax
import jax.numpy as jnp
from jax import lax
from jax.experimental import pallas as pl
from jax.experimental.pallas import tpu as pltpu
from jax.experimental.pallas import tpu_sc as plsc

F32 = jnp.float32
BF16 = jnp.bfloat16
I32 = jnp.int32
U32 = jnp.uint32

D = 1024
BATCH = 4
SEQ = 4096
T = BATCH * SEQ
GLA_H = 4
GLA_DK = 128
GLA_DV = 256
GLA_LR = 16
GLA_TAU = 16.0
GLA_C = 64
FN_G = 4
FN_GW = 128
FN_W = 512
MEM_LEN = 256
MEM_H = 4
MEM_HD = 128
MQ_W = 512
N_EXP = 32
TOP_K = 4
D_FF = 1024
SW_LIMIT = 7.0
SW_ALPHA = 1.702
LN_EPS = 1e-5
RMS_EPS = 1e-6
DN_ALPHA = 2.0 ** 0.25
A_ROWS = T * TOP_K

FFT_N1 = 128
FFT_N2 = 32

LANES = 128
NEG_BIG = -1e30
MIB = 1024 * 1024


def _ln(x, g, b):
    mu = jnp.mean(x, axis=-1, keepdims=True)
    xc = x - mu
    var = jnp.mean(xc * xc, axis=-1, keepdims=True)
    return xc * lax.rsqrt(var + LN_EPS) * g + b


def _dot(a, b):
    return jnp.dot(a, b, preferred_element_type=F32)


def _dot_nt(a, b):
    return lax.dot_general(a, b, (((1,), (1,)), ((), ())), preferred_element_type=F32)


def _dot_tn(a, b):
    return lax.dot_general(a, b, (((0,), (0,)), ((), ())), preferred_element_type=F32)


def _split_bf16(a):
    hi = a.astype(BF16)
    return hi, (a - hi.astype(F32)).astype(BF16)


INPROJ_TM = 1024
INPROJ_TN = 1024


def _inproj_kernel(x_ref, g_ref, b_ref, w_ref, bias_ref, wmq_ref, bmq_ref, wlr_ref, blr_ref,
                   qk_ref, v_ref, r_ref, gates_ref, mq_ref, lr_ref, hb_ref):
    j = pl.program_id(1)

    @pl.when(j == 0)
    def _():
        hb = _ln(x_ref[...], g_ref[...], b_ref[...]).astype(BF16)
        hb_ref[...] = hb
        lr_ref[...] = _dot(hb, wlr_ref[...]) + blr_ref[...]
        mq_ref[...] = (_dot(hb, wmq_ref[...]) + bmq_ref[...]).astype(BF16)

    y = (_dot(hb_ref[...], w_ref[...]) + bias_ref[...]).astype(BF16)
    for idx, ref in enumerate((qk_ref, v_ref, r_ref)):
        @pl.when(j == idx)
        def _(ref=ref):
            ref[...] = y

    @pl.when(j >= 3)
    def _():
        gates_ref[...] = y


def _inproj(x2, ln_g, ln_b, w_main, b_main, w_mq, b_mq, w_lr, b_lr):
    tm, tn = INPROJ_TM, INPROJ_TN
    nj = w_main.shape[1] // tn
    row = lambda i, j: (i, 0)
    const = lambda i, j: (0, 0)
    outs = (
        jax.ShapeDtypeStruct((T, 1024), BF16),
        jax.ShapeDtypeStruct((T, 1024), BF16),
        jax.ShapeDtypeStruct((T, 1024), BF16),
        jax.ShapeDtypeStruct((T, 3072), BF16),
        jax.ShapeDtypeStruct((T, MQ_W), BF16),
        jax.ShapeDtypeStruct((T, LANES), F32),
    )
    return pl.pallas_call(
        _inproj_kernel,
        grid=(T // tm, nj),
        in_specs=[
            pl.BlockSpec((tm, D), row),
            pl.BlockSpec((1, D), const),
            pl.BlockSpec((1, D), const),
            pl.BlockSpec((D, tn), lambda i, j: (0, j)),
            pl.BlockSpec((1, tn), lambda i, j: (0, j)),
            pl.BlockSpec((D, MQ_W), const),
            pl.BlockSpec((1, MQ_W), const),
            pl.BlockSpec((D, LANES), const),
            pl.BlockSpec((1, LANES), const),
        ],
        out_specs=[
            pl.BlockSpec((tm, tn), row),
            pl.BlockSpec((tm, tn), row),
            pl.BlockSpec((tm, tn), row),
            pl.BlockSpec((tm, tn), lambda i, j: (i, jnp.maximum(j - 3, 0))),
            pl.BlockSpec((tm, MQ_W), row),
            pl.BlockSpec((tm, LANES), row),
        ],
        out_shape=outs,
        scratch_shapes=[pltpu.VMEM((tm, D), BF16)],
        compiler_params=pltpu.CompilerParams(
            dimension_semantics=("arbitrary", "arbitrary"),
            vmem_limit_bytes=48 * MIB),
        name="ln_inproj",
    )(x2, ln_g, ln_b, w_main, b_main, w_mq, b_mq, w_lr, b_lr)


GLA_BULK = 256
GLA_FIN = 512
GLA_NCH = SEQ // GLA_C
GLA_CPB = GLA_BULK // GLA_C


def _gla_kernel(q_ref, k_ref, v_ref, r_ref, lr_ref, wd_ref, bd_ref, g_ref, cs_ref, o_ref,
                acc_ref, qin_ref, kin_ref, kst_ref, dec_ref, u_ref, stf_ref, stb_ref):
    C = GLA_C
    G = GLA_BULK
    DK = GLA_DK
    NG = SEQ // G
    scale = DK ** -0.5
    ii = lax.broadcasted_iota(I32, (G, G), 0)
    jj = lax.broadcasted_iota(I32, (G, G), 1)
    same = (ii // C) == (jj // C)
    lower = jnp.logical_and(same, ii >= jj)
    upper = jnp.logical_and(same, ii <= jj)
    is_fwd = lax.broadcasted_iota(I32, (G, 2 * DK), 1) < DK
    chunk_of_row = lax.broadcasted_iota(I32, (G, DK), 0) // C

    def stage_a(gi):
        rows = pl.ds(pl.multiple_of(gi * G, G), G)
        z = _dot(lr_ref[rows, :], wd_ref[...]) + bd_ref[...]
        la = -(jnp.maximum(-z, 0.0) + jnp.log(1.0 + jnp.exp(-jnp.abs(z)))) * (1.0 / GLA_TAU)
        la_hi, la_lo = _split_bf16(la)
        pre = _dot(cs_ref[...], la_hi) + _dot(cs_ref[...], la_lo)
        blast = jnp.concatenate(
            [jnp.broadcast_to(pre[ci * C + C - 1:ci * C + C, :], (C, 2 * DK))
             for ci in range(GLA_CPB)], axis=0)
        b = jnp.where(is_fwd, pre, blast - pre + la)
        qf32 = q_ref[rows, :].astype(F32)
        kf32 = k_ref[rows, :].astype(F32)
        q2 = jnp.concatenate([qf32, qf32], axis=1)
        k2 = jnp.concatenate([kf32, kf32], axis=1)
        qin_ref[rows, :] = (q2 * (scale * jnp.exp(b))).astype(BF16)
        kin_ref[rows, :] = (k2 * jnp.exp(-b)).astype(BF16)
        kst_ref[rows, :] = (k2 * jnp.exp(blast - b)).astype(BF16)
        dec = jnp.exp(blast)
        for ci in range(GLA_CPB):
            dec_ref[pl.ds(gi * GLA_CPB + ci, 1), :] = dec[ci * C:ci * C + 1, :]

    def stage_b(gi):
        rows = pl.ds(pl.multiple_of(gi * G, G), G)
        qi = qin_ref[rows, :]
        ki = kin_ref[rows, :]
        ks = kst_ref[rows, :]
        vb = v_ref[rows, :]
        att = (jnp.where(lower, _dot_nt(qi[:, :DK], ki[:, :DK]), 0.0)
               + jnp.where(upper, _dot_nt(qi[:, DK:], ki[:, DK:]), 0.0))
        acc_ref[rows, :] = _dot(att.astype(BF16), vb)
        ksb = jnp.concatenate(
            [jnp.where(chunk_of_row == ci, ks[:, d * DK:(d + 1) * DK], jnp.zeros((G, DK), BF16))
             for d in range(2) for ci in range(GLA_CPB)], axis=1)
        u = _dot_tn(vb, ksb)
        for d in range(2):
            for ci in range(GLA_CPB):
                col = (d * GLA_CPB + ci) * DK
                u_ref[d, gi * GLA_CPB + ci] = u[:, col:col + DK]

    stage_a(0)

    def bulk(gi, carry):
        stage_b(gi - 1)
        stage_a(gi)
        return carry

    lax.fori_loop(1, NG, bulk, 0)
    stage_b(NG - 1)

    stf_ref[...] = jnp.zeros_like(stf_ref)
    stb_ref[...] = jnp.zeros_like(stb_ref)

    def one(n, d, st_ref):
        lanes = slice(d * DK, (d + 1) * DK)
        rows = pl.ds(pl.multiple_of(n * C, C), C)
        st = st_ref[...]
        acc_ref[rows, :] += _dot_nt(qin_ref[rows, lanes], st.astype(BF16))
        st_ref[...] = st * dec_ref[pl.ds(n, 1), :][:, lanes] + u_ref[d, n]

    def step(i, carry):
        one(i, 0, stf_ref)
        one(GLA_NCH - 1 - i, 1, stb_ref)
        return carry

    lax.fori_loop(0, GLA_NCH, step, 0, unroll=4)

    def fin(gi, carry):
        rows = pl.ds(pl.multiple_of(gi * GLA_FIN, GLA_FIN), GLA_FIN)
        o = acc_ref[rows, :]
        o = o * lax.rsqrt(jnp.mean(o * o, axis=-1, keepdims=True) + RMS_EPS) * g_ref[...]
        rg = r_ref[rows, :].astype(F32)
        o_ref[rows, :] = (o * (rg * jax.nn.sigmoid(rg))).astype(BF16)
        return carry

    lax.fori_loop(0, SEQ // GLA_FIN, fin, 0)


def _gla(qk, v, r, lr, wd, bd, g):
    i = np.arange(GLA_BULK)
    cs = ((i[:, None] // GLA_C) == (i[None, :] // GLA_C)) & (i[:, None] >= i[None, :])
    cs = jnp.asarray(cs, dtype=F32).astype(BF16)
    return pl.pallas_call(
        _gla_kernel,
        grid=(BATCH, GLA_H),
        in_specs=[
            pl.BlockSpec((SEQ, GLA_DK), lambda b, h: (b, h)),
            pl.BlockSpec((SEQ, GLA_DK), lambda b, h: (b, GLA_H + h)),
            pl.BlockSpec((SEQ, GLA_DV), lambda b, h: (b, h)),
            pl.BlockSpec((SEQ, GLA_DV), lambda b, h: (b, h)),
            pl.BlockSpec((SEQ, LANES), lambda b, h: (b, 0)),
            pl.BlockSpec((LANES, 2 * GLA_DK), lambda b, h: (0, h)),
            pl.BlockSpec((1, 2 * GLA_DK), lambda b, h: (0, h)),
            pl.BlockSpec((1, GLA_DV), lambda b, h: (0, 0)),
            pl.BlockSpec((GLA_BULK, GLA_BULK), lambda b, h: (0, 0)),
        ],
        out_specs=pl.BlockSpec((SEQ, GLA_DV), lambda b, h: (b, h)),
        out_shape=jax.ShapeDtypeStruct((T, GLA_H * GLA_DV), BF16),
        scratch_shapes=[
            pltpu.VMEM((SEQ, GLA_DV), F32),
            pltpu.VMEM((SEQ, 2 * GLA_DK), BF16),
            pltpu.VMEM((SEQ, 2 * GLA_DK), BF16),
            pltpu.VMEM((SEQ, 2 * GLA_DK), BF16),
            pltpu.VMEM((GLA_NCH, 2 * GLA_DK), F32),
            pltpu.VMEM((2, GLA_NCH, GLA_DV, GLA_DK), F32),
            pltpu.VMEM((GLA_DV, GLA_DK), F32),
            pltpu.VMEM((GLA_DV, GLA_DK), F32),
        ],
        compiler_params=pltpu.CompilerParams(
            dimension_semantics=("arbitrary", "arbitrary"),
            vmem_limit_bytes=58 * MIB),
        name="gla",
    )(qk, qk, v, r, lr, wd, bd, g, cs)


FFT1_S = 16
FFT1_ROWS = FFT_N2 * FFT1_S
FFT2_KB = 4


def _fft1_kernel(x_ref, g_ref, b_ref, w_ref, bias_ref, fbig_ref, cw_ref, sw_ref, o_ref):
    xv = x_ref[...].reshape(FFT1_ROWS, D)
    hb = _ln(xv, g_ref[...], b_ref[...]).astype(BF16)
    fn = (_dot(hb, w_ref[...]) + bias_ref[...]).astype(BF16)
    a = _dot(fbig_ref[...], fn)
    ar = a[:FFT1_ROWS]
    ai = a[FFT1_ROWS:]
    cw = jnp.concatenate([cw_ref[...]] * (FN_W // LANES), axis=1)
    sw = jnp.concatenate([sw_ref[...]] * (FN_W // LANES), axis=1)
    o_ref[0] = (ar * cw + ai * sw).reshape(FFT_N2, FFT1_S, FN_W).astype(BF16)
    o_ref[1] = (ai * cw - ar * sw).reshape(FFT_N2, FFT1_S, FN_W).astype(BF16)


def _fft1(x4, ln_g, ln_b, w_fn, b_fn, fbig, cwt, swt):
    s = FFT1_S
    const = lambda b, j: (0, 0)
    return pl.pallas_call(
        _fft1_kernel,
        grid=(BATCH, FFT_N1 // s),
        in_specs=[
            pl.BlockSpec((None, FFT_N2, s, D), lambda b, j: (b, 0, j, 0)),
            pl.BlockSpec((1, D), const),
            pl.BlockSpec((1, D), const),
            pl.BlockSpec((D, FN_W), const),
            pl.BlockSpec((1, FN_W), const),
            pl.BlockSpec((2 * FFT1_ROWS, FFT1_ROWS), const),
            pl.BlockSpec((None, FFT1_ROWS, LANES), lambda b, j: (j, 0, 0)),
            pl.BlockSpec((None, FFT1_ROWS, LANES), lambda b, j: (j, 0, 0)),
        ],
        out_specs=pl.BlockSpec((None, 2, FFT_N2, s, FN_W), lambda b, j: (b, 0, 0, j, 0)),
        out_shape=jax.ShapeDtypeStruct((BATCH, 2, FFT_N2, FFT_N1, FN_W), BF16),
        compiler_params=pltpu.CompilerParams(
            dimension_semantics=("arbitrary", "arbitrary"),
            vmem_limit_bytes=40 * MIB),
        name="fft_stage1",
    )(x4, ln_g, ln_b, w_fn, b_fn, fbig, cwt, swt)


def _fft2_kernel(d_ref, f2_ref, o_ref):
    f2 = f2_ref[...]
    for kk in range(FFT2_KB):
        z = _dot(f2, jnp.concatenate([d_ref[0, kk], d_ref[1, kk]], axis=0))
        o_ref[0, kk] = z[:FFT_N1].astype(BF16)
        o_ref[1, kk] = z[FFT_N1:].astype(BF16)


def _fft2(dmat, f2):
    kb = FFT2_KB
    blk = (None, 2, kb, FFT_N1, FN_W)
    return pl.pallas_call(
        _fft2_kernel,
        grid=(BATCH, FFT_N2 // kb),
        in_specs=[
            pl.BlockSpec(blk, lambda b, j: (b, 0, j, 0, 0)),
            pl.BlockSpec((2 * FFT_N1, 2 * FFT_N1), lambda b, j: (0, 0)),
        ],
        out_specs=pl.BlockSpec(blk, lambda b, j: (b, 0, j, 0, 0)),
        out_shape=jax.ShapeDtypeStruct((BATCH, 2, FFT_N2, FFT_N1, FN_W), BF16),
        compiler_params=pltpu.CompilerParams(
            dimension_semantics=("arbitrary", "arbitrary")),
        name="fft_stage2",
    )(dmat, f2)


def _dft_tables(merge_tm):
    s = FFT1_S
    n2 = np.arange(FFT_N2, dtype=np.float64)
    n1 = np.arange(FFT_N1, dtype=np.float64)
    th = 2.0 * np.pi * np.outer(n2, n2) / FFT_N2
    f1 = np.stack([np.cos(th), -np.sin(th)]) / math.sqrt(SEQ)
    fbig = np.einsum("rkn,st->rksnt", f1, np.eye(s)).reshape(2 * FFT_N2 * s, FFT_N2 * s)
    tw = 2.0 * np.pi * np.outer(n2, n1) / SEQ
    tw = tw.reshape(FFT_N2, FFT_N1 // s, s).transpose(1, 0, 2).reshape(FFT_N1 // s, FFT_N2 * s)
    cwt = np.broadcast_to(np.cos(tw)[:, :, None], tw.shape + (LANES,))
    swt = np.broadcast_to(np.sin(tw)[:, :, None], tw.shape + (LANES,))
    th1 = 2.0 * np.pi * np.outer(n1, n1) / FFT_N1
    c1, s1 = np.cos(th1), np.sin(th1)
    f2 = np.block([[c1, s1], [-s1, c1]])
    cc = np.arange(FN_GW, dtype=np.float64)
    thc = 2.0 * np.pi * np.outer(cc, cc) / FN_GW
    ccs = np.concatenate([np.cos(thc), np.sin(thc)], axis=0) / math.sqrt(FN_GW)
    k1n = merge_tm // FFT_N2
    r = np.arange(merge_tm)
    perm = np.zeros((merge_tm, merge_tm))
    perm[r, (r % FFT_N2) * k1n + r // FFT_N2] = 1.0
    as32 = lambda a: jnp.asarray(np.ascontiguousarray(a), dtype=F32)
    return (as32(fbig).astype(BF16), as32(cwt), as32(swt), as32(f2).astype(BF16),
            as32(ccs).astype(BF16), as32(perm).astype(BF16))


def _memkv_kernel(m_ref, g_ref, b_ref, w_ref, o_ref):
    mn = _ln(m_ref[...], g_ref[...], b_ref[...]).astype(BF16)
    o_ref[...] = _dot(mn, w_ref[...]).astype(BF16)


def _memkv(mem2, g, b, w):
    return pl.pallas_call(
        _memkv_kernel,
        grid=(BATCH,),
        in_specs=[
            pl.BlockSpec((MEM_LEN, D), lambda i: (i, 0)),
            pl.BlockSpec((1, D), lambda i: (0, 0)),
            pl.BlockSpec((1, D), lambda i: (0, 0)),
            pl.BlockSpec((D, 2 * MQ_W), lambda i: (0, 0)),
        ],
        out_specs=pl.BlockSpec((MEM_LEN, 2 * MQ_W), lambda i: (i, 0)),
        out_shape=jax.ShapeDtypeStruct((BATCH * MEM_LEN, 2 * MQ_W), BF16),
        compiler_params=pltpu.CompilerParams(dimension_semantics=("arbitrary",)),
        name="mem_kv",
    )(mem2, g, b, w)


MERGE_TM = 512
MERGE_K1 = MERGE_TM // FFT_N2


def _pack_bf16_pair(v):
    n = v.shape[1] // 2
    bits = lax.bitcast_convert_type(v.astype(BF16).astype(F32), U32)
    return (bits[:, n:] & jnp.uint32(0xFFFF0000)) | (bits[:, :n] >> 16)


def _unpack_bf16_pair(p):
    lo = lax.bitcast_convert_type(p << 16, F32)
    hi = lax.bitcast_convert_type(p & jnp.uint32(0xFFFF0000), F32)
    return lo, hi


def _merge_kernel(x_ref, og_ref, zr_ref, zi_ref, mq_ref, gt_ref, kv_ref,
                  lng_ref, lnb_ref, wg_ref, ccs_ref, perm_ref, wf_ref, wm_ref, wo_ref, bo_ref,
                  l1g_ref, l1b_ref, wr2_ref, wrh_ref, br_ref,
                  h1_ref, h1p_ref, eidx_ref, topw_ref):
    tm = MERGE_TM
    y_gla = _dot(og_ref[...], wg_ref[...])

    zr = zr_ref[...].reshape(tm, FN_W)
    zi = zi_ref[...].reshape(tm, FN_W)
    ys = []
    for g in range(FN_G):
        sl = slice(g * FN_GW, (g + 1) * FN_GW)
        ys.append(_dot(jnp.concatenate([zr[:, sl], zi[:, sl]], axis=1), ccs_ref[...]))
    yp = jnp.concatenate(ys, axis=1).astype(BF16)
    y_fn = _dot(_dot(perm_ref[...], yp).astype(BF16), wf_ref[...])

    oms = []
    for hd in range(MEM_H):
        sl = slice(hd * MEM_HD, (hd + 1) * MEM_HD)
        s = _dot_nt(mq_ref[:, sl], kv_ref[:, sl]) * (MEM_HD ** -0.5)
        s = s - jnp.max(s, axis=-1, keepdims=True)
        p = jnp.exp(s)
        p = p / jnp.sum(p, axis=-1, keepdims=True)
        oms.append(_dot(p.astype(BF16), kv_ref[:, MQ_W + hd * MEM_HD:MQ_W + (hd + 1) * MEM_HD]))
    y_mem = _dot(jnp.concatenate(oms, axis=1).astype(BF16), wm_ref[...])

    merged = (jax.nn.sigmoid(gt_ref[:, 0:D].astype(F32)) * y_gla
              + jax.nn.sigmoid(gt_ref[:, D:2 * D].astype(F32)) * y_fn
              + jax.nn.sigmoid(gt_ref[:, 2 * D:3 * D].astype(F32)) * y_mem)
    mix = _dot(merged.astype(BF16), wo_ref[...]) + bo_ref[...]
    h = _ln(x_ref[...], lng_ref[...], lnb_ref[...])
    h1 = _ln(DN_ALPHA * h + mix, l1g_ref[...], l1b_ref[...])
    h1_ref[...] = h1
    h1p_ref[...] = _pack_bf16_pair(h1)

    h_hi, h_lo = _split_bf16(h1)
    d2 = _dot(h_hi, wr2_ref[...])
    logits = d2[:, :LANES] + d2[:, LANES:] + _dot(h_lo, wrh_ref[...]) + br_ref[...]
    lane = lax.broadcasted_iota(I32, (tm, LANES), 1)
    l = logits
    vals, idxs = [], []
    for _ in range(TOP_K):
        m = jnp.max(l, axis=-1, keepdims=True)
        idx = jnp.min(jnp.where(l == m, lane, LANES), axis=-1, keepdims=True)
        vals.append(m)
        idxs.append(idx)
        l = jnp.where(lane == idx, -jnp.inf, l)
    es = [jnp.exp(v - vals[0]) for v in vals]
    den = es[0] + es[1] + es[2] + es[3]
    eo = jnp.zeros((tm, LANES), I32)
    wo = jnp.zeros((tm, LANES), F32)
    for k in range(TOP_K):
        eo = jnp.where(lane == k, idxs[k], eo)
        wo = jnp.where(lane == k, es[k] / den, wo)
    eidx_ref[...] = eo
    topw_ref[...] = wo


def _merge(x2, og, z, mq, gates, kv, lng, lnb, wg, ccs, perm, wf, wm, wo, bo, l1g, l1b,
           wr2, wrh, br):
    tm = MERGE_TM
    per_b = SEQ // tm
    row = lambda i: (i, 0)
    const = lambda i: (0, 0)
    zblk = (None, None, FFT_N2, MERGE_K1, FN_W)
    outs = (
        jax.ShapeDtypeStruct((T, D), F32),
        jax.ShapeDtypeStruct((T, D // 2), U32),
        jax.ShapeDtypeStruct((T, LANES), I32),
        jax.ShapeDtypeStruct((T, LANES), F32),
    )
    return pl.pallas_call(
        _merge_kernel,
        grid=(T // tm,),
        in_specs=[
            pl.BlockSpec((tm, D), row),
            pl.BlockSpec((tm, D), row),
            pl.BlockSpec(zblk, lambda i: (i // per_b, 0, 0, i % per_b, 0)),
            pl.BlockSpec(zblk, lambda i: (i // per_b, 1, 0, i % per_b, 0)),
            pl.BlockSpec((tm, MQ_W), row),
            pl.BlockSpec((tm, 3 * D), row),
            pl.BlockSpec((MEM_LEN, 2 * MQ_W), lambda i: (i // per_b, 0)),
            pl.BlockSpec((1, D), const), pl.BlockSpec((1, D), const),
            pl.BlockSpec((D, D), const),
            pl.BlockSpec((2 * FN_GW, FN_GW), const),
            pl.BlockSpec((tm, tm), const),
            pl.BlockSpec((FN_W, D), const),
            pl.BlockSpec((MQ_W, D), const),
            pl.BlockSpec((D, D), const),
            pl.BlockSpec((1, D), const),
            pl.BlockSpec((1, D), const), pl.BlockSpec((1, D), const),
            pl.BlockSpec((D, 2 * LANES), const),
            pl.BlockSpec((D, LANES), const),
            pl.BlockSpec((1, LANES), const),
        ],
        out_specs=[
            pl.BlockSpec((tm, D), row),
            pl.BlockSpec((tm, D // 2), row),
            pl.BlockSpec((tm, LANES), row),
            pl.BlockSpec((tm, LANES), row),
        ],
        out_shape=outs,
        compiler_params=pltpu.CompilerParams(
            dimension_semantics=("arbitrary",),
            vmem_limit_bytes=58 * MIB),
        name="merge_ln1_router",
    )(x2, og, z, z, mq, gates, kv, lng, lnb, wg, ccs, perm, wf, wm, wo, bo, l1g, l1b,
      wr2, wrh, br)


PLAN_TP = 1024


def _plan_kernel(e_ref, dest_ref, cnt_out_ref, cnt_ref, off_ref):
    p = pl.program_id(0)
    i = pl.program_id(1)
    tp = PLAN_TP
    lane = lax.broadcasted_iota(I32, (tp, LANES), 1)
    e = e_ref[...]
    onehots = [lane == e[:, k:k + 1] for k in range(TOP_K)]
    mf = jnp.zeros((tp, LANES), F32)
    for oh in onehots:
        mf = mf + jnp.where(oh, 1.0, 0.0)
    colsum = jnp.sum(mf, axis=0, keepdims=True)

    @pl.when(jnp.logical_and(p == 0, i == 0))
    def _():
        cnt_ref[...] = jnp.zeros_like(cnt_ref)

    @pl.when(jnp.logical_and(p == 1, i == 0))
    def _():
        tot = cnt_ref[...]
        cnt_out_ref[...] = jnp.broadcast_to(tot, cnt_out_ref.shape)
        padded = jnp.floor((tot + (MOE_BM - 1)) * (1.0 / MOE_BM)) * MOE_BM
        lane1 = lax.broadcasted_iota(I32, (1, LANES), 1)
        inc = padded
        for s in (1, 2, 4, 8, 16, 32, 64):
            inc = inc + jnp.where(lane1 >= s, pltpu.roll(inc, s, 1), 0.0)
        off_ref[...] = inc - padded
        cnt_ref[...] = jnp.zeros_like(cnt_ref)

    @pl.when(p == 0)
    def _():
        dest_ref[...] = jnp.zeros_like(dest_ref)

    @pl.when(p == 1)
    def _():
        ri = lax.broadcasted_iota(I32, (tp, tp), 0)
        ci = lax.broadcasted_iota(I32, (tp, tp), 1)
        ltri = jnp.where(ri > ci, 1.0, 0.0).astype(BF16)
        rank = _dot(ltri, mf.astype(BF16)) + cnt_ref[...] + off_ref[...]
        out = jnp.zeros((tp, LANES), I32)
        for k in range(TOP_K):
            dk = jnp.sum(jnp.where(onehots[k], rank, 0.0), axis=-1, keepdims=True)
            out = jnp.where(lane == k, dk.astype(I32), out)
        dest_ref[...] = out

    cnt_ref[...] += colsum


def _plan(eidx):
    tp = PLAN_TP
    return pl.pallas_call(
        _plan_kernel,
        grid=(2, T // tp),
        in_specs=[pl.BlockSpec((tp, LANES), lambda p, i: (i, 0))],
        out_specs=[
            pl.BlockSpec((tp, LANES), lambda p, i: (i * p, 0)),
            pl.BlockSpec((8, LANES), lambda p, i: (0, 0)),
        ],
        out_shape=(jax.ShapeDtypeStruct((T, LANES), I32),
                   jax.ShapeDtypeStruct((8, LANES), F32)),
        scratch_shapes=[pltpu.VMEM((1, LANES), F32), pltpu.VMEM((1, LANES), F32)],
        compiler_params=pltpu.CompilerParams(
            dimension_semantics=("arbitrary", "arbitrary")),
        name="route_plan",
    )(eidx)


MOE_BM = 512
MOE_NW = A_ROWS // MOE_BM + N_EXP
XS_ROWS = MOE_NW * MOE_BM


def _expert_kernel(we_ref, wb_ref, wv_ref, wfe_ref, wsl_ref, wnx_ref,
                   x_ref, wgu_hbm, bgu_ref, wdn_hbm, bdn_ref, o_ref,
                   wgu_f32, wdn_f32, wgu_bf, wdn_bf, sems):
    w = pl.program_id(0)
    e = we_ref[w]

    def weight_copies(expert, slot):
        return (pltpu.make_async_copy(wgu_hbm.at[expert], wgu_f32.at[slot], sems.at[slot, 0]),
                pltpu.make_async_copy(wdn_hbm.at[expert], wdn_f32.at[slot], sems.at[slot, 1]))

    @pl.when(w == 0)
    def _():
        for cp_ in weight_copies(e, 0):
            cp_.start()

    @pl.when(wfe_ref[w] == 1)
    def _():
        slot = wsl_ref[w]
        for cp_ in weight_copies(e, slot):
            cp_.wait()
        wgu_bf[...] = wgu_f32[slot].astype(BF16)
        wdn_bf[...] = wdn_f32[slot].astype(BF16)
        nxt = wnx_ref[w]

        @pl.when(nxt >= 0)
        def _():
            for cp_ in weight_copies(nxt, 1 - slot):
                cp_.start()

    @pl.when(wv_ref[w] == 1)
    def _():
        xlo, xhi = _unpack_bf16_pair(x_ref[...])
        half = D // 2
        gu = (_dot(xlo.astype(BF16), wgu_bf[:half, :]) + _dot(xhi.astype(BF16), wgu_bf[half:, :])
              + bgu_ref[pl.ds(e, 1), :])
        gate = jnp.minimum(gu[:, :D_FF], SW_LIMIT)
        up = jnp.clip(gu[:, D_FF:], -SW_LIMIT, SW_LIMIT)
        act = (up + 1.0) * (gate * jax.nn.sigmoid(SW_ALPHA * gate))
        out = _dot(act.astype(BF16), wdn_bf[...]) + bdn_ref[pl.ds(e, 1), :]
        o_ref[...] = _pack_bf16_pair(out)


def _experts(meta, xs, w_gu, b_gu, w_down, b_down):
    return pl.pallas_call(
        _expert_kernel,
        grid_spec=pltpu.PrefetchScalarGridSpec(
            num_scalar_prefetch=len(meta),
            grid=(MOE_NW,),
            in_specs=[
                pl.BlockSpec((MOE_BM, D // 2), lambda w, we, wb, *_: (wb[w], 0)),
                pl.BlockSpec(memory_space=pl.ANY),
                pl.BlockSpec((N_EXP, 2 * D_FF), lambda w, *_: (0, 0)),
                pl.BlockSpec(memory_space=pl.ANY),
                pl.BlockSpec((N_EXP, D), lambda w, *_: (0, 0)),
            ],
            out_specs=pl.BlockSpec((MOE_BM, D // 2), lambda w, we, wb, *_: (wb[w], 0)),
            scratch_shapes=[
                pltpu.VMEM((2, D, 2 * D_FF), F32),
                pltpu.VMEM((2, D_FF, D), F32),
                pltpu.VMEM((D, 2 * D_FF), BF16),
                pltpu.VMEM((D_FF, D), BF16),
                pltpu.SemaphoreType.DMA((2, 2)),
            ],
        ),
        out_shape=jax.ShapeDtypeStruct((XS_ROWS, D // 2), U32),
        compiler_params=pltpu.CompilerParams(
            dimension_semantics=("arbitrary",),
            vmem_limit_bytes=56 * MIB),
        name="moe_experts",
    )(*meta, xs, w_gu, b_gu, w_down, b_down)


def _work_items(counts):
    n_e = (counts + MOE_BM - 1) // MOE_BM
    item_end = jnp.cumsum(n_e)
    total = item_end[-1]
    w = jnp.arange(MOE_NW, dtype=I32)
    valid = w < total
    wc = jnp.minimum(w, total - 1)
    e_w = jnp.sum((item_end[None, :] <= wc[:, None]).astype(I32), axis=1)
    e_w = jnp.minimum(e_w, N_EXP - 1)
    prev_e = jnp.concatenate([jnp.full((1,), -1, I32), e_w[:-1]])
    fe = (e_w != prev_e).astype(I32)
    slot = (jnp.cumsum(fe) - 1) % 2
    first_at = jnp.where(fe == 1, w, MOE_NW)
    next_first = jnp.concatenate([lax.cummin(first_at, reverse=True)[1:],
                                  jnp.full((1,), MOE_NW, I32)])
    nxt = jnp.where(next_first < MOE_NW, e_w[jnp.minimum(next_first, MOE_NW - 1)], -1)
    return tuple(a.astype(I32) for a in (e_w, wc, valid, fe, slot, nxt))


COMB_TM = 256
SC_CORES = 2
SC_SUBCORES = 16
SC_WORKERS = SC_CORES * SC_SUBCORES
SC_CH = 64
SC_ROWS_PER_W = A_ROWS // SC_WORKERS
SC_NCH = SC_ROWS_PER_W // SC_CH


def _sc_gather(table, idx3):
    mesh = plsc.VectorSubcoreMesh(core_axis_name="c", subcore_axis_name="s")

    @functools.partial(
        pl.kernel, mesh=mesh,
        out_type=jax.ShapeDtypeStruct((A_ROWS, D // 2), U32),
        scratch_types=[
            pltpu.VMEM((SC_NCH, SC_CH), I32),
            pltpu.VMEM((2, SC_CH, D // 2), U32),
            pltpu.SemaphoreType.DMA((2,)),
            pltpu.SemaphoreType.DMA((2,)),
        ],
    )
    def k(table_hbm, idx_hbm, out_hbm, idx_v, rows_v, gsem, psem):
        wid = lax.axis_index("s") * SC_CORES + lax.axis_index("c")
        base = wid * SC_ROWS_PER_W
        pltpu.sync_copy(idx_hbm.at[wid], idx_v)

        def gather(j, b):
            return pltpu.make_async_copy(table_hbm.at[idx_v.at[j]], rows_v.at[b], gsem.at[b])

        def put(j, b):
            return pltpu.make_async_copy(rows_v.at[b], out_hbm.at[pl.ds(base + j * SC_CH, SC_CH)],
                                         psem.at[b])

        gather(0, 0).start()

        @pl.loop(0, SC_NCH, step=2)
        def _(j0):
            for b in range(2):
                j = j0 + b

                @pl.when(j + 1 < SC_NCH)
                def _():
                    @pl.when(j >= 1)
                    def _():
                        put(j - 1, 1 - b).wait()
                    gather(j + 1, 1 - b).start()

                gather(j, b).wait()
                put(j, b).start()

        put(SC_NCH - 2, 0).wait()
        put(SC_NCH - 1, 1).wait()

    return k(table, idx3)


SCD_TOK_PER_W = T // SC_WORKERS
SCD_NCH = SCD_TOK_PER_W // SC_CH


def _sc_dispatch(h1p, idx4):
    mesh = plsc.VectorSubcoreMesh(core_axis_name="c", subcore_axis_name="s")

    @functools.partial(
        pl.kernel, mesh=mesh,
        out_type=jax.ShapeDtypeStruct((XS_ROWS, D // 2), U32),
        scratch_types=[
            pltpu.VMEM((SCD_NCH * TOP_K, SC_CH), I32),
            pltpu.VMEM((2, SC_CH, D // 2), U32),
            pltpu.SemaphoreType.DMA((2,)),
            pltpu.SemaphoreType.DMA((2,)),
        ],
    )
    def k(h_hbm, idx_hbm, xs_hbm, idx_v, rows_v, gsem, psem):
        wid = lax.axis_index("s") * SC_CORES + lax.axis_index("c")
        base = wid * SCD_TOK_PER_W
        pltpu.sync_copy(idx_hbm.at[wid], idx_v)

        def get(c, b):
            return pltpu.make_async_copy(h_hbm.at[pl.ds(base + c * SC_CH, SC_CH)], rows_v.at[b],
                                         gsem.at[b])

        def puts(c, b):
            return [pltpu.make_async_copy(rows_v.at[b], xs_hbm.at[idx_v.at[c * TOP_K + kk]],
                                          psem.at[b]) for kk in range(TOP_K)]

        get(0, 0).start()

        @pl.loop(0, SCD_NCH, step=2)
        def _(c0):
            for b in range(2):
                c = c0 + b

                @pl.when(c + 1 < SCD_NCH)
                def _():
                    @pl.when(c >= 1)
                    def _():
                        for cp_ in puts(c - 1, 1 - b):
                            cp_.wait()
                    get(c + 1, 1 - b).start()

                get(c, b).wait()
                for cp_ in puts(c, b):
                    cp_.start()

        for cp_ in puts(SCD_NCH - 2, 0) + puts(SCD_NCH - 1, 1):
            cp_.wait()

    return k(h1p, idx4)


def _combine_dense_kernel(g_ref, h1_ref, tw_ref, lg_ref, lb_ref, o_ref):
    tw = tw_ref[...]
    ylo = jnp.zeros((COMB_TM, D // 2), F32)
    yhi = jnp.zeros((COMB_TM, D // 2), F32)
    for k in range(TOP_K):
        lo, hi = _unpack_bf16_pair(g_ref[k])
        wk = tw[:, k:k + 1]
        ylo = ylo + lo * wk
        yhi = yhi + hi * wk
    ff = jnp.concatenate([ylo, yhi], axis=1)
    o_ref[...] = _ln(DN_ALPHA * h1_ref[...] + ff, lg_ref[...], lb_ref[...])


def _combine_dense(g4, h1, topw, g, b):
    tm = COMB_TM
    return pl.pallas_call(
        _combine_dense_kernel,
        grid=(T // tm,),
        in_specs=[
            pl.BlockSpec((TOP_K, tm, D // 2), lambda i: (0, i, 0)),
            pl.BlockSpec((tm, D), lambda i: (i, 0)),
            pl.BlockSpec((tm, LANES), lambda i: (i, 0)),
            pl.BlockSpec((1, D), lambda i: (0, 0)),
            pl.BlockSpec((1, D), lambda i: (0, 0)),
        ],
        out_specs=pl.BlockSpec((tm, D), lambda i: (i, 0)),
        out_shape=jax.ShapeDtypeStruct((T, D), F32),
        compiler_params=pltpu.CompilerParams(dimension_semantics=("arbitrary",)),
        name="moe_combine_dense_ln2",
    )(g4, h1, topw, g, b)


def _pad_cols(a, n):
    return jnp.pad(a, ((0, 0), (0, n - a.shape[1])))


def kernel(x, mem, ln_in_g, ln_in_b, ln_mem_g, ln_mem_b, w_in, b_in, w_decay_f, b_decay_f,
           w_decay_b, b_decay_b, gla_norm_g, w_br_gla, w_br_fnet, w_br_mem, w_mem_kv, w_out,
           b_out, ln1_g, ln1_b, w_router, b_router, w_gu, b_gu, w_down, b_down, ln2_g, ln2_b):
    assert x.shape == (BATCH, SEQ, D) and w_in.shape[0] == 1
    row = lambda a: a.reshape(1, -1)
    x2 = x.reshape(T, D)
    w_in0, b_in0 = w_in[0], b_in[0]
    c_lr, c_fn, c_mq, c_gt = 3072, 3072 + 2 * GLA_LR, 3104 + FN_W, 3104 + FN_W + MQ_W
    w_main = jnp.concatenate([w_in0[:, :c_lr], w_in0[:, c_gt:]], axis=1).astype(BF16)
    b_main = row(jnp.concatenate([b_in0[:c_lr], b_in0[c_gt:]]))
    w_lr = _pad_cols(w_in0[:, c_lr:c_fn], LANES).astype(BF16)
    b_lr = _pad_cols(row(b_in0[c_lr:c_fn]), LANES)
    w_mq = w_in0[:, c_mq:c_gt].astype(BF16)
    b_mq = row(b_in0[c_mq:c_gt])
    w_fn = w_in0[:, c_fn:c_mq].astype(BF16)
    b_fn = row(b_in0[c_fn:c_mq])
    lng, lnb = row(ln_in_g), row(ln_in_b)

    qk, v, r, gates, mq, lr = _inproj(x2, lng, lnb, w_main, b_main, w_mq, b_mq, w_lr, b_lr)

    zpad = jnp.zeros((LANES - 2 * GLA_LR, GLA_H * GLA_DK), F32)
    zlr = jnp.zeros((GLA_LR, GLA_H * GLA_DK), F32)
    wdf = jnp.concatenate([w_decay_f[0], zlr, zpad], axis=0).reshape(LANES, GLA_H, GLA_DK)
    wdb = jnp.concatenate([zlr, w_decay_b[0], zpad], axis=0).reshape(LANES, GLA_H, GLA_DK)
    wd = jnp.concatenate([wdf, wdb], axis=2).reshape(LANES, GLA_H * 2 * GLA_DK)
    bd = jnp.concatenate([b_decay_f[0].reshape(GLA_H, GLA_DK),
                          b_decay_b[0].reshape(GLA_H, GLA_DK)], axis=1).reshape(1, -1)
    og = _gla(qk, v, r, lr, wd, bd, row(gla_norm_g[0]))

    fbig, cwt, swt, f2, ccs, perm = _dft_tables(MERGE_TM)
    x4 = x.reshape(BATCH, FFT_N2, FFT_N1, D)
    z = _fft2(_fft1(x4, lng, lnb, w_fn, b_fn, fbig, cwt, swt), f2)

    kv = _memkv(mem.reshape(BATCH * MEM_LEN, D), row(ln_mem_g), row(ln_mem_b),
                w_mem_kv[0].astype(BF16))

    w_r = _pad_cols(w_router[0], LANES)
    wr_hi = w_r.astype(BF16)
    wr_lo = (w_r - wr_hi.astype(F32)).astype(BF16)
    b_r = jnp.concatenate([row(b_router[0]),
                           jnp.full((1, LANES - N_EXP), NEG_BIG, F32)], axis=1)
    h1, h1p, eidx, topw = _merge(
        x2, og, z, mq, gates, kv, lng, lnb,
        w_br_gla[0].astype(BF16), ccs, perm, w_br_fnet[0].astype(BF16),
        w_br_mem[0].astype(BF16), w_out[0].astype(BF16), row(b_out[0]),
        row(ln1_g[0]), row(ln1_b[0]), jnp.concatenate([wr_hi, wr_lo], axis=1), wr_hi, b_r)

    dest, cnt = _plan(eidx)
    counts = cnt[0, :N_EXP].astype(I32)
    dest_k = dest[:, :TOP_K]
    idx4 = dest_k.reshape(SC_WORKERS, SCD_NCH, SC_CH, TOP_K).transpose(0, 1, 3, 2)
    xs = _sc_dispatch(h1p, idx4.reshape(SC_WORKERS, SCD_NCH * TOP_K, SC_CH))
    ys = _experts(_work_items(counts), xs, w_gu[0], b_gu[0], w_down[0], b_down[0])
    dest_t = dest_k.T.reshape(SC_WORKERS, SC_NCH, SC_CH)
    g4 = _sc_gather(ys, dest_t).reshape(TOP_K, T, D // 2)
    out = _combine_dense(g4, h1, topw, row(ln2_g[0]), row(ln2_b[0]))
    return out.reshape(BATCH, SEQ, D)
```

```python
import functools
import math

import numpy as np
import jax
import jax.numpy as jnp
from jax import lax
from jax.experimental import pallas as pl
from jax.experimental.pallas import tpu as pltpu
from jax.experimental.pallas import tpu_sc as plsc

F32 = jnp.float32
BF16 = jnp.bfloat16
I32 = jnp.int32
U32 = jnp.uint32

D = 1024
BATCH = 4
SEQ = 4096
T = BATCH * SEQ
GLA_H = 4
GLA_DK = 128
GLA_DV = 256
GLA_LR = 16
GLA_TAU = 16.0
GLA_C = 64
FN_G = 4
FN_GW = 128
FN_W = 512
MEM_LEN = 256
MEM_H = 4
MEM_HD = 128
MQ_W = 512
N_EXP = 32
TOP_K = 4
D_FF = 1024
SW_LIMIT = 7.0
SW_ALPHA = 1.702
LN_EPS = 1e-5
RMS_EPS = 1e-6
DN_ALPHA = 2.0 ** 0.25
A_ROWS = T * TOP_K

FFT_N1 = 128
FFT_N2 = 32

LANES = 128
NEG_BIG = -1e30
MIB = 1024 * 1024


def _ln(x, g, b):
    mu = jnp.mean(x, axis=-1, keepdims=True)
    xc = x - mu
    var = jnp.mean(xc * xc, axis=-1, keepdims=True)
    return xc * lax.rsqrt(var + LN_EPS) * g + b


def _dot(a, b):
    return jnp.dot(a, b, preferred_element_type=F32)


def _dot_nt(a, b):
    return lax.dot_general(a, b, (((1,), (1,)), ((), ())), preferred_element_type=F32)


def _dot_tn(a, b):
    return lax.dot_general(a, b, (((0,), (0,)), ((), ())), preferred_element_type=F32)


def _split_bf16(a):
    hi = a.astype(BF16)
    return hi, (a - hi.astype(F32)).astype(BF16)


INPROJ_TM = 1024
INPROJ_TN = 1024


PROJ_W = 6 * 1024


def _inproj_kernel(x_ref, g_ref, b_ref, w_ref, bias_ref, wmq_ref, bmq_ref, wlr_ref, blr_ref,
                   proj_ref, mq_ref, lr_ref, hb_ref):
    @pl.when(pl.program_id(1) == 0)
    def _():
        hb = _ln(x_ref[...], g_ref[...], b_ref[...]).astype(BF16)
        hb_ref[...] = hb
        lr_ref[...] = _dot(hb, wlr_ref[...]) + blr_ref[...]
        mq_ref[...] = (_dot(hb, wmq_ref[...]) + bmq_ref[...]).astype(BF16)

    proj_ref[...] = (_dot(hb_ref[...], w_ref[...]) + bias_ref[...]).astype(BF16)


def _inproj(x2, ln_g, ln_b, w_main, b_main, w_mq, b_mq, w_lr, b_lr):
    tm, tn = INPROJ_TM, INPROJ_TN
    nj = PROJ_W // tn
    row = lambda i, j: (i, 0)
    const = lambda i, j: (0, 0)
    outs = (
        jax.ShapeDtypeStruct((T, PROJ_W), BF16),
        jax.ShapeDtypeStruct((T, MQ_W), BF16),
        jax.ShapeDtypeStruct((T, LANES), F32),
    )
    return pl.pallas_call(
        _inproj_kernel,
        grid=(T // tm, nj),
        in_specs=[
            pl.BlockSpec((tm, D), row),
            pl.BlockSpec((1, D), const),
            pl.BlockSpec((1, D), const),
            pl.BlockSpec((D, tn), lambda i, j: (0, j)),
            pl.BlockSpec((1, tn), lambda i, j: (0, j)),
            pl.BlockSpec((D, MQ_W), const),
            pl.BlockSpec((1, MQ_W), const),
            pl.BlockSpec((D, LANES), const),
            pl.BlockSpec((1, LANES), const),
        ],
        out_specs=[
            pl.BlockSpec((tm, tn), lambda i, j: (i, j)),
            pl.BlockSpec((tm, MQ_W), row),
            pl.BlockSpec((tm, LANES), row),
        ],
        out_shape=outs,
        scratch_shapes=[pltpu.VMEM((tm, D), BF16)],
        compiler_params=pltpu.CompilerParams(
            dimension_semantics=("arbitrary", "arbitrary"),
            vmem_limit_bytes=48 * MIB),
        name="ln_inproj",
    )(x2, ln_g, ln_b, w_main, b_main, w_mq, b_mq, w_lr, b_lr)


GLA_BULK = 256
GLA_FIN = 512
GLA_NCH = SEQ // GLA_C
GLA_CPB = GLA_BULK // GLA_C


def _gla_kernel(q_ref, k_ref, v_ref, r_ref, lr_ref, wd_ref, bd_ref, g_ref, cs_ref, o_ref,
                acc_ref, qin_ref, kin_ref, kst_ref, dec_ref, u_ref, stf_ref, stb_ref):
    C = GLA_C
    G = GLA_BULK
    DK = GLA_DK
    NG = SEQ // G
    scale = DK ** -0.5
    ii = lax.broadcasted_iota(I32, (G, G), 0)
    jj = lax.broadcasted_iota(I32, (G, G), 1)
    same = (ii // C) == (jj // C)
    lower = jnp.logical_and(same, ii >= jj)
    upper = jnp.logical_and(same, ii <= jj)
    is_fwd = lax.broadcasted_iota(I32, (G, 2 * DK), 1) < DK
    chunk_of_row = lax.broadcasted_iota(I32, (G, DK), 0) // C

    def stage_a(gi):
        rows = pl.ds(pl.multiple_of(gi * G, G), G)
        z = _dot(lr_ref[rows, :], wd_ref[...]) + bd_ref[...]
        la = -(jnp.maximum(-z, 0.0) + jnp.log(1.0 + jnp.exp(-jnp.abs(z)))) * (1.0 / GLA_TAU)
        la_hi, la_lo = _split_bf16(la)
        pre2 = _dot(cs_ref[...], jnp.concatenate([la_hi, la_lo], axis=1))
        pre = pre2[:, :2 * DK] + pre2[:, 2 * DK:]
        blast = jnp.concatenate(
            [jnp.broadcast_to(pre[ci * C + C - 1:ci * C + C, :], (C, 2 * DK))
             for ci in range(GLA_CPB)], axis=0)
        b = jnp.where(is_fwd, pre, blast - pre + la)
        qf32 = q_ref[rows, :].astype(F32)
        kf32 = k_ref[rows, :].astype(F32)
        q2 = jnp.concatenate([qf32, qf32], axis=1)
        k2 = jnp.concatenate([kf32, kf32], axis=1)
        qin_ref[rows, :] = (q2 * (scale * jnp.exp(b))).astype(BF16)
        kin_ref[rows, :] = (k2 * jnp.exp(-b)).astype(BF16)
        kst_ref[rows, :] = (k2 * jnp.exp(blast - b)).astype(BF16)
        dec = jnp.exp(blast)
        for ci in range(GLA_CPB):
            dec_ref[pl.ds(gi * GLA_CPB + ci, 1), :] = dec[ci * C:ci * C + 1, :]

    def stage_b(gi):
        rows = pl.ds(pl.multiple_of(gi * G, G), G)
        qi = qin_ref[rows, :]
        ki = kin_ref[rows, :]
        ks = kst_ref[rows, :]
        vb = v_ref[rows, :]
        att = (jnp.where(lower, _dot_nt(qi[:, :DK], ki[:, :DK]), 0.0)
               + jnp.where(upper, _dot_nt(qi[:, DK:], ki[:, DK:]), 0.0))
        acc_ref[rows, :] = _dot(att.astype(BF16), vb)
        ksb = jnp.concatenate(
            [jnp.where(chunk_of_row == ci, ks[:, d * DK:(d + 1) * DK], jnp.zeros((G, DK), BF16))
             for d in range(2) for ci in range(GLA_CPB)], axis=1)
        u = _dot_tn(vb, ksb)
        for d in range(2):
            for ci in range(GLA_CPB):
                col = (d * GLA_CPB + ci) * DK
                u_ref[d, gi * GLA_CPB + ci] = u[:, col:col + DK]

    stage_a(0)

    def bulk(gi, carry):
        stage_b(gi - 1)
        stage_a(gi)
        return carry

    lax.fori_loop(1, NG, bulk, 0)
    stage_b(NG - 1)

    stf_ref[...] = jnp.zeros_like(stf_ref)
    stb_ref[...] = jnp.zeros_like(stb_ref)

    def one(n, d, st_ref):
        lanes = slice(d * DK, (d + 1) * DK)
        rows = pl.ds(pl.multiple_of(n * C, C), C)
        st = st_ref[...]
        acc_ref[rows, :] += _dot_nt(qin_ref[rows, lanes], st.astype(BF16))
        st_ref[...] = st * dec_ref[pl.ds(n, 1), :][:, lanes] + u_ref[d, n]

    def step(i, carry):
        one(i, 0, stf_ref)
        one(GLA_NCH - 1 - i, 1, stb_ref)
        return carry

    lax.fori_loop(0, GLA_NCH, step, 0, unroll=4)

    def fin(gi, carry):
        rows = pl.ds(pl.multiple_of(gi * GLA_FIN, GLA_FIN), GLA_FIN)
        o = acc_ref[rows, :]
        o = o * lax.rsqrt(jnp.mean(o * o, axis=-1, keepdims=True) + RMS_EPS) * g_ref[...]
        rg = r_ref[rows, :].astype(F32)
        o_ref[rows, :] = (o * (rg * jax.nn.sigmoid(rg))).astype(BF16)
        return carry

    lax.fori_loop(0, SEQ // GLA_FIN, fin, 0)


def _gla(proj, lr, wd, bd, g):
    i = np.arange(GLA_BULK)
    cs = ((i[:, None] // GLA_C) == (i[None, :] // GLA_C)) & (i[:, None] >= i[None, :])
    cs = jnp.asarray(cs, dtype=F32).astype(BF16)
    v_blk = 1024 // GLA_DV
    return pl.pallas_call(
        _gla_kernel,
        grid=(BATCH, GLA_H),
        in_specs=[
            pl.BlockSpec((SEQ, GLA_DK), lambda b, h: (b, h)),
            pl.BlockSpec((SEQ, GLA_DK), lambda b, h: (b, GLA_H + h)),
            pl.BlockSpec((SEQ, GLA_DV), lambda b, h: (b, v_blk + h)),
            pl.BlockSpec((SEQ, GLA_DV), lambda b, h: (b, 2 * v_blk + h)),
            pl.BlockSpec((SEQ, LANES), lambda b, h: (b, 0)),
            pl.BlockSpec((LANES, 2 * GLA_DK), lambda b, h: (0, h)),
            pl.BlockSpec((1, 2 * GLA_DK), lambda b, h: (0, h)),
            pl.BlockSpec((1, GLA_DV), lambda b, h: (0, 0)),
            pl.BlockSpec((GLA_BULK, GLA_BULK), lambda b, h: (0, 0)),
        ],
        out_specs=pl.BlockSpec((SEQ, GLA_DV), lambda b, h: (b, h)),
        out_shape=jax.ShapeDtypeStruct((T, GLA_H * GLA_DV), BF16),
        scratch_shapes=[
            pltpu.VMEM((SEQ, GLA_DV), F32),
            pltpu.VMEM((SEQ, 2 * GLA_DK), BF16),
            pltpu.VMEM((SEQ, 2 * GLA_DK), BF16),
            pltpu.VMEM((SEQ, 2 * GLA_DK), BF16),
            pltpu.VMEM((GLA_NCH, 2 * GLA_DK), F32),
            pltpu.VMEM((2, GLA_NCH, GLA_DV, GLA_DK), F32),
            pltpu.VMEM((GLA_DV, GLA_DK), F32),
            pltpu.VMEM((GLA_DV, GLA_DK), F32),
        ],
        compiler_params=pltpu.CompilerParams(
            dimension_semantics=("arbitrary", "arbitrary"),
            vmem_limit_bytes=58 * MIB),
        name="gla",
    )(proj, proj, proj, proj, lr, wd, bd, g, cs)


FFT1_S = 16
FFT1_ROWS = FFT_N2 * FFT1_S
FFT2_KB = 4


def _fft1_kernel(x_ref, g_ref, b_ref, w_ref, bias_ref, fbig_ref, cw_ref, sw_ref, o_ref):
    xv = x_ref[...].reshape(FFT1_ROWS, D)
    hb = _ln(xv, g_ref[...], b_ref[...]).astype(BF16)
    fn = (_dot(hb, w_ref[...]) + bias_ref[...]).astype(BF16)
    a = _dot(fbig_ref[...], fn)
    ar = a[:FFT1_ROWS]
    ai = a[FFT1_ROWS:]
    cw = jnp.concatenate([cw_ref[...]] * (FN_W // LANES), axis=1)
    sw = jnp.concatenate([sw_ref[...]] * (FN_W // LANES), axis=1)
    o_ref[0] = (ar * cw + ai * sw).reshape(FFT_N2, FFT1_S, FN_W).astype(BF16)
    o_ref[1] = (ai * cw - ar * sw).reshape(FFT_N2, FFT1_S, FN_W).astype(BF16)


def _fft1(x4, ln_g, ln_b, w_fn, b_fn, fbig, cwt, swt):
    s = FFT1_S
    const = lambda b, j: (0, 0)
    return pl.pallas_call(
        _fft1_kernel,
        grid=(BATCH, FFT_N1 // s),
        in_specs=[
            pl.BlockSpec((None, FFT_N2, s, D), lambda b, j: (b, 0, j, 0)),
            pl.BlockSpec((1, D), const),
            pl.BlockSpec((1, D), const),
            pl.BlockSpec((D, FN_W), const),
            pl.BlockSpec((1, FN_W), const),
            pl.BlockSpec((2 * FFT1_ROWS, FFT1_ROWS), const),
            pl.BlockSpec((None, FFT1_ROWS, LANES), lambda b, j: (j, 0, 0)),
            pl.BlockSpec((None, FFT1_ROWS, LANES), lambda b, j: (j, 0, 0)),
        ],
        out_specs=pl.BlockSpec((None, 2, FFT_N2, s, FN_W), lambda b, j: (b, 0, 0, j, 0)),
        out_shape=jax.ShapeDtypeStruct((BATCH, 2, FFT_N2, FFT_N1, FN_W), BF16),
        compiler_params=pltpu.CompilerParams(
            dimension_semantics=("arbitrary", "arbitrary"),
            vmem_limit_bytes=40 * MIB),
        name="fft_stage1",
    )(x4, ln_g, ln_b, w_fn, b_fn, fbig, cwt, swt)


def _fft2_kernel(d_ref, f2_ref, o_ref):
    f2 = f2_ref[...]
    for kk in range(FFT2_KB):
        z = _dot(f2, jnp.concatenate([d_ref[0, kk], d_ref[1, kk]], axis=0))
        o_ref[0, kk] = z[:FFT_N1].astype(BF16)
        o_ref[1, kk] = z[FFT_N1:].astype(BF16)


def _fft2(dmat, f2):
    kb = FFT2_KB
    blk = (None, 2, kb, FFT_N1, FN_W)
    return pl.pallas_call(
        _fft2_kernel,
        grid=(BATCH, FFT_N2 // kb),
        in_specs=[
            pl.BlockSpec(blk, lambda b, j: (b, 0, j, 0, 0)),
            pl.BlockSpec((2 * FFT_N1, 2 * FFT_N1), lambda b, j: (0, 0)),
        ],
        out_specs=pl.BlockSpec(blk, lambda b, j: (b, 0, j, 0, 0)),
        out_shape=jax.ShapeDtypeStruct((BATCH, 2, FFT_N2, FFT_N1, FN_W), BF16),
        compiler_params=pltpu.CompilerParams(
            dimension_semantics=("arbitrary", "arbitrary")),
        name="fft_stage2",
    )(dmat, f2)


def _dft_tables(merge_tm):
    s = FFT1_S
    n2 = np.arange(FFT_N2, dtype=np.float64)
    n1 = np.arange(FFT_N1, dtype=np.float64)
    th = 2.0 * np.pi * np.outer(n2, n2) / FFT_N2
    f1 = np.stack([np.cos(th), -np.sin(th)]) / math.sqrt(SEQ)
    fbig = np.einsum("rkn,st->rksnt", f1, np.eye(s)).reshape(2 * FFT_N2 * s, FFT_N2 * s)
    tw = 2.0 * np.pi * np.outer(n2, n1) / SEQ
    tw = tw.reshape(FFT_N2, FFT_N1 // s, s).transpose(1, 0, 2).reshape(FFT_N1 // s, FFT_N2 * s)
    cwt = np.broadcast_to(np.cos(tw)[:, :, None], tw.shape + (LANES,))
    swt = np.broadcast_to(np.sin(tw)[:, :, None], tw.shape + (LANES,))
    th1 = 2.0 * np.pi * np.outer(n1, n1) / FFT_N1
    c1, s1 = np.cos(th1), np.sin(th1)
    f2 = np.block([[c1, s1], [-s1, c1]])
    cc = np.arange(FN_GW, dtype=np.float64)
    thc = 2.0 * np.pi * np.outer(cc, cc) / FN_GW
    ccs = np.concatenate([np.cos(thc), np.sin(thc)], axis=0) / math.sqrt(FN_GW)
    k1n = merge_tm // FFT_N2
    r = np.arange(merge_tm)
    perm = np.zeros((merge_tm, merge_tm))
    perm[r, (r % FFT_N2) * k1n + r // FFT_N2] = 1.0
    as32 = lambda a: jnp.asarray(np.ascontiguousarray(a), dtype=F32)
    return (as32(fbig).astype(BF16), as32(cwt), as32(swt), as32(f2).astype(BF16),
            as32(ccs).astype(BF16), as32(perm).astype(BF16))


def _memkv_kernel(m_ref, g_ref, b_ref, w_ref, o_ref):
    mn = _ln(m_ref[...], g_ref[...], b_ref[...]).astype(BF16)
    o_ref[...] = _dot(mn, w_ref[...]).astype(BF16)


def _memkv(mem2, g, b, w):
    return pl.pallas_call(
        _memkv_kernel,
        grid=(BATCH,),
        in_specs=[
            pl.BlockSpec((MEM_LEN, D), lambda i: (i, 0)),
            pl.BlockSpec((1, D), lambda i: (0, 0)),
            pl.BlockSpec((1, D), lambda i: (0, 0)),
            pl.BlockSpec((D, 2 * MQ_W), lambda i: (0, 0)),
        ],
        out_specs=pl.BlockSpec((MEM_LEN, 2 * MQ_W), lambda i: (i, 0)),
        out_shape=jax.ShapeDtypeStruct((BATCH * MEM_LEN, 2 * MQ_W), BF16),
        compiler_params=pltpu.CompilerParams(dimension_semantics=("arbitrary",)),
        name="mem_kv",
    )(mem2, g, b, w)


MERGE_TM = 512
MERGE_K1 = MERGE_TM // FFT_N2


def _pack_bf16_pair(v):
    n = v.shape[1] // 2
    bits = lax.bitcast_convert_type(v.astype(BF16).astype(F32), U32)
    return (bits[:, n:] & jnp.uint32(0xFFFF0000)) | (bits[:, :n] >> 16)


def _unpack_bf16_pair(p):
    lo = lax.bitcast_convert_type(p << 16, F32)
    hi = lax.bitcast_convert_type(p & jnp.uint32(0xFFFF0000), F32)
    return lo, hi


def _merge_kernel(x_ref, og_ref, zr_ref, zi_ref, mq_ref, gt_ref, kv_ref,
                  lng_ref, lnb_ref, wg_ref, ccs_ref, perm_ref, wf_ref, wm_ref, wo_ref, bo_ref,
                  l1g_ref, l1b_ref, wr2_ref, wrh_ref, br_ref,
                  h1_ref, h1p_ref, eidx_ref, topw_ref):
    tm = MERGE_TM
    y_gla = _dot(og_ref[...], wg_ref[...])

    zr = zr_ref[...].reshape(tm, FN_W)
    zi = zi_ref[...].reshape(tm, FN_W)
    ys = []
    for g in range(FN_G):
        sl = slice(g * FN_GW, (g + 1) * FN_GW)
        ys.append(_dot(jnp.concatenate([zr[:, sl], zi[:, sl]], axis=1), ccs_ref[...]))
    yp = jnp.concatenate(ys, axis=1).astype(BF16)
    y_fn = _dot(_dot(perm_ref[...], yp).astype(BF16), wf_ref[...])

    oms = []
    for hd in range(MEM_H):
        sl = slice(hd * MEM_HD, (hd + 1) * MEM_HD)
        s = _dot_nt(mq_ref[:, sl], kv_ref[:, sl]) * (MEM_HD ** -0.5)
        s = s - jnp.max(s, axis=-1, keepdims=True)
        p = jnp.exp(s)
        p = p / jnp.sum(p, axis=-1, keepdims=True)
        oms.append(_dot(p.astype(BF16), kv_ref[:, MQ_W + hd * MEM_HD:MQ_W + (hd + 1) * MEM_HD]))
    y_mem = _dot(jnp.concatenate(oms, axis=1).astype(BF16), wm_ref[...])

    merged = (jax.nn.sigmoid(gt_ref[:, 0:D].astype(F32)) * y_gla
              + jax.nn.sigmoid(gt_ref[:, D:2 * D].astype(F32)) * y_fn
              + jax.nn.sigmoid(gt_ref[:, 2 * D:3 * D].astype(F32)) * y_mem)
    mix = _dot(merged.astype(BF16), wo_ref[...]) + bo_ref[...]
    h = _ln(x_ref[...], lng_ref[...], lnb_ref[...])
    h1 = _ln(DN_ALPHA * h + mix, l1g_ref[...], l1b_ref[...])
    h1_ref[...] = h1
    h1p_ref[...] = _pack_bf16_pair(h1)

    h_hi, h_lo = _split_bf16(h1)
    d2 = _dot(h_hi, wr2_ref[...])
    logits = d2[:, :LANES] + d2[:, LANES:] + _dot(h_lo, wrh_ref[...]) + br_ref[...]
    lane = lax.broadcasted_iota(I32, (tm, LANES), 1)
    l = logits
    vals, idxs = [], []
    for _ in range(TOP_K):
        m = jnp.max(l, axis=-1, keepdims=True)
        idx = jnp.min(jnp.where(l == m, lane, LANES), axis=-1, keepdims=True)
        vals.append(m)
        idxs.append(idx)
        l = jnp.where(lane == idx, -jnp.inf, l)
    es = [jnp.exp(v - vals[0]) for v in vals]
    den = es[0] + es[1] + es[2] + es[3]
    eo = jnp.zeros((tm, LANES), I32)
    wo = jnp.zeros((tm, LANES), F32)
    for k in range(TOP_K):
        eo = jnp.where(lane == k, idxs[k], eo)
        wo = jnp.where(lane == k, es[k] / den, wo)
    eidx_ref[...] = eo
    topw_ref[...] = wo


def _merge(x2, og, z, mq, gates, kv, lng, lnb, wg, ccs, perm, wf, wm, wo, bo, l1g, l1b,
           wr2, wrh, br):
    tm = MERGE_TM
    per_b = SEQ // tm
    row = lambda i: (i, 0)
    const = lambda i: (0, 0)
    zblk = (None, None, FFT_N2, MERGE_K1, FN_W)
    outs = (
        jax.ShapeDtypeStruct((T, D), F32),
        jax.ShapeDtypeStruct((T, D // 2), U32),
        jax.ShapeDtypeStruct((T, LANES), I32),
        jax.ShapeDtypeStruct((T, LANES), F32),
    )
    return pl.pallas_call(
        _merge_kernel,
        grid=(T // tm,),
        in_specs=[
            pl.BlockSpec((tm, D), row),
            pl.BlockSpec((tm, D), row),
            pl.BlockSpec(zblk, lambda i: (i // per_b, 0, 0, i % per_b, 0)),
            pl.BlockSpec(zblk, lambda i: (i // per_b, 1, 0, i % per_b, 0)),
            pl.BlockSpec((tm, MQ_W), row),
            pl.BlockSpec((tm, 3 * D), lambda i: (i, 1)),
            pl.BlockSpec((MEM_LEN, 2 * MQ_W), lambda i: (i // per_b, 0)),
            pl.BlockSpec((1, D), const), pl.BlockSpec((1, D), const),
            pl.BlockSpec((D, D), const),
            pl.BlockSpec((2 * FN_GW, FN_GW), const),
            pl.BlockSpec((tm, tm), const),
            pl.BlockSpec((FN_W, D), const),
            pl.BlockSpec((MQ_W, D), const),
            pl.BlockSpec((D, D), const),
            pl.BlockSpec((1, D), const),
            pl.BlockSpec((1, D), const), pl.BlockSpec((1, D), const),
            pl.BlockSpec((D, 2 * LANES), const),
            pl.BlockSpec((D, LANES), const),
            pl.BlockSpec((1, LANES), const),
        ],
        out_specs=[
            pl.BlockSpec((tm, D), row),
            pl.BlockSpec((tm, D // 2), row),
            pl.BlockSpec((tm, LANES), row),
            pl.BlockSpec((tm, LANES), row),
        ],
        out_shape=outs,
        compiler_params=pltpu.CompilerParams(
            dimension_semantics=("arbitrary",),
            vmem_limit_bytes=58 * MIB),
        name="merge_ln1_router",
    )(x2, og, z, z, mq, gates, kv, lng, lnb, wg, ccs, perm, wf, wm, wo, bo, l1g, l1b,
      wr2, wrh, br)


PLAN_TP = 1024


def _plan_kernel(e_ref, dest_ref, cnt_out_ref, cnt_ref, off_ref):
    p = pl.program_id(0)
    i = pl.program_id(1)
    tp = PLAN_TP
    lane = lax.broadcasted_iota(I32, (tp, LANES), 1)
    e = e_ref[...]
    onehots = [lane == e[:, k:k + 1] for k in range(TOP_K)]
    mf = jnp.zeros((tp, LANES), F32)
    for oh in onehots:
        mf = mf + jnp.where(oh, 1.0, 0.0)
    colsum = jnp.sum(mf, axis=0, keepdims=True)

    @pl.when(jnp.logical_and(p == 0, i == 0))
    def _():
        cnt_ref[...] = jnp.zeros_like(cnt_ref)

    @pl.when(jnp.logical_and(p == 1, i == 0))
    def _():
        tot = cnt_ref[...]
        cnt_out_ref[...] = jnp.broadcast_to(tot, cnt_out_ref.shape)
        padded = jnp.floor((tot + (MOE_BM - 1)) * (1.0 / MOE_BM)) * MOE_BM
        lane1 = lax.broadcasted_iota(I32, (1, LANES), 1)
        inc = padded
        for s in (1, 2, 4, 8, 16, 32, 64):
            inc = inc + jnp.where(lane1 >= s, pltpu.roll(inc, s, 1), 0.0)
        off_ref[...] = inc - padded
        cnt_ref[...] = jnp.zeros_like(cnt_ref)

    @pl.when(p == 0)
    def _():
        dest_ref[...] = jnp.zeros_like(dest_ref)

    @pl.when(p == 1)
    def _():
        ri = lax.broadcasted_iota(I32, (tp, tp), 0)
        ci = lax.broadcasted_iota(I32, (tp, tp), 1)
        ltri = jnp.where(ri > ci, 1.0, 0.0).astype(BF16)
        rank = _dot(ltri, mf.astype(BF16)) + cnt_ref[...] + off_ref[...]
        out = jnp.zeros((tp, LANES), I32)
        for k in range(TOP_K):
            dk = jnp.sum(jnp.where(onehots[k], rank, 0.0), axis=-1, keepdims=True)
            out = jnp.where(lane == k, dk.astype(I32), out)
        dest_ref[...] = out

    cnt_ref[...] += colsum


def _plan(eidx):
    tp = PLAN_TP
    return pl.pallas_call(
        _plan_kernel,
        grid=(2, T // tp),
        in_specs=[pl.BlockSpec((tp, LANES), lambda p, i: (i, 0))],
        out_specs=[
            pl.BlockSpec((tp, LANES), lambda p, i: (i * p, 0)),
            pl.BlockSpec((8, LANES), lambda p, i: (0, 0)),
        ],
        out_shape=(jax.ShapeDtypeStruct((T, LANES), I32),
                   jax.ShapeDtypeStruct((8, LANES), F32)),
        scratch_shapes=[pltpu.VMEM((1, LANES), F32), pltpu.VMEM((1, LANES), F32)],
        compiler_params=pltpu.CompilerParams(
            dimension_semantics=("arbitrary", "arbitrary")),
        name="route_plan",
    )(eidx)


MOE_BM = 512
MOE_NW = A_ROWS // MOE_BM + N_EXP
XS_ROWS = MOE_NW * MOE_BM


def _expert_kernel(we_ref, wb_ref, wv_ref, wfe_ref, wsl_ref, wnx_ref,
                   x_ref, wgu_hbm, bgu_ref, wdn_hbm, bdn_ref, o_ref,
                   wgu_f32, wdn_f32, wgu_bf, wdn_bf, sems):
    w = pl.program_id(0)
    e = we_ref[w]

    def weight_copies(expert, slot):
        return (pltpu.make_async_copy(wgu_hbm.at[expert], wgu_f32.at[slot], sems.at[slot, 0]),
                pltpu.make_async_copy(wdn_hbm.at[expert], wdn_f32.at[slot], sems.at[slot, 1]))

    @pl.when(w == 0)
    def _():
        for cp_ in weight_copies(e, 0):
            cp_.start()

    @pl.when(wfe_ref[w] == 1)
    def _():
        slot = wsl_ref[w]
        for cp_ in weight_copies(e, slot):
            cp_.wait()
        wgu_bf[...] = wgu_f32[slot].astype(BF16)
        wdn_bf[...] = wdn_f32[slot].astype(BF16)
        nxt = wnx_ref[w]

        @pl.when(nxt >= 0)
        def _():
            for cp_ in weight_copies(nxt, 1 - slot):
                cp_.start()

    @pl.when(wv_ref[w] == 1)
    def _():
        xlo, xhi = _unpack_bf16_pair(x_ref[...])
        half = D // 2
        gu = (_dot(xlo.astype(BF16), wgu_bf[:half, :]) + _dot(xhi.astype(BF16), wgu_bf[half:, :])
              + bgu_ref[pl.ds(e, 1), :])
        gate = jnp.minimum(gu[:, :D_FF], SW_LIMIT)
        up = jnp.clip(gu[:, D_FF:], -SW_LIMIT, SW_LIMIT)
        act = (up + 1.0) * (gate * jax.nn.sigmoid(SW_ALPHA * gate))
        out = _dot(act.astype(BF16), wdn_bf[...]) + bdn_ref[pl.ds(e, 1), :]
        o_ref[...] = _pack_bf16_pair(out)


def _experts(meta, xs, w_gu, b_gu, w_down, b_down):
    return pl.pallas_call(
        _expert_kernel,
        grid_spec=pltpu.PrefetchScalarGridSpec(
            num_scalar_prefetch=len(meta),
            grid=(MOE_NW,),
            in_specs=[
                pl.BlockSpec((MOE_BM, D // 2), lambda w, we, wb, *_: (wb[w], 0)),
                pl.BlockSpec(memory_space=pl.ANY),
                pl.BlockSpec((N_EXP, 2 * D_FF), lambda w, *_: (0, 0)),
                pl.BlockSpec(memory_space=pl.ANY),
                pl.BlockSpec((N_EXP, D), lambda w, *_: (0, 0)),
            ],
            out_specs=pl.BlockSpec((MOE_BM, D // 2), lambda w, we, wb, *_: (wb[w], 0)),
            scratch_shapes=[
                pltpu.VMEM((2, D, 2 * D_FF), F32),
                pltpu.VMEM((2, D_FF, D), F32),
                pltpu.VMEM((D, 2 * D_FF), BF16),
                pltpu.VMEM((D_FF, D), BF16),
                pltpu.SemaphoreType.DMA((2, 2)),
            ],
        ),
        out_shape=jax.ShapeDtypeStruct((XS_ROWS, D // 2), U32),
        compiler_params=pltpu.CompilerParams(
            dimension_semantics=("arbitrary",),
            vmem_limit_bytes=56 * MIB),
        name="moe_experts",
    )(*meta, xs, w_gu, b_gu, w_down, b_down)


def _work_items(counts):
    n_e = (counts + MOE_BM - 1) // MOE_BM
    item_end = jnp.cumsum(n_e)
    total = item_end[-1]
    w = jnp.arange(MOE_NW, dtype=I32)
    valid = w < total
    wc = jnp.minimum(w, total - 1)
    e_w = jnp.sum((item_end[None, :] <= wc[:, None]).astype(I32), axis=1)
    e_w = jnp.minimum(e_w, N_EXP - 1)
    prev_e = jnp.concatenate([jnp.full((1,), -1, I32), e_w[:-1]])
    fe = (e_w != prev_e).astype(I32)
    slot = (jnp.cumsum(fe) - 1) % 2
    first_at = jnp.where(fe == 1, w, MOE_NW)
    next_first = jnp.concatenate([lax.cummin(first_at, reverse=True)[1:],
                                  jnp.full((1,), MOE_NW, I32)])
    nxt = jnp.where(next_first < MOE_NW, e_w[jnp.minimum(next_first, MOE_NW - 1)], -1)
    return tuple(a.astype(I32) for a in (e_w, wc, valid, fe, slot, nxt))


COMB_TM = 256
SC_CORES = 2
SC_SUBCORES = 16
SC_WORKERS = SC_CORES * SC_SUBCORES
SC_CH = 64
SC_ROWS_PER_W = A_ROWS // SC_WORKERS
SC_NCH = SC_ROWS_PER_W // SC_CH


def _sc_gather(table, idx3):
    mesh = plsc.VectorSubcoreMesh(core_axis_name="c", subcore_axis_name="s")

    @functools.partial(
        pl.kernel, mesh=mesh,
        out_type=jax.ShapeDtypeStruct((A_ROWS, D // 2), U32),
        scratch_types=[
            pltpu.VMEM((SC_NCH, SC_CH), I32),
            pltpu.VMEM((2, SC_CH, D // 2), U32),
            pltpu.SemaphoreType.DMA((2,)),
            pltpu.SemaphoreType.DMA((2,)),
        ],
    )
    def k(table_hbm, idx_hbm, out_hbm, idx_v, rows_v, gsem, psem):
        wid = lax.axis_index("s") * SC_CORES + lax.axis_index("c")
        base = wid * SC_ROWS_PER_W
        pltpu.sync_copy(idx_hbm.at[wid], idx_v)

        def gather(j, b):
            return pltpu.make_async_copy(table_hbm.at[idx_v.at[j]], rows_v.at[b], gsem.at[b])

        def put(j, b):
            return pltpu.make_async_copy(rows_v.at[b], out_hbm.at[pl.ds(base + j * SC_CH, SC_CH)],
                                         psem.at[b])

        gather(0, 0).start()

        @pl.loop(0, SC_NCH, step=2)
        def _(j0):
            for b in range(2):
                j = j0 + b

                @pl.when(j + 1 < SC_NCH)
                def _():
                    @pl.when(j >= 1)
                    def _():
                        put(j - 1, 1 - b).wait()
                    gather(j + 1, 1 - b).start()

                gather(j, b).wait()
                put(j, b).start()

        put(SC_NCH - 2, 0).wait()
        put(SC_NCH - 1, 1).wait()

    return k(table, idx3)


SCD_TOK_PER_W = T // SC_WORKERS
SCD_NCH = SCD_TOK_PER_W // SC_CH


def _sc_dispatch(h1p, idx4):
    mesh = plsc.VectorSubcoreMesh(core_axis_name="c", subcore_axis_name="s")

    @functools.partial(
        pl.kernel, mesh=mesh,
        out_type=jax.ShapeDtypeStruct((XS_ROWS, D // 2), U32),
        scratch_types=[
            pltpu.VMEM((SCD_NCH * TOP_K, SC_CH), I32),
            pltpu.VMEM((2, SC_CH, D // 2), U32),
            pltpu.SemaphoreType.DMA((2,)),
            pltpu.SemaphoreType.DMA((2,)),
        ],
    )
    def k(h_hbm, idx_hbm, xs_hbm, idx_v, rows_v, gsem, psem):
        wid = lax.axis_index("s") * SC_CORES + lax.axis_index("c")
        base = wid * SCD_TOK_PER_W
        pltpu.sync_copy(idx_hbm.at[wid], idx_v)

        def get(c, b):
            return pltpu.make_async_copy(h_hbm.at[pl.ds(base + c * SC_CH, SC_CH)], rows_v.at[b],
                                         gsem.at[b])

        def puts(c, b):
            return [pltpu.make_async_copy(rows_v.at[b], xs_hbm.at[idx_v.at[c * TOP_K + kk]],
                                          psem.at[b]) for kk in range(TOP_K)]

        get(0, 0).start()

        @pl.loop(0, SCD_NCH, step=2)
        def _(c0):
            for b in range(2):
                c = c0 + b

                @pl.when(c + 1 < SCD_NCH)
                def _():
                    @pl.when(c >= 1)
                    def _():
                        for cp_ in puts(c - 1, 1 - b):
                            cp_.wait()
                    get(c + 1, 1 - b).start()

                get(c, b).wait()
                for cp_ in puts(c, b):
                    cp_.start()

        for cp_ in puts(SCD_NCH - 2, 0) + puts(SCD_NCH - 1, 1):
            cp_.wait()

    return k(h1p, idx4)


def _combine_dense_kernel(g_ref, h1_ref, tw_ref, lg_ref, lb_ref, o_ref):
    tw = tw_ref[...]
    ylo = jnp.zeros((COMB_TM, D // 2), F32)
    yhi = jnp.zeros((COMB_TM, D // 2), F32)
    for k in range(TOP_K):
        lo, hi = _unpack_bf16_pair(g_ref[k])
        wk = tw[:, k:k + 1]
        ylo = ylo + lo * wk
        yhi = yhi + hi * wk
    ff = jnp.concatenate([ylo, yhi], axis=1)
    o_ref[...] = _ln(DN_ALPHA * h1_ref[...] + ff, lg_ref[...], lb_ref[...])


def _combine_dense(g4, h1, topw, g, b):
    tm = COMB_TM
    return pl.pallas_call(
        _combine_dense_kernel,
        grid=(T // tm,),
        in_specs=[
            pl.BlockSpec((TOP_K, tm, D // 2), lambda i: (0, i, 0)),
            pl.BlockSpec((tm, D), lambda i: (i, 0)),
            pl.BlockSpec((tm, LANES), lambda i: (i, 0)),
            pl.BlockSpec((1, D), lambda i: (0, 0)),
            pl.BlockSpec((1, D), lambda i: (0, 0)),
        ],
        out_specs=pl.BlockSpec((tm, D), lambda i: (i, 0)),
        out_shape=jax.ShapeDtypeStruct((T, D), F32),
        compiler_params=pltpu.CompilerParams(dimension_semantics=("arbitrary",)),
        name="moe_combine_dense_ln2",
    )(g4, h1, topw, g, b)


def _pad_cols(a, n):
    return jnp.pad(a, ((0, 0), (0, n - a.shape[1])))


def kernel(x, mem, ln_in_g, ln_in_b, ln_mem_g, ln_mem_b, w_in, b_in, w_decay_f, b_decay_f,
           w_decay_b, b_decay_b, gla_norm_g, w_br_gla, w_br_fnet, w_br_mem, w_mem_kv, w_out,
           b_out, ln1_g, ln1_b, w_router, b_router, w_gu, b_gu, w_down, b_down, ln2_g, ln2_b):
    assert x.shape == (BATCH, SEQ, D) and w_in.shape[0] == 1
    row = lambda a: a.reshape(1, -1)
    x2 = x.reshape(T, D)
    w_in0, b_in0 = w_in[0], b_in[0]
    c_lr, c_fn, c_mq, c_gt = 3072, 3072 + 2 * GLA_LR, 3104 + FN_W, 3104 + FN_W + MQ_W
    w_main = jnp.concatenate([w_in0[:, :c_lr], w_in0[:, c_gt:]], axis=1).astype(BF16)
    b_main = row(jnp.concatenate([b_in0[:c_lr], b_in0[c_gt:]]))
    w_lr = _pad_cols(w_in0[:, c_lr:c_fn], LANES).astype(BF16)
    b_lr = _pad_cols(row(b_in0[c_lr:c_fn]), LANES)
    w_mq = w_in0[:, c_mq:c_gt].astype(BF16)
    b_mq = row(b_in0[c_mq:c_gt])
    w_fn = w_in0[:, c_fn:c_mq].astype(BF16)
    b_fn = row(b_in0[c_fn:c_mq])
    lng, lnb = row(ln_in_g), row(ln_in_b)

    proj, mq, lr = _inproj(x2, lng, lnb, w_main, b_main, w_mq, b_mq, w_lr, b_lr)

    zpad = jnp.zeros((LANES - 2 * GLA_LR, GLA_H * GLA_DK), F32)
    zlr = jnp.zeros((GLA_LR, GLA_H * GLA_DK), F32)
    wdf = jnp.concatenate([w_decay_f[0], zlr, zpad], axis=0).reshape(LANES, GLA_H, GLA_DK)
    wdb = jnp.concatenate([zlr, w_decay_b[0], zpad], axis=0).reshape(LANES, GLA_H, GLA_DK)
    wd = jnp.concatenate([wdf, wdb], axis=2).reshape(LANES, GLA_H * 2 * GLA_DK)
    bd = jnp.concatenate([b_decay_f[0].reshape(GLA_H, GLA_DK),
                          b_decay_b[0].reshape(GLA_H, GLA_DK)], axis=1).reshape(1, -1)
    og = _gla(proj, lr, wd, bd, row(gla_norm_g[0]))

    fbig, cwt, swt, f2, ccs, perm = _dft_tables(MERGE_TM)
    x4 = x.reshape(BATCH, FFT_N2, FFT_N1, D)
    z = _fft2(_fft1(x4, lng, lnb, w_fn, b_fn, fbig, cwt, swt), f2)

    kv = _memkv(mem.reshape(BATCH * MEM_LEN, D), row(ln_mem_g), row(ln_mem_b),
                w_mem_kv[0].astype(BF16))

    w_r = _pad_cols(w_router[0], LANES)
    wr_hi = w_r.astype(BF16)
    wr_lo = (w_r - wr_hi.astype(F32)).astype(BF16)
    b_r = jnp.concatenate([row(b_router[0]),
                           jnp.full((1, LANES - N_EXP), NEG_BIG, F32)], axis=1)
    h1, h1p, eidx, topw = _merge(
        x2, og, z, mq, proj, kv, lng, lnb,
        w_br_gla[0].astype(BF16), ccs, perm, w_br_fnet[0].astype(BF16),
        w_br_mem[0].astype(BF16), w_out[0].astype(BF16), row(b_out[0]),
        row(ln1_g[0]), row(ln1_b[0]), jnp.concatenate([wr_hi, wr_lo], axis=1), wr_hi, b_r)

    dest, cnt = _plan(eidx)
    counts = cnt[0, :N_EXP].astype(I32)
    dest_k = dest[:, :TOP_K]
    idx4 = dest_k.reshape(SC_WORKERS, SCD_NCH, SC_CH, TOP_K).transpose(0, 1, 3, 2)
    xs = _sc_dispatch(h1p, idx4.reshape(SC_WORKERS, SCD_NCH * TOP_K, SC_CH))
    ys = _experts(_work_items(counts), xs, w_gu[0], b_gu[0], w_down[0], b_down[0])
    dest_t = dest_k.T.reshape(SC_WORKERS, SC_NCH, SC_CH)
    g4 = _sc_gather(ys, dest_t).reshape(TOP_K, T, D // 2)
    out = _combine_dense(g4, h1, topw, row(ln2_g[0]), row(ln2_b[0]))
    return out.reshape(BATCH, SEQ, D)
```

```python
import functools
import math

import numpy as np
import jax
import jax.numpy as jnp
from jax import lax
from jax.experimental import pallas as pl
from jax.experimental.pallas import tpu as pltpu
from jax.experimental.pallas import tpu_sc as plsc

F32 = jnp.float32
BF16 = jnp.bfloat16
I32 = jnp.int32
U32 = jnp.uint32

D = 1024
BATCH = 4
SEQ = 4096
T = BATCH * SEQ
GLA_H = 4
GLA_DK = 128
GLA_DV = 256
GLA_LR = 16
GLA_TAU = 16.0
GLA_C = 64
FN_G = 4
FN_GW = 128
FN_W = 512
MEM_LEN = 256
MEM_H = 4
MEM_HD = 128
MQ_W = 512
N_EXP = 32
TOP_K = 4
D_FF = 1024
SW_LIMIT = 7.0
SW_ALPHA = 1.702
LN_EPS = 1e-5
RMS_EPS = 1e-6
DN_ALPHA = 2.0 ** 0.25
A_ROWS = T * TOP_K

FFT_N1 = 128
FFT_N2 = 32

LANES = 128
NEG_BIG = -1e30
MIB = 1024 * 1024


def _ln(x, g, b):
    mu = jnp.mean(x, axis=-1, keepdims=True)
    xc = x - mu
    var = jnp.mean(xc * xc, axis=-1, keepdims=True)
    return xc * lax.rsqrt(var + LN_EPS) * g + b


def _dot(a, b):
    return jnp.dot(a, b, preferred_element_type=F32)


def _dot_nt(a, b):
    return lax.dot_general(a, b, (((1,), (1,)), ((), ())), preferred_element_type=F32)


def _dot_tn(a, b):
    return lax.dot_general(a, b, (((0,), (0,)), ((), ())), preferred_element_type=F32)


def _split_bf16(a):
    hi = a.astype(BF16)
    return hi, (a - hi.astype(F32)).astype(BF16)


INPROJ_TM = 1024
INPROJ_TN = 2048


PROJ_W = 6 * 1024


def _inproj_kernel(x_ref, g_ref, b_ref, w_ref, bias_ref, wmq_ref, bmq_ref, wlr_ref, blr_ref,
                   proj_ref, mq_ref, lr_ref, hb_ref):
    @pl.when(pl.program_id(1) == 0)
    def _():
        hb = _ln(x_ref[...], g_ref[...], b_ref[...]).astype(BF16)
        hb_ref[...] = hb
        lr_ref[...] = _dot(hb, wlr_ref[...]) + blr_ref[...]
        mq_ref[...] = (_dot(hb, wmq_ref[...]) + bmq_ref[...]).astype(BF16)

    proj_ref[...] = (_dot(hb_ref[...], w_ref[...]) + bias_ref[...]).astype(BF16)


def _inproj(x2, ln_g, ln_b, w_main, b_main, w_mq, b_mq, w_lr, b_lr):
    tm, tn = INPROJ_TM, INPROJ_TN
    nj = PROJ_W // tn
    row = lambda i, j: (i, 0)
    const = lambda i, j: (0, 0)
    outs = (
        jax.ShapeDtypeStruct((T, PROJ_W), BF16),
        jax.ShapeDtypeStruct((T, MQ_W), BF16),
        jax.ShapeDtypeStruct((T, LANES), F32),
    )
    return pl.pallas_call(
        _inproj_kernel,
        grid=(T // tm, nj),
        in_specs=[
            pl.BlockSpec((tm, D), row),
            pl.BlockSpec((1, D), const),
            pl.BlockSpec((1, D), const),
            pl.BlockSpec((D, tn), lambda i, j: (0, j)),
            pl.BlockSpec((1, tn), lambda i, j: (0, j)),
            pl.BlockSpec((D, MQ_W), const),
            pl.BlockSpec((1, MQ_W), const),
            pl.BlockSpec((D, LANES), const),
            pl.BlockSpec((1, LANES), const),
        ],
        out_specs=[
            pl.BlockSpec((tm, tn), lambda i, j: (i, j)),
            pl.BlockSpec((tm, MQ_W), row),
            pl.BlockSpec((tm, LANES), row),
        ],
        out_shape=outs,
        scratch_shapes=[pltpu.VMEM((tm, D), BF16)],
        compiler_params=pltpu.CompilerParams(
            dimension_semantics=("arbitrary", "arbitrary"),
            vmem_limit_bytes=48 * MIB),
        name="ln_inproj",
    )(x2, ln_g, ln_b, w_main, b_main, w_mq, b_mq, w_lr, b_lr)


GLA_BULK = 256
GLA_FIN = 512
GLA_NCH = SEQ // GLA_C
GLA_CPB = GLA_BULK // GLA_C


def _gla_kernel(q_ref, k_ref, v_ref, r_ref, lr_ref, wd_ref, bd_ref, g_ref, cs_ref, o_ref,
                acc_ref, qin_ref, kin_ref, kst_ref, dec_ref, u_ref, stf_ref, stb_ref):
    C = GLA_C
    G = GLA_BULK
    DK = GLA_DK
    NG = SEQ // G
    scale = DK ** -0.5
    ii = lax.broadcasted_iota(I32, (G, G), 0)
    jj = lax.broadcasted_iota(I32, (G, G), 1)
    same = (ii // C) == (jj // C)
    lower = jnp.logical_and(same, ii >= jj)
    upper = jnp.logical_and(same, ii <= jj)
    is_fwd = lax.broadcasted_iota(I32, (G, 2 * DK), 1) < DK
    chunk_of_row = lax.broadcasted_iota(I32, (G, DK), 0) // C

    def stage_a(gi):
        rows = pl.ds(pl.multiple_of(gi * G, G), G)
        z = _dot(lr_ref[rows, :], wd_ref[...]) + bd_ref[...]
        la = -(jnp.maximum(-z, 0.0) + jnp.log(1.0 + jnp.exp(-jnp.abs(z)))) * (1.0 / GLA_TAU)
        la_hi, la_lo = _split_bf16(la)
        pre2 = _dot(cs_ref[...], jnp.concatenate([la_hi, la_lo], axis=1))
        pre = pre2[:, :2 * DK] + pre2[:, 2 * DK:]
        blast = jnp.concatenate(
            [jnp.broadcast_to(pre[ci * C + C - 1:ci * C + C, :], (C, 2 * DK))
             for ci in range(GLA_CPB)], axis=0)
        b = jnp.where(is_fwd, pre, blast - pre + la)
        qf32 = q_ref[rows, :].astype(F32)
        kf32 = k_ref[rows, :].astype(F32)
        q2 = jnp.concatenate([qf32, qf32], axis=1)
        k2 = jnp.concatenate([kf32, kf32], axis=1)
        qin_ref[rows, :] = (q2 * (scale * jnp.exp(b))).astype(BF16)
        kin_ref[rows, :] = (k2 * jnp.exp(-b)).astype(BF16)
        kst_ref[rows, :] = (k2 * jnp.exp(blast - b)).astype(BF16)
        dec = jnp.exp(blast)
        for ci in range(GLA_CPB):
            dec_ref[pl.ds(gi * GLA_CPB + ci, 1), :] = dec[ci * C:ci * C + 1, :]

    def stage_b(gi):
        rows = pl.ds(pl.multiple_of(gi * G, G), G)
        qi = qin_ref[rows, :]
        ki = kin_ref[rows, :]
        ks = kst_ref[rows, :]
        vb = v_ref[rows, :]
        att = (jnp.where(lower, _dot_nt(qi[:, :DK], ki[:, :DK]), 0.0)
               + jnp.where(upper, _dot_nt(qi[:, DK:], ki[:, DK:]), 0.0))
        acc_ref[rows, :] = _dot(att.astype(BF16), vb)
        ksb = jnp.concatenate(
            [jnp.where(chunk_of_row == ci, ks[:, d * DK:(d + 1) * DK], jnp.zeros((G, DK), BF16))
             for d in range(2) for ci in range(GLA_CPB)], axis=1)
        u = _dot_tn(vb, ksb)
        for d in range(2):
            for ci in range(GLA_CPB):
                col = (d * GLA_CPB + ci) * DK
                u_ref[d, gi * GLA_CPB + ci] = u[:, col:col + DK]

    stage_a(0)

    def bulk(gi, carry):
        stage_b(gi - 1)
        stage_a(gi)
        return carry

    lax.fori_loop(1, NG, bulk, 0)
    stage_b(NG - 1)

    stf_ref[...] = jnp.zeros_like(stf_ref)
    stb_ref[...] = jnp.zeros_like(stb_ref)

    def one(n, d, st_ref):
        lanes = slice(d * DK, (d + 1) * DK)
        rows = pl.ds(pl.multiple_of(n * C, C), C)
        st = st_ref[...]
        acc_ref[rows, :] += _dot_nt(qin_ref[rows, lanes], st.astype(BF16))
        st_ref[...] = st * dec_ref[pl.ds(n, 1), :][:, lanes] + u_ref[d, n]

    def step(i, carry):
        one(i, 0, stf_ref)
        one(GLA_NCH - 1 - i, 1, stb_ref)
        return carry

    lax.fori_loop(0, GLA_NCH, step, 0, unroll=4)

    def fin(gi, carry):
        rows = pl.ds(pl.multiple_of(gi * GLA_FIN, GLA_FIN), GLA_FIN)
        o = acc_ref[rows, :]
        o = o * lax.rsqrt(jnp.mean(o * o, axis=-1, keepdims=True) + RMS_EPS) * g_ref[...]
        rg = r_ref[rows, :].astype(F32)
        o_ref[rows, :] = (o * (rg * jax.nn.sigmoid(rg))).astype(BF16)
        return carry

    lax.fori_loop(0, SEQ // GLA_FIN, fin, 0)


def _gla(proj, lr, wd, bd, g):
    i = np.arange(GLA_BULK)
    cs = ((i[:, None] // GLA_C) == (i[None, :] // GLA_C)) & (i[:, None] >= i[None, :])
    cs = jnp.asarray(cs, dtype=F32).astype(BF16)
    v_blk = 1024 // GLA_DV
    return pl.pallas_call(
        _gla_kernel,
        grid=(BATCH, GLA_H),
        in_specs=[
            pl.BlockSpec((SEQ, GLA_DK), lambda b, h: (b, h)),
            pl.BlockSpec((SEQ, GLA_DK), lambda b, h: (b, GLA_H + h)),
            pl.BlockSpec((SEQ, GLA_DV), lambda b, h: (b, v_blk + h)),
            pl.BlockSpec((SEQ, GLA_DV), lambda b, h: (b, 2 * v_blk + h)),
            pl.BlockSpec((SEQ, LANES), lambda b, h: (b, 0)),
            pl.BlockSpec((LANES, 2 * GLA_DK), lambda b, h: (0, h)),
            pl.BlockSpec((1, 2 * GLA_DK), lambda b, h: (0, h)),
            pl.BlockSpec((1, GLA_DV), lambda b, h: (0, 0)),
            pl.BlockSpec((GLA_BULK, GLA_BULK), lambda b, h: (0, 0)),
        ],
        out_specs=pl.BlockSpec((SEQ, GLA_DV), lambda b, h: (b, h)),
        out_shape=jax.ShapeDtypeStruct((T, GLA_H * GLA_DV), BF16),
        scratch_shapes=[
            pltpu.VMEM((SEQ, GLA_DV), F32),
            pltpu.VMEM((SEQ, 2 * GLA_DK), BF16),
            pltpu.VMEM((SEQ, 2 * GLA_DK), BF16),
            pltpu.VMEM((SEQ, 2 * GLA_DK), BF16),
            pltpu.VMEM((GLA_NCH, 2 * GLA_DK), F32),
            pltpu.VMEM((2, GLA_NCH, GLA_DV, GLA_DK), F32),
            pltpu.VMEM((GLA_DV, GLA_DK), F32),
            pltpu.VMEM((GLA_DV, GLA_DK), F32),
        ],
        compiler_params=pltpu.CompilerParams(
            dimension_semantics=("arbitrary", "arbitrary"),
            vmem_limit_bytes=58 * MIB),
        name="gla",
    )(proj, proj, proj, proj, lr, wd, bd, g, cs)


FFT1_S = 16
FFT1_ROWS = FFT_N2 * FFT1_S
FFT2_KB = 4


def _fft1_kernel(x_ref, g_ref, b_ref, w_ref, bias_ref, fbig_ref, cw_ref, sw_ref, o_ref):
    xv = x_ref[...].reshape(FFT1_ROWS, D)
    hb = _ln(xv, g_ref[...], b_ref[...]).astype(BF16)
    fn = (_dot(hb, w_ref[...]) + bias_ref[...]).astype(BF16)
    a = _dot(fbig_ref[...], fn)
    ar = a[:FFT1_ROWS]
    ai = a[FFT1_ROWS:]
    cw = jnp.concatenate([cw_ref[...]] * (FN_W // LANES), axis=1)
    sw = jnp.concatenate([sw_ref[...]] * (FN_W // LANES), axis=1)
    o_ref[0] = (ar * cw + ai * sw).reshape(FFT_N2, FFT1_S, FN_W).astype(BF16)
    o_ref[1] = (ai * cw - ar * sw).reshape(FFT_N2, FFT1_S, FN_W).astype(BF16)


def _fft1(x4, ln_g, ln_b, w_fn, b_fn, fbig, cwt, swt):
    s = FFT1_S
    const = lambda b, j: (0, 0)
    return pl.pallas_call(
        _fft1_kernel,
        grid=(BATCH, FFT_N1 // s),
        in_specs=[
            pl.BlockSpec((None, FFT_N2, s, D), lambda b, j: (b, 0, j, 0)),
            pl.BlockSpec((1, D), const),
            pl.BlockSpec((1, D), const),
            pl.BlockSpec((D, FN_W), const),
            pl.BlockSpec((1, FN_W), const),
            pl.BlockSpec((2 * FFT1_ROWS, FFT1_ROWS), const),
            pl.BlockSpec((None, FFT1_ROWS, LANES), lambda b, j: (j, 0, 0)),
            pl.BlockSpec((None, FFT1_ROWS, LANES), lambda b, j: (j, 0, 0)),
        ],
        out_specs=pl.BlockSpec((None, 2, FFT_N2, s, FN_W), lambda b, j: (b, 0, 0, j, 0)),
        out_shape=jax.ShapeDtypeStruct((BATCH, 2, FFT_N2, FFT_N1, FN_W), BF16),
        compiler_params=pltpu.CompilerParams(
            dimension_semantics=("arbitrary", "arbitrary"),
            vmem_limit_bytes=40 * MIB),
        name="fft_stage1",
    )(x4, ln_g, ln_b, w_fn, b_fn, fbig, cwt, swt)


def _fft2_kernel(d_ref, f2_ref, o_ref):
    f2 = f2_ref[...]
    for kk in range(FFT2_KB):
        z = _dot(f2, jnp.concatenate([d_ref[0, kk], d_ref[1, kk]], axis=0))
        o_ref[0, kk] = z[:FFT_N1].astype(BF16)
        o_ref[1, kk] = z[FFT_N1:].astype(BF16)


def _fft2(dmat, f2):
    kb = FFT2_KB
    blk = (None, 2, kb, FFT_N1, FN_W)
    return pl.pallas_call(
        _fft2_kernel,
        grid=(BATCH, FFT_N2 // kb),
        in_specs=[
            pl.BlockSpec(blk, lambda b, j: (b, 0, j, 0, 0)),
            pl.BlockSpec((2 * FFT_N1, 2 * FFT_N1), lambda b, j: (0, 0)),
        ],
        out_specs=pl.BlockSpec(blk, lambda b, j: (b, 0, j, 0, 0)),
        out_shape=jax.ShapeDtypeStruct((BATCH, 2, FFT_N2, FFT_N1, FN_W), BF16),
        compiler_params=pltpu.CompilerParams(
            dimension_semantics=("arbitrary", "arbitrary")),
        name="fft_stage2",
    )(dmat, f2)


def _dft_tables(merge_tm):
    s = FFT1_S
    n2 = np.arange(FFT_N2, dtype=np.float64)
    n1 = np.arange(FFT_N1, dtype=np.float64)
    th = 2.0 * np.pi * np.outer(n2, n2) / FFT_N2
    f1 = np.stack([np.cos(th), -np.sin(th)]) / math.sqrt(SEQ)
    fbig = np.einsum("rkn,st->rksnt", f1, np.eye(s)).reshape(2 * FFT_N2 * s, FFT_N2 * s)
    tw = 2.0 * np.pi * np.outer(n2, n1) / SEQ
    tw = tw.reshape(FFT_N2, FFT_N1 // s, s).transpose(1, 0, 2).reshape(FFT_N1 // s, FFT_N2 * s)
    cwt = np.broadcast_to(np.cos(tw)[:, :, None], tw.shape + (LANES,))
    swt = np.broadcast_to(np.sin(tw)[:, :, None], tw.shape + (LANES,))
    th1 = 2.0 * np.pi * np.outer(n1, n1) / FFT_N1
    c1, s1 = np.cos(th1), np.sin(th1)
    f2 = np.block([[c1, s1], [-s1, c1]])
    cc = np.arange(FN_GW, dtype=np.float64)
    thc = 2.0 * np.pi * np.outer(cc, cc) / FN_GW
    ccs = np.concatenate([np.cos(thc), np.sin(thc)], axis=0) / math.sqrt(FN_GW)
    k1n = merge_tm // FFT_N2
    r = np.arange(merge_tm)
    perm = np.zeros((merge_tm, merge_tm))
    perm[r, (r % FFT_N2) * k1n + r // FFT_N2] = 1.0
    as32 = lambda a: jnp.asarray(np.ascontiguousarray(a), dtype=F32)
    return (as32(fbig).astype(BF16), as32(cwt), as32(swt), as32(f2).astype(BF16),
            as32(ccs).astype(BF16), as32(perm).astype(BF16))


def _memkv_kernel(m_ref, g_ref, b_ref, w_ref, o_ref):
    mn = _ln(m_ref[...], g_ref[...], b_ref[...]).astype(BF16)
    o_ref[...] = _dot(mn, w_ref[...]).astype(BF16)


def _memkv(mem2, g, b, w):
    return pl.pallas_call(
        _memkv_kernel,
        grid=(BATCH,),
        in_specs=[
            pl.BlockSpec((MEM_LEN, D), lambda i: (i, 0)),
            pl.BlockSpec((1, D), lambda i: (0, 0)),
            pl.BlockSpec((1, D), lambda i: (0, 0)),
            pl.BlockSpec((D, 2 * MQ_W), lambda i: (0, 0)),
        ],
        out_specs=pl.BlockSpec((MEM_LEN, 2 * MQ_W), lambda i: (i, 0)),
        out_shape=jax.ShapeDtypeStruct((BATCH * MEM_LEN, 2 * MQ_W), BF16),
        compiler_params=pltpu.CompilerParams(dimension_semantics=("arbitrary",)),
        name="mem_kv",
    )(mem2, g, b, w)


MERGE_TM = 512
MERGE_K1 = MERGE_TM // FFT_N2


def _pack_bf16_pair(v):
    n = v.shape[1] // 2
    bits = lax.bitcast_convert_type(v.astype(BF16).astype(F32), U32)
    return (bits[:, n:] & jnp.uint32(0xFFFF0000)) | (bits[:, :n] >> 16)


def _unpack_bf16_pair(p):
    lo = lax.bitcast_convert_type(p << 16, F32)
    hi = lax.bitcast_convert_type(p & jnp.uint32(0xFFFF0000), F32)
    return lo, hi


def _merge_kernel(x_ref, og_ref, zr_ref, zi_ref, mq_ref, gt_ref, kv_ref,
                  lng_ref, lnb_ref, wg_ref, ccs_ref, perm_ref, wf_ref, wm_ref, wo_ref, bo_ref,
                  l1g_ref, l1b_ref, wr2_ref, wrh_ref, br_ref,
                  h1_ref, h1p_ref, eidx_ref, topw_ref):
    tm = MERGE_TM
    y_gla = _dot(og_ref[...], wg_ref[...])

    zr = zr_ref[...].reshape(tm, FN_W)
    zi = zi_ref[...].reshape(tm, FN_W)
    ys = []
    for g in range(FN_G):
        sl = slice(g * FN_GW, (g + 1) * FN_GW)
        ys.append(_dot(jnp.concatenate([zr[:, sl], zi[:, sl]], axis=1), ccs_ref[...]))
    yp = jnp.concatenate(ys, axis=1).astype(BF16)
    y_fn = _dot(_dot(perm_ref[...], yp).astype(BF16), wf_ref[...])

    oms = []
    for hd in range(MEM_H):
        sl = slice(hd * MEM_HD, (hd + 1) * MEM_HD)
        s = _dot_nt(mq_ref[:, sl], kv_ref[:, sl]) * (MEM_HD ** -0.5)
        s = s - jnp.max(s, axis=-1, keepdims=True)
        p = jnp.exp(s)
        p = p / jnp.sum(p, axis=-1, keepdims=True)
        oms.append(_dot(p.astype(BF16), kv_ref[:, MQ_W + hd * MEM_HD:MQ_W + (hd + 1) * MEM_HD]))
    y_mem = _dot(jnp.concatenate(oms, axis=1).astype(BF16), wm_ref[...])

    merged = (jax.nn.sigmoid(gt_ref[:, 0:D].astype(F32)) * y_gla
              + jax.nn.sigmoid(gt_ref[:, D:2 * D].astype(F32)) * y_fn
              + jax.nn.sigmoid(gt_ref[:, 2 * D:3 * D].astype(F32)) * y_mem)
    mix = _dot(merged.astype(BF16), wo_ref[...]) + bo_ref[...]
    h = _ln(x_ref[...], lng_ref[...], lnb_ref[...])
    h1 = _ln(DN_ALPHA * h + mix, l1g_ref[...], l1b_ref[...])
    h1_ref[...] = h1
    h1p_ref[...] = _pack_bf16_pair(h1)

    h_hi, h_lo = _split_bf16(h1)
    d2 = _dot(h_hi, wr2_ref[...])
    logits = d2[:, :LANES] + d2[:, LANES:] + _dot(h_lo, wrh_ref[...]) + br_ref[...]
    lane = lax.broadcasted_iota(I32, (tm, LANES), 1)
    l = logits
    vals, idxs = [], []
    for _ in range(TOP_K):
        m = jnp.max(l, axis=-1, keepdims=True)
        idx = jnp.min(jnp.where(l == m, lane, LANES), axis=-1, keepdims=True)
        vals.append(m)
        idxs.append(idx)
        l = jnp.where(lane == idx, -jnp.inf, l)
    es = [jnp.exp(v - vals[0]) for v in vals]
    den = es[0] + es[1] + es[2] + es[3]
    eo = jnp.zeros((tm, LANES), I32)
    wo = jnp.zeros((tm, LANES), F32)
    for k in range(TOP_K):
        eo = jnp.where(lane == k, idxs[k], eo)
        wo = jnp.where(lane == k, es[k] / den, wo)
    eidx_ref[...] = eo
    topw_ref[...] = wo


def _merge(x2, og, z, mq, gates, kv, lng, lnb, wg, ccs, perm, wf, wm, wo, bo, l1g, l1b,
           wr2, wrh, br):
    tm = MERGE_TM
    per_b = SEQ // tm
    row = lambda i: (i, 0)
    const = lambda i: (0, 0)
    zblk = (None, None, FFT_N2, MERGE_K1, FN_W)
    outs = (
        jax.ShapeDtypeStruct((T, D), F32),
        jax.ShapeDtypeStruct((T, D // 2), U32),
        jax.ShapeDtypeStruct((T, LANES), I32),
        jax.ShapeDtypeStruct((T, LANES), F32),
    )
    return pl.pallas_call(
        _merge_kernel,
        grid=(T // tm,),
        in_specs=[
            pl.BlockSpec((tm, D), row),
            pl.BlockSpec((tm, D), row),
            pl.BlockSpec(zblk, lambda i: (i // per_b, 0, 0, i % per_b, 0)),
            pl.BlockSpec(zblk, lambda i: (i // per_b, 1, 0, i % per_b, 0)),
            pl.BlockSpec((tm, MQ_W), row),
            pl.BlockSpec((tm, 3 * D), lambda i: (i, 1)),
            pl.BlockSpec((MEM_LEN, 2 * MQ_W), lambda i: (i // per_b, 0)),
            pl.BlockSpec((1, D), const), pl.BlockSpec((1, D), const),
            pl.BlockSpec((D, D), const),
            pl.BlockSpec((2 * FN_GW, FN_GW), const),
            pl.BlockSpec((tm, tm), const),
            pl.BlockSpec((FN_W, D), const),
            pl.BlockSpec((MQ_W, D), const),
            pl.BlockSpec((D, D), const),
            pl.BlockSpec((1, D), const),
            pl.BlockSpec((1, D), const), pl.BlockSpec((1, D), const),
            pl.BlockSpec((D, 2 * LANES), const),
            pl.BlockSpec((D, LANES), const),
            pl.BlockSpec((1, LANES), const),
        ],
        out_specs=[
            pl.BlockSpec((tm, D), row),
            pl.BlockSpec((tm, D // 2), row),
            pl.BlockSpec((tm, LANES), row),
            pl.BlockSpec((tm, LANES), row),
        ],
        out_shape=outs,
        compiler_params=pltpu.CompilerParams(
            dimension_semantics=("arbitrary",),
            vmem_limit_bytes=58 * MIB),
        name="merge_ln1_router",
    )(x2, og, z, z, mq, gates, kv, lng, lnb, wg, ccs, perm, wf, wm, wo, bo, l1g, l1b,
      wr2, wrh, br)


PLAN_TP = 1024


def _plan_kernel(e_ref, dest_ref, cnt_out_ref, cnt_ref, off_ref):
    p = pl.program_id(0)
    i = pl.program_id(1)
    tp = PLAN_TP
    lane = lax.broadcasted_iota(I32, (tp, LANES), 1)
    e = e_ref[...]
    onehots = [lane == e[:, k:k + 1] for k in range(TOP_K)]
    mf = jnp.zeros((tp, LANES), F32)
    for oh in onehots:
        mf = mf + jnp.where(oh, 1.0, 0.0)
    colsum = jnp.sum(mf, axis=0, keepdims=True)

    @pl.when(jnp.logical_and(p == 0, i == 0))
    def _():
        cnt_ref[...] = jnp.zeros_like(cnt_ref)

    @pl.when(jnp.logical_and(p == 1, i == 0))
    def _():
        tot = cnt_ref[...]
        cnt_out_ref[...] = jnp.broadcast_to(tot, cnt_out_ref.shape)
        padded = jnp.floor((tot + (MOE_BM - 1)) * (1.0 / MOE_BM)) * MOE_BM
        lane1 = lax.broadcasted_iota(I32, (1, LANES), 1)
        inc = padded
        for s in (1, 2, 4, 8, 16, 32, 64):
            inc = inc + jnp.where(lane1 >= s, pltpu.roll(inc, s, 1), 0.0)
        off_ref[...] = inc - padded
        cnt_ref[...] = jnp.zeros_like(cnt_ref)

    @pl.when(p == 0)
    def _():
        dest_ref[...] = jnp.zeros_like(dest_ref)

    @pl.when(p == 1)
    def _():
        ri = lax.broadcasted_iota(I32, (tp, tp), 0)
        ci = lax.broadcasted_iota(I32, (tp, tp), 1)
        ltri = jnp.where(ri > ci, 1.0, 0.0).astype(BF16)
        rank = _dot(ltri, mf.astype(BF16)) + cnt_ref[...] + off_ref[...]
        out = jnp.zeros((tp, LANES), I32)
        for k in range(TOP_K):
            dk = jnp.sum(jnp.where(onehots[k], rank, 0.0), axis=-1, keepdims=True)
            out = jnp.where(lane == k, dk.astype(I32), out)
        dest_ref[...] = out

    cnt_ref[...] += colsum


def _plan(eidx):
    tp = PLAN_TP
    return pl.pallas_call(
        _plan_kernel,
        grid=(2, T // tp),
        in_specs=[pl.BlockSpec((tp, LANES), lambda p, i: (i, 0))],
        out_specs=[
            pl.BlockSpec((tp, LANES), lambda p, i: (i * p, 0)),
            pl.BlockSpec((8, LANES), lambda p, i: (0, 0)),
        ],
        out_shape=(jax.ShapeDtypeStruct((T, LANES), I32),
                   jax.ShapeDtypeStruct((8, LANES), F32)),
        scratch_shapes=[pltpu.VMEM((1, LANES), F32), pltpu.VMEM((1, LANES), F32)],
        compiler_params=pltpu.CompilerParams(
            dimension_semantics=("arbitrary", "arbitrary")),
        name="route_plan",
    )(eidx)


MOE_BM = 512
MOE_NW = A_ROWS // MOE_BM + N_EXP
XS_ROWS = MOE_NW * MOE_BM


def _expert_kernel(we_ref, wb_ref, wv_ref, wfe_ref, wsl_ref, wnx_ref,
                   x_ref, wgu_hbm, bgu_ref, wdn_hbm, bdn_ref, o_ref,
                   wgu_f32, wdn_f32, wgu_bf, wdn_bf, sems):
    w = pl.program_id(0)
    e = we_ref[w]

    def weight_copies(expert, slot):
        return (pltpu.make_async_copy(wgu_hbm.at[expert], wgu_f32.at[slot], sems.at[slot, 0]),
                pltpu.make_async_copy(wdn_hbm.at[expert], wdn_f32.at[slot], sems.at[slot, 1]))

    @pl.when(w == 0)
    def _():
        for cp_ in weight_copies(e, 0):
            cp_.start()

    @pl.when(wfe_ref[w] == 1)
    def _():
        slot = wsl_ref[w]
        for cp_ in weight_copies(e, slot):
            cp_.wait()
        wgu_bf[...] = wgu_f32[slot].astype(BF16)
        wdn_bf[...] = wdn_f32[slot].astype(BF16)
        nxt = wnx_ref[w]

        @pl.when(nxt >= 0)
        def _():
            for cp_ in weight_copies(nxt, 1 - slot):
                cp_.start()

    @pl.when(wv_ref[w] == 1)
    def _():
        xlo, xhi = _unpack_bf16_pair(x_ref[...])
        half = D // 2
        gu = (_dot(xlo.astype(BF16), wgu_bf[:half, :]) + _dot(xhi.astype(BF16), wgu_bf[half:, :])
              + bgu_ref[pl.ds(e, 1), :])
        gate = jnp.minimum(gu[:, :D_FF], SW_LIMIT)
        up = jnp.clip(gu[:, D_FF:], -SW_LIMIT, SW_LIMIT)
        act = (up + 1.0) * (gate * jax.nn.sigmoid(SW_ALPHA * gate))
        out = _dot(act.astype(BF16), wdn_bf[...]) + bdn_ref[pl.ds(e, 1), :]
        o_ref[...] = _pack_bf16_pair(out)


def _experts(meta, xs, w_gu, b_gu, w_down, b_down):
    return pl.pallas_call(
        _expert_kernel,
        grid_spec=pltpu.PrefetchScalarGridSpec(
            num_scalar_prefetch=len(meta),
            grid=(MOE_NW,),
            in_specs=[
                pl.BlockSpec((MOE_BM, D // 2), lambda w, we, wb, *_: (wb[w], 0)),
                pl.BlockSpec(memory_space=pl.ANY),
                pl.BlockSpec((N_EXP, 2 * D_FF), lambda w, *_: (0, 0)),
                pl.BlockSpec(memory_space=pl.ANY),
                pl.BlockSpec((N_EXP, D), lambda w, *_: (0, 0)),
            ],
            out_specs=pl.BlockSpec((MOE_BM, D // 2), lambda w, we, wb, *_: (wb[w], 0)),
            scratch_shapes=[
                pltpu.VMEM((2, D, 2 * D_FF), F32),
                pltpu.VMEM((2, D_FF, D), F32),
                pltpu.VMEM((D, 2 * D_FF), BF16),
                pltpu.VMEM((D_FF, D), BF16),
                pltpu.SemaphoreType.DMA((2, 2)),
            ],
        ),
        out_shape=jax.ShapeDtypeStruct((XS_ROWS, D // 2), U32),
        compiler_params=pltpu.CompilerParams(
            dimension_semantics=("arbitrary",),
            vmem_limit_bytes=56 * MIB),
        name="moe_experts",
    )(*meta, xs, w_gu, b_gu, w_down, b_down)


def _work_items(counts):
    n_e = (counts + MOE_BM - 1) // MOE_BM
    item_end = jnp.cumsum(n_e)
    total = item_end[-1]
    w = jnp.arange(MOE_NW, dtype=I32)
    valid = w < total
    wc = jnp.minimum(w, total - 1)
    e_w = jnp.sum((item_end[None, :] <= wc[:, None]).astype(I32), axis=1)
    e_w = jnp.minimum(e_w, N_EXP - 1)
    prev_e = jnp.concatenate([jnp.full((1,), -1, I32), e_w[:-1]])
    fe = (e_w != prev_e).astype(I32)
    slot = (jnp.cumsum(fe) - 1) % 2
    first_at = jnp.where(fe == 1, w, MOE_NW)
    next_first = jnp.concatenate([lax.cummin(first_at, reverse=True)[1:],
                                  jnp.full((1,), MOE_NW, I32)])
    nxt = jnp.where(next_first < MOE_NW, e_w[jnp.minimum(next_first, MOE_NW - 1)], -1)
    return tuple(a.astype(I32) for a in (e_w, wc, valid, fe, slot, nxt))


COMB_TM = 256
SC_CORES = 2
SC_SUBCORES = 16
SC_WORKERS = SC_CORES * SC_SUBCORES
SC_CH = 64
COMB_GROUPS = 4
COMB_TG = T // COMB_GROUPS
SC_ROWS_PER_W = COMB_TG * TOP_K // SC_WORKERS
SC_NCH = SC_ROWS_PER_W // SC_CH


def _sc_gather(table, idx3):
    mesh = plsc.VectorSubcoreMesh(core_axis_name="c", subcore_axis_name="s")

    @functools.partial(
        pl.kernel, mesh=mesh,
        out_type=jax.ShapeDtypeStruct((COMB_TG * TOP_K, D // 2), U32),
        scratch_types=[
            pltpu.VMEM((SC_NCH, SC_CH), I32),
            pltpu.VMEM((2, SC_CH, D // 2), U32),
            pltpu.SemaphoreType.DMA((2,)),
            pltpu.SemaphoreType.DMA((2,)),
        ],
    )
    def k(table_hbm, idx_hbm, out_hbm, idx_v, rows_v, gsem, psem):
        wid = lax.axis_index("s") * SC_CORES + lax.axis_index("c")
        base = wid * SC_ROWS_PER_W
        pltpu.sync_copy(idx_hbm.at[wid], idx_v)

        def gather(j, b):
            return pltpu.make_async_copy(table_hbm.at[idx_v.at[j]], rows_v.at[b], gsem.at[b])

        def put(j, b):
            return pltpu.make_async_copy(rows_v.at[b], out_hbm.at[pl.ds(base + j * SC_CH, SC_CH)],
                                         psem.at[b])

        gather(0, 0).start()

        @pl.loop(0, SC_NCH, step=2)
        def _(j0):
            for b in range(2):
                j = j0 + b

                @pl.when(j + 1 < SC_NCH)
                def _():
                    @pl.when(j >= 1)
                    def _():
                        put(j - 1, 1 - b).wait()
                    gather(j + 1, 1 - b).start()

                gather(j, b).wait()
                put(j, b).start()

        put(SC_NCH - 2, 0).wait()
        put(SC_NCH - 1, 1).wait()

    return k(table, idx3)


SCD_TOK_PER_W = T // SC_WORKERS
SCD_NCH = SCD_TOK_PER_W // SC_CH


def _sc_dispatch(h1p, idx4):
    mesh = plsc.VectorSubcoreMesh(core_axis_name="c", subcore_axis_name="s")

    @functools.partial(
        pl.kernel, mesh=mesh,
        out_type=jax.ShapeDtypeStruct((XS_ROWS, D // 2), U32),
        scratch_types=[
            pltpu.VMEM((SCD_NCH * TOP_K, SC_CH), I32),
            pltpu.VMEM((2, SC_CH, D // 2), U32),
            pltpu.SemaphoreType.DMA((2,)),
            pltpu.SemaphoreType.DMA((2,)),
        ],
    )
    def k(h_hbm, idx_hbm, xs_hbm, idx_v, rows_v, gsem, psem):
        wid = lax.axis_index("s") * SC_CORES + lax.axis_index("c")
        base = wid * SCD_TOK_PER_W
        pltpu.sync_copy(idx_hbm.at[wid], idx_v)

        def get(c, b):
            return pltpu.make_async_copy(h_hbm.at[pl.ds(base + c * SC_CH, SC_CH)], rows_v.at[b],
                                         gsem.at[b])

        def puts(c, b):
            return [pltpu.make_async_copy(rows_v.at[b], xs_hbm.at[idx_v.at[c * TOP_K + kk]],
                                          psem.at[b]) for kk in range(TOP_K)]

        get(0, 0).start()

        @pl.loop(0, SCD_NCH, step=2)
        def _(c0):
            for b in range(2):
                c = c0 + b

                @pl.when(c + 1 < SCD_NCH)
                def _():
                    @pl.when(c >= 1)
                    def _():
                        for cp_ in puts(c - 1, 1 - b):
                            cp_.wait()
                    get(c + 1, 1 - b).start()

                get(c, b).wait()
                for cp_ in puts(c, b):
                    cp_.start()

        for cp_ in puts(SCD_NCH - 2, 0) + puts(SCD_NCH - 1, 1):
            cp_.wait()

    return k(h1p, idx4)


def _combine_dense_kernel(g_ref, h1_ref, tw_ref, lg_ref, lb_ref, o_ref):
    tw = tw_ref[...]
    ylo = jnp.zeros((COMB_TM, D // 2), F32)
    yhi = jnp.zeros((COMB_TM, D // 2), F32)
    for k in range(TOP_K):
        lo, hi = _unpack_bf16_pair(g_ref[k])
        wk = tw[:, k:k + 1]
        ylo = ylo + lo * wk
        yhi = yhi + hi * wk
    ff = jnp.concatenate([ylo, yhi], axis=1)
    o_ref[...] = _ln(DN_ALPHA * h1_ref[...] + ff, lg_ref[...], lb_ref[...])


def _combine_dense(g4, h1, topw, g, b, group):
    tm = COMB_TM
    t0 = group * (COMB_TG // tm)
    return pl.pallas_call(
        _combine_dense_kernel,
        grid=(COMB_TG // tm,),
        in_specs=[
            pl.BlockSpec((TOP_K, tm, D // 2), lambda i: (0, i, 0)),
            pl.BlockSpec((tm, D), lambda i: (t0 + i, 0)),
            pl.BlockSpec((tm, LANES), lambda i: (t0 + i, 0)),
            pl.BlockSpec((1, D), lambda i: (0, 0)),
            pl.BlockSpec((1, D), lambda i: (0, 0)),
        ],
        out_specs=pl.BlockSpec((tm, D), lambda i: (t0 + i, 0)),
        out_shape=jax.ShapeDtypeStruct((T, D), F32),
        input_output_aliases={1: 0},
        compiler_params=pltpu.CompilerParams(dimension_semantics=("arbitrary",)),
        name="moe_combine_dense_ln2",
    )(g4, h1, topw, g, b)


def _pad_cols(a, n):
    return jnp.pad(a, ((0, 0), (0, n - a.shape[1])))


def kernel(x, mem, ln_in_g, ln_in_b, ln_mem_g, ln_mem_b, w_in, b_in, w_decay_f, b_decay_f,
           w_decay_b, b_decay_b, gla_norm_g, w_br_gla, w_br_fnet, w_br_mem, w_mem_kv, w_out,
           b_out, ln1_g, ln1_b, w_router, b_router, w_gu, b_gu, w_down, b_down, ln2_g, ln2_b):
    assert x.shape == (BATCH, SEQ, D) and w_in.shape[0] == 1
    row = lambda a: a.reshape(1, -1)
    x2 = x.reshape(T, D)
    w_in0, b_in0 = w_in[0], b_in[0]
    c_lr, c_fn, c_mq, c_gt = 3072, 3072 + 2 * GLA_LR, 3104 + FN_W, 3104 + FN_W + MQ_W
    w_main = jnp.concatenate([w_in0[:, :c_lr], w_in0[:, c_gt:]], axis=1).astype(BF16)
    b_main = row(jnp.concatenate([b_in0[:c_lr], b_in0[c_gt:]]))
    w_lr = _pad_cols(w_in0[:, c_lr:c_fn], LANES).astype(BF16)
    b_lr = _pad_cols(row(b_in0[c_lr:c_fn]), LANES)
    w_mq = w_in0[:, c_mq:c_gt].astype(BF16)
    b_mq = row(b_in0[c_mq:c_gt])
    w_fn = w_in0[:, c_fn:c_mq].astype(BF16)
    b_fn = row(b_in0[c_fn:c_mq])
    lng, lnb = row(ln_in_g), row(ln_in_b)

    proj, mq, lr = _inproj(x2, lng, lnb, w_main, b_main, w_mq, b_mq, w_lr, b_lr)

    zpad = jnp.zeros((LANES - 2 * GLA_LR, GLA_H * GLA_DK), F32)
    zlr = jnp.zeros((GLA_LR, GLA_H * GLA_DK), F32)
    wdf = jnp.concatenate([w_decay_f[0], zlr, zpad], axis=0).reshape(LANES, GLA_H, GLA_DK)
    wdb = jnp.concatenate([zlr, w_decay_b[0], zpad], axis=0).reshape(LANES, GLA_H, GLA_DK)
    wd = jnp.concatenate([wdf, wdb], axis=2).reshape(LANES, GLA_H * 2 * GLA_DK)
    bd = jnp.concatenate([b_decay_f[0].reshape(GLA_H, GLA_DK),
                          b_decay_b[0].reshape(GLA_H, GLA_DK)], axis=1).reshape(1, -1)
    og = _gla(proj, lr, wd, bd, row(gla_norm_g[0]))

    fbig, cwt, swt, f2, ccs, perm = _dft_tables(MERGE_TM)
    x4 = x.reshape(BATCH, FFT_N2, FFT_N1, D)
    z = _fft2(_fft1(x4, lng, lnb, w_fn, b_fn, fbig, cwt, swt), f2)

    kv = _memkv(mem.reshape(BATCH * MEM_LEN, D), row(ln_mem_g), row(ln_mem_b),
                w_mem_kv[0].astype(BF16))

    w_r = _pad_cols(w_router[0], LANES)
    wr_hi = w_r.astype(BF16)
    wr_lo = (w_r - wr_hi.astype(F32)).astype(BF16)
    b_r = jnp.concatenate([row(b_router[0]),
                           jnp.full((1, LANES - N_EXP), NEG_BIG, F32)], axis=1)
    h1, h1p, eidx, topw = _merge(
        x2, og, z, mq, proj, kv, lng, lnb,
        w_br_gla[0].astype(BF16), ccs, perm, w_br_fnet[0].astype(BF16),
        w_br_mem[0].astype(BF16), w_out[0].astype(BF16), row(b_out[0]),
        row(ln1_g[0]), row(ln1_b[0]), jnp.concatenate([wr_hi, wr_lo], axis=1), wr_hi, b_r)

    dest, cnt = _plan(eidx)
    counts = cnt[0, :N_EXP].astype(I32)
    dest_k = dest[:, :TOP_K]
    idx4 = dest_k.reshape(SC_WORKERS, SCD_NCH, SC_CH, TOP_K).transpose(0, 1, 3, 2)
    xs = _sc_dispatch(h1p, idx4.reshape(SC_WORKERS, SCD_NCH * TOP_K, SC_CH))
    ys = _experts(_work_items(counts), xs, w_gu[0], b_gu[0], w_down[0], b_down[0])
    out = h1
    for grp in range(COMB_GROUPS):
        dest_g = dest_k[grp * COMB_TG:(grp + 1) * COMB_TG].T.reshape(SC_WORKERS, SC_NCH, SC_CH)
        g4 = _sc_gather(ys, dest_g).reshape(TOP_K, COMB_TG, D // 2)
        out = _combine_dense(g4, out, topw, row(ln2_g[0]), row(ln2_b[0]), grp)
    return out.reshape(BATCH, SEQ, D)
```

```python
import functools
import math

import numpy as np
import jax
import jax.numpy as jnp
from jax import lax
from jax.experimental import pallas as pl
from jax.experimental.pallas import tpu as pltpu
from jax.experimental.pallas import tpu_sc as plsc

F32 = jnp.float32
BF16 = jnp.bfloat16
I32 = jnp.int32
U32 = jnp.uint32

D = 1024
BATCH = 4
SEQ = 4096
T = BATCH * SEQ
GLA_H = 4
GLA_DK = 128
GLA_DV = 256
GLA_LR = 16
GLA_TAU = 16.0
GLA_C = 64
FN_G = 4
FN_GW = 128
FN_W = 512
MEM_LEN = 256
MEM_H = 4
MEM_HD = 128
MQ_W = 512
N_EXP = 32
TOP_K = 4
D_FF = 1024
SW_LIMIT = 7.0
SW_ALPHA = 1.702
LN_EPS = 1e-5
RMS_EPS = 1e-6
DN_ALPHA = 2.0 ** 0.25
A_ROWS = T * TOP_K

FFT_N1 = 128
FFT_N2 = 32

LANES = 128
NEG_BIG = -1e30
MIB = 1024 * 1024


def _ln(x, g, b):
    mu = jnp.mean(x, axis=-1, keepdims=True)
    xc = x - mu
    var = jnp.mean(xc * xc, axis=-1, keepdims=True)
    return xc * lax.rsqrt(var + LN_EPS) * g + b


def _dot(a, b):
    return jnp.dot(a, b, preferred_element_type=F32)


def _dot_nt(a, b):
    return lax.dot_general(a, b, (((1,), (1,)), ((), ())), preferred_element_type=F32)


def _dot_tn(a, b):
    return lax.dot_general(a, b, (((0,), (0,)), ((), ())), preferred_element_type=F32)


def _split_bf16(a):
    hi = a.astype(BF16)
    return hi, (a - hi.astype(F32)).astype(BF16)


INPROJ_TM = 1024
INPROJ_TN = 3072


PROJ_W = 6 * 1024


def _inproj_kernel(x_ref, g_ref, b_ref, w_ref, bias_ref, wmq_ref, bmq_ref, wlr_ref, blr_ref,
                   proj_ref, mq_ref, lr_ref, hb_ref):
    @pl.when(pl.program_id(1) == 0)
    def _():
        hb = _ln(x_ref[...], g_ref[...], b_ref[...]).astype(BF16)
        hb_ref[...] = hb
        lr_ref[...] = _dot(hb, wlr_ref[...]) + blr_ref[...]
        mq_ref[...] = (_dot(hb, wmq_ref[...]) + bmq_ref[...]).astype(BF16)

    proj_ref[...] = (_dot(hb_ref[...], w_ref[...]) + bias_ref[...]).astype(BF16)


def _inproj(x2, ln_g, ln_b, w_main, b_main, w_mq, b_mq, w_lr, b_lr):
    tm, tn = INPROJ_TM, INPROJ_TN
    nj = PROJ_W // tn
    row = lambda i, j: (i, 0)
    const = lambda i, j: (0, 0)
    outs = (
        jax.ShapeDtypeStruct((T, PROJ_W), BF16),
        jax.ShapeDtypeStruct((T, MQ_W), BF16),
        jax.ShapeDtypeStruct((T, LANES), F32),
    )
    return pl.pallas_call(
        _inproj_kernel,
        grid=(T // tm, nj),
        in_specs=[
            pl.BlockSpec((tm, D), row),
            pl.BlockSpec((1, D), const),
            pl.BlockSpec((1, D), const),
            pl.BlockSpec((D, tn), lambda i, j: (0, j)),
            pl.BlockSpec((1, tn), lambda i, j: (0, j)),
            pl.BlockSpec((D, MQ_W), const),
            pl.BlockSpec((1, MQ_W), const),
            pl.BlockSpec((D, LANES), const),
            pl.BlockSpec((1, LANES), const),
        ],
        out_specs=[
            pl.BlockSpec((tm, tn), lambda i, j: (i, j)),
            pl.BlockSpec((tm, MQ_W), row),
            pl.BlockSpec((tm, LANES), row),
        ],
        out_shape=outs,
        scratch_shapes=[pltpu.VMEM((tm, D), BF16)],
        compiler_params=pltpu.CompilerParams(
            dimension_semantics=("arbitrary", "arbitrary"),
            vmem_limit_bytes=48 * MIB),
        name="ln_inproj",
    )(x2, ln_g, ln_b, w_main, b_main, w_mq, b_mq, w_lr, b_lr)


GLA_BULK = 256
GLA_FIN = 512
GLA_NCH = SEQ // GLA_C
GLA_CPB = GLA_BULK // GLA_C


def _gla_kernel(q_ref, k_ref, v_ref, r_ref, lr_ref, wd_ref, bd_ref, g_ref, cs_ref, o_ref,
                acc_ref, qin_ref, kin_ref, kst_ref, dec_ref, u_ref, stf_ref, stb_ref):
    C = GLA_C
    G = GLA_BULK
    DK = GLA_DK
    NG = SEQ // G
    scale = DK ** -0.5
    ii = lax.broadcasted_iota(I32, (G, G), 0)
    jj = lax.broadcasted_iota(I32, (G, G), 1)
    same = (ii // C) == (jj // C)
    lower = jnp.logical_and(same, ii >= jj)
    upper = jnp.logical_and(same, ii <= jj)
    is_fwd = lax.broadcasted_iota(I32, (G, 2 * DK), 1) < DK
    chunk_of_row = lax.broadcasted_iota(I32, (G, DK), 0) // C

    def stage_a(gi):
        rows = pl.ds(pl.multiple_of(gi * G, G), G)
        z = _dot(lr_ref[rows, :], wd_ref[...]) + bd_ref[...]
        la = -(jnp.maximum(-z, 0.0) + jnp.log(1.0 + jnp.exp(-jnp.abs(z)))) * (1.0 / GLA_TAU)
        la_hi, la_lo = _split_bf16(la)
        pre2 = _dot(cs_ref[...], jnp.concatenate([la_hi, la_lo], axis=1))
        pre = pre2[:, :2 * DK] + pre2[:, 2 * DK:]
        blast = jnp.concatenate(
            [jnp.broadcast_to(pre[ci * C + C - 1:ci * C + C, :], (C, 2 * DK))
             for ci in range(GLA_CPB)], axis=0)
        b = jnp.where(is_fwd, pre, blast - pre + la)
        qf32 = q_ref[rows, :].astype(F32)
        kf32 = k_ref[rows, :].astype(F32)
        q2 = jnp.concatenate([qf32, qf32], axis=1)
        k2 = jnp.concatenate([kf32, kf32], axis=1)
        qin_ref[rows, :] = (q2 * (scale * jnp.exp(b))).astype(BF16)
        kin_ref[rows, :] = (k2 * jnp.exp(-b)).astype(BF16)
        kst_ref[rows, :] = (k2 * jnp.exp(blast - b)).astype(BF16)
        dec = jnp.exp(blast)
        for ci in range(GLA_CPB):
            dec_ref[pl.ds(gi * GLA_CPB + ci, 1), :] = dec[ci * C:ci * C + 1, :]

    def stage_b(gi):
        rows = pl.ds(pl.multiple_of(gi * G, G), G)
        qi = qin_ref[rows, :]
        ki = kin_ref[rows, :]
        ks = kst_ref[rows, :]
        vb = v_ref[rows, :]
        att = (jnp.where(lower, _dot_nt(qi[:, :DK], ki[:, :DK]), 0.0)
               + jnp.where(upper, _dot_nt(qi[:, DK:], ki[:, DK:]), 0.0))
        acc_ref[rows, :] = _dot(att.astype(BF16), vb)
        ksb = jnp.concatenate(
            [jnp.where(chunk_of_row == ci, ks[:, d * DK:(d + 1) * DK], jnp.zeros((G, DK), BF16))
             for d in range(2) for ci in range(GLA_CPB)], axis=1)
        u = _dot_tn(vb, ksb)
        for d in range(2):
            for ci in range(GLA_CPB):
                col = (d * GLA_CPB + ci) * DK
                u_ref[d, gi * GLA_CPB + ci] = u[:, col:col + DK]

    stage_a(0)

    def bulk(gi, carry):
        stage_b(gi - 1)
        stage_a(gi)
        return carry

    lax.fori_loop(1, NG, bulk, 0)
    stage_b(NG - 1)

    stf_ref[...] = jnp.zeros_like(stf_ref)
    stb_ref[...] = jnp.zeros_like(stb_ref)

    def one(n, d, st_ref):
        lanes = slice(d * DK, (d + 1) * DK)
        rows = pl.ds(pl.multiple_of(n * C, C), C)
        st = st_ref[...]
        acc_ref[rows, :] += _dot_nt(qin_ref[rows, lanes], st.astype(BF16))
        st_ref[...] = st * dec_ref[pl.ds(n, 1), :][:, lanes] + u_ref[d, n]

    def step(i, carry):
        one(i, 0, stf_ref)
        one(GLA_NCH - 1 - i, 1, stb_ref)
        return carry

    lax.fori_loop(0, GLA_NCH, step, 0, unroll=4)

    def fin(gi, carry):
        rows = pl.ds(pl.multiple_of(gi * GLA_FIN, GLA_FIN), GLA_FIN)
        o = acc_ref[rows, :]
        o = o * lax.rsqrt(jnp.mean(o * o, axis=-1, keepdims=True) + RMS_EPS) * g_ref[...]
        rg = r_ref[rows, :].astype(F32)
        o_ref[rows, :] = (o * (rg * jax.nn.sigmoid(rg))).astype(BF16)
        return carry

    lax.fori_loop(0, SEQ // GLA_FIN, fin, 0)


def _gla(proj, lr, wd, bd, g):
    i = np.arange(GLA_BULK)
    cs = ((i[:, None] // GLA_C) == (i[None, :] // GLA_C)) & (i[:, None] >= i[None, :])
    cs = jnp.asarray(cs, dtype=F32).astype(BF16)
    v_blk = 1024 // GLA_DV
    return pl.pallas_call(
        _gla_kernel,
        grid=(BATCH, GLA_H),
        in_specs=[
            pl.BlockSpec((SEQ, GLA_DK), lambda b, h: (b, h)),
            pl.BlockSpec((SEQ, GLA_DK), lambda b, h: (b, GLA_H + h)),
            pl.BlockSpec((SEQ, GLA_DV), lambda b, h: (b, v_blk + h)),
            pl.BlockSpec((SEQ, GLA_DV), lambda b, h: (b, 2 * v_blk + h)),
            pl.BlockSpec((SEQ, LANES), lambda b, h: (b, 0)),
            pl.BlockSpec((LANES, 2 * GLA_DK), lambda b, h: (0, h)),
            pl.BlockSpec((1, 2 * GLA_DK), lambda b, h: (0, h)),
            pl.BlockSpec((1, GLA_DV), lambda b, h: (0, 0)),
            pl.BlockSpec((GLA_BULK, GLA_BULK), lambda b, h: (0, 0)),
        ],
        out_specs=pl.BlockSpec((SEQ, GLA_DV), lambda b, h: (b, h)),
        out_shape=jax.ShapeDtypeStruct((T, GLA_H * GLA_DV), BF16),
        scratch_shapes=[
            pltpu.VMEM((SEQ, GLA_DV), F32),
            pltpu.VMEM((SEQ, 2 * GLA_DK), BF16),
            pltpu.VMEM((SEQ, 2 * GLA_DK), BF16),
            pltpu.VMEM((SEQ, 2 * GLA_DK), BF16),
            pltpu.VMEM((GLA_NCH, 2 * GLA_DK), F32),
            pltpu.VMEM((2, GLA_NCH, GLA_DV, GLA_DK), F32),
            pltpu.VMEM((GLA_DV, GLA_DK), F32),
            pltpu.VMEM((GLA_DV, GLA_DK), F32),
        ],
        compiler_params=pltpu.CompilerParams(
            dimension_semantics=("arbitrary", "arbitrary"),
            vmem_limit_bytes=58 * MIB),
        name="gla",
    )(proj, proj, proj, proj, lr, wd, bd, g, cs)


FFT1_S = 16
FFT1_ROWS = FFT_N2 * FFT1_S
FFT2_KB = 4


def _fft1_kernel(x_ref, g_ref, b_ref, w_ref, bias_ref, fbig_ref, cw_ref, sw_ref, o_ref):
    xv = x_ref[...].reshape(FFT1_ROWS, D)
    hb = _ln(xv, g_ref[...], b_ref[...]).astype(BF16)
    fn = (_dot(hb, w_ref[...]) + bias_ref[...]).astype(BF16)
    a = _dot(fbig_ref[...], fn)
    ar = a[:FFT1_ROWS]
    ai = a[FFT1_ROWS:]
    cw = jnp.concatenate([cw_ref[...]] * (FN_W // LANES), axis=1)
    sw = jnp.concatenate([sw_ref[...]] * (FN_W // LANES), axis=1)
    o_ref[0] = (ar * cw + ai * sw).reshape(FFT_N2, FFT1_S, FN_W).astype(BF16)
    o_ref[1] = (ai * cw - ar * sw).reshape(FFT_N2, FFT1_S, FN_W).astype(BF16)


def _fft1(x4, ln_g, ln_b, w_fn, b_fn, fbig, cwt, swt):
    s = FFT1_S
    const = lambda b, j: (0, 0)
    return pl.pallas_call(
        _fft1_kernel,
        grid=(BATCH, FFT_N1 // s),
        in_specs=[
            pl.BlockSpec((None, FFT_N2, s, D), lambda b, j: (b, 0, j, 0)),
            pl.BlockSpec((1, D), const),
            pl.BlockSpec((1, D), const),
            pl.BlockSpec((D, FN_W), const),
            pl.BlockSpec((1, FN_W), const),
            pl.BlockSpec((2 * FFT1_ROWS, FFT1_ROWS), const),
            pl.BlockSpec((None, FFT1_ROWS, LANES), lambda b, j: (j, 0, 0)),
            pl.BlockSpec((None, FFT1_ROWS, LANES), lambda b, j: (j, 0, 0)),
        ],
        out_specs=pl.BlockSpec((None, 2, FFT_N2, s, FN_W), lambda b, j: (b, 0, 0, j, 0)),
        out_shape=jax.ShapeDtypeStruct((BATCH, 2, FFT_N2, FFT_N1, FN_W), BF16),
        compiler_params=pltpu.CompilerParams(
            dimension_semantics=("arbitrary", "arbitrary"),
            vmem_limit_bytes=40 * MIB),
        name="fft_stage1",
    )(x4, ln_g, ln_b, w_fn, b_fn, fbig, cwt, swt)


def _fft2_kernel(d_ref, f2_ref, o_ref):
    f2 = f2_ref[...]
    for kk in range(FFT2_KB):
        z = _dot(f2, jnp.concatenate([d_ref[0, kk], d_ref[1, kk]], axis=0))
        o_ref[0, kk] = z[:FFT_N1].astype(BF16)
        o_ref[1, kk] = z[FFT_N1:].astype(BF16)


def _fft2(dmat, f2):
    kb = FFT2_KB
    blk = (None, 2, kb, FFT_N1, FN_W)
    return pl.pallas_call(
        _fft2_kernel,
        grid=(BATCH, FFT_N2 // kb),
        in_specs=[
            pl.BlockSpec(blk, lambda b, j: (b, 0, j, 0, 0)),
            pl.BlockSpec((2 * FFT_N1, 2 * FFT_N1), lambda b, j: (0, 0)),
        ],
        out_specs=pl.BlockSpec(blk, lambda b, j: (b, 0, j, 0, 0)),
        out_shape=jax.ShapeDtypeStruct((BATCH, 2, FFT_N2, FFT_N1, FN_W), BF16),
        compiler_params=pltpu.CompilerParams(
            dimension_semantics=("arbitrary", "arbitrary")),
        name="fft_stage2",
    )(dmat, f2)


def _dft_tables(merge_tm):
    s = FFT1_S
    n2 = np.arange(FFT_N2, dtype=np.float64)
    n1 = np.arange(FFT_N1, dtype=np.float64)
    th = 2.0 * np.pi * np.outer(n2, n2) / FFT_N2
    f1 = np.stack([np.cos(th), -np.sin(th)]) / math.sqrt(SEQ)
    fbig = np.einsum("rkn,st->rksnt", f1, np.eye(s)).reshape(2 * FFT_N2 * s, FFT_N2 * s)
    tw = 2.0 * np.pi * np.outer(n2, n1) / SEQ
    tw = tw.reshape(FFT_N2, FFT_N1 // s, s).transpose(1, 0, 2).reshape(FFT_N1 // s, FFT_N2 * s)
    cwt = np.broadcast_to(np.cos(tw)[:, :, None], tw.shape + (LANES,))
    swt = np.broadcast_to(np.sin(tw)[:, :, None], tw.shape + (LANES,))
    th1 = 2.0 * np.pi * np.outer(n1, n1) / FFT_N1
    c1, s1 = np.cos(th1), np.sin(th1)
    f2 = np.block([[c1, s1], [-s1, c1]])
    cc = np.arange(FN_GW, dtype=np.float64)
    thc = 2.0 * np.pi * np.outer(cc, cc) / FN_GW
    ccs = np.concatenate([np.cos(thc), np.sin(thc)], axis=0) / math.sqrt(FN_GW)
    k1n = merge_tm // FFT_N2
    r = np.arange(merge_tm)
    perm = np.zeros((merge_tm, merge_tm))
    perm[r, (r % FFT_N2) * k1n + r // FFT_N2] = 1.0
    as32 = lambda a: jnp.asarray(np.ascontiguousarray(a), dtype=F32)
    return (as32(fbig).astype(BF16), as32(cwt), as32(swt), as32(f2).astype(BF16),
            as32(ccs).astype(BF16), as32(perm).astype(BF16))


def _memkv_kernel(m_ref, g_ref, b_ref, w_ref, o_ref):
    mn = _ln(m_ref[...], g_ref[...], b_ref[...]).astype(BF16)
    o_ref[...] = _dot(mn, w_ref[...]).astype(BF16)


def _memkv(mem2, g, b, w):
    return pl.pallas_call(
        _memkv_kernel,
        grid=(BATCH,),
        in_specs=[
            pl.BlockSpec((MEM_LEN, D), lambda i: (i, 0)),
            pl.BlockSpec((1, D), lambda i: (0, 0)),
            pl.BlockSpec((1, D), lambda i: (0, 0)),
            pl.BlockSpec((D, 2 * MQ_W), lambda i: (0, 0)),
        ],
        out_specs=pl.BlockSpec((MEM_LEN, 2 * MQ_W), lambda i: (i, 0)),
        out_shape=jax.ShapeDtypeStruct((BATCH * MEM_LEN, 2 * MQ_W), BF16),
        compiler_params=pltpu.CompilerParams(dimension_semantics=("arbitrary",)),
        name="mem_kv",
    )(mem2, g, b, w)


MERGE_TM = 512
MERGE_K1 = MERGE_TM // FFT_N2


def _pack_bf16_pair(v):
    n = v.shape[1] // 2
    bits = lax.bitcast_convert_type(v.astype(BF16).astype(F32), U32)
    return (bits[:, n:] & jnp.uint32(0xFFFF0000)) | (bits[:, :n] >> 16)


def _unpack_bf16_pair(p):
    lo = lax.bitcast_convert_type(p << 16, F32)
    hi = lax.bitcast_convert_type(p & jnp.uint32(0xFFFF0000), F32)
    return lo, hi


def _merge_kernel(x_ref, og_ref, zr_ref, zi_ref, mq_ref, gt_ref, kv_ref,
                  lng_ref, lnb_ref, wg_ref, ccs_ref, perm_ref, wf_ref, wm_ref, wo_ref, bo_ref,
                  l1g_ref, l1b_ref, wr2_ref, wrh_ref, br_ref,
                  h1_ref, h1p_ref, eidx_ref, topw_ref):
    tm = MERGE_TM
    y_gla = _dot(og_ref[...], wg_ref[...])

    zr = zr_ref[...].reshape(tm, FN_W)
    zi = zi_ref[...].reshape(tm, FN_W)
    ys = []
    for g in range(FN_G):
        sl = slice(g * FN_GW, (g + 1) * FN_GW)
        ys.append(_dot(jnp.concatenate([zr[:, sl], zi[:, sl]], axis=1), ccs_ref[...]))
    yp = jnp.concatenate(ys, axis=1).astype(BF16)
    y_fn = _dot(_dot(perm_ref[...], yp).astype(BF16), wf_ref[...])

    oms = []
    for hd in range(MEM_H):
        sl = slice(hd * MEM_HD, (hd + 1) * MEM_HD)
        s = _dot_nt(mq_ref[:, sl], kv_ref[:, sl]) * (MEM_HD ** -0.5)
        s = s - jnp.max(s, axis=-1, keepdims=True)
        p = jnp.exp(s)
        p = p / jnp.sum(p, axis=-1, keepdims=True)
        oms.append(_dot(p.astype(BF16), kv_ref[:, MQ_W + hd * MEM_HD:MQ_W + (hd + 1) * MEM_HD]))
    y_mem = _dot(jnp.concatenate(oms, axis=1).astype(BF16), wm_ref[...])

    merged = (jax.nn.sigmoid(gt_ref[:, 0:D].astype(F32)) * y_gla
              + jax.nn.sigmoid(gt_ref[:, D:2 * D].astype(F32)) * y_fn
              + jax.nn.sigmoid(gt_ref[:, 2 * D:3 * D].astype(F32)) * y_mem)
    mix = _dot(merged.astype(BF16), wo_ref[...]) + bo_ref[...]
    h = _ln(x_ref[...], lng_ref[...], lnb_ref[...])
    h1 = _ln(DN_ALPHA * h + mix, l1g_ref[...], l1b_ref[...])
    h1_ref[...] = h1
    h1p_ref[...] = _pack_bf16_pair(h1)

    h_hi, h_lo = _split_bf16(h1)
    d2 = _dot(h_hi, wr2_ref[...])
    logits = d2[:, :LANES] + d2[:, LANES:] + _dot(h_lo, wrh_ref[...]) + br_ref[...]
    lane = lax.broadcasted_iota(I32, (tm, LANES), 1)
    l = logits
    vals, idxs = [], []
    for _ in range(TOP_K):
        m = jnp.max(l, axis=-1, keepdims=True)
        idx = jnp.min(jnp.where(l == m, lane, LANES), axis=-1, keepdims=True)
        vals.append(m)
        idxs.append(idx)
        l = jnp.where(lane == idx, -jnp.inf, l)
    es = [jnp.exp(v - vals[0]) for v in vals]
    den = es[0] + es[1] + es[2] + es[3]
    eo = jnp.zeros((tm, LANES), I32)
    wo = jnp.zeros((tm, LANES), F32)
    for k in range(TOP_K):
        eo = jnp.where(lane == k, idxs[k], eo)
        wo = jnp.where(lane == k, es[k] / den, wo)
    eidx_ref[...] = eo
    topw_ref[...] = wo


def _merge(x2, og, z, mq, gates, kv, lng, lnb, wg, ccs, perm, wf, wm, wo, bo, l1g, l1b,
           wr2, wrh, br):
    tm = MERGE_TM
    per_b = SEQ // tm
    row = lambda i: (i, 0)
    const = lambda i: (0, 0)
    zblk = (None, None, FFT_N2, MERGE_K1, FN_W)
    outs = (
        jax.ShapeDtypeStruct((T, D), F32),
        jax.ShapeDtypeStruct((T, D // 2), U32),
        jax.ShapeDtypeStruct((T, LANES), I32),
        jax.ShapeDtypeStruct((T, LANES), F32),
    )
    return pl.pallas_call(
        _merge_kernel,
        grid=(T // tm,),
        in_specs=[
            pl.BlockSpec((tm, D), row),
            pl.BlockSpec((tm, D), row),
            pl.BlockSpec(zblk, lambda i: (i // per_b, 0, 0, i % per_b, 0)),
            pl.BlockSpec(zblk, lambda i: (i // per_b, 1, 0, i % per_b, 0)),
            pl.BlockSpec((tm, MQ_W), row),
            pl.BlockSpec((tm, 3 * D), lambda i: (i, 1)),
            pl.BlockSpec((MEM_LEN, 2 * MQ_W), lambda i: (i // per_b, 0)),
            pl.BlockSpec((1, D), const), pl.BlockSpec((1, D), const),
            pl.BlockSpec((D, D), const),
            pl.BlockSpec((2 * FN_GW, FN_GW), const),
            pl.BlockSpec((tm, tm), const),
            pl.BlockSpec((FN_W, D), const),
            pl.BlockSpec((MQ_W, D), const),
            pl.BlockSpec((D, D), const),
            pl.BlockSpec((1, D), const),
            pl.BlockSpec((1, D), const), pl.BlockSpec((1, D), const),
            pl.BlockSpec((D, 2 * LANES), const),
            pl.BlockSpec((D, LANES), const),
            pl.BlockSpec((1, LANES), const),
        ],
        out_specs=[
            pl.BlockSpec((tm, D), row),
            pl.BlockSpec((tm, D // 2), row),
            pl.BlockSpec((tm, LANES), row),
            pl.BlockSpec((tm, LANES), row),
        ],
        out_shape=outs,
        compiler_params=pltpu.CompilerParams(
            dimension_semantics=("arbitrary",),
            vmem_limit_bytes=58 * MIB),
        name="merge_ln1_router",
    )(x2, og, z, z, mq, gates, kv, lng, lnb, wg, ccs, perm, wf, wm, wo, bo, l1g, l1b,
      wr2, wrh, br)


PLAN_TP = 1024


def _plan_kernel(e_ref, dest_ref, cnt_out_ref, cnt_ref, off_ref):
    p = pl.program_id(0)
    i = pl.program_id(1)
    tp = PLAN_TP
    lane = lax.broadcasted_iota(I32, (tp, LANES), 1)
    e = e_ref[...]
    onehots = [lane == e[:, k:k + 1] for k in range(TOP_K)]
    mf = jnp.zeros((tp, LANES), F32)
    for oh in onehots:
        mf = mf + jnp.where(oh, 1.0, 0.0)
    colsum = jnp.sum(mf, axis=0, keepdims=True)

    @pl.when(jnp.logical_and(p == 0, i == 0))
    def _():
        cnt_ref[...] = jnp.zeros_like(cnt_ref)

    @pl.when(jnp.logical_and(p == 1, i == 0))
    def _():
        tot = cnt_ref[...]
        cnt_out_ref[...] = jnp.broadcast_to(tot, cnt_out_ref.shape)
        padded = jnp.floor((tot + (MOE_BM - 1)) * (1.0 / MOE_BM)) * MOE_BM
        lane1 = lax.broadcasted_iota(I32, (1, LANES), 1)
        inc = padded
        for s in (1, 2, 4, 8, 16, 32, 64):
            inc = inc + jnp.where(lane1 >= s, pltpu.roll(inc, s, 1), 0.0)
        off_ref[...] = inc - padded
        cnt_ref[...] = jnp.zeros_like(cnt_ref)

    @pl.when(p == 0)
    def _():
        dest_ref[...] = jnp.zeros_like(dest_ref)

    @pl.when(p == 1)
    def _():
        ri = lax.broadcasted_iota(I32, (tp, tp), 0)
        ci = lax.broadcasted_iota(I32, (tp, tp), 1)
        ltri = jnp.where(ri > ci, 1.0, 0.0).astype(BF16)
        rank = _dot(ltri, mf.astype(BF16)) + cnt_ref[...] + off_ref[...]
        out = jnp.zeros((tp, LANES), I32)
        for k in range(TOP_K):
            dk = jnp.sum(jnp.where(onehots[k], rank, 0.0), axis=-1, keepdims=True)
            out = jnp.where(lane == k, dk.astype(I32), out)
        dest_ref[...] = out

    cnt_ref[...] += colsum


def _plan(eidx):
    tp = PLAN_TP
    return pl.pallas_call(
        _plan_kernel,
        grid=(2, T // tp),
        in_specs=[pl.BlockSpec((tp, LANES), lambda p, i: (i, 0))],
        out_specs=[
            pl.BlockSpec((tp, LANES), lambda p, i: (i * p, 0)),
            pl.BlockSpec((8, LANES), lambda p, i: (0, 0)),
        ],
        out_shape=(jax.ShapeDtypeStruct((T, LANES), I32),
                   jax.ShapeDtypeStruct((8, LANES), F32)),
        scratch_shapes=[pltpu.VMEM((1, LANES), F32), pltpu.VMEM((1, LANES), F32)],
        compiler_params=pltpu.CompilerParams(
            dimension_semantics=("arbitrary", "arbitrary")),
        name="route_plan",
    )(eidx)


MOE_BM = 512
MOE_NW = A_ROWS // MOE_BM + N_EXP
XS_ROWS = MOE_NW * MOE_BM


def _expert_kernel(we_ref, wb_ref, wv_ref, wfe_ref, wsl_ref, wnx_ref,
                   x_ref, wgu_hbm, bgu_ref, wdn_hbm, bdn_ref, o_ref,
                   wgu_f32, wdn_f32, wgu_bf, wdn_bf, sems):
    w = pl.program_id(0)
    e = we_ref[w]

    def weight_copies(expert, slot):
        return (pltpu.make_async_copy(wgu_hbm.at[expert], wgu_f32.at[slot], sems.at[slot, 0]),
                pltpu.make_async_copy(wdn_hbm.at[expert], wdn_f32.at[slot], sems.at[slot, 1]))

    @pl.when(w == 0)
    def _():
        for cp_ in weight_copies(e, 0):
            cp_.start()

    @pl.when(wfe_ref[w] == 1)
    def _():
        slot = wsl_ref[w]
        for cp_ in weight_copies(e, slot):
            cp_.wait()
        wgu_bf[...] = wgu_f32[slot].astype(BF16)
        wdn_bf[...] = wdn_f32[slot].astype(BF16)
        nxt = wnx_ref[w]

        @pl.when(nxt >= 0)
        def _():
            for cp_ in weight_copies(nxt, 1 - slot):
                cp_.start()

    def ffn(rows):
        xlo, xhi = _unpack_bf16_pair(x_ref[rows, :])
        half = D // 2
        gu = (_dot(xlo.astype(BF16), wgu_bf[:half, :]) + _dot(xhi.astype(BF16), wgu_bf[half:, :])
              + bgu_ref[pl.ds(e, 1), :])
        gate = jnp.minimum(gu[:, :D_FF], SW_LIMIT)
        up = jnp.clip(gu[:, D_FF:], -SW_LIMIT, SW_LIMIT)
        act = (up + 1.0) * (gate * jax.nn.sigmoid(SW_ALPHA * gate))
        out = _dot(act.astype(BF16), wdn_bf[...]) + bdn_ref[pl.ds(e, 1), :]
        o_ref[rows, :] = _pack_bf16_pair(out)

    @pl.when(wv_ref[w] == 1)
    def _():
        ffn(slice(None))

    @pl.when(wv_ref[w] == 2)
    def _():
        ffn(slice(0, MOE_BM // 2))


def _experts(meta, xs, w_gu, b_gu, w_down, b_down):
    return pl.pallas_call(
        _expert_kernel,
        grid_spec=pltpu.PrefetchScalarGridSpec(
            num_scalar_prefetch=len(meta),
            grid=(MOE_NW,),
            in_specs=[
                pl.BlockSpec((MOE_BM, D // 2), lambda w, we, wb, *_: (wb[w], 0)),
                pl.BlockSpec(memory_space=pl.ANY),
                pl.BlockSpec((N_EXP, 2 * D_FF), lambda w, *_: (0, 0)),
                pl.BlockSpec(memory_space=pl.ANY),
                pl.BlockSpec((N_EXP, D), lambda w, *_: (0, 0)),
            ],
            out_specs=pl.BlockSpec((MOE_BM, D // 2), lambda w, we, wb, *_: (wb[w], 0)),
            scratch_shapes=[
                pltpu.VMEM((2, D, 2 * D_FF), F32),
                pltpu.VMEM((2, D_FF, D), F32),
                pltpu.VMEM((D, 2 * D_FF), BF16),
                pltpu.VMEM((D_FF, D), BF16),
                pltpu.SemaphoreType.DMA((2, 2)),
            ],
        ),
        out_shape=jax.ShapeDtypeStruct((XS_ROWS, D // 2), U32),
        compiler_params=pltpu.CompilerParams(
            dimension_semantics=("arbitrary",),
            vmem_limit_bytes=56 * MIB),
        name="moe_experts",
    )(*meta, xs, w_gu, b_gu, w_down, b_down)


def _work_items(counts):
    n_e = (counts + MOE_BM - 1) // MOE_BM
    item_end = jnp.cumsum(n_e)
    total = item_end[-1]
    w = jnp.arange(MOE_NW, dtype=I32)
    valid = w < total
    wc = jnp.minimum(w, total - 1)
    e_w = jnp.sum((item_end[None, :] <= wc[:, None]).astype(I32), axis=1)
    e_w = jnp.minimum(e_w, N_EXP - 1)
    rows_here = counts[e_w] - (wc - (item_end - n_e)[e_w]) * MOE_BM
    valid = jnp.where(valid, jnp.where(rows_here <= MOE_BM // 2, 2, 1), 0)
    prev_e = jnp.concatenate([jnp.full((1,), -1, I32), e_w[:-1]])
    fe = (e_w != prev_e).astype(I32)
    slot = (jnp.cumsum(fe) - 1) % 2
    first_at = jnp.where(fe == 1, w, MOE_NW)
    next_first = jnp.concatenate([lax.cummin(first_at, reverse=True)[1:],
                                  jnp.full((1,), MOE_NW, I32)])
    nxt = jnp.where(next_first < MOE_NW, e_w[jnp.minimum(next_first, MOE_NW - 1)], -1)
    return tuple(a.astype(I32) for a in (e_w, wc, valid, fe, slot, nxt))


COMB_TM = 256
SC_CORES = 2
SC_SUBCORES = 16
SC_WORKERS = SC_CORES * SC_SUBCORES
SC_CH = 64
COMB_GROUPS = 4
COMB_TG = T // COMB_GROUPS
SC_ROWS_PER_W = COMB_TG * TOP_K // SC_WORKERS
SC_NCH = SC_ROWS_PER_W // SC_CH


def _sc_gather(table, idx3):
    mesh = plsc.VectorSubcoreMesh(core_axis_name="c", subcore_axis_name="s")

    @functools.partial(
        pl.kernel, mesh=mesh,
        out_type=jax.ShapeDtypeStruct((COMB_TG * TOP_K, D // 2), U32),
        scratch_types=[
            pltpu.VMEM((SC_NCH, SC_CH), I32),
            pltpu.VMEM((2, SC_CH, D // 2), U32),
            pltpu.SemaphoreType.DMA((2,)),
            pltpu.SemaphoreType.DMA((2,)),
        ],
    )
    def k(table_hbm, idx_hbm, out_hbm, idx_v, rows_v, gsem, psem):
        wid = lax.axis_index("s") * SC_CORES + lax.axis_index("c")
        base = wid * SC_ROWS_PER_W
        pltpu.sync_copy(idx_hbm.at[wid], idx_v)

        def gather(j, b):
            return pltpu.make_async_copy(table_hbm.at[idx_v.at[j]], rows_v.at[b], gsem.at[b])

        def put(j, b):
            return pltpu.make_async_copy(rows_v.at[b], out_hbm.at[pl.ds(base + j * SC_CH, SC_CH)],
                                         psem.at[b])

        gather(0, 0).start()

        @pl.loop(0, SC_NCH, step=2)
        def _(j0):
            for b in range(2):
                j = j0 + b

                @pl.when(j + 1 < SC_NCH)
                def _():
                    @pl.when(j >= 1)
                    def _():
                        put(j - 1, 1 - b).wait()
                    gather(j + 1, 1 - b).start()

                gather(j, b).wait()
                put(j, b).start()

        put(SC_NCH - 2, 0).wait()
        put(SC_NCH - 1, 1).wait()

    return k(table, idx3)


SCD_TOK_PER_W = T // SC_WORKERS
SCD_NCH = SCD_TOK_PER_W // SC_CH


def _sc_dispatch(h1p, idx4):
    mesh = plsc.VectorSubcoreMesh(core_axis_name="c", subcore_axis_name="s")

    @functools.partial(
        pl.kernel, mesh=mesh,
        out_type=jax.ShapeDtypeStruct((XS_ROWS, D // 2), U32),
        scratch_types=[
            pltpu.VMEM((SCD_NCH * TOP_K, SC_CH), I32),
            pltpu.VMEM((2, SC_CH, D // 2), U32),
            pltpu.SemaphoreType.DMA((2,)),
            pltpu.SemaphoreType.DMA((2,)),
        ],
    )
    def k(h_hbm, idx_hbm, xs_hbm, idx_v, rows_v, gsem, psem):
        wid = lax.axis_index("s") * SC_CORES + lax.axis_index("c")
        base = wid * SCD_TOK_PER_W
        pltpu.sync_copy(idx_hbm.at[wid], idx_v)

        def get(c, b):
            return pltpu.make_async_copy(h_hbm.at[pl.ds(base + c * SC_CH, SC_CH)], rows_v.at[b],
                                         gsem.at[b])

        def puts(c, b):
            return [pltpu.make_async_copy(rows_v.at[b], xs_hbm.at[idx_v.at[c * TOP_K + kk]],
                                          psem.at[b]) for kk in range(TOP_K)]

        get(0, 0).start()

        @pl.loop(0, SCD_NCH, step=2)
        def _(c0):
            for b in range(2):
                c = c0 + b

                @pl.when(c + 1 < SCD_NCH)
                def _():
                    @pl.when(c >= 1)
                    def _():
                        for cp_ in puts(c - 1, 1 - b):
                            cp_.wait()
                    get(c + 1, 1 - b).start()

                get(c, b).wait()
                for cp_ in puts(c, b):
                    cp_.start()

        for cp_ in puts(SCD_NCH - 2, 0) + puts(SCD_NCH - 1, 1):
            cp_.wait()

    return k(h1p, idx4)


def _combine_dense_kernel(g_ref, h1_ref, tw_ref, lg_ref, lb_ref, o_ref):
    tw = tw_ref[...]
    ylo = jnp.zeros((COMB_TM, D // 2), F32)
    yhi = jnp.zeros((COMB_TM, D // 2), F32)
    for k in range(TOP_K):
        lo, hi = _unpack_bf16_pair(g_ref[k])
        wk = tw[:, k:k + 1]
        ylo = ylo + lo * wk
        yhi = yhi + hi * wk
    ff = jnp.concatenate([ylo, yhi], axis=1)
    o_ref[...] = _ln(DN_ALPHA * h1_ref[...] + ff, lg_ref[...], lb_ref[...])


def _combine_dense(g4, h1, topw, g, b, group):
    tm = COMB_TM
    t0 = group * (COMB_TG // tm)
    return pl.pallas_call(
        _combine_dense_kernel,
        grid=(COMB_TG // tm,),
        in_specs=[
            pl.BlockSpec((TOP_K, tm, D // 2), lambda i: (0, i, 0)),
            pl.BlockSpec((tm, D), lambda i: (t0 + i, 0)),
            pl.BlockSpec((tm, LANES), lambda i: (t0 + i, 0)),
            pl.BlockSpec((1, D), lambda i: (0, 0)),
            pl.BlockSpec((1, D), lambda i: (0, 0)),
        ],
        out_specs=pl.BlockSpec((tm, D), lambda i: (t0 + i, 0)),
        out_shape=jax.ShapeDtypeStruct((T, D), F32),
        input_output_aliases={1: 0},
        compiler_params=pltpu.CompilerParams(dimension_semantics=("arbitrary",)),
        name="moe_combine_dense_ln2",
    )(g4, h1, topw, g, b)


def _pad_cols(a, n):
    return jnp.pad(a, ((0, 0), (0, n - a.shape[1])))


def kernel(x, mem, ln_in_g, ln_in_b, ln_mem_g, ln_mem_b, w_in, b_in, w_decay_f, b_decay_f,
           w_decay_b, b_decay_b, gla_norm_g, w_br_gla, w_br_fnet, w_br_mem, w_mem_kv, w_out,
           b_out, ln1_g, ln1_b, w_router, b_router, w_gu, b_gu, w_down, b_down, ln2_g, ln2_b):
    assert x.shape == (BATCH, SEQ, D) and w_in.shape[0] == 1
    row = lambda a: a.reshape(1, -1)
    x2 = x.reshape(T, D)
    w_in0, b_in0 = w_in[0], b_in[0]
    c_lr, c_fn, c_mq, c_gt = 3072, 3072 + 2 * GLA_LR, 3104 + FN_W, 3104 + FN_W + MQ_W
    w_main = jnp.concatenate([w_in0[:, :c_lr], w_in0[:, c_gt:]], axis=1).astype(BF16)
    b_main = row(jnp.concatenate([b_in0[:c_lr], b_in0[c_gt:]]))
    w_lr = _pad_cols(w_in0[:, c_lr:c_fn], LANES).astype(BF16)
    b_lr = _pad_cols(row(b_in0[c_lr:c_fn]), LANES)
    w_mq = w_in0[:, c_mq:c_gt].astype(BF16)
    b_mq = row(b_in0[c_mq:c_gt])
    w_fn = w_in0[:, c_fn:c_mq].astype(BF16)
    b_fn = row(b_in0[c_fn:c_mq])
    lng, lnb = row(ln_in_g), row(ln_in_b)

    proj, mq, lr = _inproj(x2, lng, lnb, w_main, b_main, w_mq, b_mq, w_lr, b_lr)

    zpad = jnp.zeros((LANES - 2 * GLA_LR, GLA_H * GLA_DK), F32)
    zlr = jnp.zeros((GLA_LR, GLA_H * GLA_DK), F32)
    wdf = jnp.concatenate([w_decay_f[0], zlr, zpad], axis=0).reshape(LANES, GLA_H, GLA_DK)
    wdb = jnp.concatenate([zlr, w_decay_b[0], zpad], axis=0).reshape(LANES, GLA_H, GLA_DK)
    wd = jnp.concatenate([wdf, wdb], axis=2).reshape(LANES, GLA_H * 2 * GLA_DK)
    bd = jnp.concatenate([b_decay_f[0].reshape(GLA_H, GLA_DK),
                          b_decay_b[0].reshape(GLA_H, GLA_DK)], axis=1).reshape(1, -1)
    og = _gla(proj, lr, wd, bd, row(gla_norm_g[0]))

    fbig, cwt, swt, f2, ccs, perm = _dft_tables(MERGE_TM)
    x4 = x.reshape(BATCH, FFT_N2, FFT_N1, D)
    z = _fft2(_fft1(x4, lng, lnb, w_fn, b_fn, fbig, cwt, swt), f2)

    kv = _memkv(mem.reshape(BATCH * MEM_LEN, D), row(ln_mem_g), row(ln_mem_b),
                w_mem_kv[0].astype(BF16))

    w_r = _pad_cols(w_router[0], LANES)
    wr_hi = w_r.astype(BF16)
    wr_lo = (w_r - wr_hi.astype(F32)).astype(BF16)
    b_r = jnp.concatenate([row(b_router[0]),
                           jnp.full((1, LANES - N_EXP), NEG_BIG, F32)], axis=1)
    h1, h1p, eidx, topw = _merge(
        x2, og, z, mq, proj, kv, lng, lnb,
        w_br_gla[0].astype(BF16), ccs, perm, w_br_fnet[0].astype(BF16),
        w_br_mem[0].astype(BF16), w_out[0].astype(BF16), row(b_out[0]),
        row(ln1_g[0]), row(ln1_b[0]), jnp.concatenate([wr_hi, wr_lo], axis=1), wr_hi, b_r)

    dest, cnt = _plan(eidx)
    counts = cnt[0, :N_EXP].astype(I32)
    dest_k = dest[:, :TOP_K]
    idx4 = dest_k.reshape(SC_WORKERS, SCD_NCH, SC_CH, TOP_K).transpose(0, 1, 3, 2)
    xs = _sc_dispatch(h1p, idx4.reshape(SC_WORKERS, SCD_NCH * TOP_K, SC_CH))
    ys = _experts(_work_items(counts), xs, w_gu[0], b_gu[0], w_down[0], b_down[0])
    out = h1
    for grp in range(COMB_GROUPS):
        dest_g = dest_k[grp * COMB_TG:(grp + 1) * COMB_TG].T.reshape(SC_WORKERS, SC_NCH, SC_CH)
        g4 = _sc_gather(ys, dest_g).reshape(TOP_K, COMB_TG, D // 2)
        out = _combine_dense(g4, out, topw, row(ln2_g[0]), row(ln2_b[0]), grp)
    return out.reshape(BATCH, SEQ, D)
```

```python
import functools
import math

import numpy as np
import jax
import jax.numpy as jnp
from jax import lax
from jax.experimental import pallas as pl
from jax.experimental.pallas import tpu as pltpu
from jax.experimental.pallas import tpu_sc as plsc

F32 = jnp.float32
BF16 = jnp.bfloat16
I32 = jnp.int32
U32 = jnp.uint32

D = 1024
BATCH = 4
SEQ = 4096
T = BATCH * SEQ
GLA_H = 4
GLA_DK = 128
GLA_DV = 256
GLA_LR = 16
GLA_TAU = 16.0
GLA_C = 64
FN_G = 4
FN_GW = 128
FN_W = 512
MEM_LEN = 256
MEM_H = 4
MEM_HD = 128
MQ_W = 512
N_EXP = 32
TOP_K = 4
D_FF = 1024
SW_LIMIT = 7.0
SW_ALPHA = 1.702
LN_EPS = 1e-5
RMS_EPS = 1e-6
DN_ALPHA = 2.0 ** 0.25
A_ROWS = T * TOP_K

FFT_N1 = 128
FFT_N2 = 32

LANES = 128
NEG_BIG = -1e30
MIB = 1024 * 1024


def _ln(x, g, b):
    mu = jnp.mean(x, axis=-1, keepdims=True)
    xc = x - mu
    var = jnp.mean(xc * xc, axis=-1, keepdims=True)
    return xc * lax.rsqrt(var + LN_EPS) * g + b


def _dot(a, b):
    return jnp.dot(a, b, preferred_element_type=F32)


def _dot_nt(a, b):
    return lax.dot_general(a, b, (((1,), (1,)), ((), ())), preferred_element_type=F32)


def _dot_tn(a, b):
    return lax.dot_general(a, b, (((0,), (0,)), ((), ())), preferred_element_type=F32)


def _split_bf16(a):
    hi = a.astype(BF16)
    return hi, (a - hi.astype(F32)).astype(BF16)


INPROJ_TM = 1024
INPROJ_TN = 3072


PROJ_W = 6 * 1024


def _inproj_kernel(x_ref, g_ref, b_ref, w_ref, bias_ref, wmq_ref, bmq_ref, wlr_ref, blr_ref,
                   proj_ref, mq_ref, lr_ref, hb_ref):
    @pl.when(pl.program_id(1) == 0)
    def _():
        hb = _ln(x_ref[...], g_ref[...], b_ref[...]).astype(BF16)
        hb_ref[...] = hb
        lr_ref[...] = _dot(hb, wlr_ref[...]) + blr_ref[...]
        mq_ref[...] = (_dot(hb, wmq_ref[...]) + bmq_ref[...]).astype(BF16)

    proj_ref[...] = (_dot(hb_ref[...], w_ref[...]) + bias_ref[...]).astype(BF16)


def _inproj(x2, ln_g, ln_b, w_main, b_main, w_mq, b_mq, w_lr, b_lr):
    tm, tn = INPROJ_TM, INPROJ_TN
    nj = PROJ_W // tn
    row = lambda i, j: (i, 0)
    const = lambda i, j: (0, 0)
    outs = (
        jax.ShapeDtypeStruct((T, PROJ_W), BF16),
        jax.ShapeDtypeStruct((T, MQ_W), BF16),
        jax.ShapeDtypeStruct((T, LANES), F32),
    )
    return pl.pallas_call(
        _inproj_kernel,
        grid=(T // tm, nj),
        in_specs=[
            pl.BlockSpec((tm, D), row),
            pl.BlockSpec((1, D), const),
            pl.BlockSpec((1, D), const),
            pl.BlockSpec((D, tn), lambda i, j: (0, j)),
            pl.BlockSpec((1, tn), lambda i, j: (0, j)),
            pl.BlockSpec((D, MQ_W), const),
            pl.BlockSpec((1, MQ_W), const),
            pl.BlockSpec((D, LANES), const),
            pl.BlockSpec((1, LANES), const),
        ],
        out_specs=[
            pl.BlockSpec((tm, tn), lambda i, j: (i, j)),
            pl.BlockSpec((tm, MQ_W), row),
            pl.BlockSpec((tm, LANES), row),
        ],
        out_shape=outs,
        scratch_shapes=[pltpu.VMEM((tm, D), BF16)],
        compiler_params=pltpu.CompilerParams(
            dimension_semantics=("arbitrary", "arbitrary"),
            vmem_limit_bytes=48 * MIB),
        name="ln_inproj",
    )(x2, ln_g, ln_b, w_main, b_main, w_mq, b_mq, w_lr, b_lr)


GLA_BULK = 256
GLA_FIN = 512
GLA_NCH = SEQ // GLA_C
GLA_CPB = GLA_BULK // GLA_C


def _gla_kernel(q_ref, k_ref, v_ref, r_ref, lr_ref, wd_ref, bd_ref, g_ref, cs_ref, o_ref,
                acc_ref, qin_ref, kin_ref, kst_ref, dec_ref, u_ref, stf_ref, stb_ref):
    C = GLA_C
    G = GLA_BULK
    DK = GLA_DK
    NG = SEQ // G
    scale = DK ** -0.5
    ii = lax.broadcasted_iota(I32, (G, G), 0)
    jj = lax.broadcasted_iota(I32, (G, G), 1)
    same = (ii // C) == (jj // C)
    lower = jnp.logical_and(same, ii >= jj)
    upper = jnp.logical_and(same, ii <= jj)
    is_fwd = lax.broadcasted_iota(I32, (G, 2 * DK), 1) < DK
    chunk_of_row = lax.broadcasted_iota(I32, (G, DK), 0) // C

    def stage_a(gi):
        rows = pl.ds(pl.multiple_of(gi * G, G), G)
        z = _dot(jnp.concatenate(_split_bf16(lr_ref[rows, :]), axis=1), wd_ref[...]) + bd_ref[...]
        la = -(jnp.maximum(-z, 0.0) + jnp.log(1.0 + jnp.exp(-jnp.abs(z)))) * (1.0 / GLA_TAU)
        la_hi, la_lo = _split_bf16(la)
        pre2 = _dot(cs_ref[...], jnp.concatenate([la_hi, la_lo], axis=1))
        pre = pre2[:, :2 * DK] + pre2[:, 2 * DK:]
        blast = jnp.concatenate(
            [jnp.broadcast_to(pre[ci * C + C - 1:ci * C + C, :], (C, 2 * DK))
             for ci in range(GLA_CPB)], axis=0)
        b = jnp.where(is_fwd, pre, blast - pre + la)
        qf32 = q_ref[rows, :].astype(F32)
        kf32 = k_ref[rows, :].astype(F32)
        q2 = jnp.concatenate([qf32, qf32], axis=1)
        k2 = jnp.concatenate([kf32, kf32], axis=1)
        qin_ref[rows, :] = (q2 * (scale * jnp.exp(b))).astype(BF16)
        kin_ref[rows, :] = (k2 * jnp.exp(-b)).astype(BF16)
        kst_ref[rows, :] = (k2 * jnp.exp(blast - b)).astype(BF16)
        dec = jnp.exp(blast)
        for ci in range(GLA_CPB):
            dec_ref[pl.ds(gi * GLA_CPB + ci, 1), :] = dec[ci * C:ci * C + 1, :]

    def stage_b(gi):
        rows = pl.ds(pl.multiple_of(gi * G, G), G)
        qi = qin_ref[rows, :]
        ki = kin_ref[rows, :]
        ks = kst_ref[rows, :]
        vb = v_ref[rows, :]
        att = (jnp.where(lower, _dot_nt(qi[:, :DK], ki[:, :DK]), 0.0)
               + jnp.where(upper, _dot_nt(qi[:, DK:], ki[:, DK:]), 0.0))
        acc_ref[rows, :] = _dot(att.astype(BF16), vb)
        ksb = jnp.concatenate(
            [jnp.where(chunk_of_row == ci, ks[:, d * DK:(d + 1) * DK], jnp.zeros((G, DK), BF16))
             for d in range(2) for ci in range(GLA_CPB)], axis=1)
        u = _dot_tn(vb, ksb)
        for d in range(2):
            for ci in range(GLA_CPB):
                col = (d * GLA_CPB + ci) * DK
                u_ref[d, gi * GLA_CPB + ci] = u[:, col:col + DK]

    stage_a(0)

    def bulk(gi, carry):
        stage_b(gi - 1)
        stage_a(gi)
        return carry

    lax.fori_loop(1, NG, bulk, 0)
    stage_b(NG - 1)

    stf_ref[...] = jnp.zeros_like(stf_ref)
    stb_ref[...] = jnp.zeros_like(stb_ref)

    def one(n, d, st_ref):
        lanes = slice(d * DK, (d + 1) * DK)
        rows = pl.ds(pl.multiple_of(n * C, C), C)
        st = st_ref[...]
        acc_ref[rows, :] += _dot_nt(qin_ref[rows, lanes], st.astype(BF16))
        st_ref[...] = st * dec_ref[pl.ds(n, 1), :][:, lanes] + u_ref[d, n]

    def step(i, carry):
        one(i, 0, stf_ref)
        one(GLA_NCH - 1 - i, 1, stb_ref)
        return carry

    lax.fori_loop(0, GLA_NCH, step, 0, unroll=4)

    def fin(gi, carry):
        rows = pl.ds(pl.multiple_of(gi * GLA_FIN, GLA_FIN), GLA_FIN)
        o = acc_ref[rows, :]
        o = o * lax.rsqrt(jnp.mean(o * o, axis=-1, keepdims=True) + RMS_EPS) * g_ref[...]
        rg = r_ref[rows, :].astype(F32)
        o_ref[rows, :] = (o * (rg * jax.nn.sigmoid(rg))).astype(BF16)
        return carry

    lax.fori_loop(0, SEQ // GLA_FIN, fin, 0)


def _gla(proj, lr, wd, bd, g):
    i = np.arange(GLA_BULK)
    cs = ((i[:, None] // GLA_C) == (i[None, :] // GLA_C)) & (i[:, None] >= i[None, :])
    cs = jnp.asarray(cs, dtype=F32).astype(BF16)
    v_blk = 1024 // GLA_DV
    return pl.pallas_call(
        _gla_kernel,
        grid=(BATCH, GLA_H),
        in_specs=[
            pl.BlockSpec((SEQ, GLA_DK), lambda b, h: (b, h)),
            pl.BlockSpec((SEQ, GLA_DK), lambda b, h: (b, GLA_H + h)),
            pl.BlockSpec((SEQ, GLA_DV), lambda b, h: (b, v_blk + h)),
            pl.BlockSpec((SEQ, GLA_DV), lambda b, h: (b, 2 * v_blk + h)),
            pl.BlockSpec((SEQ, LANES), lambda b, h: (b, 0)),
            pl.BlockSpec((2 * LANES, 2 * GLA_DK), lambda b, h: (0, h)),
            pl.BlockSpec((1, 2 * GLA_DK), lambda b, h: (0, h)),
            pl.BlockSpec((1, GLA_DV), lambda b, h: (0, 0)),
            pl.BlockSpec((GLA_BULK, GLA_BULK), lambda b, h: (0, 0)),
        ],
        out_specs=pl.BlockSpec((SEQ, GLA_DV), lambda b, h: (b, h)),
        out_shape=jax.ShapeDtypeStruct((T, GLA_H * GLA_DV), BF16),
        scratch_shapes=[
            pltpu.VMEM((SEQ, GLA_DV), F32),
            pltpu.VMEM((SEQ, 2 * GLA_DK), BF16),
            pltpu.VMEM((SEQ, 2 * GLA_DK), BF16),
            pltpu.VMEM((SEQ, 2 * GLA_DK), BF16),
            pltpu.VMEM((GLA_NCH, 2 * GLA_DK), F32),
            pltpu.VMEM((2, GLA_NCH, GLA_DV, GLA_DK), F32),
            pltpu.VMEM((GLA_DV, GLA_DK), F32),
            pltpu.VMEM((GLA_DV, GLA_DK), F32),
        ],
        compiler_params=pltpu.CompilerParams(
            dimension_semantics=("arbitrary", "arbitrary"),
            vmem_limit_bytes=58 * MIB),
        name="gla",
    )(proj, proj, proj, proj, lr, wd, bd, g, cs)


FFT1_S = 16
FFT1_ROWS = FFT_N2 * FFT1_S
FFT2_KB = 4


def _fft1_kernel(x_ref, g_ref, b_ref, w_ref, bias_ref, fbig_ref, cw_ref, sw_ref, o_ref):
    xv = x_ref[...].reshape(FFT1_ROWS, D)
    hb = _ln(xv, g_ref[...], b_ref[...]).astype(BF16)
    fn = (_dot(hb, w_ref[...]) + bias_ref[...]).astype(BF16)
    a = _dot(fbig_ref[...], fn)
    ar = a[:FFT1_ROWS]
    ai = a[FFT1_ROWS:]
    cw = jnp.concatenate([cw_ref[...]] * (FN_W // LANES), axis=1)
    sw = jnp.concatenate([sw_ref[...]] * (FN_W // LANES), axis=1)
    o_ref[0] = (ar * cw + ai * sw).reshape(FFT_N2, FFT1_S, FN_W).astype(BF16)
    o_ref[1] = (ai * cw - ar * sw).reshape(FFT_N2, FFT1_S, FN_W).astype(BF16)


def _fft1(x4, ln_g, ln_b, w_fn, b_fn, fbig, cwt, swt):
    s = FFT1_S
    const = lambda b, j: (0, 0)
    return pl.pallas_call(
        _fft1_kernel,
        grid=(BATCH, FFT_N1 // s),
        in_specs=[
            pl.BlockSpec((None, FFT_N2, s, D), lambda b, j: (b, 0, j, 0)),
            pl.BlockSpec((1, D), const),
            pl.BlockSpec((1, D), const),
            pl.BlockSpec((D, FN_W), const),
            pl.BlockSpec((1, FN_W), const),
            pl.BlockSpec((2 * FFT1_ROWS, FFT1_ROWS), const),
            pl.BlockSpec((None, FFT1_ROWS, LANES), lambda b, j: (j, 0, 0)),
            pl.BlockSpec((None, FFT1_ROWS, LANES), lambda b, j: (j, 0, 0)),
        ],
        out_specs=pl.BlockSpec((None, 2, FFT_N2, s, FN_W), lambda b, j: (b, 0, 0, j, 0)),
        out_shape=jax.ShapeDtypeStruct((BATCH, 2, FFT_N2, FFT_N1, FN_W), BF16),
        compiler_params=pltpu.CompilerParams(
            dimension_semantics=("arbitrary", "arbitrary"),
            vmem_limit_bytes=40 * MIB),
        name="fft_stage1",
    )(x4, ln_g, ln_b, w_fn, b_fn, fbig, cwt, swt)


def _fft2_kernel(d_ref, f2_ref, o_ref):
    f2 = f2_ref[...]
    for kk in range(FFT2_KB):
        z = _dot(f2, jnp.concatenate([d_ref[0, kk], d_ref[1, kk]], axis=0))
        o_ref[0, kk] = z[:FFT_N1].astype(BF16)
        o_ref[1, kk] = z[FFT_N1:].astype(BF16)


def _fft2(dmat, f2):
    kb = FFT2_KB
    blk = (None, 2, kb, FFT_N1, FN_W)
    return pl.pallas_call(
        _fft2_kernel,
        grid=(BATCH, FFT_N2 // kb),
        in_specs=[
            pl.BlockSpec(blk, lambda b, j: (b, 0, j, 0, 0)),
            pl.BlockSpec((2 * FFT_N1, 2 * FFT_N1), lambda b, j: (0, 0)),
        ],
        out_specs=pl.BlockSpec(blk, lambda b, j: (b, 0, j, 0, 0)),
        out_shape=jax.ShapeDtypeStruct((BATCH, 2, FFT_N2, FFT_N1, FN_W), BF16),
        compiler_params=pltpu.CompilerParams(
            dimension_semantics=("arbitrary", "arbitrary")),
        name="fft_stage2",
    )(dmat, f2)


def _dft_tables(merge_tm):
    s = FFT1_S
    n2 = np.arange(FFT_N2, dtype=np.float64)
    n1 = np.arange(FFT_N1, dtype=np.float64)
    th = 2.0 * np.pi * np.outer(n2, n2) / FFT_N2
    f1 = np.stack([np.cos(th), -np.sin(th)]) / math.sqrt(SEQ)
    fbig = np.einsum("rkn,st->rksnt", f1, np.eye(s)).reshape(2 * FFT_N2 * s, FFT_N2 * s)
    tw = 2.0 * np.pi * np.outer(n2, n1) / SEQ
    tw = tw.reshape(FFT_N2, FFT_N1 // s, s).transpose(1, 0, 2).reshape(FFT_N1 // s, FFT_N2 * s)
    cwt = np.broadcast_to(np.cos(tw)[:, :, None], tw.shape + (LANES,))
    swt = np.broadcast_to(np.sin(tw)[:, :, None], tw.shape + (LANES,))
    th1 = 2.0 * np.pi * np.outer(n1, n1) / FFT_N1
    c1, s1 = np.cos(th1), np.sin(th1)
    f2 = np.block([[c1, s1], [-s1, c1]])
    cc = np.arange(FN_GW, dtype=np.float64)
    thc = 2.0 * np.pi * np.outer(cc, cc) / FN_GW
    ccs = np.concatenate([np.cos(thc), np.sin(thc)], axis=0) / math.sqrt(FN_GW)
    k1n = merge_tm // FFT_N2
    r = np.arange(merge_tm)
    perm = np.zeros((merge_tm, merge_tm))
    perm[r, (r % FFT_N2) * k1n + r // FFT_N2] = 1.0
    as32 = lambda a: jnp.asarray(np.ascontiguousarray(a), dtype=F32)
    return (as32(fbig).astype(BF16), as32(cwt), as32(swt), as32(f2).astype(BF16),
            as32(ccs).astype(BF16), as32(perm).astype(BF16))


def _memkv_kernel(m_ref, g_ref, b_ref, w_ref, o_ref):
    mn = _ln(m_ref[...], g_ref[...], b_ref[...]).astype(BF16)
    o_ref[...] = _dot(mn, w_ref[...]).astype(BF16)


def _memkv(mem2, g, b, w):
    return pl.pallas_call(
        _memkv_kernel,
        grid=(BATCH,),
        in_specs=[
            pl.BlockSpec((MEM_LEN, D), lambda i: (i, 0)),
            pl.BlockSpec((1, D), lambda i: (0, 0)),
            pl.BlockSpec((1, D), lambda i: (0, 0)),
            pl.BlockSpec((D, 2 * MQ_W), lambda i: (0, 0)),
        ],
        out_specs=pl.BlockSpec((MEM_LEN, 2 * MQ_W), lambda i: (i, 0)),
        out_shape=jax.ShapeDtypeStruct((BATCH * MEM_LEN, 2 * MQ_W), BF16),
        compiler_params=pltpu.CompilerParams(dimension_semantics=("arbitrary",)),
        name="mem_kv",
    )(mem2, g, b, w)


MERGE_TM = 512
MERGE_K1 = MERGE_TM // FFT_N2


def _pack_bf16_pair(v):
    n = v.shape[1] // 2
    bits = lax.bitcast_convert_type(v.astype(BF16).astype(F32), U32)
    return (bits[:, n:] & jnp.uint32(0xFFFF0000)) | (bits[:, :n] >> 16)


def _unpack_bf16_pair(p):
    lo = lax.bitcast_convert_type(p << 16, F32)
    hi = lax.bitcast_convert_type(p & jnp.uint32(0xFFFF0000), F32)
    return lo, hi


def _merge_kernel(x_ref, og_ref, zr_ref, zi_ref, mq_ref, gt_ref, kv_ref,
                  lng_ref, lnb_ref, wg_ref, ccs_ref, perm_ref, wf_ref, wm_ref, wo_ref, bo_ref,
                  l1g_ref, l1b_ref, wr2_ref, wrh_ref, br_ref,
                  h1_ref, h1p_ref, eidx_ref, topw_ref, cnt_ref):
    tm = MERGE_TM
    y_gla = _dot(og_ref[...], wg_ref[...])

    zr = zr_ref[...].reshape(tm, FN_W)
    zi = zi_ref[...].reshape(tm, FN_W)
    ys = []
    for g in range(FN_G):
        sl = slice(g * FN_GW, (g + 1) * FN_GW)
        ys.append(_dot(jnp.concatenate([zr[:, sl], zi[:, sl]], axis=1), ccs_ref[...]))
    yp = jnp.concatenate(ys, axis=1).astype(BF16)
    y_fn = _dot(_dot(perm_ref[...], yp).astype(BF16), wf_ref[...])

    oms = []
    for hd in range(MEM_H):
        sl = slice(hd * MEM_HD, (hd + 1) * MEM_HD)
        s = _dot_nt(mq_ref[:, sl], kv_ref[:, sl]) * (MEM_HD ** -0.5)
        s = s - jnp.max(s, axis=-1, keepdims=True)
        p = jnp.exp(s)
        p = p * (1.0 / jnp.sum(p, axis=-1, keepdims=True))
        oms.append(_dot(p.astype(BF16), kv_ref[:, MQ_W + hd * MEM_HD:MQ_W + (hd + 1) * MEM_HD]))
    y_mem = _dot(jnp.concatenate(oms, axis=1).astype(BF16), wm_ref[...])

    def gate(c):
        return 0.5 + 0.5 * jnp.tanh(0.5 * gt_ref[:, c * D:(c + 1) * D].astype(F32))

    merged = gate(0) * y_gla + gate(1) * y_fn + gate(2) * y_mem
    mix = _dot(merged.astype(BF16), wo_ref[...]) + bo_ref[...]
    h = _ln(x_ref[...], lng_ref[...], lnb_ref[...])
    h1 = _ln(DN_ALPHA * h + mix, l1g_ref[...], l1b_ref[...])
    h1_ref[...] = h1
    h1p_ref[...] = _pack_bf16_pair(h1)

    h_hi, h_lo = _split_bf16(h1)
    d2 = _dot(h_hi, wr2_ref[...])
    logits = d2[:, :LANES] + d2[:, LANES:] + _dot(h_lo, wrh_ref[...]) + br_ref[...]
    lane = lax.broadcasted_iota(I32, (tm, LANES), 1)
    l = logits
    vals, idxs = [], []
    for _ in range(TOP_K):
        m = jnp.max(l, axis=-1, keepdims=True)
        idx = jnp.min(jnp.where(l == m, lane, LANES), axis=-1, keepdims=True)
        vals.append(m)
        idxs.append(idx)
        l = jnp.where(lane == idx, -jnp.inf, l)
    es = [jnp.exp(v - vals[0]) for v in vals]
    den = es[0] + es[1] + es[2] + es[3]
    eo = jnp.zeros((tm, LANES), I32)
    wo = jnp.zeros((tm, LANES), F32)
    for k in range(TOP_K):
        eo = jnp.where(lane == k, idxs[k], eo)
        wo = jnp.where(lane == k, es[k] / den, wo)
    eidx_ref[...] = eo
    topw_ref[...] = wo

    @pl.when(pl.program_id(0) == 0)
    def _():
        cnt_ref[...] = jnp.zeros_like(cnt_ref)

    chosen = jnp.zeros((tm, LANES), F32)
    for k in range(TOP_K):
        chosen = chosen + jnp.where(lane == idxs[k], 1.0, 0.0)
    cnt_ref[...] += jnp.broadcast_to(jnp.sum(chosen, axis=0, keepdims=True), cnt_ref.shape)


def _merge(x2, og, z, mq, gates, kv, lng, lnb, wg, ccs, perm, wf, wm, wo, bo, l1g, l1b,
           wr2, wrh, br):
    tm = MERGE_TM
    per_b = SEQ // tm
    row = lambda i: (i, 0)
    const = lambda i: (0, 0)
    zblk = (None, None, FFT_N2, MERGE_K1, FN_W)
    outs = (
        jax.ShapeDtypeStruct((T, D), F32),
        jax.ShapeDtypeStruct((T, D // 2), U32),
        jax.ShapeDtypeStruct((T, LANES), I32),
        jax.ShapeDtypeStruct((T, LANES), F32),
        jax.ShapeDtypeStruct((8, LANES), F32),
    )
    return pl.pallas_call(
        _merge_kernel,
        grid=(T // tm,),
        in_specs=[
            pl.BlockSpec((tm, D), row),
            pl.BlockSpec((tm, D), row),
            pl.BlockSpec(zblk, lambda i: (i // per_b, 0, 0, i % per_b, 0)),
            pl.BlockSpec(zblk, lambda i: (i // per_b, 1, 0, i % per_b, 0)),
            pl.BlockSpec((tm, MQ_W), row),
            pl.BlockSpec((tm, 3 * D), lambda i: (i, 1)),
            pl.BlockSpec((MEM_LEN, 2 * MQ_W), lambda i: (i // per_b, 0)),
            pl.BlockSpec((1, D), const), pl.BlockSpec((1, D), const),
            pl.BlockSpec((D, D), const),
            pl.BlockSpec((2 * FN_GW, FN_GW), const),
            pl.BlockSpec((tm, tm), const),
            pl.BlockSpec((FN_W, D), const),
            pl.BlockSpec((MQ_W, D), const),
            pl.BlockSpec((D, D), const),
            pl.BlockSpec((1, D), const),
            pl.BlockSpec((1, D), const), pl.BlockSpec((1, D), const),
            pl.BlockSpec((D, 2 * LANES), const),
            pl.BlockSpec((D, LANES), const),
            pl.BlockSpec((1, LANES), const),
        ],
        out_specs=[
            pl.BlockSpec((tm, D), row),
            pl.BlockSpec((tm, D // 2), row),
            pl.BlockSpec((tm, LANES), row),
            pl.BlockSpec((tm, LANES), row),
            pl.BlockSpec((8, LANES), const),
        ],
        out_shape=outs,
        compiler_params=pltpu.CompilerParams(
            dimension_semantics=("arbitrary",),
            vmem_limit_bytes=58 * MIB),
        name="merge_ln1_router",
    )(x2, og, z, z, mq, gates, kv, lng, lnb, wg, ccs, perm, wf, wm, wo, bo, l1g, l1b,
      wr2, wrh, br)


PLAN_TP = 1024


def _expert_onehots(e, lane):
    onehots = [lane == e[:, k:k + 1] for k in range(TOP_K)]
    mf = jnp.zeros(lane.shape, F32)
    for oh in onehots:
        mf = mf + jnp.where(oh, 1.0, 0.0)
    return onehots, mf


def _plan_kernel(e_ref, tot_ref, dest_ref, cnt_ref, off_ref):
    i = pl.program_id(0)
    tp = PLAN_TP
    lane = lax.broadcasted_iota(I32, (tp, LANES), 1)
    onehots, mf = _expert_onehots(e_ref[...], lane)

    @pl.when(i == 0)
    def _():
        tot = tot_ref[0:1, :]
        padded = jnp.floor((tot + (MOE_BM - 1)) * (1.0 / MOE_BM)) * MOE_BM
        lane1 = lax.broadcasted_iota(I32, (1, LANES), 1)
        inc = padded
        for s in (1, 2, 4, 8, 16, 32, 64):
            inc = inc + jnp.where(lane1 >= s, pltpu.roll(inc, s, 1), 0.0)
        off_ref[...] = inc - padded
        cnt_ref[...] = jnp.zeros_like(cnt_ref)

    ri = lax.broadcasted_iota(I32, (tp, tp), 0)
    ci = lax.broadcasted_iota(I32, (tp, tp), 1)
    ltri = jnp.where(ri > ci, 1.0, 0.0).astype(BF16)
    rank = _dot(ltri, mf.astype(BF16)) + cnt_ref[...] + off_ref[...]
    out = jnp.zeros((tp, LANES), I32)
    for k in range(TOP_K):
        dk = jnp.sum(jnp.where(onehots[k], rank, 0.0), axis=-1, keepdims=True)
        out = jnp.where(lane == k, dk.astype(I32), out)
    dest_ref[...] = out
    cnt_ref[...] += jnp.sum(mf, axis=0, keepdims=True)


def _plan(eidx, cnt):
    tp = PLAN_TP
    return pl.pallas_call(
        _plan_kernel,
        grid=(T // tp,),
        in_specs=[pl.BlockSpec((tp, LANES), lambda i: (i, 0)),
                  pl.BlockSpec((8, LANES), lambda i: (0, 0))],
        out_specs=pl.BlockSpec((tp, LANES), lambda i: (i, 0)),
        out_shape=jax.ShapeDtypeStruct((T, LANES), I32),
        scratch_shapes=[pltpu.VMEM((1, LANES), F32), pltpu.VMEM((1, LANES), F32)],
        compiler_params=pltpu.CompilerParams(dimension_semantics=("arbitrary",)),
        name="route_plan",
    )(eidx, cnt)


MOE_BM = 512
MOE_NW = A_ROWS // MOE_BM + N_EXP
XS_ROWS = MOE_NW * MOE_BM


def _expert_kernel(we_ref, wb_ref, wv_ref, wfe_ref, wsl_ref, wnx_ref,
                   x_ref, wgu_hbm, bgu_ref, wdn_hbm, bdn_ref, o_ref,
                   wgu_f32, wdn_f32, wgu_bf, wdn_bf, sems):
    w = pl.program_id(0)
    e = we_ref[w]

    def weight_copies(expert, slot):
        return (pltpu.make_async_copy(wgu_hbm.at[expert], wgu_f32.at[slot], sems.at[slot, 0]),
                pltpu.make_async_copy(wdn_hbm.at[expert], wdn_f32.at[slot], sems.at[slot, 1]))

    @pl.when(w == 0)
    def _():
        for cp_ in weight_copies(e, 0):
            cp_.start()

    @pl.when(wfe_ref[w] == 1)
    def _():
        slot = wsl_ref[w]
        for cp_ in weight_copies(e, slot):
            cp_.wait()
        wgu_bf[...] = wgu_f32[slot].astype(BF16)
        wdn_bf[...] = wdn_f32[slot].astype(BF16)
        nxt = wnx_ref[w]

        @pl.when(nxt >= 0)
        def _():
            for cp_ in weight_copies(nxt, 1 - slot):
                cp_.start()

    def ffn(rows):
        xlo, xhi = _unpack_bf16_pair(x_ref[rows, :])
        half = D // 2
        gu = (_dot(xlo.astype(BF16), wgu_bf[:half, :]) + _dot(xhi.astype(BF16), wgu_bf[half:, :])
              + bgu_ref[pl.ds(e, 1), :])
        gate = jnp.minimum(gu[:, :D_FF], SW_LIMIT)
        up = jnp.clip(gu[:, D_FF:], -SW_LIMIT, SW_LIMIT)
        act = (up + 1.0) * (gate * jax.nn.sigmoid(SW_ALPHA * gate))
        out = _dot(act.astype(BF16), wdn_bf[...]) + bdn_ref[pl.ds(e, 1), :]
        o_ref[rows, :] = _pack_bf16_pair(out)

    @pl.when(wv_ref[w] == 1)
    def _():
        ffn(slice(None))

    @pl.when(wv_ref[w] == 2)
    def _():
        ffn(slice(0, MOE_BM // 2))


def _experts(meta, xs, w_gu, b_gu, w_down, b_down):
    return pl.pallas_call(
        _expert_kernel,
        grid_spec=pltpu.PrefetchScalarGridSpec(
            num_scalar_prefetch=len(meta),
            grid=(MOE_NW,),
            in_specs=[
                pl.BlockSpec((MOE_BM, D // 2), lambda w, we, wb, *_: (wb[w], 0)),
                pl.BlockSpec(memory_space=pl.ANY),
                pl.BlockSpec((N_EXP, 2 * D_FF), lambda w, *_: (0, 0)),
                pl.BlockSpec(memory_space=pl.ANY),
                pl.BlockSpec((N_EXP, D), lambda w, *_: (0, 0)),
            ],
            out_specs=pl.BlockSpec((MOE_BM, D // 2), lambda w, we, wb, *_: (wb[w], 0)),
            scratch_shapes=[
                pltpu.VMEM((2, D, 2 * D_FF), F32),
                pltpu.VMEM((2, D_FF, D), F32),
                pltpu.VMEM((D, 2 * D_FF), BF16),
                pltpu.VMEM((D_FF, D), BF16),
                pltpu.SemaphoreType.DMA((2, 2)),
            ],
        ),
        out_shape=jax.ShapeDtypeStruct((XS_ROWS, D // 2), U32),
        compiler_params=pltpu.CompilerParams(
            dimension_semantics=("arbitrary",),
            vmem_limit_bytes=56 * MIB),
        name="moe_experts",
    )(*meta, xs, w_gu, b_gu, w_down, b_down)


def _work_items(counts):
    n_e = (counts + MOE_BM - 1) // MOE_BM
    item_end = jnp.cumsum(n_e)
    total = item_end[-1]
    w = jnp.arange(MOE_NW, dtype=I32)
    valid = w < total
    wc = jnp.minimum(w, total - 1)
    e_w = jnp.sum((item_end[None, :] <= wc[:, None]).astype(I32), axis=1)
    e_w = jnp.minimum(e_w, N_EXP - 1)
    rows_here = counts[e_w] - (wc - (item_end - n_e)[e_w]) * MOE_BM
    valid = jnp.where(valid, jnp.where(rows_here <= MOE_BM // 2, 2, 1), 0)
    prev_e = jnp.concatenate([jnp.full((1,), -1, I32), e_w[:-1]])
    fe = (e_w != prev_e).astype(I32)
    slot = (jnp.cumsum(fe) - 1) % 2
    first_at = jnp.where(fe == 1, w, MOE_NW)
    next_first = jnp.concatenate([lax.cummin(first_at, reverse=True)[1:],
                                  jnp.full((1,), MOE_NW, I32)])
    nxt = jnp.where(next_first < MOE_NW, e_w[jnp.minimum(next_first, MOE_NW - 1)], -1)
    return tuple(a.astype(I32) for a in (e_w, wc, valid, fe, slot, nxt))


COMB_TM = 512
SC_CORES = 2
SC_SUBCORES = 16
SC_WORKERS = SC_CORES * SC_SUBCORES
SC_CH = 64
COMB_GROUPS = 4
COMB_TG = T // COMB_GROUPS
SC_ROWS_PER_W = COMB_TG * TOP_K // SC_WORKERS
SC_NCH = SC_ROWS_PER_W // SC_CH


def _sc_gather(table, idx3):
    mesh = plsc.VectorSubcoreMesh(core_axis_name="c", subcore_axis_name="s")

    @functools.partial(
        pl.kernel, mesh=mesh,
        out_type=jax.ShapeDtypeStruct((COMB_TG * TOP_K, D // 2), U32),
        scratch_types=[
            pltpu.VMEM((SC_NCH, SC_CH), I32),
            pltpu.VMEM((2, SC_CH, D // 2), U32),
            pltpu.SemaphoreType.DMA((2,)),
            pltpu.SemaphoreType.DMA((2,)),
        ],
    )
    def k(table_hbm, idx_hbm, out_hbm, idx_v, rows_v, gsem, psem):
        wid = lax.axis_index("s") * SC_CORES + lax.axis_index("c")
        base = wid * SC_ROWS_PER_W
        pltpu.sync_copy(idx_hbm.at[wid], idx_v)

        def gather(j, b):
            return pltpu.make_async_copy(table_hbm.at[idx_v.at[j]], rows_v.at[b], gsem.at[b])

        def put(j, b):
            return pltpu.make_async_copy(rows_v.at[b], out_hbm.at[pl.ds(base + j * SC_CH, SC_CH)],
                                         psem.at[b])

        gather(0, 0).start()

        @pl.loop(0, SC_NCH, step=2)
        def _(j0):
            for b in range(2):
                j = j0 + b

                @pl.when(j + 1 < SC_NCH)
                def _():
                    @pl.when(j >= 1)
                    def _():
                        put(j - 1, 1 - b).wait()
                    gather(j + 1, 1 - b).start()

                gather(j, b).wait()
                put(j, b).start()

        put(SC_NCH - 2, 0).wait()
        put(SC_NCH - 1, 1).wait()

    return k(table, idx3)


SCD_TOK_PER_W = T // SC_WORKERS
SCD_NCH = SCD_TOK_PER_W // SC_CH


def _sc_dispatch(h1p, idx4):
    mesh = plsc.VectorSubcoreMesh(core_axis_name="c", subcore_axis_name="s")

    @functools.partial(
        pl.kernel, mesh=mesh,
        out_type=jax.ShapeDtypeStruct((XS_ROWS, D // 2), U32),
        scratch_types=[
            pltpu.VMEM((SCD_NCH * TOP_K, SC_CH), I32),
            pltpu.VMEM((2, SC_CH, D // 2), U32),
            pltpu.SemaphoreType.DMA((2,)),
            pltpu.SemaphoreType.DMA((2,)),
        ],
    )
    def k(h_hbm, idx_hbm, xs_hbm, idx_v, rows_v, gsem, psem):
        wid = lax.axis_index("s") * SC_CORES + lax.axis_index("c")
        base = wid * SCD_TOK_PER_W
        pltpu.sync_copy(idx_hbm.at[wid], idx_v)

        def get(c, b):
            return pltpu.make_async_copy(h_hbm.at[pl.ds(base + c * SC_CH, SC_CH)], rows_v.at[b],
                                         gsem.at[b])

        def puts(c, b):
            return [pltpu.make_async_copy(rows_v.at[b], xs_hbm.at[idx_v.at[c * TOP_K + kk]],
                                          psem.at[b]) for kk in range(TOP_K)]

        get(0, 0).start()

        @pl.loop(0, SCD_NCH, step=2)
        def _(c0):
            for b in range(2):
                c = c0 + b

                @pl.when(c + 1 < SCD_NCH)
                def _():
                    @pl.when(c >= 1)
                    def _():
                        for cp_ in puts(c - 1, 1 - b):
                            cp_.wait()
                    get(c + 1, 1 - b).start()

                get(c, b).wait()
                for cp_ in puts(c, b):
                    cp_.start()

        for cp_ in puts(SCD_NCH - 2, 0) + puts(SCD_NCH - 1, 1):
            cp_.wait()

    return k(h1p, idx4)


def _combine_dense_kernel(g_ref, h1_ref, tw_ref, lg_ref, lb_ref, o_ref):
    tw = tw_ref[...]
    ylo = jnp.zeros((COMB_TM, D // 2), F32)
    yhi = jnp.zeros((COMB_TM, D // 2), F32)
    for k in range(TOP_K):
        lo, hi = _unpack_bf16_pair(g_ref[k])
        wk = tw[:, k:k + 1]
        ylo = ylo + lo * wk
        yhi = yhi + hi * wk
    ff = jnp.concatenate([ylo, yhi], axis=1)
    o_ref[...] = _ln(DN_ALPHA * h1_ref[...] + ff, lg_ref[...], lb_ref[...])


def _combine_dense(g4, h1, topw, g, b, group):
    tm = COMB_TM
    t0 = group * (COMB_TG // tm)
    return pl.pallas_call(
        _combine_dense_kernel,
        grid=(COMB_TG // tm,),
        in_specs=[
            pl.BlockSpec((TOP_K, tm, D // 2), lambda i: (0, i, 0)),
            pl.BlockSpec((tm, D), lambda i: (t0 + i, 0)),
            pl.BlockSpec((tm, LANES), lambda i: (t0 + i, 0)),
            pl.BlockSpec((1, D), lambda i: (0, 0)),
            pl.BlockSpec((1, D), lambda i: (0, 0)),
        ],
        out_specs=pl.BlockSpec((tm, D), lambda i: (t0 + i, 0)),
        out_shape=jax.ShapeDtypeStruct((T, D), F32),
        input_output_aliases={1: 0},
        compiler_params=pltpu.CompilerParams(
            dimension_semantics=("arbitrary",),
            vmem_limit_bytes=40 * MIB),
        name="moe_combine_dense_ln2",
    )(g4, h1, topw, g, b)


def _pad_cols(a, n):
    return jnp.pad(a, ((0, 0), (0, n - a.shape[1])))


def kernel(x, mem, ln_in_g, ln_in_b, ln_mem_g, ln_mem_b, w_in, b_in, w_decay_f, b_decay_f,
           w_decay_b, b_decay_b, gla_norm_g, w_br_gla, w_br_fnet, w_br_mem, w_mem_kv, w_out,
           b_out, ln1_g, ln1_b, w_router, b_router, w_gu, b_gu, w_down, b_down, ln2_g, ln2_b):
    assert x.shape == (BATCH, SEQ, D) and w_in.shape[0] == 1
    row = lambda a: a.reshape(1, -1)
    x2 = x.reshape(T, D)
    w_in0, b_in0 = w_in[0], b_in[0]
    c_lr, c_fn, c_mq, c_gt = 3072, 3072 + 2 * GLA_LR, 3104 + FN_W, 3104 + FN_W + MQ_W
    w_main = jnp.concatenate([w_in0[:, :c_lr], w_in0[:, c_gt:]], axis=1).astype(BF16)
    b_main = row(jnp.concatenate([b_in0[:c_lr], b_in0[c_gt:]]))
    w_lr = _pad_cols(w_in0[:, c_lr:c_fn], LANES).astype(BF16)
    b_lr = _pad_cols(row(b_in0[c_lr:c_fn]), LANES)
    w_mq = w_in0[:, c_mq:c_gt].astype(BF16)
    b_mq = row(b_in0[c_mq:c_gt])
    w_fn = w_in0[:, c_fn:c_mq].astype(BF16)
    b_fn = row(b_in0[c_fn:c_mq])
    lng, lnb = row(ln_in_g), row(ln_in_b)

    proj, mq, lr = _inproj(x2, lng, lnb, w_main, b_main, w_mq, b_mq, w_lr, b_lr)

    zpad = jnp.zeros((LANES - 2 * GLA_LR, GLA_H * GLA_DK), F32)
    zlr = jnp.zeros((GLA_LR, GLA_H * GLA_DK), F32)
    wdf = jnp.concatenate([w_decay_f[0], zlr, zpad], axis=0).reshape(LANES, GLA_H, GLA_DK)
    wdb = jnp.concatenate([zlr, w_decay_b[0], zpad], axis=0).reshape(LANES, GLA_H, GLA_DK)
    wd = jnp.concatenate([wdf, wdb], axis=2).reshape(LANES, GLA_H * 2 * GLA_DK).astype(BF16)
    wd = jnp.concatenate([wd, wd], axis=0)
    bd = jnp.concatenate([b_decay_f[0].reshape(GLA_H, GLA_DK),
                          b_decay_b[0].reshape(GLA_H, GLA_DK)], axis=1).reshape(1, -1)
    og = _gla(proj, lr, wd, bd, row(gla_norm_g[0]))

    fbig, cwt, swt, f2, ccs, perm = _dft_tables(MERGE_TM)
    x4 = x.reshape(BATCH, FFT_N2, FFT_N1, D)
    z = _fft2(_fft1(x4, lng, lnb, w_fn, b_fn, fbig, cwt, swt), f2)

    kv = _memkv(mem.reshape(BATCH * MEM_LEN, D), row(ln_mem_g), row(ln_mem_b),
                w_mem_kv[0].astype(BF16))

    w_r = _pad_cols(w_router[0], LANES)
    wr_hi = w_r.astype(BF16)
    wr_lo = (w_r - wr_hi.astype(F32)).astype(BF16)
    b_r = jnp.concatenate([row(b_router[0]),
                           jnp.full((1, LANES - N_EXP), NEG_BIG, F32)], axis=1)
    h1, h1p, eidx, topw, cnt = _merge(
        x2, og, z, mq, proj, kv, lng, lnb,
        w_br_gla[0].astype(BF16), ccs, perm, w_br_fnet[0].astype(BF16),
        w_br_mem[0].astype(BF16), w_out[0].astype(BF16), row(b_out[0]),
        row(ln1_g[0]), row(ln1_b[0]), jnp.concatenate([wr_hi, wr_lo], axis=1), wr_hi, b_r)

    dest = _plan(eidx, cnt)
    counts = cnt[0, :N_EXP].astype(I32)
    dest_k = dest[:, :TOP_K]
    idx4 = dest_k.reshape(SC_WORKERS, SCD_NCH, SC_CH, TOP_K).transpose(0, 1, 3, 2)
    xs = _sc_dispatch(h1p, idx4.reshape(SC_WORKERS, SCD_NCH * TOP_K, SC_CH))
    ys = _experts(_work_items(counts), xs, w_gu[0], b_gu[0], w_down[0], b_down[0])
    out = h1
    for grp in range(COMB_GROUPS):
        dest_g = dest_k[grp * COMB_TG:(grp + 1) * COMB_TG].T.reshape(SC_WORKERS, SC_NCH, SC_CH)
        g4 = _sc_gather(ys, dest_g).reshape(TOP_K, COMB_TG, D // 2)
        out = _combine_dense(g4, out, topw, row(ln2_g[0]), row(ln2_b[0]), grp)
    return out.reshape(BATCH, SEQ, D)
```

```python
import functools
import math

import numpy as np
import jax
import jax.numpy as jnp
from jax import lax
from jax.experimental import pallas as pl
from jax.experimental.pallas import tpu as pltpu
from jax.experimental.pallas import tpu_sc as plsc

F32 = jnp.float32
BF16 = jnp.bfloat16
I32 = jnp.int32
U32 = jnp.uint32

D = 1024
BATCH = 4
SEQ = 4096
T = BATCH * SEQ
GLA_H = 4
GLA_DK = 128
GLA_DV = 256
GLA_LR = 16
GLA_TAU = 16.0
GLA_C = 64
FN_G = 4
FN_GW = 128
FN_W = 512
MEM_LEN = 256
MEM_H = 4
MEM_HD = 128
MQ_W = 512
N_EXP = 32
TOP_K = 4
D_FF = 1024
SW_LIMIT = 7.0
SW_ALPHA = 1.702
LN_EPS = 1e-5
RMS_EPS = 1e-6
DN_ALPHA = 2.0 ** 0.25
A_ROWS = T * TOP_K

FFT_N1 = 128
FFT_N2 = 32

LANES = 128
NEG_BIG = -1e30
MIB = 1024 * 1024


def _ln(x, g, b):
    mu = jnp.mean(x, axis=-1, keepdims=True)
    xc = x - mu
    var = jnp.mean(xc * xc, axis=-1, keepdims=True)
    return xc * lax.rsqrt(var + LN_EPS) * g + b


def _dot(a, b):
    return jnp.dot(a, b, preferred_element_type=F32)


def _dot_nt(a, b):
    return lax.dot_general(a, b, (((1,), (1,)), ((), ())), preferred_element_type=F32)


def _dot_tn(a, b):
    return lax.dot_general(a, b, (((0,), (0,)), ((), ())), preferred_element_type=F32)


def _split_bf16(a):
    hi = a.astype(BF16)
    return hi, (a - hi.astype(F32)).astype(BF16)


INPROJ_TM = 1024
INPROJ_TN = 3072


PROJ_W = 6 * 1024
WCAT_MQ = PROJ_W
WCAT_FN = WCAT_MQ + MQ_W
WCAT_LR = WCAT_FN + FN_W


def _inproj_kernel(x_ref, g_ref, b_ref, w_ref, bias_ref, wmq_ref, bmq_ref, wlr_ref, blr_ref,
                   proj_ref, mq_ref, lr_ref, hb_ref):
    @pl.when(pl.program_id(1) == 0)
    def _():
        hb = _ln(x_ref[...], g_ref[...], b_ref[...]).astype(BF16)
        hb_ref[...] = hb
        lr_ref[...] = _dot(hb, wlr_ref[...]) + blr_ref[...]
        mq_ref[...] = (_dot(hb, wmq_ref[...]) + bmq_ref[...]).astype(BF16)

    proj_ref[...] = (_dot(hb_ref[...], w_ref[...]) + bias_ref[...]).astype(BF16)


def _inproj(x2, ln_g, ln_b, w_cat, b_cat):
    tm, tn = INPROJ_TM, INPROJ_TN
    nj = PROJ_W // tn
    row = lambda i, j: (i, 0)
    const = lambda i, j: (0, 0)
    outs = (
        jax.ShapeDtypeStruct((T, PROJ_W), BF16),
        jax.ShapeDtypeStruct((T, MQ_W), BF16),
        jax.ShapeDtypeStruct((T, LANES), F32),
    )
    return pl.pallas_call(
        _inproj_kernel,
        grid=(T // tm, nj),
        in_specs=[
            pl.BlockSpec((tm, D), row),
            pl.BlockSpec((1, D), const),
            pl.BlockSpec((1, D), const),
            pl.BlockSpec((D, tn), lambda i, j: (0, j)),
            pl.BlockSpec((1, tn), lambda i, j: (0, j)),
            pl.BlockSpec((D, MQ_W), lambda i, j: (0, WCAT_MQ // MQ_W)),
            pl.BlockSpec((1, MQ_W), lambda i, j: (0, WCAT_MQ // MQ_W)),
            pl.BlockSpec((D, LANES), lambda i, j: (0, WCAT_LR // LANES)),
            pl.BlockSpec((1, LANES), lambda i, j: (0, WCAT_LR // LANES)),
        ],
        out_specs=[
            pl.BlockSpec((tm, tn), lambda i, j: (i, j)),
            pl.BlockSpec((tm, MQ_W), row),
            pl.BlockSpec((tm, LANES), row),
        ],
        out_shape=outs,
        scratch_shapes=[pltpu.VMEM((tm, D), BF16)],
        compiler_params=pltpu.CompilerParams(
            dimension_semantics=("arbitrary", "arbitrary"),
            vmem_limit_bytes=48 * MIB),
        name="ln_inproj",
    )(x2, ln_g, ln_b, w_cat, b_cat, w_cat, b_cat, w_cat, b_cat)


GLA_BULK = 256
GLA_FIN = 512
GLA_NCH = SEQ // GLA_C
GLA_CPB = GLA_BULK // GLA_C


def _gla_kernel(q_ref, k_ref, v_ref, r_ref, lr_ref, wd_ref, bd_ref, g_ref, cs_ref, o_ref,
                acc_ref, qin_ref, kin_ref, kst_ref, dec_ref, u_ref, stf_ref, stb_ref):
    C = GLA_C
    G = GLA_BULK
    DK = GLA_DK
    NG = SEQ // G
    scale = DK ** -0.5
    ii = lax.broadcasted_iota(I32, (G, G), 0)
    jj = lax.broadcasted_iota(I32, (G, G), 1)
    same = (ii // C) == (jj // C)
    lower = jnp.logical_and(same, ii >= jj)
    upper = jnp.logical_and(same, ii <= jj)
    is_fwd = lax.broadcasted_iota(I32, (G, 2 * DK), 1) < DK
    chunk_of_row = lax.broadcasted_iota(I32, (G, DK), 0) // C

    def stage_a(gi):
        rows = pl.ds(pl.multiple_of(gi * G, G), G)
        z = _dot(jnp.concatenate(_split_bf16(lr_ref[rows, :]), axis=1), wd_ref[...]) + bd_ref[...]
        la = -(jnp.maximum(-z, 0.0) + jnp.log(1.0 + jnp.exp(-jnp.abs(z)))) * (1.0 / GLA_TAU)
        la_hi, la_lo = _split_bf16(la)
        pre2 = _dot(cs_ref[...], jnp.concatenate([la_hi, la_lo], axis=1))
        pre = pre2[:, :2 * DK] + pre2[:, 2 * DK:]
        blast = jnp.concatenate(
            [jnp.broadcast_to(pre[ci * C + C - 1:ci * C + C, :], (C, 2 * DK))
             for ci in range(GLA_CPB)], axis=0)
        b = jnp.where(is_fwd, pre, blast - pre + la)
        qf32 = q_ref[rows, :].astype(F32)
        kf32 = k_ref[rows, :].astype(F32)
        q2 = jnp.concatenate([qf32, qf32], axis=1)
        k2 = jnp.concatenate([kf32, kf32], axis=1)
        qin_ref[rows, :] = (q2 * (scale * jnp.exp(b))).astype(BF16)
        kin_ref[rows, :] = (k2 * jnp.exp(-b)).astype(BF16)
        kst_ref[rows, :] = (k2 * jnp.exp(blast - b)).astype(BF16)
        dec = jnp.exp(blast)
        for ci in range(GLA_CPB):
            dec_ref[pl.ds(gi * GLA_CPB + ci, 1), :] = dec[ci * C:ci * C + 1, :]

    def stage_b(gi):
        rows = pl.ds(pl.multiple_of(gi * G, G), G)
        qi = qin_ref[rows, :]
        ki = kin_ref[rows, :]
        ks = kst_ref[rows, :]
        vb = v_ref[rows, :]
        att = (jnp.where(lower, _dot_nt(qi[:, :DK], ki[:, :DK]), 0.0)
               + jnp.where(upper, _dot_nt(qi[:, DK:], ki[:, DK:]), 0.0))
        acc_ref[rows, :] = _dot(att.astype(BF16), vb)
        ksb = jnp.concatenate(
            [jnp.where(chunk_of_row == ci, ks[:, d * DK:(d + 1) * DK], jnp.zeros((G, DK), BF16))
             for d in range(2) for ci in range(GLA_CPB)], axis=1)
        u = _dot_tn(vb, ksb)
        for d in range(2):
            for ci in range(GLA_CPB):
                col = (d * GLA_CPB + ci) * DK
                u_ref[d, gi * GLA_CPB + ci] = u[:, col:col + DK]

    stage_a(0)

    def bulk(gi, carry):
        stage_b(gi - 1)
        stage_a(gi)
        return carry

    lax.fori_loop(1, NG, bulk, 0)
    stage_b(NG - 1)

    stf_ref[...] = jnp.zeros_like(stf_ref)
    stb_ref[...] = jnp.zeros_like(stb_ref)

    def one(n, d, st_ref):
        lanes = slice(d * DK, (d + 1) * DK)
        rows = pl.ds(pl.multiple_of(n * C, C), C)
        st = st_ref[...]
        acc_ref[rows, :] += _dot_nt(qin_ref[rows, lanes], st.astype(BF16))
        st_ref[...] = st * dec_ref[pl.ds(n, 1), :][:, lanes] + u_ref[d, n]

    def step(i, carry):
        one(i, 0, stf_ref)
        one(GLA_NCH - 1 - i, 1, stb_ref)
        return carry

    lax.fori_loop(0, GLA_NCH, step, 0, unroll=4)

    def fin(gi, carry):
        rows = pl.ds(pl.multiple_of(gi * GLA_FIN, GLA_FIN), GLA_FIN)
        o = acc_ref[rows, :]
        o = o * lax.rsqrt(jnp.mean(o * o, axis=-1, keepdims=True) + RMS_EPS) * g_ref[...]
        rg = r_ref[rows, :].astype(F32)
        o_ref[rows, :] = (o * (rg * jax.nn.sigmoid(rg))).astype(BF16)
        return carry

    lax.fori_loop(0, SEQ // GLA_FIN, fin, 0)


def _gla(proj, lr, wd, bd, g):
    i = np.arange(GLA_BULK)
    cs = ((i[:, None] // GLA_C) == (i[None, :] // GLA_C)) & (i[:, None] >= i[None, :])
    cs = jnp.asarray(cs, dtype=F32).astype(BF16)
    v_blk = 1024 // GLA_DV
    return pl.pallas_call(
        _gla_kernel,
        grid=(BATCH, GLA_H),
        in_specs=[
            pl.BlockSpec((SEQ, GLA_DK), lambda b, h: (b, h)),
            pl.BlockSpec((SEQ, GLA_DK), lambda b, h: (b, GLA_H + h)),
            pl.BlockSpec((SEQ, GLA_DV), lambda b, h: (b, v_blk + h)),
            pl.BlockSpec((SEQ, GLA_DV), lambda b, h: (b, 2 * v_blk + h)),
            pl.BlockSpec((SEQ, LANES), lambda b, h: (b, 0)),
            pl.BlockSpec((2 * LANES, 2 * GLA_DK), lambda b, h: (0, h)),
            pl.BlockSpec((1, 2 * GLA_DK), lambda b, h: (0, h)),
            pl.BlockSpec((1, GLA_DV), lambda b, h: (0, 0)),
            pl.BlockSpec((GLA_BULK, GLA_BULK), lambda b, h: (0, 0)),
        ],
        out_specs=pl.BlockSpec((SEQ, GLA_DV), lambda b, h: (b, h)),
        out_shape=jax.ShapeDtypeStruct((T, GLA_H * GLA_DV), BF16),
        scratch_shapes=[
            pltpu.VMEM((SEQ, GLA_DV), F32),
            pltpu.VMEM((SEQ, 2 * GLA_DK), BF16),
            pltpu.VMEM((SEQ, 2 * GLA_DK), BF16),
            pltpu.VMEM((SEQ, 2 * GLA_DK), BF16),
            pltpu.VMEM((GLA_NCH, 2 * GLA_DK), F32),
            pltpu.VMEM((2, GLA_NCH, GLA_DV, GLA_DK), F32),
            pltpu.VMEM((GLA_DV, GLA_DK), F32),
            pltpu.VMEM((GLA_DV, GLA_DK), F32),
        ],
        compiler_params=pltpu.CompilerParams(
            dimension_semantics=("arbitrary", "arbitrary"),
            vmem_limit_bytes=58 * MIB),
        name="gla",
    )(proj, proj, proj, proj, lr, wd, bd, g, cs)


FFT1_S = 16
FFT1_ROWS = FFT_N2 * FFT1_S
FFT2_KB = 8


def _fft1_kernel(x_ref, g_ref, b_ref, w_ref, bias_ref, fbig_ref, cw_ref, sw_ref, o_ref):
    xv = x_ref[...].reshape(FFT1_ROWS, D)
    hb = _ln(xv, g_ref[...], b_ref[...]).astype(BF16)
    fn = (_dot(hb, w_ref[...]) + bias_ref[...]).astype(BF16)
    a = _dot(fbig_ref[...], fn)
    ar = a[:FFT1_ROWS]
    ai = a[FFT1_ROWS:]
    cw = jnp.concatenate([cw_ref[...]] * (FN_W // LANES), axis=1)
    sw = jnp.concatenate([sw_ref[...]] * (FN_W // LANES), axis=1)
    o_ref[0] = (ar * cw + ai * sw).reshape(FFT_N2, FFT1_S, FN_W).astype(BF16)
    o_ref[1] = (ai * cw - ar * sw).reshape(FFT_N2, FFT1_S, FN_W).astype(BF16)


def _fft1(x4, ln_g, ln_b, w_cat, b_cat, fbig, cwt, swt):
    s = FFT1_S
    const = lambda b, j: (0, 0)
    return pl.pallas_call(
        _fft1_kernel,
        grid=(BATCH, FFT_N1 // s),
        in_specs=[
            pl.BlockSpec((None, FFT_N2, s, D), lambda b, j: (b, 0, j, 0)),
            pl.BlockSpec((1, D), const),
            pl.BlockSpec((1, D), const),
            pl.BlockSpec((D, FN_W), lambda b, j: (0, WCAT_FN // FN_W)),
            pl.BlockSpec((1, FN_W), lambda b, j: (0, WCAT_FN // FN_W)),
            pl.BlockSpec((2 * FFT1_ROWS, FFT1_ROWS), const),
            pl.BlockSpec((None, FFT1_ROWS, LANES), lambda b, j: (j, 0, 0)),
            pl.BlockSpec((None, FFT1_ROWS, LANES), lambda b, j: (j, 0, 0)),
        ],
        out_specs=pl.BlockSpec((None, 2, FFT_N2, s, FN_W), lambda b, j: (b, 0, 0, j, 0)),
        out_shape=jax.ShapeDtypeStruct((BATCH, 2, FFT_N2, FFT_N1, FN_W), BF16),
        compiler_params=pltpu.CompilerParams(
            dimension_semantics=("arbitrary", "arbitrary"),
            vmem_limit_bytes=40 * MIB),
        name="fft_stage1",
    )(x4, ln_g, ln_b, w_cat, b_cat, fbig, cwt, swt)


def _fft2_kernel(d_ref, f2_ref, o_ref):
    f2 = f2_ref[...]
    for kk in range(FFT2_KB):
        z = _dot(f2, jnp.concatenate([d_ref[0, kk], d_ref[1, kk]], axis=0))
        o_ref[0, kk] = z[:FFT_N1].astype(BF16)
        o_ref[1, kk] = z[FFT_N1:].astype(BF16)


def _fft2(dmat, f2):
    kb = FFT2_KB
    blk = (None, 2, kb, FFT_N1, FN_W)
    return pl.pallas_call(
        _fft2_kernel,
        grid=(BATCH, FFT_N2 // kb),
        in_specs=[
            pl.BlockSpec(blk, lambda b, j: (b, 0, j, 0, 0)),
            pl.BlockSpec((2 * FFT_N1, 2 * FFT_N1), lambda b, j: (0, 0)),
        ],
        out_specs=pl.BlockSpec(blk, lambda b, j: (b, 0, j, 0, 0)),
        out_shape=jax.ShapeDtypeStruct((BATCH, 2, FFT_N2, FFT_N1, FN_W), BF16),
        compiler_params=pltpu.CompilerParams(
            dimension_semantics=("arbitrary", "arbitrary")),
        name="fft_stage2",
    )(dmat, f2)


def _dft_tables(merge_tm):
    s = FFT1_S
    n2 = np.arange(FFT_N2, dtype=np.float64)
    n1 = np.arange(FFT_N1, dtype=np.float64)
    th = 2.0 * np.pi * np.outer(n2, n2) / FFT_N2
    f1 = np.stack([np.cos(th), -np.sin(th)]) / math.sqrt(SEQ)
    fbig = np.einsum("rkn,st->rksnt", f1, np.eye(s)).reshape(2 * FFT_N2 * s, FFT_N2 * s)
    tw = 2.0 * np.pi * np.outer(n2, n1) / SEQ
    tw = tw.reshape(FFT_N2, FFT_N1 // s, s).transpose(1, 0, 2).reshape(FFT_N1 // s, FFT_N2 * s)
    cwt = np.broadcast_to(np.cos(tw)[:, :, None], tw.shape + (LANES,))
    swt = np.broadcast_to(np.sin(tw)[:, :, None], tw.shape + (LANES,))
    th1 = 2.0 * np.pi * np.outer(n1, n1) / FFT_N1
    c1, s1 = np.cos(th1), np.sin(th1)
    f2 = np.block([[c1, s1], [-s1, c1]])
    cc = np.arange(FN_GW, dtype=np.float64)
    thc = 2.0 * np.pi * np.outer(cc, cc) / FN_GW
    ccs = np.concatenate([np.cos(thc), np.sin(thc)], axis=0) / math.sqrt(FN_GW)
    k1n = merge_tm // FFT_N2
    r = np.arange(merge_tm)
    perm = np.zeros((merge_tm, merge_tm))
    perm[r, (r % FFT_N2) * k1n + r // FFT_N2] = 1.0
    as32 = lambda a: jnp.asarray(np.ascontiguousarray(a), dtype=F32)
    return (as32(fbig).astype(BF16), as32(cwt), as32(swt), as32(f2).astype(BF16),
            as32(ccs).astype(BF16), as32(perm).astype(BF16))


def _memkv_kernel(m_ref, g_ref, b_ref, w_ref, o_ref):
    mn = _ln(m_ref[...], g_ref[...], b_ref[...]).astype(BF16)
    o_ref[...] = _dot(mn, w_ref[...]).astype(BF16)


def _memkv(mem2, g, b, w):
    return pl.pallas_call(
        _memkv_kernel,
        grid=(BATCH,),
        in_specs=[
            pl.BlockSpec((MEM_LEN, D), lambda i: (i, 0)),
            pl.BlockSpec((1, D), lambda i: (0, 0)),
            pl.BlockSpec((1, D), lambda i: (0, 0)),
            pl.BlockSpec((D, 2 * MQ_W), lambda i: (0, 0)),
        ],
        out_specs=pl.BlockSpec((MEM_LEN, 2 * MQ_W), lambda i: (i, 0)),
        out_shape=jax.ShapeDtypeStruct((BATCH * MEM_LEN, 2 * MQ_W), BF16),
        compiler_params=pltpu.CompilerParams(dimension_semantics=("arbitrary",)),
        name="mem_kv",
    )(mem2, g, b, w)


MERGE_TM = 512
MERGE_K1 = MERGE_TM // FFT_N2


def _pack_bf16_pair(v):
    n = v.shape[1] // 2
    bits = lax.bitcast_convert_type(v.astype(BF16).astype(F32), U32)
    return (bits[:, n:] & jnp.uint32(0xFFFF0000)) | (bits[:, :n] >> 16)


def _unpack_bf16_pair(p):
    lo = lax.bitcast_convert_type(p << 16, F32)
    hi = lax.bitcast_convert_type(p & jnp.uint32(0xFFFF0000), F32)
    return lo, hi


def _merge_kernel(x_ref, og_ref, zr_ref, zi_ref, mq_ref, gt_ref, kv_ref,
                  lng_ref, lnb_ref, wg_ref, ccs_ref, perm_ref, wf_ref, wm_ref, wo_ref, bo_ref,
                  l1g_ref, l1b_ref, wr2_ref, wrh_ref, br_ref,
                  h1_ref, h1p_ref, eidx_ref, topw_ref, cnt_ref):
    tm = MERGE_TM
    y_gla = _dot(og_ref[...], wg_ref[...])

    zr = zr_ref[...].reshape(tm, FN_W)
    zi = zi_ref[...].reshape(tm, FN_W)
    ys = []
    for g in range(FN_G):
        sl = slice(g * FN_GW, (g + 1) * FN_GW)
        ys.append(_dot(jnp.concatenate([zr[:, sl], zi[:, sl]], axis=1), ccs_ref[...]))
    yp = jnp.concatenate(ys, axis=1).astype(BF16)
    y_fn = _dot(_dot(perm_ref[...], yp).astype(BF16), wf_ref[...])

    oms = []
    for hd in range(MEM_H):
        sl = slice(hd * MEM_HD, (hd + 1) * MEM_HD)
        s = _dot_nt(mq_ref[:, sl], kv_ref[:, sl]) * (MEM_HD ** -0.5)
        s = s - jnp.max(s, axis=-1, keepdims=True)
        p = jnp.exp(s)
        p = p * (1.0 / jnp.sum(p, axis=-1, keepdims=True))
        oms.append(_dot(p.astype(BF16), kv_ref[:, MQ_W + hd * MEM_HD:MQ_W + (hd + 1) * MEM_HD]))
    y_mem = _dot(jnp.concatenate(oms, axis=1).astype(BF16), wm_ref[...])

    def gate(c):
        return 0.5 + 0.5 * jnp.tanh(0.5 * gt_ref[:, c * D:(c + 1) * D].astype(F32))

    merged = gate(0) * y_gla + gate(1) * y_fn + gate(2) * y_mem
    mix = _dot(merged.astype(BF16), wo_ref[...]) + bo_ref[...]
    h = _ln(x_ref[...], lng_ref[...], lnb_ref[...])
    h1 = _ln(DN_ALPHA * h + mix, l1g_ref[...], l1b_ref[...])
    h1_ref[...] = h1
    h1p_ref[...] = _pack_bf16_pair(h1)

    h_hi, h_lo = _split_bf16(h1)
    d2 = _dot(h_hi, wr2_ref[...])
    logits = d2[:, :LANES] + d2[:, LANES:] + _dot(h_lo, wrh_ref[...]) + br_ref[...]
    lane = lax.broadcasted_iota(I32, (tm, LANES), 1)
    l = logits
    vals, idxs = [], []
    for _ in range(TOP_K):
        m = jnp.max(l, axis=-1, keepdims=True)
        idx = jnp.min(jnp.where(l == m, lane, LANES), axis=-1, keepdims=True)
        vals.append(m)
        idxs.append(idx)
        l = jnp.where(lane == idx, -jnp.inf, l)
    es = [jnp.exp(v - vals[0]) for v in vals]
    den = es[0] + es[1] + es[2] + es[3]
    eo = jnp.zeros((tm, LANES), I32)
    wo = jnp.zeros((tm, LANES), F32)
    for k in range(TOP_K):
        eo = jnp.where(lane == k, idxs[k], eo)
        wo = jnp.where(lane == k, es[k] / den, wo)
    eidx_ref[...] = eo
    topw_ref[...] = wo

    @pl.when(pl.program_id(0) == 0)
    def _():
        cnt_ref[...] = jnp.zeros_like(cnt_ref)

    chosen = jnp.zeros((tm, LANES), F32)
    for k in range(TOP_K):
        chosen = chosen + jnp.where(lane == idxs[k], 1.0, 0.0)
    cnt_ref[...] += jnp.broadcast_to(jnp.sum(chosen, axis=0, keepdims=True), cnt_ref.shape)


def _merge(x2, og, z, mq, gates, kv, lng, lnb, wg, ccs, perm, wf, wm, wo, bo, l1g, l1b,
           wr2, wrh, br):
    tm = MERGE_TM
    per_b = SEQ // tm
    row = lambda i: (i, 0)
    const = lambda i: (0, 0)
    zblk = (None, None, FFT_N2, MERGE_K1, FN_W)
    outs = (
        jax.ShapeDtypeStruct((T, D), F32),
        jax.ShapeDtypeStruct((T, D // 2), U32),
        jax.ShapeDtypeStruct((T, LANES), I32),
        jax.ShapeDtypeStruct((T, LANES), F32),
        jax.ShapeDtypeStruct((8, LANES), F32),
    )
    return pl.pallas_call(
        _merge_kernel,
        grid=(T // tm,),
        in_specs=[
            pl.BlockSpec((tm, D), row),
            pl.BlockSpec((tm, D), row),
            pl.BlockSpec(zblk, lambda i: (i // per_b, 0, 0, i % per_b, 0)),
            pl.BlockSpec(zblk, lambda i: (i // per_b, 1, 0, i % per_b, 0)),
            pl.BlockSpec((tm, MQ_W), row),
            pl.BlockSpec((tm, 3 * D), lambda i: (i, 1)),
            pl.BlockSpec((MEM_LEN, 2 * MQ_W), lambda i: (i // per_b, 0)),
            pl.BlockSpec((1, D), const), pl.BlockSpec((1, D), const),
            pl.BlockSpec((D, D), const),
            pl.BlockSpec((2 * FN_GW, FN_GW), const),
            pl.BlockSpec((tm, tm), const),
            pl.BlockSpec((FN_W, D), const),
            pl.BlockSpec((MQ_W, D), const),
            pl.BlockSpec((D, D), const),
            pl.BlockSpec((1, D), const),
            pl.BlockSpec((1, D), const), pl.BlockSpec((1, D), const),
            pl.BlockSpec((D, 2 * LANES), const),
            pl.BlockSpec((D, LANES), const),
            pl.BlockSpec((1, LANES), const),
        ],
        out_specs=[
            pl.BlockSpec((tm, D), row),
            pl.BlockSpec((tm, D // 2), row),
            pl.BlockSpec((tm, LANES), row),
            pl.BlockSpec((tm, LANES), row),
            pl.BlockSpec((8, LANES), const),
        ],
        out_shape=outs,
        compiler_params=pltpu.CompilerParams(
            dimension_semantics=("arbitrary",),
            vmem_limit_bytes=58 * MIB),
        name="merge_ln1_router",
    )(x2, og, z, z, mq, gates, kv, lng, lnb, wg, ccs, perm, wf, wm, wo, bo, l1g, l1b,
      wr2, wrh, br)


PLAN_TP = 1024


def _expert_onehots(e, lane):
    onehots = [lane == e[:, k:k + 1] for k in range(TOP_K)]
    mf = jnp.zeros(lane.shape, F32)
    for oh in onehots:
        mf = mf + jnp.where(oh, 1.0, 0.0)
    return onehots, mf


def _plan_kernel(e_ref, tot_ref, dest_ref, cnt_ref, off_ref):
    i = pl.program_id(0)
    tp = PLAN_TP
    lane = lax.broadcasted_iota(I32, (tp, LANES), 1)
    onehots, mf = _expert_onehots(e_ref[...], lane)

    @pl.when(i == 0)
    def _():
        tot = tot_ref[0:1, :]
        padded = jnp.floor((tot + (MOE_BM - 1)) * (1.0 / MOE_BM)) * MOE_BM
        lane1 = lax.broadcasted_iota(I32, (1, LANES), 1)
        inc = padded
        for s in (1, 2, 4, 8, 16, 32, 64):
            inc = inc + jnp.where(lane1 >= s, pltpu.roll(inc, s, 1), 0.0)
        off_ref[...] = inc - padded
        cnt_ref[...] = jnp.zeros_like(cnt_ref)

    ri = lax.broadcasted_iota(I32, (tp, tp), 0)
    ci = lax.broadcasted_iota(I32, (tp, tp), 1)
    ltri = jnp.where(ri > ci, 1.0, 0.0).astype(BF16)
    rank = _dot(ltri, mf.astype(BF16)) + cnt_ref[...] + off_ref[...]
    out = jnp.zeros((tp, LANES), I32)
    for k in range(TOP_K):
        dk = jnp.sum(jnp.where(onehots[k], rank, 0.0), axis=-1, keepdims=True)
        out = jnp.where(lane == k, dk.astype(I32), out)
    dest_ref[...] = out
    cnt_ref[...] += jnp.sum(mf, axis=0, keepdims=True)


def _plan(eidx, cnt):
    tp = PLAN_TP
    return pl.pallas_call(
        _plan_kernel,
        grid=(T // tp,),
        in_specs=[pl.BlockSpec((tp, LANES), lambda i: (i, 0)),
                  pl.BlockSpec((8, LANES), lambda i: (0, 0))],
        out_specs=pl.BlockSpec((tp, LANES), lambda i: (i, 0)),
        out_shape=jax.ShapeDtypeStruct((T, LANES), I32),
        scratch_shapes=[pltpu.VMEM((1, LANES), F32), pltpu.VMEM((1, LANES), F32)],
        compiler_params=pltpu.CompilerParams(dimension_semantics=("arbitrary",)),
        name="route_plan",
    )(eidx, cnt)


MOE_BM = 512
MOE_NW = A_ROWS // MOE_BM + N_EXP
XS_ROWS = MOE_NW * MOE_BM


def _expert_kernel(we_ref, wb_ref, wv_ref, wfe_ref, wsl_ref, wnx_ref,
                   x_ref, wgu_hbm, bgu_ref, wdn_hbm, bdn_ref, o_ref,
                   wgu_f32, wdn_f32, wgu_bf, wdn_bf, sems):
    w = pl.program_id(0)
    e = we_ref[w]

    def weight_copies(expert, slot):
        return (pltpu.make_async_copy(wgu_hbm.at[expert], wgu_f32.at[slot], sems.at[slot, 0]),
                pltpu.make_async_copy(wdn_hbm.at[expert], wdn_f32.at[slot], sems.at[slot, 1]))

    @pl.when(w == 0)
    def _():
        for cp_ in weight_copies(e, 0):
            cp_.start()

    @pl.when(wfe_ref[w] == 1)
    def _():
        slot = wsl_ref[w]
        for cp_ in weight_copies(e, slot):
            cp_.wait()
        wgu_bf[...] = wgu_f32[slot].astype(BF16)
        wdn_bf[...] = wdn_f32[slot].astype(BF16)
        nxt = wnx_ref[w]

        @pl.when(nxt >= 0)
        def _():
            for cp_ in weight_copies(nxt, 1 - slot):
                cp_.start()

    def ffn(rows):
        xlo, xhi = _unpack_bf16_pair(x_ref[rows, :])
        half = D // 2
        gu = (_dot(xlo.astype(BF16), wgu_bf[:half, :]) + _dot(xhi.astype(BF16), wgu_bf[half:, :])
              + bgu_ref[pl.ds(e, 1), :])
        gate = jnp.minimum(gu[:, :D_FF], SW_LIMIT)
        up = jnp.clip(gu[:, D_FF:], -SW_LIMIT, SW_LIMIT)
        act = (up + 1.0) * (gate * jax.nn.sigmoid(SW_ALPHA * gate))
        out = _dot(act.astype(BF16), wdn_bf[...]) + bdn_ref[pl.ds(e, 1), :]
        o_ref[rows, :] = _pack_bf16_pair(out)

    @pl.when(wv_ref[w] == 1)
    def _():
        ffn(slice(None))

    @pl.when(wv_ref[w] == 2)
    def _():
        ffn(slice(0, MOE_BM // 2))


def _experts(meta, xs, w_gu, b_gu, w_down, b_down):
    return pl.pallas_call(
        _expert_kernel,
        grid_spec=pltpu.PrefetchScalarGridSpec(
            num_scalar_prefetch=len(meta),
            grid=(MOE_NW,),
            in_specs=[
                pl.BlockSpec((MOE_BM, D // 2), lambda w, we, wb, *_: (wb[w], 0)),
                pl.BlockSpec(memory_space=pl.ANY),
                pl.BlockSpec((N_EXP, 2 * D_FF), lambda w, *_: (0, 0)),
                pl.BlockSpec(memory_space=pl.ANY),
                pl.BlockSpec((N_EXP, D), lambda w, *_: (0, 0)),
            ],
            out_specs=pl.BlockSpec((MOE_BM, D // 2), lambda w, we, wb, *_: (wb[w], 0)),
            scratch_shapes=[
                pltpu.VMEM((2, D, 2 * D_FF), F32),
                pltpu.VMEM((2, D_FF, D), F32),
                pltpu.VMEM((D, 2 * D_FF), BF16),
                pltpu.VMEM((D_FF, D), BF16),
                pltpu.SemaphoreType.DMA((2, 2)),
            ],
        ),
        out_shape=jax.ShapeDtypeStruct((XS_ROWS, D // 2), U32),
        compiler_params=pltpu.CompilerParams(
            dimension_semantics=("arbitrary",),
            vmem_limit_bytes=56 * MIB),
        name="moe_experts",
    )(*meta, xs, w_gu, b_gu, w_down, b_down)


def _work_items(counts):
    n_e = (counts + MOE_BM - 1) // MOE_BM
    item_end = jnp.cumsum(n_e)
    total = item_end[-1]
    w = jnp.arange(MOE_NW, dtype=I32)
    valid = w < total
    wc = jnp.minimum(w, total - 1)
    e_w = jnp.sum((item_end[None, :] <= wc[:, None]).astype(I32), axis=1)
    e_w = jnp.minimum(e_w, N_EXP - 1)
    rows_here = counts[e_w] - (wc - (item_end - n_e)[e_w]) * MOE_BM
    valid = jnp.where(valid, jnp.where(rows_here <= MOE_BM // 2, 2, 1), 0)
    prev_e = jnp.concatenate([jnp.full((1,), -1, I32), e_w[:-1]])
    fe = (e_w != prev_e).astype(I32)
    slot = (jnp.cumsum(fe) - 1) % 2
    first_at = jnp.where(fe == 1, w, MOE_NW)
    next_first = jnp.concatenate([lax.cummin(first_at, reverse=True)[1:],
                                  jnp.full((1,), MOE_NW, I32)])
    nxt = jnp.where(next_first < MOE_NW, e_w[jnp.minimum(next_first, MOE_NW - 1)], -1)
    return tuple(a.astype(I32) for a in (e_w, wc, valid, fe, slot, nxt))


COMB_TM = 512
SC_CORES = 2
SC_SUBCORES = 16
SC_WORKERS = SC_CORES * SC_SUBCORES
SC_CH = 64
COMB_GROUPS = 4
COMB_TG = T // COMB_GROUPS
SC_ROWS_PER_W = COMB_TG * TOP_K // SC_WORKERS
SC_NCH = SC_ROWS_PER_W // SC_CH


def _sc_gather(table, idx3):
    mesh = plsc.VectorSubcoreMesh(core_axis_name="c", subcore_axis_name="s")

    @functools.partial(
        pl.kernel, mesh=mesh,
        out_type=jax.ShapeDtypeStruct((COMB_TG * TOP_K, D // 2), U32),
        scratch_types=[
            pltpu.VMEM((SC_NCH, SC_CH), I32),
            pltpu.VMEM((2, SC_CH, D // 2), U32),
            pltpu.SemaphoreType.DMA((2,)),
            pltpu.SemaphoreType.DMA((2,)),
        ],
    )
    def k(table_hbm, idx_hbm, out_hbm, idx_v, rows_v, gsem, psem):
        wid = lax.axis_index("s") * SC_CORES + lax.axis_index("c")
        base = wid * SC_ROWS_PER_W
        pltpu.sync_copy(idx_hbm.at[wid], idx_v)

        def gather(j, b):
            return pltpu.make_async_copy(table_hbm.at[idx_v.at[j]], rows_v.at[b], gsem.at[b])

        def put(j, b):
            return pltpu.make_async_copy(rows_v.at[b], out_hbm.at[pl.ds(base + j * SC_CH, SC_CH)],
                                         psem.at[b])

        gather(0, 0).start()

        @pl.loop(0, SC_NCH, step=2)
        def _(j0):
            for b in range(2):
                j = j0 + b

                @pl.when(j + 1 < SC_NCH)
                def _():
                    @pl.when(j >= 1)
                    def _():
                        put(j - 1, 1 - b).wait()
                    gather(j + 1, 1 - b).start()

                gather(j, b).wait()
                put(j, b).start()

        put(SC_NCH - 2, 0).wait()
        put(SC_NCH - 1, 1).wait()

    return k(table, idx3)


SCD_TOK_PER_W = T // SC_WORKERS
SCD_NCH = SCD_TOK_PER_W // SC_CH


def _sc_dispatch(h1p, idx4):
    mesh = plsc.VectorSubcoreMesh(core_axis_name="c", subcore_axis_name="s")

    @functools.partial(
        pl.kernel, mesh=mesh,
        out_type=jax.ShapeDtypeStruct((XS_ROWS, D // 2), U32),
        scratch_types=[
            pltpu.VMEM((SCD_NCH * TOP_K, SC_CH), I32),
            pltpu.VMEM((2, SC_CH, D // 2), U32),
            pltpu.SemaphoreType.DMA((2,)),
            pltpu.SemaphoreType.DMA((2,)),
        ],
    )
    def k(h_hbm, idx_hbm, xs_hbm, idx_v, rows_v, gsem, psem):
        wid = lax.axis_index("s") * SC_CORES + lax.axis_index("c")
        base = wid * SCD_TOK_PER_W
        pltpu.sync_copy(idx_hbm.at[wid], idx_v)

        def get(c, b):
            return pltpu.make_async_copy(h_hbm.at[pl.ds(base + c * SC_CH, SC_CH)], rows_v.at[b],
                                         gsem.at[b])

        def puts(c, b):
            return [pltpu.make_async_copy(rows_v.at[b], xs_hbm.at[idx_v.at[c * TOP_K + kk]],
                                          psem.at[b]) for kk in range(TOP_K)]

        get(0, 0).start()

        @pl.loop(0, SCD_NCH, step=2)
        def _(c0):
            for b in range(2):
                c = c0 + b

                @pl.when(c + 1 < SCD_NCH)
                def _():
                    @pl.when(c >= 1)
                    def _():
                        for cp_ in puts(c - 1, 1 - b):
                            cp_.wait()
                    get(c + 1, 1 - b).start()

                get(c, b).wait()
                for cp_ in puts(c, b):
                    cp_.start()

        for cp_ in puts(SCD_NCH - 2, 0) + puts(SCD_NCH - 1, 1):
            cp_.wait()

    return k(h1p, idx4)


def _combine_dense_kernel(g_ref, h1_ref, tw_ref, lg_ref, lb_ref, o_ref):
    tw = tw_ref[...]
    ylo = jnp.zeros((COMB_TM, D // 2), F32)
    yhi = jnp.zeros((COMB_TM, D // 2), F32)
    for k in range(TOP_K):
        lo, hi = _unpack_bf16_pair(g_ref[k])
        wk = tw[:, k:k + 1]
        ylo = ylo + lo * wk
        yhi = yhi + hi * wk
    ff = jnp.concatenate([ylo, yhi], axis=1)
    o_ref[...] = _ln(DN_ALPHA * h1_ref[...] + ff, lg_ref[...], lb_ref[...])


def _combine_dense(g4, h1, topw, g, b, group):
    tm = COMB_TM
    t0 = group * (COMB_TG // tm)
    return pl.pallas_call(
        _combine_dense_kernel,
        grid=(COMB_TG // tm,),
        in_specs=[
            pl.BlockSpec((TOP_K, tm, D // 2), lambda i: (0, i, 0)),
            pl.BlockSpec((tm, D), lambda i: (t0 + i, 0)),
            pl.BlockSpec((tm, LANES), lambda i: (t0 + i, 0)),
            pl.BlockSpec((1, D), lambda i: (0, 0)),
            pl.BlockSpec((1, D), lambda i: (0, 0)),
        ],
        out_specs=pl.BlockSpec((tm, D), lambda i: (t0 + i, 0)),
        out_shape=jax.ShapeDtypeStruct((T, D), F32),
        input_output_aliases={1: 0},
        compiler_params=pltpu.CompilerParams(
            dimension_semantics=("arbitrary",),
            vmem_limit_bytes=40 * MIB),
        name="moe_combine_dense_ln2",
    )(g4, h1, topw, g, b)


def _pad_cols(a, n):
    return jnp.pad(a, ((0, 0), (0, n - a.shape[1])))


def kernel(x, mem, ln_in_g, ln_in_b, ln_mem_g, ln_mem_b, w_in, b_in, w_decay_f, b_decay_f,
           w_decay_b, b_decay_b, gla_norm_g, w_br_gla, w_br_fnet, w_br_mem, w_mem_kv, w_out,
           b_out, ln1_g, ln1_b, w_router, b_router, w_gu, b_gu, w_down, b_down, ln2_g, ln2_b):
    assert x.shape == (BATCH, SEQ, D) and w_in.shape[0] == 1
    row = lambda a: a.reshape(1, -1)
    x2 = x.reshape(T, D)
    w_in0, b_in0 = w_in[0], b_in[0]
    c_lr, c_fn, c_mq, c_gt = 3072, 3072 + 2 * GLA_LR, 3104 + FN_W, 3104 + FN_W + MQ_W
    def reorder(a):
        pad = jnp.zeros(a.shape[:-1] + (LANES - 2 * GLA_LR,), a.dtype)
        return jnp.concatenate([a[..., :c_lr], a[..., c_gt:], a[..., c_mq:c_gt],
                                a[..., c_fn:c_mq], a[..., c_lr:c_fn], pad], axis=-1)

    w_cat = reorder(w_in0).astype(BF16)
    b_cat = row(reorder(b_in0))
    lng, lnb = row(ln_in_g), row(ln_in_b)

    proj, mq, lr = _inproj(x2, lng, lnb, w_cat, b_cat)

    zpad = jnp.zeros((LANES - 2 * GLA_LR, GLA_H * GLA_DK), F32)
    zlr = jnp.zeros((GLA_LR, GLA_H * GLA_DK), F32)
    wdf = jnp.concatenate([w_decay_f[0], zlr, zpad], axis=0).reshape(LANES, GLA_H, GLA_DK)
    wdb = jnp.concatenate([zlr, w_decay_b[0], zpad], axis=0).reshape(LANES, GLA_H, GLA_DK)
    wd = jnp.concatenate([wdf, wdb], axis=2).reshape(LANES, GLA_H * 2 * GLA_DK).astype(BF16)
    wd = jnp.concatenate([wd, wd], axis=0)
    bd = jnp.concatenate([b_decay_f[0].reshape(GLA_H, GLA_DK),
                          b_decay_b[0].reshape(GLA_H, GLA_DK)], axis=1).reshape(1, -1)
    og = _gla(proj, lr, wd, bd, row(gla_norm_g[0]))

    fbig, cwt, swt, f2, ccs, perm = _dft_tables(MERGE_TM)
    x4 = x.reshape(BATCH, FFT_N2, FFT_N1, D)
    z = _fft2(_fft1(x4, lng, lnb, w_cat, b_cat, fbig, cwt, swt), f2)

    kv = _memkv(mem.reshape(BATCH * MEM_LEN, D), row(ln_mem_g), row(ln_mem_b),
                w_mem_kv[0].astype(BF16))

    w_r = _pad_cols(w_router[0], LANES)
    wr_hi = w_r.astype(BF16)
    wr_lo = (w_r - wr_hi.astype(F32)).astype(BF16)
    b_r = jnp.concatenate([row(b_router[0]),
                           jnp.full((1, LANES - N_EXP), NEG_BIG, F32)], axis=1)
    h1, h1p, eidx, topw, cnt = _merge(
        x2, og, z, mq, proj, kv, lng, lnb,
        w_br_gla[0].astype(BF16), ccs, perm, w_br_fnet[0].astype(BF16),
        w_br_mem[0].astype(BF16), w_out[0].astype(BF16), row(b_out[0]),
        row(ln1_g[0]), row(ln1_b[0]), jnp.concatenate([wr_hi, wr_lo], axis=1), wr_hi, b_r)

    dest = _plan(eidx, cnt)
    counts = cnt[0, :N_EXP].astype(I32)
    dest_k = dest[:, :TOP_K]
    idx4 = dest_k.reshape(SC_WORKERS, SCD_NCH, SC_CH, TOP_K).transpose(0, 1, 3, 2)
    xs = _sc_dispatch(h1p, idx4.reshape(SC_WORKERS, SCD_NCH * TOP_K, SC_CH))
    ys = _experts(_work_items(counts), xs, w_gu[0], b_gu[0], w_down[0], b_down[0])
    out = h1
    for grp in range(COMB_GROUPS):
        dest_g = dest_k[grp * COMB_TG:(grp + 1) * COMB_TG].T.reshape(SC_WORKERS, SC_NCH, SC_CH)
        g4 = _sc_gather(ys, dest_g).reshape(TOP_K, COMB_TG, D // 2)
        out = _combine_dense(g4, out, topw, row(ln2_g[0]), row(ln2_b[0]), grp)
    return out.reshape(BATCH, SEQ, D)
```

```python
import functools
import math

import numpy as np
import jax
import jax.numpy as jnp
from jax import lax
from jax.experimental import pallas as pl
from jax.experimental.pallas import tpu as pltpu
from jax.experimental.pallas import tpu_sc as plsc

F32 = jnp.float32
BF16 = jnp.bfloat16
I32 = jnp.int32
U32 = jnp.uint32

D = 1024
BATCH = 4
SEQ = 4096
T = BATCH * SEQ
GLA_H = 4
GLA_DK = 128
GLA_DV = 256
GLA_LR = 16
GLA_TAU = 16.0
GLA_C = 64
FN_G = 4
FN_GW = 128
FN_W = 512
MEM_LEN = 256
MEM_H = 4
MEM_HD = 128
MQ_W = 512
N_EXP = 32
TOP_K = 4
D_FF = 1024
SW_LIMIT = 7.0
SW_ALPHA = 1.702
LN_EPS = 1e-5
RMS_EPS = 1e-6
DN_ALPHA = 2.0 ** 0.25
A_ROWS = T * TOP_K

FFT_N1 = 128
FFT_N2 = 32

LANES = 128
NEG_BIG = -1e30
MIB = 1024 * 1024


def _ln(x, g, b):
    mu = jnp.mean(x, axis=-1, keepdims=True)
    xc = x - mu
    var = jnp.mean(xc * xc, axis=-1, keepdims=True)
    return xc * lax.rsqrt(var + LN_EPS) * g + b


def _dot(a, b):
    return jnp.dot(a, b, preferred_element_type=F32)


def _dot_nt(a, b):
    return lax.dot_general(a, b, (((1,), (1,)), ((), ())), preferred_element_type=F32)


def _dot_tn(a, b):
    return lax.dot_general(a, b, (((0,), (0,)), ((), ())), preferred_element_type=F32)


def _split_bf16(a):
    hi = a.astype(BF16)
    return hi, (a - hi.astype(F32)).astype(BF16)


INPROJ_TM = 1024
INPROJ_TN = 3072


PROJ_W = 6 * 1024


def _inproj_kernel(x_ref, g_ref, b_ref, w_ref, bias_ref, wmq_ref, bmq_ref, wlr_ref, blr_ref,
                   proj_ref, mq_ref, lr_ref, hb_ref):
    @pl.when(pl.program_id(1) == 0)
    def _():
        hb = _ln(x_ref[...], g_ref[...], b_ref[...]).astype(BF16)
        hb_ref[...] = hb
        lr_ref[...] = _dot(hb, wlr_ref[...]) + blr_ref[...]
        mq_ref[...] = (_dot(hb, wmq_ref[...]) + bmq_ref[...]).astype(BF16)

    proj_ref[...] = (_dot(hb_ref[...], w_ref[...]) + bias_ref[...]).astype(BF16)


def _inproj(x2, ln_g, ln_b, w_main, b_main, w_mq, b_mq, w_lr, b_lr):
    tm, tn = INPROJ_TM, INPROJ_TN
    nj = PROJ_W // tn
    row = lambda i, j: (i, 0)
    const = lambda i, j: (0, 0)
    outs = (
        jax.ShapeDtypeStruct((T, PROJ_W), BF16),
        jax.ShapeDtypeStruct((T, MQ_W), BF16),
        jax.ShapeDtypeStruct((T, LANES), F32),
    )
    return pl.pallas_call(
        _inproj_kernel,
        grid=(T // tm, nj),
        in_specs=[
            pl.BlockSpec((tm, D), row),
            pl.BlockSpec((1, D), const),
            pl.BlockSpec((1, D), const),
            pl.BlockSpec((D, tn), lambda i, j: (0, j)),
            pl.BlockSpec((1, tn), lambda i, j: (0, j)),
            pl.BlockSpec((D, MQ_W), const),
            pl.BlockSpec((1, MQ_W), const),
            pl.BlockSpec((D, LANES), const),
            pl.BlockSpec((1, LANES), const),
        ],
        out_specs=[
            pl.BlockSpec((tm, tn), lambda i, j: (i, j)),
            pl.BlockSpec((tm, MQ_W), row),
            pl.BlockSpec((tm, LANES), row),
        ],
        out_shape=outs,
        scratch_shapes=[pltpu.VMEM((tm, D), BF16)],
        compiler_params=pltpu.CompilerParams(
            dimension_semantics=("arbitrary", "arbitrary"),
            vmem_limit_bytes=48 * MIB),
        name="ln_inproj",
    )(x2, ln_g, ln_b, w_main, b_main, w_mq, b_mq, w_lr, b_lr)


GLA_BULK = 256
GLA_FIN = 512
GLA_NCH = SEQ // GLA_C
GLA_CPB = GLA_BULK // GLA_C


def _gla_kernel(q_ref, k_ref, v_ref, r_ref, lr_ref, wd_ref, bd_ref, g_ref, cs_ref, o_ref,
                acc_ref, qin_ref, kin_ref, kst_ref, dec_ref, u_ref, stf_ref, stb_ref):
    C = GLA_C
    G = GLA_BULK
    DK = GLA_DK
    NG = SEQ // G
    scale = DK ** -0.5
    ii = lax.broadcasted_iota(I32, (G, G), 0)
    jj = lax.broadcasted_iota(I32, (G, G), 1)
    same = (ii // C) == (jj // C)
    lower = jnp.logical_and(same, ii >= jj)
    upper = jnp.logical_and(same, ii <= jj)
    is_fwd = lax.broadcasted_iota(I32, (G, 2 * DK), 1) < DK
    chunk_of_row = lax.broadcasted_iota(I32, (G, DK), 0) // C

    def stage_a(gi):
        rows = pl.ds(pl.multiple_of(gi * G, G), G)
        z = _dot(jnp.concatenate(_split_bf16(lr_ref[rows, :]), axis=1), wd_ref[...]) + bd_ref[...]
        la = -(jnp.maximum(-z, 0.0) + jnp.log(1.0 + jnp.exp(-jnp.abs(z)))) * (1.0 / GLA_TAU)
        la_hi, la_lo = _split_bf16(la)
        pre2 = _dot(cs_ref[...], jnp.concatenate([la_hi, la_lo], axis=1))
        pre = pre2[:, :2 * DK] + pre2[:, 2 * DK:]
        blast = jnp.concatenate(
            [jnp.broadcast_to(pre[ci * C + C - 1:ci * C + C, :], (C, 2 * DK))
             for ci in range(GLA_CPB)], axis=0)
        b = jnp.where(is_fwd, pre, blast - pre + la)
        qf32 = q_ref[rows, :].astype(F32)
        kf32 = k_ref[rows, :].astype(F32)
        q2 = jnp.concatenate([qf32, qf32], axis=1)
        k2 = jnp.concatenate([kf32, kf32], axis=1)
        qin_ref[rows, :] = (q2 * (scale * jnp.exp(b))).astype(BF16)
        kin_ref[rows, :] = (k2 * jnp.exp(-b)).astype(BF16)
        kst_ref[rows, :] = (k2 * jnp.exp(blast - b)).astype(BF16)
        dec = jnp.exp(blast)
        for ci in range(GLA_CPB):
            dec_ref[pl.ds(gi * GLA_CPB + ci, 1), :] = dec[ci * C:ci * C + 1, :]

    def stage_b(gi):
        rows = pl.ds(pl.multiple_of(gi * G, G), G)
        qi = qin_ref[rows, :]
        ki = kin_ref[rows, :]
        ks = kst_ref[rows, :]
        vb = v_ref[rows, :]
        att = (jnp.where(lower, _dot_nt(qi[:, :DK], ki[:, :DK]), 0.0)
               + jnp.where(upper, _dot_nt(qi[:, DK:], ki[:, DK:]), 0.0))
        acc_ref[rows, :] = _dot(att.astype(BF16), vb)
        ksb = jnp.concatenate(
            [jnp.where(chunk_of_row == ci, ks[:, d * DK:(d + 1) * DK], jnp.zeros((G, DK), BF16))
             for d in range(2) for ci in range(GLA_CPB)], axis=1)
        u = _dot_tn(vb, ksb)
        for d in range(2):
            for ci in range(GLA_CPB):
                col = (d * GLA_CPB + ci) * DK
                u_ref[d, gi * GLA_CPB + ci] = u[:, col:col + DK]

    stage_a(0)

    def bulk(gi, carry):
        stage_b(gi - 1)
        stage_a(gi)
        return carry

    lax.fori_loop(1, NG, bulk, 0)
    stage_b(NG - 1)

    stf_ref[...] = jnp.zeros_like(stf_ref)
    stb_ref[...] = jnp.zeros_like(stb_ref)

    def one(n, d, st_ref):
        lanes = slice(d * DK, (d + 1) * DK)
        rows = pl.ds(pl.multiple_of(n * C, C), C)
        st = st_ref[...]
        acc_ref[rows, :] += _dot_nt(qin_ref[rows, lanes], st.astype(BF16))
        st_ref[...] = st * dec_ref[pl.ds(n, 1), :][:, lanes] + u_ref[d, n]

    def step(i, carry):
        one(i, 0, stf_ref)
        one(GLA_NCH - 1 - i, 1, stb_ref)
        return carry

    lax.fori_loop(0, GLA_NCH, step, 0, unroll=4)

    def fin(gi, carry):
        rows = pl.ds(pl.multiple_of(gi * GLA_FIN, GLA_FIN), GLA_FIN)
        o = acc_ref[rows, :]
        o = o * lax.rsqrt(jnp.mean(o * o, axis=-1, keepdims=True) + RMS_EPS) * g_ref[...]
        rg = r_ref[rows, :].astype(F32)
        o_ref[rows, :] = (o * (rg * jax.nn.sigmoid(rg))).astype(BF16)
        return carry

    lax.fori_loop(0, SEQ // GLA_FIN, fin, 0)


def _gla(proj, lr, wd, bd, g):
    i = np.arange(GLA_BULK)
    cs = ((i[:, None] // GLA_C) == (i[None, :] // GLA_C)) & (i[:, None] >= i[None, :])
    cs = jnp.asarray(cs, dtype=F32).astype(BF16)
    v_blk = 1024 // GLA_DV
    return pl.pallas_call(
        _gla_kernel,
        grid=(BATCH, GLA_H),
        in_specs=[
            pl.BlockSpec((SEQ, GLA_DK), lambda b, h: (b, h)),
            pl.BlockSpec((SEQ, GLA_DK), lambda b, h: (b, GLA_H + h)),
            pl.BlockSpec((SEQ, GLA_DV), lambda b, h: (b, v_blk + h)),
            pl.BlockSpec((SEQ, GLA_DV), lambda b, h: (b, 2 * v_blk + h)),
            pl.BlockSpec((SEQ, LANES), lambda b, h: (b, 0)),
            pl.BlockSpec((2 * LANES, 2 * GLA_DK), lambda b, h: (0, h)),
            pl.BlockSpec((1, 2 * GLA_DK), lambda b, h: (0, h)),
            pl.BlockSpec((1, GLA_DV), lambda b, h: (0, 0)),
            pl.BlockSpec((GLA_BULK, GLA_BULK), lambda b, h: (0, 0)),
        ],
        out_specs=pl.BlockSpec((SEQ, GLA_DV), lambda b, h: (b, h)),
        out_shape=jax.ShapeDtypeStruct((T, GLA_H * GLA_DV), BF16),
        scratch_shapes=[
            pltpu.VMEM((SEQ, GLA_DV), F32),
            pltpu.VMEM((SEQ, 2 * GLA_DK), BF16),
            pltpu.VMEM((SEQ, 2 * GLA_DK), BF16),
            pltpu.VMEM((SEQ, 2 * GLA_DK), BF16),
            pltpu.VMEM((GLA_NCH, 2 * GLA_DK), F32),
            pltpu.VMEM((2, GLA_NCH, GLA_DV, GLA_DK), F32),
            pltpu.VMEM((GLA_DV, GLA_DK), F32),
            pltpu.VMEM((GLA_DV, GLA_DK), F32),
        ],
        compiler_params=pltpu.CompilerParams(
            dimension_semantics=("arbitrary", "arbitrary"),
            vmem_limit_bytes=58 * MIB),
        name="gla",
    )(proj, proj, proj, proj, lr, wd, bd, g, cs)


FFT1_S = 16
FFT1_ROWS = FFT_N2 * FFT1_S
FFT2_KB = 8


def _fft1_kernel(x_ref, g_ref, b_ref, w_ref, bias_ref, fbig_ref, cw_ref, sw_ref, o_ref):
    xv = x_ref[...].reshape(FFT1_ROWS, D)
    hb = _ln(xv, g_ref[...], b_ref[...]).astype(BF16)
    fn = (_dot(hb, w_ref[...]) + bias_ref[...]).astype(BF16)
    a = _dot(fbig_ref[...], fn)
    ar = a[:FFT1_ROWS]
    ai = a[FFT1_ROWS:]
    cw = jnp.concatenate([cw_ref[...]] * (FN_W // LANES), axis=1)
    sw = jnp.concatenate([sw_ref[...]] * (FN_W // LANES), axis=1)
    o_ref[0] = (ar * cw + ai * sw).reshape(FFT_N2, FFT1_S, FN_W).astype(BF16)
    o_ref[1] = (ai * cw - ar * sw).reshape(FFT_N2, FFT1_S, FN_W).astype(BF16)


def _fft1(x4, ln_g, ln_b, w_fn, b_fn, fbig, cwt, swt):
    s = FFT1_S
    const = lambda b, j: (0, 0)
    return pl.pallas_call(
        _fft1_kernel,
        grid=(BATCH, FFT_N1 // s),
        in_specs=[
            pl.BlockSpec((None, FFT_N2, s, D), lambda b, j: (b, 0, j, 0)),
            pl.BlockSpec((1, D), const),
            pl.BlockSpec((1, D), const),
            pl.BlockSpec((D, FN_W), const),
            pl.BlockSpec((1, FN_W), const),
            pl.BlockSpec((2 * FFT1_ROWS, FFT1_ROWS), const),
            pl.BlockSpec((None, FFT1_ROWS, LANES), lambda b, j: (j, 0, 0)),
            pl.BlockSpec((None, FFT1_ROWS, LANES), lambda b, j: (j, 0, 0)),
        ],
        out_specs=pl.BlockSpec((None, 2, FFT_N2, s, FN_W), lambda b, j: (b, 0, 0, j, 0)),
        out_shape=jax.ShapeDtypeStruct((BATCH, 2, FFT_N2, FFT_N1, FN_W), BF16),
        compiler_params=pltpu.CompilerParams(
            dimension_semantics=("arbitrary", "arbitrary"),
            vmem_limit_bytes=40 * MIB),
        name="fft_stage1",
    )(x4, ln_g, ln_b, w_fn, b_fn, fbig, cwt, swt)


def _fft2_kernel(d_ref, f2_ref, o_ref):
    f2 = f2_ref[...]
    for kk in range(FFT2_KB):
        z = _dot(f2, jnp.concatenate([d_ref[0, kk], d_ref[1, kk]], axis=0))
        o_ref[0, kk] = z[:FFT_N1].astype(BF16)
        o_ref[1, kk] = z[FFT_N1:].astype(BF16)


def _fft2(dmat, f2):
    kb = FFT2_KB
    blk = (None, 2, kb, FFT_N1, FN_W)
    return pl.pallas_call(
        _fft2_kernel,
        grid=(BATCH, FFT_N2 // kb),
        in_specs=[
            pl.BlockSpec(blk, lambda b, j: (b, 0, j, 0, 0)),
            pl.BlockSpec((2 * FFT_N1, 2 * FFT_N1), lambda b, j: (0, 0)),
        ],
        out_specs=pl.BlockSpec(blk, lambda b, j: (b, 0, j, 0, 0)),
        out_shape=jax.ShapeDtypeStruct((BATCH, 2, FFT_N2, FFT_N1, FN_W), BF16),
        compiler_params=pltpu.CompilerParams(
            dimension_semantics=("arbitrary", "arbitrary")),
        name="fft_stage2",
    )(dmat, f2)


def _dft_tables(merge_tm):
    s = FFT1_S
    n2 = np.arange(FFT_N2, dtype=np.float64)
    n1 = np.arange(FFT_N1, dtype=np.float64)
    th = 2.0 * np.pi * np.outer(n2, n2) / FFT_N2
    f1 = np.stack([np.cos(th), -np.sin(th)]) / math.sqrt(SEQ)
    fbig = np.einsum("rkn,st->rksnt", f1, np.eye(s)).reshape(2 * FFT_N2 * s, FFT_N2 * s)
    tw = 2.0 * np.pi * np.outer(n2, n1) / SEQ
    tw = tw.reshape(FFT_N2, FFT_N1 // s, s).transpose(1, 0, 2).reshape(FFT_N1 // s, FFT_N2 * s)
    cwt = np.broadcast_to(np.cos(tw)[:, :, None], tw.shape + (LANES,))
    swt = np.broadcast_to(np.sin(tw)[:, :, None], tw.shape + (LANES,))
    th1 = 2.0 * np.pi * np.outer(n1, n1) / FFT_N1
    c1, s1 = np.cos(th1), np.sin(th1)
    f2 = np.block([[c1, s1], [-s1, c1]])
    cc = np.arange(FN_GW, dtype=np.float64)
    thc = 2.0 * np.pi * np.outer(cc, cc) / FN_GW
    ccs = np.concatenate([np.cos(thc), np.sin(thc)], axis=0) / math.sqrt(FN_GW)
    k1n = merge_tm // FFT_N2
    r = np.arange(merge_tm)
    perm = np.zeros((merge_tm, merge_tm))
    perm[r, (r % FFT_N2) * k1n + r // FFT_N2] = 1.0
    as32 = lambda a: jnp.asarray(np.ascontiguousarray(a), dtype=F32)
    return (as32(fbig).astype(BF16), as32(cwt), as32(swt), as32(f2).astype(BF16),
            as32(ccs).astype(BF16), as32(perm).astype(BF16))


def _memkv_kernel(m_ref, g_ref, b_ref, w_ref, o_ref):
    mn = _ln(m_ref[...], g_ref[...], b_ref[...]).astype(BF16)
    o_ref[...] = _dot(mn, w_ref[...]).astype(BF16)


def _memkv(mem2, g, b, w):
    return pl.pallas_call(
        _memkv_kernel,
        grid=(BATCH,),
        in_specs=[
            pl.BlockSpec((MEM_LEN, D), lambda i: (i, 0)),
            pl.BlockSpec((1, D), lambda i: (0, 0)),
            pl.BlockSpec((1, D), lambda i: (0, 0)),
            pl.BlockSpec((D, 2 * MQ_W), lambda i: (0, 0)),
        ],
        out_specs=pl.BlockSpec((MEM_LEN, 2 * MQ_W), lambda i: (i, 0)),
        out_shape=jax.ShapeDtypeStruct((BATCH * MEM_LEN, 2 * MQ_W), BF16),
        compiler_params=pltpu.CompilerParams(dimension_semantics=("arbitrary",)),
        name="mem_kv",
    )(mem2, g, b, w)


MERGE_TM = 512
MERGE_K1 = MERGE_TM // FFT_N2


def _pack_bf16_pair(v):
    n = v.shape[1] // 2
    bits = lax.bitcast_convert_type(v.astype(BF16).astype(F32), U32)
    return (bits[:, n:] & jnp.uint32(0xFFFF0000)) | (bits[:, :n] >> 16)


def _unpack_bf16_pair(p):
    lo = lax.bitcast_convert_type(p << 16, F32)
    hi = lax.bitcast_convert_type(p & jnp.uint32(0xFFFF0000), F32)
    return lo, hi


def _merge_kernel(x_ref, og_ref, zr_ref, zi_ref, mq_ref, gt_ref, kv_ref,
                  lng_ref, lnb_ref, wg_ref, ccs_ref, perm_ref, wf_ref, wm_ref, wo_ref, bo_ref,
                  l1g_ref, l1b_ref, wr2_ref, wrh_ref, br_ref,
                  h1_ref, h1p_ref, eidx_ref, topw_ref, cnt_ref):
    tm = MERGE_TM
    y_gla = _dot(og_ref[...], wg_ref[...])

    zr = zr_ref[...].reshape(tm, FN_W)
    zi = zi_ref[...].reshape(tm, FN_W)
    ys = []
    for g in range(FN_G):
        sl = slice(g * FN_GW, (g + 1) * FN_GW)
        ys.append(_dot(jnp.concatenate([zr[:, sl], zi[:, sl]], axis=1), ccs_ref[...]))
    yp = jnp.concatenate(ys, axis=1).astype(BF16)
    y_fn = _dot(_dot(perm_ref[...], yp).astype(BF16), wf_ref[...])

    oms = []
    for hd in range(MEM_H):
        sl = slice(hd * MEM_HD, (hd + 1) * MEM_HD)
        s = _dot_nt(mq_ref[:, sl], kv_ref[:, sl]) * (MEM_HD ** -0.5)
        s = s - jnp.max(s, axis=-1, keepdims=True)
        p = jnp.exp(s)
        p = p * (1.0 / jnp.sum(p, axis=-1, keepdims=True))
        oms.append(_dot(p.astype(BF16), kv_ref[:, MQ_W + hd * MEM_HD:MQ_W + (hd + 1) * MEM_HD]))
    y_mem = _dot(jnp.concatenate(oms, axis=1).astype(BF16), wm_ref[...])

    def gate(c):
        return 0.5 + 0.5 * jnp.tanh(0.5 * gt_ref[:, c * D:(c + 1) * D].astype(F32))

    merged = gate(0) * y_gla + gate(1) * y_fn + gate(2) * y_mem
    mix = _dot(merged.astype(BF16), wo_ref[...]) + bo_ref[...]
    h = _ln(x_ref[...], lng_ref[...], lnb_ref[...])
    h1 = _ln(DN_ALPHA * h + mix, l1g_ref[...], l1b_ref[...])
    h1_ref[...] = h1
    h1p_ref[...] = _pack_bf16_pair(h1)

    h_hi, h_lo = _split_bf16(h1)
    d2 = _dot(h_hi, wr2_ref[...])
    logits = d2[:, :LANES] + d2[:, LANES:] + _dot(h_lo, wrh_ref[...]) + br_ref[...]
    lane = lax.broadcasted_iota(I32, (tm, LANES), 1)
    l = logits
    vals, idxs = [], []
    for _ in range(TOP_K):
        m = jnp.max(l, axis=-1, keepdims=True)
        idx = jnp.min(jnp.where(l == m, lane, LANES), axis=-1, keepdims=True)
        vals.append(m)
        idxs.append(idx)
        l = jnp.where(lane == idx, -jnp.inf, l)
    es = [jnp.exp(v - vals[0]) for v in vals]
    den = es[0] + es[1] + es[2] + es[3]
    eo = jnp.zeros((tm, LANES), I32)
    wo = jnp.zeros((tm, LANES), F32)
    for k in range(TOP_K):
        eo = jnp.where(lane == k, idxs[k], eo)
        wo = jnp.where(lane == k, es[k] / den, wo)
    eidx_ref[...] = eo
    topw_ref[...] = wo

    @pl.when(pl.program_id(0) == 0)
    def _():
        cnt_ref[...] = jnp.zeros_like(cnt_ref)

    chosen = jnp.zeros((tm, LANES), F32)
    for k in range(TOP_K):
        chosen = chosen + jnp.where(lane == idxs[k], 1.0, 0.0)
    cnt_ref[...] += jnp.broadcast_to(jnp.sum(chosen, axis=0, keepdims=True), cnt_ref.shape)


def _merge(x2, og, z, mq, gates, kv, lng, lnb, wg, ccs, perm, wf, wm, wo, bo, l1g, l1b,
           wr2, wrh, br):
    tm = MERGE_TM
    per_b = SEQ // tm
    row = lambda i: (i, 0)
    const = lambda i: (0, 0)
    zblk = (None, None, FFT_N2, MERGE_K1, FN_W)
    outs = (
        jax.ShapeDtypeStruct((T, D), F32),
        jax.ShapeDtypeStruct((T, D // 2), U32),
        jax.ShapeDtypeStruct((T, LANES), I32),
        jax.ShapeDtypeStruct((T, LANES), F32),
        jax.ShapeDtypeStruct((8, LANES), F32),
    )
    return pl.pallas_call(
        _merge_kernel,
        grid=(T // tm,),
        in_specs=[
            pl.BlockSpec((tm, D), row),
            pl.BlockSpec((tm, D), row),
            pl.BlockSpec(zblk, lambda i: (i // per_b, 0, 0, i % per_b, 0)),
            pl.BlockSpec(zblk, lambda i: (i // per_b, 1, 0, i % per_b, 0)),
            pl.BlockSpec((tm, MQ_W), row),
            pl.BlockSpec((tm, 3 * D), lambda i: (i, 1)),
            pl.BlockSpec((MEM_LEN, 2 * MQ_W), lambda i: (i // per_b, 0)),
            pl.BlockSpec((1, D), const), pl.BlockSpec((1, D), const),
            pl.BlockSpec((D, D), const),
            pl.BlockSpec((2 * FN_GW, FN_GW), const),
            pl.BlockSpec((tm, tm), const),
            pl.BlockSpec((FN_W, D), const),
            pl.BlockSpec((MQ_W, D), const),
            pl.BlockSpec((D, D), const),
            pl.BlockSpec((1, D), const),
            pl.BlockSpec((1, D), const), pl.BlockSpec((1, D), const),
            pl.BlockSpec((D, 2 * LANES), const),
            pl.BlockSpec((D, LANES), const),
            pl.BlockSpec((1, LANES), const),
        ],
        out_specs=[
            pl.BlockSpec((tm, D), row),
            pl.BlockSpec((tm, D // 2), row),
            pl.BlockSpec((tm, LANES), row),
            pl.BlockSpec((tm, LANES), row),
            pl.BlockSpec((8, LANES), const),
        ],
        out_shape=outs,
        compiler_params=pltpu.CompilerParams(
            dimension_semantics=("arbitrary",),
            vmem_limit_bytes=58 * MIB),
        name="merge_ln1_router",
    )(x2, og, z, z, mq, gates, kv, lng, lnb, wg, ccs, perm, wf, wm, wo, bo, l1g, l1b,
      wr2, wrh, br)


PLAN_TP = 1024


def _expert_onehots(e, lane):
    onehots = [lane == e[:, k:k + 1] for k in range(TOP_K)]
    mf = jnp.zeros(lane.shape, F32)
    for oh in onehots:
        mf = mf + jnp.where(oh, 1.0, 0.0)
    return onehots, mf


def _plan_kernel(e_ref, tot_ref, dest_ref, cnt_ref, off_ref):
    i = pl.program_id(0)
    tp = PLAN_TP
    lane = lax.broadcasted_iota(I32, (tp, LANES), 1)
    onehots, mf = _expert_onehots(e_ref[...], lane)

    @pl.when(i == 0)
    def _():
        tot = tot_ref[0:1, :]
        padded = jnp.floor((tot + (MOE_BM - 1)) * (1.0 / MOE_BM)) * MOE_BM
        lane1 = lax.broadcasted_iota(I32, (1, LANES), 1)
        inc = padded
        for s in (1, 2, 4, 8, 16, 32, 64):
            inc = inc + jnp.where(lane1 >= s, pltpu.roll(inc, s, 1), 0.0)
        off_ref[...] = inc - padded
        cnt_ref[...] = jnp.zeros_like(cnt_ref)

    ri = lax.broadcasted_iota(I32, (tp, tp), 0)
    ci = lax.broadcasted_iota(I32, (tp, tp), 1)
    ltri = jnp.where(ri > ci, 1.0, 0.0).astype(BF16)
    rank = _dot(ltri, mf.astype(BF16)) + cnt_ref[...] + off_ref[...]
    out = jnp.zeros((tp, LANES), I32)
    for k in range(TOP_K):
        dk = jnp.sum(jnp.where(onehots[k], rank, 0.0), axis=-1, keepdims=True)
        out = jnp.where(lane == k, dk.astype(I32), out)
    dest_ref[...] = out
    cnt_ref[...] += jnp.sum(mf, axis=0, keepdims=True)


def _plan(eidx, cnt):
    tp = PLAN_TP
    return pl.pallas_call(
        _plan_kernel,
        grid=(T // tp,),
        in_specs=[pl.BlockSpec((tp, LANES), lambda i: (i, 0)),
                  pl.BlockSpec((8, LANES), lambda i: (0, 0))],
        out_specs=pl.BlockSpec((tp, LANES), lambda i: (i, 0)),
        out_shape=jax.ShapeDtypeStruct((T, LANES), I32),
        scratch_shapes=[pltpu.VMEM((1, LANES), F32), pltpu.VMEM((1, LANES), F32)],
        compiler_params=pltpu.CompilerParams(dimension_semantics=("arbitrary",)),
        name="route_plan",
    )(eidx, cnt)


MOE_BM = 512
MOE_NW = A_ROWS // MOE_BM + N_EXP
XS_ROWS = MOE_NW * MOE_BM


def _expert_kernel(we_ref, wb_ref, wv_ref, wfe_ref, wsl_ref, wnx_ref,
                   x_ref, wgu_hbm, bgu_ref, wdn_hbm, bdn_ref, o_ref,
                   wgu_f32, wdn_f32, wgu_bf, wdn_bf, sems):
    w = pl.program_id(0)
    e = we_ref[w]

    def weight_copies(expert, slot):
        return (pltpu.make_async_copy(wgu_hbm.at[expert], wgu_f32.at[slot], sems.at[slot, 0]),
                pltpu.make_async_copy(wdn_hbm.at[expert], wdn_f32.at[slot], sems.at[slot, 1]))

    @pl.when(w == 0)
    def _():
        for cp_ in weight_copies(e, 0):
            cp_.start()

    @pl.when(wfe_ref[w] == 1)
    def _():
        slot = wsl_ref[w]
        for cp_ in weight_copies(e, slot):
            cp_.wait()
        wgu_bf[...] = wgu_f32[slot].astype(BF16)
        wdn_bf[...] = wdn_f32[slot].astype(BF16)
        nxt = wnx_ref[w]

        @pl.when(nxt >= 0)
        def _():
            for cp_ in weight_copies(nxt, 1 - slot):
                cp_.start()

    def ffn(rows):
        xlo, xhi = _unpack_bf16_pair(x_ref[rows, :])
        half = D // 2
        gu = (_dot(xlo.astype(BF16), wgu_bf[:half, :]) + _dot(xhi.astype(BF16), wgu_bf[half:, :])
              + bgu_ref[pl.ds(e, 1), :])
        gate = jnp.minimum(gu[:, :D_FF], SW_LIMIT)
        up = jnp.clip(gu[:, D_FF:], -SW_LIMIT, SW_LIMIT)
        act = (up + 1.0) * (gate * jax.nn.sigmoid(SW_ALPHA * gate))
        out = _dot(act.astype(BF16), wdn_bf[...]) + bdn_ref[pl.ds(e, 1), :]
        o_ref[rows, :] = _pack_bf16_pair(out)

    @pl.when(wv_ref[w] == 1)
    def _():
        ffn(slice(None))

    @pl.when(wv_ref[w] == 2)
    def _():
        ffn(slice(0, MOE_BM // 2))


def _experts(meta, xs, w_gu, b_gu, w_down, b_down):
    return pl.pallas_call(
        _expert_kernel,
        grid_spec=pltpu.PrefetchScalarGridSpec(
            num_scalar_prefetch=len(meta),
            grid=(MOE_NW,),
            in_specs=[
                pl.BlockSpec((MOE_BM, D // 2), lambda w, we, wb, *_: (wb[w], 0)),
                pl.BlockSpec(memory_space=pl.ANY),
                pl.BlockSpec((N_EXP, 2 * D_FF), lambda w, *_: (0, 0)),
                pl.BlockSpec(memory_space=pl.ANY),
                pl.BlockSpec((N_EXP, D), lambda w, *_: (0, 0)),
            ],
            out_specs=pl.BlockSpec((MOE_BM, D // 2), lambda w, we, wb, *_: (wb[w], 0)),
            scratch_shapes=[
                pltpu.VMEM((2, D, 2 * D_FF), F32),
                pltpu.VMEM((2, D_FF, D), F32),
                pltpu.VMEM((D, 2 * D_FF), BF16),
                pltpu.VMEM((D_FF, D), BF16),
                pltpu.SemaphoreType.DMA((2, 2)),
            ],
        ),
        out_shape=jax.ShapeDtypeStruct((XS_ROWS, D // 2), U32),
        compiler_params=pltpu.CompilerParams(
            dimension_semantics=("arbitrary",),
            vmem_limit_bytes=56 * MIB),
        name="moe_experts",
    )(*meta, xs, w_gu, b_gu, w_down, b_down)


def _work_items(counts):
    n_e = (counts + MOE_BM - 1) // MOE_BM
    item_end = jnp.cumsum(n_e)
    total = item_end[-1]
    w = jnp.arange(MOE_NW, dtype=I32)
    valid = w < total
    wc = jnp.minimum(w, total - 1)
    e_w = jnp.sum((item_end[None, :] <= wc[:, None]).astype(I32), axis=1)
    e_w = jnp.minimum(e_w, N_EXP - 1)
    rows_here = counts[e_w] - (wc - (item_end - n_e)[e_w]) * MOE_BM
    valid = jnp.where(valid, jnp.where(rows_here <= MOE_BM // 2, 2, 1), 0)
    prev_e = jnp.concatenate([jnp.full((1,), -1, I32), e_w[:-1]])
    fe = (e_w != prev_e).astype(I32)
    slot = (jnp.cumsum(fe) - 1) % 2
    first_at = jnp.where(fe == 1, w, MOE_NW)
    next_first = jnp.concatenate([lax.cummin(first_at, reverse=True)[1:],
                                  jnp.full((1,), MOE_NW, I32)])
    nxt = jnp.where(next_first < MOE_NW, e_w[jnp.minimum(next_first, MOE_NW - 1)], -1)
    return tuple(a.astype(I32) for a in (e_w, wc, valid, fe, slot, nxt))


COMB_TM = 512
SC_CORES = 2
SC_SUBCORES = 16
SC_WORKERS = SC_CORES * SC_SUBCORES
SC_CH = 64
COMB_GROUPS = 4
COMB_TG = T // COMB_GROUPS
SC_ROWS_PER_W = COMB_TG * TOP_K // SC_WORKERS
SC_NCH = SC_ROWS_PER_W // SC_CH


def _sc_gather(table, idx3):
    mesh = plsc.VectorSubcoreMesh(core_axis_name="c", subcore_axis_name="s")

    @functools.partial(
        pl.kernel, mesh=mesh,
        out_type=jax.ShapeDtypeStruct((COMB_TG * TOP_K, D // 2), U32),
        scratch_types=[
            pltpu.VMEM((SC_NCH, SC_CH), I32),
            pltpu.VMEM((2, SC_CH, D // 2), U32),
            pltpu.SemaphoreType.DMA((2,)),
            pltpu.SemaphoreType.DMA((2,)),
        ],
    )
    def k(table_hbm, idx_hbm, out_hbm, idx_v, rows_v, gsem, psem):
        wid = lax.axis_index("s") * SC_CORES + lax.axis_index("c")
        base = wid * SC_ROWS_PER_W
        pltpu.sync_copy(idx_hbm.at[wid], idx_v)

        def gather(j, b):
            return pltpu.make_async_copy(table_hbm.at[idx_v.at[j]], rows_v.at[b], gsem.at[b])

        def put(j, b):
            return pltpu.make_async_copy(rows_v.at[b], out_hbm.at[pl.ds(base + j * SC_CH, SC_CH)],
                                         psem.at[b])

        gather(0, 0).start()

        @pl.loop(0, SC_NCH, step=2)
        def _(j0):
            for b in range(2):
                j = j0 + b

                @pl.when(j + 1 < SC_NCH)
                def _():
                    @pl.when(j >= 1)
                    def _():
                        put(j - 1, 1 - b).wait()
                    gather(j + 1, 1 - b).start()

                gather(j, b).wait()
                put(j, b).start()

        put(SC_NCH - 2, 0).wait()
        put(SC_NCH - 1, 1).wait()

    return k(table, idx3)


SCD_TOK_PER_W = T // SC_WORKERS
SCD_NCH = SCD_TOK_PER_W // SC_CH


def _sc_dispatch(h1p, idx4):
    mesh = plsc.VectorSubcoreMesh(core_axis_name="c", subcore_axis_name="s")

    @functools.partial(
        pl.kernel, mesh=mesh,
        out_type=jax.ShapeDtypeStruct((XS_ROWS, D // 2), U32),
        scratch_types=[
            pltpu.VMEM((SCD_NCH * TOP_K, SC_CH), I32),
            pltpu.VMEM((2, SC_CH, D // 2), U32),
            pltpu.SemaphoreType.DMA((2,)),
            pltpu.SemaphoreType.DMA((2,)),
        ],
    )
    def k(h_hbm, idx_hbm, xs_hbm, idx_v, rows_v, gsem, psem):
        wid = lax.axis_index("s") * SC_CORES + lax.axis_index("c")
        base = wid * SCD_TOK_PER_W
        pltpu.sync_copy(idx_hbm.at[wid], idx_v)

        def get(c, b):
            return pltpu.make_async_copy(h_hbm.at[pl.ds(base + c * SC_CH, SC_CH)], rows_v.at[b],
                                         gsem.at[b])

        def puts(c, b):
            return [pltpu.make_async_copy(rows_v.at[b], xs_hbm.at[idx_v.at[c * TOP_K + kk]],
                                          psem.at[b]) for kk in range(TOP_K)]

        get(0, 0).start()

        @pl.loop(0, SCD_NCH, step=2)
        def _(c0):
            for b in range(2):
                c = c0 + b

                @pl.when(c + 1 < SCD_NCH)
                def _():
                    @pl.when(c >= 1)
                    def _():
                        for cp_ in puts(c - 1, 1 - b):
                            cp_.wait()
                    get(c + 1, 1 - b).start()

                get(c, b).wait()
                for cp_ in puts(c, b):
                    cp_.start()

        for cp_ in puts(SCD_NCH - 2, 0) + puts(SCD_NCH - 1, 1):
            cp_.wait()

    return k(h1p, idx4)


def _combine_dense_kernel(g_ref, h1_ref, tw_ref, lg_ref, lb_ref, o_ref):
    tw = tw_ref[...]
    ylo = jnp.zeros((COMB_TM, D // 2), F32)
    yhi = jnp.zeros((COMB_TM, D // 2), F32)
    for k in range(TOP_K):
        lo, hi = _unpack_bf16_pair(g_ref[k])
        wk = tw[:, k:k + 1]
        ylo = ylo + lo * wk
        yhi = yhi + hi * wk
    ff = jnp.concatenate([ylo, yhi], axis=1)
    o_ref[...] = _ln(DN_ALPHA * h1_ref[...] + ff, lg_ref[...], lb_ref[...])


def _combine_dense(g4, h1, topw, g, b, group):
    tm = COMB_TM
    t0 = group * (COMB_TG // tm)
    return pl.pallas_call(
        _combine_dense_kernel,
        grid=(COMB_TG // tm,),
        in_specs=[
            pl.BlockSpec((TOP_K, tm, D // 2), lambda i: (0, i, 0)),
            pl.BlockSpec((tm, D), lambda i: (t0 + i, 0)),
            pl.BlockSpec((tm, LANES), lambda i: (t0 + i, 0)),
            pl.BlockSpec((1, D), lambda i: (0, 0)),
            pl.BlockSpec((1, D), lambda i: (0, 0)),
        ],
        out_specs=pl.BlockSpec((tm, D), lambda i: (t0 + i, 0)),
        out_shape=jax.ShapeDtypeStruct((T, D), F32),
        input_output_aliases={1: 0},
        compiler_params=pltpu.CompilerParams(
            dimension_semantics=("arbitrary",),
            vmem_limit_bytes=40 * MIB),
        name="moe_combine_dense_ln2",
    )(g4, h1, topw, g, b)


def _pad_cols(a, n):
    return jnp.pad(a, ((0, 0), (0, n - a.shape[1])))


def kernel(x, mem, ln_in_g, ln_in_b, ln_mem_g, ln_mem_b, w_in, b_in, w_decay_f, b_decay_f,
           w_decay_b, b_decay_b, gla_norm_g, w_br_gla, w_br_fnet, w_br_mem, w_mem_kv, w_out,
           b_out, ln1_g, ln1_b, w_router, b_router, w_gu, b_gu, w_down, b_down, ln2_g, ln2_b):
    assert x.shape == (BATCH, SEQ, D) and w_in.shape[0] == 1
    row = lambda a: a.reshape(1, -1)
    x2 = x.reshape(T, D)
    w_in0, b_in0 = w_in[0], b_in[0]
    c_lr, c_fn, c_mq, c_gt = 3072, 3072 + 2 * GLA_LR, 3104 + FN_W, 3104 + FN_W + MQ_W
    w_main = jnp.concatenate([w_in0[:, :c_lr], w_in0[:, c_gt:]], axis=1).astype(BF16)
    b_main = row(jnp.concatenate([b_in0[:c_lr], b_in0[c_gt:]]))
    w_lr = _pad_cols(w_in0[:, c_lr:c_fn], LANES).astype(BF16)
    b_lr = _pad_cols(row(b_in0[c_lr:c_fn]), LANES)
    w_mq = w_in0[:, c_mq:c_gt].astype(BF16)
    b_mq = row(b_in0[c_mq:c_gt])
    w_fn = w_in0[:, c_fn:c_mq].astype(BF16)
    b_fn = row(b_in0[c_fn:c_mq])
    lng, lnb = row(ln_in_g), row(ln_in_b)

    proj, mq, lr = _inproj(x2, lng, lnb, w_main, b_main, w_mq, b_mq, w_lr, b_lr)

    zpad = jnp.zeros((LANES - 2 * GLA_LR, GLA_H * GLA_DK), F32)
    zlr = jnp.zeros((GLA_LR, GLA_H * GLA_DK), F32)
    wdf = jnp.concatenate([w_decay_f[0], zlr, zpad], axis=0).reshape(LANES, GLA_H, GLA_DK)
    wdb = jnp.concatenate([zlr, w_decay_b[0], zpad], axis=0).reshape(LANES, GLA_H, GLA_DK)
    wd = jnp.concatenate([wdf, wdb], axis=2).reshape(LANES, GLA_H * 2 * GLA_DK).astype(BF16)
    wd = jnp.concatenate([wd, wd], axis=0)
    bd = jnp.concatenate([b_decay_f[0].reshape(GLA_H, GLA_DK),
                          b_decay_b[0].reshape(GLA_H, GLA_DK)], axis=1).reshape(1, -1)
    og = _gla(proj, lr, wd, bd, row(gla_norm_g[0]))

    fbig, cwt, swt, f2, ccs, perm = _dft_tables(MERGE_TM)
    x4 = x.reshape(BATCH, FFT_N2, FFT_N1, D)
    z = _fft2(_fft1(x4, lng, lnb, w_fn, b_fn, fbig, cwt, swt), f2)

    kv = _memkv(mem.reshape(BATCH * MEM_LEN, D), row(ln_mem_g), row(ln_mem_b),
                w_mem_kv[0].astype(BF16))

    w_r = _pad_cols(w_router[0], LANES)
    wr_hi = w_r.astype(BF16)
    wr_lo = (w_r - wr_hi.astype(F32)).astype(BF16)
    b_r = jnp.concatenate([row(b_router[0]),
                           jnp.full((1, LANES - N_EXP), NEG_BIG, F32)], axis=1)
    h1, h1p, eidx, topw, cnt = _merge(
        x2, og, z, mq, proj, kv, lng, lnb,
        w_br_gla[0].astype(BF16), ccs, perm, w_br_fnet[0].astype(BF16),
        w_br_mem[0].astype(BF16), w_out[0].astype(BF16), row(b_out[0]),
        row(ln1_g[0]), row(ln1_b[0]), jnp.concatenate([wr_hi, wr_lo], axis=1), wr_hi, b_r)

    dest = _plan(eidx, cnt)
    counts = cnt[0, :N_EXP].astype(I32)
    dest_k = dest[:, :TOP_K]
    idx4 = dest_k.reshape(SC_WORKERS, SCD_NCH, SC_CH, TOP_K).transpose(0, 1, 3, 2)
    xs = _sc_dispatch(h1p, idx4.reshape(SC_WORKERS, SCD_NCH * TOP_K, SC_CH))
    ys = _experts(_work_items(counts), xs, w_gu[0], b_gu[0], w_down[0], b_down[0])
    out = h1
    for grp in range(COMB_GROUPS):
        dest_g = dest_k[grp * COMB_TG:(grp + 1) * COMB_TG].T.reshape(SC_WORKERS, SC_NCH, SC_CH)
        g4 = _sc_gather(ys, dest_g).reshape(TOP_K, COMB_TG, D // 2)
        out = _combine_dense(g4, out, topw, row(ln2_g[0]), row(ln2_b[0]), grp)
    return out.reshape(BATCH, SEQ, D)
```

```python
import functools
import math

import numpy as np
import jax
import jax.numpy as jnp
from jax import lax
from jax.experimental import pallas as pl
from jax.experimental.pallas import tpu as pltpu
from jax.experimental.pallas import tpu_sc as plsc

F32 = jnp.float32
BF16 = jnp.bfloat16
I32 = jnp.int32
U32 = jnp.uint32

D = 1024
BATCH = 4
SEQ = 4096
T = BATCH * SEQ
GLA_H = 4
GLA_DK = 128
GLA_DV = 256
GLA_LR = 16
GLA_TAU = 16.0
GLA_C = 64
FN_G = 4
FN_GW = 128
FN_W = 512
MEM_LEN = 256
MEM_H = 4
MEM_HD = 128
MQ_W = 512
N_EXP = 32
TOP_K = 4
D_FF = 1024
SW_LIMIT = 7.0
SW_ALPHA = 1.702
LN_EPS = 1e-5
RMS_EPS = 1e-6
DN_ALPHA = 2.0 ** 0.25
A_ROWS = T * TOP_K

FFT_N1 = 128
FFT_N2 = 32

LANES = 128
NEG_BIG = -1e30
MIB = 1024 * 1024


def _ln(x, g, b):
    mu = jnp.mean(x, axis=-1, keepdims=True)
    xc = x - mu
    var = jnp.mean(xc * xc, axis=-1, keepdims=True)
    return xc * lax.rsqrt(var + LN_EPS) * g + b


def _dot(a, b):
    return jnp.dot(a, b, preferred_element_type=F32)


def _dot_nt(a, b):
    return lax.dot_general(a, b, (((1,), (1,)), ((), ())), preferred_element_type=F32)


def _dot_tn(a, b):
    return lax.dot_general(a, b, (((0,), (0,)), ((), ())), preferred_element_type=F32)


def _interleave(*stages):
    live = list(stages)
    while live:
        for st in list(live):
            try:
                next(st)
            except StopIteration:
                live.remove(st)


def _split_bf16(a):
    hi = a.astype(BF16)
    return hi, (a - hi.astype(F32)).astype(BF16)


INPROJ_TM = 1024
INPROJ_TN = 3072


PROJ_W = 6 * 1024


def _inproj_kernel(x_ref, g_ref, b_ref, w_ref, bias_ref, wmq_ref, bmq_ref, wlr_ref, blr_ref,
                   proj_ref, mq_ref, lr_ref, hb_ref):
    @pl.when(pl.program_id(1) == 0)
    def _():
        hb = _ln(x_ref[...], g_ref[...], b_ref[...]).astype(BF16)
        hb_ref[...] = hb
        lr_ref[...] = _dot(hb, wlr_ref[...]) + blr_ref[...]
        mq_ref[...] = (_dot(hb, wmq_ref[...]) + bmq_ref[...]).astype(BF16)

    proj_ref[...] = (_dot(hb_ref[...], w_ref[...]) + bias_ref[...]).astype(BF16)


def _inproj(x2, ln_g, ln_b, w_main, b_main, w_mq, b_mq, w_lr, b_lr):
    tm, tn = INPROJ_TM, INPROJ_TN
    nj = PROJ_W // tn
    row = lambda i, j: (i, 0)
    const = lambda i, j: (0, 0)
    outs = (
        jax.ShapeDtypeStruct((T, PROJ_W), BF16),
        jax.ShapeDtypeStruct((T, MQ_W), BF16),
        jax.ShapeDtypeStruct((T, LANES), F32),
    )
    return pl.pallas_call(
        _inproj_kernel,
        grid=(T // tm, nj),
        in_specs=[
            pl.BlockSpec((tm, D), row),
            pl.BlockSpec((1, D), const),
            pl.BlockSpec((1, D), const),
            pl.BlockSpec((D, tn), lambda i, j: (0, j)),
            pl.BlockSpec((1, tn), lambda i, j: (0, j)),
            pl.BlockSpec((D, MQ_W), const),
            pl.BlockSpec((1, MQ_W), const),
            pl.BlockSpec((D, LANES), const),
            pl.BlockSpec((1, LANES), const),
        ],
        out_specs=[
            pl.BlockSpec((tm, tn), lambda i, j: (i, j)),
            pl.BlockSpec((tm, MQ_W), row),
            pl.BlockSpec((tm, LANES), row),
        ],
        out_shape=outs,
        scratch_shapes=[pltpu.VMEM((tm, D), BF16)],
        compiler_params=pltpu.CompilerParams(
            dimension_semantics=("arbitrary", "arbitrary"),
            vmem_limit_bytes=48 * MIB),
        name="ln_inproj",
    )(x2, ln_g, ln_b, w_main, b_main, w_mq, b_mq, w_lr, b_lr)


GLA_BULK = 256
GLA_FIN = 512
GLA_NCH = SEQ // GLA_C
GLA_CPB = GLA_BULK // GLA_C


def _gla_kernel(q_ref, k_ref, v_ref, r_ref, lr_ref, wd_ref, bd_ref, g_ref, cs_ref, o_ref,
                acc_ref, qin_ref, kin_ref, kst_ref, dec_ref, u_ref, stf_ref, stb_ref):
    C = GLA_C
    G = GLA_BULK
    DK = GLA_DK
    NG = SEQ // G
    scale = DK ** -0.5
    ii = lax.broadcasted_iota(I32, (G, G), 0)
    jj = lax.broadcasted_iota(I32, (G, G), 1)
    same = (ii // C) == (jj // C)
    lower = jnp.logical_and(same, ii >= jj)
    upper = jnp.logical_and(same, ii <= jj)
    is_fwd = lax.broadcasted_iota(I32, (G, 2 * DK), 1) < DK
    chunk_of_row = lax.broadcasted_iota(I32, (G, DK), 0) // C

    def stage_a(gi):
        rows = pl.ds(pl.multiple_of(gi * G, G), G)
        z = _dot(jnp.concatenate(_split_bf16(lr_ref[rows, :]), axis=1), wd_ref[...]) + bd_ref[...]
        yield
        la = -(jnp.maximum(-z, 0.0) + jnp.log(1.0 + jnp.exp(-jnp.abs(z)))) * (1.0 / GLA_TAU)
        la_hi, la_lo = _split_bf16(la)
        pre2 = _dot(cs_ref[...], jnp.concatenate([la_hi, la_lo], axis=1))
        yield
        pre = pre2[:, :2 * DK] + pre2[:, 2 * DK:]
        blast = jnp.concatenate(
            [jnp.broadcast_to(pre[ci * C + C - 1:ci * C + C, :], (C, 2 * DK))
             for ci in range(GLA_CPB)], axis=0)
        b = jnp.where(is_fwd, pre, blast - pre + la)
        qf32 = q_ref[rows, :].astype(F32)
        kf32 = k_ref[rows, :].astype(F32)
        q2 = jnp.concatenate([qf32, qf32], axis=1)
        k2 = jnp.concatenate([kf32, kf32], axis=1)
        qin_ref[rows, :] = (q2 * (scale * jnp.exp(b))).astype(BF16)
        kin_ref[rows, :] = (k2 * jnp.exp(-b)).astype(BF16)
        kst_ref[rows, :] = (k2 * jnp.exp(blast - b)).astype(BF16)
        dec = jnp.exp(blast)
        for ci in range(GLA_CPB):
            dec_ref[pl.ds(gi * GLA_CPB + ci, 1), :] = dec[ci * C:ci * C + 1, :]

    def stage_b(gi):
        rows = pl.ds(pl.multiple_of(gi * G, G), G)
        qi = qin_ref[rows, :]
        ki = kin_ref[rows, :]
        ks = kst_ref[rows, :]
        vb = v_ref[rows, :]
        att_f = _dot_nt(qi[:, :DK], ki[:, :DK])
        att_b = _dot_nt(qi[:, DK:], ki[:, DK:])
        yield
        att = jnp.where(lower, att_f, 0.0) + jnp.where(upper, att_b, 0.0)
        acc_ref[rows, :] = _dot(att.astype(BF16), vb)
        yield
        ksb = jnp.concatenate(
            [jnp.where(chunk_of_row == ci, ks[:, d * DK:(d + 1) * DK], jnp.zeros((G, DK), BF16))
             for d in range(2) for ci in range(GLA_CPB)], axis=1)
        u = _dot_tn(vb, ksb)
        for d in range(2):
            for ci in range(GLA_CPB):
                col = (d * GLA_CPB + ci) * DK
                u_ref[d, gi * GLA_CPB + ci] = u[:, col:col + DK]

    _interleave(stage_a(0), stage_a(1))

    def bulk(i, carry):
        g = 2 * i
        _interleave(stage_b(g - 2), stage_a(g), stage_b(g - 1), stage_a(g + 1))
        return carry

    lax.fori_loop(1, NG // 2, bulk, 0)
    _interleave(stage_b(NG - 2), stage_b(NG - 1))

    stf_ref[...] = jnp.zeros_like(stf_ref)
    stb_ref[...] = jnp.zeros_like(stb_ref)

    def one(n, d, st_ref):
        lanes = slice(d * DK, (d + 1) * DK)
        rows = pl.ds(pl.multiple_of(n * C, C), C)
        st = st_ref[...]
        acc_ref[rows, :] += _dot_nt(qin_ref[rows, lanes], st.astype(BF16))
        st_ref[...] = st * dec_ref[pl.ds(n, 1), :][:, lanes] + u_ref[d, n]

    def step(i, carry):
        one(i, 0, stf_ref)
        one(GLA_NCH - 1 - i, 1, stb_ref)
        return carry

    lax.fori_loop(0, GLA_NCH, step, 0, unroll=4)

    def fin(gi, carry):
        rows = pl.ds(pl.multiple_of(gi * GLA_FIN, GLA_FIN), GLA_FIN)
        o = acc_ref[rows, :]
        o = o * lax.rsqrt(jnp.mean(o * o, axis=-1, keepdims=True) + RMS_EPS) * g_ref[...]
        rg = r_ref[rows, :].astype(F32)
        o_ref[rows, :] = (o * (rg * jax.nn.sigmoid(rg))).astype(BF16)
        return carry

    lax.fori_loop(0, SEQ // GLA_FIN, fin, 0)


def _gla(proj, lr, wd, bd, g):
    i = np.arange(GLA_BULK)
    cs = ((i[:, None] // GLA_C) == (i[None, :] // GLA_C)) & (i[:, None] >= i[None, :])
    cs = jnp.asarray(cs, dtype=F32).astype(BF16)
    v_blk = 1024 // GLA_DV
    return pl.pallas_call(
        _gla_kernel,
        grid=(BATCH, GLA_H),
        in_specs=[
            pl.BlockSpec((SEQ, GLA_DK), lambda b, h: (b, h)),
            pl.BlockSpec((SEQ, GLA_DK), lambda b, h: (b, GLA_H + h)),
            pl.BlockSpec((SEQ, GLA_DV), lambda b, h: (b, v_blk + h)),
            pl.BlockSpec((SEQ, GLA_DV), lambda b, h: (b, 2 * v_blk + h)),
            pl.BlockSpec((SEQ, LANES), lambda b, h: (b, 0)),
            pl.BlockSpec((2 * LANES, 2 * GLA_DK), lambda b, h: (0, h)),
            pl.BlockSpec((1, 2 * GLA_DK), lambda b, h: (0, h)),
            pl.BlockSpec((1, GLA_DV), lambda b, h: (0, 0)),
            pl.BlockSpec((GLA_BULK, GLA_BULK), lambda b, h: (0, 0)),
        ],
        out_specs=pl.BlockSpec((SEQ, GLA_DV), lambda b, h: (b, h)),
        out_shape=jax.ShapeDtypeStruct((T, GLA_H * GLA_DV), BF16),
        scratch_shapes=[
            pltpu.VMEM((SEQ, GLA_DV), F32),
            pltpu.VMEM((SEQ, 2 * GLA_DK), BF16),
            pltpu.VMEM((SEQ, 2 * GLA_DK), BF16),
            pltpu.VMEM((SEQ, 2 * GLA_DK), BF16),
            pltpu.VMEM((GLA_NCH, 2 * GLA_DK), F32),
            pltpu.VMEM((2, GLA_NCH, GLA_DV, GLA_DK), F32),
            pltpu.VMEM((GLA_DV, GLA_DK), F32),
            pltpu.VMEM((GLA_DV, GLA_DK), F32),
        ],
        compiler_params=pltpu.CompilerParams(
            dimension_semantics=("arbitrary", "arbitrary"),
            vmem_limit_bytes=58 * MIB),
        name="gla",
    )(proj, proj, proj, proj, lr, wd, bd, g, cs)


FFT1_S = 16
FFT1_ROWS = FFT_N2 * FFT1_S
FFT2_KB = 8


def _fft1_kernel(x_ref, g_ref, b_ref, w_ref, bias_ref, fbig_ref, cw_ref, sw_ref, o_ref):
    xv = x_ref[...].reshape(FFT1_ROWS, D)
    hb = _ln(xv, g_ref[...], b_ref[...]).astype(BF16)
    fn = (_dot(hb, w_ref[...]) + bias_ref[...]).astype(BF16)
    a = _dot(fbig_ref[...], fn)
    ar = a[:FFT1_ROWS]
    ai = a[FFT1_ROWS:]
    cw = jnp.concatenate([cw_ref[...]] * (FN_W // LANES), axis=1)
    sw = jnp.concatenate([sw_ref[...]] * (FN_W // LANES), axis=1)
    o_ref[0] = (ar * cw + ai * sw).reshape(FFT_N2, FFT1_S, FN_W).astype(BF16)
    o_ref[1] = (ai * cw - ar * sw).reshape(FFT_N2, FFT1_S, FN_W).astype(BF16)


def _fft1(x4, ln_g, ln_b, w_fn, b_fn, fbig, cwt, swt):
    s = FFT1_S
    const = lambda b, j: (0, 0)
    return pl.pallas_call(
        _fft1_kernel,
        grid=(BATCH, FFT_N1 // s),
        in_specs=[
            pl.BlockSpec((None, FFT_N2, s, D), lambda b, j: (b, 0, j, 0)),
            pl.BlockSpec((1, D), const),
            pl.BlockSpec((1, D), const),
            pl.BlockSpec((D, FN_W), const),
            pl.BlockSpec((1, FN_W), const),
            pl.BlockSpec((2 * FFT1_ROWS, FFT1_ROWS), const),
            pl.BlockSpec((None, FFT1_ROWS, LANES), lambda b, j: (j, 0, 0)),
            pl.BlockSpec((None, FFT1_ROWS, LANES), lambda b, j: (j, 0, 0)),
        ],
        out_specs=pl.BlockSpec((None, 2, FFT_N2, s, FN_W), lambda b, j: (b, 0, 0, j, 0)),
        out_shape=jax.ShapeDtypeStruct((BATCH, 2, FFT_N2, FFT_N1, FN_W), BF16),
        compiler_params=pltpu.CompilerParams(
            dimension_semantics=("arbitrary", "arbitrary"),
            vmem_limit_bytes=40 * MIB),
        name="fft_stage1",
    )(x4, ln_g, ln_b, w_fn, b_fn, fbig, cwt, swt)


def _fft2_kernel(d_ref, f2_ref, o_ref):
    f2 = f2_ref[...]
    for kk in range(FFT2_KB):
        z = _dot(f2, jnp.concatenate([d_ref[0, kk], d_ref[1, kk]], axis=0))
        o_ref[0, kk] = z[:FFT_N1].astype(BF16)
        o_ref[1, kk] = z[FFT_N1:].astype(BF16)


def _fft2(dmat, f2):
    kb = FFT2_KB
    blk = (None, 2, kb, FFT_N1, FN_W)
    return pl.pallas_call(
        _fft2_kernel,
        grid=(BATCH, FFT_N2 // kb),
        in_specs=[
            pl.BlockSpec(blk, lambda b, j: (b, 0, j, 0, 0)),
            pl.BlockSpec((2 * FFT_N1, 2 * FFT_N1), lambda b, j: (0, 0)),
        ],
        out_specs=pl.BlockSpec(blk, lambda b, j: (b, 0, j, 0, 0)),
        out_shape=jax.ShapeDtypeStruct((BATCH, 2, FFT_N2, FFT_N1, FN_W), BF16),
        compiler_params=pltpu.CompilerParams(
            dimension_semantics=("arbitrary", "arbitrary")),
        name="fft_stage2",
    )(dmat, f2)


def _dft_tables(merge_tm):
    s = FFT1_S
    n2 = np.arange(FFT_N2, dtype=np.float64)
    n1 = np.arange(FFT_N1, dtype=np.float64)
    th = 2.0 * np.pi * np.outer(n2, n2) / FFT_N2
    f1 = np.stack([np.cos(th), -np.sin(th)]) / math.sqrt(SEQ)
    fbig = np.einsum("rkn,st->rksnt", f1, np.eye(s)).reshape(2 * FFT_N2 * s, FFT_N2 * s)
    tw = 2.0 * np.pi * np.outer(n2, n1) / SEQ
    tw = tw.reshape(FFT_N2, FFT_N1 // s, s).transpose(1, 0, 2).reshape(FFT_N1 // s, FFT_N2 * s)
    cwt = np.broadcast_to(np.cos(tw)[:, :, None], tw.shape + (LANES,))
    swt = np.broadcast_to(np.sin(tw)[:, :, None], tw.shape + (LANES,))
    th1 = 2.0 * np.pi * np.outer(n1, n1) / FFT_N1
    c1, s1 = np.cos(th1), np.sin(th1)
    f2 = np.block([[c1, s1], [-s1, c1]])
    cc = np.arange(FN_GW, dtype=np.float64)
    thc = 2.0 * np.pi * np.outer(cc, cc) / FN_GW
    ccs = np.concatenate([np.cos(thc), np.sin(thc)], axis=0) / math.sqrt(FN_GW)
    k1n = merge_tm // FFT_N2
    r = np.arange(merge_tm)
    perm = np.zeros((merge_tm, merge_tm))
    perm[r, (r % FFT_N2) * k1n + r // FFT_N2] = 1.0
    as32 = lambda a: jnp.asarray(np.ascontiguousarray(a), dtype=F32)
    return (as32(fbig).astype(BF16), as32(cwt), as32(swt), as32(f2).astype(BF16),
            as32(ccs).astype(BF16), as32(perm).astype(BF16))


def _memkv_kernel(m_ref, g_ref, b_ref, w_ref, o_ref):
    mn = _ln(m_ref[...], g_ref[...], b_ref[...]).astype(BF16)
    o_ref[...] = _dot(mn, w_ref[...]).astype(BF16)


def _memkv(mem2, g, b, w):
    return pl.pallas_call(
        _memkv_kernel,
        grid=(BATCH,),
        in_specs=[
            pl.BlockSpec((MEM_LEN, D), lambda i: (i, 0)),
            pl.BlockSpec((1, D), lambda i: (0, 0)),
            pl.BlockSpec((1, D), lambda i: (0, 0)),
            pl.BlockSpec((D, 2 * MQ_W), lambda i: (0, 0)),
        ],
        out_specs=pl.BlockSpec((MEM_LEN, 2 * MQ_W), lambda i: (i, 0)),
        out_shape=jax.ShapeDtypeStruct((BATCH * MEM_LEN, 2 * MQ_W), BF16),
        compiler_params=pltpu.CompilerParams(dimension_semantics=("arbitrary",)),
        name="mem_kv",
    )(mem2, g, b, w)


MERGE_TM = 512
MERGE_K1 = MERGE_TM // FFT_N2


def _pack_bf16_pair(v):
    n = v.shape[1] // 2
    bits = lax.bitcast_convert_type(v.astype(BF16).astype(F32), U32)
    return (bits[:, n:] & jnp.uint32(0xFFFF0000)) | (bits[:, :n] >> 16)


def _unpack_bf16_pair(p):
    lo = lax.bitcast_convert_type(p << 16, F32)
    hi = lax.bitcast_convert_type(p & jnp.uint32(0xFFFF0000), F32)
    return lo, hi


def _merge_kernel(x_ref, og_ref, zr_ref, zi_ref, mq_ref, gt_ref, kv_ref,
                  lng_ref, lnb_ref, wg_ref, ccs_ref, perm_ref, wf_ref, wm_ref, wo_ref, bo_ref,
                  l1g_ref, l1b_ref, wr2_ref, wrh_ref, br_ref,
                  h1_ref, h1p_ref, eidx_ref, topw_ref, cnt_ref):
    tm = MERGE_TM
    y = {}

    def branch_fnet():
        zr = zr_ref[...].reshape(tm, FN_W)
        zi = zi_ref[...].reshape(tm, FN_W)
        ys = []
        for g in range(FN_G):
            sl = slice(g * FN_GW, (g + 1) * FN_GW)
            ys.append(_dot(jnp.concatenate([zr[:, sl], zi[:, sl]], axis=1), ccs_ref[...]))
        yield
        yp = _dot(perm_ref[...], jnp.concatenate(ys, axis=1).astype(BF16))
        yield
        y["fnet"] = _dot(yp.astype(BF16), wf_ref[...])

    def branch_mem():
        heads = [slice(hd * MEM_HD, (hd + 1) * MEM_HD) for hd in range(MEM_H)]
        ss = [_dot_nt(mq_ref[:, sl], kv_ref[:, sl]) for sl in heads]
        yield
        oms = []
        for hd, s in enumerate(ss):
            s = s * (MEM_HD ** -0.5)
            s = s - jnp.max(s, axis=-1, keepdims=True)
            p = jnp.exp(s)
            p = p * (1.0 / jnp.sum(p, axis=-1, keepdims=True))
            oms.append(_dot(p.astype(BF16),
                            kv_ref[:, MQ_W + hd * MEM_HD:MQ_W + (hd + 1) * MEM_HD]))
        yield
        y["mem"] = _dot(jnp.concatenate(oms, axis=1).astype(BF16), wm_ref[...])

    def branch_gla():
        y["gla"] = _dot(og_ref[...], wg_ref[...])
        yield

    _interleave(branch_fnet(), branch_mem(), branch_gla())

    def gate(c):
        return 0.5 + 0.5 * jnp.tanh(0.5 * gt_ref[:, c * D:(c + 1) * D].astype(F32))

    merged = gate(0) * y["gla"] + gate(1) * y["fnet"] + gate(2) * y["mem"]
    mix = _dot(merged.astype(BF16), wo_ref[...]) + bo_ref[...]
    h = _ln(x_ref[...], lng_ref[...], lnb_ref[...])
    h1 = _ln(DN_ALPHA * h + mix, l1g_ref[...], l1b_ref[...])
    h1_ref[...] = h1
    h1p_ref[...] = _pack_bf16_pair(h1)

    h_hi, h_lo = _split_bf16(h1)
    d2 = _dot(h_hi, wr2_ref[...])
    logits = d2[:, :LANES] + d2[:, LANES:] + _dot(h_lo, wrh_ref[...]) + br_ref[...]
    lane = lax.broadcasted_iota(I32, (tm, LANES), 1)
    l = logits
    vals, idxs = [], []
    for _ in range(TOP_K):
        m = jnp.max(l, axis=-1, keepdims=True)
        idx = jnp.min(jnp.where(l == m, lane, LANES), axis=-1, keepdims=True)
        vals.append(m)
        idxs.append(idx)
        l = jnp.where(lane == idx, -jnp.inf, l)
    es = [jnp.exp(v - vals[0]) for v in vals]
    den = es[0] + es[1] + es[2] + es[3]
    eo = jnp.zeros((tm, LANES), I32)
    wo = jnp.zeros((tm, LANES), F32)
    for k in range(TOP_K):
        eo = jnp.where(lane == k, idxs[k], eo)
        wo = jnp.where(lane == k, es[k] / den, wo)
    eidx_ref[...] = eo
    topw_ref[...] = wo

    @pl.when(pl.program_id(0) == 0)
    def _():
        cnt_ref[...] = jnp.zeros_like(cnt_ref)

    chosen = jnp.zeros((tm, LANES), F32)
    for k in range(TOP_K):
        chosen = chosen + jnp.where(lane == idxs[k], 1.0, 0.0)
    cnt_ref[...] += jnp.broadcast_to(jnp.sum(chosen, axis=0, keepdims=True), cnt_ref.shape)


def _merge(x2, og, z, mq, gates, kv, lng, lnb, wg, ccs, perm, wf, wm, wo, bo, l1g, l1b,
           wr2, wrh, br):
    tm = MERGE_TM
    per_b = SEQ // tm
    row = lambda i: (i, 0)
    const = lambda i: (0, 0)
    zblk = (None, None, FFT_N2, MERGE_K1, FN_W)
    outs = (
        jax.ShapeDtypeStruct((T, D), F32),
        jax.ShapeDtypeStruct((T, D // 2), U32),
        jax.ShapeDtypeStruct((T, LANES), I32),
        jax.ShapeDtypeStruct((T, LANES), F32),
        jax.ShapeDtypeStruct((8, LANES), F32),
    )
    return pl.pallas_call(
        _merge_kernel,
        grid=(T // tm,),
        in_specs=[
            pl.BlockSpec((tm, D), row),
            pl.BlockSpec((tm, D), row),
            pl.BlockSpec(zblk, lambda i: (i // per_b, 0, 0, i % per_b, 0)),
            pl.BlockSpec(zblk, lambda i: (i // per_b, 1, 0, i % per_b, 0)),
            pl.BlockSpec((tm, MQ_W), row),
            pl.BlockSpec((tm, 3 * D), lambda i: (i, 1)),
            pl.BlockSpec((MEM_LEN, 2 * MQ_W), lambda i: (i // per_b, 0)),
            pl.BlockSpec((1, D), const), pl.BlockSpec((1, D), const),
            pl.BlockSpec((D, D), const),
            pl.BlockSpec((2 * FN_GW, FN_GW), const),
            pl.BlockSpec((tm, tm), const),
            pl.BlockSpec((FN_W, D), const),
            pl.BlockSpec((MQ_W, D), const),
            pl.BlockSpec((D, D), const),
            pl.BlockSpec((1, D), const),
            pl.BlockSpec((1, D), const), pl.BlockSpec((1, D), const),
            pl.BlockSpec((D, 2 * LANES), const),
            pl.BlockSpec((D, LANES), const),
            pl.BlockSpec((1, LANES), const),
        ],
        out_specs=[
            pl.BlockSpec((tm, D), row),
            pl.BlockSpec((tm, D // 2), row),
            pl.BlockSpec((tm, LANES), row),
            pl.BlockSpec((tm, LANES), row),
            pl.BlockSpec((8, LANES), const),
        ],
        out_shape=outs,
        compiler_params=pltpu.CompilerParams(
            dimension_semantics=("arbitrary",),
            vmem_limit_bytes=58 * MIB),
        name="merge_ln1_router",
    )(x2, og, z, z, mq, gates, kv, lng, lnb, wg, ccs, perm, wf, wm, wo, bo, l1g, l1b,
      wr2, wrh, br)


PLAN_TP = 1024


def _expert_onehots(e, lane):
    onehots = [lane == e[:, k:k + 1] for k in range(TOP_K)]
    mf = jnp.zeros(lane.shape, F32)
    for oh in onehots:
        mf = mf + jnp.where(oh, 1.0, 0.0)
    return onehots, mf


def _plan_kernel(e_ref, tot_ref, dest_ref, cnt_ref, off_ref):
    i = pl.program_id(0)
    tp = PLAN_TP
    lane = lax.broadcasted_iota(I32, (tp, LANES), 1)
    onehots, mf = _expert_onehots(e_ref[...], lane)

    @pl.when(i == 0)
    def _():
        tot = tot_ref[0:1, :]
        padded = jnp.floor((tot + (MOE_BM - 1)) * (1.0 / MOE_BM)) * MOE_BM
        lane1 = lax.broadcasted_iota(I32, (1, LANES), 1)
        inc = padded
        for s in (1, 2, 4, 8, 16, 32, 64):
            inc = inc + jnp.where(lane1 >= s, pltpu.roll(inc, s, 1), 0.0)
        off_ref[...] = inc - padded
        cnt_ref[...] = jnp.zeros_like(cnt_ref)

    ri = lax.broadcasted_iota(I32, (tp, tp), 0)
    ci = lax.broadcasted_iota(I32, (tp, tp), 1)
    ltri = jnp.where(ri > ci, 1.0, 0.0).astype(BF16)
    rank = _dot(ltri, mf.astype(BF16)) + cnt_ref[...] + off_ref[...]
    out = jnp.zeros((tp, LANES), I32)
    for k in range(TOP_K):
        dk = jnp.sum(jnp.where(onehots[k], rank, 0.0), axis=-1, keepdims=True)
        out = jnp.where(lane == k, dk.astype(I32), out)
    dest_ref[...] = out
    cnt_ref[...] += jnp.sum(mf, axis=0, keepdims=True)


def _plan(eidx, cnt):
    tp = PLAN_TP
    return pl.pallas_call(
        _plan_kernel,
        grid=(T // tp,),
        in_specs=[pl.BlockSpec((tp, LANES), lambda i: (i, 0)),
                  pl.BlockSpec((8, LANES), lambda i: (0, 0))],
        out_specs=pl.BlockSpec((tp, LANES), lambda i: (i, 0)),
        out_shape=jax.ShapeDtypeStruct((T, LANES), I32),
        scratch_shapes=[pltpu.VMEM((1, LANES), F32), pltpu.VMEM((1, LANES), F32)],
        compiler_params=pltpu.CompilerParams(dimension_semantics=("arbitrary",)),
        name="route_plan",
    )(eidx, cnt)


MOE_BM = 512
MOE_NW = A_ROWS // MOE_BM + N_EXP
XS_ROWS = MOE_NW * MOE_BM


def _expert_kernel(we_ref, wb_ref, wv_ref, wfe_ref, wsl_ref, wnx_ref,
                   x_ref, wgu_hbm, bgu_ref, wdn_hbm, bdn_ref, o_ref,
                   wgu_f32, wdn_f32, wgu_bf, wdn_bf, sems):
    w = pl.program_id(0)
    e = we_ref[w]

    def weight_copies(expert, slot):
        return (pltpu.make_async_copy(wgu_hbm.at[expert], wgu_f32.at[slot], sems.at[slot, 0]),
                pltpu.make_async_copy(wdn_hbm.at[expert], wdn_f32.at[slot], sems.at[slot, 1]))

    @pl.when(w == 0)
    def _():
        for cp_ in weight_copies(e, 0):
            cp_.start()

    @pl.when(wfe_ref[w] == 1)
    def _():
        slot = wsl_ref[w]
        for cp_ in weight_copies(e, slot):
            cp_.wait()
        wgu_bf[...] = wgu_f32[slot].astype(BF16)
        wdn_bf[...] = wdn_f32[slot].astype(BF16)
        nxt = wnx_ref[w]

        @pl.when(nxt >= 0)
        def _():
            for cp_ in weight_copies(nxt, 1 - slot):
                cp_.start()

    def ffn(rows):
        xlo, xhi = _unpack_bf16_pair(x_ref[rows, :])
        half = D // 2
        gu = (_dot(xlo.astype(BF16), wgu_bf[:half, :]) + _dot(xhi.astype(BF16), wgu_bf[half:, :])
              + bgu_ref[pl.ds(e, 1), :])
        gate = jnp.minimum(gu[:, :D_FF], SW_LIMIT)
        up = jnp.clip(gu[:, D_FF:], -SW_LIMIT, SW_LIMIT)
        act = (up + 1.0) * (gate * jax.nn.sigmoid(SW_ALPHA * gate))
        out = _dot(act.astype(BF16), wdn_bf[...]) + bdn_ref[pl.ds(e, 1), :]
        o_ref[rows, :] = _pack_bf16_pair(out)

    @pl.when(wv_ref[w] == 1)
    def _():
        ffn(slice(None))

    @pl.when(wv_ref[w] == 2)
    def _():
        ffn(slice(0, MOE_BM // 2))


def _experts(meta, xs, w_gu, b_gu, w_down, b_down):
    return pl.pallas_call(
        _expert_kernel,
        grid_spec=pltpu.PrefetchScalarGridSpec(
            num_scalar_prefetch=len(meta),
            grid=(MOE_NW,),
            in_specs=[
                pl.BlockSpec((MOE_BM, D // 2), lambda w, we, wb, *_: (wb[w], 0)),
                pl.BlockSpec(memory_space=pl.ANY),
                pl.BlockSpec((N_EXP, 2 * D_FF), lambda w, *_: (0, 0)),
                pl.BlockSpec(memory_space=pl.ANY),
                pl.BlockSpec((N_EXP, D), lambda w, *_: (0, 0)),
            ],
            out_specs=pl.BlockSpec((MOE_BM, D // 2), lambda w, we, wb, *_: (wb[w], 0)),
            scratch_shapes=[
                pltpu.VMEM((2, D, 2 * D_FF), F32),
                pltpu.VMEM((2, D_FF, D), F32),
                pltpu.VMEM((D, 2 * D_FF), BF16),
                pltpu.VMEM((D_FF, D), BF16),
                pltpu.SemaphoreType.DMA((2, 2)),
            ],
        ),
        out_shape=jax.ShapeDtypeStruct((XS_ROWS, D // 2), U32),
        compiler_params=pltpu.CompilerParams(
            dimension_semantics=("arbitrary",),
            vmem_limit_bytes=56 * MIB),
        name="moe_experts",
    )(*meta, xs, w_gu, b_gu, w_down, b_down)


def _work_items(counts):
    n_e = (counts + MOE_BM - 1) // MOE_BM
    item_end = jnp.cumsum(n_e)
    total = item_end[-1]
    w = jnp.arange(MOE_NW, dtype=I32)
    valid = w < total
    wc = jnp.minimum(w, total - 1)
    e_w = jnp.sum((item_end[None, :] <= wc[:, None]).astype(I32), axis=1)
    e_w = jnp.minimum(e_w, N_EXP - 1)
    rows_here = counts[e_w] - (wc - (item_end - n_e)[e_w]) * MOE_BM
    valid = jnp.where(valid, jnp.where(rows_here <= MOE_BM // 2, 2, 1), 0)
    prev_e = jnp.concatenate([jnp.full((1,), -1, I32), e_w[:-1]])
    fe = (e_w != prev_e).astype(I32)
    slot = (jnp.cumsum(fe) - 1) % 2
    first_at = jnp.where(fe == 1, w, MOE_NW)
    next_first = jnp.concatenate([lax.cummin(first_at, reverse=True)[1:],
                                  jnp.full((1,), MOE_NW, I32)])
    nxt = jnp.where(next_first < MOE_NW, e_w[jnp.minimum(next_first, MOE_NW - 1)], -1)
    return tuple(a.astype(I32) for a in (e_w, wc, valid, fe, slot, nxt))


COMB_TM = 512
SC_CORES = 2
SC_SUBCORES = 16
SC_WORKERS = SC_CORES * SC_SUBCORES
SC_CH = 64
COMB_GROUPS = 4
COMB_TG = T // COMB_GROUPS
SC_ROWS_PER_W = COMB_TG * TOP_K // SC_WORKERS
SC_NCH = SC_ROWS_PER_W // SC_CH


def _sc_gather(table, idx3):
    mesh = plsc.VectorSubcoreMesh(core_axis_name="c", subcore_axis_name="s")

    @functools.partial(
        pl.kernel, mesh=mesh,
        out_type=jax.ShapeDtypeStruct((COMB_TG * TOP_K, D // 2), U32),
        scratch_types=[
            pltpu.VMEM((SC_NCH, SC_CH), I32),
            pltpu.VMEM((2, SC_CH, D // 2), U32),
            pltpu.SemaphoreType.DMA((2,)),
            pltpu.SemaphoreType.DMA((2,)),
        ],
    )
    def k(table_hbm, idx_hbm, out_hbm, idx_v, rows_v, gsem, psem):
        wid = lax.axis_index("s") * SC_CORES + lax.axis_index("c")
        base = wid * SC_ROWS_PER_W
        pltpu.sync_copy(idx_hbm.at[wid], idx_v)

        def gather(j, b):
            return pltpu.make_async_copy(table_hbm.at[idx_v.at[j]], rows_v.at[b], gsem.at[b])

        def put(j, b):
            return pltpu.make_async_copy(rows_v.at[b], out_hbm.at[pl.ds(base + j * SC_CH, SC_CH)],
                                         psem.at[b])

        gather(0, 0).start()

        @pl.loop(0, SC_NCH, step=2)
        def _(j0):
            for b in range(2):
                j = j0 + b

                @pl.when(j + 1 < SC_NCH)
                def _():
                    @pl.when(j >= 1)
                    def _():
                        put(j - 1, 1 - b).wait()
                    gather(j + 1, 1 - b).start()

                gather(j, b).wait()
                put(j, b).start()

        put(SC_NCH - 2, 0).wait()
        put(SC_NCH - 1, 1).wait()

    return k(table, idx3)


SCD_TOK_PER_W = T // SC_WORKERS
SCD_NCH = SCD_TOK_PER_W // SC_CH


def _sc_dispatch(h1p, idx4):
    mesh = plsc.VectorSubcoreMesh(core_axis_name="c", subcore_axis_name="s")

    @functools.partial(
        pl.kernel, mesh=mesh,
        out_type=jax.ShapeDtypeStruct((XS_ROWS, D // 2), U32),
        scratch_types=[
            pltpu.VMEM((SCD_NCH * TOP_K, SC_CH), I32),
            pltpu.VMEM((2, SC_CH, D // 2), U32),
            pltpu.SemaphoreType.DMA((2,)),
            pltpu.SemaphoreType.DMA((2,)),
        ],
    )
    def k(h_hbm, idx_hbm, xs_hbm, idx_v, rows_v, gsem, psem):
        wid = lax.axis_index("s") * SC_CORES + lax.axis_index("c")
        base = wid * SCD_TOK_PER_W
        pltpu.sync_copy(idx_hbm.at[wid], idx_v)

        def get(c, b):
            return pltpu.make_async_copy(h_hbm.at[pl.ds(base + c * SC_CH, SC_CH)], rows_v.at[b],
                                         gsem.at[b])

        def puts(c, b):
            return [pltpu.make_async_copy(rows_v.at[b], xs_hbm.at[idx_v.at[c * TOP_K + kk]],
                                          psem.at[b]) for kk in range(TOP_K)]

        get(0, 0).start()

        @pl.loop(0, SCD_NCH, step=2)
        def _(c0):
            for b in range(2):
                c = c0 + b

                @pl.when(c + 1 < SCD_NCH)
                def _():
                    @pl.when(c >= 1)
                    def _():
                        for cp_ in puts(c - 1, 1 - b):
                            cp_.wait()
                    get(c + 1, 1 - b).start()

                get(c, b).wait()
                for cp_ in puts(c, b):
                    cp_.start()

        for cp_ in puts(SCD_NCH - 2, 0) + puts(SCD_NCH - 1, 1):
            cp_.wait()

    return k(h1p, idx4)


def _combine_dense_kernel(g_ref, h1_ref, tw_ref, lg_ref, lb_ref, o_ref):
    tw = tw_ref[...]
    ylo = jnp.zeros((COMB_TM, D // 2), F32)
    yhi = jnp.zeros((COMB_TM, D // 2), F32)
    for k in range(TOP_K):
        lo, hi = _unpack_bf16_pair(g_ref[k])
        wk = tw[:, k:k + 1]
        ylo = ylo + lo * wk
        yhi = yhi + hi * wk
    ff = jnp.concatenate([ylo, yhi], axis=1)
    o_ref[...] = _ln(DN_ALPHA * h1_ref[...] + ff, lg_ref[...], lb_ref[...])


def _combine_dense(g4, h1, topw, g, b, group):
    tm = COMB_TM
    t0 = group * (COMB_TG // tm)
    return pl.pallas_call(
        _combine_dense_kernel,
        grid=(COMB_TG // tm,),
        in_specs=[
            pl.BlockSpec((TOP_K, tm, D // 2), lambda i: (0, i, 0)),
            pl.BlockSpec((tm, D), lambda i: (t0 + i, 0)),
            pl.BlockSpec((tm, LANES), lambda i: (t0 + i, 0)),
            pl.BlockSpec((1, D), lambda i: (0, 0)),
            pl.BlockSpec((1, D), lambda i: (0, 0)),
        ],
        out_specs=pl.BlockSpec((tm, D), lambda i: (t0 + i, 0)),
        out_shape=jax.ShapeDtypeStruct((T, D), F32),
        input_output_aliases={1: 0},
        compiler_params=pltpu.CompilerParams(
            dimension_semantics=("arbitrary",),
            vmem_limit_bytes=40 * MIB),
        name="moe_combine_dense_ln2",
    )(g4, h1, topw, g, b)


def _pad_cols(a, n):
    return jnp.pad(a, ((0, 0), (0, n - a.shape[1])))


def kernel(x, mem, ln_in_g, ln_in_b, ln_mem_g, ln_mem_b, w_in, b_in, w_decay_f, b_decay_f,
           w_decay_b, b_decay_b, gla_norm_g, w_br_gla, w_br_fnet, w_br_mem, w_mem_kv, w_out,
           b_out, ln1_g, ln1_b, w_router, b_router, w_gu, b_gu, w_down, b_down, ln2_g, ln2_b):
    assert x.shape == (BATCH, SEQ, D) and w_in.shape[0] == 1
    row = lambda a: a.reshape(1, -1)
    x2 = x.reshape(T, D)
    w_in0, b_in0 = w_in[0], b_in[0]
    c_lr, c_fn, c_mq, c_gt = 3072, 3072 + 2 * GLA_LR, 3104 + FN_W, 3104 + FN_W + MQ_W
    w_main = jnp.concatenate([w_in0[:, :c_lr], w_in0[:, c_gt:]], axis=1).astype(BF16)
    b_main = row(jnp.concatenate([b_in0[:c_lr], b_in0[c_gt:]]))
    w_lr = _pad_cols(w_in0[:, c_lr:c_fn], LANES).astype(BF16)
    b_lr = _pad_cols(row(b_in0[c_lr:c_fn]), LANES)
    w_mq = w_in0[:, c_mq:c_gt].astype(BF16)
    b_mq = row(b_in0[c_mq:c_gt])
    w_fn = w_in0[:, c_fn:c_mq].astype(BF16)
    b_fn = row(b_in0[c_fn:c_mq])
    lng, lnb = row(ln_in_g), row(ln_in_b)

    proj, mq, lr = _inproj(x2, lng, lnb, w_main, b_main, w_mq, b_mq, w_lr, b_lr)

    zpad = jnp.zeros((LANES - 2 * GLA_LR, GLA_H * GLA_DK), F32)
    zlr = jnp.zeros((GLA_LR, GLA_H * GLA_DK), F32)
    wdf = jnp.concatenate([w_decay_f[0], zlr, zpad], axis=0).reshape(LANES, GLA_H, GLA_DK)
    wdb = jnp.concatenate([zlr, w_decay_b[0], zpad], axis=0).reshape(LANES, GLA_H, GLA_DK)
    wd = jnp.concatenate([wdf, wdb], axis=2).reshape(LANES, GLA_H * 2 * GLA_DK).astype(BF16)
    wd = jnp.concatenate([wd, wd], axis=0)
    bd = jnp.concatenate([b_decay_f[0].reshape(GLA_H, GLA_DK),
                          b_decay_b[0].reshape(GLA_H, GLA_DK)], axis=1).reshape(1, -1)
    og = _gla(proj, lr, wd, bd, row(gla_norm_g[0]))

    fbig, cwt, swt, f2, ccs, perm = _dft_tables(MERGE_TM)
    x4 = x.reshape(BATCH, FFT_N2, FFT_N1, D)
    z = _fft2(_fft1(x4, lng, lnb, w_fn, b_fn, fbig, cwt, swt), f2)

    kv = _memkv(mem.reshape(BATCH * MEM_LEN, D), row(ln_mem_g), row(ln_mem_b),
                w_mem_kv[0].astype(BF16))

    w_r = _pad_cols(w_router[0], LANES)
    wr_hi = w_r.astype(BF16)
    wr_lo = (w_r - wr_hi.astype(F32)).astype(BF16)
    b_r = jnp.concatenate([row(b_router[0]),
                           jnp.full((1, LANES - N_EXP), NEG_BIG, F32)], axis=1)
    h1, h1p, eidx, topw, cnt = _merge(
        x2, og, z, mq, proj, kv, lng, lnb,
        w_br_gla[0].astype(BF16), ccs, perm, w_br_fnet[0].astype(BF16),
        w_br_mem[0].astype(BF16), w_out[0].astype(BF16), row(b_out[0]),
        row(ln1_g[0]), row(ln1_b[0]), jnp.concatenate([wr_hi, wr_lo], axis=1), wr_hi, b_r)

    dest = _plan(eidx, cnt)
    counts = cnt[0, :N_EXP].astype(I32)
    dest_k = dest[:, :TOP_K]
    idx4 = dest_k.reshape(SC_WORKERS, SCD_NCH, SC_CH, TOP_K).transpose(0, 1, 3, 2)
    xs = _sc_dispatch(h1p, idx4.reshape(SC_WORKERS, SCD_NCH * TOP_K, SC_CH))
    ys = _experts(_work_items(counts), xs, w_gu[0], b_gu[0], w_down[0], b_down[0])
    out = h1
    for grp in range(COMB_GROUPS):
        dest_g = dest_k[grp * COMB_TG:(grp + 1) * COMB_TG].T.reshape(SC_WORKERS, SC_NCH, SC_CH)
        g4 = _sc_gather(ys, dest_g).reshape(TOP_K, COMB_TG, D // 2)
        out = _combine_dense(g4, out, topw, row(ln2_g[0]), row(ln2_b[0]), grp)
    return out.reshape(BATCH, SEQ, D)
```

```python
import functools
import math

import numpy as np
import jax
import jax.numpy as jnp
from jax import lax
from jax.experimental import pallas as pl
from jax.experimental.pallas import tpu as pltpu
from jax.experimental.pallas import tpu_sc as plsc

F32 = jnp.float32
BF16 = jnp.bfloat16
I32 = jnp.int32
U32 = jnp.uint32

D = 1024
BATCH = 4
SEQ = 4096
T = BATCH * SEQ
GLA_H = 4
GLA_DK = 128
GLA_DV = 256
GLA_LR = 16
GLA_TAU = 16.0
GLA_C = 64
FN_G = 4
FN_GW = 128
FN_W = 512
MEM_LEN = 256
MEM_H = 4
MEM_HD = 128
MQ_W = 512
N_EXP = 32
TOP_K = 4
D_FF = 1024
SW_LIMIT = 7.0
SW_ALPHA = 1.702
LN_EPS = 1e-5
RMS_EPS = 1e-6
DN_ALPHA = 2.0 ** 0.25
A_ROWS = T * TOP_K

FFT_N1 = 128
FFT_N2 = 32

LANES = 128
NEG_BIG = -1e30
MIB = 1024 * 1024


def _ln(x, g, b):
    mu = jnp.mean(x, axis=-1, keepdims=True)
    xc = x - mu
    var = jnp.mean(xc * xc, axis=-1, keepdims=True)
    return xc * lax.rsqrt(var + LN_EPS) * g + b


def _dot(a, b):
    return jnp.dot(a, b, preferred_element_type=F32)


def _dot_nt(a, b):
    return lax.dot_general(a, b, (((1,), (1,)), ((), ())), preferred_element_type=F32)


def _dot_tn(a, b):
    return lax.dot_general(a, b, (((0,), (0,)), ((), ())), preferred_element_type=F32)


def _interleave(*stages):
    live = list(stages)
    while live:
        for st in list(live):
            try:
                next(st)
            except StopIteration:
                live.remove(st)


def _split_bf16(a):
    hi = a.astype(BF16)
    return hi, (a - hi.astype(F32)).astype(BF16)


INPROJ_TM = 1024
INPROJ_TN = 3072


PROJ_W = 6 * 1024


def _inproj_kernel(x_ref, g_ref, b_ref, w_ref, bias_ref, wmq_ref, bmq_ref, wlr_ref, blr_ref,
                   proj_ref, mq_ref, lr_ref, hb_ref):
    @pl.when(pl.program_id(1) == 0)
    def _():
        hb = _ln(x_ref[...], g_ref[...], b_ref[...]).astype(BF16)
        hb_ref[...] = hb
        lr_ref[...] = _dot(hb, wlr_ref[...]) + blr_ref[...]
        mq_ref[...] = (_dot(hb, wmq_ref[...]) + bmq_ref[...]).astype(BF16)

    proj_ref[...] = (_dot(hb_ref[...], w_ref[...]) + bias_ref[...]).astype(BF16)


def _inproj(x2, ln_g, ln_b, w_main, b_main, w_mq, b_mq, w_lr, b_lr):
    tm, tn = INPROJ_TM, INPROJ_TN
    nj = PROJ_W // tn
    row = lambda i, j: (i, 0)
    const = lambda i, j: (0, 0)
    outs = (
        jax.ShapeDtypeStruct((T, PROJ_W), BF16),
        jax.ShapeDtypeStruct((T, MQ_W), BF16),
        jax.ShapeDtypeStruct((T, LANES), F32),
    )
    return pl.pallas_call(
        _inproj_kernel,
        grid=(T // tm, nj),
        in_specs=[
            pl.BlockSpec((tm, D), row),
            pl.BlockSpec((1, D), const),
            pl.BlockSpec((1, D), const),
            pl.BlockSpec((D, tn), lambda i, j: (0, j)),
            pl.BlockSpec((1, tn), lambda i, j: (0, j)),
            pl.BlockSpec((D, MQ_W), const),
            pl.BlockSpec((1, MQ_W), const),
            pl.BlockSpec((D, LANES), const),
            pl.BlockSpec((1, LANES), const),
        ],
        out_specs=[
            pl.BlockSpec((tm, tn), lambda i, j: (i, j)),
            pl.BlockSpec((tm, MQ_W), row),
            pl.BlockSpec((tm, LANES), row),
        ],
        out_shape=outs,
        scratch_shapes=[pltpu.VMEM((tm, D), BF16)],
        compiler_params=pltpu.CompilerParams(
            dimension_semantics=("arbitrary", "arbitrary"),
            vmem_limit_bytes=48 * MIB),
        name="ln_inproj",
    )(x2, ln_g, ln_b, w_main, b_main, w_mq, b_mq, w_lr, b_lr)


GLA_BULK = 256
GLA_FIN = 512
GLA_NCH = SEQ // GLA_C
GLA_CPB = GLA_BULK // GLA_C


def _gla_kernel(q_ref, k_ref, v_ref, r_ref, lr_ref, wd_ref, bd_ref, g_ref, cs_ref, o_ref,
                acc_ref, qin_ref, kin_ref, kst_ref, dec_ref, u_ref, stf_ref, stb_ref):
    C = GLA_C
    G = GLA_BULK
    DK = GLA_DK
    NG = SEQ // G
    scale = DK ** -0.5
    ii = lax.broadcasted_iota(I32, (G, G), 0)
    jj = lax.broadcasted_iota(I32, (G, G), 1)
    same = (ii // C) == (jj // C)
    lower = jnp.logical_and(same, ii >= jj)
    upper = jnp.logical_and(same, ii <= jj)
    is_fwd = lax.broadcasted_iota(I32, (G, 2 * DK), 1) < DK
    chunk_of_row = lax.broadcasted_iota(I32, (G, DK), 0) // C

    def stage_a(gi):
        rows = pl.ds(pl.multiple_of(gi * G, G), G)
        z = _dot(jnp.concatenate(_split_bf16(lr_ref[rows, :]), axis=1), wd_ref[...]) + bd_ref[...]
        yield
        la = -(jnp.maximum(-z, 0.0) + jnp.log(1.0 + jnp.exp(-jnp.abs(z)))) * (1.0 / GLA_TAU)
        la_hi, la_lo = _split_bf16(la)
        pre2 = _dot(cs_ref[...], jnp.concatenate([la_hi, la_lo], axis=1))
        yield
        pre = pre2[:, :2 * DK] + pre2[:, 2 * DK:]
        blast = jnp.concatenate(
            [jnp.broadcast_to(pre[ci * C + C - 1:ci * C + C, :], (C, 2 * DK))
             for ci in range(GLA_CPB)], axis=0)
        b = jnp.where(is_fwd, pre, blast - pre + la)
        qf32 = q_ref[rows, :].astype(F32)
        kf32 = k_ref[rows, :].astype(F32)
        q2 = jnp.concatenate([qf32, qf32], axis=1)
        k2 = jnp.concatenate([kf32, kf32], axis=1)
        qin_ref[rows, :] = (q2 * (scale * jnp.exp(b))).astype(BF16)
        kin_ref[rows, :] = (k2 * jnp.exp(-b)).astype(BF16)
        kst_ref[rows, :] = (k2 * jnp.exp(blast - b)).astype(BF16)
        dec = jnp.exp(blast)
        for ci in range(GLA_CPB):
            dec_ref[pl.ds(gi * GLA_CPB + ci, 1), :] = dec[ci * C:ci * C + 1, :]

    def stage_b(gi):
        rows = pl.ds(pl.multiple_of(gi * G, G), G)
        qi = qin_ref[rows, :]
        ki = kin_ref[rows, :]
        ks = kst_ref[rows, :]
        vb = v_ref[rows, :]
        att_f = _dot_nt(qi[:, :DK], ki[:, :DK])
        att_b = _dot_nt(qi[:, DK:], ki[:, DK:])
        yield
        att = jnp.where(lower, att_f, 0.0) + jnp.where(upper, att_b, 0.0)
        acc_ref[rows, :] = _dot(att.astype(BF16), vb)
        yield
        ksb = jnp.concatenate(
            [jnp.where(chunk_of_row == ci, ks[:, d * DK:(d + 1) * DK], jnp.zeros((G, DK), BF16))
             for d in range(2) for ci in range(GLA_CPB)], axis=1)
        u = _dot_tn(vb, ksb)
        for d in range(2):
            for ci in range(GLA_CPB):
                col = (d * GLA_CPB + ci) * DK
                u_ref[d, gi * GLA_CPB + ci] = u[:, col:col + DK]

    _interleave(stage_a(0), stage_a(1))

    def bulk(i, carry):
        g = 2 * i
        _interleave(stage_b(g - 2), stage_a(g), stage_b(g - 1), stage_a(g + 1))
        return carry

    lax.fori_loop(1, NG // 2, bulk, 0)
    _interleave(stage_b(NG - 2), stage_b(NG - 1))

    stf_ref[...] = jnp.zeros_like(stf_ref)
    stb_ref[...] = jnp.zeros_like(stb_ref)

    def one(n, d, st_ref):
        lanes = slice(d * DK, (d + 1) * DK)
        rows = pl.ds(pl.multiple_of(n * C, C), C)
        st = st_ref[...]
        acc_ref[rows, :] += _dot_nt(qin_ref[rows, lanes], st.astype(BF16))
        st_ref[...] = st * dec_ref[pl.ds(n, 1), :][:, lanes] + u_ref[d, n]

    def step(i, carry):
        one(i, 0, stf_ref)
        one(GLA_NCH - 1 - i, 1, stb_ref)
        return carry

    lax.fori_loop(0, GLA_NCH, step, 0, unroll=4)

    def fin(gi, carry):
        rows = pl.ds(pl.multiple_of(gi * GLA_FIN, GLA_FIN), GLA_FIN)
        o = acc_ref[rows, :]
        o = o * lax.rsqrt(jnp.mean(o * o, axis=-1, keepdims=True) + RMS_EPS) * g_ref[...]
        rg = r_ref[rows, :].astype(F32)
        o_ref[rows, :] = (o * (rg * jax.nn.sigmoid(rg))).astype(BF16)
        return carry

    lax.fori_loop(0, SEQ // GLA_FIN, fin, 0)


def _gla(proj, lr, wd, bd, g):
    i = np.arange(GLA_BULK)
    cs = ((i[:, None] // GLA_C) == (i[None, :] // GLA_C)) & (i[:, None] >= i[None, :])
    cs = jnp.asarray(cs, dtype=F32).astype(BF16)
    v_blk = 1024 // GLA_DV
    return pl.pallas_call(
        _gla_kernel,
        grid=(BATCH, GLA_H),
        in_specs=[
            pl.BlockSpec((SEQ, GLA_DK), lambda b, h: (b, h)),
            pl.BlockSpec((SEQ, GLA_DK), lambda b, h: (b, GLA_H + h)),
            pl.BlockSpec((SEQ, GLA_DV), lambda b, h: (b, v_blk + h)),
            pl.BlockSpec((SEQ, GLA_DV), lambda b, h: (b, 2 * v_blk + h)),
            pl.BlockSpec((SEQ, LANES), lambda b, h: (b, 0)),
            pl.BlockSpec((2 * LANES, 2 * GLA_DK), lambda b, h: (0, h)),
            pl.BlockSpec((1, 2 * GLA_DK), lambda b, h: (0, h)),
            pl.BlockSpec((1, GLA_DV), lambda b, h: (0, 0)),
            pl.BlockSpec((GLA_BULK, GLA_BULK), lambda b, h: (0, 0)),
        ],
        out_specs=pl.BlockSpec((SEQ, GLA_DV), lambda b, h: (b, h)),
        out_shape=jax.ShapeDtypeStruct((T, GLA_H * GLA_DV), BF16),
        scratch_shapes=[
            pltpu.VMEM((SEQ, GLA_DV), F32),
            pltpu.VMEM((SEQ, 2 * GLA_DK), BF16),
            pltpu.VMEM((SEQ, 2 * GLA_DK), BF16),
            pltpu.VMEM((SEQ, 2 * GLA_DK), BF16),
            pltpu.VMEM((GLA_NCH, 2 * GLA_DK), F32),
            pltpu.VMEM((2, GLA_NCH, GLA_DV, GLA_DK), F32),
            pltpu.VMEM((GLA_DV, GLA_DK), F32),
            pltpu.VMEM((GLA_DV, GLA_DK), F32),
        ],
        compiler_params=pltpu.CompilerParams(
            dimension_semantics=("arbitrary", "arbitrary"),
            vmem_limit_bytes=58 * MIB),
        name="gla",
    )(proj, proj, proj, proj, lr, wd, bd, g, cs)


FFT1_S = 16
FFT1_ROWS = FFT_N2 * FFT1_S
FFT2_KB = 8


def _fft1_kernel(x_ref, g_ref, b_ref, w_ref, bias_ref, fbig_ref, cw_ref, sw_ref, o_ref):
    xv = x_ref[...].reshape(FFT1_ROWS, D)
    hb = _ln(xv, g_ref[...], b_ref[...]).astype(BF16)
    fn = (_dot(hb, w_ref[...]) + bias_ref[...]).astype(BF16)
    a = _dot(fbig_ref[...], fn)
    ar = a[:FFT1_ROWS]
    ai = a[FFT1_ROWS:]
    cw = jnp.concatenate([cw_ref[...]] * (FN_W // LANES), axis=1)
    sw = jnp.concatenate([sw_ref[...]] * (FN_W // LANES), axis=1)
    o_ref[0] = (ar * cw + ai * sw).reshape(FFT_N2, FFT1_S, FN_W).astype(BF16)
    o_ref[1] = (ai * cw - ar * sw).reshape(FFT_N2, FFT1_S, FN_W).astype(BF16)


def _fft1(x4, ln_g, ln_b, w_fn, b_fn, fbig, cwt, swt):
    s = FFT1_S
    const = lambda b, j: (0, 0)
    return pl.pallas_call(
        _fft1_kernel,
        grid=(BATCH, FFT_N1 // s),
        in_specs=[
            pl.BlockSpec((None, FFT_N2, s, D), lambda b, j: (b, 0, j, 0)),
            pl.BlockSpec((1, D), const),
            pl.BlockSpec((1, D), const),
            pl.BlockSpec((D, FN_W), const),
            pl.BlockSpec((1, FN_W), const),
            pl.BlockSpec((2 * FFT1_ROWS, FFT1_ROWS), const),
            pl.BlockSpec((None, FFT1_ROWS, LANES), lambda b, j: (j, 0, 0)),
            pl.BlockSpec((None, FFT1_ROWS, LANES), lambda b, j: (j, 0, 0)),
        ],
        out_specs=pl.BlockSpec((None, 2, FFT_N2, s, FN_W), lambda b, j: (b, 0, 0, j, 0)),
        out_shape=jax.ShapeDtypeStruct((BATCH, 2, FFT_N2, FFT_N1, FN_W), BF16),
        compiler_params=pltpu.CompilerParams(
            dimension_semantics=("arbitrary", "arbitrary"),
            vmem_limit_bytes=40 * MIB),
        name="fft_stage1",
    )(x4, ln_g, ln_b, w_fn, b_fn, fbig, cwt, swt)


def _fft2_kernel(d_ref, f2_ref, o_ref):
    f2 = f2_ref[...]
    for kk in range(FFT2_KB):
        z = _dot(f2, jnp.concatenate([d_ref[0, kk], d_ref[1, kk]], axis=0))
        o_ref[0, kk] = z[:FFT_N1].astype(BF16)
        o_ref[1, kk] = z[FFT_N1:].astype(BF16)


def _fft2(dmat, f2):
    kb = FFT2_KB
    blk = (None, 2, kb, FFT_N1, FN_W)
    return pl.pallas_call(
        _fft2_kernel,
        grid=(BATCH, FFT_N2 // kb),
        in_specs=[
            pl.BlockSpec(blk, lambda b, j: (b, 0, j, 0, 0)),
            pl.BlockSpec((2 * FFT_N1, 2 * FFT_N1), lambda b, j: (0, 0)),
        ],
        out_specs=pl.BlockSpec(blk, lambda b, j: (b, 0, j, 0, 0)),
        out_shape=jax.ShapeDtypeStruct((BATCH, 2, FFT_N2, FFT_N1, FN_W), BF16),
        compiler_params=pltpu.CompilerParams(
            dimension_semantics=("arbitrary", "arbitrary")),
        name="fft_stage2",
    )(dmat, f2)


def _dft_tables(merge_tm):
    s = FFT1_S
    n2 = np.arange(FFT_N2, dtype=np.float64)
    n1 = np.arange(FFT_N1, dtype=np.float64)
    th = 2.0 * np.pi * np.outer(n2, n2) / FFT_N2
    f1 = np.stack([np.cos(th), -np.sin(th)]) / math.sqrt(SEQ)
    fbig = np.einsum("rkn,st->rksnt", f1, np.eye(s)).reshape(2 * FFT_N2 * s, FFT_N2 * s)
    tw = 2.0 * np.pi * np.outer(n2, n1) / SEQ
    tw = tw.reshape(FFT_N2, FFT_N1 // s, s).transpose(1, 0, 2).reshape(FFT_N1 // s, FFT_N2 * s)
    cwt = np.broadcast_to(np.cos(tw)[:, :, None], tw.shape + (LANES,))
    swt = np.broadcast_to(np.sin(tw)[:, :, None], tw.shape + (LANES,))
    th1 = 2.0 * np.pi * np.outer(n1, n1) / FFT_N1
    c1, s1 = np.cos(th1), np.sin(th1)
    f2 = np.block([[c1, s1], [-s1, c1]])
    cc = np.arange(FN_GW, dtype=np.float64)
    thc = 2.0 * np.pi * np.outer(cc, cc) / FN_GW
    ccs = np.concatenate([np.cos(thc), np.sin(thc)], axis=0) / math.sqrt(FN_GW)
    k1n = merge_tm // FFT_N2
    r = np.arange(merge_tm)
    perm = np.zeros((merge_tm, merge_tm))
    perm[r, (r % FFT_N2) * k1n + r // FFT_N2] = 1.0
    as32 = lambda a: jnp.asarray(np.ascontiguousarray(a), dtype=F32)
    return (as32(fbig).astype(BF16), as32(cwt), as32(swt), as32(f2).astype(BF16),
            as32(ccs).astype(BF16), as32(perm).astype(BF16))


def _memkv_kernel(m_ref, g_ref, b_ref, w_ref, o_ref):
    mn = _ln(m_ref[...], g_ref[...], b_ref[...]).astype(BF16)
    o_ref[...] = _dot(mn, w_ref[...]).astype(BF16)


def _memkv(mem2, g, b, w):
    return pl.pallas_call(
        _memkv_kernel,
        grid=(BATCH,),
        in_specs=[
            pl.BlockSpec((MEM_LEN, D), lambda i: (i, 0)),
            pl.BlockSpec((1, D), lambda i: (0, 0)),
            pl.BlockSpec((1, D), lambda i: (0, 0)),
            pl.BlockSpec((D, 2 * MQ_W), lambda i: (0, 0)),
        ],
        out_specs=pl.BlockSpec((MEM_LEN, 2 * MQ_W), lambda i: (i, 0)),
        out_shape=jax.ShapeDtypeStruct((BATCH * MEM_LEN, 2 * MQ_W), BF16),
        compiler_params=pltpu.CompilerParams(dimension_semantics=("arbitrary",)),
        name="mem_kv",
    )(mem2, g, b, w)


MERGE_TM = 512
MERGE_K1 = MERGE_TM // FFT_N2


def _pack_bf16_pair(v):
    n = v.shape[1] // 2
    bits = lax.bitcast_convert_type(v.astype(BF16).astype(F32), U32)
    return (bits[:, n:] & jnp.uint32(0xFFFF0000)) | (bits[:, :n] >> 16)


def _unpack_bf16_pair(p):
    lo = lax.bitcast_convert_type(p << 16, F32)
    hi = lax.bitcast_convert_type(p & jnp.uint32(0xFFFF0000), F32)
    return lo, hi


def _merge_kernel(x_ref, og_ref, zr_ref, zi_ref, mq_ref, gt_ref, kv_ref,
                  lng_ref, lnb_ref, wg_ref, ccs_ref, perm_ref, wf_ref, wm_ref, wo_ref, bo_ref,
                  l1g_ref, l1b_ref, wr2_ref, wrh_ref, br_ref,
                  h1_ref, h1p_ref, eidx_ref, topw_ref, cnt_ref):
    tm = MERGE_TM
    y = {}

    def branch_fnet():
        zr = zr_ref[...].reshape(tm, FN_W)
        zi = zi_ref[...].reshape(tm, FN_W)
        ys = []
        for g in range(FN_G):
            sl = slice(g * FN_GW, (g + 1) * FN_GW)
            ys.append(_dot(jnp.concatenate([zr[:, sl], zi[:, sl]], axis=1), ccs_ref[...]))
        yield
        yp = _dot(perm_ref[...], jnp.concatenate(ys, axis=1).astype(BF16))
        yield
        y["fnet"] = _dot(yp.astype(BF16), wf_ref[...])

    def branch_mem():
        heads = [slice(hd * MEM_HD, (hd + 1) * MEM_HD) for hd in range(MEM_H)]
        ss = [_dot_nt(mq_ref[:, sl], kv_ref[:, sl]) for sl in heads]
        yield
        oms = []
        for hd, s in enumerate(ss):
            s = s * (MEM_HD ** -0.5)
            s = s - jnp.max(s, axis=-1, keepdims=True)
            p = jnp.exp(s)
            p = p * (1.0 / jnp.sum(p, axis=-1, keepdims=True))
            oms.append(_dot(p.astype(BF16),
                            kv_ref[:, MQ_W + hd * MEM_HD:MQ_W + (hd + 1) * MEM_HD]))
        yield
        y["mem"] = _dot(jnp.concatenate(oms, axis=1).astype(BF16), wm_ref[...])

    def branch_gla():
        y["gla"] = _dot(og_ref[...], wg_ref[...])
        yield

    _interleave(branch_fnet(), branch_mem(), branch_gla())

    def gate(c):
        return 0.5 + 0.5 * jnp.tanh(0.5 * gt_ref[:, c * D:(c + 1) * D].astype(F32))

    merged = gate(0) * y["gla"] + gate(1) * y["fnet"] + gate(2) * y["mem"]
    mix = _dot(merged.astype(BF16), wo_ref[...]) + bo_ref[...]
    h = _ln(x_ref[...], lng_ref[...], lnb_ref[...])
    h1 = _ln(DN_ALPHA * h + mix, l1g_ref[...], l1b_ref[...])
    h1_ref[...] = h1
    h1p_ref[...] = _pack_bf16_pair(h1)

    h_hi, h_lo = _split_bf16(h1)
    d2 = _dot(h_hi, wr2_ref[...])
    l = d2[:, :LANES] + d2[:, LANES:] + _dot(h_lo, wrh_ref[...]) + br_ref[...]
    lane = lax.broadcasted_iota(I32, (tm, LANES), 1)
    vals, idxs = [], []
    for _ in range(TOP_K):
        m = jnp.max(l, axis=-1, keepdims=True)
        idx = jnp.min(jnp.where(l == m, lane, LANES), axis=-1, keepdims=True)
        vals.append(m)
        idxs.append(idx)
        l = jnp.where(lane == idx, -jnp.inf, l)
    es = [jnp.exp(v - vals[0]) for v in vals]
    den = es[0] + es[1] + es[2] + es[3]
    eo = jnp.zeros((tm, LANES), I32)
    wo = jnp.zeros((tm, LANES), F32)
    chosen = jnp.zeros((tm, LANES), F32)
    for k in range(TOP_K):
        eo = jnp.where(lane == k, idxs[k], eo)
        wo = jnp.where(lane == k, es[k] / den, wo)
        chosen = chosen + jnp.where(lane == idxs[k], 1.0, 0.0)
    eidx_ref[...] = eo
    topw_ref[...] = wo

    @pl.when(pl.program_id(0) == 0)
    def _():
        cnt_ref[...] = jnp.zeros_like(cnt_ref)

    cnt_ref[...] += jnp.broadcast_to(jnp.sum(chosen, axis=0, keepdims=True), cnt_ref.shape)


def _merge(x2, og, z, mq, gates, kv, lng, lnb, wg, ccs, perm, wf, wm, wo, bo, l1g, l1b,
           wr2, wrh, br):
    tm = MERGE_TM
    per_b = SEQ // tm
    row = lambda i: (i, 0)
    const = lambda i: (0, 0)
    zblk = (None, None, FFT_N2, MERGE_K1, FN_W)
    outs = (
        jax.ShapeDtypeStruct((T, D), F32),
        jax.ShapeDtypeStruct((T, D // 2), U32),
        jax.ShapeDtypeStruct((T, LANES), I32),
        jax.ShapeDtypeStruct((T, LANES), F32),
        jax.ShapeDtypeStruct((8, LANES), F32),
    )
    return pl.pallas_call(
        _merge_kernel,
        grid=(T // tm,),
        in_specs=[
            pl.BlockSpec((tm, D), row),
            pl.BlockSpec((tm, D), row),
            pl.BlockSpec(zblk, lambda i: (i // per_b, 0, 0, i % per_b, 0)),
            pl.BlockSpec(zblk, lambda i: (i // per_b, 1, 0, i % per_b, 0)),
            pl.BlockSpec((tm, MQ_W), row),
            pl.BlockSpec((tm, 3 * D), lambda i: (i, 1)),
            pl.BlockSpec((MEM_LEN, 2 * MQ_W), lambda i: (i // per_b, 0)),
            pl.BlockSpec((1, D), const), pl.BlockSpec((1, D), const),
            pl.BlockSpec((D, D), const),
            pl.BlockSpec((2 * FN_GW, FN_GW), const),
            pl.BlockSpec((tm, tm), const),
            pl.BlockSpec((FN_W, D), const),
            pl.BlockSpec((MQ_W, D), const),
            pl.BlockSpec((D, D), const),
            pl.BlockSpec((1, D), const),
            pl.BlockSpec((1, D), const), pl.BlockSpec((1, D), const),
            pl.BlockSpec((D, 2 * LANES), const),
            pl.BlockSpec((D, LANES), const),
            pl.BlockSpec((1, LANES), const),
        ],
        out_specs=[
            pl.BlockSpec((tm, D), row),
            pl.BlockSpec((tm, D // 2), row),
            pl.BlockSpec((tm, LANES), row),
            pl.BlockSpec((tm, LANES), row),
            pl.BlockSpec((8, LANES), const),
        ],
        out_shape=outs,
        compiler_params=pltpu.CompilerParams(
            dimension_semantics=("arbitrary",),
            vmem_limit_bytes=58 * MIB),
        name="merge_ln1_router",
    )(x2, og, z, z, mq, gates, kv, lng, lnb, wg, ccs, perm, wf, wm, wo, bo, l1g, l1b,
      wr2, wrh, br)


PLAN_TP = 1024


def _expert_onehots(e, lane):
    onehots = [lane == e[:, k:k + 1] for k in range(TOP_K)]
    mf = jnp.zeros(lane.shape, F32)
    for oh in onehots:
        mf = mf + jnp.where(oh, 1.0, 0.0)
    return onehots, mf


def _plan_kernel(e_ref, tot_ref, dest_ref, cnt_ref, off_ref):
    i = pl.program_id(0)
    tp = PLAN_TP
    lane = lax.broadcasted_iota(I32, (tp, LANES), 1)
    onehots, mf = _expert_onehots(e_ref[...], lane)

    @pl.when(i == 0)
    def _():
        tot = tot_ref[0:1, :]
        padded = jnp.floor((tot + (MOE_BM - 1)) * (1.0 / MOE_BM)) * MOE_BM
        lane1 = lax.broadcasted_iota(I32, (1, LANES), 1)
        inc = padded
        for s in (1, 2, 4, 8, 16, 32, 64):
            inc = inc + jnp.where(lane1 >= s, pltpu.roll(inc, s, 1), 0.0)
        off_ref[...] = inc - padded
        cnt_ref[...] = jnp.zeros_like(cnt_ref)

    ri = lax.broadcasted_iota(I32, (tp, tp), 0)
    ci = lax.broadcasted_iota(I32, (tp, tp), 1)
    ltri = jnp.where(ri > ci, 1.0, 0.0).astype(BF16)
    rank = _dot(ltri, mf.astype(BF16)) + cnt_ref[...] + off_ref[...]
    out = jnp.zeros((tp, LANES), I32)
    for k in range(TOP_K):
        dk = jnp.sum(jnp.where(onehots[k], rank, 0.0), axis=-1, keepdims=True)
        out = jnp.where(lane == k, dk.astype(I32), out)
    dest_ref[...] = out
    cnt_ref[...] += jnp.sum(mf, axis=0, keepdims=True)


def _plan(eidx, cnt):
    tp = PLAN_TP
    return pl.pallas_call(
        _plan_kernel,
        grid=(T // tp,),
        in_specs=[pl.BlockSpec((tp, LANES), lambda i: (i, 0)),
                  pl.BlockSpec((8, LANES), lambda i: (0, 0))],
        out_specs=pl.BlockSpec((tp, LANES), lambda i: (i, 0)),
        out_shape=jax.ShapeDtypeStruct((T, LANES), I32),
        scratch_shapes=[pltpu.VMEM((1, LANES), F32), pltpu.VMEM((1, LANES), F32)],
        compiler_params=pltpu.CompilerParams(dimension_semantics=("arbitrary",)),
        name="route_plan",
    )(eidx, cnt)


MOE_BM = 512
MOE_NW = A_ROWS // MOE_BM + N_EXP
XS_ROWS = MOE_NW * MOE_BM


def _expert_kernel(we_ref, wb_ref, wv_ref, wfe_ref, wsl_ref, wnx_ref,
                   x_ref, wgu_hbm, bgu_ref, wdn_hbm, bdn_ref, o_ref,
                   wgu_f32, wdn_f32, wgu_bf, wdn_bf, sems):
    w = pl.program_id(0)
    e = we_ref[w]

    def weight_copies(expert, slot):
        return (pltpu.make_async_copy(wgu_hbm.at[expert], wgu_f32.at[slot], sems.at[slot, 0]),
                pltpu.make_async_copy(wdn_hbm.at[expert], wdn_f32.at[slot], sems.at[slot, 1]))

    @pl.when(w == 0)
    def _():
        for cp_ in weight_copies(e, 0):
            cp_.start()

    @pl.when(wfe_ref[w] == 1)
    def _():
        slot = wsl_ref[w]
        for cp_ in weight_copies(e, slot):
            cp_.wait()
        wgu_bf[...] = wgu_f32[slot].astype(BF16)
        wdn_bf[...] = wdn_f32[slot].astype(BF16)
        nxt = wnx_ref[w]

        @pl.when(nxt >= 0)
        def _():
            for cp_ in weight_copies(nxt, 1 - slot):
                cp_.start()

    def ffn(rows):
        xlo, xhi = _unpack_bf16_pair(x_ref[rows, :])
        xb = jnp.concatenate([xlo.astype(BF16), xhi.astype(BF16)], axis=1)
        bgu = bgu_ref[pl.ds(e, 1), :]
        hw = D_FF // 2
        gus = []
        for hf in range(2):
            gc = slice(hf * hw, (hf + 1) * hw)
            uc = slice(D_FF + hf * hw, D_FF + (hf + 1) * hw)
            gus.append((_dot(xb, wgu_bf[:, gc]) + bgu[:, gc], _dot(xb, wgu_bf[:, uc]) + bgu[:, uc]))
        out = bdn_ref[pl.ds(e, 1), :]
        for hf, (g, u) in enumerate(gus):
            gate = jnp.minimum(g, SW_LIMIT)
            up = jnp.clip(u, -SW_LIMIT, SW_LIMIT)
            act = (up + 1.0) * (gate * jax.nn.sigmoid(SW_ALPHA * gate))
            out = out + _dot(act.astype(BF16), wdn_bf[hf * hw:(hf + 1) * hw, :])
        o_ref[rows, :] = _pack_bf16_pair(out)

    @pl.when(wv_ref[w] == 1)
    def _():
        ffn(slice(None))

    @pl.when(wv_ref[w] == 2)
    def _():
        ffn(slice(0, MOE_BM // 2))


def _experts(meta, xs, w_gu, b_gu, w_down, b_down):
    return pl.pallas_call(
        _expert_kernel,
        grid_spec=pltpu.PrefetchScalarGridSpec(
            num_scalar_prefetch=len(meta),
            grid=(MOE_NW,),
            in_specs=[
                pl.BlockSpec((MOE_BM, D // 2), lambda w, we, wb, *_: (wb[w], 0)),
                pl.BlockSpec(memory_space=pl.ANY),
                pl.BlockSpec((N_EXP, 2 * D_FF), lambda w, *_: (0, 0)),
                pl.BlockSpec(memory_space=pl.ANY),
                pl.BlockSpec((N_EXP, D), lambda w, *_: (0, 0)),
            ],
            out_specs=pl.BlockSpec((MOE_BM, D // 2), lambda w, we, wb, *_: (wb[w], 0)),
            scratch_shapes=[
                pltpu.VMEM((2, D, 2 * D_FF), F32),
                pltpu.VMEM((2, D_FF, D), F32),
                pltpu.VMEM((D, 2 * D_FF), BF16),
                pltpu.VMEM((D_FF, D), BF16),
                pltpu.SemaphoreType.DMA((2, 2)),
            ],
        ),
        out_shape=jax.ShapeDtypeStruct((XS_ROWS, D // 2), U32),
        compiler_params=pltpu.CompilerParams(
            dimension_semantics=("arbitrary",),
            vmem_limit_bytes=56 * MIB),
        name="moe_experts",
    )(*meta, xs, w_gu, b_gu, w_down, b_down)


def _work_items(counts):
    n_e = (counts + MOE_BM - 1) // MOE_BM
    item_end = jnp.cumsum(n_e)
    total = item_end[-1]
    w = jnp.arange(MOE_NW, dtype=I32)
    valid = w < total
    wc = jnp.minimum(w, total - 1)
    e_w = jnp.sum((item_end[None, :] <= wc[:, None]).astype(I32), axis=1)
    e_w = jnp.minimum(e_w, N_EXP - 1)
    rows_here = counts[e_w] - (wc - (item_end - n_e)[e_w]) * MOE_BM
    valid = jnp.where(valid, jnp.where(rows_here <= MOE_BM // 2, 2, 1), 0)
    prev_e = jnp.concatenate([jnp.full((1,), -1, I32), e_w[:-1]])
    fe = (e_w != prev_e).astype(I32)
    slot = (jnp.cumsum(fe) - 1) % 2
    first_at = jnp.where(fe == 1, w, MOE_NW)
    next_first = jnp.concatenate([lax.cummin(first_at, reverse=True)[1:],
                                  jnp.full((1,), MOE_NW, I32)])
    nxt = jnp.where(next_first < MOE_NW, e_w[jnp.minimum(next_first, MOE_NW - 1)], -1)
    return tuple(a.astype(I32) for a in (e_w, wc, valid, fe, slot, nxt))


COMB_TM = 512
SC_CORES = 2
SC_SUBCORES = 16
SC_WORKERS = SC_CORES * SC_SUBCORES
SC_CH = 64
COMB_GROUPS = 4
COMB_TG = T // COMB_GROUPS
SC_ROWS_PER_W = COMB_TG * TOP_K // SC_WORKERS
SC_NCH = SC_ROWS_PER_W // SC_CH


def _sc_gather(table, idx3):
    mesh = plsc.VectorSubcoreMesh(core_axis_name="c", subcore_axis_name="s")

    @functools.partial(
        pl.kernel, mesh=mesh,
        out_type=jax.ShapeDtypeStruct((COMB_TG * TOP_K, D // 2), U32),
        scratch_types=[
            pltpu.VMEM((SC_NCH, SC_CH), I32),
            pltpu.VMEM((2, SC_CH, D // 2), U32),
            pltpu.SemaphoreType.DMA((2,)),
            pltpu.SemaphoreType.DMA((2,)),
        ],
    )
    def k(table_hbm, idx_hbm, out_hbm, idx_v, rows_v, gsem, psem):
        wid = lax.axis_index("s") * SC_CORES + lax.axis_index("c")
        base = wid * SC_ROWS_PER_W
        pltpu.sync_copy(idx_hbm.at[wid], idx_v)

        def gather(j, b):
            return pltpu.make_async_copy(table_hbm.at[idx_v.at[j]], rows_v.at[b], gsem.at[b])

        def put(j, b):
            return pltpu.make_async_copy(rows_v.at[b], out_hbm.at[pl.ds(base + j * SC_CH, SC_CH)],
                                         psem.at[b])

        gather(0, 0).start()

        @pl.loop(0, SC_NCH, step=2)
        def _(j0):
            for b in range(2):
                j = j0 + b

                @pl.when(j + 1 < SC_NCH)
                def _():
                    @pl.when(j >= 1)
                    def _():
                        put(j - 1, 1 - b).wait()
                    gather(j + 1, 1 - b).start()

                gather(j, b).wait()
                put(j, b).start()

        put(SC_NCH - 2, 0).wait()
        put(SC_NCH - 1, 1).wait()

    return k(table, idx3)


SCD_TOK_PER_W = T // SC_WORKERS
SCD_NCH = SCD_TOK_PER_W // SC_CH


def _sc_dispatch(h1p, idx4):
    mesh = plsc.VectorSubcoreMesh(core_axis_name="c", subcore_axis_name="s")

    @functools.partial(
        pl.kernel, mesh=mesh,
        out_type=jax.ShapeDtypeStruct((XS_ROWS, D // 2), U32),
        scratch_types=[
            pltpu.VMEM((SCD_NCH * TOP_K, SC_CH), I32),
            pltpu.VMEM((2, SC_CH, D // 2), U32),
            pltpu.SemaphoreType.DMA((2,)),
            pltpu.SemaphoreType.DMA((2,)),
        ],
    )
    def k(h_hbm, idx_hbm, xs_hbm, idx_v, rows_v, gsem, psem):
        wid = lax.axis_index("s") * SC_CORES + lax.axis_index("c")
        base = wid * SCD_TOK_PER_W
        pltpu.sync_copy(idx_hbm.at[wid], idx_v)

        def get(c, b):
            return pltpu.make_async_copy(h_hbm.at[pl.ds(base + c * SC_CH, SC_CH)], rows_v.at[b],
                                         gsem.at[b])

        def puts(c, b):
            return [pltpu.make_async_copy(rows_v.at[b], xs_hbm.at[idx_v.at[c * TOP_K + kk]],
                                          psem.at[b]) for kk in range(TOP_K)]

        get(0, 0).start()

        @pl.loop(0, SCD_NCH, step=2)
        def _(c0):
            for b in range(2):
                c = c0 + b

                @pl.when(c + 1 < SCD_NCH)
                def _():
                    @pl.when(c >= 1)
                    def _():
                        for cp_ in puts(c - 1, 1 - b):
                            cp_.wait()
                    get(c + 1, 1 - b).start()

                get(c, b).wait()
                for cp_ in puts(c, b):
                    cp_.start()

        for cp_ in puts(SCD_NCH - 2, 0) + puts(SCD_NCH - 1, 1):
            cp_.wait()

    return k(h1p, idx4)


def _combine_dense_kernel(g_ref, h1_ref, tw_ref, lg_ref, lb_ref, o_ref):
    tw = tw_ref[...]
    ylo = jnp.zeros((COMB_TM, D // 2), F32)
    yhi = jnp.zeros((COMB_TM, D // 2), F32)
    for k in range(TOP_K):
        lo, hi = _unpack_bf16_pair(g_ref[k])
        wk = tw[:, k:k + 1]
        ylo = ylo + lo * wk
        yhi = yhi + hi * wk
    ff = jnp.concatenate([ylo, yhi], axis=1)
    o_ref[...] = _ln(DN_ALPHA * h1_ref[...] + ff, lg_ref[...], lb_ref[...])


def _combine_dense(g4, h1, topw, g, b, group):
    tm = COMB_TM
    t0 = group * (COMB_TG // tm)
    return pl.pallas_call(
        _combine_dense_kernel,
        grid=(COMB_TG // tm,),
        in_specs=[
            pl.BlockSpec((TOP_K, tm, D // 2), lambda i: (0, i, 0)),
            pl.BlockSpec((tm, D), lambda i: (t0 + i, 0)),
            pl.BlockSpec((tm, LANES), lambda i: (t0 + i, 0)),
            pl.BlockSpec((1, D), lambda i: (0, 0)),
            pl.BlockSpec((1, D), lambda i: (0, 0)),
        ],
        out_specs=pl.BlockSpec((tm, D), lambda i: (t0 + i, 0)),
        out_shape=jax.ShapeDtypeStruct((T, D), F32),
        input_output_aliases={1: 0},
        compiler_params=pltpu.CompilerParams(
            dimension_semantics=("arbitrary",),
            vmem_limit_bytes=40 * MIB),
        name="moe_combine_dense_ln2",
    )(g4, h1, topw, g, b)


def _pad_cols(a, n):
    return jnp.pad(a, ((0, 0), (0, n - a.shape[1])))


def kernel(x, mem, ln_in_g, ln_in_b, ln_mem_g, ln_mem_b, w_in, b_in, w_decay_f, b_decay_f,
           w_decay_b, b_decay_b, gla_norm_g, w_br_gla, w_br_fnet, w_br_mem, w_mem_kv, w_out,
           b_out, ln1_g, ln1_b, w_router, b_router, w_gu, b_gu, w_down, b_down, ln2_g, ln2_b):
    assert x.shape == (BATCH, SEQ, D) and w_in.shape[0] == 1
    row = lambda a: a.reshape(1, -1)
    x2 = x.reshape(T, D)
    w_in0, b_in0 = w_in[0], b_in[0]
    c_lr, c_fn, c_mq, c_gt = 3072, 3072 + 2 * GLA_LR, 3104 + FN_W, 3104 + FN_W + MQ_W
    w_main = jnp.concatenate([w_in0[:, :c_lr], w_in0[:, c_gt:]], axis=1).astype(BF16)
    b_main = row(jnp.concatenate([b_in0[:c_lr], b_in0[c_gt:]]))
    w_lr = _pad_cols(w_in0[:, c_lr:c_fn], LANES).astype(BF16)
    b_lr = _pad_cols(row(b_in0[c_lr:c_fn]), LANES)
    w_mq = w_in0[:, c_mq:c_gt].astype(BF16)
    b_mq = row(b_in0[c_mq:c_gt])
    w_fn = w_in0[:, c_fn:c_mq].astype(BF16)
    b_fn = row(b_in0[c_fn:c_mq])
    lng, lnb = row(ln_in_g), row(ln_in_b)

    proj, mq, lr = _inproj(x2, lng, lnb, w_main, b_main, w_mq, b_mq, w_lr, b_lr)

    zpad = jnp.zeros((LANES - 2 * GLA_LR, GLA_H * GLA_DK), F32)
    zlr = jnp.zeros((GLA_LR, GLA_H * GLA_DK), F32)
    wdf = jnp.concatenate([w_decay_f[0], zlr, zpad], axis=0).reshape(LANES, GLA_H, GLA_DK)
    wdb = jnp.concatenate([zlr, w_decay_b[0], zpad], axis=0).reshape(LANES, GLA_H, GLA_DK)
    wd = jnp.concatenate([wdf, wdb], axis=2).reshape(LANES, GLA_H * 2 * GLA_DK).astype(BF16)
    wd = jnp.concatenate([wd, wd], axis=0)
    bd = jnp.concatenate([b_decay_f[0].reshape(GLA_H, GLA_DK),
                          b_decay_b[0].reshape(GLA_H, GLA_DK)], axis=1).reshape(1, -1)
    og = _gla(proj, lr, wd, bd, row(gla_norm_g[0]))

    fbig, cwt, swt, f2, ccs, perm = _dft_tables(MERGE_TM)
    x4 = x.reshape(BATCH, FFT_N2, FFT_N1, D)
    z = _fft2(_fft1(x4, lng, lnb, w_fn, b_fn, fbig, cwt, swt), f2)

    kv = _memkv(mem.reshape(BATCH * MEM_LEN, D), row(ln_mem_g), row(ln_mem_b),
                w_mem_kv[0].astype(BF16))

    w_r = _pad_cols(w_router[0], LANES)
    wr_hi = w_r.astype(BF16)
    wr_lo = (w_r - wr_hi.astype(F32)).astype(BF16)
    b_r = jnp.concatenate([row(b_router[0]),
                           jnp.full((1, LANES - N_EXP), NEG_BIG, F32)], axis=1)
    h1, h1p, eidx, topw, cnt = _merge(
        x2, og, z, mq, proj, kv, lng, lnb,
        w_br_gla[0].astype(BF16), ccs, perm, w_br_fnet[0].astype(BF16),
        w_br_mem[0].astype(BF16), w_out[0].astype(BF16), row(b_out[0]),
        row(ln1_g[0]), row(ln1_b[0]), jnp.concatenate([wr_hi, wr_lo], axis=1), wr_hi, b_r)

    dest = _plan(eidx, cnt)
    counts = cnt[0, :N_EXP].astype(I32)
    dest_k = dest[:, :TOP_K]
    idx4 = dest_k.reshape(SC_WORKERS, SCD_NCH, SC_CH, TOP_K).transpose(0, 1, 3, 2)
    xs = _sc_dispatch(h1p, idx4.reshape(SC_WORKERS, SCD_NCH * TOP_K, SC_CH))
    ys = _experts(_work_items(counts), xs, w_gu[0], b_gu[0], w_down[0], b_down[0])
    out = h1
    for grp in range(COMB_GROUPS):
        dest_g = dest_k[grp * COMB_TG:(grp + 1) * COMB_TG].T.reshape(SC_WORKERS, SC_NCH, SC_CH)
        g4 = _sc_gather(ys, dest_g).reshape(TOP_K, COMB_TG, D // 2)
        out = _combine_dense(g4, out, topw, row(ln2_g[0]), row(ln2_b[0]), grp)
    return out.reshape(BATCH, SEQ, D)
```

```python
import functools
import math

import numpy as np
import jax
import jax.numpy as jnp
from jax import lax
from jax.experimental import pallas as pl
from jax.experimental.pallas import tpu as pltpu
from jax.experimental.pallas import tpu_sc as plsc

F32 = jnp.float32
BF16 = jnp.bfloat16
I32 = jnp.int32
U32 = jnp.uint32

D = 1024
BATCH = 4
SEQ = 4096
T = BATCH * SEQ
GLA_H = 4
GLA_DK = 128
GLA_DV = 256
GLA_LR = 16
GLA_TAU = 16.0
GLA_C = 64
FN_G = 4
FN_GW = 128
FN_W = 512
MEM_LEN = 256
MEM_H = 4
MEM_HD = 128
MQ_W = 512
N_EXP = 32
TOP_K = 4
D_FF = 1024
SW_LIMIT = 7.0
SW_ALPHA = 1.702
LN_EPS = 1e-5
RMS_EPS = 1e-6
DN_ALPHA = 2.0 ** 0.25
A_ROWS = T * TOP_K

FFT_N1 = 128
FFT_N2 = 32

LANES = 128
NEG_BIG = -1e30
MIB = 1024 * 1024


def _ln(x, g, b):
    mu = jnp.mean(x, axis=-1, keepdims=True)
    xc = x - mu
    var = jnp.mean(xc * xc, axis=-1, keepdims=True)
    return xc * lax.rsqrt(var + LN_EPS) * g + b


def _dot(a, b):
    return jnp.dot(a, b, preferred_element_type=F32)


def _dot_nt(a, b):
    return lax.dot_general(a, b, (((1,), (1,)), ((), ())), preferred_element_type=F32)


def _dot_tn(a, b):
    return lax.dot_general(a, b, (((0,), (0,)), ((), ())), preferred_element_type=F32)


def _interleave(*stages):
    live = list(stages)
    while live:
        for st in list(live):
            try:
                next(st)
            except StopIteration:
                live.remove(st)


def _split_bf16(a):
    hi = a.astype(BF16)
    return hi, (a - hi.astype(F32)).astype(BF16)


INPROJ_TM = 1024
INPROJ_TN = 3072


PROJ_W = 6 * 1024


def _inproj_kernel(x_ref, g_ref, b_ref, w_ref, bias_ref, wmq_ref, bmq_ref, wlr_ref, blr_ref,
                   proj_ref, mq_ref, lr_ref, hb_ref):
    @pl.when(pl.program_id(1) == 0)
    def _():
        hb = _ln(x_ref[...], g_ref[...], b_ref[...]).astype(BF16)
        hb_ref[...] = hb
        lr_ref[...] = _dot(hb, wlr_ref[...]) + blr_ref[...]
        mq_ref[...] = (_dot(hb, wmq_ref[...]) + bmq_ref[...]).astype(BF16)

    proj_ref[...] = (_dot(hb_ref[...], w_ref[...]) + bias_ref[...]).astype(BF16)


def _inproj(x2, ln_g, ln_b, w_main, b_main, w_mq, b_mq, w_lr, b_lr):
    tm, tn = INPROJ_TM, INPROJ_TN
    nj = PROJ_W // tn
    row = lambda i, j: (i, 0)
    const = lambda i, j: (0, 0)
    outs = (
        jax.ShapeDtypeStruct((T, PROJ_W), BF16),
        jax.ShapeDtypeStruct((T, MQ_W), BF16),
        jax.ShapeDtypeStruct((T, LANES), F32),
    )
    return pl.pallas_call(
        _inproj_kernel,
        grid=(T // tm, nj),
        in_specs=[
            pl.BlockSpec((tm, D), row),
            pl.BlockSpec((1, D), const),
            pl.BlockSpec((1, D), const),
            pl.BlockSpec((D, tn), lambda i, j: (0, j)),
            pl.BlockSpec((1, tn), lambda i, j: (0, j)),
            pl.BlockSpec((D, MQ_W), const),
            pl.BlockSpec((1, MQ_W), const),
            pl.BlockSpec((D, LANES), const),
            pl.BlockSpec((1, LANES), const),
        ],
        out_specs=[
            pl.BlockSpec((tm, tn), lambda i, j: (i, j)),
            pl.BlockSpec((tm, MQ_W), row),
            pl.BlockSpec((tm, LANES), row),
        ],
        out_shape=outs,
        scratch_shapes=[pltpu.VMEM((tm, D), BF16)],
        compiler_params=pltpu.CompilerParams(
            dimension_semantics=("arbitrary", "arbitrary"),
            vmem_limit_bytes=48 * MIB),
        name="ln_inproj",
    )(x2, ln_g, ln_b, w_main, b_main, w_mq, b_mq, w_lr, b_lr)


GLA_BULK = 256
GLA_FIN = 512
GLA_NCH = SEQ // GLA_C
GLA_CPB = GLA_BULK // GLA_C
GLA_PIPE = 4


def _gla_kernel(q_ref, k_ref, v_ref, r_ref, lr_ref, wd_ref, bd_ref, g_ref, cs_ref, o_ref,
                acc_ref, qin_ref, kin_ref, kst_ref, dec_ref, u_ref, stf_ref, stb_ref):
    C = GLA_C
    G = GLA_BULK
    DK = GLA_DK
    NG = SEQ // G
    scale = DK ** -0.5
    ii = lax.broadcasted_iota(I32, (G, G), 0)
    jj = lax.broadcasted_iota(I32, (G, G), 1)
    same = (ii // C) == (jj // C)
    lower = jnp.logical_and(same, ii >= jj)
    upper = jnp.logical_and(same, ii <= jj)
    is_fwd = lax.broadcasted_iota(I32, (G, 2 * DK), 1) < DK
    chunk_of_row = lax.broadcasted_iota(I32, (G, DK), 0) // C

    def stage_a(gi):
        rows = pl.ds(pl.multiple_of(gi * G, G), G)
        z = _dot(jnp.concatenate(_split_bf16(lr_ref[rows, :]), axis=1), wd_ref[...]) + bd_ref[...]
        yield
        la = -(jnp.maximum(-z, 0.0) + jnp.log(1.0 + jnp.exp(-jnp.abs(z)))) * (1.0 / GLA_TAU)
        la_hi, la_lo = _split_bf16(la)
        pre2 = _dot(cs_ref[...], jnp.concatenate([la_hi, la_lo], axis=1))
        yield
        pre = pre2[:, :2 * DK] + pre2[:, 2 * DK:]
        blast = jnp.concatenate(
            [jnp.broadcast_to(pre[ci * C + C - 1:ci * C + C, :], (C, 2 * DK))
             for ci in range(GLA_CPB)], axis=0)
        b = jnp.where(is_fwd, pre, blast - pre + la)
        qf32 = q_ref[rows, :].astype(F32)
        kf32 = k_ref[rows, :].astype(F32)
        q2 = jnp.concatenate([qf32, qf32], axis=1)
        k2 = jnp.concatenate([kf32, kf32], axis=1)
        qin_ref[rows, :] = (q2 * (scale * jnp.exp(b))).astype(BF16)
        kin_ref[rows, :] = (k2 * jnp.exp(-b)).astype(BF16)
        kst_ref[rows, :] = (k2 * jnp.exp(blast - b)).astype(BF16)
        dec = jnp.exp(blast)
        for ci in range(GLA_CPB):
            dec_ref[pl.ds(gi * GLA_CPB + ci, 1), :] = dec[ci * C:ci * C + 1, :]

    def stage_b(gi):
        rows = pl.ds(pl.multiple_of(gi * G, G), G)
        qi = qin_ref[rows, :]
        ki = kin_ref[rows, :]
        ks = kst_ref[rows, :]
        vb = v_ref[rows, :]
        att_f = _dot_nt(qi[:, :DK], ki[:, :DK])
        att_b = _dot_nt(qi[:, DK:], ki[:, DK:])
        yield
        att = jnp.where(lower, att_f, 0.0) + jnp.where(upper, att_b, 0.0)
        acc_ref[rows, :] = _dot(att.astype(BF16), vb)
        yield
        ksb = jnp.concatenate(
            [jnp.where(chunk_of_row == ci, ks[:, d * DK:(d + 1) * DK], jnp.zeros((G, DK), BF16))
             for d in range(2) for ci in range(GLA_CPB)], axis=1)
        u = _dot_tn(vb, ksb)
        for d in range(2):
            for ci in range(GLA_CPB):
                col = (d * GLA_CPB + ci) * DK
                u_ref[d, gi * GLA_CPB + ci] = u[:, col:col + DK]

    P = GLA_PIPE
    _interleave(*[stage_a(j) for j in range(P)])

    def bulk(i, carry):
        g = P * i
        _interleave(*[st for j in range(P) for st in (stage_b(g - P + j), stage_a(g + j))])
        return carry

    lax.fori_loop(1, NG // P, bulk, 0)
    _interleave(*[stage_b(NG - P + j) for j in range(P)])

    stf_ref[...] = jnp.zeros_like(stf_ref)
    stb_ref[...] = jnp.zeros_like(stb_ref)

    def one(n, d, st_ref):
        lanes = slice(d * DK, (d + 1) * DK)
        rows = pl.ds(pl.multiple_of(n * C, C), C)
        st = st_ref[...]
        acc_ref[rows, :] += _dot_nt(qin_ref[rows, lanes], st.astype(BF16))
        st_ref[...] = st * dec_ref[pl.ds(n, 1), :][:, lanes] + u_ref[d, n]

    def step(i, carry):
        one(i, 0, stf_ref)
        one(GLA_NCH - 1 - i, 1, stb_ref)
        return carry

    lax.fori_loop(0, GLA_NCH, step, 0, unroll=4)

    def fin(gi, carry):
        rows = pl.ds(pl.multiple_of(gi * GLA_FIN, GLA_FIN), GLA_FIN)
        o = acc_ref[rows, :]
        o = o * lax.rsqrt(jnp.mean(o * o, axis=-1, keepdims=True) + RMS_EPS) * g_ref[...]
        rg = r_ref[rows, :].astype(F32)
        o_ref[rows, :] = (o * (rg * jax.nn.sigmoid(rg))).astype(BF16)
        return carry

    lax.fori_loop(0, SEQ // GLA_FIN, fin, 0)


def _gla(proj, lr, wd, bd, g):
    i = np.arange(GLA_BULK)
    cs = ((i[:, None] // GLA_C) == (i[None, :] // GLA_C)) & (i[:, None] >= i[None, :])
    cs = jnp.asarray(cs, dtype=F32).astype(BF16)
    v_blk = 1024 // GLA_DV
    return pl.pallas_call(
        _gla_kernel,
        grid=(BATCH, GLA_H),
        in_specs=[
            pl.BlockSpec((SEQ, GLA_DK), lambda b, h: (b, h)),
            pl.BlockSpec((SEQ, GLA_DK), lambda b, h: (b, GLA_H + h)),
            pl.BlockSpec((SEQ, GLA_DV), lambda b, h: (b, v_blk + h)),
            pl.BlockSpec((SEQ, GLA_DV), lambda b, h: (b, 2 * v_blk + h)),
            pl.BlockSpec((SEQ, LANES), lambda b, h: (b, 0)),
            pl.BlockSpec((2 * LANES, 2 * GLA_DK), lambda b, h: (0, h)),
            pl.BlockSpec((1, 2 * GLA_DK), lambda b, h: (0, h)),
            pl.BlockSpec((1, GLA_DV), lambda b, h: (0, 0)),
            pl.BlockSpec((GLA_BULK, GLA_BULK), lambda b, h: (0, 0)),
        ],
        out_specs=pl.BlockSpec((SEQ, GLA_DV), lambda b, h: (b, h)),
        out_shape=jax.ShapeDtypeStruct((T, GLA_H * GLA_DV), BF16),
        scratch_shapes=[
            pltpu.VMEM((SEQ, GLA_DV), F32),
            pltpu.VMEM((SEQ, 2 * GLA_DK), BF16),
            pltpu.VMEM((SEQ, 2 * GLA_DK), BF16),
            pltpu.VMEM((SEQ, 2 * GLA_DK), BF16),
            pltpu.VMEM((GLA_NCH, 2 * GLA_DK), F32),
            pltpu.VMEM((2, GLA_NCH, GLA_DV, GLA_DK), F32),
            pltpu.VMEM((GLA_DV, GLA_DK), F32),
            pltpu.VMEM((GLA_DV, GLA_DK), F32),
        ],
        compiler_params=pltpu.CompilerParams(
            dimension_semantics=("arbitrary", "arbitrary"),
            vmem_limit_bytes=58 * MIB),
        name="gla",
    )(proj, proj, proj, proj, lr, wd, bd, g, cs)


FFT1_S = 16
FFT1_ROWS = FFT_N2 * FFT1_S
FFT2_KB = 8


def _fft1_kernel(x_ref, g_ref, b_ref, w_ref, bias_ref, fbig_ref, cw_ref, sw_ref, o_ref):
    xv = x_ref[...].reshape(FFT1_ROWS, D)
    hb = _ln(xv, g_ref[...], b_ref[...]).astype(BF16)
    fn = (_dot(hb, w_ref[...]) + bias_ref[...]).astype(BF16)
    a = _dot(fbig_ref[...], fn)
    ar = a[:FFT1_ROWS]
    ai = a[FFT1_ROWS:]
    cw = jnp.concatenate([cw_ref[...]] * (FN_W // LANES), axis=1)
    sw = jnp.concatenate([sw_ref[...]] * (FN_W // LANES), axis=1)
    o_ref[0] = (ar * cw + ai * sw).reshape(FFT_N2, FFT1_S, FN_W).astype(BF16)
    o_ref[1] = (ai * cw - ar * sw).reshape(FFT_N2, FFT1_S, FN_W).astype(BF16)


def _fft1(x4, ln_g, ln_b, w_fn, b_fn, fbig, cwt, swt):
    s = FFT1_S
    const = lambda b, j: (0, 0)
    return pl.pallas_call(
        _fft1_kernel,
        grid=(BATCH, FFT_N1 // s),
        in_specs=[
            pl.BlockSpec((None, FFT_N2, s, D), lambda b, j: (b, 0, j, 0)),
            pl.BlockSpec((1, D), const),
            pl.BlockSpec((1, D), const),
            pl.BlockSpec((D, FN_W), const),
            pl.BlockSpec((1, FN_W), const),
            pl.BlockSpec((2 * FFT1_ROWS, FFT1_ROWS), const),
            pl.BlockSpec((None, FFT1_ROWS, LANES), lambda b, j: (j, 0, 0)),
            pl.BlockSpec((None, FFT1_ROWS, LANES), lambda b, j: (j, 0, 0)),
        ],
        out_specs=pl.BlockSpec((None, 2, FFT_N2, s, FN_W), lambda b, j: (b, 0, 0, j, 0)),
        out_shape=jax.ShapeDtypeStruct((BATCH, 2, FFT_N2, FFT_N1, FN_W), BF16),
        compiler_params=pltpu.CompilerParams(
            dimension_semantics=("arbitrary", "arbitrary"),
            vmem_limit_bytes=40 * MIB),
        name="fft_stage1",
    )(x4, ln_g, ln_b, w_fn, b_fn, fbig, cwt, swt)


def _fft2_kernel(d_ref, f2_ref, o_ref):
    f2 = f2_ref[...]
    for kk in range(FFT2_KB):
        z = _dot(f2, jnp.concatenate([d_ref[0, kk], d_ref[1, kk]], axis=0))
        o_ref[0, kk] = z[:FFT_N1].astype(BF16)
        o_ref[1, kk] = z[FFT_N1:].astype(BF16)


def _fft2(dmat, f2):
    kb = FFT2_KB
    blk = (None, 2, kb, FFT_N1, FN_W)
    return pl.pallas_call(
        _fft2_kernel,
        grid=(BATCH, FFT_N2 // kb),
        in_specs=[
            pl.BlockSpec(blk, lambda b, j: (b, 0, j, 0, 0)),
            pl.BlockSpec((2 * FFT_N1, 2 * FFT_N1), lambda b, j: (0, 0)),
        ],
        out_specs=pl.BlockSpec(blk, lambda b, j: (b, 0, j, 0, 0)),
        out_shape=jax.ShapeDtypeStruct((BATCH, 2, FFT_N2, FFT_N1, FN_W), BF16),
        compiler_params=pltpu.CompilerParams(
            dimension_semantics=("arbitrary", "arbitrary")),
        name="fft_stage2",
    )(dmat, f2)


def _dft_tables(merge_tm):
    s = FFT1_S
    n2 = np.arange(FFT_N2, dtype=np.float64)
    n1 = np.arange(FFT_N1, dtype=np.float64)
    th = 2.0 * np.pi * np.outer(n2, n2) / FFT_N2
    f1 = np.stack([np.cos(th), -np.sin(th)]) / math.sqrt(SEQ)
    fbig = np.einsum("rkn,st->rksnt", f1, np.eye(s)).reshape(2 * FFT_N2 * s, FFT_N2 * s)
    tw = 2.0 * np.pi * np.outer(n2, n1) / SEQ
    tw = tw.reshape(FFT_N2, FFT_N1 // s, s).transpose(1, 0, 2).reshape(FFT_N1 // s, FFT_N2 * s)
    cwt = np.broadcast_to(np.cos(tw)[:, :, None], tw.shape + (LANES,))
    swt = np.broadcast_to(np.sin(tw)[:, :, None], tw.shape + (LANES,))
    th1 = 2.0 * np.pi * np.outer(n1, n1) / FFT_N1
    c1, s1 = np.cos(th1), np.sin(th1)
    f2 = np.block([[c1, s1], [-s1, c1]])
    cc = np.arange(FN_GW, dtype=np.float64)
    thc = 2.0 * np.pi * np.outer(cc, cc) / FN_GW
    ccs = np.concatenate([np.cos(thc), np.sin(thc)], axis=0) / math.sqrt(FN_GW)
    k1n = merge_tm // FFT_N2
    r = np.arange(merge_tm)
    perm = np.zeros((merge_tm, merge_tm))
    perm[r, (r % FFT_N2) * k1n + r // FFT_N2] = 1.0
    as32 = lambda a: jnp.asarray(np.ascontiguousarray(a), dtype=F32)
    return (as32(fbig).astype(BF16), as32(cwt), as32(swt), as32(f2).astype(BF16),
            as32(ccs).astype(BF16), as32(perm).astype(BF16))


def _memkv_kernel(m_ref, g_ref, b_ref, w_ref, o_ref):
    mn = _ln(m_ref[...], g_ref[...], b_ref[...]).astype(BF16)
    o_ref[...] = _dot(mn, w_ref[...]).astype(BF16)


def _memkv(mem2, g, b, w):
    return pl.pallas_call(
        _memkv_kernel,
        grid=(BATCH,),
        in_specs=[
            pl.BlockSpec((MEM_LEN, D), lambda i: (i, 0)),
            pl.BlockSpec((1, D), lambda i: (0, 0)),
            pl.BlockSpec((1, D), lambda i: (0, 0)),
            pl.BlockSpec((D, 2 * MQ_W), lambda i: (0, 0)),
        ],
        out_specs=pl.BlockSpec((MEM_LEN, 2 * MQ_W), lambda i: (i, 0)),
        out_shape=jax.ShapeDtypeStruct((BATCH * MEM_LEN, 2 * MQ_W), BF16),
        compiler_params=pltpu.CompilerParams(dimension_semantics=("arbitrary",)),
        name="mem_kv",
    )(mem2, g, b, w)


MERGE_TM = 512
MERGE_K1 = MERGE_TM // FFT_N2


def _pack_bf16_pair(v):
    n = v.shape[1] // 2
    bits = lax.bitcast_convert_type(v.astype(BF16).astype(F32), U32)
    return (bits[:, n:] & jnp.uint32(0xFFFF0000)) | (bits[:, :n] >> 16)


def _unpack_bf16_pair(p):
    lo = lax.bitcast_convert_type(p << 16, F32)
    hi = lax.bitcast_convert_type(p & jnp.uint32(0xFFFF0000), F32)
    return lo, hi


def _merge_kernel(x_ref, og_ref, zr_ref, zi_ref, mq_ref, gt_ref, kv_ref,
                  lng_ref, lnb_ref, wg_ref, ccs_ref, perm_ref, wf_ref, wm_ref, wo_ref, bo_ref,
                  l1g_ref, l1b_ref, wr2_ref, wrh_ref, br_ref,
                  h1_ref, h1p_ref, eidx_ref, topw_ref, cnt_ref):
    tm = MERGE_TM
    y = {}

    def branch_fnet():
        zr = zr_ref[...].reshape(tm, FN_W)
        zi = zi_ref[...].reshape(tm, FN_W)
        ys = []
        for g in range(FN_G):
            sl = slice(g * FN_GW, (g + 1) * FN_GW)
            ys.append(_dot(jnp.concatenate([zr[:, sl], zi[:, sl]], axis=1), ccs_ref[...]))
        yield
        yp = _dot(perm_ref[...], jnp.concatenate(ys, axis=1).astype(BF16))
        yield
        y["fnet"] = _dot(yp.astype(BF16), wf_ref[...])

    def branch_mem():
        heads = [slice(hd * MEM_HD, (hd + 1) * MEM_HD) for hd in range(MEM_H)]
        ss = [_dot_nt(mq_ref[:, sl], kv_ref[:, sl]) for sl in heads]
        yield
        oms = []
        for hd, s in enumerate(ss):
            s = s * (MEM_HD ** -0.5)
            s = s - jnp.max(s, axis=-1, keepdims=True)
            p = jnp.exp(s)
            p = p * (1.0 / jnp.sum(p, axis=-1, keepdims=True))
            oms.append(_dot(p.astype(BF16),
                            kv_ref[:, MQ_W + hd * MEM_HD:MQ_W + (hd + 1) * MEM_HD]))
        yield
        y["mem"] = _dot(jnp.concatenate(oms, axis=1).astype(BF16), wm_ref[...])

    def branch_gla():
        y["gla"] = _dot(og_ref[...], wg_ref[...])
        yield

    _interleave(branch_fnet(), branch_mem(), branch_gla())

    def gate(c):
        return 0.5 + 0.5 * jnp.tanh(0.5 * gt_ref[:, c * D:(c + 1) * D].astype(F32))

    merged = gate(0) * y["gla"] + gate(1) * y["fnet"] + gate(2) * y["mem"]
    mix = _dot(merged.astype(BF16), wo_ref[...]) + bo_ref[...]
    h = _ln(x_ref[...], lng_ref[...], lnb_ref[...])
    h1 = _ln(DN_ALPHA * h + mix, l1g_ref[...], l1b_ref[...])
    h1_ref[...] = h1
    h1p_ref[...] = _pack_bf16_pair(h1)

    h_hi, h_lo = _split_bf16(h1)
    d2 = _dot(h_hi, wr2_ref[...])
    l = d2[:, :LANES] + d2[:, LANES:] + _dot(h_lo, wrh_ref[...]) + br_ref[...]
    lane = lax.broadcasted_iota(I32, (tm, LANES), 1)
    vals, idxs = [], []
    for _ in range(TOP_K):
        m = jnp.max(l, axis=-1, keepdims=True)
        idx = jnp.min(jnp.where(l == m, lane, LANES), axis=-1, keepdims=True)
        vals.append(m)
        idxs.append(idx)
        l = jnp.where(lane == idx, -jnp.inf, l)
    es = [jnp.exp(v - vals[0]) for v in vals]
    den = es[0] + es[1] + es[2] + es[3]
    eo = jnp.zeros((tm, LANES), I32)
    wo = jnp.zeros((tm, LANES), F32)
    chosen = jnp.zeros((tm, LANES), F32)
    for k in range(TOP_K):
        eo = jnp.where(lane == k, idxs[k], eo)
        wo = jnp.where(lane == k, es[k] / den, wo)
        chosen = chosen + jnp.where(lane == idxs[k], 1.0, 0.0)
    eidx_ref[...] = eo
    topw_ref[...] = wo

    @pl.when(pl.program_id(0) == 0)
    def _():
        cnt_ref[...] = jnp.zeros_like(cnt_ref)

    cnt_ref[...] += jnp.broadcast_to(jnp.sum(chosen, axis=0, keepdims=True), cnt_ref.shape)


def _merge(x2, og, z, mq, gates, kv, lng, lnb, wg, ccs, perm, wf, wm, wo, bo, l1g, l1b,
           wr2, wrh, br):
    tm = MERGE_TM
    per_b = SEQ // tm
    row = lambda i: (i, 0)
    const = lambda i: (0, 0)
    zblk = (None, None, FFT_N2, MERGE_K1, FN_W)
    outs = (
        jax.ShapeDtypeStruct((T, D), F32),
        jax.ShapeDtypeStruct((T, D // 2), U32),
        jax.ShapeDtypeStruct((T, LANES), I32),
        jax.ShapeDtypeStruct((T, LANES), F32),
        jax.ShapeDtypeStruct((8, LANES), F32),
    )
    return pl.pallas_call(
        _merge_kernel,
        grid=(T // tm,),
        in_specs=[
            pl.BlockSpec((tm, D), row),
            pl.BlockSpec((tm, D), row),
            pl.BlockSpec(zblk, lambda i: (i // per_b, 0, 0, i % per_b, 0)),
            pl.BlockSpec(zblk, lambda i: (i // per_b, 1, 0, i % per_b, 0)),
            pl.BlockSpec((tm, MQ_W), row),
            pl.BlockSpec((tm, 3 * D), lambda i: (i, 1)),
            pl.BlockSpec((MEM_LEN, 2 * MQ_W), lambda i: (i // per_b, 0)),
            pl.BlockSpec((1, D), const), pl.BlockSpec((1, D), const),
            pl.BlockSpec((D, D), const),
            pl.BlockSpec((2 * FN_GW, FN_GW), const),
            pl.BlockSpec((tm, tm), const),
            pl.BlockSpec((FN_W, D), const),
            pl.BlockSpec((MQ_W, D), const),
            pl.BlockSpec((D, D), const),
            pl.BlockSpec((1, D), const),
            pl.BlockSpec((1, D), const), pl.BlockSpec((1, D), const),
            pl.BlockSpec((D, 2 * LANES), const),
            pl.BlockSpec((D, LANES), const),
            pl.BlockSpec((1, LANES), const),
        ],
        out_specs=[
            pl.BlockSpec((tm, D), row),
            pl.BlockSpec((tm, D // 2), row),
            pl.BlockSpec((tm, LANES), row),
            pl.BlockSpec((tm, LANES), row),
            pl.BlockSpec((8, LANES), const),
        ],
        out_shape=outs,
        compiler_params=pltpu.CompilerParams(
            dimension_semantics=("arbitrary",),
            vmem_limit_bytes=58 * MIB),
        name="merge_ln1_router",
    )(x2, og, z, z, mq, gates, kv, lng, lnb, wg, ccs, perm, wf, wm, wo, bo, l1g, l1b,
      wr2, wrh, br)


PLAN_TP = 1024


def _expert_onehots(e, lane):
    onehots = [lane == e[:, k:k + 1] for k in range(TOP_K)]
    mf = jnp.zeros(lane.shape, F32)
    for oh in onehots:
        mf = mf + jnp.where(oh, 1.0, 0.0)
    return onehots, mf


def _plan_kernel(e_ref, tot_ref, dest_ref, cnt_ref, off_ref):
    i = pl.program_id(0)
    tp = PLAN_TP
    lane = lax.broadcasted_iota(I32, (tp, LANES), 1)
    onehots, mf = _expert_onehots(e_ref[...], lane)

    @pl.when(i == 0)
    def _():
        tot = tot_ref[0:1, :]
        padded = jnp.floor((tot + (MOE_BM - 1)) * (1.0 / MOE_BM)) * MOE_BM
        lane1 = lax.broadcasted_iota(I32, (1, LANES), 1)
        inc = padded
        for s in (1, 2, 4, 8, 16, 32, 64):
            inc = inc + jnp.where(lane1 >= s, pltpu.roll(inc, s, 1), 0.0)
        off_ref[...] = inc - padded
        cnt_ref[...] = jnp.zeros_like(cnt_ref)

    ri = lax.broadcasted_iota(I32, (tp, tp), 0)
    ci = lax.broadcasted_iota(I32, (tp, tp), 1)
    ltri = jnp.where(ri > ci, 1.0, 0.0).astype(BF16)
    rank = _dot(ltri, mf.astype(BF16)) + cnt_ref[...] + off_ref[...]
    out = jnp.zeros((tp, LANES), I32)
    for k in range(TOP_K):
        dk = jnp.sum(jnp.where(onehots[k], rank, 0.0), axis=-1, keepdims=True)
        out = jnp.where(lane == k, dk.astype(I32), out)
    dest_ref[...] = out
    cnt_ref[...] += jnp.sum(mf, axis=0, keepdims=True)


def _plan(eidx, cnt):
    tp = PLAN_TP
    return pl.pallas_call(
        _plan_kernel,
        grid=(T // tp,),
        in_specs=[pl.BlockSpec((tp, LANES), lambda i: (i, 0)),
                  pl.BlockSpec((8, LANES), lambda i: (0, 0))],
        out_specs=pl.BlockSpec((tp, LANES), lambda i: (i, 0)),
        out_shape=jax.ShapeDtypeStruct((T, LANES), I32),
        scratch_shapes=[pltpu.VMEM((1, LANES), F32), pltpu.VMEM((1, LANES), F32)],
        compiler_params=pltpu.CompilerParams(dimension_semantics=("arbitrary",)),
        name="route_plan",
    )(eidx, cnt)


MOE_BM = 512
MOE_NW = A_ROWS // MOE_BM + N_EXP
XS_ROWS = MOE_NW * MOE_BM
MOE_FF_SLICES = 4


def _expert_kernel(we_ref, wb_ref, wv_ref, wfe_ref, wsl_ref, wnx_ref,
                   x_ref, wgu_hbm, bgu_ref, wdn_hbm, bdn_ref, o_ref,
                   wgu_f32, wdn_f32, wgu_bf, wdn_bf, sems):
    w = pl.program_id(0)
    e = we_ref[w]

    def weight_copies(expert, slot):
        return (pltpu.make_async_copy(wgu_hbm.at[expert], wgu_f32.at[slot], sems.at[slot, 0]),
                pltpu.make_async_copy(wdn_hbm.at[expert], wdn_f32.at[slot], sems.at[slot, 1]))

    @pl.when(w == 0)
    def _():
        for cp_ in weight_copies(e, 0):
            cp_.start()

    @pl.when(wfe_ref[w] == 1)
    def _():
        slot = wsl_ref[w]
        for cp_ in weight_copies(e, slot):
            cp_.wait()
        wgu_bf[...] = wgu_f32[slot].astype(BF16)
        wdn_bf[...] = wdn_f32[slot].astype(BF16)
        nxt = wnx_ref[w]

        @pl.when(nxt >= 0)
        def _():
            for cp_ in weight_copies(nxt, 1 - slot):
                cp_.start()

    def ffn(rows):
        xlo, xhi = _unpack_bf16_pair(x_ref[rows, :])
        xb = jnp.concatenate([xlo.astype(BF16), xhi.astype(BF16)], axis=1)
        bgu = bgu_ref[pl.ds(e, 1), :]
        hw = D_FF // MOE_FF_SLICES
        gus = []
        for hf in range(MOE_FF_SLICES):
            gc = slice(hf * hw, (hf + 1) * hw)
            uc = slice(D_FF + hf * hw, D_FF + (hf + 1) * hw)
            gus.append((_dot(xb, wgu_bf[:, gc]) + bgu[:, gc], _dot(xb, wgu_bf[:, uc]) + bgu[:, uc]))
        out = bdn_ref[pl.ds(e, 1), :]
        for hf, (g, u) in enumerate(gus):
            gate = jnp.minimum(g, SW_LIMIT)
            up = jnp.clip(u, -SW_LIMIT, SW_LIMIT)
            act = (up + 1.0) * (gate * jax.nn.sigmoid(SW_ALPHA * gate))
            out = out + _dot(act.astype(BF16), wdn_bf[hf * hw:(hf + 1) * hw, :])
        o_ref[rows, :] = _pack_bf16_pair(out)

    @pl.when(wv_ref[w] == 1)
    def _():
        ffn(slice(None))

    @pl.when(wv_ref[w] == 2)
    def _():
        ffn(slice(0, MOE_BM // 2))


def _experts(meta, xs, w_gu, b_gu, w_down, b_down):
    return pl.pallas_call(
        _expert_kernel,
        grid_spec=pltpu.PrefetchScalarGridSpec(
            num_scalar_prefetch=len(meta),
            grid=(MOE_NW,),
            in_specs=[
                pl.BlockSpec((MOE_BM, D // 2), lambda w, we, wb, *_: (wb[w], 0)),
                pl.BlockSpec(memory_space=pl.ANY),
                pl.BlockSpec((N_EXP, 2 * D_FF), lambda w, *_: (0, 0)),
                pl.BlockSpec(memory_space=pl.ANY),
                pl.BlockSpec((N_EXP, D), lambda w, *_: (0, 0)),
            ],
            out_specs=pl.BlockSpec((MOE_BM, D // 2), lambda w, we, wb, *_: (wb[w], 0)),
            scratch_shapes=[
                pltpu.VMEM((2, D, 2 * D_FF), F32),
                pltpu.VMEM((2, D_FF, D), F32),
                pltpu.VMEM((D, 2 * D_FF), BF16),
                pltpu.VMEM((D_FF, D), BF16),
                pltpu.SemaphoreType.DMA((2, 2)),
            ],
        ),
        out_shape=jax.ShapeDtypeStruct((XS_ROWS, D // 2), U32),
        compiler_params=pltpu.CompilerParams(
            dimension_semantics=("arbitrary",),
            vmem_limit_bytes=56 * MIB),
        name="moe_experts",
    )(*meta, xs, w_gu, b_gu, w_down, b_down)


def _work_items(counts):
    n_e = (counts + MOE_BM - 1) // MOE_BM
    item_end = jnp.cumsum(n_e)
    total = item_end[-1]
    w = jnp.arange(MOE_NW, dtype=I32)
    valid = w < total
    wc = jnp.minimum(w, total - 1)
    e_w = jnp.sum((item_end[None, :] <= wc[:, None]).astype(I32), axis=1)
    e_w = jnp.minimum(e_w, N_EXP - 1)
    rows_here = counts[e_w] - (wc - (item_end - n_e)[e_w]) * MOE_BM
    valid = jnp.where(valid, jnp.where(rows_here <= MOE_BM // 2, 2, 1), 0)
    prev_e = jnp.concatenate([jnp.full((1,), -1, I32), e_w[:-1]])
    fe = (e_w != prev_e).astype(I32)
    slot = (jnp.cumsum(fe) - 1) % 2
    first_at = jnp.where(fe == 1, w, MOE_NW)
    next_first = jnp.concatenate([lax.cummin(first_at, reverse=True)[1:],
                                  jnp.full((1,), MOE_NW, I32)])
    nxt = jnp.where(next_first < MOE_NW, e_w[jnp.minimum(next_first, MOE_NW - 1)], -1)
    return tuple(a.astype(I32) for a in (e_w, wc, valid, fe, slot, nxt))


COMB_TM = 512
SC_CORES = 2
SC_SUBCORES = 16
SC_WORKERS = SC_CORES * SC_SUBCORES
SC_CH = 64
COMB_GROUPS = 4
COMB_TG = T // COMB_GROUPS
SC_ROWS_PER_W = COMB_TG * TOP_K // SC_WORKERS
SC_NCH = SC_ROWS_PER_W // SC_CH


def _sc_gather(table, idx3):
    mesh = plsc.VectorSubcoreMesh(core_axis_name="c", subcore_axis_name="s")

    @functools.partial(
        pl.kernel, mesh=mesh,
        out_type=jax.ShapeDtypeStruct((COMB_TG * TOP_K, D // 2), U32),
        scratch_types=[
            pltpu.VMEM((SC_NCH, SC_CH), I32),
            pltpu.VMEM((2, SC_CH, D // 2), U32),
            pltpu.SemaphoreType.DMA((2,)),
            pltpu.SemaphoreType.DMA((2,)),
        ],
    )
    def k(table_hbm, idx_hbm, out_hbm, idx_v, rows_v, gsem, psem):
        wid = lax.axis_index("s") * SC_CORES + lax.axis_index("c")
        base = wid * SC_ROWS_PER_W
        pltpu.sync_copy(idx_hbm.at[wid], idx_v)

        def gather(j, b):
            return pltpu.make_async_copy(table_hbm.at[idx_v.at[j]], rows_v.at[b], gsem.at[b])

        def put(j, b):
            return pltpu.make_async_copy(rows_v.at[b], out_hbm.at[pl.ds(base + j * SC_CH, SC_CH)],
                                         psem.at[b])

        gather(0, 0).start()

        @pl.loop(0, SC_NCH, step=2)
        def _(j0):
            for b in range(2):
                j = j0 + b

                @pl.when(j + 1 < SC_NCH)
                def _():
                    @pl.when(j >= 1)
                    def _():
                        put(j - 1, 1 - b).wait()
                    gather(j + 1, 1 - b).start()

                gather(j, b).wait()
                put(j, b).start()

        put(SC_NCH - 2, 0).wait()
        put(SC_NCH - 1, 1).wait()

    return k(table, idx3)


SCD_TOK_PER_W = T // SC_WORKERS
SCD_NCH = SCD_TOK_PER_W // SC_CH


def _sc_dispatch(h1p, idx4):
    mesh = plsc.VectorSubcoreMesh(core_axis_name="c", subcore_axis_name="s")

    @functools.partial(
        pl.kernel, mesh=mesh,
        out_type=jax.ShapeDtypeStruct((XS_ROWS, D // 2), U32),
        scratch_types=[
            pltpu.VMEM((SCD_NCH * TOP_K, SC_CH), I32),
            pltpu.VMEM((2, SC_CH, D // 2), U32),
            pltpu.SemaphoreType.DMA((2,)),
            pltpu.SemaphoreType.DMA((2,)),
        ],
    )
    def k(h_hbm, idx_hbm, xs_hbm, idx_v, rows_v, gsem, psem):
        wid = lax.axis_index("s") * SC_CORES + lax.axis_index("c")
        base = wid * SCD_TOK_PER_W
        pltpu.sync_copy(idx_hbm.at[wid], idx_v)

        def get(c, b):
            return pltpu.make_async_copy(h_hbm.at[pl.ds(base + c * SC_CH, SC_CH)], rows_v.at[b],
                                         gsem.at[b])

        def puts(c, b):
            return [pltpu.make_async_copy(rows_v.at[b], xs_hbm.at[idx_v.at[c * TOP_K + kk]],
                                          psem.at[b]) for kk in range(TOP_K)]

        get(0, 0).start()

        @pl.loop(0, SCD_NCH, step=2)
        def _(c0):
            for b in range(2):
                c = c0 + b

                @pl.when(c + 1 < SCD_NCH)
                def _():
                    @pl.when(c >= 1)
                    def _():
                        for cp_ in puts(c - 1, 1 - b):
                            cp_.wait()
                    get(c + 1, 1 - b).start()

                get(c, b).wait()
                for cp_ in puts(c, b):
                    cp_.start()

        for cp_ in puts(SCD_NCH - 2, 0) + puts(SCD_NCH - 1, 1):
            cp_.wait()

    return k(h1p, idx4)


def _combine_dense_kernel(g_ref, h1_ref, tw_ref, lg_ref, lb_ref, o_ref):
    tw = tw_ref[...]
    ylo = jnp.zeros((COMB_TM, D // 2), F32)
    yhi = jnp.zeros((COMB_TM, D // 2), F32)
    for k in range(TOP_K):
        lo, hi = _unpack_bf16_pair(g_ref[k])
        wk = tw[:, k:k + 1]
        ylo = ylo + lo * wk
        yhi = yhi + hi * wk
    ff = jnp.concatenate([ylo, yhi], axis=1)
    o_ref[...] = _ln(DN_ALPHA * h1_ref[...] + ff, lg_ref[...], lb_ref[...])


def _combine_dense(g4, h1, topw, g, b, group):
    tm = COMB_TM
    t0 = group * (COMB_TG // tm)
    return pl.pallas_call(
        _combine_dense_kernel,
        grid=(COMB_TG // tm,),
        in_specs=[
            pl.BlockSpec((TOP_K, tm, D // 2), lambda i: (0, i, 0)),
            pl.BlockSpec((tm, D), lambda i: (t0 + i, 0)),
            pl.BlockSpec((tm, LANES), lambda i: (t0 + i, 0)),
            pl.BlockSpec((1, D), lambda i: (0, 0)),
            pl.BlockSpec((1, D), lambda i: (0, 0)),
        ],
        out_specs=pl.BlockSpec((tm, D), lambda i: (t0 + i, 0)),
        out_shape=jax.ShapeDtypeStruct((T, D), F32),
        input_output_aliases={1: 0},
        compiler_params=pltpu.CompilerParams(
            dimension_semantics=("arbitrary",),
            vmem_limit_bytes=40 * MIB),
        name="moe_combine_dense_ln2",
    )(g4, h1, topw, g, b)


def _pad_cols(a, n):
    return jnp.pad(a, ((0, 0), (0, n - a.shape[1])))


def kernel(x, mem, ln_in_g, ln_in_b, ln_mem_g, ln_mem_b, w_in, b_in, w_decay_f, b_decay_f,
           w_decay_b, b_decay_b, gla_norm_g, w_br_gla, w_br_fnet, w_br_mem, w_mem_kv, w_out,
           b_out, ln1_g, ln1_b, w_router, b_router, w_gu, b_gu, w_down, b_down, ln2_g, ln2_b):
    assert x.shape == (BATCH, SEQ, D) and w_in.shape[0] == 1
    row = lambda a: a.reshape(1, -1)
    x2 = x.reshape(T, D)
    w_in0, b_in0 = w_in[0], b_in[0]
    c_lr, c_fn, c_mq, c_gt = 3072, 3072 + 2 * GLA_LR, 3104 + FN_W, 3104 + FN_W + MQ_W
    w_main = jnp.concatenate([w_in0[:, :c_lr], w_in0[:, c_gt:]], axis=1).astype(BF16)
    b_main = row(jnp.concatenate([b_in0[:c_lr], b_in0[c_gt:]]))
    w_lr = _pad_cols(w_in0[:, c_lr:c_fn], LANES).astype(BF16)
    b_lr = _pad_cols(row(b_in0[c_lr:c_fn]), LANES)
    w_mq = w_in0[:, c_mq:c_gt].astype(BF16)
    b_mq = row(b_in0[c_mq:c_gt])
    w_fn = w_in0[:, c_fn:c_mq].astype(BF16)
    b_fn = row(b_in0[c_fn:c_mq])
    lng, lnb = row(ln_in_g), row(ln_in_b)

    proj, mq, lr = _inproj(x2, lng, lnb, w_main, b_main, w_mq, b_mq, w_lr, b_lr)

    zpad = jnp.zeros((LANES - 2 * GLA_LR, GLA_H * GLA_DK), F32)
    zlr = jnp.zeros((GLA_LR, GLA_H * GLA_DK), F32)
    wdf = jnp.concatenate([w_decay_f[0], zlr, zpad], axis=0).reshape(LANES, GLA_H, GLA_DK)
    wdb = jnp.concatenate([zlr, w_decay_b[0], zpad], axis=0).reshape(LANES, GLA_H, GLA_DK)
    wd = jnp.concatenate([wdf, wdb], axis=2).reshape(LANES, GLA_H * 2 * GLA_DK).astype(BF16)
    wd = jnp.concatenate([wd, wd], axis=0)
    bd = jnp.concatenate([b_decay_f[0].reshape(GLA_H, GLA_DK),
                          b_decay_b[0].reshape(GLA_H, GLA_DK)], axis=1).reshape(1, -1)
    og = _gla(proj, lr, wd, bd, row(gla_norm_g[0]))

    fbig, cwt, swt, f2, ccs, perm = _dft_tables(MERGE_TM)
    x4 = x.reshape(BATCH, FFT_N2, FFT_N1, D)
    z = _fft2(_fft1(x4, lng, lnb, w_fn, b_fn, fbig, cwt, swt), f2)

    kv = _memkv(mem.reshape(BATCH * MEM_LEN, D), row(ln_mem_g), row(ln_mem_b),
                w_mem_kv[0].astype(BF16))

    w_r = _pad_cols(w_router[0], LANES)
    wr_hi = w_r.astype(BF16)
    wr_lo = (w_r - wr_hi.astype(F32)).astype(BF16)
    b_r = jnp.concatenate([row(b_router[0]),
                           jnp.full((1, LANES - N_EXP), NEG_BIG, F32)], axis=1)
    h1, h1p, eidx, topw, cnt = _merge(
        x2, og, z, mq, proj, kv, lng, lnb,
        w_br_gla[0].astype(BF16), ccs, perm, w_br_fnet[0].astype(BF16),
        w_br_mem[0].astype(BF16), w_out[0].astype(BF16), row(b_out[0]),
        row(ln1_g[0]), row(ln1_b[0]), jnp.concatenate([wr_hi, wr_lo], axis=1), wr_hi, b_r)

    dest = _plan(eidx, cnt)
    counts = cnt[0, :N_EXP].astype(I32)
    dest_k = dest[:, :TOP_K]
    idx4 = dest_k.reshape(SC_WORKERS, SCD_NCH, SC_CH, TOP_K).transpose(0, 1, 3, 2)
    xs = _sc_dispatch(h1p, idx4.reshape(SC_WORKERS, SCD_NCH * TOP_K, SC_CH))
    ys = _experts(_work_items(counts), xs, w_gu[0], b_gu[0], w_down[0], b_down[0])
    out = h1
    for grp in range(COMB_GROUPS):
        dest_g = dest_k[grp * COMB_TG:(grp + 1) * COMB_TG].T.reshape(SC_WORKERS, SC_NCH, SC_CH)
        g4 = _sc_gather(ys, dest_g).reshape(TOP_K, COMB_TG, D // 2)
        out = _combine_dense(g4, out, topw, row(ln2_g[0]), row(ln2_b[0]), grp)
    return out.reshape(BATCH, SEQ, D)
```

```python
import functools
import math

import numpy as np
import jax
import jax.numpy as jnp
from jax import lax
from jax.experimental import pallas as pl
from jax.experimental.pallas import tpu as pltpu
from jax.experimental.pallas import tpu_sc as plsc

F32 = jnp.float32
BF16 = jnp.bfloat16
I32 = jnp.int32
U32 = jnp.uint32

D = 1024
BATCH = 4
SEQ = 4096
T = BATCH * SEQ
GLA_H = 4
GLA_DK = 128
GLA_DV = 256
GLA_LR = 16
GLA_TAU = 16.0
GLA_C = 64
FN_G = 4
FN_GW = 128
FN_W = 512
MEM_LEN = 256
MEM_H = 4
MEM_HD = 128
MQ_W = 512
N_EXP = 32
TOP_K = 4
D_FF = 1024
SW_LIMIT = 7.0
SW_ALPHA = 1.702
LN_EPS = 1e-5
RMS_EPS = 1e-6
DN_ALPHA = 2.0 ** 0.25
A_ROWS = T * TOP_K

FFT_N1 = 128
FFT_N2 = 32

LANES = 128
NEG_BIG = -1e30
MIB = 1024 * 1024


def _ln(x, g, b):
    mu = jnp.mean(x, axis=-1, keepdims=True)
    xc = x - mu
    var = jnp.mean(xc * xc, axis=-1, keepdims=True)
    return xc * lax.rsqrt(var + LN_EPS) * g + b


def _dot(a, b):
    return jnp.dot(a, b, preferred_element_type=F32)


def _dot_nt(a, b):
    return lax.dot_general(a, b, (((1,), (1,)), ((), ())), preferred_element_type=F32)


def _dot_tn(a, b):
    return lax.dot_general(a, b, (((0,), (0,)), ((), ())), preferred_element_type=F32)


def _interleave(*stages):
    live = list(stages)
    while live:
        for st in list(live):
            try:
                next(st)
            except StopIteration:
                live.remove(st)


def _split_bf16(a):
    hi = a.astype(BF16)
    return hi, (a - hi.astype(F32)).astype(BF16)


INPROJ_TM = 1024
INPROJ_TN = 3072


PROJ_W = 6 * 1024


def _inproj_kernel(x_ref, g_ref, b_ref, w_ref, bias_ref, wmq_ref, bmq_ref, wlr_ref, blr_ref,
                   proj_ref, mq_ref, lr_ref, hb_ref):
    @pl.when(pl.program_id(1) == 0)
    def _():
        hb = _ln(x_ref[...], g_ref[...], b_ref[...]).astype(BF16)
        hb_ref[...] = hb
        lr_ref[...] = _dot(hb, wlr_ref[...]) + blr_ref[...]
        mq_ref[...] = (_dot(hb, wmq_ref[...]) + bmq_ref[...]).astype(BF16)

    proj_ref[...] = (_dot(hb_ref[...], w_ref[...]) + bias_ref[...]).astype(BF16)


def _inproj(x2, ln_g, ln_b, w_main, b_main, w_mq, b_mq, w_lr, b_lr):
    tm, tn = INPROJ_TM, INPROJ_TN
    nj = PROJ_W // tn
    row = lambda i, j: (i, 0)
    const = lambda i, j: (0, 0)
    outs = (
        jax.ShapeDtypeStruct((T, PROJ_W), BF16),
        jax.ShapeDtypeStruct((T, MQ_W), BF16),
        jax.ShapeDtypeStruct((T, LANES), F32),
    )
    return pl.pallas_call(
        _inproj_kernel,
        grid=(T // tm, nj),
        in_specs=[
            pl.BlockSpec((tm, D), row),
            pl.BlockSpec((1, D), const),
            pl.BlockSpec((1, D), const),
            pl.BlockSpec((D, tn), lambda i, j: (0, j)),
            pl.BlockSpec((1, tn), lambda i, j: (0, j)),
            pl.BlockSpec((D, MQ_W), const),
            pl.BlockSpec((1, MQ_W), const),
            pl.BlockSpec((D, LANES), const),
            pl.BlockSpec((1, LANES), const),
        ],
        out_specs=[
            pl.BlockSpec((tm, tn), lambda i, j: (i, j)),
            pl.BlockSpec((tm, MQ_W), row),
            pl.BlockSpec((tm, LANES), row),
        ],
        out_shape=outs,
        scratch_shapes=[pltpu.VMEM((tm, D), BF16)],
        compiler_params=pltpu.CompilerParams(
            dimension_semantics=("arbitrary", "arbitrary"),
            vmem_limit_bytes=48 * MIB),
        name="ln_inproj",
    )(x2, ln_g, ln_b, w_main, b_main, w_mq, b_mq, w_lr, b_lr)


GLA_BULK = 256
GLA_FIN = 512
GLA_NCH = SEQ // GLA_C
GLA_CPB = GLA_BULK // GLA_C
GLA_PIPE = 4


def _gla_kernel(q_ref, k_ref, v_ref, r_ref, lr_ref, wd_ref, bd_ref, g_ref, cs_ref, o_ref,
                acc_ref, qin_ref, kin_ref, kst_ref, dec_ref, u_ref, stf_ref, stb_ref):
    C = GLA_C
    G = GLA_BULK
    DK = GLA_DK
    NG = SEQ // G
    scale = DK ** -0.5
    ii = lax.broadcasted_iota(I32, (G, G), 0)
    jj = lax.broadcasted_iota(I32, (G, G), 1)
    same = (ii // C) == (jj // C)
    lower = jnp.logical_and(same, ii >= jj)
    upper = jnp.logical_and(same, ii <= jj)
    is_fwd = lax.broadcasted_iota(I32, (G, 2 * DK), 1) < DK
    chunk_of_row = lax.broadcasted_iota(I32, (G, DK), 0) // C

    def stage_a(gi):
        rows = pl.ds(pl.multiple_of(gi * G, G), G)
        z = _dot(jnp.concatenate(_split_bf16(lr_ref[rows, :]), axis=1), wd_ref[...]) + bd_ref[...]
        yield
        la = -(jnp.maximum(-z, 0.0) + jnp.log(1.0 + jnp.exp(-jnp.abs(z)))) * (1.0 / GLA_TAU)
        la_hi, la_lo = _split_bf16(la)
        pre2 = _dot(cs_ref[...], jnp.concatenate([la_hi, la_lo], axis=1))
        yield
        pre = pre2[:, :2 * DK] + pre2[:, 2 * DK:]
        blast = jnp.concatenate(
            [jnp.broadcast_to(pre[ci * C + C - 1:ci * C + C, :], (C, 2 * DK))
             for ci in range(GLA_CPB)], axis=0)
        b = jnp.where(is_fwd, pre, blast - pre + la)
        qf32 = q_ref[rows, :].astype(F32)
        kf32 = k_ref[rows, :].astype(F32)
        q2 = jnp.concatenate([qf32, qf32], axis=1)
        k2 = jnp.concatenate([kf32, kf32], axis=1)
        qin_ref[rows, :] = (q2 * (scale * jnp.exp(b))).astype(BF16)
        kin_ref[rows, :] = (k2 * jnp.exp(-b)).astype(BF16)
        kst_ref[rows, :] = (k2 * jnp.exp(blast - b)).astype(BF16)
        dec = jnp.exp(blast)
        for ci in range(GLA_CPB):
            dec_ref[pl.ds(gi * GLA_CPB + ci, 1), :] = dec[ci * C:ci * C + 1, :]

    def stage_b(gi):
        rows = pl.ds(pl.multiple_of(gi * G, G), G)
        qi = qin_ref[rows, :]
        ki = kin_ref[rows, :]
        ks = kst_ref[rows, :]
        vb = v_ref[rows, :]
        att_f = _dot_nt(qi[:, :DK], ki[:, :DK])
        att_b = _dot_nt(qi[:, DK:], ki[:, DK:])
        yield
        att = jnp.where(lower, att_f, 0.0) + jnp.where(upper, att_b, 0.0)
        acc_ref[rows, :] = _dot(att.astype(BF16), vb)
        yield
        ksb = jnp.concatenate(
            [jnp.where(chunk_of_row == ci, ks[:, d * DK:(d + 1) * DK], jnp.zeros((G, DK), BF16))
             for d in range(2) for ci in range(GLA_CPB)], axis=1)
        u = _dot_tn(vb, ksb)
        for d in range(2):
            for ci in range(GLA_CPB):
                col = (d * GLA_CPB + ci) * DK
                u_ref[d, gi * GLA_CPB + ci] = u[:, col:col + DK]

    P = GLA_PIPE
    _interleave(*[stage_a(j) for j in range(P)])

    def bulk(i, carry):
        g = P * i
        _interleave(*[st for j in range(P) for st in (stage_b(g - P + j), stage_a(g + j))])
        return carry

    lax.fori_loop(1, NG // P, bulk, 0)
    _interleave(*[stage_b(NG - P + j) for j in range(P)])

    stf_ref[...] = jnp.zeros_like(stf_ref)
    stb_ref[...] = jnp.zeros_like(stb_ref)

    def one(n, d, st_ref):
        lanes = slice(d * DK, (d + 1) * DK)
        rows = pl.ds(pl.multiple_of(n * C, C), C)
        st = st_ref[...]
        acc_ref[rows, :] += _dot_nt(qin_ref[rows, lanes], st.astype(BF16))
        st_ref[...] = st * dec_ref[pl.ds(n, 1), :][:, lanes] + u_ref[d, n]

    def step(i, carry):
        one(i, 0, stf_ref)
        one(GLA_NCH - 1 - i, 1, stb_ref)
        return carry

    lax.fori_loop(0, GLA_NCH, step, 0, unroll=8)

    def fin(gi, carry):
        rows = pl.ds(pl.multiple_of(gi * GLA_FIN, GLA_FIN), GLA_FIN)
        o = acc_ref[rows, :]
        o = o * lax.rsqrt(jnp.mean(o * o, axis=-1, keepdims=True) + RMS_EPS) * g_ref[...]
        rg = r_ref[rows, :].astype(F32)
        o_ref[rows, :] = (o * (rg * jax.nn.sigmoid(rg))).astype(BF16)
        return carry

    lax.fori_loop(0, SEQ // GLA_FIN, fin, 0)


def _gla(proj, lr, wd, bd, g):
    i = np.arange(GLA_BULK)
    cs = ((i[:, None] // GLA_C) == (i[None, :] // GLA_C)) & (i[:, None] >= i[None, :])
    cs = jnp.asarray(cs, dtype=F32).astype(BF16)
    v_blk = 1024 // GLA_DV
    return pl.pallas_call(
        _gla_kernel,
        grid=(BATCH, GLA_H),
        in_specs=[
            pl.BlockSpec((SEQ, GLA_DK), lambda b, h: (b, h)),
            pl.BlockSpec((SEQ, GLA_DK), lambda b, h: (b, GLA_H + h)),
            pl.BlockSpec((SEQ, GLA_DV), lambda b, h: (b, v_blk + h)),
            pl.BlockSpec((SEQ, GLA_DV), lambda b, h: (b, 2 * v_blk + h)),
            pl.BlockSpec((SEQ, LANES), lambda b, h: (b, 0)),
            pl.BlockSpec((2 * LANES, 2 * GLA_DK), lambda b, h: (0, h)),
            pl.BlockSpec((1, 2 * GLA_DK), lambda b, h: (0, h)),
            pl.BlockSpec((1, GLA_DV), lambda b, h: (0, 0)),
            pl.BlockSpec((GLA_BULK, GLA_BULK), lambda b, h: (0, 0)),
        ],
        out_specs=pl.BlockSpec((SEQ, GLA_DV), lambda b, h: (b, h)),
        out_shape=jax.ShapeDtypeStruct((T, GLA_H * GLA_DV), BF16),
        scratch_shapes=[
            pltpu.VMEM((SEQ, GLA_DV), F32),
            pltpu.VMEM((SEQ, 2 * GLA_DK), BF16),
            pltpu.VMEM((SEQ, 2 * GLA_DK), BF16),
            pltpu.VMEM((SEQ, 2 * GLA_DK), BF16),
            pltpu.VMEM((GLA_NCH, 2 * GLA_DK), F32),
            pltpu.VMEM((2, GLA_NCH, GLA_DV, GLA_DK), F32),
            pltpu.VMEM((GLA_DV, GLA_DK), F32),
            pltpu.VMEM((GLA_DV, GLA_DK), F32),
        ],
        compiler_params=pltpu.CompilerParams(
            dimension_semantics=("arbitrary", "arbitrary"),
            vmem_limit_bytes=58 * MIB),
        name="gla",
    )(proj, proj, proj, proj, lr, wd, bd, g, cs)


FFT1_S = 16
FFT1_ROWS = FFT_N2 * FFT1_S
FFT2_KB = 8


def _fft1_kernel(x_ref, g_ref, b_ref, w_ref, bias_ref, fbig_ref, cw_ref, sw_ref, o_ref):
    xv = x_ref[...].reshape(FFT1_ROWS, D)
    hb = _ln(xv, g_ref[...], b_ref[...]).astype(BF16)
    fn = (_dot(hb, w_ref[...]) + bias_ref[...]).astype(BF16)
    a = _dot(fbig_ref[...], fn)
    ar = a[:FFT1_ROWS]
    ai = a[FFT1_ROWS:]
    cw = jnp.concatenate([cw_ref[...]] * (FN_W // LANES), axis=1)
    sw = jnp.concatenate([sw_ref[...]] * (FN_W // LANES), axis=1)
    o_ref[0] = (ar * cw + ai * sw).reshape(FFT_N2, FFT1_S, FN_W).astype(BF16)
    o_ref[1] = (ai * cw - ar * sw).reshape(FFT_N2, FFT1_S, FN_W).astype(BF16)


def _fft1(x4, ln_g, ln_b, w_fn, b_fn, fbig, cwt, swt):
    s = FFT1_S
    const = lambda b, j: (0, 0)
    return pl.pallas_call(
        _fft1_kernel,
        grid=(BATCH, FFT_N1 // s),
        in_specs=[
            pl.BlockSpec((None, FFT_N2, s, D), lambda b, j: (b, 0, j, 0)),
            pl.BlockSpec((1, D), const),
            pl.BlockSpec((1, D), const),
            pl.BlockSpec((D, FN_W), const),
            pl.BlockSpec((1, FN_W), const),
            pl.BlockSpec((2 * FFT1_ROWS, FFT1_ROWS), const),
            pl.BlockSpec((None, FFT1_ROWS, LANES), lambda b, j: (j, 0, 0)),
            pl.BlockSpec((None, FFT1_ROWS, LANES), lambda b, j: (j, 0, 0)),
        ],
        out_specs=pl.BlockSpec((None, 2, FFT_N2, s, FN_W), lambda b, j: (b, 0, 0, j, 0)),
        out_shape=jax.ShapeDtypeStruct((BATCH, 2, FFT_N2, FFT_N1, FN_W), BF16),
        compiler_params=pltpu.CompilerParams(
            dimension_semantics=("arbitrary", "arbitrary"),
            vmem_limit_bytes=40 * MIB),
        name="fft_stage1",
    )(x4, ln_g, ln_b, w_fn, b_fn, fbig, cwt, swt)


def _fft2_kernel(d_ref, f2_ref, o_ref):
    f2 = f2_ref[...]
    for kk in range(FFT2_KB):
        z = _dot(f2, jnp.concatenate([d_ref[0, kk], d_ref[1, kk]], axis=0))
        o_ref[0, kk] = z[:FFT_N1].astype(BF16)
        o_ref[1, kk] = z[FFT_N1:].astype(BF16)


def _fft2(dmat, f2):
    kb = FFT2_KB
    blk = (None, 2, kb, FFT_N1, FN_W)
    return pl.pallas_call(
        _fft2_kernel,
        grid=(BATCH, FFT_N2 // kb),
        in_specs=[
            pl.BlockSpec(blk, lambda b, j: (b, 0, j, 0, 0)),
            pl.BlockSpec((2 * FFT_N1, 2 * FFT_N1), lambda b, j: (0, 0)),
        ],
        out_specs=pl.BlockSpec(blk, lambda b, j: (b, 0, j, 0, 0)),
        out_shape=jax.ShapeDtypeStruct((BATCH, 2, FFT_N2, FFT_N1, FN_W), BF16),
        compiler_params=pltpu.CompilerParams(
            dimension_semantics=("arbitrary", "arbitrary")),
        name="fft_stage2",
    )(dmat, f2)


def _dft_tables(merge_tm):
    s = FFT1_S
    n2 = np.arange(FFT_N2, dtype=np.float64)
    n1 = np.arange(FFT_N1, dtype=np.float64)
    th = 2.0 * np.pi * np.outer(n2, n2) / FFT_N2
    f1 = np.stack([np.cos(th), -np.sin(th)]) / math.sqrt(SEQ)
    fbig = np.einsum("rkn,st->rksnt", f1, np.eye(s)).reshape(2 * FFT_N2 * s, FFT_N2 * s)
    tw = 2.0 * np.pi * np.outer(n2, n1) / SEQ
    tw = tw.reshape(FFT_N2, FFT_N1 // s, s).transpose(1, 0, 2).reshape(FFT_N1 // s, FFT_N2 * s)
    cwt = np.broadcast_to(np.cos(tw)[:, :, None], tw.shape + (LANES,))
    swt = np.broadcast_to(np.sin(tw)[:, :, None], tw.shape + (LANES,))
    th1 = 2.0 * np.pi * np.outer(n1, n1) / FFT_N1
    c1, s1 = np.cos(th1), np.sin(th1)
    f2 = np.block([[c1, s1], [-s1, c1]])
    cc = np.arange(FN_GW, dtype=np.float64)
    thc = 2.0 * np.pi * np.outer(cc, cc) / FN_GW
    ccs = np.concatenate([np.cos(thc), np.sin(thc)], axis=0) / math.sqrt(FN_GW)
    k1n = merge_tm // FFT_N2
    r = np.arange(merge_tm)
    perm = np.zeros((merge_tm, merge_tm))
    perm[r, (r % FFT_N2) * k1n + r // FFT_N2] = 1.0
    as32 = lambda a: jnp.asarray(np.ascontiguousarray(a), dtype=F32)
    return (as32(fbig).astype(BF16), as32(cwt), as32(swt), as32(f2).astype(BF16),
            as32(ccs).astype(BF16), as32(perm).astype(BF16))


def _memkv_kernel(m_ref, g_ref, b_ref, w_ref, o_ref):
    mn = _ln(m_ref[...], g_ref[...], b_ref[...]).astype(BF16)
    o_ref[...] = _dot(mn, w_ref[...]).astype(BF16)


def _memkv(mem2, g, b, w):
    return pl.pallas_call(
        _memkv_kernel,
        grid=(BATCH,),
        in_specs=[
            pl.BlockSpec((MEM_LEN, D), lambda i: (i, 0)),
            pl.BlockSpec((1, D), lambda i: (0, 0)),
            pl.BlockSpec((1, D), lambda i: (0, 0)),
            pl.BlockSpec((D, 2 * MQ_W), lambda i: (0, 0)),
        ],
        out_specs=pl.BlockSpec((MEM_LEN, 2 * MQ_W), lambda i: (i, 0)),
        out_shape=jax.ShapeDtypeStruct((BATCH * MEM_LEN, 2 * MQ_W), BF16),
        compiler_params=pltpu.CompilerParams(dimension_semantics=("arbitrary",)),
        name="mem_kv",
    )(mem2, g, b, w)


MERGE_TM = 512
MERGE_K1 = MERGE_TM // FFT_N2


def _pack_bf16_pair(v):
    n = v.shape[1] // 2
    bits = lax.bitcast_convert_type(v.astype(BF16).astype(F32), U32)
    return (bits[:, n:] & jnp.uint32(0xFFFF0000)) | (bits[:, :n] >> 16)


def _unpack_bf16_pair(p):
    lo = lax.bitcast_convert_type(p << 16, F32)
    hi = lax.bitcast_convert_type(p & jnp.uint32(0xFFFF0000), F32)
    return lo, hi


def _merge_kernel(x_ref, og_ref, zr_ref, zi_ref, mq_ref, gt_ref, kv_ref,
                  lng_ref, lnb_ref, wg_ref, ccs_ref, perm_ref, wf_ref, wm_ref, wo_ref, bo_ref,
                  l1g_ref, l1b_ref, wr2_ref, wrh_ref, br_ref,
                  h1_ref, h1p_ref, eidx_ref, topw_ref, cnt_ref):
    tm = MERGE_TM
    y = {}

    def branch_fnet():
        zr = zr_ref[...].reshape(tm, FN_W)
        zi = zi_ref[...].reshape(tm, FN_W)
        ys = []
        for g in range(FN_G):
            sl = slice(g * FN_GW, (g + 1) * FN_GW)
            ys.append(_dot(jnp.concatenate([zr[:, sl], zi[:, sl]], axis=1), ccs_ref[...]))
        yield
        yp = _dot(perm_ref[...], jnp.concatenate(ys, axis=1).astype(BF16))
        yield
        y["fnet"] = _dot(yp.astype(BF16), wf_ref[...])

    def branch_mem():
        heads = [slice(hd * MEM_HD, (hd + 1) * MEM_HD) for hd in range(MEM_H)]
        ss = [_dot_nt(mq_ref[:, sl], kv_ref[:, sl]) for sl in heads]
        yield
        oms = []
        for hd, s in enumerate(ss):
            s = s * (MEM_HD ** -0.5)
            s = s - jnp.max(s, axis=-1, keepdims=True)
            p = jnp.exp(s)
            p = p * (1.0 / jnp.sum(p, axis=-1, keepdims=True))
            oms.append(_dot(p.astype(BF16),
                            kv_ref[:, MQ_W + hd * MEM_HD:MQ_W + (hd + 1) * MEM_HD]))
        yield
        y["mem"] = _dot(jnp.concatenate(oms, axis=1).astype(BF16), wm_ref[...])

    def branch_gla():
        y["gla"] = _dot(og_ref[...], wg_ref[...])
        yield

    _interleave(branch_fnet(), branch_mem(), branch_gla())

    def gate(c):
        return 0.5 + 0.5 * jnp.tanh(0.5 * gt_ref[:, c * D:(c + 1) * D].astype(F32))

    merged = gate(0) * y["gla"] + gate(1) * y["fnet"] + gate(2) * y["mem"]
    mix = _dot(merged.astype(BF16), wo_ref[...]) + bo_ref[...]
    h = _ln(x_ref[...], lng_ref[...], lnb_ref[...])
    h1 = _ln(DN_ALPHA * h + mix, l1g_ref[...], l1b_ref[...])
    h1_ref[...] = h1
    h1p_ref[...] = _pack_bf16_pair(h1)

    h_hi, h_lo = _split_bf16(h1)
    d2 = _dot(h_hi, wr2_ref[...])
    l = d2[:, :LANES] + d2[:, LANES:] + _dot(h_lo, wrh_ref[...]) + br_ref[...]
    lane = lax.broadcasted_iota(I32, (tm, LANES), 1)
    vals, idxs = [], []
    for _ in range(TOP_K):
        m = jnp.max(l, axis=-1, keepdims=True)
        idx = jnp.min(jnp.where(l == m, lane, LANES), axis=-1, keepdims=True)
        vals.append(m)
        idxs.append(idx)
        l = jnp.where(lane == idx, -jnp.inf, l)
    es = [jnp.exp(v - vals[0]) for v in vals]
    den = es[0] + es[1] + es[2] + es[3]
    eo = jnp.zeros((tm, LANES), I32)
    wo = jnp.zeros((tm, LANES), F32)
    chosen = jnp.zeros((tm, LANES), F32)
    for k in range(TOP_K):
        eo = jnp.where(lane == k, idxs[k], eo)
        wo = jnp.where(lane == k, es[k] / den, wo)
        chosen = chosen + jnp.where(lane == idxs[k], 1.0, 0.0)
    eidx_ref[...] = eo
    topw_ref[...] = wo

    @pl.when(pl.program_id(0) == 0)
    def _():
        cnt_ref[...] = jnp.zeros_like(cnt_ref)

    cnt_ref[...] += jnp.broadcast_to(jnp.sum(chosen, axis=0, keepdims=True), cnt_ref.shape)


def _merge(x2, og, z, mq, gates, kv, lng, lnb, wg, ccs, perm, wf, wm, wo, bo, l1g, l1b,
           wr2, wrh, br):
    tm = MERGE_TM
    per_b = SEQ // tm
    row = lambda i: (i, 0)
    const = lambda i: (0, 0)
    zblk = (None, None, FFT_N2, MERGE_K1, FN_W)
    outs = (
        jax.ShapeDtypeStruct((T, D), F32),
        jax.ShapeDtypeStruct((T, D // 2), U32),
        jax.ShapeDtypeStruct((T, LANES), I32),
        jax.ShapeDtypeStruct((T, LANES), F32),
        jax.ShapeDtypeStruct((8, LANES), F32),
    )
    return pl.pallas_call(
        _merge_kernel,
        grid=(T // tm,),
        in_specs=[
            pl.BlockSpec((tm, D), row),
            pl.BlockSpec((tm, D), row),
            pl.BlockSpec(zblk, lambda i: (i // per_b, 0, 0, i % per_b, 0)),
            pl.BlockSpec(zblk, lambda i: (i // per_b, 1, 0, i % per_b, 0)),
            pl.BlockSpec((tm, MQ_W), row),
            pl.BlockSpec((tm, 3 * D), lambda i: (i, 1)),
            pl.BlockSpec((MEM_LEN, 2 * MQ_W), lambda i: (i // per_b, 0)),
            pl.BlockSpec((1, D), const), pl.BlockSpec((1, D), const),
            pl.BlockSpec((D, D), const),
            pl.BlockSpec((2 * FN_GW, FN_GW), const),
            pl.BlockSpec((tm, tm), const),
            pl.BlockSpec((FN_W, D), const),
            pl.BlockSpec((MQ_W, D), const),
            pl.BlockSpec((D, D), const),
            pl.BlockSpec((1, D), const),
            pl.BlockSpec((1, D), const), pl.BlockSpec((1, D), const),
            pl.BlockSpec((D, 2 * LANES), const),
            pl.BlockSpec((D, LANES), const),
            pl.BlockSpec((1, LANES), const),
        ],
        out_specs=[
            pl.BlockSpec((tm, D), row),
            pl.BlockSpec((tm, D // 2), row),
            pl.BlockSpec((tm, LANES), row),
            pl.BlockSpec((tm, LANES), row),
            pl.BlockSpec((8, LANES), const),
        ],
        out_shape=outs,
        compiler_params=pltpu.CompilerParams(
            dimension_semantics=("arbitrary",),
            vmem_limit_bytes=58 * MIB),
        name="merge_ln1_router",
    )(x2, og, z, z, mq, gates, kv, lng, lnb, wg, ccs, perm, wf, wm, wo, bo, l1g, l1b,
      wr2, wrh, br)


PLAN_TP = 1024


def _expert_onehots(e, lane):
    onehots = [lane == e[:, k:k + 1] for k in range(TOP_K)]
    mf = jnp.zeros(lane.shape, F32)
    for oh in onehots:
        mf = mf + jnp.where(oh, 1.0, 0.0)
    return onehots, mf


def _plan_kernel(e_ref, tot_ref, dest_ref, cnt_ref, off_ref):
    i = pl.program_id(0)
    tp = PLAN_TP
    lane = lax.broadcasted_iota(I32, (tp, LANES), 1)
    onehots, mf = _expert_onehots(e_ref[...], lane)

    @pl.when(i == 0)
    def _():
        tot = tot_ref[0:1, :]
        padded = jnp.floor((tot + (MOE_BM - 1)) * (1.0 / MOE_BM)) * MOE_BM
        lane1 = lax.broadcasted_iota(I32, (1, LANES), 1)
        inc = padded
        for s in (1, 2, 4, 8, 16, 32, 64):
            inc = inc + jnp.where(lane1 >= s, pltpu.roll(inc, s, 1), 0.0)
        off_ref[...] = inc - padded
        cnt_ref[...] = jnp.zeros_like(cnt_ref)

    ri = lax.broadcasted_iota(I32, (tp, tp), 0)
    ci = lax.broadcasted_iota(I32, (tp, tp), 1)
    ltri = jnp.where(ri > ci, 1.0, 0.0).astype(BF16)
    rank = _dot(ltri, mf.astype(BF16)) + cnt_ref[...] + off_ref[...]
    out = jnp.zeros((tp, LANES), I32)
    for k in range(TOP_K):
        dk = jnp.sum(jnp.where(onehots[k], rank, 0.0), axis=-1, keepdims=True)
        out = jnp.where(lane == k, dk.astype(I32), out)
    dest_ref[...] = out
    cnt_ref[...] += jnp.sum(mf, axis=0, keepdims=True)


def _plan(eidx, cnt):
    tp = PLAN_TP
    return pl.pallas_call(
        _plan_kernel,
        grid=(T // tp,),
        in_specs=[pl.BlockSpec((tp, LANES), lambda i: (i, 0)),
                  pl.BlockSpec((8, LANES), lambda i: (0, 0))],
        out_specs=pl.BlockSpec((tp, LANES), lambda i: (i, 0)),
        out_shape=jax.ShapeDtypeStruct((T, LANES), I32),
        scratch_shapes=[pltpu.VMEM((1, LANES), F32), pltpu.VMEM((1, LANES), F32)],
        compiler_params=pltpu.CompilerParams(dimension_semantics=("arbitrary",)),
        name="route_plan",
    )(eidx, cnt)


MOE_BM = 512
MOE_NW = A_ROWS // MOE_BM + N_EXP
XS_ROWS = MOE_NW * MOE_BM
MOE_FF_SLICES = 2


def _expert_kernel(we_ref, wb_ref, wv_ref, wfe_ref, wsl_ref, wnx_ref,
                   x_ref, wgu_hbm, bgu_ref, wdn_hbm, bdn_ref, o_ref,
                   wgu_f32, wdn_f32, wgu_bf, wdn_bf, sems):
    w = pl.program_id(0)
    e = we_ref[w]

    def weight_copies(expert, slot):
        return (pltpu.make_async_copy(wgu_hbm.at[expert], wgu_f32.at[slot], sems.at[slot, 0]),
                pltpu.make_async_copy(wdn_hbm.at[expert], wdn_f32.at[slot], sems.at[slot, 1]))

    @pl.when(w == 0)
    def _():
        for cp_ in weight_copies(e, 0):
            cp_.start()

    @pl.when(wfe_ref[w] == 1)
    def _():
        slot = wsl_ref[w]
        for cp_ in weight_copies(e, slot):
            cp_.wait()
        wgu_bf[...] = wgu_f32[slot].astype(BF16)
        wdn_bf[...] = wdn_f32[slot].astype(BF16)
        nxt = wnx_ref[w]

        @pl.when(nxt >= 0)
        def _():
            for cp_ in weight_copies(nxt, 1 - slot):
                cp_.start()

    def ffn(rows):
        xlo, xhi = _unpack_bf16_pair(x_ref[rows, :])
        xb = jnp.concatenate([xlo.astype(BF16), xhi.astype(BF16)], axis=1)
        bgu = bgu_ref[pl.ds(e, 1), :]
        hw = D_FF // MOE_FF_SLICES
        gus = []
        for hf in range(MOE_FF_SLICES):
            gc = slice(hf * hw, (hf + 1) * hw)
            uc = slice(D_FF + hf * hw, D_FF + (hf + 1) * hw)
            gus.append((_dot(xb, wgu_bf[:, gc]) + bgu[:, gc], _dot(xb, wgu_bf[:, uc]) + bgu[:, uc]))
        out = bdn_ref[pl.ds(e, 1), :]
        for hf, (g, u) in enumerate(gus):
            gate = jnp.minimum(g, SW_LIMIT)
            up = jnp.clip(u, -SW_LIMIT, SW_LIMIT)
            act = (up + 1.0) * (gate * jax.nn.sigmoid(SW_ALPHA * gate))
            out = out + _dot(act.astype(BF16), wdn_bf[hf * hw:(hf + 1) * hw, :])
        o_ref[rows, :] = _pack_bf16_pair(out)

    @pl.when(wv_ref[w] == 1)
    def _():
        ffn(slice(None))

    @pl.when(wv_ref[w] == 2)
    def _():
        ffn(slice(0, MOE_BM // 2))


def _experts(meta, xs, w_gu, b_gu, w_down, b_down):
    return pl.pallas_call(
        _expert_kernel,
        grid_spec=pltpu.PrefetchScalarGridSpec(
            num_scalar_prefetch=len(meta),
            grid=(MOE_NW,),
            in_specs=[
                pl.BlockSpec((MOE_BM, D // 2), lambda w, we, wb, *_: (wb[w], 0)),
                pl.BlockSpec(memory_space=pl.ANY),
                pl.BlockSpec((N_EXP, 2 * D_FF), lambda w, *_: (0, 0)),
                pl.BlockSpec(memory_space=pl.ANY),
                pl.BlockSpec((N_EXP, D), lambda w, *_: (0, 0)),
            ],
            out_specs=pl.BlockSpec((MOE_BM, D // 2), lambda w, we, wb, *_: (wb[w], 0)),
            scratch_shapes=[
                pltpu.VMEM((2, D, 2 * D_FF), F32),
                pltpu.VMEM((2, D_FF, D), F32),
                pltpu.VMEM((D, 2 * D_FF), BF16),
                pltpu.VMEM((D_FF, D), BF16),
                pltpu.SemaphoreType.DMA((2, 2)),
            ],
        ),
        out_shape=jax.ShapeDtypeStruct((XS_ROWS, D // 2), U32),
        compiler_params=pltpu.CompilerParams(
            dimension_semantics=("arbitrary",),
            vmem_limit_bytes=56 * MIB),
        name="moe_experts",
    )(*meta, xs, w_gu, b_gu, w_down, b_down)


def _work_items(counts):
    n_e = (counts + MOE_BM - 1) // MOE_BM
    item_end = jnp.cumsum(n_e)
    total = item_end[-1]
    w = jnp.arange(MOE_NW, dtype=I32)
    valid = w < total
    wc = jnp.minimum(w, total - 1)
    e_w = jnp.sum((item_end[None, :] <= wc[:, None]).astype(I32), axis=1)
    e_w = jnp.minimum(e_w, N_EXP - 1)
    rows_here = counts[e_w] - (wc - (item_end - n_e)[e_w]) * MOE_BM
    valid = jnp.where(valid, jnp.where(rows_here <= MOE_BM // 2, 2, 1), 0)
    prev_e = jnp.concatenate([jnp.full((1,), -1, I32), e_w[:-1]])
    fe = (e_w != prev_e).astype(I32)
    slot = (jnp.cumsum(fe) - 1) % 2
    first_at = jnp.where(fe == 1, w, MOE_NW)
    next_first = jnp.concatenate([lax.cummin(first_at, reverse=True)[1:],
                                  jnp.full((1,), MOE_NW, I32)])
    nxt = jnp.where(next_first < MOE_NW, e_w[jnp.minimum(next_first, MOE_NW - 1)], -1)
    return tuple(a.astype(I32) for a in (e_w, wc, valid, fe, slot, nxt))


COMB_TM = 512
SC_CORES = 2
SC_SUBCORES = 16
SC_WORKERS = SC_CORES * SC_SUBCORES
SC_CH = 64
COMB_GROUPS = 4
COMB_TG = T // COMB_GROUPS
SC_ROWS_PER_W = COMB_TG * TOP_K // SC_WORKERS
SC_NCH = SC_ROWS_PER_W // SC_CH


def _sc_gather(table, idx3):
    mesh = plsc.VectorSubcoreMesh(core_axis_name="c", subcore_axis_name="s")

    @functools.partial(
        pl.kernel, mesh=mesh,
        out_type=jax.ShapeDtypeStruct((COMB_TG * TOP_K, D // 2), U32),
        scratch_types=[
            pltpu.VMEM((SC_NCH, SC_CH), I32),
            pltpu.VMEM((2, SC_CH, D // 2), U32),
            pltpu.SemaphoreType.DMA((2,)),
            pltpu.SemaphoreType.DMA((2,)),
        ],
    )
    def k(table_hbm, idx_hbm, out_hbm, idx_v, rows_v, gsem, psem):
        wid = lax.axis_index("s") * SC_CORES + lax.axis_index("c")
        base = wid * SC_ROWS_PER_W
        pltpu.sync_copy(idx_hbm.at[wid], idx_v)

        def gather(j, b):
            return pltpu.make_async_copy(table_hbm.at[idx_v.at[j]], rows_v.at[b], gsem.at[b])

        def put(j, b):
            return pltpu.make_async_copy(rows_v.at[b], out_hbm.at[pl.ds(base + j * SC_CH, SC_CH)],
                                         psem.at[b])

        gather(0, 0).start()

        @pl.loop(0, SC_NCH, step=2)
        def _(j0):
            for b in range(2):
                j = j0 + b

                @pl.when(j + 1 < SC_NCH)
                def _():
                    @pl.when(j >= 1)
                    def _():
                        put(j - 1, 1 - b).wait()
                    gather(j + 1, 1 - b).start()

                gather(j, b).wait()
                put(j, b).start()

        put(SC_NCH - 2, 0).wait()
        put(SC_NCH - 1, 1).wait()

    return k(table, idx3)


SCD_TOK_PER_W = T // SC_WORKERS
SCD_NCH = SCD_TOK_PER_W // SC_CH


def _sc_dispatch(h1p, idx4):
    mesh = plsc.VectorSubcoreMesh(core_axis_name="c", subcore_axis_name="s")

    @functools.partial(
        pl.kernel, mesh=mesh,
        out_type=jax.ShapeDtypeStruct((XS_ROWS, D // 2), U32),
        scratch_types=[
            pltpu.VMEM((SCD_NCH * TOP_K, SC_CH), I32),
            pltpu.VMEM((2, SC_CH, D // 2), U32),
            pltpu.SemaphoreType.DMA((2,)),
            pltpu.SemaphoreType.DMA((2,)),
        ],
    )
    def k(h_hbm, idx_hbm, xs_hbm, idx_v, rows_v, gsem, psem):
        wid = lax.axis_index("s") * SC_CORES + lax.axis_index("c")
        base = wid * SCD_TOK_PER_W
        pltpu.sync_copy(idx_hbm.at[wid], idx_v)

        def get(c, b):
            return pltpu.make_async_copy(h_hbm.at[pl.ds(base + c * SC_CH, SC_CH)], rows_v.at[b],
                                         gsem.at[b])

        def puts(c, b):
            return [pltpu.make_async_copy(rows_v.at[b], xs_hbm.at[idx_v.at[c * TOP_K + kk]],
                                          psem.at[b]) for kk in range(TOP_K)]

        get(0, 0).start()

        @pl.loop(0, SCD_NCH, step=2)
        def _(c0):
            for b in range(2):
                c = c0 + b

                @pl.when(c + 1 < SCD_NCH)
                def _():
                    @pl.when(c >= 1)
                    def _():
                        for cp_ in puts(c - 1, 1 - b):
                            cp_.wait()
                    get(c + 1, 1 - b).start()

                get(c, b).wait()
                for cp_ in puts(c, b):
                    cp_.start()

        for cp_ in puts(SCD_NCH - 2, 0) + puts(SCD_NCH - 1, 1):
            cp_.wait()

    return k(h1p, idx4)


def _combine_dense_kernel(g_ref, h1_ref, tw_ref, lg_ref, lb_ref, o_ref):
    tw = tw_ref[...]
    ylo = jnp.zeros((COMB_TM, D // 2), F32)
    yhi = jnp.zeros((COMB_TM, D // 2), F32)
    for k in range(TOP_K):
        lo, hi = _unpack_bf16_pair(g_ref[k])
        wk = tw[:, k:k + 1]
        ylo = ylo + lo * wk
        yhi = yhi + hi * wk
    ff = jnp.concatenate([ylo, yhi], axis=1)
    o_ref[...] = _ln(DN_ALPHA * h1_ref[...] + ff, lg_ref[...], lb_ref[...])


def _combine_dense(g4, h1, topw, g, b, group):
    tm = COMB_TM
    t0 = group * (COMB_TG // tm)
    return pl.pallas_call(
        _combine_dense_kernel,
        grid=(COMB_TG // tm,),
        in_specs=[
            pl.BlockSpec((TOP_K, tm, D // 2), lambda i: (0, i, 0)),
            pl.BlockSpec((tm, D), lambda i: (t0 + i, 0)),
            pl.BlockSpec((tm, LANES), lambda i: (t0 + i, 0)),
            pl.BlockSpec((1, D), lambda i: (0, 0)),
            pl.BlockSpec((1, D), lambda i: (0, 0)),
        ],
        out_specs=pl.BlockSpec((tm, D), lambda i: (t0 + i, 0)),
        out_shape=jax.ShapeDtypeStruct((T, D), F32),
        input_output_aliases={1: 0},
        compiler_params=pltpu.CompilerParams(
            dimension_semantics=("arbitrary",),
            vmem_limit_bytes=40 * MIB),
        name="moe_combine_dense_ln2",
    )(g4, h1, topw, g, b)


def _pad_cols(a, n):
    return jnp.pad(a, ((0, 0), (0, n - a.shape[1])))


def kernel(x, mem, ln_in_g, ln_in_b, ln_mem_g, ln_mem_b, w_in, b_in, w_decay_f, b_decay_f,
           w_decay_b, b_decay_b, gla_norm_g, w_br_gla, w_br_fnet, w_br_mem, w_mem_kv, w_out,
           b_out, ln1_g, ln1_b, w_router, b_router, w_gu, b_gu, w_down, b_down, ln2_g, ln2_b):
    assert x.shape == (BATCH, SEQ, D) and w_in.shape[0] == 1
    row = lambda a: a.reshape(1, -1)
    x2 = x.reshape(T, D)
    w_in0, b_in0 = w_in[0], b_in[0]
    c_lr, c_fn, c_mq, c_gt = 3072, 3072 + 2 * GLA_LR, 3104 + FN_W, 3104 + FN_W + MQ_W
    w_main = jnp.concatenate([w_in0[:, :c_lr], w_in0[:, c_gt:]], axis=1).astype(BF16)
    b_main = row(jnp.concatenate([b_in0[:c_lr], b_in0[c_gt:]]))
    w_lr = _pad_cols(w_in0[:, c_lr:c_fn], LANES).astype(BF16)
    b_lr = _pad_cols(row(b_in0[c_lr:c_fn]), LANES)
    w_mq = w_in0[:, c_mq:c_gt].astype(BF16)
    b_mq = row(b_in0[c_mq:c_gt])
    w_fn = w_in0[:, c_fn:c_mq].astype(BF16)
    b_fn = row(b_in0[c_fn:c_mq])
    lng, lnb = row(ln_in_g), row(ln_in_b)

    proj, mq, lr = _inproj(x2, lng, lnb, w_main, b_main, w_mq, b_mq, w_lr, b_lr)

    zpad = jnp.zeros((LANES - 2 * GLA_LR, GLA_H * GLA_DK), F32)
    zlr = jnp.zeros((GLA_LR, GLA_H * GLA_DK), F32)
    wdf = jnp.concatenate([w_decay_f[0], zlr, zpad], axis=0).reshape(LANES, GLA_H, GLA_DK)
    wdb = jnp.concatenate([zlr, w_decay_b[0], zpad], axis=0).reshape(LANES, GLA_H, GLA_DK)
    wd = jnp.concatenate([wdf, wdb], axis=2).reshape(LANES, GLA_H * 2 * GLA_DK).astype(BF16)
    wd = jnp.concatenate([wd, wd], axis=0)
    bd = jnp.concatenate([b_decay_f[0].reshape(GLA_H, GLA_DK),
                          b_decay_b[0].reshape(GLA_H, GLA_DK)], axis=1).reshape(1, -1)
    og = _gla(proj, lr, wd, bd, row(gla_norm_g[0]))

    fbig, cwt, swt, f2, ccs, perm = _dft_tables(MERGE_TM)
    x4 = x.reshape(BATCH, FFT_N2, FFT_N1, D)
    z = _fft2(_fft1(x4, lng, lnb, w_fn, b_fn, fbig, cwt, swt), f2)

    kv = _memkv(mem.reshape(BATCH * MEM_LEN, D), row(ln_mem_g), row(ln_mem_b),
                w_mem_kv[0].astype(BF16))

    w_r = _pad_cols(w_router[0], LANES)
    wr_hi = w_r.astype(BF16)
    wr_lo = (w_r - wr_hi.astype(F32)).astype(BF16)
    b_r = jnp.concatenate([row(b_router[0]),
                           jnp.full((1, LANES - N_EXP), NEG_BIG, F32)], axis=1)
    h1, h1p, eidx, topw, cnt = _merge(
        x2, og, z, mq, proj, kv, lng, lnb,
        w_br_gla[0].astype(BF16), ccs, perm, w_br_fnet[0].astype(BF16),
        w_br_mem[0].astype(BF16), w_out[0].astype(BF16), row(b_out[0]),
        row(ln1_g[0]), row(ln1_b[0]), jnp.concatenate([wr_hi, wr_lo], axis=1), wr_hi, b_r)

    dest = _plan(eidx, cnt)
    counts = cnt[0, :N_EXP].astype(I32)
    dest_k = dest[:, :TOP_K]
    idx4 = dest_k.reshape(SC_WORKERS, SCD_NCH, SC_CH, TOP_K).transpose(0, 1, 3, 2)
    xs = _sc_dispatch(h1p, idx4.reshape(SC_WORKERS, SCD_NCH * TOP_K, SC_CH))
    ys = _experts(_work_items(counts), xs, w_gu[0], b_gu[0], w_down[0], b_down[0])
    out = h1
    for grp in range(COMB_GROUPS):
        dest_g = dest_k[grp * COMB_TG:(grp + 1) * COMB_TG].T.reshape(SC_WORKERS, SC_NCH, SC_CH)
        g4 = _sc_gather(ys, dest_g).reshape(TOP_K, COMB_TG, D // 2)
        out = _combine_dense(g4, out, topw, row(ln2_g[0]), row(ln2_b[0]), grp)
    return out.reshape(BATCH, SEQ, D)
```

```python
import functools
import math

import numpy as np
import jax
import jax.numpy as jnp
from jax import lax
from jax.experimental import pallas as pl
from jax.experimental.pallas import tpu as pltpu
from jax.experimental.pallas import tpu_sc as plsc

F32 = jnp.float32
BF16 = jnp.bfloat16
I32 = jnp.int32
U32 = jnp.uint32

D = 1024
BATCH = 4
SEQ = 4096
T = BATCH * SEQ
GLA_H = 4
GLA_DK = 128
GLA_DV = 256
GLA_LR = 16
GLA_TAU = 16.0
GLA_C = 64
FN_G = 4
FN_GW = 128
FN_W = 512
MEM_LEN = 256
MEM_H = 4
MEM_HD = 128
MQ_W = 512
N_EXP = 32
TOP_K = 4
D_FF = 1024
SW_LIMIT = 7.0
SW_ALPHA = 1.702
LN_EPS = 1e-5
RMS_EPS = 1e-6
DN_ALPHA = 2.0 ** 0.25
A_ROWS = T * TOP_K

FFT_N1 = 128
FFT_N2 = 32

LANES = 128
NEG_BIG = -1e30
MIB = 1024 * 1024


def _ln(x, g, b):
    mu = jnp.mean(x, axis=-1, keepdims=True)
    xc = x - mu
    var = jnp.mean(xc * xc, axis=-1, keepdims=True)
    return xc * lax.rsqrt(var + LN_EPS) * g + b


def _dot(a, b):
    return jnp.dot(a, b, preferred_element_type=F32)


def _dot_nt(a, b):
    return lax.dot_general(a, b, (((1,), (1,)), ((), ())), preferred_element_type=F32)


def _dot_tn(a, b):
    return lax.dot_general(a, b, (((0,), (0,)), ((), ())), preferred_element_type=F32)


def _interleave(*stages):
    live = list(stages)
    while live:
        for st in list(live):
            try:
                next(st)
            except StopIteration:
                live.remove(st)


def _split_bf16(a):
    hi = a.astype(BF16)
    return hi, (a - hi.astype(F32)).astype(BF16)


INPROJ_TM = 1024
INPROJ_TN = 3072


PROJ_W = 6 * 1024


def _inproj_kernel(x_ref, g_ref, b_ref, w_ref, bias_ref, wmq_ref, bmq_ref, wlr_ref, blr_ref,
                   proj_ref, mq_ref, lr_ref, hb_ref):
    @pl.when(pl.program_id(1) == 0)
    def _():
        hb = _ln(x_ref[...], g_ref[...], b_ref[...]).astype(BF16)
        hb_ref[...] = hb
        lr_ref[...] = _dot(hb, wlr_ref[...]) + blr_ref[...]
        mq_ref[...] = (_dot(hb, wmq_ref[...]) + bmq_ref[...]).astype(BF16)

    proj_ref[...] = (_dot(hb_ref[...], w_ref[...]) + bias_ref[...]).astype(BF16)


def _inproj(x2, ln_g, ln_b, w_main, b_main, w_mq, b_mq, w_lr, b_lr):
    tm, tn = INPROJ_TM, INPROJ_TN
    nj = PROJ_W // tn
    row = lambda i, j: (i, 0)
    const = lambda i, j: (0, 0)
    outs = (
        jax.ShapeDtypeStruct((T, PROJ_W), BF16),
        jax.ShapeDtypeStruct((T, MQ_W), BF16),
        jax.ShapeDtypeStruct((T, LANES), F32),
    )
    return pl.pallas_call(
        _inproj_kernel,
        grid=(T // tm, nj),
        in_specs=[
            pl.BlockSpec((tm, D), row),
            pl.BlockSpec((1, D), const),
            pl.BlockSpec((1, D), const),
            pl.BlockSpec((D, tn), lambda i, j: (0, j)),
            pl.BlockSpec((1, tn), lambda i, j: (0, j)),
            pl.BlockSpec((D, MQ_W), const),
            pl.BlockSpec((1, MQ_W), const),
            pl.BlockSpec((D, LANES), const),
            pl.BlockSpec((1, LANES), const),
        ],
        out_specs=[
            pl.BlockSpec((tm, tn), lambda i, j: (i, j)),
            pl.BlockSpec((tm, MQ_W), row),
            pl.BlockSpec((tm, LANES), row),
        ],
        out_shape=outs,
        scratch_shapes=[pltpu.VMEM((tm, D), BF16)],
        compiler_params=pltpu.CompilerParams(
            dimension_semantics=("arbitrary", "arbitrary"),
            vmem_limit_bytes=48 * MIB),
        name="ln_inproj",
    )(x2, ln_g, ln_b, w_main, b_main, w_mq, b_mq, w_lr, b_lr)


GLA_BULK = 256
GLA_FIN = 512
GLA_NCH = SEQ // GLA_C
GLA_CPB = GLA_BULK // GLA_C
GLA_PIPE = 4


def _gla_kernel(q_ref, k_ref, v_ref, r_ref, lr_ref, wd_ref, bd_ref, g_ref, cs_ref, o_ref,
                acc_ref, qin_ref, kin_ref, kst_ref, dec_ref, u_ref, stf_ref, stb_ref):
    C = GLA_C
    G = GLA_BULK
    DK = GLA_DK
    NG = SEQ // G
    scale = DK ** -0.5
    ii = lax.broadcasted_iota(I32, (G, G), 0)
    jj = lax.broadcasted_iota(I32, (G, G), 1)
    same = (ii // C) == (jj // C)
    lower = jnp.logical_and(same, ii >= jj)
    upper = jnp.logical_and(same, ii <= jj)
    is_fwd = lax.broadcasted_iota(I32, (G, 2 * DK), 1) < DK
    chunk_of_row = lax.broadcasted_iota(I32, (G, DK), 0) // C

    def stage_a(gi):
        rows = pl.ds(pl.multiple_of(gi * G, G), G)
        z = _dot(jnp.concatenate(_split_bf16(lr_ref[rows, :]), axis=1), wd_ref[...]) + bd_ref[...]
        yield
        la = -(jnp.maximum(-z, 0.0) + jnp.log(1.0 + jnp.exp(-jnp.abs(z)))) * (1.0 / GLA_TAU)
        la_hi, la_lo = _split_bf16(la)
        pre2 = _dot(cs_ref[...], jnp.concatenate([la_hi, la_lo], axis=1))
        yield
        pre = pre2[:, :2 * DK] + pre2[:, 2 * DK:]
        blast = jnp.concatenate(
            [jnp.broadcast_to(pre[ci * C + C - 1:ci * C + C, :], (C, 2 * DK))
             for ci in range(GLA_CPB)], axis=0)
        b = jnp.where(is_fwd, pre, blast - pre + la)
        qf32 = q_ref[rows, :].astype(F32)
        kf32 = k_ref[rows, :].astype(F32)
        q2 = jnp.concatenate([qf32, qf32], axis=1)
        k2 = jnp.concatenate([kf32, kf32], axis=1)
        qin_ref[rows, :] = (q2 * (scale * jnp.exp(b))).astype(BF16)
        kin_ref[rows, :] = (k2 * jnp.exp(-b)).astype(BF16)
        kst_ref[rows, :] = (k2 * jnp.exp(blast - b)).astype(BF16)
        dec = jnp.exp(blast)
        for ci in range(GLA_CPB):
            dec_ref[pl.ds(gi * GLA_CPB + ci, 1), :] = dec[ci * C:ci * C + 1, :]

    def stage_b(gi):
        rows = pl.ds(pl.multiple_of(gi * G, G), G)
        qi = qin_ref[rows, :]
        ki = kin_ref[rows, :]
        ks = kst_ref[rows, :]
        vb = v_ref[rows, :]
        att_f = _dot_nt(qi[:, :DK], ki[:, :DK])
        att_b = _dot_nt(qi[:, DK:], ki[:, DK:])
        yield
        att = jnp.where(lower, att_f, 0.0) + jnp.where(upper, att_b, 0.0)
        acc_ref[rows, :] = _dot(att.astype(BF16), vb)
        yield
        ksb = jnp.concatenate(
            [jnp.where(chunk_of_row == ci, ks[:, d * DK:(d + 1) * DK], jnp.zeros((G, DK), BF16))
             for d in range(2) for ci in range(GLA_CPB)], axis=1)
        u = _dot_tn(vb, ksb)
        for d in range(2):
            for ci in range(GLA_CPB):
                col = (d * GLA_CPB + ci) * DK
                u_ref[d, gi * GLA_CPB + ci] = u[:, col:col + DK]

    P = GLA_PIPE
    _interleave(*[stage_a(j) for j in range(P)])

    def bulk(i, carry):
        g = P * i
        _interleave(*[st for j in range(P) for st in (stage_b(g - P + j), stage_a(g + j))])
        return carry

    lax.fori_loop(1, NG // P, bulk, 0)
    _interleave(*[stage_b(NG - P + j) for j in range(P)])

    stf_ref[...] = jnp.zeros_like(stf_ref)
    stb_ref[...] = jnp.zeros_like(stb_ref)

    def one(n, d, st_ref):
        lanes = slice(d * DK, (d + 1) * DK)
        rows = pl.ds(pl.multiple_of(n * C, C), C)
        st = st_ref[...]
        acc_ref[rows, :] += _dot_nt(qin_ref[rows, lanes], st.astype(BF16))
        st_ref[...] = st * dec_ref[pl.ds(n, 1), :][:, lanes] + u_ref[d, n]

    def step(i, carry):
        one(i, 0, stf_ref)
        one(GLA_NCH - 1 - i, 1, stb_ref)
        return carry

    lax.fori_loop(0, GLA_NCH, step, 0, unroll=8)

    def fin(gi, carry):
        rows = pl.ds(pl.multiple_of(gi * GLA_FIN, GLA_FIN), GLA_FIN)
        o = acc_ref[rows, :]
        o = o * lax.rsqrt(jnp.mean(o * o, axis=-1, keepdims=True) + RMS_EPS) * g_ref[...]
        rg = r_ref[rows, :].astype(F32)
        o_ref[rows, :] = (o * (rg * jax.nn.sigmoid(rg))).astype(BF16)
        return carry

    lax.fori_loop(0, SEQ // GLA_FIN, fin, 0)


def _gla(proj, lr, wd, bd, g):
    i = np.arange(GLA_BULK)
    cs = ((i[:, None] // GLA_C) == (i[None, :] // GLA_C)) & (i[:, None] >= i[None, :])
    cs = jnp.asarray(cs, dtype=F32).astype(BF16)
    v_blk = 1024 // GLA_DV
    return pl.pallas_call(
        _gla_kernel,
        grid=(BATCH, GLA_H),
        in_specs=[
            pl.BlockSpec((SEQ, GLA_DK), lambda b, h: (b, h)),
            pl.BlockSpec((SEQ, GLA_DK), lambda b, h: (b, GLA_H + h)),
            pl.BlockSpec((SEQ, GLA_DV), lambda b, h: (b, v_blk + h)),
            pl.BlockSpec((SEQ, GLA_DV), lambda b, h: (b, 2 * v_blk + h)),
            pl.BlockSpec((SEQ, LANES), lambda b, h: (b, 0)),
            pl.BlockSpec((2 * LANES, 2 * GLA_DK), lambda b, h: (0, h)),
            pl.BlockSpec((1, 2 * GLA_DK), lambda b, h: (0, h)),
            pl.BlockSpec((1, GLA_DV), lambda b, h: (0, 0)),
            pl.BlockSpec((GLA_BULK, GLA_BULK), lambda b, h: (0, 0)),
        ],
        out_specs=pl.BlockSpec((SEQ, GLA_DV), lambda b, h: (b, h)),
        out_shape=jax.ShapeDtypeStruct((T, GLA_H * GLA_DV), BF16),
        scratch_shapes=[
            pltpu.VMEM((SEQ, GLA_DV), F32),
            pltpu.VMEM((SEQ, 2 * GLA_DK), BF16),
            pltpu.VMEM((SEQ, 2 * GLA_DK), BF16),
            pltpu.VMEM((SEQ, 2 * GLA_DK), BF16),
            pltpu.VMEM((GLA_NCH, 2 * GLA_DK), F32),
            pltpu.VMEM((2, GLA_NCH, GLA_DV, GLA_DK), F32),
            pltpu.VMEM((GLA_DV, GLA_DK), F32),
            pltpu.VMEM((GLA_DV, GLA_DK), F32),
        ],
        compiler_params=pltpu.CompilerParams(
            dimension_semantics=("arbitrary", "arbitrary"),
            vmem_limit_bytes=58 * MIB),
        name="gla",
    )(proj, proj, proj, proj, lr, wd, bd, g, cs)


FFT1_S = 16
FFT1_SUB = 2
FFT1_ROWS = FFT_N2 * FFT1_S
FFT2_KB = 8


def _fft1_kernel(x_ref, g_ref, b_ref, w_ref, bias_ref, fbig_ref, cw_ref, sw_ref, o_ref):
    sh = FFT1_S // FFT1_SUB
    rows = FFT_N2 * sh
    res = {}

    def sub(h):
        xv = x_ref[:, h * sh:(h + 1) * sh, :].reshape(rows, D)
        hb = _ln(xv, g_ref[...], b_ref[...]).astype(BF16)
        fn = (_dot(hb, w_ref[...]) + bias_ref[...]).astype(BF16)
        yield
        a = _dot(fbig_ref[...], fn)
        yield
        ar = a[:rows]
        ai = a[rows:]
        cw = jnp.concatenate([cw_ref[h]] * (FN_W // LANES), axis=1)
        sw = jnp.concatenate([sw_ref[h]] * (FN_W // LANES), axis=1)
        res[h] = ((ar * cw + ai * sw).reshape(FFT_N2, sh, FN_W),
                  (ai * cw - ar * sw).reshape(FFT_N2, sh, FN_W))

    _interleave(*[sub(h) for h in range(FFT1_SUB)])
    for ri in range(2):
        o_ref[ri] = jnp.concatenate([res[h][ri] for h in range(FFT1_SUB)], axis=1).astype(BF16)


def _fft1(x4, ln_g, ln_b, w_fn, b_fn, fbig, cwt, swt):
    s = FFT1_S
    const = lambda b, j: (0, 0)
    return pl.pallas_call(
        _fft1_kernel,
        grid=(BATCH, FFT_N1 // s),
        in_specs=[
            pl.BlockSpec((None, FFT_N2, s, D), lambda b, j: (b, 0, j, 0)),
            pl.BlockSpec((1, D), const),
            pl.BlockSpec((1, D), const),
            pl.BlockSpec((D, FN_W), const),
            pl.BlockSpec((1, FN_W), const),
            pl.BlockSpec((2 * FFT1_ROWS // FFT1_SUB, FFT1_ROWS // FFT1_SUB), const),
            pl.BlockSpec((FFT1_SUB, FFT1_ROWS // FFT1_SUB, LANES), lambda b, j: (j, 0, 0)),
            pl.BlockSpec((FFT1_SUB, FFT1_ROWS // FFT1_SUB, LANES), lambda b, j: (j, 0, 0)),
        ],
        out_specs=pl.BlockSpec((None, 2, FFT_N2, s, FN_W), lambda b, j: (b, 0, 0, j, 0)),
        out_shape=jax.ShapeDtypeStruct((BATCH, 2, FFT_N2, FFT_N1, FN_W), BF16),
        compiler_params=pltpu.CompilerParams(
            dimension_semantics=("arbitrary", "arbitrary"),
            vmem_limit_bytes=40 * MIB),
        name="fft_stage1",
    )(x4, ln_g, ln_b, w_fn, b_fn, fbig, cwt, swt)


def _fft2_kernel(d_ref, f2_ref, o_ref):
    f2 = f2_ref[...]
    for kk in range(FFT2_KB):
        z = _dot(f2, jnp.concatenate([d_ref[0, kk], d_ref[1, kk]], axis=0))
        o_ref[0, kk] = z[:FFT_N1].astype(BF16)
        o_ref[1, kk] = z[FFT_N1:].astype(BF16)


def _fft2(dmat, f2):
    kb = FFT2_KB
    blk = (None, 2, kb, FFT_N1, FN_W)
    return pl.pallas_call(
        _fft2_kernel,
        grid=(BATCH, FFT_N2 // kb),
        in_specs=[
            pl.BlockSpec(blk, lambda b, j: (b, 0, j, 0, 0)),
            pl.BlockSpec((2 * FFT_N1, 2 * FFT_N1), lambda b, j: (0, 0)),
        ],
        out_specs=pl.BlockSpec(blk, lambda b, j: (b, 0, j, 0, 0)),
        out_shape=jax.ShapeDtypeStruct((BATCH, 2, FFT_N2, FFT_N1, FN_W), BF16),
        compiler_params=pltpu.CompilerParams(
            dimension_semantics=("arbitrary", "arbitrary")),
        name="fft_stage2",
    )(dmat, f2)


def _dft_tables(merge_tm):
    s = FFT1_S // FFT1_SUB
    n2 = np.arange(FFT_N2, dtype=np.float64)
    n1 = np.arange(FFT_N1, dtype=np.float64)
    th = 2.0 * np.pi * np.outer(n2, n2) / FFT_N2
    f1 = np.stack([np.cos(th), -np.sin(th)]) / math.sqrt(SEQ)
    fbig = np.einsum("rkn,st->rksnt", f1, np.eye(s)).reshape(2 * FFT_N2 * s, FFT_N2 * s)
    tw = 2.0 * np.pi * np.outer(n2, n1) / SEQ
    tw = tw.reshape(FFT_N2, FFT_N1 // s, s).transpose(1, 0, 2).reshape(FFT_N1 // s, FFT_N2 * s)
    cwt = np.broadcast_to(np.cos(tw)[:, :, None], tw.shape + (LANES,))
    swt = np.broadcast_to(np.sin(tw)[:, :, None], tw.shape + (LANES,))
    th1 = 2.0 * np.pi * np.outer(n1, n1) / FFT_N1
    c1, s1 = np.cos(th1), np.sin(th1)
    f2 = np.block([[c1, s1], [-s1, c1]])
    cc = np.arange(FN_GW, dtype=np.float64)
    thc = 2.0 * np.pi * np.outer(cc, cc) / FN_GW
    ccs = np.concatenate([np.cos(thc), np.sin(thc)], axis=0) / math.sqrt(FN_GW)
    k1n = merge_tm // FFT_N2
    r = np.arange(merge_tm)
    perm = np.zeros((merge_tm, merge_tm))
    perm[r, (r % FFT_N2) * k1n + r // FFT_N2] = 1.0
    as32 = lambda a: jnp.asarray(np.ascontiguousarray(a), dtype=F32)
    return (as32(fbig).astype(BF16), as32(cwt), as32(swt), as32(f2).astype(BF16),
            as32(ccs).astype(BF16), as32(perm).astype(BF16))


def _memkv_kernel(m_ref, g_ref, b_ref, w_ref, o_ref):
    mn = _ln(m_ref[...], g_ref[...], b_ref[...]).astype(BF16)
    o_ref[...] = _dot(mn, w_ref[...]).astype(BF16)


def _memkv(mem2, g, b, w):
    return pl.pallas_call(
        _memkv_kernel,
        grid=(BATCH,),
        in_specs=[
            pl.BlockSpec((MEM_LEN, D), lambda i: (i, 0)),
            pl.BlockSpec((1, D), lambda i: (0, 0)),
            pl.BlockSpec((1, D), lambda i: (0, 0)),
            pl.BlockSpec((D, 2 * MQ_W), lambda i: (0, 0)),
        ],
        out_specs=pl.BlockSpec((MEM_LEN, 2 * MQ_W), lambda i: (i, 0)),
        out_shape=jax.ShapeDtypeStruct((BATCH * MEM_LEN, 2 * MQ_W), BF16),
        compiler_params=pltpu.CompilerParams(dimension_semantics=("arbitrary",)),
        name="mem_kv",
    )(mem2, g, b, w)


MERGE_TM = 512
MERGE_K1 = MERGE_TM // FFT_N2


def _pack_bf16_pair(v):
    n = v.shape[1] // 2
    bits = lax.bitcast_convert_type(v.astype(BF16).astype(F32), U32)
    return (bits[:, n:] & jnp.uint32(0xFFFF0000)) | (bits[:, :n] >> 16)


def _unpack_bf16_pair(p):
    lo = lax.bitcast_convert_type(p << 16, F32)
    hi = lax.bitcast_convert_type(p & jnp.uint32(0xFFFF0000), F32)
    return lo, hi


def _merge_kernel(x_ref, og_ref, zr_ref, zi_ref, mq_ref, gt_ref, kv_ref,
                  lng_ref, lnb_ref, wg_ref, ccs_ref, perm_ref, wf_ref, wm_ref, wo_ref, bo_ref,
                  l1g_ref, l1b_ref, wr2_ref, wrh_ref, br_ref,
                  h1_ref, h1p_ref, eidx_ref, topw_ref, cnt_ref):
    tm = MERGE_TM
    y = {}

    def branch_fnet():
        zr = zr_ref[...].reshape(tm, FN_W)
        zi = zi_ref[...].reshape(tm, FN_W)
        ys = []
        for g in range(FN_G):
            sl = slice(g * FN_GW, (g + 1) * FN_GW)
            ys.append(_dot(jnp.concatenate([zr[:, sl], zi[:, sl]], axis=1), ccs_ref[...]))
        yield
        yp = _dot(perm_ref[...], jnp.concatenate(ys, axis=1).astype(BF16))
        yield
        y["fnet"] = _dot(yp.astype(BF16), wf_ref[...])

    def branch_mem():
        heads = [slice(hd * MEM_HD, (hd + 1) * MEM_HD) for hd in range(MEM_H)]
        ss = [_dot_nt(mq_ref[:, sl], kv_ref[:, sl]) for sl in heads]
        yield
        oms = []
        for hd, s in enumerate(ss):
            s = s * (MEM_HD ** -0.5)
            s = s - jnp.max(s, axis=-1, keepdims=True)
            p = jnp.exp(s)
            p = p * (1.0 / jnp.sum(p, axis=-1, keepdims=True))
            oms.append(_dot(p.astype(BF16),
                            kv_ref[:, MQ_W + hd * MEM_HD:MQ_W + (hd + 1) * MEM_HD]))
        yield
        y["mem"] = _dot(jnp.concatenate(oms, axis=1).astype(BF16), wm_ref[...])

    def branch_gla():
        y["gla"] = _dot(og_ref[...], wg_ref[...])
        yield

    _interleave(branch_fnet(), branch_mem(), branch_gla())

    def gate(c):
        return 0.5 + 0.5 * jnp.tanh(0.5 * gt_ref[:, c * D:(c + 1) * D].astype(F32))

    merged = gate(0) * y["gla"] + gate(1) * y["fnet"] + gate(2) * y["mem"]
    mix = _dot(merged.astype(BF16), wo_ref[...]) + bo_ref[...]
    h = _ln(x_ref[...], lng_ref[...], lnb_ref[...])
    h1 = _ln(DN_ALPHA * h + mix, l1g_ref[...], l1b_ref[...])
    h1_ref[...] = h1
    h1p_ref[...] = _pack_bf16_pair(h1)

    h_hi, h_lo = _split_bf16(h1)
    d2 = _dot(h_hi, wr2_ref[...])
    l = d2[:, :LANES] + d2[:, LANES:] + _dot(h_lo, wrh_ref[...]) + br_ref[...]
    lane = lax.broadcasted_iota(I32, (tm, LANES), 1)
    vals, idxs = [], []
    for _ in range(TOP_K):
        m = jnp.max(l, axis=-1, keepdims=True)
        idx = jnp.min(jnp.where(l == m, lane, LANES), axis=-1, keepdims=True)
        vals.append(m)
        idxs.append(idx)
        l = jnp.where(lane == idx, -jnp.inf, l)
    es = [jnp.exp(v - vals[0]) for v in vals]
    den = es[0] + es[1] + es[2] + es[3]
    eo = jnp.zeros((tm, LANES), I32)
    wo = jnp.zeros((tm, LANES), F32)
    chosen = jnp.zeros((tm, LANES), F32)
    for k in range(TOP_K):
        eo = jnp.where(lane == k, idxs[k], eo)
        wo = jnp.where(lane == k, es[k] / den, wo)
        chosen = chosen + jnp.where(lane == idxs[k], 1.0, 0.0)
    eidx_ref[...] = eo
    topw_ref[...] = wo

    @pl.when(pl.program_id(0) == 0)
    def _():
        cnt_ref[...] = jnp.zeros_like(cnt_ref)

    cnt_ref[...] += jnp.broadcast_to(jnp.sum(chosen, axis=0, keepdims=True), cnt_ref.shape)


def _merge(x2, og, z, mq, gates, kv, lng, lnb, wg, ccs, perm, wf, wm, wo, bo, l1g, l1b,
           wr2, wrh, br):
    tm = MERGE_TM
    per_b = SEQ // tm
    row = lambda i: (i, 0)
    const = lambda i: (0, 0)
    zblk = (None, None, FFT_N2, MERGE_K1, FN_W)
    outs = (
        jax.ShapeDtypeStruct((T, D), F32),
        jax.ShapeDtypeStruct((T, D // 2), U32),
        jax.ShapeDtypeStruct((T, LANES), I32),
        jax.ShapeDtypeStruct((T, LANES), F32),
        jax.ShapeDtypeStruct((8, LANES), F32),
    )
    return pl.pallas_call(
        _merge_kernel,
        grid=(T // tm,),
        in_specs=[
            pl.BlockSpec((tm, D), row),
            pl.BlockSpec((tm, D), row),
            pl.BlockSpec(zblk, lambda i: (i // per_b, 0, 0, i % per_b, 0)),
            pl.BlockSpec(zblk, lambda i: (i // per_b, 1, 0, i % per_b, 0)),
            pl.BlockSpec((tm, MQ_W), row),
            pl.BlockSpec((tm, 3 * D), lambda i: (i, 1)),
            pl.BlockSpec((MEM_LEN, 2 * MQ_W), lambda i: (i // per_b, 0)),
            pl.BlockSpec((1, D), const), pl.BlockSpec((1, D), const),
            pl.BlockSpec((D, D), const),
            pl.BlockSpec((2 * FN_GW, FN_GW), const),
            pl.BlockSpec((tm, tm), const),
            pl.BlockSpec((FN_W, D), const),
            pl.BlockSpec((MQ_W, D), const),
            pl.BlockSpec((D, D), const),
            pl.BlockSpec((1, D), const),
            pl.BlockSpec((1, D), const), pl.BlockSpec((1, D), const),
            pl.BlockSpec((D, 2 * LANES), const),
            pl.BlockSpec((D, LANES), const),
            pl.BlockSpec((1, LANES), const),
        ],
        out_specs=[
            pl.BlockSpec((tm, D), row),
            pl.BlockSpec((tm, D // 2), row),
            pl.BlockSpec((tm, LANES), row),
            pl.BlockSpec((tm, LANES), row),
            pl.BlockSpec((8, LANES), const),
        ],
        out_shape=outs,
        compiler_params=pltpu.CompilerParams(
            dimension_semantics=("arbitrary",),
            vmem_limit_bytes=58 * MIB),
        name="merge_ln1_router",
    )(x2, og, z, z, mq, gates, kv, lng, lnb, wg, ccs, perm, wf, wm, wo, bo, l1g, l1b,
      wr2, wrh, br)


PLAN_TP = 1024


def _expert_onehots(e, lane):
    onehots = [lane == e[:, k:k + 1] for k in range(TOP_K)]
    mf = jnp.zeros(lane.shape, F32)
    for oh in onehots:
        mf = mf + jnp.where(oh, 1.0, 0.0)
    return onehots, mf


def _plan_kernel(e_ref, tot_ref, dest_ref, cnt_ref, off_ref):
    i = pl.program_id(0)
    tp = PLAN_TP
    lane = lax.broadcasted_iota(I32, (tp, LANES), 1)
    onehots, mf = _expert_onehots(e_ref[...], lane)

    @pl.when(i == 0)
    def _():
        tot = tot_ref[0:1, :]
        padded = jnp.floor((tot + (MOE_BM - 1)) * (1.0 / MOE_BM)) * MOE_BM
        lane1 = lax.broadcasted_iota(I32, (1, LANES), 1)
        inc = padded
        for s in (1, 2, 4, 8, 16, 32, 64):
            inc = inc + jnp.where(lane1 >= s, pltpu.roll(inc, s, 1), 0.0)
        off_ref[...] = inc - padded
        cnt_ref[...] = jnp.zeros_like(cnt_ref)

    ri = lax.broadcasted_iota(I32, (tp, tp), 0)
    ci = lax.broadcasted_iota(I32, (tp, tp), 1)
    ltri = jnp.where(ri > ci, 1.0, 0.0).astype(BF16)
    rank = _dot(ltri, mf.astype(BF16)) + cnt_ref[...] + off_ref[...]
    out = jnp.zeros((tp, LANES), I32)
    for k in range(TOP_K):
        dk = jnp.sum(jnp.where(onehots[k], rank, 0.0), axis=-1, keepdims=True)
        out = jnp.where(lane == k, dk.astype(I32), out)
    dest_ref[...] = out
    cnt_ref[...] += jnp.sum(mf, axis=0, keepdims=True)


def _plan(eidx, cnt):
    tp = PLAN_TP
    return pl.pallas_call(
        _plan_kernel,
        grid=(T // tp,),
        in_specs=[pl.BlockSpec((tp, LANES), lambda i: (i, 0)),
                  pl.BlockSpec((8, LANES), lambda i: (0, 0))],
        out_specs=pl.BlockSpec((tp, LANES), lambda i: (i, 0)),
        out_shape=jax.ShapeDtypeStruct((T, LANES), I32),
        scratch_shapes=[pltpu.VMEM((1, LANES), F32), pltpu.VMEM((1, LANES), F32)],
        compiler_params=pltpu.CompilerParams(dimension_semantics=("arbitrary",)),
        name="route_plan",
    )(eidx, cnt)


MOE_BM = 512
MOE_NW = A_ROWS // MOE_BM + N_EXP
XS_ROWS = MOE_NW * MOE_BM
MOE_FF_SLICES = 2


def _expert_kernel(we_ref, wb_ref, wv_ref, wfe_ref, wsl_ref, wnx_ref,
                   x_ref, wgu_hbm, bgu_ref, wdn_hbm, bdn_ref, o_ref,
                   wgu_f32, wdn_f32, wgu_bf, wdn_bf, sems):
    w = pl.program_id(0)
    e = we_ref[w]

    def weight_copies(expert, slot):
        return (pltpu.make_async_copy(wgu_hbm.at[expert], wgu_f32.at[slot], sems.at[slot, 0]),
                pltpu.make_async_copy(wdn_hbm.at[expert], wdn_f32.at[slot], sems.at[slot, 1]))

    @pl.when(w == 0)
    def _():
        for cp_ in weight_copies(e, 0):
            cp_.start()

    @pl.when(wfe_ref[w] == 1)
    def _():
        slot = wsl_ref[w]
        for cp_ in weight_copies(e, slot):
            cp_.wait()
        wgu_bf[...] = wgu_f32[slot].astype(BF16)
        wdn_bf[...] = wdn_f32[slot].astype(BF16)
        nxt = wnx_ref[w]

        @pl.when(nxt >= 0)
        def _():
            for cp_ in weight_copies(nxt, 1 - slot):
                cp_.start()

    def ffn(rows):
        xlo, xhi = _unpack_bf16_pair(x_ref[rows, :])
        xb = jnp.concatenate([xlo.astype(BF16), xhi.astype(BF16)], axis=1)
        bgu = bgu_ref[pl.ds(e, 1), :]
        hw = D_FF // MOE_FF_SLICES
        gus = []
        for hf in range(MOE_FF_SLICES):
            gc = slice(hf * hw, (hf + 1) * hw)
            uc = slice(D_FF + hf * hw, D_FF + (hf + 1) * hw)
            gus.append((_dot(xb, wgu_bf[:, gc]) + bgu[:, gc], _dot(xb, wgu_bf[:, uc]) + bgu[:, uc]))
        out = bdn_ref[pl.ds(e, 1), :]
        for hf, (g, u) in enumerate(gus):
            gate = jnp.minimum(g, SW_LIMIT)
            up = jnp.clip(u, -SW_LIMIT, SW_LIMIT)
            act = (up + 1.0) * (gate * jax.nn.sigmoid(SW_ALPHA * gate))
            out = out + _dot(act.astype(BF16), wdn_bf[hf * hw:(hf + 1) * hw, :])
        o_ref[rows, :] = _pack_bf16_pair(out)

    @pl.when(wv_ref[w] == 1)
    def _():
        ffn(slice(None))

    @pl.when(wv_ref[w] == 2)
    def _():
        ffn(slice(0, MOE_BM // 2))


def _experts(meta, xs, w_gu, b_gu, w_down, b_down):
    return pl.pallas_call(
        _expert_kernel,
        grid_spec=pltpu.PrefetchScalarGridSpec(
            num_scalar_prefetch=len(meta),
            grid=(MOE_NW,),
            in_specs=[
                pl.BlockSpec((MOE_BM, D // 2), lambda w, we, wb, *_: (wb[w], 0)),
                pl.BlockSpec(memory_space=pl.ANY),
                pl.BlockSpec((N_EXP, 2 * D_FF), lambda w, *_: (0, 0)),
                pl.BlockSpec(memory_space=pl.ANY),
                pl.BlockSpec((N_EXP, D), lambda w, *_: (0, 0)),
            ],
            out_specs=pl.BlockSpec((MOE_BM, D // 2), lambda w, we, wb, *_: (wb[w], 0)),
            scratch_shapes=[
                pltpu.VMEM((2, D, 2 * D_FF), F32),
                pltpu.VMEM((2, D_FF, D), F32),
                pltpu.VMEM((D, 2 * D_FF), BF16),
                pltpu.VMEM((D_FF, D), BF16),
                pltpu.SemaphoreType.DMA((2, 2)),
            ],
        ),
        out_shape=jax.ShapeDtypeStruct((XS_ROWS, D // 2), U32),
        compiler_params=pltpu.CompilerParams(
            dimension_semantics=("arbitrary",),
            vmem_limit_bytes=56 * MIB),
        name="moe_experts",
    )(*meta, xs, w_gu, b_gu, w_down, b_down)


def _work_items(counts):
    n_e = (counts + MOE_BM - 1) // MOE_BM
    item_end = jnp.cumsum(n_e)
    total = item_end[-1]
    w = jnp.arange(MOE_NW, dtype=I32)
    valid = w < total
    wc = jnp.minimum(w, total - 1)
    e_w = jnp.sum((item_end[None, :] <= wc[:, None]).astype(I32), axis=1)
    e_w = jnp.minimum(e_w, N_EXP - 1)
    rows_here = counts[e_w] - (wc - (item_end - n_e)[e_w]) * MOE_BM
    valid = jnp.where(valid, jnp.where(rows_here <= MOE_BM // 2, 2, 1), 0)
    prev_e = jnp.concatenate([jnp.full((1,), -1, I32), e_w[:-1]])
    fe = (e_w != prev_e).astype(I32)
    slot = (jnp.cumsum(fe) - 1) % 2
    first_at = jnp.where(fe == 1, w, MOE_NW)
    next_first = jnp.concatenate([lax.cummin(first_at, reverse=True)[1:],
                                  jnp.full((1,), MOE_NW, I32)])
    nxt = jnp.where(next_first < MOE_NW, e_w[jnp.minimum(next_first, MOE_NW - 1)], -1)
    return tuple(a.astype(I32) for a in (e_w, wc, valid, fe, slot, nxt))


COMB_TM = 512
SC_CORES = 2
SC_SUBCORES = 16
SC_WORKERS = SC_CORES * SC_SUBCORES
SC_CH = 64
COMB_GROUPS = 4
COMB_TG = T // COMB_GROUPS
SC_ROWS_PER_W = COMB_TG * TOP_K // SC_WORKERS
SC_NCH = SC_ROWS_PER_W // SC_CH


def _sc_gather(table, idx3):
    mesh = plsc.VectorSubcoreMesh(core_axis_name="c", subcore_axis_name="s")

    @functools.partial(
        pl.kernel, mesh=mesh,
        out_type=jax.ShapeDtypeStruct((COMB_TG * TOP_K, D // 2), U32),
        scratch_types=[
            pltpu.VMEM((SC_NCH, SC_CH), I32),
            pltpu.VMEM((2, SC_CH, D // 2), U32),
            pltpu.SemaphoreType.DMA((2,)),
            pltpu.SemaphoreType.DMA((2,)),
        ],
    )
    def k(table_hbm, idx_hbm, out_hbm, idx_v, rows_v, gsem, psem):
        wid = lax.axis_index("s") * SC_CORES + lax.axis_index("c")
        base = wid * SC_ROWS_PER_W
        pltpu.sync_copy(idx_hbm.at[wid], idx_v)

        def gather(j, b):
            return pltpu.make_async_copy(table_hbm.at[idx_v.at[j]], rows_v.at[b], gsem.at[b])

        def put(j, b):
            return pltpu.make_async_copy(rows_v.at[b], out_hbm.at[pl.ds(base + j * SC_CH, SC_CH)],
                                         psem.at[b])

        gather(0, 0).start()

        @pl.loop(0, SC_NCH, step=2)
        def _(j0):
            for b in range(2):
                j = j0 + b

                @pl.when(j + 1 < SC_NCH)
                def _():
                    @pl.when(j >= 1)
                    def _():
                        put(j - 1, 1 - b).wait()
                    gather(j + 1, 1 - b).start()

                gather(j, b).wait()
                put(j, b).start()

        put(SC_NCH - 2, 0).wait()
        put(SC_NCH - 1, 1).wait()

    return k(table, idx3)


SCD_TOK_PER_W = T // SC_WORKERS
SCD_NCH = SCD_TOK_PER_W // SC_CH


def _sc_dispatch(h1p, idx4):
    mesh = plsc.VectorSubcoreMesh(core_axis_name="c", subcore_axis_name="s")

    @functools.partial(
        pl.kernel, mesh=mesh,
        out_type=jax.ShapeDtypeStruct((XS_ROWS, D // 2), U32),
        scratch_types=[
            pltpu.VMEM((SCD_NCH * TOP_K, SC_CH), I32),
            pltpu.VMEM((2, SC_CH, D // 2), U32),
            pltpu.SemaphoreType.DMA((2,)),
            pltpu.SemaphoreType.DMA((2,)),
        ],
    )
    def k(h_hbm, idx_hbm, xs_hbm, idx_v, rows_v, gsem, psem):
        wid = lax.axis_index("s") * SC_CORES + lax.axis_index("c")
        base = wid * SCD_TOK_PER_W
        pltpu.sync_copy(idx_hbm.at[wid], idx_v)

        def get(c, b):
            return pltpu.make_async_copy(h_hbm.at[pl.ds(base + c * SC_CH, SC_CH)], rows_v.at[b],
                                         gsem.at[b])

        def puts(c, b):
            return [pltpu.make_async_copy(rows_v.at[b], xs_hbm.at[idx_v.at[c * TOP_K + kk]],
                                          psem.at[b]) for kk in range(TOP_K)]

        get(0, 0).start()

        @pl.loop(0, SCD_NCH, step=2)
        def _(c0):
            for b in range(2):
                c = c0 + b

                @pl.when(c + 1 < SCD_NCH)
                def _():
                    @pl.when(c >= 1)
                    def _():
                        for cp_ in puts(c - 1, 1 - b):
                            cp_.wait()
                    get(c + 1, 1 - b).start()

                get(c, b).wait()
                for cp_ in puts(c, b):
                    cp_.start()

        for cp_ in puts(SCD_NCH - 2, 0) + puts(SCD_NCH - 1, 1):
            cp_.wait()

    return k(h1p, idx4)


def _combine_dense_kernel(g_ref, h1_ref, tw_ref, lg_ref, lb_ref, o_ref):
    tw = tw_ref[...]
    ylo = jnp.zeros((COMB_TM, D // 2), F32)
    yhi = jnp.zeros((COMB_TM, D // 2), F32)
    for k in range(TOP_K):
        lo, hi = _unpack_bf16_pair(g_ref[k])
        wk = tw[:, k:k + 1]
        ylo = ylo + lo * wk
        yhi = yhi + hi * wk
    ff = jnp.concatenate([ylo, yhi], axis=1)
    o_ref[...] = _ln(DN_ALPHA * h1_ref[...] + ff, lg_ref[...], lb_ref[...])


def _combine_dense(g4, h1, topw, g, b, group):
    tm = COMB_TM
    t0 = group * (COMB_TG // tm)
    return pl.pallas_call(
        _combine_dense_kernel,
        grid=(COMB_TG // tm,),
        in_specs=[
            pl.BlockSpec((TOP_K, tm, D // 2), lambda i: (0, i, 0)),
            pl.BlockSpec((tm, D), lambda i: (t0 + i, 0)),
            pl.BlockSpec((tm, LANES), lambda i: (t0 + i, 0)),
            pl.BlockSpec((1, D), lambda i: (0, 0)),
            pl.BlockSpec((1, D), lambda i: (0, 0)),
        ],
        out_specs=pl.BlockSpec((tm, D), lambda i: (t0 + i, 0)),
        out_shape=jax.ShapeDtypeStruct((T, D), F32),
        input_output_aliases={1: 0},
        compiler_params=pltpu.CompilerParams(
            dimension_semantics=("arbitrary",),
            vmem_limit_bytes=40 * MIB),
        name="moe_combine_dense_ln2",
    )(g4, h1, topw, g, b)


def _pad_cols(a, n):
    return jnp.pad(a, ((0, 0), (0, n - a.shape[1])))


def kernel(x, mem, ln_in_g, ln_in_b, ln_mem_g, ln_mem_b, w_in, b_in, w_decay_f, b_decay_f,
           w_decay_b, b_decay_b, gla_norm_g, w_br_gla, w_br_fnet, w_br_mem, w_mem_kv, w_out,
           b_out, ln1_g, ln1_b, w_router, b_router, w_gu, b_gu, w_down, b_down, ln2_g, ln2_b):
    assert x.shape == (BATCH, SEQ, D) and w_in.shape[0] == 1
    row = lambda a: a.reshape(1, -1)
    x2 = x.reshape(T, D)
    w_in0, b_in0 = w_in[0], b_in[0]
    c_lr, c_fn, c_mq, c_gt = 3072, 3072 + 2 * GLA_LR, 3104 + FN_W, 3104 + FN_W + MQ_W
    w_main = jnp.concatenate([w_in0[:, :c_lr], w_in0[:, c_gt:]], axis=1).astype(BF16)
    b_main = row(jnp.concatenate([b_in0[:c_lr], b_in0[c_gt:]]))
    w_lr = _pad_cols(w_in0[:, c_lr:c_fn], LANES).astype(BF16)
    b_lr = _pad_cols(row(b_in0[c_lr:c_fn]), LANES)
    w_mq = w_in0[:, c_mq:c_gt].astype(BF16)
    b_mq = row(b_in0[c_mq:c_gt])
    w_fn = w_in0[:, c_fn:c_mq].astype(BF16)
    b_fn = row(b_in0[c_fn:c_mq])
    lng, lnb = row(ln_in_g), row(ln_in_b)

    proj, mq, lr = _inproj(x2, lng, lnb, w_main, b_main, w_mq, b_mq, w_lr, b_lr)

    zpad = jnp.zeros((LANES - 2 * GLA_LR, GLA_H * GLA_DK), F32)
    zlr = jnp.zeros((GLA_LR, GLA_H * GLA_DK), F32)
    wdf = jnp.concatenate([w_decay_f[0], zlr, zpad], axis=0).reshape(LANES, GLA_H, GLA_DK)
    wdb = jnp.concatenate([zlr, w_decay_b[0], zpad], axis=0).reshape(LANES, GLA_H, GLA_DK)
    wd = jnp.concatenate([wdf, wdb], axis=2).reshape(LANES, GLA_H * 2 * GLA_DK).astype(BF16)
    wd = jnp.concatenate([wd, wd], axis=0)
    bd = jnp.concatenate([b_decay_f[0].reshape(GLA_H, GLA_DK),
                          b_decay_b[0].reshape(GLA_H, GLA_DK)], axis=1).reshape(1, -1)
    og = _gla(proj, lr, wd, bd, row(gla_norm_g[0]))

    fbig, cwt, swt, f2, ccs, perm = _dft_tables(MERGE_TM)
    x4 = x.reshape(BATCH, FFT_N2, FFT_N1, D)
    z = _fft2(_fft1(x4, lng, lnb, w_fn, b_fn, fbig, cwt, swt), f2)

    kv = _memkv(mem.reshape(BATCH * MEM_LEN, D), row(ln_mem_g), row(ln_mem_b),
                w_mem_kv[0].astype(BF16))

    w_r = _pad_cols(w_router[0], LANES)
    wr_hi = w_r.astype(BF16)
    wr_lo = (w_r - wr_hi.astype(F32)).astype(BF16)
    b_r = jnp.concatenate([row(b_router[0]),
                           jnp.full((1, LANES - N_EXP), NEG_BIG, F32)], axis=1)
    h1, h1p, eidx, topw, cnt = _merge(
        x2, og, z, mq, proj, kv, lng, lnb,
        w_br_gla[0].astype(BF16), ccs, perm, w_br_fnet[0].astype(BF16),
        w_br_mem[0].astype(BF16), w_out[0].astype(BF16), row(b_out[0]),
        row(ln1_g[0]), row(ln1_b[0]), jnp.concatenate([wr_hi, wr_lo], axis=1), wr_hi, b_r)

    dest = _plan(eidx, cnt)
    counts = cnt[0, :N_EXP].astype(I32)
    dest_k = dest[:, :TOP_K]
    idx4 = dest_k.reshape(SC_WORKERS, SCD_NCH, SC_CH, TOP_K).transpose(0, 1, 3, 2)
    xs = _sc_dispatch(h1p, idx4.reshape(SC_WORKERS, SCD_NCH * TOP_K, SC_CH))
    ys = _experts(_work_items(counts), xs, w_gu[0], b_gu[0], w_down[0], b_down[0])
    out = h1
    for grp in range(COMB_GROUPS):
        dest_g = dest_k[grp * COMB_TG:(grp + 1) * COMB_TG].T.reshape(SC_WORKERS, SC_NCH, SC_CH)
        g4 = _sc_gather(ys, dest_g).reshape(TOP_K, COMB_TG, D // 2)
        out = _combine_dense(g4, out, topw, row(ln2_g[0]), row(ln2_b[0]), grp)
    return out.reshape(BATCH, SEQ, D)
```

```python
import functools
import math

import numpy as np
import jax
import jax.numpy as jnp
from jax import lax
from jax.experimental import pallas as pl
from jax.experimental.pallas import tpu as pltpu
from jax.experimental.pallas import tpu_sc as plsc

F32 = jnp.float32
BF16 = jnp.bfloat16
I32 = jnp.int32
U32 = jnp.uint32

D = 1024
BATCH = 4
SEQ = 4096
T = BATCH * SEQ
GLA_H = 4
GLA_DK = 128
GLA_DV = 256
GLA_LR = 16
GLA_TAU = 16.0
GLA_C = 64
FN_G = 4
FN_GW = 128
FN_W = 512
MEM_LEN = 256
MEM_H = 4
MEM_HD = 128
MQ_W = 512
N_EXP = 32
TOP_K = 4
D_FF = 1024
SW_LIMIT = 7.0
SW_ALPHA = 1.702
LN_EPS = 1e-5
RMS_EPS = 1e-6
DN_ALPHA = 2.0 ** 0.25
A_ROWS = T * TOP_K

FFT_N1 = 128
FFT_N2 = 32

LANES = 128
NEG_BIG = -1e30
MIB = 1024 * 1024


def _ln(x, g, b):
    mu = jnp.mean(x, axis=-1, keepdims=True)
    xc = x - mu
    var = jnp.mean(xc * xc, axis=-1, keepdims=True)
    return xc * lax.rsqrt(var + LN_EPS) * g + b


def _dot(a, b):
    return jnp.dot(a, b, preferred_element_type=F32)


def _dot_nt(a, b):
    return lax.dot_general(a, b, (((1,), (1,)), ((), ())), preferred_element_type=F32)


def _dot_tn(a, b):
    return lax.dot_general(a, b, (((0,), (0,)), ((), ())), preferred_element_type=F32)


def _interleave(*stages):
    live = list(stages)
    while live:
        for st in list(live):
            try:
                next(st)
            except StopIteration:
                live.remove(st)


def _split_bf16(a):
    hi = a.astype(BF16)
    return hi, (a - hi.astype(F32)).astype(BF16)


INPROJ_TM = 1024
INPROJ_TN = 3072


PROJ_W = 6 * 1024


def _inproj_kernel(x_ref, g_ref, b_ref, w_ref, bias_ref, wmq_ref, bmq_ref, wlr_ref, blr_ref,
                   proj_ref, mq_ref, lr_ref, hb_ref):
    @pl.when(pl.program_id(1) == 0)
    def _():
        hb = _ln(x_ref[...], g_ref[...], b_ref[...]).astype(BF16)
        hb_ref[...] = hb
        lr_ref[...] = _dot(hb, wlr_ref[...]) + blr_ref[...]
        mq_ref[...] = (_dot(hb, wmq_ref[...]) + bmq_ref[...]).astype(BF16)

    proj_ref[...] = (_dot(hb_ref[...], w_ref[...]) + bias_ref[...]).astype(BF16)


def _inproj(x2, ln_g, ln_b, w_main, b_main, w_mq, b_mq, w_lr, b_lr):
    tm, tn = INPROJ_TM, INPROJ_TN
    nj = PROJ_W // tn
    row = lambda i, j: (i, 0)
    const = lambda i, j: (0, 0)
    outs = (
        jax.ShapeDtypeStruct((T, PROJ_W), BF16),
        jax.ShapeDtypeStruct((T, MQ_W), BF16),
        jax.ShapeDtypeStruct((T, LANES), F32),
    )
    return pl.pallas_call(
        _inproj_kernel,
        grid=(T // tm, nj),
        in_specs=[
            pl.BlockSpec((tm, D), row),
            pl.BlockSpec((1, D), const),
            pl.BlockSpec((1, D), const),
            pl.BlockSpec((D, tn), lambda i, j: (0, j)),
            pl.BlockSpec((1, tn), lambda i, j: (0, j)),
            pl.BlockSpec((D, MQ_W), const),
            pl.BlockSpec((1, MQ_W), const),
            pl.BlockSpec((D, LANES), const),
            pl.BlockSpec((1, LANES), const),
        ],
        out_specs=[
            pl.BlockSpec((tm, tn), lambda i, j: (i, j)),
            pl.BlockSpec((tm, MQ_W), row),
            pl.BlockSpec((tm, LANES), row),
        ],
        out_shape=outs,
        scratch_shapes=[pltpu.VMEM((tm, D), BF16)],
        compiler_params=pltpu.CompilerParams(
            dimension_semantics=("arbitrary", "arbitrary"),
            vmem_limit_bytes=48 * MIB),
        name="ln_inproj",
    )(x2, ln_g, ln_b, w_main, b_main, w_mq, b_mq, w_lr, b_lr)


GLA_BULK = 256
GLA_FIN = 512
GLA_NCH = SEQ // GLA_C
GLA_CPB = GLA_BULK // GLA_C
GLA_PIPE = 4


def _gla_kernel(q_ref, k_ref, v_ref, r_ref, lr_ref, wd_ref, bd_ref, g_ref, cs_ref, o_ref,
                acc_ref, qin_ref, kin_ref, kst_ref, dec_ref, u_ref, stf_ref, stb_ref):
    C = GLA_C
    G = GLA_BULK
    DK = GLA_DK
    NG = SEQ // G
    scale = DK ** -0.5
    ii = lax.broadcasted_iota(I32, (G, G), 0)
    jj = lax.broadcasted_iota(I32, (G, G), 1)
    same = (ii // C) == (jj // C)
    lower = jnp.logical_and(same, ii >= jj)
    upper = jnp.logical_and(same, ii <= jj)
    is_fwd = lax.broadcasted_iota(I32, (G, 2 * DK), 1) < DK
    chunk_of_row = lax.broadcasted_iota(I32, (G, DK), 0) // C

    def stage_a(gi):
        rows = pl.ds(pl.multiple_of(gi * G, G), G)
        z = _dot(jnp.concatenate(_split_bf16(lr_ref[rows, :]), axis=1), wd_ref[...]) + bd_ref[...]
        yield
        la = -(jnp.maximum(-z, 0.0) + jnp.log(1.0 + jnp.exp(-jnp.abs(z)))) * (1.0 / GLA_TAU)
        la_hi, la_lo = _split_bf16(la)
        pre2 = _dot(cs_ref[...], jnp.concatenate([la_hi, la_lo], axis=1))
        yield
        pre = pre2[:, :2 * DK] + pre2[:, 2 * DK:]
        blast = jnp.concatenate(
            [jnp.broadcast_to(pre[ci * C + C - 1:ci * C + C, :], (C, 2 * DK))
             for ci in range(GLA_CPB)], axis=0)
        b = jnp.where(is_fwd, pre, blast - pre + la)
        qf32 = q_ref[rows, :].astype(F32)
        kf32 = k_ref[rows, :].astype(F32)
        q2 = jnp.concatenate([qf32, qf32], axis=1)
        k2 = jnp.concatenate([kf32, kf32], axis=1)
        qin_ref[rows, :] = (q2 * (scale * jnp.exp(b))).astype(BF16)
        kin_ref[rows, :] = (k2 * jnp.exp(-b)).astype(BF16)
        kst_ref[rows, :] = (k2 * jnp.exp(blast - b)).astype(BF16)
        dec = jnp.exp(blast)
        for ci in range(GLA_CPB):
            dec_ref[pl.ds(gi * GLA_CPB + ci, 1), :] = dec[ci * C:ci * C + 1, :]

    def stage_b(gi):
        rows = pl.ds(pl.multiple_of(gi * G, G), G)
        qi = qin_ref[rows, :]
        ki = kin_ref[rows, :]
        ks = kst_ref[rows, :]
        vb = v_ref[rows, :]
        att_f = _dot_nt(qi[:, :DK], ki[:, :DK])
        att_b = _dot_nt(qi[:, DK:], ki[:, DK:])
        yield
        att = jnp.where(lower, att_f, 0.0) + jnp.where(upper, att_b, 0.0)
        acc_ref[rows, :] = _dot(att.astype(BF16), vb)
        yield
        ksb = jnp.concatenate(
            [jnp.where(chunk_of_row == ci, ks[:, d * DK:(d + 1) * DK], jnp.zeros((G, DK), BF16))
             for d in range(2) for ci in range(GLA_CPB)], axis=1)
        u = _dot_tn(vb, ksb)
        for d in range(2):
            for ci in range(GLA_CPB):
                col = (d * GLA_CPB + ci) * DK
                u_ref[d, gi * GLA_CPB + ci] = u[:, col:col + DK]

    P = GLA_PIPE
    _interleave(*[stage_a(j) for j in range(P)])

    def bulk(i, carry):
        g = P * i
        _interleave(*[st for j in range(P) for st in (stage_b(g - P + j), stage_a(g + j))])
        return carry

    lax.fori_loop(1, NG // P, bulk, 0)
    _interleave(*[stage_b(NG - P + j) for j in range(P)])

    stf_ref[...] = jnp.zeros_like(stf_ref)
    stb_ref[...] = jnp.zeros_like(stb_ref)

    def one(n, d, st_ref):
        lanes = slice(d * DK, (d + 1) * DK)
        rows = pl.ds(pl.multiple_of(n * C, C), C)
        st = st_ref[...]
        acc_ref[rows, :] += _dot_nt(qin_ref[rows, lanes], st.astype(BF16))
        st_ref[...] = st * dec_ref[pl.ds(n, 1), :][:, lanes] + u_ref[d, n]

    def step(i, carry):
        one(i, 0, stf_ref)
        one(GLA_NCH - 1 - i, 1, stb_ref)
        return carry

    lax.fori_loop(0, GLA_NCH, step, 0, unroll=8)

    def fin(gi, carry):
        rows = pl.ds(pl.multiple_of(gi * GLA_FIN, GLA_FIN), GLA_FIN)
        o = acc_ref[rows, :]
        o = o * lax.rsqrt(jnp.mean(o * o, axis=-1, keepdims=True) + RMS_EPS) * g_ref[...]
        rg = r_ref[rows, :].astype(F32)
        o_ref[rows, :] = (o * (rg * jax.nn.sigmoid(rg))).astype(BF16)
        return carry

    lax.fori_loop(0, SEQ // GLA_FIN, fin, 0)


def _gla(proj, lr, wd, bd, g):
    i = np.arange(GLA_BULK)
    cs = ((i[:, None] // GLA_C) == (i[None, :] // GLA_C)) & (i[:, None] >= i[None, :])
    cs = jnp.asarray(cs, dtype=F32).astype(BF16)
    v_blk = 1024 // GLA_DV
    return pl.pallas_call(
        _gla_kernel,
        grid=(BATCH, GLA_H),
        in_specs=[
            pl.BlockSpec((SEQ, GLA_DK), lambda b, h: (b, h)),
            pl.BlockSpec((SEQ, GLA_DK), lambda b, h: (b, GLA_H + h)),
            pl.BlockSpec((SEQ, GLA_DV), lambda b, h: (b, v_blk + h)),
            pl.BlockSpec((SEQ, GLA_DV), lambda b, h: (b, 2 * v_blk + h)),
            pl.BlockSpec((SEQ, LANES), lambda b, h: (b, 0)),
            pl.BlockSpec((2 * LANES, 2 * GLA_DK), lambda b, h: (0, h)),
            pl.BlockSpec((1, 2 * GLA_DK), lambda b, h: (0, h)),
            pl.BlockSpec((1, GLA_DV), lambda b, h: (0, 0)),
            pl.BlockSpec((GLA_BULK, GLA_BULK), lambda b, h: (0, 0)),
        ],
        out_specs=pl.BlockSpec((SEQ, GLA_DV), lambda b, h: (b, h)),
        out_shape=jax.ShapeDtypeStruct((T, GLA_H * GLA_DV), BF16),
        scratch_shapes=[
            pltpu.VMEM((SEQ, GLA_DV), F32),
            pltpu.VMEM((SEQ, 2 * GLA_DK), BF16),
            pltpu.VMEM((SEQ, 2 * GLA_DK), BF16),
            pltpu.VMEM((SEQ, 2 * GLA_DK), BF16),
            pltpu.VMEM((GLA_NCH, 2 * GLA_DK), F32),
            pltpu.VMEM((2, GLA_NCH, GLA_DV, GLA_DK), F32),
            pltpu.VMEM((GLA_DV, GLA_DK), F32),
            pltpu.VMEM((GLA_DV, GLA_DK), F32),
        ],
        compiler_params=pltpu.CompilerParams(
            dimension_semantics=("arbitrary", "arbitrary"),
            vmem_limit_bytes=58 * MIB),
        name="gla",
    )(proj, proj, proj, proj, lr, wd, bd, g, cs)


FFT1_S = 16
FFT1_SUB = 2
FFT1_ROWS = FFT_N2 * FFT1_S
FFT2_KB = 8


def _fft1_kernel(x_ref, g_ref, b_ref, w_ref, bias_ref, fbig_ref, cw_ref, sw_ref, o_ref):
    sh = FFT1_S // FFT1_SUB
    rows = FFT_N2 * sh
    res = {}

    def sub(h):
        xv = x_ref[:, h * sh:(h + 1) * sh, :].reshape(rows, D)
        hb = _ln(xv, g_ref[...], b_ref[...]).astype(BF16)
        fn = (_dot(hb, w_ref[...]) + bias_ref[...]).astype(BF16)
        yield
        a = _dot(fbig_ref[...], fn)
        yield
        ar = a[:rows]
        ai = a[rows:]
        cw = jnp.concatenate([cw_ref[h]] * (FN_W // LANES), axis=1)
        sw = jnp.concatenate([sw_ref[h]] * (FN_W // LANES), axis=1)
        res[h] = ((ar * cw + ai * sw).reshape(FFT_N2, sh, FN_W),
                  (ai * cw - ar * sw).reshape(FFT_N2, sh, FN_W))

    _interleave(*[sub(h) for h in range(FFT1_SUB)])
    for ri in range(2):
        o_ref[ri] = jnp.concatenate([res[h][ri] for h in range(FFT1_SUB)], axis=1).astype(BF16)


def _fft1(x4, ln_g, ln_b, w_fn, b_fn, fbig, cwt, swt):
    s = FFT1_S
    const = lambda b, j: (0, 0)
    return pl.pallas_call(
        _fft1_kernel,
        grid=(BATCH, FFT_N1 // s),
        in_specs=[
            pl.BlockSpec((None, FFT_N2, s, D), lambda b, j: (b, 0, j, 0)),
            pl.BlockSpec((1, D), const),
            pl.BlockSpec((1, D), const),
            pl.BlockSpec((D, FN_W), const),
            pl.BlockSpec((1, FN_W), const),
            pl.BlockSpec((2 * FFT1_ROWS // FFT1_SUB, FFT1_ROWS // FFT1_SUB), const),
            pl.BlockSpec((FFT1_SUB, FFT1_ROWS // FFT1_SUB, LANES), lambda b, j: (j, 0, 0)),
            pl.BlockSpec((FFT1_SUB, FFT1_ROWS // FFT1_SUB, LANES), lambda b, j: (j, 0, 0)),
        ],
        out_specs=pl.BlockSpec((None, 2, FFT_N2, s, FN_W), lambda b, j: (b, 0, 0, j, 0)),
        out_shape=jax.ShapeDtypeStruct((BATCH, 2, FFT_N2, FFT_N1, FN_W), BF16),
        compiler_params=pltpu.CompilerParams(
            dimension_semantics=("arbitrary", "arbitrary"),
            vmem_limit_bytes=40 * MIB),
        name="fft_stage1",
    )(x4, ln_g, ln_b, w_fn, b_fn, fbig, cwt, swt)


def _fft2_kernel(d_ref, f2_ref, o_ref):
    f2 = f2_ref[...]
    for kk in range(FFT2_KB):
        z = _dot(f2, jnp.concatenate([d_ref[0, kk], d_ref[1, kk]], axis=0))
        o_ref[0, kk] = z[:FFT_N1].astype(BF16)
        o_ref[1, kk] = z[FFT_N1:].astype(BF16)


def _fft2(dmat, f2):
    kb = FFT2_KB
    blk = (None, 2, kb, FFT_N1, FN_W)
    return pl.pallas_call(
        _fft2_kernel,
        grid=(BATCH, FFT_N2 // kb),
        in_specs=[
            pl.BlockSpec(blk, lambda b, j: (b, 0, j, 0, 0)),
            pl.BlockSpec((2 * FFT_N1, 2 * FFT_N1), lambda b, j: (0, 0)),
        ],
        out_specs=pl.BlockSpec(blk, lambda b, j: (b, 0, j, 0, 0)),
        out_shape=jax.ShapeDtypeStruct((BATCH, 2, FFT_N2, FFT_N1, FN_W), BF16),
        compiler_params=pltpu.CompilerParams(
            dimension_semantics=("arbitrary", "arbitrary")),
        name="fft_stage2",
    )(dmat, f2)


def _dft_tables(merge_tm):
    s = FFT1_S // FFT1_SUB
    n2 = np.arange(FFT_N2, dtype=np.float64)
    n1 = np.arange(FFT_N1, dtype=np.float64)
    th = 2.0 * np.pi * np.outer(n2, n2) / FFT_N2
    f1 = np.stack([np.cos(th), -np.sin(th)]) / math.sqrt(SEQ)
    fbig = np.einsum("rkn,st->rksnt", f1, np.eye(s)).reshape(2 * FFT_N2 * s, FFT_N2 * s)
    tw = 2.0 * np.pi * np.outer(n2, n1) / SEQ
    tw = tw.reshape(FFT_N2, FFT_N1 // s, s).transpose(1, 0, 2).reshape(FFT_N1 // s, FFT_N2 * s)
    cwt = np.broadcast_to(np.cos(tw)[:, :, None], tw.shape + (LANES,))
    swt = np.broadcast_to(np.sin(tw)[:, :, None], tw.shape + (LANES,))
    th1 = 2.0 * np.pi * np.outer(n1, n1) / FFT_N1
    c1, s1 = np.cos(th1), np.sin(th1)
    f2 = np.block([[c1, s1], [-s1, c1]])
    cc = np.arange(FN_GW, dtype=np.float64)
    thc = 2.0 * np.pi * np.outer(cc, cc) / FN_GW
    ccs = np.concatenate([np.cos(thc), np.sin(thc)], axis=0) / math.sqrt(FN_GW)
    k1n = merge_tm // FFT_N2
    r = np.arange(merge_tm)
    perm = np.zeros((merge_tm, merge_tm))
    perm[r, (r % FFT_N2) * k1n + r // FFT_N2] = 1.0
    as32 = lambda a: jnp.asarray(np.ascontiguousarray(a), dtype=F32)
    return (as32(fbig).astype(BF16), as32(cwt), as32(swt), as32(f2).astype(BF16),
            as32(ccs).astype(BF16), as32(perm).astype(BF16))


def _memkv_kernel(m_ref, g_ref, b_ref, w_ref, o_ref):
    mn = _ln(m_ref[...], g_ref[...], b_ref[...]).astype(BF16)
    o_ref[...] = _dot(mn, w_ref[...]).astype(BF16)


def _memkv(mem2, g, b, w):
    return pl.pallas_call(
        _memkv_kernel,
        grid=(BATCH,),
        in_specs=[
            pl.BlockSpec((MEM_LEN, D), lambda i: (i, 0)),
            pl.BlockSpec((1, D), lambda i: (0, 0)),
            pl.BlockSpec((1, D), lambda i: (0, 0)),
            pl.BlockSpec((D, 2 * MQ_W), lambda i: (0, 0)),
        ],
        out_specs=pl.BlockSpec((MEM_LEN, 2 * MQ_W), lambda i: (i, 0)),
        out_shape=jax.ShapeDtypeStruct((BATCH * MEM_LEN, 2 * MQ_W), BF16),
        compiler_params=pltpu.CompilerParams(dimension_semantics=("arbitrary",)),
        name="mem_kv",
    )(mem2, g, b, w)


MERGE_TM = 512
MERGE_K1 = MERGE_TM // FFT_N2


def _pack_bf16_pair(v):
    n = v.shape[1] // 2
    bits = lax.bitcast_convert_type(v.astype(BF16).astype(F32), U32)
    return (bits[:, n:] & jnp.uint32(0xFFFF0000)) | (bits[:, :n] >> 16)


def _unpack_bf16_pair(p):
    lo = lax.bitcast_convert_type(p << 16, F32)
    hi = lax.bitcast_convert_type(p & jnp.uint32(0xFFFF0000), F32)
    return lo, hi


def _merge_kernel(x_ref, og_ref, zr_ref, zi_ref, mq_ref, gt_ref, kv_ref,
                  lng_ref, lnb_ref, wg_ref, ccs_ref, perm_ref, wf_ref, wm_ref, wo_ref, bo_ref,
                  l1g_ref, l1b_ref, wr2_ref, wrh_ref, br_ref,
                  h1_ref, h1p_ref, eidx_ref, topw_ref, cnt_ref):
    tm = MERGE_TM
    y = {}

    def branch_fnet():
        zr = zr_ref[...].reshape(tm, FN_W)
        zi = zi_ref[...].reshape(tm, FN_W)
        ys = []
        for g in range(FN_G):
            sl = slice(g * FN_GW, (g + 1) * FN_GW)
            ys.append(_dot(jnp.concatenate([zr[:, sl], zi[:, sl]], axis=1), ccs_ref[...]))
        yield
        yp = _dot(perm_ref[...], jnp.concatenate(ys, axis=1).astype(BF16))
        yield
        y["fnet"] = _dot(yp.astype(BF16), wf_ref[...])

    def branch_mem():
        heads = [slice(hd * MEM_HD, (hd + 1) * MEM_HD) for hd in range(MEM_H)]
        ss = [_dot_nt(mq_ref[:, sl], kv_ref[:, sl]) for sl in heads]
        yield
        oms = []
        for hd, s in enumerate(ss):
            s = s * (MEM_HD ** -0.5)
            s = s - jnp.max(s, axis=-1, keepdims=True)
            p = jnp.exp(s)
            p = p * (1.0 / jnp.sum(p, axis=-1, keepdims=True))
            oms.append(_dot(p.astype(BF16),
                            kv_ref[:, MQ_W + hd * MEM_HD:MQ_W + (hd + 1) * MEM_HD]))
        yield
        y["mem"] = _dot(jnp.concatenate(oms, axis=1).astype(BF16), wm_ref[...])

    def branch_gla():
        y["gla"] = _dot(og_ref[...], wg_ref[...])
        yield

    _interleave(branch_fnet(), branch_mem(), branch_gla())

    def gate(c):
        return 0.5 + 0.5 * jnp.tanh(0.5 * gt_ref[:, c * D:(c + 1) * D].astype(F32))

    merged = gate(0) * y["gla"] + gate(1) * y["fnet"] + gate(2) * y["mem"]
    mix = _dot(merged.astype(BF16), wo_ref[...]) + bo_ref[...]
    h = _ln(x_ref[...], lng_ref[...], lnb_ref[...])
    h1 = _ln(DN_ALPHA * h + mix, l1g_ref[...], l1b_ref[...])
    h1_ref[...] = h1
    h1p_ref[...] = _pack_bf16_pair(h1)

    h_hi, h_lo = _split_bf16(h1)
    d2 = _dot(h_hi, wr2_ref[...])
    l = d2[:, :LANES] + d2[:, LANES:] + _dot(h_lo, wrh_ref[...]) + br_ref[...]
    lane = lax.broadcasted_iota(I32, (tm, LANES), 1)
    vals, idxs = [], []
    for _ in range(TOP_K):
        m = jnp.max(l, axis=-1, keepdims=True)
        idx = jnp.min(jnp.where(l == m, lane, LANES), axis=-1, keepdims=True)
        vals.append(m)
        idxs.append(idx)
        l = jnp.where(lane == idx, -jnp.inf, l)
    es = [jnp.exp(v - vals[0]) for v in vals]
    den = es[0] + es[1] + es[2] + es[3]
    eo = jnp.zeros((tm, LANES), I32)
    wo = jnp.zeros((tm, LANES), F32)
    chosen = jnp.zeros((tm, LANES), F32)
    for k in range(TOP_K):
        eo = jnp.where(lane == k, idxs[k], eo)
        wo = jnp.where(lane == k, es[k] / den, wo)
        chosen = chosen + jnp.where(lane == idxs[k], 1.0, 0.0)
    eidx_ref[...] = eo
    topw_ref[...] = wo

    @pl.when(pl.program_id(0) == 0)
    def _():
        cnt_ref[...] = jnp.zeros_like(cnt_ref)

    cnt_ref[...] += jnp.broadcast_to(jnp.sum(chosen, axis=0, keepdims=True), cnt_ref.shape)


def _merge(x2, og, z, mq, gates, kv, lng, lnb, wg, ccs, perm, wf, wm, wo, bo, l1g, l1b,
           wr2, wrh, br):
    tm = MERGE_TM
    per_b = SEQ // tm
    row = lambda i: (i, 0)
    const = lambda i: (0, 0)
    zblk = (None, None, FFT_N2, MERGE_K1, FN_W)
    outs = (
        jax.ShapeDtypeStruct((T, D), F32),
        jax.ShapeDtypeStruct((T, D // 2), U32),
        jax.ShapeDtypeStruct((T, LANES), I32),
        jax.ShapeDtypeStruct((T, LANES), F32),
        jax.ShapeDtypeStruct((8, LANES), F32),
    )
    return pl.pallas_call(
        _merge_kernel,
        grid=(T // tm,),
        in_specs=[
            pl.BlockSpec((tm, D), row),
            pl.BlockSpec((tm, D), row),
            pl.BlockSpec(zblk, lambda i: (i // per_b, 0, 0, i % per_b, 0)),
            pl.BlockSpec(zblk, lambda i: (i // per_b, 1, 0, i % per_b, 0)),
            pl.BlockSpec((tm, MQ_W), row),
            pl.BlockSpec((tm, 3 * D), lambda i: (i, 1)),
            pl.BlockSpec((MEM_LEN, 2 * MQ_W), lambda i: (i // per_b, 0)),
            pl.BlockSpec((1, D), const), pl.BlockSpec((1, D), const),
            pl.BlockSpec((D, D), const),
            pl.BlockSpec((2 * FN_GW, FN_GW), const),
            pl.BlockSpec((tm, tm), const),
            pl.BlockSpec((FN_W, D), const),
            pl.BlockSpec((MQ_W, D), const),
            pl.BlockSpec((D, D), const),
            pl.BlockSpec((1, D), const),
            pl.BlockSpec((1, D), const), pl.BlockSpec((1, D), const),
            pl.BlockSpec((D, 2 * LANES), const),
            pl.BlockSpec((D, LANES), const),
            pl.BlockSpec((1, LANES), const),
        ],
        out_specs=[
            pl.BlockSpec((tm, D), row),
            pl.BlockSpec((tm, D // 2), row),
            pl.BlockSpec((tm, LANES), row),
            pl.BlockSpec((tm, LANES), row),
            pl.BlockSpec((8, LANES), const),
        ],
        out_shape=outs,
        compiler_params=pltpu.CompilerParams(
            dimension_semantics=("arbitrary",),
            vmem_limit_bytes=58 * MIB),
        name="merge_ln1_router",
    )(x2, og, z, z, mq, gates, kv, lng, lnb, wg, ccs, perm, wf, wm, wo, bo, l1g, l1b,
      wr2, wrh, br)


PLAN_TP = 1024


def _expert_onehots(e, lane):
    onehots = [lane == e[:, k:k + 1] for k in range(TOP_K)]
    mf = jnp.zeros(lane.shape, F32)
    for oh in onehots:
        mf = mf + jnp.where(oh, 1.0, 0.0)
    return onehots, mf


def _plan_kernel(e_ref, tot_ref, dest_ref, cnt_ref, off_ref):
    i = pl.program_id(0)
    tp = PLAN_TP
    lane = lax.broadcasted_iota(I32, (tp, LANES), 1)
    onehots, mf = _expert_onehots(e_ref[...], lane)

    @pl.when(i == 0)
    def _():
        tot = tot_ref[0:1, :]
        padded = jnp.floor((tot + (MOE_BM - 1)) * (1.0 / MOE_BM)) * MOE_BM
        lane1 = lax.broadcasted_iota(I32, (1, LANES), 1)
        inc = padded
        for s in (1, 2, 4, 8, 16, 32, 64):
            inc = inc + jnp.where(lane1 >= s, pltpu.roll(inc, s, 1), 0.0)
        off_ref[...] = inc - padded
        cnt_ref[...] = jnp.zeros_like(cnt_ref)

    ri = lax.broadcasted_iota(I32, (tp, tp), 0)
    ci = lax.broadcasted_iota(I32, (tp, tp), 1)
    ltri = jnp.where(ri > ci, 1.0, 0.0).astype(BF16)
    rank = _dot(ltri, mf.astype(BF16)) + cnt_ref[...] + off_ref[...]
    out = jnp.zeros((tp, LANES), I32)
    for k in range(TOP_K):
        dk = jnp.sum(jnp.where(onehots[k], rank, 0.0), axis=-1, keepdims=True)
        out = jnp.where(lane == k, dk.astype(I32), out)
    dest_ref[...] = out
    cnt_ref[...] += jnp.sum(mf, axis=0, keepdims=True)


def _plan(eidx, cnt):
    tp = PLAN_TP
    return pl.pallas_call(
        _plan_kernel,
        grid=(T // tp,),
        in_specs=[pl.BlockSpec((tp, LANES), lambda i: (i, 0)),
                  pl.BlockSpec((8, LANES), lambda i: (0, 0))],
        out_specs=pl.BlockSpec((tp, LANES), lambda i: (i, 0)),
        out_shape=jax.ShapeDtypeStruct((T, LANES), I32),
        scratch_shapes=[pltpu.VMEM((1, LANES), F32), pltpu.VMEM((1, LANES), F32)],
        compiler_params=pltpu.CompilerParams(dimension_semantics=("arbitrary",)),
        name="route_plan",
    )(eidx, cnt)


MOE_BM = 512
MOE_NW = A_ROWS // MOE_BM + N_EXP
XS_ROWS = MOE_NW * MOE_BM
MOE_FF_SLICES = 2


def _expert_kernel(we_ref, wb_ref, wv_ref, wfe_ref, wsl_ref, wnx_ref,
                   x_ref, wgu_hbm, bgu_ref, wdn_hbm, bdn_ref, o_ref,
                   wgu_f32, wdn_f32, wgu_bf, wdn_bf, sems):
    w = pl.program_id(0)
    e = we_ref[w]

    def weight_copies(expert, slot):
        return (pltpu.make_async_copy(wgu_hbm.at[expert], wgu_f32.at[slot], sems.at[slot, 0]),
                pltpu.make_async_copy(wdn_hbm.at[expert], wdn_f32.at[slot], sems.at[slot, 1]))

    @pl.when(w == 0)
    def _():
        for cp_ in weight_copies(e, 0):
            cp_.start()

    @pl.when(wfe_ref[w] == 1)
    def _():
        slot = wsl_ref[w]
        for cp_ in weight_copies(e, slot):
            cp_.wait()
        wgu_bf[...] = wgu_f32[slot].astype(BF16)
        wdn_bf[...] = wdn_f32[slot].astype(BF16)
        nxt = wnx_ref[w]

        @pl.when(nxt >= 0)
        def _():
            for cp_ in weight_copies(nxt, 1 - slot):
                cp_.start()

    def ffn(rows):
        xlo, xhi = _unpack_bf16_pair(x_ref[rows, :])
        xb = jnp.concatenate([xlo.astype(BF16), xhi.astype(BF16)], axis=1)
        bgu = bgu_ref[pl.ds(e, 1), :]
        hw = D_FF // MOE_FF_SLICES
        gus = []
        for hf in range(MOE_FF_SLICES):
            gc = slice(hf * hw, (hf + 1) * hw)
            uc = slice(D_FF + hf * hw, D_FF + (hf + 1) * hw)
            gus.append((_dot(xb, wgu_bf[:, gc]) + bgu[:, gc], _dot(xb, wgu_bf[:, uc]) + bgu[:, uc]))
        out = bdn_ref[pl.ds(e, 1), :]
        for hf, (g, u) in enumerate(gus):
            gate = jnp.minimum(g, SW_LIMIT)
            up = jnp.clip(u, -SW_LIMIT, SW_LIMIT)
            act = (up + 1.0) * (gate * jax.nn.sigmoid(SW_ALPHA * gate))
            out = out + _dot(act.astype(BF16), wdn_bf[hf * hw:(hf + 1) * hw, :])
        o_ref[rows, :] = _pack_bf16_pair(out)

    @pl.when(wv_ref[w] == 1)
    def _():
        ffn(slice(None))

    @pl.when(wv_ref[w] == 2)
    def _():
        ffn(slice(0, MOE_BM // 2))


def _experts(meta, xs, w_gu, b_gu, w_down, b_down):
    return pl.pallas_call(
        _expert_kernel,
        grid_spec=pltpu.PrefetchScalarGridSpec(
            num_scalar_prefetch=len(meta),
            grid=(MOE_NW,),
            in_specs=[
                pl.BlockSpec((MOE_BM, D // 2), lambda w, we, wb, *_: (wb[w], 0)),
                pl.BlockSpec(memory_space=pl.ANY),
                pl.BlockSpec((N_EXP, 2 * D_FF), lambda w, *_: (0, 0)),
                pl.BlockSpec(memory_space=pl.ANY),
                pl.BlockSpec((N_EXP, D), lambda w, *_: (0, 0)),
            ],
            out_specs=pl.BlockSpec((MOE_BM, D // 2), lambda w, we, wb, *_: (wb[w], 0)),
            scratch_shapes=[
                pltpu.VMEM((2, D, 2 * D_FF), F32),
                pltpu.VMEM((2, D_FF, D), F32),
                pltpu.VMEM((D, 2 * D_FF), BF16),
                pltpu.VMEM((D_FF, D), BF16),
                pltpu.SemaphoreType.DMA((2, 2)),
            ],
        ),
        out_shape=jax.ShapeDtypeStruct((XS_ROWS, D // 2), U32),
        compiler_params=pltpu.CompilerParams(
            dimension_semantics=("arbitrary",),
            vmem_limit_bytes=56 * MIB),
        name="moe_experts",
    )(*meta, xs, w_gu, b_gu, w_down, b_down)


def _work_items(counts):
    n_e = (counts + MOE_BM - 1) // MOE_BM
    item_end = jnp.cumsum(n_e)
    total = item_end[-1]
    w = jnp.arange(MOE_NW, dtype=I32)
    valid = w < total
    wc = jnp.minimum(w, total - 1)
    e_w = jnp.sum((item_end[None, :] <= wc[:, None]).astype(I32), axis=1)
    e_w = jnp.minimum(e_w, N_EXP - 1)
    rows_here = counts[e_w] - (wc - (item_end - n_e)[e_w]) * MOE_BM
    valid = jnp.where(valid, jnp.where(rows_here <= MOE_BM // 2, 2, 1), 0)
    prev_e = jnp.concatenate([jnp.full((1,), -1, I32), e_w[:-1]])
    fe = (e_w != prev_e).astype(I32)
    slot = (jnp.cumsum(fe) - 1) % 2
    first_at = jnp.where(fe == 1, w, MOE_NW)
    next_first = jnp.concatenate([lax.cummin(first_at, reverse=True)[1:],
                                  jnp.full((1,), MOE_NW, I32)])
    nxt = jnp.where(next_first < MOE_NW, e_w[jnp.minimum(next_first, MOE_NW - 1)], -1)
    return tuple(a.astype(I32) for a in (e_w, wc, valid, fe, slot, nxt))


COMB_TM = 512
SC_CORES = 2
SC_SUBCORES = 16
SC_WORKERS = SC_CORES * SC_SUBCORES
SC_CH = 64
COMB_GROUPS = 8
COMB_TG = T // COMB_GROUPS
SC_ROWS_PER_W = COMB_TG * TOP_K // SC_WORKERS
SC_NCH = SC_ROWS_PER_W // SC_CH


def _sc_gather(table, idx3):
    mesh = plsc.VectorSubcoreMesh(core_axis_name="c", subcore_axis_name="s")

    @functools.partial(
        pl.kernel, mesh=mesh,
        out_type=jax.ShapeDtypeStruct((COMB_TG * TOP_K, D // 2), U32),
        scratch_types=[
            pltpu.VMEM((SC_NCH, SC_CH), I32),
            pltpu.VMEM((2, SC_CH, D // 2), U32),
            pltpu.SemaphoreType.DMA((2,)),
            pltpu.SemaphoreType.DMA((2,)),
        ],
    )
    def k(table_hbm, idx_hbm, out_hbm, idx_v, rows_v, gsem, psem):
        wid = lax.axis_index("s") * SC_CORES + lax.axis_index("c")
        base = wid * SC_ROWS_PER_W
        pltpu.sync_copy(idx_hbm.at[wid], idx_v)

        def gather(j, b):
            return pltpu.make_async_copy(table_hbm.at[idx_v.at[j]], rows_v.at[b], gsem.at[b])

        def put(j, b):
            return pltpu.make_async_copy(rows_v.at[b], out_hbm.at[pl.ds(base + j * SC_CH, SC_CH)],
                                         psem.at[b])

        gather(0, 0).start()

        @pl.loop(0, SC_NCH, step=2)
        def _(j0):
            for b in range(2):
                j = j0 + b

                @pl.when(j + 1 < SC_NCH)
                def _():
                    @pl.when(j >= 1)
                    def _():
                        put(j - 1, 1 - b).wait()
                    gather(j + 1, 1 - b).start()

                gather(j, b).wait()
                put(j, b).start()

        put(SC_NCH - 2, 0).wait()
        put(SC_NCH - 1, 1).wait()

    return k(table, idx3)


SCD_TOK_PER_W = T // SC_WORKERS
SCD_NCH = SCD_TOK_PER_W // SC_CH


def _sc_dispatch(h1p, idx4):
    mesh = plsc.VectorSubcoreMesh(core_axis_name="c", subcore_axis_name="s")

    @functools.partial(
        pl.kernel, mesh=mesh,
        out_type=jax.ShapeDtypeStruct((XS_ROWS, D // 2), U32),
        scratch_types=[
            pltpu.VMEM((SCD_NCH * TOP_K, SC_CH), I32),
            pltpu.VMEM((2, SC_CH, D // 2), U32),
            pltpu.SemaphoreType.DMA((2,)),
            pltpu.SemaphoreType.DMA((2,)),
        ],
    )
    def k(h_hbm, idx_hbm, xs_hbm, idx_v, rows_v, gsem, psem):
        wid = lax.axis_index("s") * SC_CORES + lax.axis_index("c")
        base = wid * SCD_TOK_PER_W
        pltpu.sync_copy(idx_hbm.at[wid], idx_v)

        def get(c, b):
            return pltpu.make_async_copy(h_hbm.at[pl.ds(base + c * SC_CH, SC_CH)], rows_v.at[b],
                                         gsem.at[b])

        def puts(c, b):
            return [pltpu.make_async_copy(rows_v.at[b], xs_hbm.at[idx_v.at[c * TOP_K + kk]],
                                          psem.at[b]) for kk in range(TOP_K)]

        get(0, 0).start()

        @pl.loop(0, SCD_NCH, step=2)
        def _(c0):
            for b in range(2):
                c = c0 + b

                @pl.when(c + 1 < SCD_NCH)
                def _():
                    @pl.when(c >= 1)
                    def _():
                        for cp_ in puts(c - 1, 1 - b):
                            cp_.wait()
                    get(c + 1, 1 - b).start()

                get(c, b).wait()
                for cp_ in puts(c, b):
                    cp_.start()

        for cp_ in puts(SCD_NCH - 2, 0) + puts(SCD_NCH - 1, 1):
            cp_.wait()

    return k(h1p, idx4)


def _combine_dense_kernel(g_ref, h1_ref, tw_ref, lg_ref, lb_ref, o_ref):
    tw = tw_ref[...]
    ylo = jnp.zeros((COMB_TM, D // 2), F32)
    yhi = jnp.zeros((COMB_TM, D // 2), F32)
    for k in range(TOP_K):
        lo, hi = _unpack_bf16_pair(g_ref[k])
        wk = tw[:, k:k + 1]
        ylo = ylo + lo * wk
        yhi = yhi + hi * wk
    ff = jnp.concatenate([ylo, yhi], axis=1)
    o_ref[...] = _ln(DN_ALPHA * h1_ref[...] + ff, lg_ref[...], lb_ref[...])


def _combine_dense(g4, h1, topw, g, b, group):
    tm = COMB_TM
    t0 = group * (COMB_TG // tm)
    return pl.pallas_call(
        _combine_dense_kernel,
        grid=(COMB_TG // tm,),
        in_specs=[
            pl.BlockSpec((TOP_K, tm, D // 2), lambda i: (0, i, 0)),
            pl.BlockSpec((tm, D), lambda i: (t0 + i, 0)),
            pl.BlockSpec((tm, LANES), lambda i: (t0 + i, 0)),
            pl.BlockSpec((1, D), lambda i: (0, 0)),
            pl.BlockSpec((1, D), lambda i: (0, 0)),
        ],
        out_specs=pl.BlockSpec((tm, D), lambda i: (t0 + i, 0)),
        out_shape=jax.ShapeDtypeStruct((T, D), F32),
        input_output_aliases={1: 0},
        compiler_params=pltpu.CompilerParams(
            dimension_semantics=("arbitrary",),
            vmem_limit_bytes=40 * MIB),
        name="moe_combine_dense_ln2",
    )(g4, h1, topw, g, b)


def _pad_cols(a, n):
    return jnp.pad(a, ((0, 0), (0, n - a.shape[1])))


def kernel(x, mem, ln_in_g, ln_in_b, ln_mem_g, ln_mem_b, w_in, b_in, w_decay_f, b_decay_f,
           w_decay_b, b_decay_b, gla_norm_g, w_br_gla, w_br_fnet, w_br_mem, w_mem_kv, w_out,
           b_out, ln1_g, ln1_b, w_router, b_router, w_gu, b_gu, w_down, b_down, ln2_g, ln2_b):
    assert x.shape == (BATCH, SEQ, D) and w_in.shape[0] == 1
    row = lambda a: a.reshape(1, -1)
    x2 = x.reshape(T, D)
    w_in0, b_in0 = w_in[0], b_in[0]
    c_lr, c_fn, c_mq, c_gt = 3072, 3072 + 2 * GLA_LR, 3104 + FN_W, 3104 + FN_W + MQ_W
    w_main = jnp.concatenate([w_in0[:, :c_lr], w_in0[:, c_gt:]], axis=1).astype(BF16)
    b_main = row(jnp.concatenate([b_in0[:c_lr], b_in0[c_gt:]]))
    w_lr = _pad_cols(w_in0[:, c_lr:c_fn], LANES).astype(BF16)
    b_lr = _pad_cols(row(b_in0[c_lr:c_fn]), LANES)
    w_mq = w_in0[:, c_mq:c_gt].astype(BF16)
    b_mq = row(b_in0[c_mq:c_gt])
    w_fn = w_in0[:, c_fn:c_mq].astype(BF16)
    b_fn = row(b_in0[c_fn:c_mq])
    lng, lnb = row(ln_in_g), row(ln_in_b)

    proj, mq, lr = _inproj(x2, lng, lnb, w_main, b_main, w_mq, b_mq, w_lr, b_lr)

    zpad = jnp.zeros((LANES - 2 * GLA_LR, GLA_H * GLA_DK), F32)
    zlr = jnp.zeros((GLA_LR, GLA_H * GLA_DK), F32)
    wdf = jnp.concatenate([w_decay_f[0], zlr, zpad], axis=0).reshape(LANES, GLA_H, GLA_DK)
    wdb = jnp.concatenate([zlr, w_decay_b[0], zpad], axis=0).reshape(LANES, GLA_H, GLA_DK)
    wd = jnp.concatenate([wdf, wdb], axis=2).reshape(LANES, GLA_H * 2 * GLA_DK).astype(BF16)
    wd = jnp.concatenate([wd, wd], axis=0)
    bd = jnp.concatenate([b_decay_f[0].reshape(GLA_H, GLA_DK),
                          b_decay_b[0].reshape(GLA_H, GLA_DK)], axis=1).reshape(1, -1)
    og = _gla(proj, lr, wd, bd, row(gla_norm_g[0]))

    fbig, cwt, swt, f2, ccs, perm = _dft_tables(MERGE_TM)
    x4 = x.reshape(BATCH, FFT_N2, FFT_N1, D)
    z = _fft2(_fft1(x4, lng, lnb, w_fn, b_fn, fbig, cwt, swt), f2)

    kv = _memkv(mem.reshape(BATCH * MEM_LEN, D), row(ln_mem_g), row(ln_mem_b),
                w_mem_kv[0].astype(BF16))

    w_r = _pad_cols(w_router[0], LANES)
    wr_hi = w_r.astype(BF16)
    wr_lo = (w_r - wr_hi.astype(F32)).astype(BF16)
    b_r = jnp.concatenate([row(b_router[0]),
                           jnp.full((1, LANES - N_EXP), NEG_BIG, F32)], axis=1)
    h1, h1p, eidx, topw, cnt = _merge(
        x2, og, z, mq, proj, kv, lng, lnb,
        w_br_gla[0].astype(BF16), ccs, perm, w_br_fnet[0].astype(BF16),
        w_br_mem[0].astype(BF16), w_out[0].astype(BF16), row(b_out[0]),
        row(ln1_g[0]), row(ln1_b[0]), jnp.concatenate([wr_hi, wr_lo], axis=1), wr_hi, b_r)

    dest = _plan(eidx, cnt)
    counts = cnt[0, :N_EXP].astype(I32)
    dest_k = dest[:, :TOP_K]
    idx4 = dest_k.reshape(SC_WORKERS, SCD_NCH, SC_CH, TOP_K).transpose(0, 1, 3, 2)
    xs = _sc_dispatch(h1p, idx4.reshape(SC_WORKERS, SCD_NCH * TOP_K, SC_CH))
    ys = _experts(_work_items(counts), xs, w_gu[0], b_gu[0], w_down[0], b_down[0])
    out = h1
    for grp in range(COMB_GROUPS):
        dest_g = dest_k[grp * COMB_TG:(grp + 1) * COMB_TG].T.reshape(SC_WORKERS, SC_NCH, SC_CH)
        g4 = _sc_gather(ys, dest_g).reshape(TOP_K, COMB_TG, D // 2)
        out = _combine_dense(g4, out, topw, row(ln2_g[0]), row(ln2_b[0]), grp)
    return out.reshape(BATCH, SEQ, D)
```

```python
import functools
import math

import numpy as np
import jax
import jax.numpy as jnp
from jax import lax
from jax.experimental import pallas as pl
from jax.experimental.pallas import tpu as pltpu
from jax.experimental.pallas import tpu_sc as plsc

F32 = jnp.float32
BF16 = jnp.bfloat16
I32 = jnp.int32
U32 = jnp.uint32

D = 1024
BATCH = 4
SEQ = 4096
T = BATCH * SEQ
GLA_H = 4
GLA_DK = 128
GLA_DV = 256
GLA_LR = 16
GLA_TAU = 16.0
GLA_C = 64
FN_G = 4
FN_GW = 128
FN_W = 512
MEM_LEN = 256
MEM_H = 4
MEM_HD = 128
MQ_W = 512
N_EXP = 32
TOP_K = 4
D_FF = 1024
SW_LIMIT = 7.0
SW_ALPHA = 1.702
LN_EPS = 1e-5
RMS_EPS = 1e-6
DN_ALPHA = 2.0 ** 0.25
A_ROWS = T * TOP_K

FFT_N1 = 128
FFT_N2 = 32

LANES = 128
NEG_BIG = -1e30
MIB = 1024 * 1024


def _ln(x, g, b):
    mu = jnp.mean(x, axis=-1, keepdims=True)
    xc = x - mu
    var = jnp.mean(xc * xc, axis=-1, keepdims=True)
    return xc * lax.rsqrt(var + LN_EPS) * g + b


def _dot(a, b):
    return jnp.dot(a, b, preferred_element_type=F32)


def _dot_nt(a, b):
    return lax.dot_general(a, b, (((1,), (1,)), ((), ())), preferred_element_type=F32)


def _dot_tn(a, b):
    return lax.dot_general(a, b, (((0,), (0,)), ((), ())), preferred_element_type=F32)


def _interleave(*stages):
    live = list(stages)
    while live:
        for st in list(live):
            try:
                next(st)
            except StopIteration:
                live.remove(st)


def _split_bf16(a):
    hi = a.astype(BF16)
    return hi, (a - hi.astype(F32)).astype(BF16)


INPROJ_TM = 1024
INPROJ_TN = 3072


PROJ_W = 6 * 1024


def _inproj_kernel(x_ref, g_ref, b_ref, w_ref, bias_ref, wmq_ref, bmq_ref, wlr_ref, blr_ref,
                   proj_ref, mq_ref, lr_ref, hb_ref):
    @pl.when(pl.program_id(1) == 0)
    def _():
        hb = _ln(x_ref[...], g_ref[...], b_ref[...]).astype(BF16)
        hb_ref[...] = hb
        lr_ref[...] = _dot(hb, wlr_ref[...]) + blr_ref[...]
        mq_ref[...] = (_dot(hb, wmq_ref[...]) + bmq_ref[...]).astype(BF16)

    proj_ref[...] = (_dot(hb_ref[...], w_ref[...]) + bias_ref[...]).astype(BF16)


def _inproj(x2, ln_g, ln_b, w_main, b_main, w_mq, b_mq, w_lr, b_lr):
    tm, tn = INPROJ_TM, INPROJ_TN
    nj = PROJ_W // tn
    row = lambda i, j: (i, 0)
    const = lambda i, j: (0, 0)
    outs = (
        jax.ShapeDtypeStruct((T, PROJ_W), BF16),
        jax.ShapeDtypeStruct((T, MQ_W), BF16),
        jax.ShapeDtypeStruct((T, LANES), F32),
    )
    return pl.pallas_call(
        _inproj_kernel,
        grid=(T // tm, nj),
        in_specs=[
            pl.BlockSpec((tm, D), row),
            pl.BlockSpec((1, D), const),
            pl.BlockSpec((1, D), const),
            pl.BlockSpec((D, tn), lambda i, j: (0, j)),
            pl.BlockSpec((1, tn), lambda i, j: (0, j)),
            pl.BlockSpec((D, MQ_W), const),
            pl.BlockSpec((1, MQ_W), const),
            pl.BlockSpec((D, LANES), const),
            pl.BlockSpec((1, LANES), const),
        ],
        out_specs=[
            pl.BlockSpec((tm, tn), lambda i, j: (i, j)),
            pl.BlockSpec((tm, MQ_W), row),
            pl.BlockSpec((tm, LANES), row),
        ],
        out_shape=outs,
        scratch_shapes=[pltpu.VMEM((tm, D), BF16)],
        compiler_params=pltpu.CompilerParams(
            dimension_semantics=("arbitrary", "arbitrary"),
            vmem_limit_bytes=48 * MIB),
        name="ln_inproj",
    )(x2, ln_g, ln_b, w_main, b_main, w_mq, b_mq, w_lr, b_lr)


GLA_BULK = 256
GLA_FIN = 512
GLA_NCH = SEQ // GLA_C
GLA_CPB = GLA_BULK // GLA_C
GLA_PIPE = 4


def _gla_kernel(q_ref, k_ref, v_ref, r_ref, lr_ref, wd_ref, bd_ref, g_ref, cs_ref, o_ref,
                acc_ref, qin_ref, kin_ref, kst_ref, dec_ref, u_ref, stf_ref, stb_ref):
    C = GLA_C
    G = GLA_BULK
    DK = GLA_DK
    NG = SEQ // G
    scale = DK ** -0.5
    ii = lax.broadcasted_iota(I32, (G, G), 0)
    jj = lax.broadcasted_iota(I32, (G, G), 1)
    same = (ii // C) == (jj // C)
    lower = jnp.logical_and(same, ii >= jj)
    upper = jnp.logical_and(same, ii <= jj)
    is_fwd = lax.broadcasted_iota(I32, (G, 2 * DK), 1) < DK
    chunk_of_row = lax.broadcasted_iota(I32, (G, DK), 0) // C

    def stage_a(gi):
        rows = pl.ds(pl.multiple_of(gi * G, G), G)
        z = _dot(jnp.concatenate(_split_bf16(lr_ref[rows, :]), axis=1), wd_ref[...]) + bd_ref[...]
        yield
        la = -(jnp.maximum(-z, 0.0) + jnp.log(1.0 + jnp.exp(-jnp.abs(z)))) * (1.0 / GLA_TAU)
        la_hi, la_lo = _split_bf16(la)
        pre2 = _dot(cs_ref[...], jnp.concatenate([la_hi, la_lo], axis=1))
        yield
        pre = pre2[:, :2 * DK] + pre2[:, 2 * DK:]
        blast = jnp.concatenate(
            [jnp.broadcast_to(pre[ci * C + C - 1:ci * C + C, :], (C, 2 * DK))
             for ci in range(GLA_CPB)], axis=0)
        b = jnp.where(is_fwd, pre, blast - pre + la)
        qf32 = q_ref[rows, :].astype(F32)
        kf32 = k_ref[rows, :].astype(F32)
        q2 = jnp.concatenate([qf32, qf32], axis=1)
        k2 = jnp.concatenate([kf32, kf32], axis=1)
        qin_ref[rows, :] = (q2 * (scale * jnp.exp(b))).astype(BF16)
        kin_ref[rows, :] = (k2 * jnp.exp(-b)).astype(BF16)
        kst_ref[rows, :] = (k2 * jnp.exp(blast - b)).astype(BF16)
        dec = jnp.exp(blast)
        for ci in range(GLA_CPB):
            dec_ref[pl.ds(gi * GLA_CPB + ci, 1), :] = dec[ci * C:ci * C + 1, :]

    def stage_b(gi):
        rows = pl.ds(pl.multiple_of(gi * G, G), G)
        qi = qin_ref[rows, :]
        ki = kin_ref[rows, :]
        ks = kst_ref[rows, :]
        vb = v_ref[rows, :]
        att_f = _dot_nt(qi[:, :DK], ki[:, :DK])
        att_b = _dot_nt(qi[:, DK:], ki[:, DK:])
        yield
        att = jnp.where(lower, att_f, 0.0) + jnp.where(upper, att_b, 0.0)
        acc_ref[rows, :] = _dot(att.astype(BF16), vb)
        yield
        ksb = jnp.concatenate(
            [jnp.where(chunk_of_row == ci, ks[:, d * DK:(d + 1) * DK], jnp.zeros((G, DK), BF16))
             for d in range(2) for ci in range(GLA_CPB)], axis=1)
        u = _dot_tn(vb, ksb)
        for d in range(2):
            for ci in range(GLA_CPB):
                col = (d * GLA_CPB + ci) * DK
                u_ref[d, gi * GLA_CPB + ci] = u[:, col:col + DK]

    P = GLA_PIPE
    _interleave(*[stage_a(j) for j in range(P)])

    def bulk(i, carry):
        g = P * i
        _interleave(*[st for j in range(P) for st in (stage_b(g - P + j), stage_a(g + j))])
        return carry

    lax.fori_loop(1, NG // P, bulk, 0)
    _interleave(*[stage_b(NG - P + j) for j in range(P)])

    stf_ref[...] = jnp.zeros_like(stf_ref)
    stb_ref[...] = jnp.zeros_like(stb_ref)

    def one(n, d, st_ref):
        lanes = slice(d * DK, (d + 1) * DK)
        rows = pl.ds(pl.multiple_of(n * C, C), C)
        st = st_ref[...]
        acc_ref[rows, :] += _dot_nt(qin_ref[rows, lanes], st.astype(BF16))
        st_ref[...] = st * dec_ref[pl.ds(n, 1), :][:, lanes] + u_ref[d, n]

    def step(i, carry):
        one(i, 0, stf_ref)
        one(GLA_NCH - 1 - i, 1, stb_ref)
        return carry

    lax.fori_loop(0, GLA_NCH, step, 0, unroll=8)

    def fin(gi, carry):
        rows = pl.ds(pl.multiple_of(gi * GLA_FIN, GLA_FIN), GLA_FIN)
        o = acc_ref[rows, :]
        o = o * lax.rsqrt(jnp.mean(o * o, axis=-1, keepdims=True) + RMS_EPS) * g_ref[...]
        rg = r_ref[rows, :].astype(F32)
        o_ref[rows, :] = (o * (rg * jax.nn.sigmoid(rg))).astype(BF16)
        return carry

    lax.fori_loop(0, SEQ // GLA_FIN, fin, 0)


def _gla(proj, lr, wd, bd, g):
    i = np.arange(GLA_BULK)
    cs = ((i[:, None] // GLA_C) == (i[None, :] // GLA_C)) & (i[:, None] >= i[None, :])
    cs = jnp.asarray(cs, dtype=F32).astype(BF16)
    v_blk = 1024 // GLA_DV
    return pl.pallas_call(
        _gla_kernel,
        grid=(BATCH, GLA_H),
        in_specs=[
            pl.BlockSpec((SEQ, GLA_DK), lambda b, h: (b, h)),
            pl.BlockSpec((SEQ, GLA_DK), lambda b, h: (b, GLA_H + h)),
            pl.BlockSpec((SEQ, GLA_DV), lambda b, h: (b, v_blk + h)),
            pl.BlockSpec((SEQ, GLA_DV), lambda b, h: (b, 2 * v_blk + h)),
            pl.BlockSpec((SEQ, LANES), lambda b, h: (b, 0)),
            pl.BlockSpec((2 * LANES, 2 * GLA_DK), lambda b, h: (0, h)),
            pl.BlockSpec((1, 2 * GLA_DK), lambda b, h: (0, h)),
            pl.BlockSpec((1, GLA_DV), lambda b, h: (0, 0)),
            pl.BlockSpec((GLA_BULK, GLA_BULK), lambda b, h: (0, 0)),
        ],
        out_specs=pl.BlockSpec((SEQ, GLA_DV), lambda b, h: (b, h)),
        out_shape=jax.ShapeDtypeStruct((T, GLA_H * GLA_DV), BF16),
        scratch_shapes=[
            pltpu.VMEM((SEQ, GLA_DV), F32),
            pltpu.VMEM((SEQ, 2 * GLA_DK), BF16),
            pltpu.VMEM((SEQ, 2 * GLA_DK), BF16),
            pltpu.VMEM((SEQ, 2 * GLA_DK), BF16),
            pltpu.VMEM((GLA_NCH, 2 * GLA_DK), F32),
            pltpu.VMEM((2, GLA_NCH, GLA_DV, GLA_DK), F32),
            pltpu.VMEM((GLA_DV, GLA_DK), F32),
            pltpu.VMEM((GLA_DV, GLA_DK), F32),
        ],
        compiler_params=pltpu.CompilerParams(
            dimension_semantics=("arbitrary", "arbitrary"),
            vmem_limit_bytes=58 * MIB),
        name="gla",
    )(proj, proj, proj, proj, lr, wd, bd, g, cs)


FFT1_S = 16
FFT1_SUB = 2
FFT1_ROWS = FFT_N2 * FFT1_S
FFT2_KB = 8


def _fft1_kernel(x_ref, g_ref, b_ref, w_ref, bias_ref, fbig_ref, cw_ref, sw_ref, o_ref):
    sh = FFT1_S // FFT1_SUB
    rows = FFT_N2 * sh
    res = {}

    def sub(h):
        xv = x_ref[:, h * sh:(h + 1) * sh, :].reshape(rows, D)
        hb = _ln(xv, g_ref[...], b_ref[...]).astype(BF16)
        fn = (_dot(hb, w_ref[...]) + bias_ref[...]).astype(BF16)
        yield
        a = _dot(fbig_ref[...], fn)
        yield
        ar = a[:rows]
        ai = a[rows:]
        cw = jnp.concatenate([cw_ref[h]] * (FN_W // LANES), axis=1)
        sw = jnp.concatenate([sw_ref[h]] * (FN_W // LANES), axis=1)
        res[h] = ((ar * cw + ai * sw).reshape(FFT_N2, sh, FN_W),
                  (ai * cw - ar * sw).reshape(FFT_N2, sh, FN_W))

    _interleave(*[sub(h) for h in range(FFT1_SUB)])
    for ri in range(2):
        o_ref[ri] = jnp.concatenate([res[h][ri] for h in range(FFT1_SUB)], axis=1).astype(BF16)


def _fft1(x4, ln_g, ln_b, w_fn, b_fn, fbig, cwt, swt):
    s = FFT1_S
    const = lambda b, j: (0, 0)
    return pl.pallas_call(
        _fft1_kernel,
        grid=(BATCH, FFT_N1 // s),
        in_specs=[
            pl.BlockSpec((None, FFT_N2, s, D), lambda b, j: (b, 0, j, 0)),
            pl.BlockSpec((1, D), const),
            pl.BlockSpec((1, D), const),
            pl.BlockSpec((D, FN_W), const),
            pl.BlockSpec((1, FN_W), const),
            pl.BlockSpec((2 * FFT1_ROWS // FFT1_SUB, FFT1_ROWS // FFT1_SUB), const),
            pl.BlockSpec((FFT1_SUB, FFT1_ROWS // FFT1_SUB, LANES), lambda b, j: (j, 0, 0)),
            pl.BlockSpec((FFT1_SUB, FFT1_ROWS // FFT1_SUB, LANES), lambda b, j: (j, 0, 0)),
        ],
        out_specs=pl.BlockSpec((None, 2, FFT_N2, s, FN_W), lambda b, j: (b, 0, 0, j, 0)),
        out_shape=jax.ShapeDtypeStruct((BATCH, 2, FFT_N2, FFT_N1, FN_W), BF16),
        compiler_params=pltpu.CompilerParams(
            dimension_semantics=("arbitrary", "arbitrary"),
            vmem_limit_bytes=40 * MIB),
        name="fft_stage1",
    )(x4, ln_g, ln_b, w_fn, b_fn, fbig, cwt, swt)


def _fft2_kernel(d_ref, f2_ref, o_ref):
    f2 = f2_ref[...]
    for kk in range(FFT2_KB):
        z = _dot(f2, jnp.concatenate([d_ref[0, kk], d_ref[1, kk]], axis=0))
        o_ref[0, kk] = z[:FFT_N1].astype(BF16)
        o_ref[1, kk] = z[FFT_N1:].astype(BF16)


def _fft2(dmat, f2):
    kb = FFT2_KB
    blk = (None, 2, kb, FFT_N1, FN_W)
    return pl.pallas_call(
        _fft2_kernel,
        grid=(BATCH, FFT_N2 // kb),
        in_specs=[
            pl.BlockSpec(blk, lambda b, j: (b, 0, j, 0, 0)),
            pl.BlockSpec((2 * FFT_N1, 2 * FFT_N1), lambda b, j: (0, 0)),
        ],
        out_specs=pl.BlockSpec(blk, lambda b, j: (b, 0, j, 0, 0)),
        out_shape=jax.ShapeDtypeStruct((BATCH, 2, FFT_N2, FFT_N1, FN_W), BF16),
        compiler_params=pltpu.CompilerParams(
            dimension_semantics=("arbitrary", "arbitrary")),
        name="fft_stage2",
    )(dmat, f2)


def _dft_tables(merge_tm):
    s = FFT1_S // FFT1_SUB
    n2 = np.arange(FFT_N2, dtype=np.float64)
    n1 = np.arange(FFT_N1, dtype=np.float64)
    th = 2.0 * np.pi * np.outer(n2, n2) / FFT_N2
    f1 = np.stack([np.cos(th), -np.sin(th)]) / math.sqrt(SEQ)
    fbig = np.einsum("rkn,st->rksnt", f1, np.eye(s)).reshape(2 * FFT_N2 * s, FFT_N2 * s)
    tw = 2.0 * np.pi * np.outer(n2, n1) / SEQ
    tw = tw.reshape(FFT_N2, FFT_N1 // s, s).transpose(1, 0, 2).reshape(FFT_N1 // s, FFT_N2 * s)
    cwt = np.broadcast_to(np.cos(tw)[:, :, None], tw.shape + (LANES,))
    swt = np.broadcast_to(np.sin(tw)[:, :, None], tw.shape + (LANES,))
    th1 = 2.0 * np.pi * np.outer(n1, n1) / FFT_N1
    c1, s1 = np.cos(th1), np.sin(th1)
    f2 = np.block([[c1, s1], [-s1, c1]])
    cc = np.arange(FN_GW, dtype=np.float64)
    thc = 2.0 * np.pi * np.outer(cc, cc) / FN_GW
    ccs = np.concatenate([np.cos(thc), np.sin(thc)], axis=0) / math.sqrt(FN_GW)
    k1n = merge_tm // FFT_N2
    r = np.arange(merge_tm)
    perm = np.zeros((merge_tm, merge_tm))
    perm[r, (r % FFT_N2) * k1n + r // FFT_N2] = 1.0
    as32 = lambda a: jnp.asarray(np.ascontiguousarray(a), dtype=F32)
    return (as32(fbig).astype(BF16), as32(cwt), as32(swt), as32(f2).astype(BF16),
            as32(ccs).astype(BF16), as32(perm).astype(BF16))


def _memkv_kernel(m_ref, g_ref, b_ref, w_ref, o_ref):
    mn = _ln(m_ref[...], g_ref[...], b_ref[...]).astype(BF16)
    o_ref[...] = _dot(mn, w_ref[...]).astype(BF16)


def _memkv(mem2, g, b, w):
    return pl.pallas_call(
        _memkv_kernel,
        grid=(BATCH,),
        in_specs=[
            pl.BlockSpec((MEM_LEN, D), lambda i: (i, 0)),
            pl.BlockSpec((1, D), lambda i: (0, 0)),
            pl.BlockSpec((1, D), lambda i: (0, 0)),
            pl.BlockSpec((D, 2 * MQ_W), lambda i: (0, 0)),
        ],
        out_specs=pl.BlockSpec((MEM_LEN, 2 * MQ_W), lambda i: (i, 0)),
        out_shape=jax.ShapeDtypeStruct((BATCH * MEM_LEN, 2 * MQ_W), BF16),
        compiler_params=pltpu.CompilerParams(dimension_semantics=("arbitrary",)),
        name="mem_kv",
    )(mem2, g, b, w)


MERGE_TM = 512
MERGE_K1 = MERGE_TM // FFT_N2


def _pack_bf16_pair(v):
    n = v.shape[1] // 2
    bits = lax.bitcast_convert_type(v.astype(BF16).astype(F32), U32)
    return (bits[:, n:] & jnp.uint32(0xFFFF0000)) | (bits[:, :n] >> 16)


def _unpack_bf16_pair(p):
    lo = lax.bitcast_convert_type(p << 16, F32)
    hi = lax.bitcast_convert_type(p & jnp.uint32(0xFFFF0000), F32)
    return lo, hi


def _merge_kernel(x_ref, og_ref, zr_ref, zi_ref, mq_ref, gt_ref, kv_ref,
                  lng_ref, lnb_ref, wg_ref, ccs_ref, perm_ref, wf_ref, wm_ref, wo_ref, bo_ref,
                  l1g_ref, l1b_ref, wr2_ref, wrh_ref, br_ref,
                  h1_ref, h1p_ref, eidx_ref, topw_ref, cnt_ref):
    tm = MERGE_TM
    y = {}

    def branch_fnet():
        zr = zr_ref[...].reshape(tm, FN_W)
        zi = zi_ref[...].reshape(tm, FN_W)
        ys = []
        for g in range(FN_G):
            sl = slice(g * FN_GW, (g + 1) * FN_GW)
            ys.append(_dot(jnp.concatenate([zr[:, sl], zi[:, sl]], axis=1), ccs_ref[...]))
        yield
        yp = _dot(perm_ref[...], jnp.concatenate(ys, axis=1).astype(BF16))
        yield
        y["fnet"] = _dot(yp.astype(BF16), wf_ref[...])

    def branch_mem():
        heads = [slice(hd * MEM_HD, (hd + 1) * MEM_HD) for hd in range(MEM_H)]
        ss = [_dot_nt(mq_ref[:, sl], kv_ref[:, sl]) for sl in heads]
        yield
        oms = []
        for hd, s in enumerate(ss):
            s = s * (MEM_HD ** -0.5)
            s = s - jnp.max(s, axis=-1, keepdims=True)
            p = jnp.exp(s)
            p = p * (1.0 / jnp.sum(p, axis=-1, keepdims=True))
            oms.append(_dot(p.astype(BF16),
                            kv_ref[:, MQ_W + hd * MEM_HD:MQ_W + (hd + 1) * MEM_HD]))
        yield
        y["mem"] = _dot(jnp.concatenate(oms, axis=1).astype(BF16), wm_ref[...])

    def branch_gla():
        y["gla"] = _dot(og_ref[...], wg_ref[...])
        yield

    _interleave(branch_fnet(), branch_mem(), branch_gla())

    def gate(c):
        return 0.5 + 0.5 * jnp.tanh(0.5 * gt_ref[:, c * D:(c + 1) * D].astype(F32))

    merged = gate(0) * y["gla"] + gate(1) * y["fnet"] + gate(2) * y["mem"]
    mix = _dot(merged.astype(BF16), wo_ref[...]) + bo_ref[...]
    h = _ln(x_ref[...], lng_ref[...], lnb_ref[...])
    h1 = _ln(DN_ALPHA * h + mix, l1g_ref[...], l1b_ref[...])
    h1_ref[...] = h1
    h1p_ref[...] = _pack_bf16_pair(h1)

    h_hi, h_lo = _split_bf16(h1)
    d2 = _dot(h_hi, wr2_ref[...])
    l = d2[:, :LANES] + d2[:, LANES:] + _dot(h_lo, wrh_ref[...]) + br_ref[...]
    lane = lax.broadcasted_iota(I32, (tm, LANES), 1)
    vals, idxs = [], []
    for _ in range(TOP_K):
        m = jnp.max(l, axis=-1, keepdims=True)
        idx = jnp.min(jnp.where(l == m, lane, LANES), axis=-1, keepdims=True)
        vals.append(m)
        idxs.append(idx)
        l = jnp.where(lane == idx, -jnp.inf, l)
    es = [jnp.exp(v - vals[0]) for v in vals]
    den = es[0] + es[1] + es[2] + es[3]
    eo = jnp.zeros((tm, LANES), I32)
    wo = jnp.zeros((tm, LANES), F32)
    chosen = jnp.zeros((tm, LANES), F32)
    for k in range(TOP_K):
        eo = jnp.where(lane == k, idxs[k], eo)
        wo = jnp.where(lane == k, es[k] / den, wo)
        chosen = chosen + jnp.where(lane == idxs[k], 1.0, 0.0)
    eidx_ref[...] = eo
    topw_ref[...] = wo

    @pl.when(pl.program_id(0) == 0)
    def _():
        cnt_ref[...] = jnp.zeros_like(cnt_ref)

    cnt_ref[...] += jnp.broadcast_to(jnp.sum(chosen, axis=0, keepdims=True), cnt_ref.shape)


def _merge(x2, og, z, mq, gates, kv, lng, lnb, wg, ccs, perm, wf, wm, wo, bo, l1g, l1b,
           wr2, wrh, br):
    tm = MERGE_TM
    per_b = SEQ // tm
    row = lambda i: (i, 0)
    const = lambda i: (0, 0)
    zblk = (None, None, FFT_N2, MERGE_K1, FN_W)
    outs = (
        jax.ShapeDtypeStruct((T, D), F32),
        jax.ShapeDtypeStruct((T, D // 2), U32),
        jax.ShapeDtypeStruct((T, LANES), I32),
        jax.ShapeDtypeStruct((T, LANES), F32),
        jax.ShapeDtypeStruct((8, LANES), F32),
    )
    return pl.pallas_call(
        _merge_kernel,
        grid=(T // tm,),
        in_specs=[
            pl.BlockSpec((tm, D), row),
            pl.BlockSpec((tm, D), row),
            pl.BlockSpec(zblk, lambda i: (i // per_b, 0, 0, i % per_b, 0)),
            pl.BlockSpec(zblk, lambda i: (i // per_b, 1, 0, i % per_b, 0)),
            pl.BlockSpec((tm, MQ_W), row),
            pl.BlockSpec((tm, 3 * D), lambda i: (i, 1)),
            pl.BlockSpec((MEM_LEN, 2 * MQ_W), lambda i: (i // per_b, 0)),
            pl.BlockSpec((1, D), const), pl.BlockSpec((1, D), const),
            pl.BlockSpec((D, D), const),
            pl.BlockSpec((2 * FN_GW, FN_GW), const),
            pl.BlockSpec((tm, tm), const),
            pl.BlockSpec((FN_W, D), const),
            pl.BlockSpec((MQ_W, D), const),
            pl.BlockSpec((D, D), const),
            pl.BlockSpec((1, D), const),
            pl.BlockSpec((1, D), const), pl.BlockSpec((1, D), const),
            pl.BlockSpec((D, 2 * LANES), const),
            pl.BlockSpec((D, LANES), const),
            pl.BlockSpec((1, LANES), const),
        ],
        out_specs=[
            pl.BlockSpec((tm, D), row),
            pl.BlockSpec((tm, D // 2), row),
            pl.BlockSpec((tm, LANES), row),
            pl.BlockSpec((tm, LANES), row),
            pl.BlockSpec((8, LANES), const),
        ],
        out_shape=outs,
        compiler_params=pltpu.CompilerParams(
            dimension_semantics=("arbitrary",),
            vmem_limit_bytes=58 * MIB),
        name="merge_ln1_router",
    )(x2, og, z, z, mq, gates, kv, lng, lnb, wg, ccs, perm, wf, wm, wo, bo, l1g, l1b,
      wr2, wrh, br)


PLAN_TP = 1024


def _expert_onehots(e, lane):
    onehots = [lane == e[:, k:k + 1] for k in range(TOP_K)]
    mf = jnp.zeros(lane.shape, F32)
    for oh in onehots:
        mf = mf + jnp.where(oh, 1.0, 0.0)
    return onehots, mf


def _plan_kernel(e_ref, tot_ref, dest_ref, cnt_ref, off_ref):
    i = pl.program_id(0)
    tp = PLAN_TP
    lane = lax.broadcasted_iota(I32, (tp, LANES), 1)
    onehots, mf = _expert_onehots(e_ref[...], lane)

    @pl.when(i == 0)
    def _():
        tot = tot_ref[0:1, :]
        padded = jnp.floor((tot + (MOE_BM - 1)) * (1.0 / MOE_BM)) * MOE_BM
        lane1 = lax.broadcasted_iota(I32, (1, LANES), 1)
        inc = padded
        for s in (1, 2, 4, 8, 16, 32, 64):
            inc = inc + jnp.where(lane1 >= s, pltpu.roll(inc, s, 1), 0.0)
        off_ref[...] = inc - padded
        cnt_ref[...] = jnp.zeros_like(cnt_ref)

    ri = lax.broadcasted_iota(I32, (tp, tp), 0)
    ci = lax.broadcasted_iota(I32, (tp, tp), 1)
    ltri = jnp.where(ri > ci, 1.0, 0.0).astype(BF16)
    rank = _dot(ltri, mf.astype(BF16)) + cnt_ref[...] + off_ref[...]
    out = jnp.zeros((tp, LANES), I32)
    for k in range(TOP_K):
        dk = jnp.sum(jnp.where(onehots[k], rank, 0.0), axis=-1, keepdims=True)
        out = jnp.where(lane == k, dk.astype(I32), out)
    dest_ref[...] = out
    cnt_ref[...] += jnp.sum(mf, axis=0, keepdims=True)


def _plan(eidx, cnt):
    tp = PLAN_TP
    return pl.pallas_call(
        _plan_kernel,
        grid=(T // tp,),
        in_specs=[pl.BlockSpec((tp, LANES), lambda i: (i, 0)),
                  pl.BlockSpec((8, LANES), lambda i: (0, 0))],
        out_specs=pl.BlockSpec((tp, LANES), lambda i: (i, 0)),
        out_shape=jax.ShapeDtypeStruct((T, LANES), I32),
        scratch_shapes=[pltpu.VMEM((1, LANES), F32), pltpu.VMEM((1, LANES), F32)],
        compiler_params=pltpu.CompilerParams(dimension_semantics=("arbitrary",)),
        name="route_plan",
    )(eidx, cnt)


MOE_BM = 512
MOE_NW = A_ROWS // MOE_BM + N_EXP
XS_ROWS = MOE_NW * MOE_BM
MOE_FF_SLICES = 2


def _expert_kernel(we_ref, wb_ref, wv_ref, wfe_ref, wsl_ref, wnx_ref,
                   x_ref, wgu_hbm, bgu_ref, wdn_hbm, bdn_ref, o_ref,
                   wgu_f32, wdn_f32, wgu_bf, wdn_bf, sems):
    w = pl.program_id(0)
    e = we_ref[w]

    def weight_copies(expert, slot):
        return (pltpu.make_async_copy(wgu_hbm.at[expert], wgu_f32.at[slot], sems.at[slot, 0]),
                pltpu.make_async_copy(wdn_hbm.at[expert], wdn_f32.at[slot], sems.at[slot, 1]))

    @pl.when(w == 0)
    def _():
        for cp_ in weight_copies(e, 0):
            cp_.start()

    @pl.when(wfe_ref[w] == 1)
    def _():
        slot = wsl_ref[w]
        for cp_ in weight_copies(e, slot):
            cp_.wait()
        wgu_bf[...] = wgu_f32[slot].astype(BF16)
        wdn_bf[...] = wdn_f32[slot].astype(BF16)
        nxt = wnx_ref[w]

        @pl.when(nxt >= 0)
        def _():
            for cp_ in weight_copies(nxt, 1 - slot):
                cp_.start()

    def ffn(rows):
        xlo, xhi = _unpack_bf16_pair(x_ref[rows, :])
        xb = jnp.concatenate([xlo.astype(BF16), xhi.astype(BF16)], axis=1)
        bgu = bgu_ref[pl.ds(e, 1), :]
        hw = D_FF // MOE_FF_SLICES
        gus = []
        for hf in range(MOE_FF_SLICES):
            gc = slice(hf * hw, (hf + 1) * hw)
            uc = slice(D_FF + hf * hw, D_FF + (hf + 1) * hw)
            gus.append((_dot(xb, wgu_bf[:, gc]) + bgu[:, gc], _dot(xb, wgu_bf[:, uc]) + bgu[:, uc]))
        out = bdn_ref[pl.ds(e, 1), :]
        for hf, (g, u) in enumerate(gus):
            gate = jnp.minimum(g, SW_LIMIT)
            up = jnp.clip(u, -SW_LIMIT, SW_LIMIT)
            act = (up + 1.0) * (gate * jax.nn.sigmoid(SW_ALPHA * gate))
            out = out + _dot(act.astype(BF16), wdn_bf[hf * hw:(hf + 1) * hw, :])
        o_ref[rows, :] = _pack_bf16_pair(out)

    @pl.when(wv_ref[w] == 1)
    def _():
        ffn(slice(None))

    @pl.when(wv_ref[w] == 2)
    def _():
        ffn(slice(0, MOE_BM // 2))


def _experts(meta, xs, w_gu, b_gu, w_down, b_down):
    return pl.pallas_call(
        _expert_kernel,
        grid_spec=pltpu.PrefetchScalarGridSpec(
            num_scalar_prefetch=len(meta),
            grid=(MOE_NW,),
            in_specs=[
                pl.BlockSpec((MOE_BM, D // 2), lambda w, we, wb, *_: (wb[w], 0)),
                pl.BlockSpec(memory_space=pl.ANY),
                pl.BlockSpec((N_EXP, 2 * D_FF), lambda w, *_: (0, 0)),
                pl.BlockSpec(memory_space=pl.ANY),
                pl.BlockSpec((N_EXP, D), lambda w, *_: (0, 0)),
            ],
            out_specs=pl.BlockSpec((MOE_BM, D // 2), lambda w, we, wb, *_: (wb[w], 0)),
            scratch_shapes=[
                pltpu.VMEM((2, D, 2 * D_FF), F32),
                pltpu.VMEM((2, D_FF, D), F32),
                pltpu.VMEM((D, 2 * D_FF), BF16),
                pltpu.VMEM((D_FF, D), BF16),
                pltpu.SemaphoreType.DMA((2, 2)),
            ],
        ),
        out_shape=jax.ShapeDtypeStruct((XS_ROWS, D // 2), U32),
        compiler_params=pltpu.CompilerParams(
            dimension_semantics=("arbitrary",),
            vmem_limit_bytes=56 * MIB),
        name="moe_experts",
    )(*meta, xs, w_gu, b_gu, w_down, b_down)


def _work_items(counts):
    n_e = (counts + MOE_BM - 1) // MOE_BM
    item_end = jnp.cumsum(n_e)
    total = item_end[-1]
    w = jnp.arange(MOE_NW, dtype=I32)
    valid = w < total
    wc = jnp.minimum(w, total - 1)
    e_w = jnp.sum((item_end[None, :] <= wc[:, None]).astype(I32), axis=1)
    e_w = jnp.minimum(e_w, N_EXP - 1)
    rows_here = counts[e_w] - (wc - (item_end - n_e)[e_w]) * MOE_BM
    valid = jnp.where(valid, jnp.where(rows_here <= MOE_BM // 2, 2, 1), 0)
    prev_e = jnp.concatenate([jnp.full((1,), -1, I32), e_w[:-1]])
    fe = (e_w != prev_e).astype(I32)
    slot = (jnp.cumsum(fe) - 1) % 2
    first_at = jnp.where(fe == 1, w, MOE_NW)
    next_first = jnp.concatenate([lax.cummin(first_at, reverse=True)[1:],
                                  jnp.full((1,), MOE_NW, I32)])
    nxt = jnp.where(next_first < MOE_NW, e_w[jnp.minimum(next_first, MOE_NW - 1)], -1)
    return tuple(a.astype(I32) for a in (e_w, wc, valid, fe, slot, nxt))


COMB_TM = 512
SC_CORES = 2
SC_SUBCORES = 16
SC_WORKERS = SC_CORES * SC_SUBCORES
SC_CH = 64
COMB_GROUPS = 4
COMB_TG = T // COMB_GROUPS
SC_ROWS_PER_W = COMB_TG * TOP_K // SC_WORKERS
SC_NCH = SC_ROWS_PER_W // SC_CH


def _sc_gather(table, idx3):
    mesh = plsc.VectorSubcoreMesh(core_axis_name="c", subcore_axis_name="s")

    @functools.partial(
        pl.kernel, mesh=mesh,
        out_type=jax.ShapeDtypeStruct((COMB_TG * TOP_K, D // 2), U32),
        scratch_types=[
            pltpu.VMEM((SC_NCH, SC_CH), I32),
            pltpu.VMEM((2, SC_CH, D // 2), U32),
            pltpu.SemaphoreType.DMA((2,)),
            pltpu.SemaphoreType.DMA((2,)),
        ],
    )
    def k(table_hbm, idx_hbm, out_hbm, idx_v, rows_v, gsem, psem):
        wid = lax.axis_index("s") * SC_CORES + lax.axis_index("c")
        base = wid * SC_ROWS_PER_W
        pltpu.sync_copy(idx_hbm.at[wid], idx_v)

        def gather(j, b):
            return pltpu.make_async_copy(table_hbm.at[idx_v.at[j]], rows_v.at[b], gsem.at[b])

        def put(j, b):
            return pltpu.make_async_copy(rows_v.at[b], out_hbm.at[pl.ds(base + j * SC_CH, SC_CH)],
                                         psem.at[b])

        gather(0, 0).start()

        @pl.loop(0, SC_NCH, step=2)
        def _(j0):
            for b in range(2):
                j = j0 + b

                @pl.when(j + 1 < SC_NCH)
                def _():
                    @pl.when(j >= 1)
                    def _():
                        put(j - 1, 1 - b).wait()
                    gather(j + 1, 1 - b).start()

                gather(j, b).wait()
                put(j, b).start()

        put(SC_NCH - 2, 0).wait()
        put(SC_NCH - 1, 1).wait()

    return k(table, idx3)


SCD_TOK_PER_W = T // SC_WORKERS
SCD_NCH = SCD_TOK_PER_W // SC_CH


def _sc_dispatch(h1p, idx4):
    mesh = plsc.VectorSubcoreMesh(core_axis_name="c", subcore_axis_name="s")

    @functools.partial(
        pl.kernel, mesh=mesh,
        out_type=jax.ShapeDtypeStruct((XS_ROWS, D // 2), U32),
        scratch_types=[
            pltpu.VMEM((SCD_NCH * TOP_K, SC_CH), I32),
            pltpu.VMEM((2, SC_CH, D // 2), U32),
            pltpu.SemaphoreType.DMA((2,)),
            pltpu.SemaphoreType.DMA((2,)),
        ],
    )
    def k(h_hbm, idx_hbm, xs_hbm, idx_v, rows_v, gsem, psem):
        wid = lax.axis_index("s") * SC_CORES + lax.axis_index("c")
        base = wid * SCD_TOK_PER_W
        pltpu.sync_copy(idx_hbm.at[wid], idx_v)

        def get(c, b):
            return pltpu.make_async_copy(h_hbm.at[pl.ds(base + c * SC_CH, SC_CH)], rows_v.at[b],
                                         gsem.at[b])

        def puts(c, b):
            return [pltpu.make_async_copy(rows_v.at[b], xs_hbm.at[idx_v.at[c * TOP_K + kk]],
                                          psem.at[b]) for kk in range(TOP_K)]

        get(0, 0).start()

        @pl.loop(0, SCD_NCH, step=2)
        def _(c0):
            for b in range(2):
                c = c0 + b

                @pl.when(c + 1 < SCD_NCH)
                def _():
                    @pl.when(c >= 1)
                    def _():
                        for cp_ in puts(c - 1, 1 - b):
                            cp_.wait()
                    get(c + 1, 1 - b).start()

                get(c, b).wait()
                for cp_ in puts(c, b):
                    cp_.start()

        for cp_ in puts(SCD_NCH - 2, 0) + puts(SCD_NCH - 1, 1):
            cp_.wait()

    return k(h1p, idx4)


def _combine_dense_kernel(g_ref, h1_ref, tw_ref, lg_ref, lb_ref, o_ref):
    tw = tw_ref[...]
    ylo = jnp.zeros((COMB_TM, D // 2), F32)
    yhi = jnp.zeros((COMB_TM, D // 2), F32)
    for k in range(TOP_K):
        lo, hi = _unpack_bf16_pair(g_ref[k])
        wk = tw[:, k:k + 1]
        ylo = ylo + lo * wk
        yhi = yhi + hi * wk
    ff = jnp.concatenate([ylo, yhi], axis=1)
    o_ref[...] = _ln(DN_ALPHA * h1_ref[...] + ff, lg_ref[...], lb_ref[...])


def _combine_dense(g4, h1, topw, g, b, group):
    tm = COMB_TM
    t0 = group * (COMB_TG // tm)
    return pl.pallas_call(
        _combine_dense_kernel,
        grid=(COMB_TG // tm,),
        in_specs=[
            pl.BlockSpec((TOP_K, tm, D // 2), lambda i: (0, i, 0)),
            pl.BlockSpec((tm, D), lambda i: (t0 + i, 0)),
            pl.BlockSpec((tm, LANES), lambda i: (t0 + i, 0)),
            pl.BlockSpec((1, D), lambda i: (0, 0)),
            pl.BlockSpec((1, D), lambda i: (0, 0)),
        ],
        out_specs=pl.BlockSpec((tm, D), lambda i: (t0 + i, 0)),
        out_shape=jax.ShapeDtypeStruct((T, D), F32),
        input_output_aliases={1: 0},
        compiler_params=pltpu.CompilerParams(
            dimension_semantics=("arbitrary",),
            vmem_limit_bytes=40 * MIB),
        name="moe_combine_dense_ln2",
    )(g4, h1, topw, g, b)


def _pad_cols(a, n):
    return jnp.pad(a, ((0, 0), (0, n - a.shape[1])))


def kernel(x, mem, ln_in_g, ln_in_b, ln_mem_g, ln_mem_b, w_in, b_in, w_decay_f, b_decay_f,
           w_decay_b, b_decay_b, gla_norm_g, w_br_gla, w_br_fnet, w_br_mem, w_mem_kv, w_out,
           b_out, ln1_g, ln1_b, w_router, b_router, w_gu, b_gu, w_down, b_down, ln2_g, ln2_b):
    assert x.shape == (BATCH, SEQ, D) and w_in.shape[0] == 1
    row = lambda a: a.reshape(1, -1)
    x2 = x.reshape(T, D)
    w_in0, b_in0 = w_in[0], b_in[0]
    c_lr, c_fn, c_mq, c_gt = 3072, 3072 + 2 * GLA_LR, 3104 + FN_W, 3104 + FN_W + MQ_W
    w_main = jnp.concatenate([w_in0[:, :c_lr], w_in0[:, c_gt:]], axis=1).astype(BF16)
    b_main = row(jnp.concatenate([b_in0[:c_lr], b_in0[c_gt:]]))
    w_lr = _pad_cols(w_in0[:, c_lr:c_fn], LANES).astype(BF16)
    b_lr = _pad_cols(row(b_in0[c_lr:c_fn]), LANES)
    w_mq = w_in0[:, c_mq:c_gt].astype(BF16)
    b_mq = row(b_in0[c_mq:c_gt])
    w_fn = w_in0[:, c_fn:c_mq].astype(BF16)
    b_fn = row(b_in0[c_fn:c_mq])
    lng, lnb = row(ln_in_g), row(ln_in_b)

    proj, mq, lr = _inproj(x2, lng, lnb, w_main, b_main, w_mq, b_mq, w_lr, b_lr)

    zpad = jnp.zeros((LANES - 2 * GLA_LR, GLA_H * GLA_DK), F32)
    zlr = jnp.zeros((GLA_LR, GLA_H * GLA_DK), F32)
    wdf = jnp.concatenate([w_decay_f[0], zlr, zpad], axis=0).reshape(LANES, GLA_H, GLA_DK)
    wdb = jnp.concatenate([zlr, w_decay_b[0], zpad], axis=0).reshape(LANES, GLA_H, GLA_DK)
    wd = jnp.concatenate([wdf, wdb], axis=2).reshape(LANES, GLA_H * 2 * GLA_DK).astype(BF16)
    wd = jnp.concatenate([wd, wd], axis=0)
    bd = jnp.concatenate([b_decay_f[0].reshape(GLA_H, GLA_DK),
                          b_decay_b[0].reshape(GLA_H, GLA_DK)], axis=1).reshape(1, -1)
    og = _gla(proj, lr, wd, bd, row(gla_norm_g[0]))

    fbig, cwt, swt, f2, ccs, perm = _dft_tables(MERGE_TM)
    x4 = x.reshape(BATCH, FFT_N2, FFT_N1, D)
    z = _fft2(_fft1(x4, lng, lnb, w_fn, b_fn, fbig, cwt, swt), f2)

    kv = _memkv(mem.reshape(BATCH * MEM_LEN, D), row(ln_mem_g), row(ln_mem_b),
                w_mem_kv[0].astype(BF16))

    w_r = _pad_cols(w_router[0], LANES)
    wr_hi = w_r.astype(BF16)
    wr_lo = (w_r - wr_hi.astype(F32)).astype(BF16)
    b_r = jnp.concatenate([row(b_router[0]),
                           jnp.full((1, LANES - N_EXP), NEG_BIG, F32)], axis=1)
    h1, h1p, eidx, topw, cnt = _merge(
        x2, og, z, mq, proj, kv, lng, lnb,
        w_br_gla[0].astype(BF16), ccs, perm, w_br_fnet[0].astype(BF16),
        w_br_mem[0].astype(BF16), w_out[0].astype(BF16), row(b_out[0]),
        row(ln1_g[0]), row(ln1_b[0]), jnp.concatenate([wr_hi, wr_lo], axis=1), wr_hi, b_r)

    dest = _plan(eidx, cnt)
    counts = cnt[0, :N_EXP].astype(I32)
    dest_k = dest[:, :TOP_K]
    idx4 = dest_k.reshape(SC_WORKERS, SCD_NCH, SC_CH, TOP_K).transpose(0, 1, 3, 2)
    xs = _sc_dispatch(h1p, idx4.reshape(SC_WORKERS, SCD_NCH * TOP_K, SC_CH))
    ys = _experts(_work_items(counts), xs, w_gu[0], b_gu[0], w_down[0], b_down[0])
    out = h1
    for grp in range(COMB_GROUPS):
        dest_g = dest_k[grp * COMB_TG:(grp + 1) * COMB_TG].T.reshape(SC_WORKERS, SC_NCH, SC_CH)
        g4 = _sc_gather(ys, dest_g).reshape(TOP_K, COMB_TG, D // 2)
        out = _combine_dense(g4, out, topw, row(ln2_g[0]), row(ln2_b[0]), grp)
    return out.reshape(BATCH, SEQ, D)
```

```python
import functools
import math

import numpy as np
import jax
import jax.numpy as jnp
from jax import lax
from jax.experimental import pallas as pl
from jax.experimental.pallas import tpu as pltpu
from jax.experimental.pallas import tpu_sc as plsc

F32 = jnp.float32
BF16 = jnp.bfloat16
I32 = jnp.int32
U32 = jnp.uint32

D = 1024
BATCH = 4
SEQ = 4096
T = BATCH * SEQ
GLA_H = 4
GLA_DK = 128
GLA_DV = 256
GLA_LR = 16
GLA_TAU = 16.0
GLA_C = 64
FN_G = 4
FN_GW = 128
FN_W = 512
MEM_LEN = 256
MEM_H = 4
MEM_HD = 128
MQ_W = 512
N_EXP = 32
TOP_K = 4
D_FF = 1024
SW_LIMIT = 7.0
SW_ALPHA = 1.702
LN_EPS = 1e-5
RMS_EPS = 1e-6
DN_ALPHA = 2.0 ** 0.25
A_ROWS = T * TOP_K

FFT_N1 = 128
FFT_N2 = 32

LANES = 128
NEG_BIG = -1e30
MIB = 1024 * 1024


def _ln(x, g, b):
    mu = jnp.mean(x, axis=-1, keepdims=True)
    xc = x - mu
    var = jnp.mean(xc * xc, axis=-1, keepdims=True)
    return xc * lax.rsqrt(var + LN_EPS) * g + b


def _dot(a, b):
    return jnp.dot(a, b, preferred_element_type=F32)


def _dot_nt(a, b):
    return lax.dot_general(a, b, (((1,), (1,)), ((), ())), preferred_element_type=F32)


def _dot_tn(a, b):
    return lax.dot_general(a, b, (((0,), (0,)), ((), ())), preferred_element_type=F32)


def _interleave(*stages):
    live = list(stages)
    while live:
        for st in list(live):
            try:
                next(st)
            except StopIteration:
                live.remove(st)


def _split_bf16(a):
    hi = a.astype(BF16)
    return hi, (a - hi.astype(F32)).astype(BF16)


INPROJ_TM = 1024
INPROJ_TN = 3072


PROJ_W = 6 * 1024


def _inproj_kernel(x_ref, g_ref, b_ref, w_ref, bias_ref, wmq_ref, bmq_ref, wlr_ref, blr_ref,
                   proj_ref, mq_ref, lr_ref, hb_ref):
    @pl.when(pl.program_id(1) == 0)
    def _():
        hb = _ln(x_ref[...], g_ref[...], b_ref[...]).astype(BF16)
        hb_ref[...] = hb
        lr_ref[...] = _dot(hb, wlr_ref[...]) + blr_ref[...]
        mq_ref[...] = (_dot(hb, wmq_ref[...]) + bmq_ref[...]).astype(BF16)

    proj_ref[...] = (_dot(hb_ref[...], w_ref[...]) + bias_ref[...]).astype(BF16)


def _inproj(x2, ln_g, ln_b, w_main, b_main, w_mq, b_mq, w_lr, b_lr):
    tm, tn = INPROJ_TM, INPROJ_TN
    nj = PROJ_W // tn
    row = lambda i, j: (i, 0)
    const = lambda i, j: (0, 0)
    outs = (
        jax.ShapeDtypeStruct((T, PROJ_W), BF16),
        jax.ShapeDtypeStruct((T, MQ_W), BF16),
        jax.ShapeDtypeStruct((T, LANES), F32),
    )
    return pl.pallas_call(
        _inproj_kernel,
        grid=(T // tm, nj),
        in_specs=[
            pl.BlockSpec((tm, D), row),
            pl.BlockSpec((1, D), const),
            pl.BlockSpec((1, D), const),
            pl.BlockSpec((D, tn), lambda i, j: (0, j)),
            pl.BlockSpec((1, tn), lambda i, j: (0, j)),
            pl.BlockSpec((D, MQ_W), const),
            pl.BlockSpec((1, MQ_W), const),
            pl.BlockSpec((D, LANES), const),
            pl.BlockSpec((1, LANES), const),
        ],
        out_specs=[
            pl.BlockSpec((tm, tn), lambda i, j: (i, j)),
            pl.BlockSpec((tm, MQ_W), row),
            pl.BlockSpec((tm, LANES), row),
        ],
        out_shape=outs,
        scratch_shapes=[pltpu.VMEM((tm, D), BF16)],
        compiler_params=pltpu.CompilerParams(
            dimension_semantics=("arbitrary", "arbitrary"),
            vmem_limit_bytes=48 * MIB),
        name="ln_inproj",
    )(x2, ln_g, ln_b, w_main, b_main, w_mq, b_mq, w_lr, b_lr)


GLA_BULK = 256
GLA_FIN = 512
GLA_NCH = SEQ // GLA_C
GLA_CPB = GLA_BULK // GLA_C
GLA_PIPE = 4


def _gla_kernel(q_ref, k_ref, v_ref, r_ref, lr_ref, wd_ref, bd_ref, g_ref, cs_ref, o_ref,
                acc_ref, qin_ref, kin_ref, kst_ref, dec_ref, u_ref, stf_ref, stb_ref):
    C = GLA_C
    G = GLA_BULK
    DK = GLA_DK
    NG = SEQ // G
    scale = DK ** -0.5
    ii = lax.broadcasted_iota(I32, (G, G), 0)
    jj = lax.broadcasted_iota(I32, (G, G), 1)
    same = (ii // C) == (jj // C)
    lower = jnp.logical_and(same, ii >= jj)
    upper = jnp.logical_and(same, ii <= jj)
    is_fwd = lax.broadcasted_iota(I32, (G, 2 * DK), 1) < DK
    chunk_of_row = lax.broadcasted_iota(I32, (G, DK), 0) // C

    def stage_a(gi):
        rows = pl.ds(pl.multiple_of(gi * G, G), G)
        z = _dot(jnp.concatenate(_split_bf16(lr_ref[rows, :]), axis=1), wd_ref[...]) + bd_ref[...]
        yield
        la = -(jnp.maximum(-z, 0.0) + jnp.log(1.0 + jnp.exp(-jnp.abs(z)))) * (1.0 / GLA_TAU)
        la_hi, la_lo = _split_bf16(la)
        pre2 = _dot(cs_ref[...], jnp.concatenate([la_hi, la_lo], axis=1))
        yield
        pre = pre2[:, :2 * DK] + pre2[:, 2 * DK:]
        blast = jnp.concatenate(
            [jnp.broadcast_to(pre[ci * C + C - 1:ci * C + C, :], (C, 2 * DK))
             for ci in range(GLA_CPB)], axis=0)
        b = jnp.where(is_fwd, pre, blast - pre + la)
        qf32 = q_ref[rows, :].astype(F32)
        kf32 = k_ref[rows, :].astype(F32)
        q2 = jnp.concatenate([qf32, qf32], axis=1)
        k2 = jnp.concatenate([kf32, kf32], axis=1)
        qin_ref[rows, :] = (q2 * (scale * jnp.exp(b))).astype(BF16)
        kin_ref[rows, :] = (k2 * jnp.exp(-b)).astype(BF16)
        kst_ref[rows, :] = (k2 * jnp.exp(blast - b)).astype(BF16)
        dec = jnp.exp(blast)
        for ci in range(GLA_CPB):
            dec_ref[pl.ds(gi * GLA_CPB + ci, 1), :] = dec[ci * C:ci * C + 1, :]

    def stage_b(gi):
        rows = pl.ds(pl.multiple_of(gi * G, G), G)
        qi = qin_ref[rows, :]
        ki = kin_ref[rows, :]
        ks = kst_ref[rows, :]
        vb = v_ref[rows, :]
        att_f = _dot_nt(qi[:, :DK], ki[:, :DK])
        att_b = _dot_nt(qi[:, DK:], ki[:, DK:])
        yield
        att = jnp.where(lower, att_f, 0.0) + jnp.where(upper, att_b, 0.0)
        acc_ref[rows, :] = _dot(att.astype(BF16), vb)
        yield
        ksb = jnp.concatenate(
            [jnp.where(chunk_of_row == ci, ks[:, d * DK:(d + 1) * DK], jnp.zeros((G, DK), BF16))
             for d in range(2) for ci in range(GLA_CPB)], axis=1)
        u = _dot_tn(vb, ksb)
        for d in range(2):
            for ci in range(GLA_CPB):
                col = (d * GLA_CPB + ci) * DK
                u_ref[d, gi * GLA_CPB + ci] = u[:, col:col + DK]

    P = GLA_PIPE
    _interleave(*[stage_a(j) for j in range(P)])

    def bulk(i, carry):
        g = P * i
        _interleave(*[st for j in range(P) for st in (stage_b(g - P + j), stage_a(g + j))])
        return carry

    lax.fori_loop(1, NG // P, bulk, 0)
    _interleave(*[stage_b(NG - P + j) for j in range(P)])

    stf_ref[...] = jnp.zeros_like(stf_ref)
    stb_ref[...] = jnp.zeros_like(stb_ref)

    def one(n, d, st_ref):
        lanes = slice(d * DK, (d + 1) * DK)
        rows = pl.ds(pl.multiple_of(n * C, C), C)
        st = st_ref[...]
        acc_ref[rows, :] += _dot_nt(qin_ref[rows, lanes], st.astype(BF16))
        st_ref[...] = st * dec_ref[pl.ds(n, 1), :][:, lanes] + u_ref[d, n]

    def step(i, carry):
        one(i, 0, stf_ref)
        one(GLA_NCH - 1 - i, 1, stb_ref)
        return carry

    lax.fori_loop(0, GLA_NCH, step, 0, unroll=16)

    def fin(gi, carry):
        rows = pl.ds(pl.multiple_of(gi * GLA_FIN, GLA_FIN), GLA_FIN)
        o = acc_ref[rows, :]
        o = o * lax.rsqrt(jnp.mean(o * o, axis=-1, keepdims=True) + RMS_EPS) * g_ref[...]
        rg = r_ref[rows, :].astype(F32)
        o_ref[rows, :] = (o * (rg * jax.nn.sigmoid(rg))).astype(BF16)
        return carry

    lax.fori_loop(0, SEQ // GLA_FIN, fin, 0)


def _gla(proj, lr, wd, bd, g):
    i = np.arange(GLA_BULK)
    cs = ((i[:, None] // GLA_C) == (i[None, :] // GLA_C)) & (i[:, None] >= i[None, :])
    cs = jnp.asarray(cs, dtype=F32).astype(BF16)
    v_blk = 1024 // GLA_DV
    return pl.pallas_call(
        _gla_kernel,
        grid=(BATCH, GLA_H),
        in_specs=[
            pl.BlockSpec((SEQ, GLA_DK), lambda b, h: (b, h)),
            pl.BlockSpec((SEQ, GLA_DK), lambda b, h: (b, GLA_H + h)),
            pl.BlockSpec((SEQ, GLA_DV), lambda b, h: (b, v_blk + h)),
            pl.BlockSpec((SEQ, GLA_DV), lambda b, h: (b, 2 * v_blk + h)),
            pl.BlockSpec((SEQ, LANES), lambda b, h: (b, 0)),
            pl.BlockSpec((2 * LANES, 2 * GLA_DK), lambda b, h: (0, h)),
            pl.BlockSpec((1, 2 * GLA_DK), lambda b, h: (0, h)),
            pl.BlockSpec((1, GLA_DV), lambda b, h: (0, 0)),
            pl.BlockSpec((GLA_BULK, GLA_BULK), lambda b, h: (0, 0)),
        ],
        out_specs=pl.BlockSpec((SEQ, GLA_DV), lambda b, h: (b, h)),
        out_shape=jax.ShapeDtypeStruct((T, GLA_H * GLA_DV), BF16),
        scratch_shapes=[
            pltpu.VMEM((SEQ, GLA_DV), F32),
            pltpu.VMEM((SEQ, 2 * GLA_DK), BF16),
            pltpu.VMEM((SEQ, 2 * GLA_DK), BF16),
            pltpu.VMEM((SEQ, 2 * GLA_DK), BF16),
            pltpu.VMEM((GLA_NCH, 2 * GLA_DK), F32),
            pltpu.VMEM((2, GLA_NCH, GLA_DV, GLA_DK), F32),
            pltpu.VMEM((GLA_DV, GLA_DK), F32),
            pltpu.VMEM((GLA_DV, GLA_DK), F32),
        ],
        compiler_params=pltpu.CompilerParams(
            dimension_semantics=("arbitrary", "arbitrary"),
            vmem_limit_bytes=58 * MIB),
        name="gla",
    )(proj, proj, proj, proj, lr, wd, bd, g, cs)


FFT1_S = 16
FFT1_SUB = 2
FFT1_ROWS = FFT_N2 * FFT1_S
FFT2_KB = 8


def _fft1_kernel(x_ref, g_ref, b_ref, w_ref, bias_ref, fbig_ref, cw_ref, sw_ref, o_ref):
    sh = FFT1_S // FFT1_SUB
    rows = FFT_N2 * sh
    res = {}

    def sub(h):
        xv = x_ref[:, h * sh:(h + 1) * sh, :].reshape(rows, D)
        hb = _ln(xv, g_ref[...], b_ref[...]).astype(BF16)
        fn = (_dot(hb, w_ref[...]) + bias_ref[...]).astype(BF16)
        yield
        a = _dot(fbig_ref[...], fn)
        yield
        ar = a[:rows]
        ai = a[rows:]
        cw = jnp.concatenate([cw_ref[h]] * (FN_W // LANES), axis=1)
        sw = jnp.concatenate([sw_ref[h]] * (FN_W // LANES), axis=1)
        res[h] = ((ar * cw + ai * sw).reshape(FFT_N2, sh, FN_W),
                  (ai * cw - ar * sw).reshape(FFT_N2, sh, FN_W))

    _interleave(*[sub(h) for h in range(FFT1_SUB)])
    for ri in range(2):
        o_ref[ri] = jnp.concatenate([res[h][ri] for h in range(FFT1_SUB)], axis=1).astype(BF16)


def _fft1(x4, ln_g, ln_b, w_fn, b_fn, fbig, cwt, swt):
    s = FFT1_S
    const = lambda b, j: (0, 0)
    return pl.pallas_call(
        _fft1_kernel,
        grid=(BATCH, FFT_N1 // s),
        in_specs=[
            pl.BlockSpec((None, FFT_N2, s, D), lambda b, j: (b, 0, j, 0)),
            pl.BlockSpec((1, D), const),
            pl.BlockSpec((1, D), const),
            pl.BlockSpec((D, FN_W), const),
            pl.BlockSpec((1, FN_W), const),
            pl.BlockSpec((2 * FFT1_ROWS // FFT1_SUB, FFT1_ROWS // FFT1_SUB), const),
            pl.BlockSpec((FFT1_SUB, FFT1_ROWS // FFT1_SUB, LANES), lambda b, j: (j, 0, 0)),
            pl.BlockSpec((FFT1_SUB, FFT1_ROWS // FFT1_SUB, LANES), lambda b, j: (j, 0, 0)),
        ],
        out_specs=pl.BlockSpec((None, 2, FFT_N2, s, FN_W), lambda b, j: (b, 0, 0, j, 0)),
        out_shape=jax.ShapeDtypeStruct((BATCH, 2, FFT_N2, FFT_N1, FN_W), BF16),
        compiler_params=pltpu.CompilerParams(
            dimension_semantics=("arbitrary", "arbitrary"),
            vmem_limit_bytes=40 * MIB),
        name="fft_stage1",
    )(x4, ln_g, ln_b, w_fn, b_fn, fbig, cwt, swt)


def _fft2_kernel(d_ref, f2_ref, o_ref):
    f2 = f2_ref[...]
    for kk in range(FFT2_KB):
        z = _dot(f2, jnp.concatenate([d_ref[0, kk], d_ref[1, kk]], axis=0))
        o_ref[0, kk] = z[:FFT_N1].astype(BF16)
        o_ref[1, kk] = z[FFT_N1:].astype(BF16)


def _fft2(dmat, f2):
    kb = FFT2_KB
    blk = (None, 2, kb, FFT_N1, FN_W)
    return pl.pallas_call(
        _fft2_kernel,
        grid=(BATCH, FFT_N2 // kb),
        in_specs=[
            pl.BlockSpec(blk, lambda b, j: (b, 0, j, 0, 0)),
            pl.BlockSpec((2 * FFT_N1, 2 * FFT_N1), lambda b, j: (0, 0)),
        ],
        out_specs=pl.BlockSpec(blk, lambda b, j: (b, 0, j, 0, 0)),
        out_shape=jax.ShapeDtypeStruct((BATCH, 2, FFT_N2, FFT_N1, FN_W), BF16),
        compiler_params=pltpu.CompilerParams(
            dimension_semantics=("arbitrary", "arbitrary")),
        name="fft_stage2",
    )(dmat, f2)


def _dft_tables(merge_tm):
    s = FFT1_S // FFT1_SUB
    n2 = np.arange(FFT_N2, dtype=np.float64)
    n1 = np.arange(FFT_N1, dtype=np.float64)
    th = 2.0 * np.pi * np.outer(n2, n2) / FFT_N2
    f1 = np.stack([np.cos(th), -np.sin(th)]) / math.sqrt(SEQ)
    fbig = np.einsum("rkn,st->rksnt", f1, np.eye(s)).reshape(2 * FFT_N2 * s, FFT_N2 * s)
    tw = 2.0 * np.pi * np.outer(n2, n1) / SEQ
    tw = tw.reshape(FFT_N2, FFT_N1 // s, s).transpose(1, 0, 2).reshape(FFT_N1 // s, FFT_N2 * s)
    cwt = np.broadcast_to(np.cos(tw)[:, :, None], tw.shape + (LANES,))
    swt = np.broadcast_to(np.sin(tw)[:, :, None], tw.shape + (LANES,))
    th1 = 2.0 * np.pi * np.outer(n1, n1) / FFT_N1
    c1, s1 = np.cos(th1), np.sin(th1)
    f2 = np.block([[c1, s1], [-s1, c1]])
    cc = np.arange(FN_GW, dtype=np.float64)
    thc = 2.0 * np.pi * np.outer(cc, cc) / FN_GW
    ccs = np.concatenate([np.cos(thc), np.sin(thc)], axis=0) / math.sqrt(FN_GW)
    k1n = merge_tm // FFT_N2
    r = np.arange(merge_tm)
    perm = np.zeros((merge_tm, merge_tm))
    perm[r, (r % FFT_N2) * k1n + r // FFT_N2] = 1.0
    as32 = lambda a: jnp.asarray(np.ascontiguousarray(a), dtype=F32)
    return (as32(fbig).astype(BF16), as32(cwt), as32(swt), as32(f2).astype(BF16),
            as32(ccs).astype(BF16), as32(perm).astype(BF16))


def _memkv_kernel(m_ref, g_ref, b_ref, w_ref, o_ref):
    mn = _ln(m_ref[...], g_ref[...], b_ref[...]).astype(BF16)
    o_ref[...] = _dot(mn, w_ref[...]).astype(BF16)


def _memkv(mem2, g, b, w):
    return pl.pallas_call(
        _memkv_kernel,
        grid=(BATCH,),
        in_specs=[
            pl.BlockSpec((MEM_LEN, D), lambda i: (i, 0)),
            pl.BlockSpec((1, D), lambda i: (0, 0)),
            pl.BlockSpec((1, D), lambda i: (0, 0)),
            pl.BlockSpec((D, 2 * MQ_W), lambda i: (0, 0)),
        ],
        out_specs=pl.BlockSpec((MEM_LEN, 2 * MQ_W), lambda i: (i, 0)),
        out_shape=jax.ShapeDtypeStruct((BATCH * MEM_LEN, 2 * MQ_W), BF16),
        compiler_params=pltpu.CompilerParams(dimension_semantics=("arbitrary",)),
        name="mem_kv",
    )(mem2, g, b, w)


MERGE_TM = 512
MERGE_K1 = MERGE_TM // FFT_N2


def _pack_bf16_pair(v):
    n = v.shape[1] // 2
    bits = lax.bitcast_convert_type(v.astype(BF16).astype(F32), U32)
    return (bits[:, n:] & jnp.uint32(0xFFFF0000)) | (bits[:, :n] >> 16)


def _unpack_bf16_pair(p):
    lo = lax.bitcast_convert_type(p << 16, F32)
    hi = lax.bitcast_convert_type(p & jnp.uint32(0xFFFF0000), F32)
    return lo, hi


def _merge_kernel(x_ref, og_ref, zr_ref, zi_ref, mq_ref, gt_ref, kv_ref,
                  lng_ref, lnb_ref, wg_ref, ccs_ref, perm_ref, wf_ref, wm_ref, wo_ref, bo_ref,
                  l1g_ref, l1b_ref, wr2_ref, wrh_ref, br_ref,
                  h1_ref, h1p_ref, eidx_ref, topw_ref, cnt_ref):
    tm = MERGE_TM
    y = {}

    def branch_fnet():
        zr = zr_ref[...].reshape(tm, FN_W)
        zi = zi_ref[...].reshape(tm, FN_W)
        ys = []
        for g in range(FN_G):
            sl = slice(g * FN_GW, (g + 1) * FN_GW)
            ys.append(_dot(jnp.concatenate([zr[:, sl], zi[:, sl]], axis=1), ccs_ref[...]))
        yield
        yp = _dot(perm_ref[...], jnp.concatenate(ys, axis=1).astype(BF16))
        yield
        y["fnet"] = _dot(yp.astype(BF16), wf_ref[...])

    def branch_mem():
        heads = [slice(hd * MEM_HD, (hd + 1) * MEM_HD) for hd in range(MEM_H)]
        ss = [_dot_nt(mq_ref[:, sl], kv_ref[:, sl]) for sl in heads]
        yield
        oms = []
        for hd, s in enumerate(ss):
            s = s * (MEM_HD ** -0.5)
            s = s - jnp.max(s, axis=-1, keepdims=True)
            p = jnp.exp(s)
            p = p * (1.0 / jnp.sum(p, axis=-1, keepdims=True))
            oms.append(_dot(p.astype(BF16),
                            kv_ref[:, MQ_W + hd * MEM_HD:MQ_W + (hd + 1) * MEM_HD]))
        yield
        y["mem"] = _dot(jnp.concatenate(oms, axis=1).astype(BF16), wm_ref[...])

    def branch_gla():
        y["gla"] = _dot(og_ref[...], wg_ref[...])
        yield

    _interleave(branch_fnet(), branch_mem(), branch_gla())

    def gate(c):
        return 0.5 + 0.5 * jnp.tanh(0.5 * gt_ref[:, c * D:(c + 1) * D].astype(F32))

    merged = gate(0) * y["gla"] + gate(1) * y["fnet"] + gate(2) * y["mem"]
    mix = _dot(merged.astype(BF16), wo_ref[...]) + bo_ref[...]
    h = _ln(x_ref[...], lng_ref[...], lnb_ref[...])
    h1 = _ln(DN_ALPHA * h + mix, l1g_ref[...], l1b_ref[...])
    h1_ref[...] = h1
    h1p_ref[...] = _pack_bf16_pair(h1)

    h_hi, h_lo = _split_bf16(h1)
    d2 = _dot(h_hi, wr2_ref[...])
    l = d2[:, :LANES] + d2[:, LANES:] + _dot(h_lo, wrh_ref[...]) + br_ref[...]
    lane = lax.broadcasted_iota(I32, (tm, LANES), 1)
    vals, idxs = [], []
    for _ in range(TOP_K):
        m = jnp.max(l, axis=-1, keepdims=True)
        idx = jnp.min(jnp.where(l == m, lane, LANES), axis=-1, keepdims=True)
        vals.append(m)
        idxs.append(idx)
        l = jnp.where(lane == idx, -jnp.inf, l)
    es = [jnp.exp(v - vals[0]) for v in vals]
    den = es[0] + es[1] + es[2] + es[3]
    eo = jnp.zeros((tm, LANES), I32)
    wo = jnp.zeros((tm, LANES), F32)
    chosen = jnp.zeros((tm, LANES), F32)
    for k in range(TOP_K):
        eo = jnp.where(lane == k, idxs[k], eo)
        wo = jnp.where(lane == k, es[k] / den, wo)
        chosen = chosen + jnp.where(lane == idxs[k], 1.0, 0.0)
    eidx_ref[...] = eo
    topw_ref[...] = wo

    @pl.when(pl.program_id(0) == 0)
    def _():
        cnt_ref[...] = jnp.zeros_like(cnt_ref)

    cnt_ref[...] += jnp.broadcast_to(jnp.sum(chosen, axis=0, keepdims=True), cnt_ref.shape)


def _merge(x2, og, z, mq, gates, kv, lng, lnb, wg, ccs, perm, wf, wm, wo, bo, l1g, l1b,
           wr2, wrh, br):
    tm = MERGE_TM
    per_b = SEQ // tm
    row = lambda i: (i, 0)
    const = lambda i: (0, 0)
    zblk = (None, None, FFT_N2, MERGE_K1, FN_W)
    outs = (
        jax.ShapeDtypeStruct((T, D), F32),
        jax.ShapeDtypeStruct((T, D // 2), U32),
        jax.ShapeDtypeStruct((T, LANES), I32),
        jax.ShapeDtypeStruct((T, LANES), F32),
        jax.ShapeDtypeStruct((8, LANES), F32),
    )
    return pl.pallas_call(
        _merge_kernel,
        grid=(T // tm,),
        in_specs=[
            pl.BlockSpec((tm, D), row),
            pl.BlockSpec((tm, D), row),
            pl.BlockSpec(zblk, lambda i: (i // per_b, 0, 0, i % per_b, 0)),
            pl.BlockSpec(zblk, lambda i: (i // per_b, 1, 0, i % per_b, 0)),
            pl.BlockSpec((tm, MQ_W), row),
            pl.BlockSpec((tm, 3 * D), lambda i: (i, 1)),
            pl.BlockSpec((MEM_LEN, 2 * MQ_W), lambda i: (i // per_b, 0)),
            pl.BlockSpec((1, D), const), pl.BlockSpec((1, D), const),
            pl.BlockSpec((D, D), const),
            pl.BlockSpec((2 * FN_GW, FN_GW), const),
            pl.BlockSpec((tm, tm), const),
            pl.BlockSpec((FN_W, D), const),
            pl.BlockSpec((MQ_W, D), const),
            pl.BlockSpec((D, D), const),
            pl.BlockSpec((1, D), const),
            pl.BlockSpec((1, D), const), pl.BlockSpec((1, D), const),
            pl.BlockSpec((D, 2 * LANES), const),
            pl.BlockSpec((D, LANES), const),
            pl.BlockSpec((1, LANES), const),
        ],
        out_specs=[
            pl.BlockSpec((tm, D), row),
            pl.BlockSpec((tm, D // 2), row),
            pl.BlockSpec((tm, LANES), row),
            pl.BlockSpec((tm, LANES), row),
            pl.BlockSpec((8, LANES), const),
        ],
        out_shape=outs,
        compiler_params=pltpu.CompilerParams(
            dimension_semantics=("arbitrary",),
            vmem_limit_bytes=58 * MIB),
        name="merge_ln1_router",
    )(x2, og, z, z, mq, gates, kv, lng, lnb, wg, ccs, perm, wf, wm, wo, bo, l1g, l1b,
      wr2, wrh, br)


PLAN_TP = 1024


def _expert_onehots(e, lane):
    onehots = [lane == e[:, k:k + 1] for k in range(TOP_K)]
    mf = jnp.zeros(lane.shape, F32)
    for oh in onehots:
        mf = mf + jnp.where(oh, 1.0, 0.0)
    return onehots, mf


def _plan_kernel(e_ref, tot_ref, dest_ref, cnt_ref, off_ref):
    i = pl.program_id(0)
    tp = PLAN_TP
    lane = lax.broadcasted_iota(I32, (tp, LANES), 1)
    onehots, mf = _expert_onehots(e_ref[...], lane)

    @pl.when(i == 0)
    def _():
        tot = tot_ref[0:1, :]
        padded = jnp.floor((tot + (MOE_BM - 1)) * (1.0 / MOE_BM)) * MOE_BM
        lane1 = lax.broadcasted_iota(I32, (1, LANES), 1)
        inc = padded
        for s in (1, 2, 4, 8, 16, 32, 64):
            inc = inc + jnp.where(lane1 >= s, pltpu.roll(inc, s, 1), 0.0)
        off_ref[...] = inc - padded
        cnt_ref[...] = jnp.zeros_like(cnt_ref)

    ri = lax.broadcasted_iota(I32, (tp, tp), 0)
    ci = lax.broadcasted_iota(I32, (tp, tp), 1)
    ltri = jnp.where(ri > ci, 1.0, 0.0).astype(BF16)
    rank = _dot(ltri, mf.astype(BF16)) + cnt_ref[...] + off_ref[...]
    out = jnp.zeros((tp, LANES), I32)
    for k in range(TOP_K):
        dk = jnp.sum(jnp.where(onehots[k], rank, 0.0), axis=-1, keepdims=True)
        out = jnp.where(lane == k, dk.astype(I32), out)
    dest_ref[...] = out
    cnt_ref[...] += jnp.sum(mf, axis=0, keepdims=True)


def _plan(eidx, cnt):
    tp = PLAN_TP
    return pl.pallas_call(
        _plan_kernel,
        grid=(T // tp,),
        in_specs=[pl.BlockSpec((tp, LANES), lambda i: (i, 0)),
                  pl.BlockSpec((8, LANES), lambda i: (0, 0))],
        out_specs=pl.BlockSpec((tp, LANES), lambda i: (i, 0)),
        out_shape=jax.ShapeDtypeStruct((T, LANES), I32),
        scratch_shapes=[pltpu.VMEM((1, LANES), F32), pltpu.VMEM((1, LANES), F32)],
        compiler_params=pltpu.CompilerParams(dimension_semantics=("arbitrary",)),
        name="route_plan",
    )(eidx, cnt)


MOE_BM = 512
MOE_NW = A_ROWS // MOE_BM + N_EXP
XS_ROWS = MOE_NW * MOE_BM
MOE_FF_SLICES = 2


def _expert_kernel(we_ref, wb_ref, wv_ref, wfe_ref, wsl_ref, wnx_ref,
                   x_ref, wgu_hbm, bgu_ref, wdn_hbm, bdn_ref, o_ref,
                   wgu_f32, wdn_f32, wgu_bf, wdn_bf, sems):
    w = pl.program_id(0)
    e = we_ref[w]

    def weight_copies(expert, slot):
        return (pltpu.make_async_copy(wgu_hbm.at[expert], wgu_f32.at[slot], sems.at[slot, 0]),
                pltpu.make_async_copy(wdn_hbm.at[expert], wdn_f32.at[slot], sems.at[slot, 1]))

    @pl.when(w == 0)
    def _():
        for cp_ in weight_copies(e, 0):
            cp_.start()

    first = wfe_ref[w] == 1
    slot = wsl_ref[w]

    @pl.when(first)
    def _():
        for cp_ in weight_copies(e, slot):
            cp_.wait()
        nxt = wnx_ref[w]

        @pl.when(nxt >= 0)
        def _():
            for cp_ in weight_copies(nxt, 1 - slot):
                cp_.start()

    hw = D_FF // MOE_FF_SLICES
    col_slices = [(slice(hf * hw, (hf + 1) * hw), slice(D_FF + hf * hw, D_FF + (hf + 1) * hw))
                  for hf in range(MOE_FF_SLICES)]

    def cast_weights():
        for gc, uc in col_slices:
            wgu_bf[:, gc] = wgu_f32[slot, :, gc].astype(BF16)
            wgu_bf[:, uc] = wgu_f32[slot, :, uc].astype(BF16)
            yield
        for hf in range(MOE_FF_SLICES):
            rs = slice(hf * hw, (hf + 1) * hw)
            wdn_bf[rs, :] = wdn_f32[slot, rs, :].astype(BF16)
            yield

    def ffn(rows):
        xlo, xhi = _unpack_bf16_pair(x_ref[rows, :])
        xb = jnp.concatenate([xlo.astype(BF16), xhi.astype(BF16)], axis=1)
        bgu = bgu_ref[pl.ds(e, 1), :]
        gus = []
        for gc, uc in col_slices:
            gus.append((_dot(xb, wgu_bf[:, gc]) + bgu[:, gc], _dot(xb, wgu_bf[:, uc]) + bgu[:, uc]))
            yield
        out = bdn_ref[pl.ds(e, 1), :]
        for hf, (g, u) in enumerate(gus):
            gate = jnp.minimum(g, SW_LIMIT)
            up = jnp.clip(u, -SW_LIMIT, SW_LIMIT)
            act = (up + 1.0) * (gate * jax.nn.sigmoid(SW_ALPHA * gate))
            out = out + _dot(act.astype(BF16), wdn_bf[hf * hw:(hf + 1) * hw, :])
            if hf + 1 < MOE_FF_SLICES:
                yield
        o_ref[rows, :] = _pack_bf16_pair(out)

    for code, rows in ((1, slice(None)), (2, slice(0, MOE_BM // 2))):
        @pl.when(jnp.logical_and(wv_ref[w] == code, first))
        def _(rows=rows):
            _interleave(cast_weights(), ffn(rows))

        @pl.when(jnp.logical_and(wv_ref[w] == code, jnp.logical_not(first)))
        def _(rows=rows):
            _interleave(ffn(rows))

    @pl.when(wv_ref[w] == 2)
    def _():
        o_ref[MOE_BM // 2:, :] = jnp.zeros((MOE_BM // 2, D // 2), U32)

    @pl.when(wv_ref[w] == 0)
    def _():
        o_ref[...] = jnp.zeros_like(o_ref)


def _experts(meta, xs, w_gu, b_gu, w_down, b_down):
    return pl.pallas_call(
        _expert_kernel,
        grid_spec=pltpu.PrefetchScalarGridSpec(
            num_scalar_prefetch=len(meta),
            grid=(MOE_NW,),
            in_specs=[
                pl.BlockSpec((MOE_BM, D // 2), lambda w, we, wb, *_: (wb[w], 0)),
                pl.BlockSpec(memory_space=pl.ANY),
                pl.BlockSpec((N_EXP, 2 * D_FF), lambda w, *_: (0, 0)),
                pl.BlockSpec(memory_space=pl.ANY),
                pl.BlockSpec((N_EXP, D), lambda w, *_: (0, 0)),
            ],
            out_specs=pl.BlockSpec((MOE_BM, D // 2), lambda w, we, wb, *_: (wb[w], 0)),
            scratch_shapes=[
                pltpu.VMEM((2, D, 2 * D_FF), F32),
                pltpu.VMEM((2, D_FF, D), F32),
                pltpu.VMEM((D, 2 * D_FF), BF16),
                pltpu.VMEM((D_FF, D), BF16),
                pltpu.SemaphoreType.DMA((2, 2)),
            ],
        ),
        out_shape=jax.ShapeDtypeStruct((XS_ROWS, D // 2), U32),
        compiler_params=pltpu.CompilerParams(
            dimension_semantics=("arbitrary",),
            vmem_limit_bytes=56 * MIB),
        name="moe_experts",
    )(*meta, xs, w_gu, b_gu, w_down, b_down)


def _work_items(counts):
    n_e = (counts + MOE_BM - 1) // MOE_BM
    item_end = jnp.cumsum(n_e)
    total = item_end[-1]
    w = jnp.arange(MOE_NW, dtype=I32)
    valid = w < total
    wc = jnp.minimum(w, total - 1)
    e_w = jnp.sum((item_end[None, :] <= wc[:, None]).astype(I32), axis=1)
    e_w = jnp.minimum(e_w, N_EXP - 1)
    rows_here = counts[e_w] - (wc - (item_end - n_e)[e_w]) * MOE_BM
    valid = jnp.where(valid, jnp.where(rows_here <= MOE_BM // 2, 2, 1), 0)
    prev_e = jnp.concatenate([jnp.full((1,), -1, I32), e_w[:-1]])
    fe = (e_w != prev_e).astype(I32)
    slot = (jnp.cumsum(fe) - 1) % 2
    first_at = jnp.where(fe == 1, w, MOE_NW)
    next_first = jnp.concatenate([lax.cummin(first_at, reverse=True)[1:],
                                  jnp.full((1,), MOE_NW, I32)])
    nxt = jnp.where(next_first < MOE_NW, e_w[jnp.minimum(next_first, MOE_NW - 1)], -1)
    return tuple(a.astype(I32) for a in (e_w, w, valid, fe, slot, nxt))


COMB_TM = 512
SC_CORES = 2
SC_SUBCORES = 16
SC_WORKERS = SC_CORES * SC_SUBCORES
SC_CH = 64
COMB_GROUPS = 4
COMB_TG = T // COMB_GROUPS
SC_ROWS_PER_W = COMB_TG * TOP_K // SC_WORKERS
SC_NCH = SC_ROWS_PER_W // SC_CH


def _sc_gather(table, idx3):
    mesh = plsc.VectorSubcoreMesh(core_axis_name="c", subcore_axis_name="s")

    @functools.partial(
        pl.kernel, mesh=mesh,
        out_type=jax.ShapeDtypeStruct((COMB_TG * TOP_K, D // 2), U32),
        scratch_types=[
            pltpu.VMEM((SC_NCH, SC_CH), I32),
            pltpu.VMEM((2, SC_CH, D // 2), U32),
            pltpu.SemaphoreType.DMA((2,)),
            pltpu.SemaphoreType.DMA((2,)),
        ],
    )
    def k(table_hbm, idx_hbm, out_hbm, idx_v, rows_v, gsem, psem):
        wid = lax.axis_index("s") * SC_CORES + lax.axis_index("c")
        base = wid * SC_ROWS_PER_W
        pltpu.sync_copy(idx_hbm.at[wid], idx_v)

        def gather(j, b):
            return pltpu.make_async_copy(table_hbm.at[idx_v.at[j]], rows_v.at[b], gsem.at[b])

        def put(j, b):
            return pltpu.make_async_copy(rows_v.at[b], out_hbm.at[pl.ds(base + j * SC_CH, SC_CH)],
                                         psem.at[b])

        gather(0, 0).start()

        @pl.loop(0, SC_NCH, step=2)
        def _(j0):
            for b in range(2):
                j = j0 + b

                @pl.when(j + 1 < SC_NCH)
                def _():
                    @pl.when(j >= 1)
                    def _():
                        put(j - 1, 1 - b).wait()
                    gather(j + 1, 1 - b).start()

                gather(j, b).wait()
                put(j, b).start()

        put(SC_NCH - 2, 0).wait()
        put(SC_NCH - 1, 1).wait()

    return k(table, idx3)


SCD_TOK_PER_W = T // SC_WORKERS
SCD_NCH = SCD_TOK_PER_W // SC_CH


def _sc_dispatch(h1p, idx4):
    mesh = plsc.VectorSubcoreMesh(core_axis_name="c", subcore_axis_name="s")

    @functools.partial(
        pl.kernel, mesh=mesh,
        out_type=jax.ShapeDtypeStruct((XS_ROWS, D // 2), U32),
        scratch_types=[
            pltpu.VMEM((SCD_NCH * TOP_K, SC_CH), I32),
            pltpu.VMEM((2, SC_CH, D // 2), U32),
            pltpu.SemaphoreType.DMA((2,)),
            pltpu.SemaphoreType.DMA((2,)),
        ],
    )
    def k(h_hbm, idx_hbm, xs_hbm, idx_v, rows_v, gsem, psem):
        wid = lax.axis_index("s") * SC_CORES + lax.axis_index("c")
        base = wid * SCD_TOK_PER_W
        pltpu.sync_copy(idx_hbm.at[wid], idx_v)

        def get(c, b):
            return pltpu.make_async_copy(h_hbm.at[pl.ds(base + c * SC_CH, SC_CH)], rows_v.at[b],
                                         gsem.at[b])

        def puts(c, b):
            return [pltpu.make_async_copy(rows_v.at[b], xs_hbm.at[idx_v.at[c * TOP_K + kk]],
                                          psem.at[b]) for kk in range(TOP_K)]

        get(0, 0).start()

        @pl.loop(0, SCD_NCH, step=2)
        def _(c0):
            for b in range(2):
                c = c0 + b

                @pl.when(c + 1 < SCD_NCH)
                def _():
                    @pl.when(c >= 1)
                    def _():
                        for cp_ in puts(c - 1, 1 - b):
                            cp_.wait()
                    get(c + 1, 1 - b).start()

                get(c, b).wait()
                for cp_ in puts(c, b):
                    cp_.start()

        for cp_ in puts(SCD_NCH - 2, 0) + puts(SCD_NCH - 1, 1):
            cp_.wait()

    return k(h1p, idx4)


def _combine_dense_kernel(g_ref, h1_ref, tw_ref, lg_ref, lb_ref, o_ref):
    tw = tw_ref[...]
    ylo = jnp.zeros((COMB_TM, D // 2), F32)
    yhi = jnp.zeros((COMB_TM, D // 2), F32)
    for k in range(TOP_K):
        lo, hi = _unpack_bf16_pair(g_ref[k])
        wk = tw[:, k:k + 1]
        ylo = ylo + lo * wk
        yhi = yhi + hi * wk
    ff = jnp.concatenate([ylo, yhi], axis=1)
    o_ref[...] = _ln(DN_ALPHA * h1_ref[...] + ff, lg_ref[...], lb_ref[...])


def _combine_dense(g4, h1, topw, g, b, group):
    tm = COMB_TM
    t0 = group * (COMB_TG // tm)
    return pl.pallas_call(
        _combine_dense_kernel,
        grid=(COMB_TG // tm,),
        in_specs=[
            pl.BlockSpec((TOP_K, tm, D // 2), lambda i: (0, i, 0)),
            pl.BlockSpec((tm, D), lambda i: (t0 + i, 0)),
            pl.BlockSpec((tm, LANES), lambda i: (t0 + i, 0)),
            pl.BlockSpec((1, D), lambda i: (0, 0)),
            pl.BlockSpec((1, D), lambda i: (0, 0)),
        ],
        out_specs=pl.BlockSpec((tm, D), lambda i: (t0 + i, 0)),
        out_shape=jax.ShapeDtypeStruct((T, D), F32),
        input_output_aliases={1: 0},
        compiler_params=pltpu.CompilerParams(
            dimension_semantics=("arbitrary",),
            vmem_limit_bytes=40 * MIB),
        name="moe_combine_dense_ln2",
    )(g4, h1, topw, g, b)


def _pad_cols(a, n):
    return jnp.pad(a, ((0, 0), (0, n - a.shape[1])))


def kernel(x, mem, ln_in_g, ln_in_b, ln_mem_g, ln_mem_b, w_in, b_in, w_decay_f, b_decay_f,
           w_decay_b, b_decay_b, gla_norm_g, w_br_gla, w_br_fnet, w_br_mem, w_mem_kv, w_out,
           b_out, ln1_g, ln1_b, w_router, b_router, w_gu, b_gu, w_down, b_down, ln2_g, ln2_b):
    assert x.shape == (BATCH, SEQ, D) and w_in.shape[0] == 1
    row = lambda a: a.reshape(1, -1)
    x2 = x.reshape(T, D)
    w_in0, b_in0 = w_in[0], b_in[0]
    c_lr, c_fn, c_mq, c_gt = 3072, 3072 + 2 * GLA_LR, 3104 + FN_W, 3104 + FN_W + MQ_W
    w_main = jnp.concatenate([w_in0[:, :c_lr], w_in0[:, c_gt:]], axis=1).astype(BF16)
    b_main = row(jnp.concatenate([b_in0[:c_lr], b_in0[c_gt:]]))
    w_lr = _pad_cols(w_in0[:, c_lr:c_fn], LANES).astype(BF16)
    b_lr = _pad_cols(row(b_in0[c_lr:c_fn]), LANES)
    w_mq = w_in0[:, c_mq:c_gt].astype(BF16)
    b_mq = row(b_in0[c_mq:c_gt])
    w_fn = w_in0[:, c_fn:c_mq].astype(BF16)
    b_fn = row(b_in0[c_fn:c_mq])
    lng, lnb = row(ln_in_g), row(ln_in_b)

    proj, mq, lr = _inproj(x2, lng, lnb, w_main, b_main, w_mq, b_mq, w_lr, b_lr)

    zpad = jnp.zeros((LANES - 2 * GLA_LR, GLA_H * GLA_DK), F32)
    zlr = jnp.zeros((GLA_LR, GLA_H * GLA_DK), F32)
    wdf = jnp.concatenate([w_decay_f[0], zlr, zpad], axis=0).reshape(LANES, GLA_H, GLA_DK)
    wdb = jnp.concatenate([zlr, w_decay_b[0], zpad], axis=0).reshape(LANES, GLA_H, GLA_DK)
    wd = jnp.concatenate([wdf, wdb], axis=2).reshape(LANES, GLA_H * 2 * GLA_DK).astype(BF16)
    wd = jnp.concatenate([wd, wd], axis=0)
    bd = jnp.concatenate([b_decay_f[0].reshape(GLA_H, GLA_DK),
                          b_decay_b[0].reshape(GLA_H, GLA_DK)], axis=1).reshape(1, -1)
    og = _gla(proj, lr, wd, bd, row(gla_norm_g[0]))

    fbig, cwt, swt, f2, ccs, perm = _dft_tables(MERGE_TM)
    x4 = x.reshape(BATCH, FFT_N2, FFT_N1, D)
    z = _fft2(_fft1(x4, lng, lnb, w_fn, b_fn, fbig, cwt, swt), f2)

    kv = _memkv(mem.reshape(BATCH * MEM_LEN, D), row(ln_mem_g), row(ln_mem_b),
                w_mem_kv[0].astype(BF16))

    w_r = _pad_cols(w_router[0], LANES)
    wr_hi = w_r.astype(BF16)
    wr_lo = (w_r - wr_hi.astype(F32)).astype(BF16)
    b_r = jnp.concatenate([row(b_router[0]),
                           jnp.full((1, LANES - N_EXP), NEG_BIG, F32)], axis=1)
    h1, h1p, eidx, topw, cnt = _merge(
        x2, og, z, mq, proj, kv, lng, lnb,
        w_br_gla[0].astype(BF16), ccs, perm, w_br_fnet[0].astype(BF16),
        w_br_mem[0].astype(BF16), w_out[0].astype(BF16), row(b_out[0]),
        row(ln1_g[0]), row(ln1_b[0]), jnp.concatenate([wr_hi, wr_lo], axis=1), wr_hi, b_r)

    dest = _plan(eidx, cnt)
    counts = cnt[0, :N_EXP].astype(I32)
    dest_k = dest[:, :TOP_K]
    idx4 = dest_k.reshape(SC_WORKERS, SCD_NCH, SC_CH, TOP_K).transpose(0, 1, 3, 2)
    xs = _sc_dispatch(h1p, idx4.reshape(SC_WORKERS, SCD_NCH * TOP_K, SC_CH))
    ys = _experts(_work_items(counts), xs, w_gu[0], b_gu[0], w_down[0], b_down[0])
    out = h1
    for grp in range(COMB_GROUPS):
        dest_g = dest_k[grp * COMB_TG:(grp + 1) * COMB_TG].T.reshape(SC_WORKERS, SC_NCH, SC_CH)
        g4 = _sc_gather(ys, dest_g).reshape(TOP_K, COMB_TG, D // 2)
        out = _combine_dense(g4, out, topw, row(ln2_g[0]), row(ln2_b[0]), grp)
    return out.reshape(BATCH, SEQ, D)
```

```python
import functools
import math

import numpy as np
import jax
import jax.numpy as jnp
from jax import lax
from jax.experimental import pallas as pl
from jax.experimental.pallas import tpu as pltpu
from jax.experimental.pallas import tpu_sc as plsc

F32 = jnp.float32
BF16 = jnp.bfloat16
I32 = jnp.int32
U32 = jnp.uint32

D = 1024
BATCH = 4
SEQ = 4096
T = BATCH * SEQ
GLA_H = 4
GLA_DK = 128
GLA_DV = 256
GLA_LR = 16
GLA_TAU = 16.0
GLA_C = 64
FN_G = 4
FN_GW = 128
FN_W = 512
MEM_LEN = 256
MEM_H = 4
MEM_HD = 128
MQ_W = 512
N_EXP = 32
TOP_K = 4
D_FF = 1024
SW_LIMIT = 7.0
SW_ALPHA = 1.702
LN_EPS = 1e-5
RMS_EPS = 1e-6
DN_ALPHA = 2.0 ** 0.25
A_ROWS = T * TOP_K

FFT_N1 = 128
FFT_N2 = 32

LANES = 128
NEG_BIG = -1e30
MIB = 1024 * 1024


def _ln(x, g, b):
    mu = jnp.mean(x, axis=-1, keepdims=True)
    xc = x - mu
    var = jnp.mean(xc * xc, axis=-1, keepdims=True)
    return xc * lax.rsqrt(var + LN_EPS) * g + b


def _dot(a, b):
    return jnp.dot(a, b, preferred_element_type=F32)


def _dot_nt(a, b):
    return lax.dot_general(a, b, (((1,), (1,)), ((), ())), preferred_element_type=F32)


def _dot_tn(a, b):
    return lax.dot_general(a, b, (((0,), (0,)), ((), ())), preferred_element_type=F32)


def _interleave(*stages):
    live = list(stages)
    while live:
        for st in list(live):
            try:
                next(st)
            except StopIteration:
                live.remove(st)


def _split_bf16(a):
    hi = a.astype(BF16)
    return hi, (a - hi.astype(F32)).astype(BF16)


INPROJ_TM = 1024
INPROJ_TN = 3072


PROJ_W = 6 * 1024


def _inproj_kernel(x_ref, g_ref, b_ref, w_ref, bias_ref, wmq_ref, bmq_ref, wlr_ref, blr_ref,
                   proj_ref, mq_ref, lr_ref, hb_ref):
    @pl.when(pl.program_id(1) == 0)
    def _():
        hb = _ln(x_ref[...], g_ref[...], b_ref[...]).astype(BF16)
        hb_ref[...] = hb
        lr_ref[...] = _dot(hb, wlr_ref[...]) + blr_ref[...]
        mq_ref[...] = (_dot(hb, wmq_ref[...]) + bmq_ref[...]).astype(BF16)

    proj_ref[...] = (_dot(hb_ref[...], w_ref[...]) + bias_ref[...]).astype(BF16)


def _inproj(x2, ln_g, ln_b, w_main, b_main, w_mq, b_mq, w_lr, b_lr):
    tm, tn = INPROJ_TM, INPROJ_TN
    nj = PROJ_W // tn
    row = lambda i, j: (i, 0)
    const = lambda i, j: (0, 0)
    outs = (
        jax.ShapeDtypeStruct((T, PROJ_W), BF16),
        jax.ShapeDtypeStruct((T, MQ_W), BF16),
        jax.ShapeDtypeStruct((T, LANES), F32),
    )
    return pl.pallas_call(
        _inproj_kernel,
        grid=(T // tm, nj),
        in_specs=[
            pl.BlockSpec((tm, D), row),
            pl.BlockSpec((1, D), const),
            pl.BlockSpec((1, D), const),
            pl.BlockSpec((D, tn), lambda i, j: (0, j)),
            pl.BlockSpec((1, tn), lambda i, j: (0, j)),
            pl.BlockSpec((D, MQ_W), const),
            pl.BlockSpec((1, MQ_W), const),
            pl.BlockSpec((D, LANES), const),
            pl.BlockSpec((1, LANES), const),
        ],
        out_specs=[
            pl.BlockSpec((tm, tn), lambda i, j: (i, j)),
            pl.BlockSpec((tm, MQ_W), row),
            pl.BlockSpec((tm, LANES), row),
        ],
        out_shape=outs,
        scratch_shapes=[pltpu.VMEM((tm, D), BF16)],
        compiler_params=pltpu.CompilerParams(
            dimension_semantics=("arbitrary", "arbitrary"),
            vmem_limit_bytes=48 * MIB),
        name="ln_inproj",
    )(x2, ln_g, ln_b, w_main, b_main, w_mq, b_mq, w_lr, b_lr)


GLA_BULK = 256
GLA_FIN = 512
GLA_NCH = SEQ // GLA_C
GLA_CPB = GLA_BULK // GLA_C
GLA_PIPE = 4


def _gla_kernel(q_ref, k_ref, v_ref, r_ref, lr_ref, wd_ref, bd_ref, g_ref, cs_ref, o_ref,
                acc_ref, qin_ref, kin_ref, kst_ref, dec_ref, u_ref, stf_ref, stb_ref):
    C = GLA_C
    G = GLA_BULK
    DK = GLA_DK
    NG = SEQ // G
    scale = DK ** -0.5
    ii = lax.broadcasted_iota(I32, (G, G), 0)
    jj = lax.broadcasted_iota(I32, (G, G), 1)
    same = (ii // C) == (jj // C)
    lower = jnp.logical_and(same, ii >= jj)
    upper = jnp.logical_and(same, ii <= jj)
    is_fwd = lax.broadcasted_iota(I32, (G, 2 * DK), 1) < DK
    chunk_of_row = lax.broadcasted_iota(I32, (G, DK), 0) // C

    def stage_a(gi):
        rows = pl.ds(pl.multiple_of(gi * G, G), G)
        z = _dot(jnp.concatenate(_split_bf16(lr_ref[rows, :]), axis=1), wd_ref[...]) + bd_ref[...]
        yield
        la = -(jnp.maximum(-z, 0.0) + jnp.log(1.0 + jnp.exp(-jnp.abs(z)))) * (1.0 / GLA_TAU)
        la_hi, la_lo = _split_bf16(la)
        pre2 = _dot(cs_ref[...], jnp.concatenate([la_hi, la_lo], axis=1))
        yield
        pre = pre2[:, :2 * DK] + pre2[:, 2 * DK:]
        blast = jnp.concatenate(
            [jnp.broadcast_to(pre[ci * C + C - 1:ci * C + C, :], (C, 2 * DK))
             for ci in range(GLA_CPB)], axis=0)
        b = jnp.where(is_fwd, pre, blast - pre + la)
        qf32 = q_ref[rows, :].astype(F32)
        kf32 = k_ref[rows, :].astype(F32)
        q2 = jnp.concatenate([qf32, qf32], axis=1)
        k2 = jnp.concatenate([kf32, kf32], axis=1)
        qin_ref[rows, :] = (q2 * (scale * jnp.exp(b))).astype(BF16)
        kin_ref[rows, :] = (k2 * jnp.exp(-b)).astype(BF16)
        kst_ref[rows, :] = (k2 * jnp.exp(blast - b)).astype(BF16)
        dec = jnp.exp(blast)
        for ci in range(GLA_CPB):
            dec_ref[pl.ds(gi * GLA_CPB + ci, 1), :] = dec[ci * C:ci * C + 1, :]

    def stage_b(gi):
        rows = pl.ds(pl.multiple_of(gi * G, G), G)
        qi = qin_ref[rows, :]
        ki = kin_ref[rows, :]
        ks = kst_ref[rows, :]
        vb = v_ref[rows, :]
        att_f = _dot_nt(qi[:, :DK], ki[:, :DK])
        att_b = _dot_nt(qi[:, DK:], ki[:, DK:])
        yield
        att = jnp.where(lower, att_f, 0.0) + jnp.where(upper, att_b, 0.0)
        acc_ref[rows, :] = _dot(att.astype(BF16), vb)
        yield
        ksb = jnp.concatenate(
            [jnp.where(chunk_of_row == ci, ks[:, d * DK:(d + 1) * DK], jnp.zeros((G, DK), BF16))
             for d in range(2) for ci in range(GLA_CPB)], axis=1)
        u = _dot_tn(vb, ksb)
        for d in range(2):
            for ci in range(GLA_CPB):
                col = (d * GLA_CPB + ci) * DK
                u_ref[d, gi * GLA_CPB + ci] = u[:, col:col + DK]

    P = GLA_PIPE
    _interleave(*[stage_a(j) for j in range(P)])

    def bulk(i, carry):
        g = P * i
        _interleave(*[st for j in range(P) for st in (stage_b(g - P + j), stage_a(g + j))])
        return carry

    lax.fori_loop(1, NG // P, bulk, 0)
    _interleave(*[stage_b(NG - P + j) for j in range(P)])

    stf_ref[...] = jnp.zeros_like(stf_ref)
    stb_ref[...] = jnp.zeros_like(stb_ref)

    def one(n, d, st_ref):
        lanes = slice(d * DK, (d + 1) * DK)
        rows = pl.ds(pl.multiple_of(n * C, C), C)
        st = st_ref[...]
        acc_ref[rows, :] += _dot_nt(qin_ref[rows, lanes], st.astype(BF16))
        st_ref[...] = st * dec_ref[pl.ds(n, 1), :][:, lanes] + u_ref[d, n]

    def step(i, carry):
        one(i, 0, stf_ref)
        one(GLA_NCH - 1 - i, 1, stb_ref)
        return carry

    lax.fori_loop(0, GLA_NCH, step, 0, unroll=16)

    def fin(gi, carry):
        rows = pl.ds(pl.multiple_of(gi * GLA_FIN, GLA_FIN), GLA_FIN)
        o = acc_ref[rows, :]
        o = o * lax.rsqrt(jnp.mean(o * o, axis=-1, keepdims=True) + RMS_EPS) * g_ref[...]
        rg = r_ref[rows, :].astype(F32)
        o_ref[rows, :] = (o * (rg * jax.nn.sigmoid(rg))).astype(BF16)
        return carry

    lax.fori_loop(0, SEQ // GLA_FIN, fin, 0)


def _gla(proj, lr, wd, bd, g):
    i = np.arange(GLA_BULK)
    cs = ((i[:, None] // GLA_C) == (i[None, :] // GLA_C)) & (i[:, None] >= i[None, :])
    cs = jnp.asarray(cs, dtype=F32).astype(BF16)
    v_blk = 1024 // GLA_DV
    return pl.pallas_call(
        _gla_kernel,
        grid=(BATCH, GLA_H),
        in_specs=[
            pl.BlockSpec((SEQ, GLA_DK), lambda b, h: (b, h)),
            pl.BlockSpec((SEQ, GLA_DK), lambda b, h: (b, GLA_H + h)),
            pl.BlockSpec((SEQ, GLA_DV), lambda b, h: (b, v_blk + h)),
            pl.BlockSpec((SEQ, GLA_DV), lambda b, h: (b, 2 * v_blk + h)),
            pl.BlockSpec((SEQ, LANES), lambda b, h: (b, 0)),
            pl.BlockSpec((2 * LANES, 2 * GLA_DK), lambda b, h: (0, h)),
            pl.BlockSpec((1, 2 * GLA_DK), lambda b, h: (0, h)),
            pl.BlockSpec((1, GLA_DV), lambda b, h: (0, 0)),
            pl.BlockSpec((GLA_BULK, GLA_BULK), lambda b, h: (0, 0)),
        ],
        out_specs=pl.BlockSpec((SEQ, GLA_DV), lambda b, h: (b, h)),
        out_shape=jax.ShapeDtypeStruct((T, GLA_H * GLA_DV), BF16),
        scratch_shapes=[
            pltpu.VMEM((SEQ, GLA_DV), F32),
            pltpu.VMEM((SEQ, 2 * GLA_DK), BF16),
            pltpu.VMEM((SEQ, 2 * GLA_DK), BF16),
            pltpu.VMEM((SEQ, 2 * GLA_DK), BF16),
            pltpu.VMEM((GLA_NCH, 2 * GLA_DK), F32),
            pltpu.VMEM((2, GLA_NCH, GLA_DV, GLA_DK), F32),
            pltpu.VMEM((GLA_DV, GLA_DK), F32),
            pltpu.VMEM((GLA_DV, GLA_DK), F32),
        ],
        compiler_params=pltpu.CompilerParams(
            dimension_semantics=("arbitrary", "arbitrary"),
            vmem_limit_bytes=58 * MIB),
        name="gla",
    )(proj, proj, proj, proj, lr, wd, bd, g, cs)


FFT1_S = 16
FFT1_SUB = 2
FFT1_ROWS = FFT_N2 * FFT1_S
FFT2_KB = 8


def _fft1_kernel(x_ref, g_ref, b_ref, w_ref, bias_ref, fbig_ref, cw_ref, sw_ref, o_ref):
    sh = FFT1_S // FFT1_SUB
    rows = FFT_N2 * sh
    res = {}

    def sub(h):
        xv = x_ref[:, h * sh:(h + 1) * sh, :].reshape(rows, D)
        hb = _ln(xv, g_ref[...], b_ref[...]).astype(BF16)
        fn = (_dot(hb, w_ref[...]) + bias_ref[...]).astype(BF16)
        yield
        a = _dot(fbig_ref[...], fn)
        yield
        ar = a[:rows]
        ai = a[rows:]
        cw = jnp.concatenate([cw_ref[h]] * (FN_W // LANES), axis=1)
        sw = jnp.concatenate([sw_ref[h]] * (FN_W // LANES), axis=1)
        res[h] = ((ar * cw + ai * sw).reshape(FFT_N2, sh, FN_W),
                  (ai * cw - ar * sw).reshape(FFT_N2, sh, FN_W))

    _interleave(*[sub(h) for h in range(FFT1_SUB)])
    for ri in range(2):
        o_ref[ri] = jnp.concatenate([res[h][ri] for h in range(FFT1_SUB)], axis=1).astype(BF16)


def _fft1(x4, ln_g, ln_b, w_fn, b_fn, fbig, cwt, swt):
    s = FFT1_S
    const = lambda b, j: (0, 0)
    return pl.pallas_call(
        _fft1_kernel,
        grid=(BATCH, FFT_N1 // s),
        in_specs=[
            pl.BlockSpec((None, FFT_N2, s, D), lambda b, j: (b, 0, j, 0)),
            pl.BlockSpec((1, D), const),
            pl.BlockSpec((1, D), const),
            pl.BlockSpec((D, FN_W), const),
            pl.BlockSpec((1, FN_W), const),
            pl.BlockSpec((2 * FFT1_ROWS // FFT1_SUB, FFT1_ROWS // FFT1_SUB), const),
            pl.BlockSpec((FFT1_SUB, FFT1_ROWS // FFT1_SUB, LANES), lambda b, j: (j, 0, 0)),
            pl.BlockSpec((FFT1_SUB, FFT1_ROWS // FFT1_SUB, LANES), lambda b, j: (j, 0, 0)),
        ],
        out_specs=pl.BlockSpec((None, 2, FFT_N2, s, FN_W), lambda b, j: (b, 0, 0, j, 0)),
        out_shape=jax.ShapeDtypeStruct((BATCH, 2, FFT_N2, FFT_N1, FN_W), BF16),
        compiler_params=pltpu.CompilerParams(
            dimension_semantics=("arbitrary", "arbitrary"),
            vmem_limit_bytes=40 * MIB),
        name="fft_stage1",
    )(x4, ln_g, ln_b, w_fn, b_fn, fbig, cwt, swt)


def _fft2_kernel(d_ref, f2_ref, o_ref):
    f2 = f2_ref[...]
    for kk in range(FFT2_KB):
        z = _dot(f2, jnp.concatenate([d_ref[0, kk], d_ref[1, kk]], axis=0))
        o_ref[0, kk] = z[:FFT_N1].astype(BF16)
        o_ref[1, kk] = z[FFT_N1:].astype(BF16)


def _fft2(dmat, f2):
    kb = FFT2_KB
    blk = (None, 2, kb, FFT_N1, FN_W)
    return pl.pallas_call(
        _fft2_kernel,
        grid=(BATCH, FFT_N2 // kb),
        in_specs=[
            pl.BlockSpec(blk, lambda b, j: (b, 0, j, 0, 0)),
            pl.BlockSpec((2 * FFT_N1, 2 * FFT_N1), lambda b, j: (0, 0)),
        ],
        out_specs=pl.BlockSpec(blk, lambda b, j: (b, 0, j, 0, 0)),
        out_shape=jax.ShapeDtypeStruct((BATCH, 2, FFT_N2, FFT_N1, FN_W), BF16),
        compiler_params=pltpu.CompilerParams(
            dimension_semantics=("arbitrary", "arbitrary")),
        name="fft_stage2",
    )(dmat, f2)


def _dft_tables(merge_tm):
    s = FFT1_S // FFT1_SUB
    n2 = np.arange(FFT_N2, dtype=np.float64)
    n1 = np.arange(FFT_N1, dtype=np.float64)
    th = 2.0 * np.pi * np.outer(n2, n2) / FFT_N2
    f1 = np.stack([np.cos(th), -np.sin(th)]) / math.sqrt(SEQ)
    fbig = np.einsum("rkn,st->rksnt", f1, np.eye(s)).reshape(2 * FFT_N2 * s, FFT_N2 * s)
    tw = 2.0 * np.pi * np.outer(n2, n1) / SEQ
    tw = tw.reshape(FFT_N2, FFT_N1 // s, s).transpose(1, 0, 2).reshape(FFT_N1 // s, FFT_N2 * s)
    cwt = np.broadcast_to(np.cos(tw)[:, :, None], tw.shape + (LANES,))
    swt = np.broadcast_to(np.sin(tw)[:, :, None], tw.shape + (LANES,))
    th1 = 2.0 * np.pi * np.outer(n1, n1) / FFT_N1
    c1, s1 = np.cos(th1), np.sin(th1)
    f2 = np.block([[c1, s1], [-s1, c1]])
    cc = np.arange(FN_GW, dtype=np.float64)
    thc = 2.0 * np.pi * np.outer(cc, cc) / FN_GW
    ccs = np.concatenate([np.cos(thc), np.sin(thc)], axis=0) / math.sqrt(FN_GW)
    k1n = merge_tm // FFT_N2
    r = np.arange(merge_tm)
    perm = np.zeros((merge_tm, merge_tm))
    perm[r, (r % FFT_N2) * k1n + r // FFT_N2] = 1.0
    as32 = lambda a: jnp.asarray(np.ascontiguousarray(a), dtype=F32)
    return (as32(fbig).astype(BF16), as32(cwt), as32(swt), as32(f2).astype(BF16),
            as32(ccs).astype(BF16), as32(perm).astype(BF16))


def _memkv_kernel(m_ref, g_ref, b_ref, w_ref, o_ref):
    mn = _ln(m_ref[...], g_ref[...], b_ref[...]).astype(BF16)
    o_ref[...] = _dot(mn, w_ref[...]).astype(BF16)


def _memkv(mem2, g, b, w):
    return pl.pallas_call(
        _memkv_kernel,
        grid=(BATCH,),
        in_specs=[
            pl.BlockSpec((MEM_LEN, D), lambda i: (i, 0)),
            pl.BlockSpec((1, D), lambda i: (0, 0)),
            pl.BlockSpec((1, D), lambda i: (0, 0)),
            pl.BlockSpec((D, 2 * MQ_W), lambda i: (0, 0)),
        ],
        out_specs=pl.BlockSpec((MEM_LEN, 2 * MQ_W), lambda i: (i, 0)),
        out_shape=jax.ShapeDtypeStruct((BATCH * MEM_LEN, 2 * MQ_W), BF16),
        compiler_params=pltpu.CompilerParams(dimension_semantics=("arbitrary",)),
        name="mem_kv",
    )(mem2, g, b, w)


MERGE_TM = 512
MERGE_K1 = MERGE_TM // FFT_N2


def _pack_bf16_pair(v):
    n = v.shape[1] // 2
    bits = lax.bitcast_convert_type(v.astype(BF16).astype(F32), U32)
    return (bits[:, n:] & jnp.uint32(0xFFFF0000)) | (bits[:, :n] >> 16)


def _unpack_bf16_pair(p):
    lo = lax.bitcast_convert_type(p << 16, F32)
    hi = lax.bitcast_convert_type(p & jnp.uint32(0xFFFF0000), F32)
    return lo, hi


def _merge_kernel(x_ref, og_ref, zr_ref, zi_ref, mq_ref, gt_ref, kv_ref,
                  lng_ref, lnb_ref, wg_ref, ccs_ref, perm_ref, wf_ref, wm_ref, wo_ref, bo_ref,
                  l1g_ref, l1b_ref, wr2_ref, wrh_ref, br_ref,
                  h1_ref, h1p_ref, eidx_ref, topw_ref, cnt_ref):
    tm = MERGE_TM
    y = {}

    def branch_fnet():
        zr = zr_ref[...].reshape(tm, FN_W)
        zi = zi_ref[...].reshape(tm, FN_W)
        ys = []
        for g in range(FN_G):
            sl = slice(g * FN_GW, (g + 1) * FN_GW)
            ys.append(_dot(jnp.concatenate([zr[:, sl], zi[:, sl]], axis=1), ccs_ref[...]))
        yield
        yp = _dot(perm_ref[...], jnp.concatenate(ys, axis=1).astype(BF16))
        yield
        y["fnet"] = _dot(yp.astype(BF16), wf_ref[...])

    def branch_mem():
        heads = [slice(hd * MEM_HD, (hd + 1) * MEM_HD) for hd in range(MEM_H)]
        ss = [_dot_nt(mq_ref[:, sl], kv_ref[:, sl]) for sl in heads]
        yield
        oms = []
        for hd, s in enumerate(ss):
            s = s * (MEM_HD ** -0.5)
            s = s - jnp.max(s, axis=-1, keepdims=True)
            p = jnp.exp(s)
            p = p * (1.0 / jnp.sum(p, axis=-1, keepdims=True))
            oms.append(_dot(p.astype(BF16),
                            kv_ref[:, MQ_W + hd * MEM_HD:MQ_W + (hd + 1) * MEM_HD]))
        yield
        y["mem"] = _dot(jnp.concatenate(oms, axis=1).astype(BF16), wm_ref[...])

    def branch_gla():
        y["gla"] = _dot(og_ref[...], wg_ref[...])
        yield

    _interleave(branch_fnet(), branch_mem(), branch_gla())

    def gate(c):
        return 0.5 + 0.5 * jnp.tanh(0.5 * gt_ref[:, c * D:(c + 1) * D].astype(F32))

    merged = gate(0) * y["gla"] + gate(1) * y["fnet"] + gate(2) * y["mem"]
    mix = _dot(merged.astype(BF16), wo_ref[...]) + bo_ref[...]
    h = _ln(x_ref[...], lng_ref[...], lnb_ref[...])
    h1 = _ln(DN_ALPHA * h + mix, l1g_ref[...], l1b_ref[...])
    h1_ref[...] = h1
    h1p_ref[...] = _pack_bf16_pair(h1)

    h_hi, h_lo = _split_bf16(h1)
    d2 = _dot(h_hi, wr2_ref[...])
    l = d2[:, :LANES] + d2[:, LANES:] + _dot(h_lo, wrh_ref[...]) + br_ref[...]
    lane = lax.broadcasted_iota(I32, (tm, LANES), 1)
    vals, idxs = [], []
    for _ in range(TOP_K):
        m = jnp.max(l, axis=-1, keepdims=True)
        idx = jnp.min(jnp.where(l == m, lane, LANES), axis=-1, keepdims=True)
        vals.append(m)
        idxs.append(idx)
        l = jnp.where(lane == idx, -jnp.inf, l)
    es = [jnp.exp(v - vals[0]) for v in vals]
    den = es[0] + es[1] + es[2] + es[3]
    eo = jnp.zeros((tm, LANES), I32)
    wo = jnp.zeros((tm, LANES), F32)
    chosen = jnp.zeros((tm, LANES), F32)
    for k in range(TOP_K):
        eo = jnp.where(lane == k, idxs[k], eo)
        wo = jnp.where(lane == k, es[k] / den, wo)
        chosen = chosen + jnp.where(lane == idxs[k], 1.0, 0.0)
    eidx_ref[...] = eo
    topw_ref[...] = wo

    @pl.when(pl.program_id(0) == 0)
    def _():
        cnt_ref[...] = jnp.zeros_like(cnt_ref)

    cnt_ref[...] += jnp.broadcast_to(jnp.sum(chosen, axis=0, keepdims=True), cnt_ref.shape)


def _merge(x2, og, z, mq, gates, kv, lng, lnb, wg, ccs, perm, wf, wm, wo, bo, l1g, l1b,
           wr2, wrh, br):
    tm = MERGE_TM
    per_b = SEQ // tm
    row = lambda i: (i, 0)
    const = lambda i: (0, 0)
    zblk = (None, None, FFT_N2, MERGE_K1, FN_W)
    outs = (
        jax.ShapeDtypeStruct((T, D), F32),
        jax.ShapeDtypeStruct((T, D // 2), U32),
        jax.ShapeDtypeStruct((T, LANES), I32),
        jax.ShapeDtypeStruct((T, LANES), F32),
        jax.ShapeDtypeStruct((8, LANES), F32),
    )
    return pl.pallas_call(
        _merge_kernel,
        grid=(T // tm,),
        in_specs=[
            pl.BlockSpec((tm, D), row),
            pl.BlockSpec((tm, D), row),
            pl.BlockSpec(zblk, lambda i: (i // per_b, 0, 0, i % per_b, 0)),
            pl.BlockSpec(zblk, lambda i: (i // per_b, 1, 0, i % per_b, 0)),
            pl.BlockSpec((tm, MQ_W), row),
            pl.BlockSpec((tm, 3 * D), lambda i: (i, 1)),
            pl.BlockSpec((MEM_LEN, 2 * MQ_W), lambda i: (i // per_b, 0)),
            pl.BlockSpec((1, D), const), pl.BlockSpec((1, D), const),
            pl.BlockSpec((D, D), const),
            pl.BlockSpec((2 * FN_GW, FN_GW), const),
            pl.BlockSpec((tm, tm), const),
            pl.BlockSpec((FN_W, D), const),
            pl.BlockSpec((MQ_W, D), const),
            pl.BlockSpec((D, D), const),
            pl.BlockSpec((1, D), const),
            pl.BlockSpec((1, D), const), pl.BlockSpec((1, D), const),
            pl.BlockSpec((D, 2 * LANES), const),
            pl.BlockSpec((D, LANES), const),
            pl.BlockSpec((1, LANES), const),
        ],
        out_specs=[
            pl.BlockSpec((tm, D), row),
            pl.BlockSpec((tm, D // 2), row),
            pl.BlockSpec((tm, LANES), row),
            pl.BlockSpec((tm, LANES), row),
            pl.BlockSpec((8, LANES), const),
        ],
        out_shape=outs,
        compiler_params=pltpu.CompilerParams(
            dimension_semantics=("arbitrary",),
            vmem_limit_bytes=58 * MIB),
        name="merge_ln1_router",
    )(x2, og, z, z, mq, gates, kv, lng, lnb, wg, ccs, perm, wf, wm, wo, bo, l1g, l1b,
      wr2, wrh, br)


PLAN_TP = 1024


def _expert_onehots(e, lane):
    onehots = [lane == e[:, k:k + 1] for k in range(TOP_K)]
    mf = jnp.zeros(lane.shape, F32)
    for oh in onehots:
        mf = mf + jnp.where(oh, 1.0, 0.0)
    return onehots, mf


def _plan_kernel(e_ref, tot_ref, dest_ref, cnt_ref, off_ref):
    i = pl.program_id(0)
    tp = PLAN_TP
    lane = lax.broadcasted_iota(I32, (tp, LANES), 1)
    onehots, mf = _expert_onehots(e_ref[...], lane)

    @pl.when(i == 0)
    def _():
        tot = tot_ref[0:1, :]
        padded = jnp.floor((tot + (MOE_BM - 1)) * (1.0 / MOE_BM)) * MOE_BM
        lane1 = lax.broadcasted_iota(I32, (1, LANES), 1)
        inc = padded
        for s in (1, 2, 4, 8, 16, 32, 64):
            inc = inc + jnp.where(lane1 >= s, pltpu.roll(inc, s, 1), 0.0)
        off_ref[...] = inc - padded
        cnt_ref[...] = jnp.zeros_like(cnt_ref)

    ri = lax.broadcasted_iota(I32, (tp, tp), 0)
    ci = lax.broadcasted_iota(I32, (tp, tp), 1)
    ltri = jnp.where(ri > ci, 1.0, 0.0).astype(BF16)
    rank = _dot(ltri, mf.astype(BF16)) + cnt_ref[...] + off_ref[...]
    out = jnp.zeros((tp, LANES), I32)
    for k in range(TOP_K):
        dk = jnp.sum(jnp.where(onehots[k], rank, 0.0), axis=-1, keepdims=True)
        out = jnp.where(lane == k, dk.astype(I32), out)
    dest_ref[...] = out
    cnt_ref[...] += jnp.sum(mf, axis=0, keepdims=True)


def _plan(eidx, cnt):
    tp = PLAN_TP
    return pl.pallas_call(
        _plan_kernel,
        grid=(T // tp,),
        in_specs=[pl.BlockSpec((tp, LANES), lambda i: (i, 0)),
                  pl.BlockSpec((8, LANES), lambda i: (0, 0))],
        out_specs=pl.BlockSpec((tp, LANES), lambda i: (i, 0)),
        out_shape=jax.ShapeDtypeStruct((T, LANES), I32),
        scratch_shapes=[pltpu.VMEM((1, LANES), F32), pltpu.VMEM((1, LANES), F32)],
        compiler_params=pltpu.CompilerParams(dimension_semantics=("arbitrary",)),
        name="route_plan",
    )(eidx, cnt)


MOE_BM = 512
MOE_NW = A_ROWS // MOE_BM + N_EXP
XS_ROWS = MOE_NW * MOE_BM
MOE_FF_SLICES = 2


def _expert_kernel(we_ref, wb_ref, wv_ref, wfe_ref, wsl_ref, wnx_ref,
                   x_ref, wgu_hbm, bgu_ref, wdn_hbm, bdn_ref, o_ref,
                   wgu_f32, wdn_f32, wgu_bf, wdn_bf, sems):
    w = pl.program_id(0)
    e = we_ref[w]

    def weight_copies(expert, slot):
        return (pltpu.make_async_copy(wgu_hbm.at[expert], wgu_f32.at[slot], sems.at[slot, 0]),
                pltpu.make_async_copy(wdn_hbm.at[expert], wdn_f32.at[slot], sems.at[slot, 1]))

    @pl.when(w == 0)
    def _():
        for cp_ in weight_copies(e, 0):
            cp_.start()

    first = wfe_ref[w] == 1
    slot = wsl_ref[w]

    @pl.when(first)
    def _():
        for cp_ in weight_copies(e, slot):
            cp_.wait()
        nxt = wnx_ref[w]

        @pl.when(nxt >= 0)
        def _():
            for cp_ in weight_copies(nxt, 1 - slot):
                cp_.start()

    hw = D_FF // MOE_FF_SLICES
    col_slices = [(slice(hf * hw, (hf + 1) * hw), slice(D_FF + hf * hw, D_FF + (hf + 1) * hw))
                  for hf in range(MOE_FF_SLICES)]

    def cast_weights():
        for gc, uc in col_slices:
            wgu_bf[:, gc] = wgu_f32[slot, :, gc].astype(BF16)
            wgu_bf[:, uc] = wgu_f32[slot, :, uc].astype(BF16)
            yield
        for hf in range(MOE_FF_SLICES):
            rs = slice(hf * hw, (hf + 1) * hw)
            wdn_bf[rs, :] = wdn_f32[slot, rs, :].astype(BF16)
            yield

    def ffn(rows):
        xlo, xhi = _unpack_bf16_pair(x_ref[rows, :])
        xb = jnp.concatenate([xlo.astype(BF16), xhi.astype(BF16)], axis=1)
        bgu = bgu_ref[pl.ds(e, 1), :]
        gus = []
        for gc, uc in col_slices:
            gus.append((_dot(xb, wgu_bf[:, gc]) + bgu[:, gc], _dot(xb, wgu_bf[:, uc]) + bgu[:, uc]))
            yield
        out = bdn_ref[pl.ds(e, 1), :]
        for hf, (g, u) in enumerate(gus):
            gate = jnp.minimum(g, SW_LIMIT)
            up = jnp.clip(u, -SW_LIMIT, SW_LIMIT)
            act = (up + 1.0) * (gate * jax.nn.sigmoid(SW_ALPHA * gate))
            out = out + _dot(act.astype(BF16), wdn_bf[hf * hw:(hf + 1) * hw, :])
            if hf + 1 < MOE_FF_SLICES:
                yield
        o_ref[rows, :] = _pack_bf16_pair(out)

    for code, rows in ((1, slice(None)), (2, slice(0, MOE_BM // 2))):
        @pl.when(jnp.logical_and(wv_ref[w] == code, first))
        def _(rows=rows):
            _interleave(cast_weights(), ffn(rows))

        @pl.when(jnp.logical_and(wv_ref[w] == code, jnp.logical_not(first)))
        def _(rows=rows):
            _interleave(ffn(rows))

    @pl.when(wv_ref[w] == 2)
    def _():
        o_ref[MOE_BM // 2:, :] = jnp.zeros((MOE_BM // 2, D // 2), U32)

    @pl.when(wv_ref[w] == 0)
    def _():
        o_ref[...] = jnp.zeros_like(o_ref)


def _experts(meta, xs, w_gu, b_gu, w_down, b_down):
    return pl.pallas_call(
        _expert_kernel,
        grid_spec=pltpu.PrefetchScalarGridSpec(
            num_scalar_prefetch=len(meta),
            grid=(MOE_NW,),
            in_specs=[
                pl.BlockSpec((MOE_BM, D // 2), lambda w, we, wb, *_: (wb[w], 0)),
                pl.BlockSpec(memory_space=pl.ANY),
                pl.BlockSpec((N_EXP, 2 * D_FF), lambda w, *_: (0, 0)),
                pl.BlockSpec(memory_space=pl.ANY),
                pl.BlockSpec((N_EXP, D), lambda w, *_: (0, 0)),
            ],
            out_specs=pl.BlockSpec((MOE_BM, D // 2), lambda w, *_: (w, 0)),
            scratch_shapes=[
                pltpu.VMEM((2, D, 2 * D_FF), F32),
                pltpu.VMEM((2, D_FF, D), F32),
                pltpu.VMEM((D, 2 * D_FF), BF16),
                pltpu.VMEM((D_FF, D), BF16),
                pltpu.SemaphoreType.DMA((2, 2)),
            ],
        ),
        out_shape=jax.ShapeDtypeStruct((XS_ROWS, D // 2), U32),
        compiler_params=pltpu.CompilerParams(
            dimension_semantics=("arbitrary",),
            vmem_limit_bytes=56 * MIB),
        name="moe_experts",
    )(*meta, xs, w_gu, b_gu, w_down, b_down)


def _work_items(counts):
    n_e = (counts + MOE_BM - 1) // MOE_BM
    item_end = jnp.cumsum(n_e)
    total = item_end[-1]
    w = jnp.arange(MOE_NW, dtype=I32)
    valid = w < total
    wc = jnp.minimum(w, total - 1)
    e_w = jnp.sum((item_end[None, :] <= wc[:, None]).astype(I32), axis=1)
    e_w = jnp.minimum(e_w, N_EXP - 1)
    rows_here = counts[e_w] - (wc - (item_end - n_e)[e_w]) * MOE_BM
    valid = jnp.where(valid, jnp.where(rows_here <= MOE_BM // 2, 2, 1), 0)
    prev_e = jnp.concatenate([jnp.full((1,), -1, I32), e_w[:-1]])
    fe = (e_w != prev_e).astype(I32)
    slot = (jnp.cumsum(fe) - 1) % 2
    first_at = jnp.where(fe == 1, w, MOE_NW)
    next_first = jnp.concatenate([lax.cummin(first_at, reverse=True)[1:],
                                  jnp.full((1,), MOE_NW, I32)])
    nxt = jnp.where(next_first < MOE_NW, e_w[jnp.minimum(next_first, MOE_NW - 1)], -1)
    return tuple(a.astype(I32) for a in (e_w, wc, valid, fe, slot, nxt))


COMB_TM = 512
SC_CORES = 2
SC_SUBCORES = 16
SC_WORKERS = SC_CORES * SC_SUBCORES
SC_CH = 64
COMB_GROUPS = 4
COMB_TG = T // COMB_GROUPS
SC_ROWS_PER_W = COMB_TG * TOP_K // SC_WORKERS
SC_NCH = SC_ROWS_PER_W // SC_CH


def _sc_gather(table, idx3):
    mesh = plsc.VectorSubcoreMesh(core_axis_name="c", subcore_axis_name="s")

    @functools.partial(
        pl.kernel, mesh=mesh,
        out_type=jax.ShapeDtypeStruct((COMB_TG * TOP_K, D // 2), U32),
        scratch_types=[
            pltpu.VMEM((SC_NCH, SC_CH), I32),
            pltpu.VMEM((2, SC_CH, D // 2), U32),
            pltpu.SemaphoreType.DMA((2,)),
            pltpu.SemaphoreType.DMA((2,)),
        ],
    )
    def k(table_hbm, idx_hbm, out_hbm, idx_v, rows_v, gsem, psem):
        wid = lax.axis_index("s") * SC_CORES + lax.axis_index("c")
        base = wid * SC_ROWS_PER_W
        pltpu.sync_copy(idx_hbm.at[wid], idx_v)

        def gather(j, b):
            return pltpu.make_async_copy(table_hbm.at[idx_v.at[j]], rows_v.at[b], gsem.at[b])

        def put(j, b):
            return pltpu.make_async_copy(rows_v.at[b], out_hbm.at[pl.ds(base + j * SC_CH, SC_CH)],
                                         psem.at[b])

        gather(0, 0).start()

        @pl.loop(0, SC_NCH, step=2)
        def _(j0):
            for b in range(2):
                j = j0 + b

                @pl.when(j + 1 < SC_NCH)
                def _():
                    @pl.when(j >= 1)
                    def _():
                        put(j - 1, 1 - b).wait()
                    gather(j + 1, 1 - b).start()

                gather(j, b).wait()
                put(j, b).start()

        put(SC_NCH - 2, 0).wait()
        put(SC_NCH - 1, 1).wait()

    return k(table, idx3)


SCD_TOK_PER_W = T // SC_WORKERS
SCD_NCH = SCD_TOK_PER_W // SC_CH


def _sc_dispatch(h1p, idx4):
    mesh = plsc.VectorSubcoreMesh(core_axis_name="c", subcore_axis_name="s")

    @functools.partial(
        pl.kernel, mesh=mesh,
        out_type=jax.ShapeDtypeStruct((XS_ROWS, D // 2), U32),
        scratch_types=[
            pltpu.VMEM((SCD_NCH * TOP_K, SC_CH), I32),
            pltpu.VMEM((2, SC_CH, D // 2), U32),
            pltpu.SemaphoreType.DMA((2,)),
            pltpu.SemaphoreType.DMA((2,)),
        ],
    )
    def k(h_hbm, idx_hbm, xs_hbm, idx_v, rows_v, gsem, psem):
        wid = lax.axis_index("s") * SC_CORES + lax.axis_index("c")
        base = wid * SCD_TOK_PER_W
        pltpu.sync_copy(idx_hbm.at[wid], idx_v)

        def get(c, b):
            return pltpu.make_async_copy(h_hbm.at[pl.ds(base + c * SC_CH, SC_CH)], rows_v.at[b],
                                         gsem.at[b])

        def puts(c, b):
            return [pltpu.make_async_copy(rows_v.at[b], xs_hbm.at[idx_v.at[c * TOP_K + kk]],
                                          psem.at[b]) for kk in range(TOP_K)]

        get(0, 0).start()

        @pl.loop(0, SCD_NCH, step=2)
        def _(c0):
            for b in range(2):
                c = c0 + b

                @pl.when(c + 1 < SCD_NCH)
                def _():
                    @pl.when(c >= 1)
                    def _():
                        for cp_ in puts(c - 1, 1 - b):
                            cp_.wait()
                    get(c + 1, 1 - b).start()

                get(c, b).wait()
                for cp_ in puts(c, b):
                    cp_.start()

        for cp_ in puts(SCD_NCH - 2, 0) + puts(SCD_NCH - 1, 1):
            cp_.wait()

    return k(h1p, idx4)


def _combine_dense_kernel(g_ref, h1_ref, tw_ref, lg_ref, lb_ref, o_ref):
    tw = tw_ref[...]
    ylo = jnp.zeros((COMB_TM, D // 2), F32)
    yhi = jnp.zeros((COMB_TM, D // 2), F32)
    for k in range(TOP_K):
        lo, hi = _unpack_bf16_pair(g_ref[k])
        wk = tw[:, k:k + 1]
        ylo = ylo + lo * wk
        yhi = yhi + hi * wk
    ff = jnp.concatenate([ylo, yhi], axis=1)
    o_ref[...] = _ln(DN_ALPHA * h1_ref[...] + ff, lg_ref[...], lb_ref[...])


def _combine_dense(g4, h1, topw, g, b, group):
    tm = COMB_TM
    t0 = group * (COMB_TG // tm)
    return pl.pallas_call(
        _combine_dense_kernel,
        grid=(COMB_TG // tm,),
        in_specs=[
            pl.BlockSpec((TOP_K, tm, D // 2), lambda i: (0, i, 0)),
            pl.BlockSpec((tm, D), lambda i: (t0 + i, 0)),
            pl.BlockSpec((tm, LANES), lambda i: (t0 + i, 0)),
            pl.BlockSpec((1, D), lambda i: (0, 0)),
            pl.BlockSpec((1, D), lambda i: (0, 0)),
        ],
        out_specs=pl.BlockSpec((tm, D), lambda i: (t0 + i, 0)),
        out_shape=jax.ShapeDtypeStruct((T, D), F32),
        input_output_aliases={1: 0},
        compiler_params=pltpu.CompilerParams(
            dimension_semantics=("arbitrary",),
            vmem_limit_bytes=40 * MIB),
        name="moe_combine_dense_ln2",
    )(g4, h1, topw, g, b)


def _pad_cols(a, n):
    return jnp.pad(a, ((0, 0), (0, n - a.shape[1])))


def kernel(x, mem, ln_in_g, ln_in_b, ln_mem_g, ln_mem_b, w_in, b_in, w_decay_f, b_decay_f,
           w_decay_b, b_decay_b, gla_norm_g, w_br_gla, w_br_fnet, w_br_mem, w_mem_kv, w_out,
           b_out, ln1_g, ln1_b, w_router, b_router, w_gu, b_gu, w_down, b_down, ln2_g, ln2_b):
    assert x.shape == (BATCH, SEQ, D) and w_in.shape[0] == 1
    row = lambda a: a.reshape(1, -1)
    x2 = x.reshape(T, D)
    w_in0, b_in0 = w_in[0], b_in[0]
    c_lr, c_fn, c_mq, c_gt = 3072, 3072 + 2 * GLA_LR, 3104 + FN_W, 3104 + FN_W + MQ_W
    w_main = jnp.concatenate([w_in0[:, :c_lr], w_in0[:, c_gt:]], axis=1).astype(BF16)
    b_main = row(jnp.concatenate([b_in0[:c_lr], b_in0[c_gt:]]))
    w_lr = _pad_cols(w_in0[:, c_lr:c_fn], LANES).astype(BF16)
    b_lr = _pad_cols(row(b_in0[c_lr:c_fn]), LANES)
    w_mq = w_in0[:, c_mq:c_gt].astype(BF16)
    b_mq = row(b_in0[c_mq:c_gt])
    w_fn = w_in0[:, c_fn:c_mq].astype(BF16)
    b_fn = row(b_in0[c_fn:c_mq])
    lng, lnb = row(ln_in_g), row(ln_in_b)

    proj, mq, lr = _inproj(x2, lng, lnb, w_main, b_main, w_mq, b_mq, w_lr, b_lr)

    zpad = jnp.zeros((LANES - 2 * GLA_LR, GLA_H * GLA_DK), F32)
    zlr = jnp.zeros((GLA_LR, GLA_H * GLA_DK), F32)
    wdf = jnp.concatenate([w_decay_f[0], zlr, zpad], axis=0).reshape(LANES, GLA_H, GLA_DK)
    wdb = jnp.concatenate([zlr, w_decay_b[0], zpad], axis=0).reshape(LANES, GLA_H, GLA_DK)
    wd = jnp.concatenate([wdf, wdb], axis=2).reshape(LANES, GLA_H * 2 * GLA_DK).astype(BF16)
    wd = jnp.concatenate([wd, wd], axis=0)
    bd = jnp.concatenate([b_decay_f[0].reshape(GLA_H, GLA_DK),
                          b_decay_b[0].reshape(GLA_H, GLA_DK)], axis=1).reshape(1, -1)
    og = _gla(proj, lr, wd, bd, row(gla_norm_g[0]))

    fbig, cwt, swt, f2, ccs, perm = _dft_tables(MERGE_TM)
    x4 = x.reshape(BATCH, FFT_N2, FFT_N1, D)
    z = _fft2(_fft1(x4, lng, lnb, w_fn, b_fn, fbig, cwt, swt), f2)

    kv = _memkv(mem.reshape(BATCH * MEM_LEN, D), row(ln_mem_g), row(ln_mem_b),
                w_mem_kv[0].astype(BF16))

    w_r = _pad_cols(w_router[0], LANES)
    wr_hi = w_r.astype(BF16)
    wr_lo = (w_r - wr_hi.astype(F32)).astype(BF16)
    b_r = jnp.concatenate([row(b_router[0]),
                           jnp.full((1, LANES - N_EXP), NEG_BIG, F32)], axis=1)
    h1, h1p, eidx, topw, cnt = _merge(
        x2, og, z, mq, proj, kv, lng, lnb,
        w_br_gla[0].astype(BF16), ccs, perm, w_br_fnet[0].astype(BF16),
        w_br_mem[0].astype(BF16), w_out[0].astype(BF16), row(b_out[0]),
        row(ln1_g[0]), row(ln1_b[0]), jnp.concatenate([wr_hi, wr_lo], axis=1), wr_hi, b_r)

    dest = _plan(eidx, cnt)
    counts = cnt[0, :N_EXP].astype(I32)
    dest_k = dest[:, :TOP_K]
    idx4 = dest_k.reshape(SC_WORKERS, SCD_NCH, SC_CH, TOP_K).transpose(0, 1, 3, 2)
    xs = _sc_dispatch(h1p, idx4.reshape(SC_WORKERS, SCD_NCH * TOP_K, SC_CH))
    ys = _experts(_work_items(counts), xs, w_gu[0], b_gu[0], w_down[0], b_down[0])
    out = h1
    for grp in range(COMB_GROUPS):
        dest_g = dest_k[grp * COMB_TG:(grp + 1) * COMB_TG].T.reshape(SC_WORKERS, SC_NCH, SC_CH)
        g4 = _sc_gather(ys, dest_g).reshape(TOP_K, COMB_TG, D // 2)
        out = _combine_dense(g4, out, topw, row(ln2_g[0]), row(ln2_b[0]), grp)
    return out.reshape(BATCH, SEQ, D)
```

```python
import functools
import math

import numpy as np
import jax
import jax.numpy as jnp
from jax import lax
from jax.experimental import pallas as pl
from jax.experimental.pallas import tpu as pltpu
from jax.experimental.pallas import tpu_sc as plsc

F32 = jnp.float32
BF16 = jnp.bfloat16
I32 = jnp.int32
U32 = jnp.uint32

D = 1024
BATCH = 4
SEQ = 4096
T = BATCH * SEQ
GLA_H = 4
GLA_DK = 128
GLA_DV = 256
GLA_LR = 16
GLA_TAU = 16.0
GLA_C = 64
FN_G = 4
FN_GW = 128
FN_W = 512
MEM_LEN = 256
MEM_H = 4
MEM_HD = 128
MQ_W = 512
N_EXP = 32
TOP_K = 4
D_FF = 1024
SW_LIMIT = 7.0
SW_ALPHA = 1.702
LN_EPS = 1e-5
RMS_EPS = 1e-6
DN_ALPHA = 2.0 ** 0.25
A_ROWS = T * TOP_K

FFT_N1 = 128
FFT_N2 = 32

LANES = 128
NEG_BIG = -1e30
MIB = 1024 * 1024


def _ln(x, g, b):
    mu = jnp.mean(x, axis=-1, keepdims=True)
    xc = x - mu
    var = jnp.mean(xc * xc, axis=-1, keepdims=True)
    return xc * lax.rsqrt(var + LN_EPS) * g + b


def _dot(a, b):
    return jnp.dot(a, b, preferred_element_type=F32)


def _dot_nt(a, b):
    return lax.dot_general(a, b, (((1,), (1,)), ((), ())), preferred_element_type=F32)


def _dot_tn(a, b):
    return lax.dot_general(a, b, (((0,), (0,)), ((), ())), preferred_element_type=F32)


def _interleave(*stages):
    live = list(stages)
    while live:
        for st in list(live):
            try:
                next(st)
            except StopIteration:
                live.remove(st)


def _split_bf16(a):
    hi = a.astype(BF16)
    return hi, (a - hi.astype(F32)).astype(BF16)


INPROJ_TM = 1024
INPROJ_TN = 3072


PROJ_W = 6 * 1024


def _inproj_kernel(x_ref, g_ref, b_ref, w_ref, bias_ref, wmq_ref, bmq_ref, wlr_ref, blr_ref,
                   proj_ref, mq_ref, lr_ref, hb_ref):
    @pl.when(pl.program_id(1) == 0)
    def _():
        hb = _ln(x_ref[...], g_ref[...], b_ref[...]).astype(BF16)
        hb_ref[...] = hb
        lr_ref[...] = _dot(hb, wlr_ref[...]) + blr_ref[...]
        mq_ref[...] = (_dot(hb, wmq_ref[...]) + bmq_ref[...]).astype(BF16)

    proj_ref[...] = (_dot(hb_ref[...], w_ref[...]) + bias_ref[...]).astype(BF16)


def _inproj(x2, ln_g, ln_b, w_main, b_main, w_mq, b_mq, w_lr, b_lr):
    tm, tn = INPROJ_TM, INPROJ_TN
    nj = PROJ_W // tn
    row = lambda i, j: (i, 0)
    const = lambda i, j: (0, 0)
    outs = (
        jax.ShapeDtypeStruct((T, PROJ_W), BF16),
        jax.ShapeDtypeStruct((T, MQ_W), BF16),
        jax.ShapeDtypeStruct((T, LANES), F32),
    )
    return pl.pallas_call(
        _inproj_kernel,
        grid=(T // tm, nj),
        in_specs=[
            pl.BlockSpec((tm, D), row),
            pl.BlockSpec((1, D), const),
            pl.BlockSpec((1, D), const),
            pl.BlockSpec((D, tn), lambda i, j: (0, j)),
            pl.BlockSpec((1, tn), lambda i, j: (0, j)),
            pl.BlockSpec((D, MQ_W), const),
            pl.BlockSpec((1, MQ_W), const),
            pl.BlockSpec((D, LANES), const),
            pl.BlockSpec((1, LANES), const),
        ],
        out_specs=[
            pl.BlockSpec((tm, tn), lambda i, j: (i, j)),
            pl.BlockSpec((tm, MQ_W), row),
            pl.BlockSpec((tm, LANES), row),
        ],
        out_shape=outs,
        scratch_shapes=[pltpu.VMEM((tm, D), BF16)],
        compiler_params=pltpu.CompilerParams(
            dimension_semantics=("arbitrary", "arbitrary"),
            vmem_limit_bytes=48 * MIB),
        name="ln_inproj",
    )(x2, ln_g, ln_b, w_main, b_main, w_mq, b_mq, w_lr, b_lr)


GLA_BULK = 256
GLA_FIN = 512
GLA_NCH = SEQ // GLA_C
GLA_CPB = GLA_BULK // GLA_C
GLA_PIPE = 4


def _gla_kernel(q_ref, k_ref, v_ref, r_ref, lr_ref, wd_ref, bd_ref, g_ref, cs_ref, o_ref,
                acc_ref, qin_ref, kin_ref, kst_ref, dec_ref, u_ref, stf_ref, stb_ref):
    C = GLA_C
    G = GLA_BULK
    DK = GLA_DK
    NG = SEQ // G
    scale = DK ** -0.5
    ii = lax.broadcasted_iota(I32, (G, G), 0)
    jj = lax.broadcasted_iota(I32, (G, G), 1)
    same = (ii // C) == (jj // C)
    lower = jnp.logical_and(same, ii >= jj)
    upper = jnp.logical_and(same, ii <= jj)
    is_fwd = lax.broadcasted_iota(I32, (G, 2 * DK), 1) < DK
    chunk_of_row = lax.broadcasted_iota(I32, (G, DK), 0) // C

    def stage_a(gi):
        rows = pl.ds(pl.multiple_of(gi * G, G), G)
        z = _dot(jnp.concatenate(_split_bf16(lr_ref[rows, :]), axis=1), wd_ref[...]) + bd_ref[...]
        yield
        la = -(jnp.maximum(-z, 0.0) + jnp.log(1.0 + jnp.exp(-jnp.abs(z)))) * (1.0 / GLA_TAU)
        la_hi, la_lo = _split_bf16(la)
        pre2 = _dot(cs_ref[...], jnp.concatenate([la_hi, la_lo], axis=1))
        yield
        pre = pre2[:, :2 * DK] + pre2[:, 2 * DK:]
        blast = jnp.concatenate(
            [jnp.broadcast_to(pre[ci * C + C - 1:ci * C + C, :], (C, 2 * DK))
             for ci in range(GLA_CPB)], axis=0)
        b = jnp.where(is_fwd, pre, blast - pre + la)
        qf32 = q_ref[rows, :].astype(F32)
        kf32 = k_ref[rows, :].astype(F32)
        q2 = jnp.concatenate([qf32, qf32], axis=1)
        k2 = jnp.concatenate([kf32, kf32], axis=1)
        qin_ref[rows, :] = (q2 * (scale * jnp.exp(b))).astype(BF16)
        kin_ref[rows, :] = (k2 * jnp.exp(-b)).astype(BF16)
        kst_ref[rows, :] = (k2 * jnp.exp(blast - b)).astype(BF16)
        dec = jnp.exp(blast)
        for ci in range(GLA_CPB):
            dec_ref[pl.ds(gi * GLA_CPB + ci, 1), :] = dec[ci * C:ci * C + 1, :]

    def stage_b(gi):
        rows = pl.ds(pl.multiple_of(gi * G, G), G)
        qi = qin_ref[rows, :]
        ki = kin_ref[rows, :]
        ks = kst_ref[rows, :]
        vb = v_ref[rows, :]
        att_f = _dot_nt(qi[:, :DK], ki[:, :DK])
        att_b = _dot_nt(qi[:, DK:], ki[:, DK:])
        yield
        att = jnp.where(lower, att_f, 0.0) + jnp.where(upper, att_b, 0.0)
        acc_ref[rows, :] = _dot(att.astype(BF16), vb)
        yield
        ksb = jnp.concatenate(
            [jnp.where(chunk_of_row == ci, ks[:, d * DK:(d + 1) * DK], jnp.zeros((G, DK), BF16))
             for d in range(2) for ci in range(GLA_CPB)], axis=1)
        u = _dot_tn(vb, ksb)
        for d in range(2):
            for ci in range(GLA_CPB):
                col = (d * GLA_CPB + ci) * DK
                u_ref[d, gi * GLA_CPB + ci] = u[:, col:col + DK]

    P = GLA_PIPE
    _interleave(*[stage_a(j) for j in range(P)])

    def bulk(i, carry):
        g = P * i
        _interleave(*[st for j in range(P) for st in (stage_b(g - P + j), stage_a(g + j))])
        return carry

    lax.fori_loop(1, NG // P, bulk, 0)
    _interleave(*[stage_b(NG - P + j) for j in range(P)])

    stf_ref[...] = jnp.zeros_like(stf_ref)
    stb_ref[...] = jnp.zeros_like(stb_ref)

    def one(n, d, st_ref):
        lanes = slice(d * DK, (d + 1) * DK)
        rows = pl.ds(pl.multiple_of(n * C, C), C)
        st = st_ref[...]
        acc_ref[rows, :] += _dot_nt(qin_ref[rows, lanes], st.astype(BF16))
        st_ref[...] = st * dec_ref[pl.ds(n, 1), :][:, lanes] + u_ref[d, n]

    def step(i, carry):
        one(i, 0, stf_ref)
        one(GLA_NCH - 1 - i, 1, stb_ref)
        return carry

    lax.fori_loop(0, GLA_NCH, step, 0, unroll=16)

    def fin(gi, carry):
        rows = pl.ds(pl.multiple_of(gi * GLA_FIN, GLA_FIN), GLA_FIN)
        o = acc_ref[rows, :]
        o = o * lax.rsqrt(jnp.mean(o * o, axis=-1, keepdims=True) + RMS_EPS) * g_ref[...]
        rg = r_ref[rows, :].astype(F32)
        o_ref[rows, :] = (o * (rg * jax.nn.sigmoid(rg))).astype(BF16)
        return carry

    lax.fori_loop(0, SEQ // GLA_FIN, fin, 0)


def _gla(proj, lr, wd, bd, g):
    i = np.arange(GLA_BULK)
    cs = ((i[:, None] // GLA_C) == (i[None, :] // GLA_C)) & (i[:, None] >= i[None, :])
    cs = jnp.asarray(cs, dtype=F32).astype(BF16)
    v_blk = 1024 // GLA_DV
    return pl.pallas_call(
        _gla_kernel,
        grid=(BATCH, GLA_H),
        in_specs=[
            pl.BlockSpec((SEQ, GLA_DK), lambda b, h: (b, h)),
            pl.BlockSpec((SEQ, GLA_DK), lambda b, h: (b, GLA_H + h)),
            pl.BlockSpec((SEQ, GLA_DV), lambda b, h: (b, v_blk + h)),
            pl.BlockSpec((SEQ, GLA_DV), lambda b, h: (b, 2 * v_blk + h)),
            pl.BlockSpec((SEQ, LANES), lambda b, h: (b, 0)),
            pl.BlockSpec((2 * LANES, 2 * GLA_DK), lambda b, h: (0, h)),
            pl.BlockSpec((1, 2 * GLA_DK), lambda b, h: (0, h)),
            pl.BlockSpec((1, GLA_DV), lambda b, h: (0, 0)),
            pl.BlockSpec((GLA_BULK, GLA_BULK), lambda b, h: (0, 0)),
        ],
        out_specs=pl.BlockSpec((SEQ, GLA_DV), lambda b, h: (b, h)),
        out_shape=jax.ShapeDtypeStruct((T, GLA_H * GLA_DV), BF16),
        scratch_shapes=[
            pltpu.VMEM((SEQ, GLA_DV), F32),
            pltpu.VMEM((SEQ, 2 * GLA_DK), BF16),
            pltpu.VMEM((SEQ, 2 * GLA_DK), BF16),
            pltpu.VMEM((SEQ, 2 * GLA_DK), BF16),
            pltpu.VMEM((GLA_NCH, 2 * GLA_DK), F32),
            pltpu.VMEM((2, GLA_NCH, GLA_DV, GLA_DK), F32),
            pltpu.VMEM((GLA_DV, GLA_DK), F32),
            pltpu.VMEM((GLA_DV, GLA_DK), F32),
        ],
        compiler_params=pltpu.CompilerParams(
            dimension_semantics=("arbitrary", "arbitrary"),
            vmem_limit_bytes=58 * MIB),
        name="gla",
    )(proj, proj, proj, proj, lr, wd, bd, g, cs)


FFT1_S = 16
FFT1_SUB = 2
FFT1_ROWS = FFT_N2 * FFT1_S
FFT2_KB = 8


def _fft1_kernel(x_ref, g_ref, b_ref, w_ref, bias_ref, fbig_ref, cw_ref, sw_ref, o_ref):
    sh = FFT1_S // FFT1_SUB
    rows = FFT_N2 * sh
    res = {}

    def sub(h):
        xv = x_ref[:, h * sh:(h + 1) * sh, :].reshape(rows, D)
        hb = _ln(xv, g_ref[...], b_ref[...]).astype(BF16)
        fn = (_dot(hb, w_ref[...]) + bias_ref[...]).astype(BF16)
        yield
        a = _dot(fbig_ref[...], fn)
        yield
        ar = a[:rows]
        ai = a[rows:]
        cw = jnp.concatenate([cw_ref[h]] * (FN_W // LANES), axis=1)
        sw = jnp.concatenate([sw_ref[h]] * (FN_W // LANES), axis=1)
        res[h] = ((ar * cw + ai * sw).reshape(FFT_N2, sh, FN_W),
                  (ai * cw - ar * sw).reshape(FFT_N2, sh, FN_W))

    _interleave(*[sub(h) for h in range(FFT1_SUB)])
    for ri in range(2):
        o_ref[ri] = jnp.concatenate([res[h][ri] for h in range(FFT1_SUB)], axis=1).astype(BF16)


def _fft1(x4, ln_g, ln_b, w_fn, b_fn, fbig, cwt, swt):
    s = FFT1_S
    const = lambda b, j: (0, 0)
    return pl.pallas_call(
        _fft1_kernel,
        grid=(BATCH, FFT_N1 // s),
        in_specs=[
            pl.BlockSpec((None, FFT_N2, s, D), lambda b, j: (b, 0, j, 0)),
            pl.BlockSpec((1, D), const),
            pl.BlockSpec((1, D), const),
            pl.BlockSpec((D, FN_W), const),
            pl.BlockSpec((1, FN_W), const),
            pl.BlockSpec((2 * FFT1_ROWS // FFT1_SUB, FFT1_ROWS // FFT1_SUB), const),
            pl.BlockSpec((FFT1_SUB, FFT1_ROWS // FFT1_SUB, LANES), lambda b, j: (j, 0, 0)),
            pl.BlockSpec((FFT1_SUB, FFT1_ROWS // FFT1_SUB, LANES), lambda b, j: (j, 0, 0)),
        ],
        out_specs=pl.BlockSpec((None, 2, FFT_N2, s, FN_W), lambda b, j: (b, 0, 0, j, 0)),
        out_shape=jax.ShapeDtypeStruct((BATCH, 2, FFT_N2, FFT_N1, FN_W), BF16),
        compiler_params=pltpu.CompilerParams(
            dimension_semantics=("arbitrary", "arbitrary"),
            vmem_limit_bytes=40 * MIB),
        name="fft_stage1",
    )(x4, ln_g, ln_b, w_fn, b_fn, fbig, cwt, swt)


def _fft2_kernel(d_ref, f2_ref, o_ref):
    f2 = f2_ref[...]
    for kk in range(FFT2_KB):
        z = _dot(f2, jnp.concatenate([d_ref[0, kk], d_ref[1, kk]], axis=0))
        o_ref[0, kk] = z[:FFT_N1].astype(BF16)
        o_ref[1, kk] = z[FFT_N1:].astype(BF16)


def _fft2(dmat, f2):
    kb = FFT2_KB
    blk = (None, 2, kb, FFT_N1, FN_W)
    return pl.pallas_call(
        _fft2_kernel,
        grid=(BATCH, FFT_N2 // kb),
        in_specs=[
            pl.BlockSpec(blk, lambda b, j: (b, 0, j, 0, 0)),
            pl.BlockSpec((2 * FFT_N1, 2 * FFT_N1), lambda b, j: (0, 0)),
        ],
        out_specs=pl.BlockSpec(blk, lambda b, j: (b, 0, j, 0, 0)),
        out_shape=jax.ShapeDtypeStruct((BATCH, 2, FFT_N2, FFT_N1, FN_W), BF16),
        compiler_params=pltpu.CompilerParams(
            dimension_semantics=("arbitrary", "arbitrary")),
        name="fft_stage2",
    )(dmat, f2)


def _dft_tables(merge_tm):
    s = FFT1_S // FFT1_SUB
    n2 = np.arange(FFT_N2, dtype=np.float64)
    n1 = np.arange(FFT_N1, dtype=np.float64)
    th = 2.0 * np.pi * np.outer(n2, n2) / FFT_N2
    f1 = np.stack([np.cos(th), -np.sin(th)]) / math.sqrt(SEQ)
    fbig = np.einsum("rkn,st->rksnt", f1, np.eye(s)).reshape(2 * FFT_N2 * s, FFT_N2 * s)
    tw = 2.0 * np.pi * np.outer(n2, n1) / SEQ
    tw = tw.reshape(FFT_N2, FFT_N1 // s, s).transpose(1, 0, 2).reshape(FFT_N1 // s, FFT_N2 * s)
    cwt = np.broadcast_to(np.cos(tw)[:, :, None], tw.shape + (LANES,))
    swt = np.broadcast_to(np.sin(tw)[:, :, None], tw.shape + (LANES,))
    th1 = 2.0 * np.pi * np.outer(n1, n1) / FFT_N1
    c1, s1 = np.cos(th1), np.sin(th1)
    f2 = np.block([[c1, s1], [-s1, c1]])
    cc = np.arange(FN_GW, dtype=np.float64)
    thc = 2.0 * np.pi * np.outer(cc, cc) / FN_GW
    ccs = np.concatenate([np.cos(thc), np.sin(thc)], axis=0) / math.sqrt(FN_GW)
    k1n = merge_tm // FFT_N2
    r = np.arange(merge_tm)
    perm = np.zeros((merge_tm, merge_tm))
    perm[r, (r % FFT_N2) * k1n + r // FFT_N2] = 1.0
    as32 = lambda a: jnp.asarray(np.ascontiguousarray(a), dtype=F32)
    return (as32(fbig).astype(BF16), as32(cwt), as32(swt), as32(f2).astype(BF16),
            as32(ccs).astype(BF16), as32(perm).astype(BF16))


def _memkv_kernel(m_ref, g_ref, b_ref, w_ref, o_ref):
    mn = _ln(m_ref[...], g_ref[...], b_ref[...]).astype(BF16)
    o_ref[...] = _dot(mn, w_ref[...]).astype(BF16)


def _memkv(mem2, g, b, w):
    return pl.pallas_call(
        _memkv_kernel,
        grid=(BATCH,),
        in_specs=[
            pl.BlockSpec((MEM_LEN, D), lambda i: (i, 0)),
            pl.BlockSpec((1, D), lambda i: (0, 0)),
            pl.BlockSpec((1, D), lambda i: (0, 0)),
            pl.BlockSpec((D, 2 * MQ_W), lambda i: (0, 0)),
        ],
        out_specs=pl.BlockSpec((MEM_LEN, 2 * MQ_W), lambda i: (i, 0)),
        out_shape=jax.ShapeDtypeStruct((BATCH * MEM_LEN, 2 * MQ_W), BF16),
        compiler_params=pltpu.CompilerParams(dimension_semantics=("arbitrary",)),
        name="mem_kv",
    )(mem2, g, b, w)


MERGE_TM = 512
MERGE_K1 = MERGE_TM // FFT_N2


def _pack_bf16_pair(v):
    n = v.shape[1] // 2
    bits = lax.bitcast_convert_type(v.astype(BF16).astype(F32), U32)
    return (bits[:, n:] & jnp.uint32(0xFFFF0000)) | (bits[:, :n] >> 16)


def _unpack_bf16_pair(p):
    lo = lax.bitcast_convert_type(p << 16, F32)
    hi = lax.bitcast_convert_type(p & jnp.uint32(0xFFFF0000), F32)
    return lo, hi


def _merge_kernel(x_ref, og_ref, zr_ref, zi_ref, mq_ref, gt_ref, kv_ref,
                  lng_ref, lnb_ref, wg_ref, ccs_ref, perm_ref, wf_ref, wm_ref, wo_ref, bo_ref,
                  l1g_ref, l1b_ref, wr2_ref, wrh_ref, br_ref,
                  h1_ref, h1p_ref, eidx_ref, topw_ref, cnt_ref):
    tm = MERGE_TM
    y = {}

    def branch_fnet():
        zr = zr_ref[...].reshape(tm, FN_W)
        zi = zi_ref[...].reshape(tm, FN_W)
        ys = []
        for g in range(FN_G):
            sl = slice(g * FN_GW, (g + 1) * FN_GW)
            ys.append(_dot(jnp.concatenate([zr[:, sl], zi[:, sl]], axis=1), ccs_ref[...]))
        yield
        yp = _dot(perm_ref[...], jnp.concatenate(ys, axis=1).astype(BF16))
        yield
        y["fnet"] = _dot(yp.astype(BF16), wf_ref[...])

    def branch_mem():
        heads = [slice(hd * MEM_HD, (hd + 1) * MEM_HD) for hd in range(MEM_H)]
        ss = [_dot_nt(mq_ref[:, sl], kv_ref[:, sl]) for sl in heads]
        yield
        oms = []
        for hd, s in enumerate(ss):
            s = s * (MEM_HD ** -0.5)
            s = s - jnp.max(s, axis=-1, keepdims=True)
            p = jnp.exp(s)
            p = p * (1.0 / jnp.sum(p, axis=-1, keepdims=True))
            oms.append(_dot(p.astype(BF16),
                            kv_ref[:, MQ_W + hd * MEM_HD:MQ_W + (hd + 1) * MEM_HD]))
        yield
        y["mem"] = _dot(jnp.concatenate(oms, axis=1).astype(BF16), wm_ref[...])

    def branch_gla():
        y["gla"] = _dot(og_ref[...], wg_ref[...])
        yield

    _interleave(branch_fnet(), branch_mem(), branch_gla())

    def gate(c):
        return 0.5 + 0.5 * jnp.tanh(0.5 * gt_ref[:, c * D:(c + 1) * D].astype(F32))

    merged = gate(0) * y["gla"] + gate(1) * y["fnet"] + gate(2) * y["mem"]
    mix = _dot(merged.astype(BF16), wo_ref[...]) + bo_ref[...]
    h = _ln(x_ref[...], lng_ref[...], lnb_ref[...])
    h1 = _ln(DN_ALPHA * h + mix, l1g_ref[...], l1b_ref[...])
    h1_ref[...] = h1
    h1p_ref[...] = _pack_bf16_pair(h1)

    h_hi, h_lo = _split_bf16(h1)
    d2 = _dot(h_hi, wr2_ref[...])
    l = d2[:, :LANES] + d2[:, LANES:] + _dot(h_lo, wrh_ref[...]) + br_ref[...]
    lane = lax.broadcasted_iota(I32, (tm, LANES), 1)
    vals, idxs = [], []
    for _ in range(TOP_K):
        m = jnp.max(l, axis=-1, keepdims=True)
        idx = jnp.min(jnp.where(l == m, lane, LANES), axis=-1, keepdims=True)
        vals.append(m)
        idxs.append(idx)
        l = jnp.where(lane == idx, -jnp.inf, l)
    es = [jnp.exp(v - vals[0]) for v in vals]
    den = es[0] + es[1] + es[2] + es[3]
    eo = jnp.zeros((tm, LANES), I32)
    wo = jnp.zeros((tm, LANES), F32)
    chosen = jnp.zeros((tm, LANES), F32)
    for k in range(TOP_K):
        eo = jnp.where(lane == k, idxs[k], eo)
        wo = jnp.where(lane == k, es[k] / den, wo)
        chosen = chosen + jnp.where(lane == idxs[k], 1.0, 0.0)
    eidx_ref[...] = eo
    topw_ref[...] = wo

    @pl.when(pl.program_id(0) == 0)
    def _():
        cnt_ref[...] = jnp.zeros_like(cnt_ref)

    cnt_ref[...] += jnp.broadcast_to(jnp.sum(chosen, axis=0, keepdims=True), cnt_ref.shape)


def _merge(x2, og, z, mq, gates, kv, lng, lnb, wg, ccs, perm, wf, wm, wo, bo, l1g, l1b,
           wr2, wrh, br):
    tm = MERGE_TM
    per_b = SEQ // tm
    row = lambda i: (i, 0)
    const = lambda i: (0, 0)
    zblk = (None, None, FFT_N2, MERGE_K1, FN_W)
    outs = (
        jax.ShapeDtypeStruct((T, D), F32),
        jax.ShapeDtypeStruct((T, D // 2), U32),
        jax.ShapeDtypeStruct((T, LANES), I32),
        jax.ShapeDtypeStruct((T, LANES), F32),
        jax.ShapeDtypeStruct((8, LANES), F32),
    )
    return pl.pallas_call(
        _merge_kernel,
        grid=(T // tm,),
        in_specs=[
            pl.BlockSpec((tm, D), row),
            pl.BlockSpec((tm, D), row),
            pl.BlockSpec(zblk, lambda i: (i // per_b, 0, 0, i % per_b, 0)),
            pl.BlockSpec(zblk, lambda i: (i // per_b, 1, 0, i % per_b, 0)),
            pl.BlockSpec((tm, MQ_W), row),
            pl.BlockSpec((tm, 3 * D), lambda i: (i, 1)),
            pl.BlockSpec((MEM_LEN, 2 * MQ_W), lambda i: (i // per_b, 0)),
            pl.BlockSpec((1, D), const), pl.BlockSpec((1, D), const),
            pl.BlockSpec((D, D), const),
            pl.BlockSpec((2 * FN_GW, FN_GW), const),
            pl.BlockSpec((tm, tm), const),
            pl.BlockSpec((FN_W, D), const),
            pl.BlockSpec((MQ_W, D), const),
            pl.BlockSpec((D, D), const),
            pl.BlockSpec((1, D), const),
            pl.BlockSpec((1, D), const), pl.BlockSpec((1, D), const),
            pl.BlockSpec((D, 2 * LANES), const),
            pl.BlockSpec((D, LANES), const),
            pl.BlockSpec((1, LANES), const),
        ],
        out_specs=[
            pl.BlockSpec((tm, D), row),
            pl.BlockSpec((tm, D // 2), row),
            pl.BlockSpec((tm, LANES), row),
            pl.BlockSpec((tm, LANES), row),
            pl.BlockSpec((8, LANES), const),
        ],
        out_shape=outs,
        compiler_params=pltpu.CompilerParams(
            dimension_semantics=("arbitrary",),
            vmem_limit_bytes=58 * MIB),
        name="merge_ln1_router",
    )(x2, og, z, z, mq, gates, kv, lng, lnb, wg, ccs, perm, wf, wm, wo, bo, l1g, l1b,
      wr2, wrh, br)


PLAN_TP = 1024


def _expert_onehots(e, lane):
    onehots = [lane == e[:, k:k + 1] for k in range(TOP_K)]
    mf = jnp.zeros(lane.shape, F32)
    for oh in onehots:
        mf = mf + jnp.where(oh, 1.0, 0.0)
    return onehots, mf


def _plan_kernel(e_ref, tot_ref, dest_ref, cnt_ref, off_ref):
    i = pl.program_id(0)
    tp = PLAN_TP
    lane = lax.broadcasted_iota(I32, (tp, LANES), 1)
    onehots, mf = _expert_onehots(e_ref[...], lane)

    @pl.when(i == 0)
    def _():
        tot = tot_ref[0:1, :]
        padded = jnp.floor((tot + (MOE_BM - 1)) * (1.0 / MOE_BM)) * MOE_BM
        lane1 = lax.broadcasted_iota(I32, (1, LANES), 1)
        inc = padded
        for s in (1, 2, 4, 8, 16, 32, 64):
            inc = inc + jnp.where(lane1 >= s, pltpu.roll(inc, s, 1), 0.0)
        off_ref[...] = inc - padded
        cnt_ref[...] = jnp.zeros_like(cnt_ref)

    ri = lax.broadcasted_iota(I32, (tp, tp), 0)
    ci = lax.broadcasted_iota(I32, (tp, tp), 1)
    ltri = jnp.where(ri > ci, 1.0, 0.0).astype(BF16)
    rank = _dot(ltri, mf.astype(BF16)) + cnt_ref[...] + off_ref[...]
    out = jnp.zeros((tp, LANES), I32)
    for k in range(TOP_K):
        dk = jnp.sum(jnp.where(onehots[k], rank, 0.0), axis=-1, keepdims=True)
        out = jnp.where(lane == k, dk.astype(I32), out)
    dest_ref[...] = out
    cnt_ref[...] += jnp.sum(mf, axis=0, keepdims=True)


def _plan(eidx, cnt):
    tp = PLAN_TP
    return pl.pallas_call(
        _plan_kernel,
        grid=(T // tp,),
        in_specs=[pl.BlockSpec((tp, LANES), lambda i: (i, 0)),
                  pl.BlockSpec((8, LANES), lambda i: (0, 0))],
        out_specs=pl.BlockSpec((tp, LANES), lambda i: (i, 0)),
        out_shape=jax.ShapeDtypeStruct((T, LANES), I32),
        scratch_shapes=[pltpu.VMEM((1, LANES), F32), pltpu.VMEM((1, LANES), F32)],
        compiler_params=pltpu.CompilerParams(dimension_semantics=("arbitrary",)),
        name="route_plan",
    )(eidx, cnt)


MOE_BM = 512
MOE_NW = A_ROWS // MOE_BM + N_EXP
XS_ROWS = MOE_NW * MOE_BM
MOE_FF_SLICES = 2


def _expert_kernel(we_ref, wb_ref, wv_ref, wfe_ref, wsl_ref, wnx_ref,
                   x_ref, wgu_hbm, bgu_ref, wdn_hbm, bdn_ref, o_ref,
                   wgu_f32, wdn_f32, wgu_bf, wdn_bf, sems):
    w = pl.program_id(0)
    e = we_ref[w]

    def weight_copies(expert, slot):
        return (pltpu.make_async_copy(wgu_hbm.at[expert], wgu_f32.at[slot], sems.at[slot, 0]),
                pltpu.make_async_copy(wdn_hbm.at[expert], wdn_f32.at[slot], sems.at[slot, 1]))

    @pl.when(w == 0)
    def _():
        for cp_ in weight_copies(e, 0):
            cp_.start()

    first = wfe_ref[w] == 1
    slot = wsl_ref[w]

    @pl.when(first)
    def _():
        for cp_ in weight_copies(e, slot):
            cp_.wait()
        nxt = wnx_ref[w]

        @pl.when(nxt >= 0)
        def _():
            for cp_ in weight_copies(nxt, 1 - slot):
                cp_.start()

    hw = D_FF // MOE_FF_SLICES
    col_slices = [(slice(hf * hw, (hf + 1) * hw), slice(D_FF + hf * hw, D_FF + (hf + 1) * hw))
                  for hf in range(MOE_FF_SLICES)]

    def cast_weights():
        for gc, uc in col_slices:
            wgu_bf[:, gc] = wgu_f32[slot, :, gc].astype(BF16)
            wgu_bf[:, uc] = wgu_f32[slot, :, uc].astype(BF16)
            yield
        for hf in range(MOE_FF_SLICES):
            rs = slice(hf * hw, (hf + 1) * hw)
            wdn_bf[rs, :] = wdn_f32[slot, rs, :].astype(BF16)
            yield

    def ffn(rows):
        xlo, xhi = _unpack_bf16_pair(x_ref[rows, :])
        xb = jnp.concatenate([xlo.astype(BF16), xhi.astype(BF16)], axis=1)
        bgu = bgu_ref[pl.ds(e, 1), :]
        gus = []
        for gc, uc in col_slices:
            gus.append((_dot(xb, wgu_bf[:, gc]) + bgu[:, gc], _dot(xb, wgu_bf[:, uc]) + bgu[:, uc]))
            yield
        out = bdn_ref[pl.ds(e, 1), :]
        for hf, (g, u) in enumerate(gus):
            gate = jnp.minimum(g, SW_LIMIT)
            up = jnp.clip(u, -SW_LIMIT, SW_LIMIT)
            act = (up + 1.0) * (gate * jax.nn.sigmoid(SW_ALPHA * gate))
            out = out + _dot(act.astype(BF16), wdn_bf[hf * hw:(hf + 1) * hw, :])
            if hf + 1 < MOE_FF_SLICES:
                yield
        o_ref[rows, :] = _pack_bf16_pair(out)
        if rows != slice(None):
            o_ref[rows.stop:, :] = jnp.zeros((MOE_BM - rows.stop, D // 2), U32)

    for code, rows in ((1, slice(None)), (2, slice(0, MOE_BM // 2))):
        @pl.when(jnp.logical_and(wv_ref[w] == code, first))
        def _(rows=rows):
            _interleave(cast_weights(), ffn(rows))

        @pl.when(jnp.logical_and(wv_ref[w] == code, jnp.logical_not(first)))
        def _(rows=rows):
            _interleave(ffn(rows))

    @pl.when(wv_ref[w] == 0)
    def _():
        o_ref[...] = jnp.zeros_like(o_ref)


def _experts(meta, xs, w_gu, b_gu, w_down, b_down):
    return pl.pallas_call(
        _expert_kernel,
        grid_spec=pltpu.PrefetchScalarGridSpec(
            num_scalar_prefetch=len(meta),
            grid=(MOE_NW,),
            in_specs=[
                pl.BlockSpec((MOE_BM, D // 2), lambda w, we, wb, *_: (wb[w], 0)),
                pl.BlockSpec(memory_space=pl.ANY),
                pl.BlockSpec((N_EXP, 2 * D_FF), lambda w, *_: (0, 0)),
                pl.BlockSpec(memory_space=pl.ANY),
                pl.BlockSpec((N_EXP, D), lambda w, *_: (0, 0)),
            ],
            out_specs=pl.BlockSpec((MOE_BM, D // 2), lambda w, *_: (w, 0)),
            scratch_shapes=[
                pltpu.VMEM((2, D, 2 * D_FF), F32),
                pltpu.VMEM((2, D_FF, D), F32),
                pltpu.VMEM((D, 2 * D_FF), BF16),
                pltpu.VMEM((D_FF, D), BF16),
                pltpu.SemaphoreType.DMA((2, 2)),
            ],
        ),
        out_shape=jax.ShapeDtypeStruct((XS_ROWS, D // 2), U32),
        compiler_params=pltpu.CompilerParams(
            dimension_semantics=("arbitrary",),
            vmem_limit_bytes=56 * MIB),
        name="moe_experts",
    )(*meta, xs, w_gu, b_gu, w_down, b_down)


def _work_items(counts):
    n_e = (counts + MOE_BM - 1) // MOE_BM
    item_end = jnp.cumsum(n_e)
    total = item_end[-1]
    w = jnp.arange(MOE_NW, dtype=I32)
    valid = w < total
    wc = jnp.minimum(w, total - 1)
    e_w = jnp.sum((item_end[None, :] <= wc[:, None]).astype(I32), axis=1)
    e_w = jnp.minimum(e_w, N_EXP - 1)
    rows_here = counts[e_w] - (wc - (item_end - n_e)[e_w]) * MOE_BM
    valid = jnp.where(valid, jnp.where(rows_here <= MOE_BM // 2, 2, 1), 0)
    prev_e = jnp.concatenate([jnp.full((1,), -1, I32), e_w[:-1]])
    fe = (e_w != prev_e).astype(I32)
    slot = (jnp.cumsum(fe) - 1) % 2
    first_at = jnp.where(fe == 1, w, MOE_NW)
    next_first = jnp.concatenate([lax.cummin(first_at, reverse=True)[1:],
                                  jnp.full((1,), MOE_NW, I32)])
    nxt = jnp.where(next_first < MOE_NW, e_w[jnp.minimum(next_first, MOE_NW - 1)], -1)
    return tuple(a.astype(I32) for a in (e_w, wc, valid, fe, slot, nxt))


COMB_TM = 512
SC_CORES = 2
SC_SUBCORES = 16
SC_WORKERS = SC_CORES * SC_SUBCORES
SC_CH = 64
COMB_GROUPS = 4
COMB_TG = T // COMB_GROUPS
SC_ROWS_PER_W = COMB_TG * TOP_K // SC_WORKERS
SC_NCH = SC_ROWS_PER_W // SC_CH


def _sc_gather(table, idx3):
    mesh = plsc.VectorSubcoreMesh(core_axis_name="c", subcore_axis_name="s")

    @functools.partial(
        pl.kernel, mesh=mesh,
        out_type=jax.ShapeDtypeStruct((COMB_TG * TOP_K, D // 2), U32),
        scratch_types=[
            pltpu.VMEM((SC_NCH, SC_CH), I32),
            pltpu.VMEM((2, SC_CH, D // 2), U32),
            pltpu.SemaphoreType.DMA((2,)),
            pltpu.SemaphoreType.DMA((2,)),
        ],
    )
    def k(table_hbm, idx_hbm, out_hbm, idx_v, rows_v, gsem, psem):
        wid = lax.axis_index("s") * SC_CORES + lax.axis_index("c")
        base = wid * SC_ROWS_PER_W
        pltpu.sync_copy(idx_hbm.at[wid], idx_v)

        def gather(j, b):
            return pltpu.make_async_copy(table_hbm.at[idx_v.at[j]], rows_v.at[b], gsem.at[b])

        def put(j, b):
            return pltpu.make_async_copy(rows_v.at[b], out_hbm.at[pl.ds(base + j * SC_CH, SC_CH)],
                                         psem.at[b])

        gather(0, 0).start()

        @pl.loop(0, SC_NCH, step=2)
        def _(j0):
            for b in range(2):
                j = j0 + b

                @pl.when(j + 1 < SC_NCH)
                def _():
                    @pl.when(j >= 1)
                    def _():
                        put(j - 1, 1 - b).wait()
                    gather(j + 1, 1 - b).start()

                gather(j, b).wait()
                put(j, b).start()

        put(SC_NCH - 2, 0).wait()
        put(SC_NCH - 1, 1).wait()

    return k(table, idx3)


SCD_TOK_PER_W = T // SC_WORKERS
SCD_NCH = SCD_TOK_PER_W // SC_CH


def _sc_dispatch(h1p, idx4):
    mesh = plsc.VectorSubcoreMesh(core_axis_name="c", subcore_axis_name="s")

    @functools.partial(
        pl.kernel, mesh=mesh,
        out_type=jax.ShapeDtypeStruct((XS_ROWS, D // 2), U32),
        scratch_types=[
            pltpu.VMEM((SCD_NCH * TOP_K, SC_CH), I32),
            pltpu.VMEM((2, SC_CH, D // 2), U32),
            pltpu.SemaphoreType.DMA((2,)),
            pltpu.SemaphoreType.DMA((2,)),
        ],
    )
    def k(h_hbm, idx_hbm, xs_hbm, idx_v, rows_v, gsem, psem):
        wid = lax.axis_index("s") * SC_CORES + lax.axis_index("c")
        base = wid * SCD_TOK_PER_W
        pltpu.sync_copy(idx_hbm.at[wid], idx_v)

        def get(c, b):
            return pltpu.make_async_copy(h_hbm.at[pl.ds(base + c * SC_CH, SC_CH)], rows_v.at[b],
                                         gsem.at[b])

        def puts(c, b):
            return [pltpu.make_async_copy(rows_v.at[b], xs_hbm.at[idx_v.at[c * TOP_K + kk]],
                                          psem.at[b]) for kk in range(TOP_K)]

        get(0, 0).start()

        @pl.loop(0, SCD_NCH, step=2)
        def _(c0):
            for b in range(2):
                c = c0 + b

                @pl.when(c + 1 < SCD_NCH)
                def _():
                    @pl.when(c >= 1)
                    def _():
                        for cp_ in puts(c - 1, 1 - b):
                            cp_.wait()
                    get(c + 1, 1 - b).start()

                get(c, b).wait()
                for cp_ in puts(c, b):
                    cp_.start()

        for cp_ in puts(SCD_NCH - 2, 0) + puts(SCD_NCH - 1, 1):
            cp_.wait()

    return k(h1p, idx4)


def _combine_dense_kernel(g_ref, h1_ref, tw_ref, lg_ref, lb_ref, o_ref):
    tw = tw_ref[...]
    ylo = jnp.zeros((COMB_TM, D // 2), F32)
    yhi = jnp.zeros((COMB_TM, D // 2), F32)
    for k in range(TOP_K):
        lo, hi = _unpack_bf16_pair(g_ref[k])
        wk = tw[:, k:k + 1]
        ylo = ylo + lo * wk
        yhi = yhi + hi * wk
    ff = jnp.concatenate([ylo, yhi], axis=1)
    o_ref[...] = _ln(DN_ALPHA * h1_ref[...] + ff, lg_ref[...], lb_ref[...])


def _combine_dense(g4, h1, topw, g, b, group):
    tm = COMB_TM
    t0 = group * (COMB_TG // tm)
    return pl.pallas_call(
        _combine_dense_kernel,
        grid=(COMB_TG // tm,),
        in_specs=[
            pl.BlockSpec((TOP_K, tm, D // 2), lambda i: (0, i, 0)),
            pl.BlockSpec((tm, D), lambda i: (t0 + i, 0)),
            pl.BlockSpec((tm, LANES), lambda i: (t0 + i, 0)),
            pl.BlockSpec((1, D), lambda i: (0, 0)),
            pl.BlockSpec((1, D), lambda i: (0, 0)),
        ],
        out_specs=pl.BlockSpec((tm, D), lambda i: (t0 + i, 0)),
        out_shape=jax.ShapeDtypeStruct((T, D), F32),
        input_output_aliases={1: 0},
        compiler_params=pltpu.CompilerParams(
            dimension_semantics=("arbitrary",),
            vmem_limit_bytes=40 * MIB),
        name="moe_combine_dense_ln2",
    )(g4, h1, topw, g, b)


def _pad_cols(a, n):
    return jnp.pad(a, ((0, 0), (0, n - a.shape[1])))


def kernel(x, mem, ln_in_g, ln_in_b, ln_mem_g, ln_mem_b, w_in, b_in, w_decay_f, b_decay_f,
           w_decay_b, b_decay_b, gla_norm_g, w_br_gla, w_br_fnet, w_br_mem, w_mem_kv, w_out,
           b_out, ln1_g, ln1_b, w_router, b_router, w_gu, b_gu, w_down, b_down, ln2_g, ln2_b):
    assert x.shape == (BATCH, SEQ, D) and w_in.shape[0] == 1
    row = lambda a: a.reshape(1, -1)
    x2 = x.reshape(T, D)
    w_in0, b_in0 = w_in[0], b_in[0]
    c_lr, c_fn, c_mq, c_gt = 3072, 3072 + 2 * GLA_LR, 3104 + FN_W, 3104 + FN_W + MQ_W
    w_main = jnp.concatenate([w_in0[:, :c_lr], w_in0[:, c_gt:]], axis=1).astype(BF16)
    b_main = row(jnp.concatenate([b_in0[:c_lr], b_in0[c_gt:]]))
    w_lr = _pad_cols(w_in0[:, c_lr:c_fn], LANES).astype(BF16)
    b_lr = _pad_cols(row(b_in0[c_lr:c_fn]), LANES)
    w_mq = w_in0[:, c_mq:c_gt].astype(BF16)
    b_mq = row(b_in0[c_mq:c_gt])
    w_fn = w_in0[:, c_fn:c_mq].astype(BF16)
    b_fn = row(b_in0[c_fn:c_mq])
    lng, lnb = row(ln_in_g), row(ln_in_b)

    proj, mq, lr = _inproj(x2, lng, lnb, w_main, b_main, w_mq, b_mq, w_lr, b_lr)

    zpad = jnp.zeros((LANES - 2 * GLA_LR, GLA_H * GLA_DK), F32)
    zlr = jnp.zeros((GLA_LR, GLA_H * GLA_DK), F32)
    wdf = jnp.concatenate([w_decay_f[0], zlr, zpad], axis=0).reshape(LANES, GLA_H, GLA_DK)
    wdb = jnp.concatenate([zlr, w_decay_b[0], zpad], axis=0).reshape(LANES, GLA_H, GLA_DK)
    wd = jnp.concatenate([wdf, wdb], axis=2).reshape(LANES, GLA_H * 2 * GLA_DK).astype(BF16)
    wd = jnp.concatenate([wd, wd], axis=0)
    bd = jnp.concatenate([b_decay_f[0].reshape(GLA_H, GLA_DK),
                          b_decay_b[0].reshape(GLA_H, GLA_DK)], axis=1).reshape(1, -1)
    og = _gla(proj, lr, wd, bd, row(gla_norm_g[0]))

    fbig, cwt, swt, f2, ccs, perm = _dft_tables(MERGE_TM)
    x4 = x.reshape(BATCH, FFT_N2, FFT_N1, D)
    z = _fft2(_fft1(x4, lng, lnb, w_fn, b_fn, fbig, cwt, swt), f2)

    kv = _memkv(mem.reshape(BATCH * MEM_LEN, D), row(ln_mem_g), row(ln_mem_b),
                w_mem_kv[0].astype(BF16))

    w_r = _pad_cols(w_router[0], LANES)
    wr_hi = w_r.astype(BF16)
    wr_lo = (w_r - wr_hi.astype(F32)).astype(BF16)
    b_r = jnp.concatenate([row(b_router[0]),
                           jnp.full((1, LANES - N_EXP), NEG_BIG, F32)], axis=1)
    h1, h1p, eidx, topw, cnt = _merge(
        x2, og, z, mq, proj, kv, lng, lnb,
        w_br_gla[0].astype(BF16), ccs, perm, w_br_fnet[0].astype(BF16),
        w_br_mem[0].astype(BF16), w_out[0].astype(BF16), row(b_out[0]),
        row(ln1_g[0]), row(ln1_b[0]), jnp.concatenate([wr_hi, wr_lo], axis=1), wr_hi, b_r)

    dest = _plan(eidx, cnt)
    counts = cnt[0, :N_EXP].astype(I32)
    dest_k = dest[:, :TOP_K]
    idx4 = dest_k.reshape(SC_WORKERS, SCD_NCH, SC_CH, TOP_K).transpose(0, 1, 3, 2)
    xs = _sc_dispatch(h1p, idx4.reshape(SC_WORKERS, SCD_NCH * TOP_K, SC_CH))
    ys = _experts(_work_items(counts), xs, w_gu[0], b_gu[0], w_down[0], b_down[0])
    out = h1
    for grp in range(COMB_GROUPS):
        dest_g = dest_k[grp * COMB_TG:(grp + 1) * COMB_TG].T.reshape(SC_WORKERS, SC_NCH, SC_CH)
        g4 = _sc_gather(ys, dest_g).reshape(TOP_K, COMB_TG, D // 2)
        out = _combine_dense(g4, out, topw, row(ln2_g[0]), row(ln2_b[0]), grp)
    return out.reshape(BATCH, SEQ, D)
```

```python
import functools
import math

import numpy as np
import jax
import jax.numpy as jnp
from jax import lax
from jax.experimental import pallas as pl
from jax.experimental.pallas import tpu as pltpu
from jax.experimental.pallas import tpu_sc as plsc

F32 = jnp.float32
BF16 = jnp.bfloat16
I32 = jnp.int32
U32 = jnp.uint32

D = 1024
BATCH = 4
SEQ = 4096
T = BATCH * SEQ
GLA_H = 4
GLA_DK = 128
GLA_DV = 256
GLA_LR = 16
GLA_TAU = 16.0
GLA_C = 64
FN_G = 4
FN_GW = 128
FN_W = 512
MEM_LEN = 256
MEM_H = 4
MEM_HD = 128
MQ_W = 512
N_EXP = 32
TOP_K = 4
D_FF = 1024
SW_LIMIT = 7.0
SW_ALPHA = 1.702
LN_EPS = 1e-5
RMS_EPS = 1e-6
DN_ALPHA = 2.0 ** 0.25
A_ROWS = T * TOP_K

FFT_N1 = 128
FFT_N2 = 32

LANES = 128
NEG_BIG = -1e30
MIB = 1024 * 1024


def _ln(x, g, b):
    mu = jnp.mean(x, axis=-1, keepdims=True)
    xc = x - mu
    var = jnp.mean(xc * xc, axis=-1, keepdims=True)
    return xc * lax.rsqrt(var + LN_EPS) * g + b


def _dot(a, b):
    return jnp.dot(a, b, preferred_element_type=F32)


def _dot_nt(a, b):
    return lax.dot_general(a, b, (((1,), (1,)), ((), ())), preferred_element_type=F32)


def _dot_tn(a, b):
    return lax.dot_general(a, b, (((0,), (0,)), ((), ())), preferred_element_type=F32)


def _interleave(*stages):
    live = list(stages)
    while live:
        for st in list(live):
            try:
                next(st)
            except StopIteration:
                live.remove(st)


def _split_bf16(a):
    hi = a.astype(BF16)
    return hi, (a - hi.astype(F32)).astype(BF16)


INPROJ_TM = 1024
INPROJ_TN = 3072


PROJ_W = 6 * 1024


def _inproj_kernel(x_ref, g_ref, b_ref, w_ref, bias_ref, wmq_ref, bmq_ref, wlr_ref, blr_ref,
                   proj_ref, mq_ref, lr_ref, hb_ref):
    @pl.when(pl.program_id(1) == 0)
    def _():
        hb = _ln(x_ref[...], g_ref[...], b_ref[...]).astype(BF16)
        hb_ref[...] = hb
        lr_ref[...] = _dot(hb, wlr_ref[...]) + blr_ref[...]
        mq_ref[...] = (_dot(hb, wmq_ref[...]) + bmq_ref[...]).astype(BF16)

    proj_ref[...] = (_dot(hb_ref[...], w_ref[...]) + bias_ref[...]).astype(BF16)


def _inproj(x2, ln_g, ln_b, w_main, b_main, w_mq, b_mq, w_lr, b_lr):
    tm, tn = INPROJ_TM, INPROJ_TN
    nj = PROJ_W // tn
    row = lambda i, j: (i, 0)
    const = lambda i, j: (0, 0)
    outs = (
        jax.ShapeDtypeStruct((T, PROJ_W), BF16),
        jax.ShapeDtypeStruct((T, MQ_W), BF16),
        jax.ShapeDtypeStruct((T, LANES), F32),
    )
    return pl.pallas_call(
        _inproj_kernel,
        grid=(T // tm, nj),
        in_specs=[
            pl.BlockSpec((tm, D), row),
            pl.BlockSpec((1, D), const),
            pl.BlockSpec((1, D), const),
            pl.BlockSpec((D, tn), lambda i, j: (0, j)),
            pl.BlockSpec((1, tn), lambda i, j: (0, j)),
            pl.BlockSpec((D, MQ_W), const),
            pl.BlockSpec((1, MQ_W), const),
            pl.BlockSpec((D, LANES), const),
            pl.BlockSpec((1, LANES), const),
        ],
        out_specs=[
            pl.BlockSpec((tm, tn), lambda i, j: (i, j)),
            pl.BlockSpec((tm, MQ_W), row),
            pl.BlockSpec((tm, LANES), row),
        ],
        out_shape=outs,
        scratch_shapes=[pltpu.VMEM((tm, D), BF16)],
        compiler_params=pltpu.CompilerParams(
            dimension_semantics=("arbitrary", "arbitrary"),
            vmem_limit_bytes=48 * MIB),
        name="ln_inproj",
    )(x2, ln_g, ln_b, w_main, b_main, w_mq, b_mq, w_lr, b_lr)


GLA_BULK = 256
GLA_FIN = 512
GLA_NCH = SEQ // GLA_C
GLA_CPB = GLA_BULK // GLA_C
GLA_PIPE = 4


def _gla_kernel(q_ref, k_ref, v_ref, r_ref, lr_ref, wd_ref, bd_ref, g_ref, cs_ref, o_ref,
                acc_ref, qin_ref, kin_ref, kst_ref, dec_ref, u_ref, stf_ref, stb_ref):
    C = GLA_C
    G = GLA_BULK
    DK = GLA_DK
    NG = SEQ // G
    scale = DK ** -0.5
    ii = lax.broadcasted_iota(I32, (G, G), 0)
    jj = lax.broadcasted_iota(I32, (G, G), 1)
    same = (ii // C) == (jj // C)
    lower = jnp.logical_and(same, ii >= jj)
    upper = jnp.logical_and(same, ii <= jj)
    is_fwd = lax.broadcasted_iota(I32, (G, 2 * DK), 1) < DK
    chunk_of_row = lax.broadcasted_iota(I32, (G, DK), 0) // C

    def stage_a(gi):
        rows = pl.ds(pl.multiple_of(gi * G, G), G)
        z = _dot(jnp.concatenate(_split_bf16(lr_ref[rows, :]), axis=1), wd_ref[...]) + bd_ref[...]
        yield
        la = -(jnp.maximum(-z, 0.0) + jnp.log(1.0 + jnp.exp(-jnp.abs(z)))) * (1.0 / GLA_TAU)
        la_hi, la_lo = _split_bf16(la)
        pre2 = _dot(cs_ref[...], jnp.concatenate([la_hi, la_lo], axis=1))
        yield
        pre = pre2[:, :2 * DK] + pre2[:, 2 * DK:]
        blast = jnp.concatenate(
            [jnp.broadcast_to(pre[ci * C + C - 1:ci * C + C, :], (C, 2 * DK))
             for ci in range(GLA_CPB)], axis=0)
        b = jnp.where(is_fwd, pre, blast - pre + la)
        qf32 = q_ref[rows, :].astype(F32)
        kf32 = k_ref[rows, :].astype(F32)
        q2 = jnp.concatenate([qf32, qf32], axis=1)
        k2 = jnp.concatenate([kf32, kf32], axis=1)
        qin_ref[rows, :] = (q2 * (scale * jnp.exp(b))).astype(BF16)
        kin_ref[rows, :] = (k2 * jnp.exp(-b)).astype(BF16)
        kst_ref[rows, :] = (k2 * jnp.exp(blast - b)).astype(BF16)
        dec = jnp.exp(blast)
        for ci in range(GLA_CPB):
            dec_ref[pl.ds(gi * GLA_CPB + ci, 1), :] = dec[ci * C:ci * C + 1, :]

    def stage_b(gi):
        rows = pl.ds(pl.multiple_of(gi * G, G), G)
        qi = qin_ref[rows, :]
        ki = kin_ref[rows, :]
        ks = kst_ref[rows, :]
        vb = v_ref[rows, :]
        att_f = _dot_nt(qi[:, :DK], ki[:, :DK])
        att_b = _dot_nt(qi[:, DK:], ki[:, DK:])
        yield
        att = jnp.where(lower, att_f, 0.0) + jnp.where(upper, att_b, 0.0)
        acc_ref[rows, :] = _dot(att.astype(BF16), vb)
        yield
        ksb = jnp.concatenate(
            [jnp.where(chunk_of_row == ci, ks[:, d * DK:(d + 1) * DK], jnp.zeros((G, DK), BF16))
             for d in range(2) for ci in range(GLA_CPB)], axis=1)
        u = _dot_tn(vb, ksb)
        for d in range(2):
            for ci in range(GLA_CPB):
                col = (d * GLA_CPB + ci) * DK
                u_ref[d, gi * GLA_CPB + ci] = u[:, col:col + DK]

    P = GLA_PIPE
    _interleave(*[stage_a(j) for j in range(P)])

    def bulk(i, carry):
        g = P * i
        _interleave(*[st for j in range(P) for st in (stage_b(g - P + j), stage_a(g + j))])
        return carry

    lax.fori_loop(1, NG // P, bulk, 0)
    _interleave(*[stage_b(NG - P + j) for j in range(P)])

    stf_ref[...] = jnp.zeros_like(stf_ref)
    stb_ref[...] = jnp.zeros_like(stb_ref)

    def one(n, d, st_ref):
        lanes = slice(d * DK, (d + 1) * DK)
        rows = pl.ds(pl.multiple_of(n * C, C), C)
        st = st_ref[...]
        acc_ref[rows, :] += _dot_nt(qin_ref[rows, lanes], st.astype(BF16))
        st_ref[...] = st * dec_ref[pl.ds(n, 1), :][:, lanes] + u_ref[d, n]

    def step(i, carry):
        one(i, 0, stf_ref)
        one(GLA_NCH - 1 - i, 1, stb_ref)
        return carry

    lax.fori_loop(0, GLA_NCH, step, 0, unroll=16)

    def fin(gi, carry):
        rows = pl.ds(pl.multiple_of(gi * GLA_FIN, GLA_FIN), GLA_FIN)
        o = acc_ref[rows, :]
        o = o * lax.rsqrt(jnp.mean(o * o, axis=-1, keepdims=True) + RMS_EPS) * g_ref[...]
        rg = r_ref[rows, :].astype(F32)
        o_ref[rows, :] = (o * (rg * jax.nn.sigmoid(rg))).astype(BF16)
        return carry

    lax.fori_loop(0, SEQ // GLA_FIN, fin, 0)


def _gla(proj, lr, wd, bd, g):
    i = np.arange(GLA_BULK)
    cs = ((i[:, None] // GLA_C) == (i[None, :] // GLA_C)) & (i[:, None] >= i[None, :])
    cs = jnp.asarray(cs, dtype=F32).astype(BF16)
    v_blk = 1024 // GLA_DV
    return pl.pallas_call(
        _gla_kernel,
        grid=(BATCH, GLA_H),
        in_specs=[
            pl.BlockSpec((SEQ, GLA_DK), lambda b, h: (b, h)),
            pl.BlockSpec((SEQ, GLA_DK), lambda b, h: (b, GLA_H + h)),
            pl.BlockSpec((SEQ, GLA_DV), lambda b, h: (b, v_blk + h)),
            pl.BlockSpec((SEQ, GLA_DV), lambda b, h: (b, 2 * v_blk + h)),
            pl.BlockSpec((SEQ, LANES), lambda b, h: (b, 0)),
            pl.BlockSpec((2 * LANES, 2 * GLA_DK), lambda b, h: (0, h)),
            pl.BlockSpec((1, 2 * GLA_DK), lambda b, h: (0, h)),
            pl.BlockSpec((1, GLA_DV), lambda b, h: (0, 0)),
            pl.BlockSpec((GLA_BULK, GLA_BULK), lambda b, h: (0, 0)),
        ],
        out_specs=pl.BlockSpec((SEQ, GLA_DV), lambda b, h: (b, h)),
        out_shape=jax.ShapeDtypeStruct((T, GLA_H * GLA_DV), BF16),
        scratch_shapes=[
            pltpu.VMEM((SEQ, GLA_DV), F32),
            pltpu.VMEM((SEQ, 2 * GLA_DK), BF16),
            pltpu.VMEM((SEQ, 2 * GLA_DK), BF16),
            pltpu.VMEM((SEQ, 2 * GLA_DK), BF16),
            pltpu.VMEM((GLA_NCH, 2 * GLA_DK), F32),
            pltpu.VMEM((2, GLA_NCH, GLA_DV, GLA_DK), F32),
            pltpu.VMEM((GLA_DV, GLA_DK), F32),
            pltpu.VMEM((GLA_DV, GLA_DK), F32),
        ],
        compiler_params=pltpu.CompilerParams(
            dimension_semantics=("arbitrary", "arbitrary"),
            vmem_limit_bytes=58 * MIB),
        name="gla",
    )(proj, proj, proj, proj, lr, wd, bd, g, cs)


FFT1_S = 16
FFT1_SUB = 2
FFT1_ROWS = FFT_N2 * FFT1_S
FFT2_KB = 8


def _fft1_kernel(x_ref, g_ref, b_ref, w_ref, bias_ref, fbig_ref, cw_ref, sw_ref, o_ref):
    sh = FFT1_S // FFT1_SUB
    rows = FFT_N2 * sh
    res = {}

    def sub(h):
        xv = x_ref[:, h * sh:(h + 1) * sh, :].reshape(rows, D)
        hb = _ln(xv, g_ref[...], b_ref[...]).astype(BF16)
        fn = (_dot(hb, w_ref[...]) + bias_ref[...]).astype(BF16)
        yield
        a = _dot(fbig_ref[...], fn)
        yield
        ar = a[:rows]
        ai = a[rows:]
        cw = jnp.concatenate([cw_ref[h]] * (FN_W // LANES), axis=1)
        sw = jnp.concatenate([sw_ref[h]] * (FN_W // LANES), axis=1)
        res[h] = ((ar * cw + ai * sw).reshape(FFT_N2, sh, FN_W),
                  (ai * cw - ar * sw).reshape(FFT_N2, sh, FN_W))

    _interleave(*[sub(h) for h in range(FFT1_SUB)])
    for ri in range(2):
        o_ref[ri] = jnp.concatenate([res[h][ri] for h in range(FFT1_SUB)], axis=1).astype(BF16)


def _fft1(x4, ln_g, ln_b, w_fn, b_fn, fbig, cwt, swt):
    s = FFT1_S
    const = lambda b, j: (0, 0)
    return pl.pallas_call(
        _fft1_kernel,
        grid=(BATCH, FFT_N1 // s),
        in_specs=[
            pl.BlockSpec((None, FFT_N2, s, D), lambda b, j: (b, 0, j, 0)),
            pl.BlockSpec((1, D), const),
            pl.BlockSpec((1, D), const),
            pl.BlockSpec((D, FN_W), const),
            pl.BlockSpec((1, FN_W), const),
            pl.BlockSpec((2 * FFT1_ROWS // FFT1_SUB, FFT1_ROWS // FFT1_SUB), const),
            pl.BlockSpec((FFT1_SUB, FFT1_ROWS // FFT1_SUB, LANES), lambda b, j: (j, 0, 0)),
            pl.BlockSpec((FFT1_SUB, FFT1_ROWS // FFT1_SUB, LANES), lambda b, j: (j, 0, 0)),
        ],
        out_specs=pl.BlockSpec((None, 2, FFT_N2, s, FN_W), lambda b, j: (b, 0, 0, j, 0)),
        out_shape=jax.ShapeDtypeStruct((BATCH, 2, FFT_N2, FFT_N1, FN_W), BF16),
        compiler_params=pltpu.CompilerParams(
            dimension_semantics=("arbitrary", "arbitrary"),
            vmem_limit_bytes=40 * MIB),
        name="fft_stage1",
    )(x4, ln_g, ln_b, w_fn, b_fn, fbig, cwt, swt)


def _fft2_kernel(d_ref, f2_ref, o_ref):
    f2 = f2_ref[...]
    for kk in range(FFT2_KB):
        z = _dot(f2, jnp.concatenate([d_ref[0, kk], d_ref[1, kk]], axis=0))
        o_ref[0, kk] = z[:FFT_N1].astype(BF16)
        o_ref[1, kk] = z[FFT_N1:].astype(BF16)


def _fft2(dmat, f2):
    kb = FFT2_KB
    blk = (None, 2, kb, FFT_N1, FN_W)
    return pl.pallas_call(
        _fft2_kernel,
        grid=(BATCH, FFT_N2 // kb),
        in_specs=[
            pl.BlockSpec(blk, lambda b, j: (b, 0, j, 0, 0)),
            pl.BlockSpec((2 * FFT_N1, 2 * FFT_N1), lambda b, j: (0, 0)),
        ],
        out_specs=pl.BlockSpec(blk, lambda b, j: (b, 0, j, 0, 0)),
        out_shape=jax.ShapeDtypeStruct((BATCH, 2, FFT_N2, FFT_N1, FN_W), BF16),
        compiler_params=pltpu.CompilerParams(
            dimension_semantics=("arbitrary", "arbitrary")),
        name="fft_stage2",
    )(dmat, f2)


def _dft_tables(merge_tm):
    s = FFT1_S // FFT1_SUB
    n2 = np.arange(FFT_N2, dtype=np.float64)
    n1 = np.arange(FFT_N1, dtype=np.float64)
    th = 2.0 * np.pi * np.outer(n2, n2) / FFT_N2
    f1 = np.stack([np.cos(th), -np.sin(th)]) / math.sqrt(SEQ)
    fbig = np.einsum("rkn,st->rksnt", f1, np.eye(s)).reshape(2 * FFT_N2 * s, FFT_N2 * s)
    tw = 2.0 * np.pi * np.outer(n2, n1) / SEQ
    tw = tw.reshape(FFT_N2, FFT_N1 // s, s).transpose(1, 0, 2).reshape(FFT_N1 // s, FFT_N2 * s)
    cwt = np.broadcast_to(np.cos(tw)[:, :, None], tw.shape + (LANES,))
    swt = np.broadcast_to(np.sin(tw)[:, :, None], tw.shape + (LANES,))
    th1 = 2.0 * np.pi * np.outer(n1, n1) / FFT_N1
    c1, s1 = np.cos(th1), np.sin(th1)
    f2 = np.block([[c1, s1], [-s1, c1]])
    cc = np.arange(FN_GW, dtype=np.float64)
    thc = 2.0 * np.pi * np.outer(cc, cc) / FN_GW
    ccs = np.concatenate([np.cos(thc), np.sin(thc)], axis=0) / math.sqrt(FN_GW)
    k1n = merge_tm // FFT_N2
    r = np.arange(merge_tm)
    perm = np.zeros((merge_tm, merge_tm))
    perm[r, (r % FFT_N2) * k1n + r // FFT_N2] = 1.0
    as32 = lambda a: jnp.asarray(np.ascontiguousarray(a), dtype=F32)
    return (as32(fbig).astype(BF16), as32(cwt), as32(swt), as32(f2).astype(BF16),
            as32(ccs).astype(BF16), as32(perm).astype(BF16))


def _memkv_kernel(m_ref, g_ref, b_ref, w_ref, o_ref):
    mn = _ln(m_ref[...], g_ref[...], b_ref[...]).astype(BF16)
    o_ref[...] = _dot(mn, w_ref[...]).astype(BF16)


def _memkv(mem2, g, b, w):
    return pl.pallas_call(
        _memkv_kernel,
        grid=(BATCH,),
        in_specs=[
            pl.BlockSpec((MEM_LEN, D), lambda i: (i, 0)),
            pl.BlockSpec((1, D), lambda i: (0, 0)),
            pl.BlockSpec((1, D), lambda i: (0, 0)),
            pl.BlockSpec((D, 2 * MQ_W), lambda i: (0, 0)),
        ],
        out_specs=pl.BlockSpec((MEM_LEN, 2 * MQ_W), lambda i: (i, 0)),
        out_shape=jax.ShapeDtypeStruct((BATCH * MEM_LEN, 2 * MQ_W), BF16),
        compiler_params=pltpu.CompilerParams(dimension_semantics=("arbitrary",)),
        name="mem_kv",
    )(mem2, g, b, w)


MERGE_TM = 512
MERGE_K1 = MERGE_TM // FFT_N2


def _pack_bf16_pair(v):
    n = v.shape[1] // 2
    bits = lax.bitcast_convert_type(v.astype(BF16).astype(F32), U32)
    return (bits[:, n:] & jnp.uint32(0xFFFF0000)) | (bits[:, :n] >> 16)


def _unpack_bf16_pair(p):
    lo = lax.bitcast_convert_type(p << 16, F32)
    hi = lax.bitcast_convert_type(p & jnp.uint32(0xFFFF0000), F32)
    return lo, hi


def _merge_kernel(x_ref, og_ref, zr_ref, zi_ref, mq_ref, gt_ref, kv_ref,
                  lng_ref, lnb_ref, wg_ref, ccs_ref, perm_ref, wf_ref, wm_ref, wo_ref, bo_ref,
                  l1g_ref, l1b_ref, wr2_ref, wrh_ref, br_ref,
                  h1_ref, h1p_ref, eidx_ref, topw_ref, cnt_ref, mg_ref):
    tm = MERGE_TM
    i = pl.program_id(0)
    y = {}

    @pl.when(i == 0)
    def _():
        mg_ref[1] = jnp.zeros((tm, D), BF16)
        cnt_ref[...] = jnp.zeros_like(cnt_ref)

    def branch_fnet():
        zr = zr_ref[...].reshape(tm, FN_W)
        zi = zi_ref[...].reshape(tm, FN_W)
        ys = []
        for g in range(FN_G):
            sl = slice(g * FN_GW, (g + 1) * FN_GW)
            ys.append(_dot(jnp.concatenate([zr[:, sl], zi[:, sl]], axis=1), ccs_ref[...]))
        yield
        yp = _dot(perm_ref[...], jnp.concatenate(ys, axis=1).astype(BF16))
        yield
        y["fnet"] = _dot(yp.astype(BF16), wf_ref[...])

    def branch_mem():
        heads = [slice(hd * MEM_HD, (hd + 1) * MEM_HD) for hd in range(MEM_H)]
        ss = [_dot_nt(mq_ref[:, sl], kv_ref[:, sl]) for sl in heads]
        yield
        oms = []
        for hd, s in enumerate(ss):
            s = s * (MEM_HD ** -0.5)
            s = s - jnp.max(s, axis=-1, keepdims=True)
            p = jnp.exp(s)
            p = p * (1.0 / jnp.sum(p, axis=-1, keepdims=True))
            oms.append(_dot(p.astype(BF16),
                            kv_ref[:, MQ_W + hd * MEM_HD:MQ_W + (hd + 1) * MEM_HD]))
        yield
        y["mem"] = _dot(jnp.concatenate(oms, axis=1).astype(BF16), wm_ref[...])

    def branch_gla():
        y["gla"] = _dot(og_ref[...], wg_ref[...])
        yield

    def tail():
        mix = _dot(mg_ref[1 - lax.rem(i, 2)], wo_ref[...]) + bo_ref[...]
        yield
        h = _ln(x_ref[...], lng_ref[...], lnb_ref[...])
        h1 = _ln(DN_ALPHA * h + mix, l1g_ref[...], l1b_ref[...])
        h1_ref[...] = h1
        h1p_ref[...] = _pack_bf16_pair(h1)
        h_hi, h_lo = _split_bf16(h1)
        d2 = _dot(h_hi, wr2_ref[...])
        d1 = _dot(h_lo, wrh_ref[...])
        yield
        l = d2[:, :LANES] + d2[:, LANES:] + d1 + br_ref[...]
        lane = lax.broadcasted_iota(I32, (tm, LANES), 1)
        vals, idxs = [], []
        for _ in range(TOP_K):
            m = jnp.max(l, axis=-1, keepdims=True)
            idx = jnp.min(jnp.where(l == m, lane, LANES), axis=-1, keepdims=True)
            vals.append(m)
            idxs.append(idx)
            l = jnp.where(lane == idx, -jnp.inf, l)
        es = [jnp.exp(v - vals[0]) for v in vals]
        den = es[0] + es[1] + es[2] + es[3]
        eo = jnp.zeros((tm, LANES), I32)
        wo = jnp.zeros((tm, LANES), F32)
        chosen = jnp.zeros((tm, LANES), F32)
        for k in range(TOP_K):
            eo = jnp.where(lane == k, idxs[k], eo)
            wo = jnp.where(lane == k, es[k] / den, wo)
            chosen = chosen + jnp.where(lane == idxs[k], 1.0, 0.0)
        eidx_ref[...] = eo
        topw_ref[...] = wo
        real = jnp.where(i > 0, 1.0, 0.0)
        cnt_ref[...] += real * jnp.broadcast_to(jnp.sum(chosen, axis=0, keepdims=True),
                                               cnt_ref.shape)

    _interleave(branch_fnet(), branch_mem(), branch_gla(), tail())

    def gate(c):
        return 0.5 + 0.5 * jnp.tanh(0.5 * gt_ref[:, c * D:(c + 1) * D].astype(F32))

    merged = gate(0) * y["gla"] + gate(1) * y["fnet"] + gate(2) * y["mem"]
    mg_ref[lax.rem(i, 2)] = merged.astype(BF16)


def _merge(x2, og, z, mq, gates, kv, lng, lnb, wg, ccs, perm, wf, wm, wo, bo, l1g, l1b,
           wr2, wrh, br):
    tm = MERGE_TM
    per_b = SEQ // tm
    nt = T // tm
    cur = lambda i: jnp.minimum(i, nt - 1)
    prv = lambda i: jnp.maximum(i - 1, 0)
    row = lambda i: (cur(i), 0)
    prow = lambda i: (prv(i), 0)
    const = lambda i: (0, 0)
    zblk = (None, None, FFT_N2, MERGE_K1, FN_W)
    outs = (
        jax.ShapeDtypeStruct((T, D), F32),
        jax.ShapeDtypeStruct((T, D // 2), U32),
        jax.ShapeDtypeStruct((T, LANES), I32),
        jax.ShapeDtypeStruct((T, LANES), F32),
        jax.ShapeDtypeStruct((8, LANES), F32),
    )
    return pl.pallas_call(
        _merge_kernel,
        grid=(nt + 1,),
        in_specs=[
            pl.BlockSpec((tm, D), prow),
            pl.BlockSpec((tm, D), row),
            pl.BlockSpec(zblk, lambda i: (cur(i) // per_b, 0, 0, cur(i) % per_b, 0)),
            pl.BlockSpec(zblk, lambda i: (cur(i) // per_b, 1, 0, cur(i) % per_b, 0)),
            pl.BlockSpec((tm, MQ_W), row),
            pl.BlockSpec((tm, 3 * D), lambda i: (cur(i), 1)),
            pl.BlockSpec((MEM_LEN, 2 * MQ_W), lambda i: (cur(i) // per_b, 0)),
            pl.BlockSpec((1, D), const), pl.BlockSpec((1, D), const),
            pl.BlockSpec((D, D), const),
            pl.BlockSpec((2 * FN_GW, FN_GW), const),
            pl.BlockSpec((tm, tm), const),
            pl.BlockSpec((FN_W, D), const),
            pl.BlockSpec((MQ_W, D), const),
            pl.BlockSpec((D, D), const),
            pl.BlockSpec((1, D), const),
            pl.BlockSpec((1, D), const), pl.BlockSpec((1, D), const),
            pl.BlockSpec((D, 2 * LANES), const),
            pl.BlockSpec((D, LANES), const),
            pl.BlockSpec((1, LANES), const),
        ],
        out_specs=[
            pl.BlockSpec((tm, D), prow),
            pl.BlockSpec((tm, D // 2), prow),
            pl.BlockSpec((tm, LANES), prow),
            pl.BlockSpec((tm, LANES), prow),
            pl.BlockSpec((8, LANES), const),
        ],
        out_shape=outs,
        scratch_shapes=[pltpu.VMEM((2, tm, D), BF16)],
        compiler_params=pltpu.CompilerParams(
            dimension_semantics=("arbitrary",),
            vmem_limit_bytes=58 * MIB),
        name="merge_ln1_router",
    )(x2, og, z, z, mq, gates, kv, lng, lnb, wg, ccs, perm, wf, wm, wo, bo, l1g, l1b,
      wr2, wrh, br)


PLAN_TP = 1024


def _expert_onehots(e, lane):
    onehots = [lane == e[:, k:k + 1] for k in range(TOP_K)]
    mf = jnp.zeros(lane.shape, F32)
    for oh in onehots:
        mf = mf + jnp.where(oh, 1.0, 0.0)
    return onehots, mf


def _plan_kernel(e_ref, tot_ref, dest_ref, cnt_ref, off_ref):
    i = pl.program_id(0)
    tp = PLAN_TP
    lane = lax.broadcasted_iota(I32, (tp, LANES), 1)
    onehots, mf = _expert_onehots(e_ref[...], lane)

    @pl.when(i == 0)
    def _():
        tot = tot_ref[0:1, :]
        padded = jnp.floor((tot + (MOE_BM - 1)) * (1.0 / MOE_BM)) * MOE_BM
        lane1 = lax.broadcasted_iota(I32, (1, LANES), 1)
        inc = padded
        for s in (1, 2, 4, 8, 16, 32, 64):
            inc = inc + jnp.where(lane1 >= s, pltpu.roll(inc, s, 1), 0.0)
        off_ref[...] = inc - padded
        cnt_ref[...] = jnp.zeros_like(cnt_ref)

    ri = lax.broadcasted_iota(I32, (tp, tp), 0)
    ci = lax.broadcasted_iota(I32, (tp, tp), 1)
    ltri = jnp.where(ri > ci, 1.0, 0.0).astype(BF16)
    rank = _dot(ltri, mf.astype(BF16)) + cnt_ref[...] + off_ref[...]
    out = jnp.zeros((tp, LANES), I32)
    for k in range(TOP_K):
        dk = jnp.sum(jnp.where(onehots[k], rank, 0.0), axis=-1, keepdims=True)
        out = jnp.where(lane == k, dk.astype(I32), out)
    dest_ref[...] = out
    cnt_ref[...] += jnp.sum(mf, axis=0, keepdims=True)


def _plan(eidx, cnt):
    tp = PLAN_TP
    return pl.pallas_call(
        _plan_kernel,
        grid=(T // tp,),
        in_specs=[pl.BlockSpec((tp, LANES), lambda i: (i, 0)),
                  pl.BlockSpec((8, LANES), lambda i: (0, 0))],
        out_specs=pl.BlockSpec((tp, LANES), lambda i: (i, 0)),
        out_shape=jax.ShapeDtypeStruct((T, LANES), I32),
        scratch_shapes=[pltpu.VMEM((1, LANES), F32), pltpu.VMEM((1, LANES), F32)],
        compiler_params=pltpu.CompilerParams(dimension_semantics=("arbitrary",)),
        name="route_plan",
    )(eidx, cnt)


MOE_BM = 512
MOE_NW = A_ROWS // MOE_BM + N_EXP
XS_ROWS = MOE_NW * MOE_BM
MOE_FF_SLICES = 2


def _expert_kernel(we_ref, wb_ref, wv_ref, wfe_ref, wsl_ref, wnx_ref,
                   x_ref, wgu_hbm, bgu_ref, wdn_hbm, bdn_ref, o_ref,
                   wgu_f32, wdn_f32, wgu_bf, wdn_bf, sems):
    w = pl.program_id(0)
    e = we_ref[w]

    def weight_copies(expert, slot):
        return (pltpu.make_async_copy(wgu_hbm.at[expert], wgu_f32.at[slot], sems.at[slot, 0]),
                pltpu.make_async_copy(wdn_hbm.at[expert], wdn_f32.at[slot], sems.at[slot, 1]))

    @pl.when(w == 0)
    def _():
        for cp_ in weight_copies(e, 0):
            cp_.start()

    first = wfe_ref[w] == 1
    slot = wsl_ref[w]

    @pl.when(first)
    def _():
        for cp_ in weight_copies(e, slot):
            cp_.wait()
        nxt = wnx_ref[w]

        @pl.when(nxt >= 0)
        def _():
            for cp_ in weight_copies(nxt, 1 - slot):
                cp_.start()

    hw = D_FF // MOE_FF_SLICES
    col_slices = [(slice(hf * hw, (hf + 1) * hw), slice(D_FF + hf * hw, D_FF + (hf + 1) * hw))
                  for hf in range(MOE_FF_SLICES)]

    def cast_weights():
        for gc, uc in col_slices:
            wgu_bf[:, gc] = wgu_f32[slot, :, gc].astype(BF16)
            wgu_bf[:, uc] = wgu_f32[slot, :, uc].astype(BF16)
            yield
        for hf in range(MOE_FF_SLICES):
            rs = slice(hf * hw, (hf + 1) * hw)
            wdn_bf[rs, :] = wdn_f32[slot, rs, :].astype(BF16)
            yield

    def ffn(rows):
        xlo, xhi = _unpack_bf16_pair(x_ref[rows, :])
        xb = jnp.concatenate([xlo.astype(BF16), xhi.astype(BF16)], axis=1)
        bgu = bgu_ref[pl.ds(e, 1), :]
        gus = []
        for gc, uc in col_slices:
            gus.append((_dot(xb, wgu_bf[:, gc]) + bgu[:, gc], _dot(xb, wgu_bf[:, uc]) + bgu[:, uc]))
            yield
        out = bdn_ref[pl.ds(e, 1), :]
        for hf, (g, u) in enumerate(gus):
            gate = jnp.minimum(g, SW_LIMIT)
            up = jnp.clip(u, -SW_LIMIT, SW_LIMIT)
            act = (up + 1.0) * (gate * jax.nn.sigmoid(SW_ALPHA * gate))
            out = out + _dot(act.astype(BF16), wdn_bf[hf * hw:(hf + 1) * hw, :])
            if hf + 1 < MOE_FF_SLICES:
                yield
        o_ref[rows, :] = _pack_bf16_pair(out)

    for code, rows in ((1, slice(None)), (2, slice(0, MOE_BM // 2))):
        @pl.when(jnp.logical_and(wv_ref[w] == code, first))
        def _(rows=rows):
            _interleave(cast_weights(), ffn(rows))

        @pl.when(jnp.logical_and(wv_ref[w] == code, jnp.logical_not(first)))
        def _(rows=rows):
            _interleave(ffn(rows))

    @pl.when(wv_ref[w] == 2)
    def _():
        o_ref[MOE_BM // 2:, :] = jnp.zeros((MOE_BM // 2, D // 2), U32)

    @pl.when(wv_ref[w] == 0)
    def _():
        o_ref[...] = jnp.zeros_like(o_ref)


def _experts(meta, xs, w_gu, b_gu, w_down, b_down):
    return pl.pallas_call(
        _expert_kernel,
        grid_spec=pltpu.PrefetchScalarGridSpec(
            num_scalar_prefetch=len(meta),
            grid=(MOE_NW,),
            in_specs=[
                pl.BlockSpec((MOE_BM, D // 2), lambda w, we, wb, *_: (wb[w], 0)),
                pl.BlockSpec(memory_space=pl.ANY),
                pl.BlockSpec((N_EXP, 2 * D_FF), lambda w, *_: (0, 0)),
                pl.BlockSpec(memory_space=pl.ANY),
                pl.BlockSpec((N_EXP, D), lambda w, *_: (0, 0)),
            ],
            out_specs=pl.BlockSpec((MOE_BM, D // 2), lambda w, *_: (w, 0)),
            scratch_shapes=[
                pltpu.VMEM((2, D, 2 * D_FF), F32),
                pltpu.VMEM((2, D_FF, D), F32),
                pltpu.VMEM((D, 2 * D_FF), BF16),
                pltpu.VMEM((D_FF, D), BF16),
                pltpu.SemaphoreType.DMA((2, 2)),
            ],
        ),
        out_shape=jax.ShapeDtypeStruct((XS_ROWS, D // 2), U32),
        compiler_params=pltpu.CompilerParams(
            dimension_semantics=("arbitrary",),
            vmem_limit_bytes=56 * MIB),
        name="moe_experts",
    )(*meta, xs, w_gu, b_gu, w_down, b_down)


def _work_items(counts):
    n_e = (counts + MOE_BM - 1) // MOE_BM
    item_end = jnp.cumsum(n_e)
    total = item_end[-1]
    w = jnp.arange(MOE_NW, dtype=I32)
    valid = w < total
    wc = jnp.minimum(w, total - 1)
    e_w = jnp.sum((item_end[None, :] <= wc[:, None]).astype(I32), axis=1)
    e_w = jnp.minimum(e_w, N_EXP - 1)
    rows_here = counts[e_w] - (wc - (item_end - n_e)[e_w]) * MOE_BM
    valid = jnp.where(valid, jnp.where(rows_here <= MOE_BM // 2, 2, 1), 0)
    prev_e = jnp.concatenate([jnp.full((1,), -1, I32), e_w[:-1]])
    fe = (e_w != prev_e).astype(I32)
    slot = (jnp.cumsum(fe) - 1) % 2
    first_at = jnp.where(fe == 1, w, MOE_NW)
    next_first = jnp.concatenate([lax.cummin(first_at, reverse=True)[1:],
                                  jnp.full((1,), MOE_NW, I32)])
    nxt = jnp.where(next_first < MOE_NW, e_w[jnp.minimum(next_first, MOE_NW - 1)], -1)
    return tuple(a.astype(I32) for a in (e_w, wc, valid, fe, slot, nxt))


COMB_TM = 512
SC_CORES = 2
SC_SUBCORES = 16
SC_WORKERS = SC_CORES * SC_SUBCORES
SC_CH = 64
COMB_GROUPS = 4
COMB_TG = T // COMB_GROUPS
SC_ROWS_PER_W = COMB_TG * TOP_K // SC_WORKERS
SC_NCH = SC_ROWS_PER_W // SC_CH


def _sc_gather(table, idx3):
    mesh = plsc.VectorSubcoreMesh(core_axis_name="c", subcore_axis_name="s")

    @functools.partial(
        pl.kernel, mesh=mesh,
        out_type=jax.ShapeDtypeStruct((COMB_TG * TOP_K, D // 2), U32),
        scratch_types=[
            pltpu.VMEM((SC_NCH, SC_CH), I32),
            pltpu.VMEM((2, SC_CH, D // 2), U32),
            pltpu.SemaphoreType.DMA((2,)),
            pltpu.SemaphoreType.DMA((2,)),
        ],
    )
    def k(table_hbm, idx_hbm, out_hbm, idx_v, rows_v, gsem, psem):
        wid = lax.axis_index("s") * SC_CORES + lax.axis_index("c")
        base = wid * SC_ROWS_PER_W
        pltpu.sync_copy(idx_hbm.at[wid], idx_v)

        def gather(j, b):
            return pltpu.make_async_copy(table_hbm.at[idx_v.at[j]], rows_v.at[b], gsem.at[b])

        def put(j, b):
            return pltpu.make_async_copy(rows_v.at[b], out_hbm.at[pl.ds(base + j * SC_CH, SC_CH)],
                                         psem.at[b])

        gather(0, 0).start()

        @pl.loop(0, SC_NCH, step=2)
        def _(j0):
            for b in range(2):
                j = j0 + b

                @pl.when(j + 1 < SC_NCH)
                def _():
                    @pl.when(j >= 1)
                    def _():
                        put(j - 1, 1 - b).wait()
                    gather(j + 1, 1 - b).start()

                gather(j, b).wait()
                put(j, b).start()

        put(SC_NCH - 2, 0).wait()
        put(SC_NCH - 1, 1).wait()

    return k(table, idx3)


SCD_TOK_PER_W = T // SC_WORKERS
SCD_NCH = SCD_TOK_PER_W // SC_CH


def _sc_dispatch(h1p, idx4):
    mesh = plsc.VectorSubcoreMesh(core_axis_name="c", subcore_axis_name="s")

    @functools.partial(
        pl.kernel, mesh=mesh,
        out_type=jax.ShapeDtypeStruct((XS_ROWS, D // 2), U32),
        scratch_types=[
            pltpu.VMEM((SCD_NCH * TOP_K, SC_CH), I32),
            pltpu.VMEM((2, SC_CH, D // 2), U32),
            pltpu.SemaphoreType.DMA((2,)),
            pltpu.SemaphoreType.DMA((2,)),
        ],
    )
    def k(h_hbm, idx_hbm, xs_hbm, idx_v, rows_v, gsem, psem):
        wid = lax.axis_index("s") * SC_CORES + lax.axis_index("c")
        base = wid * SCD_TOK_PER_W
        pltpu.sync_copy(idx_hbm.at[wid], idx_v)

        def get(c, b):
            return pltpu.make_async_copy(h_hbm.at[pl.ds(base + c * SC_CH, SC_CH)], rows_v.at[b],
                                         gsem.at[b])

        def puts(c, b):
            return [pltpu.make_async_copy(rows_v.at[b], xs_hbm.at[idx_v.at[c * TOP_K + kk]],
                                          psem.at[b]) for kk in range(TOP_K)]

        get(0, 0).start()

        @pl.loop(0, SCD_NCH, step=2)
        def _(c0):
            for b in range(2):
                c = c0 + b

                @pl.when(c + 1 < SCD_NCH)
                def _():
                    @pl.when(c >= 1)
                    def _():
                        for cp_ in puts(c - 1, 1 - b):
                            cp_.wait()
                    get(c + 1, 1 - b).start()

                get(c, b).wait()
                for cp_ in puts(c, b):
                    cp_.start()

        for cp_ in puts(SCD_NCH - 2, 0) + puts(SCD_NCH - 1, 1):
            cp_.wait()

    return k(h1p, idx4)


def _combine_dense_kernel(g_ref, h1_ref, tw_ref, lg_ref, lb_ref, o_ref):
    tw = tw_ref[...]
    ylo = jnp.zeros((COMB_TM, D // 2), F32)
    yhi = jnp.zeros((COMB_TM, D // 2), F32)
    for k in range(TOP_K):
        lo, hi = _unpack_bf16_pair(g_ref[k])
        wk = tw[:, k:k + 1]
        ylo = ylo + lo * wk
        yhi = yhi + hi * wk
    ff = jnp.concatenate([ylo, yhi], axis=1)
    o_ref[...] = _ln(DN_ALPHA * h1_ref[...] + ff, lg_ref[...], lb_ref[...])


def _combine_dense(g4, h1, topw, g, b, group):
    tm = COMB_TM
    t0 = group * (COMB_TG // tm)
    return pl.pallas_call(
        _combine_dense_kernel,
        grid=(COMB_TG // tm,),
        in_specs=[
            pl.BlockSpec((TOP_K, tm, D // 2), lambda i: (0, i, 0)),
            pl.BlockSpec((tm, D), lambda i: (t0 + i, 0)),
            pl.BlockSpec((tm, LANES), lambda i: (t0 + i, 0)),
            pl.BlockSpec((1, D), lambda i: (0, 0)),
            pl.BlockSpec((1, D), lambda i: (0, 0)),
        ],
        out_specs=pl.BlockSpec((tm, D), lambda i: (t0 + i, 0)),
        out_shape=jax.ShapeDtypeStruct((T, D), F32),
        input_output_aliases={1: 0},
        compiler_params=pltpu.CompilerParams(
            dimension_semantics=("arbitrary",),
            vmem_limit_bytes=40 * MIB),
        name="moe_combine_dense_ln2",
    )(g4, h1, topw, g, b)


def _pad_cols(a, n):
    return jnp.pad(a, ((0, 0), (0, n - a.shape[1])))


def kernel(x, mem, ln_in_g, ln_in_b, ln_mem_g, ln_mem_b, w_in, b_in, w_decay_f, b_decay_f,
           w_decay_b, b_decay_b, gla_norm_g, w_br_gla, w_br_fnet, w_br_mem, w_mem_kv, w_out,
           b_out, ln1_g, ln1_b, w_router, b_router, w_gu, b_gu, w_down, b_down, ln2_g, ln2_b):
    assert x.shape == (BATCH, SEQ, D) and w_in.shape[0] == 1
    row = lambda a: a.reshape(1, -1)
    x2 = x.reshape(T, D)
    w_in0, b_in0 = w_in[0], b_in[0]
    c_lr, c_fn, c_mq, c_gt = 3072, 3072 + 2 * GLA_LR, 3104 + FN_W, 3104 + FN_W + MQ_W
    w_main = jnp.concatenate([w_in0[:, :c_lr], w_in0[:, c_gt:]], axis=1).astype(BF16)
    b_main = row(jnp.concatenate([b_in0[:c_lr], b_in0[c_gt:]]))
    w_lr = _pad_cols(w_in0[:, c_lr:c_fn], LANES).astype(BF16)
    b_lr = _pad_cols(row(b_in0[c_lr:c_fn]), LANES)
    w_mq = w_in0[:, c_mq:c_gt].astype(BF16)
    b_mq = row(b_in0[c_mq:c_gt])
    w_fn = w_in0[:, c_fn:c_mq].astype(BF16)
    b_fn = row(b_in0[c_fn:c_mq])
    lng, lnb = row(ln_in_g), row(ln_in_b)

    proj, mq, lr = _inproj(x2, lng, lnb, w_main, b_main, w_mq, b_mq, w_lr, b_lr)

    zpad = jnp.zeros((LANES - 2 * GLA_LR, GLA_H * GLA_DK), F32)
    zlr = jnp.zeros((GLA_LR, GLA_H * GLA_DK), F32)
    wdf = jnp.concatenate([w_decay_f[0], zlr, zpad], axis=0).reshape(LANES, GLA_H, GLA_DK)
    wdb = jnp.concatenate([zlr, w_decay_b[0], zpad], axis=0).reshape(LANES, GLA_H, GLA_DK)
    wd = jnp.concatenate([wdf, wdb], axis=2).reshape(LANES, GLA_H * 2 * GLA_DK).astype(BF16)
    wd = jnp.concatenate([wd, wd], axis=0)
    bd = jnp.concatenate([b_decay_f[0].reshape(GLA_H, GLA_DK),
                          b_decay_b[0].reshape(GLA_H, GLA_DK)], axis=1).reshape(1, -1)
    og = _gla(proj, lr, wd, bd, row(gla_norm_g[0]))

    fbig, cwt, swt, f2, ccs, perm = _dft_tables(MERGE_TM)
    x4 = x.reshape(BATCH, FFT_N2, FFT_N1, D)
    z = _fft2(_fft1(x4, lng, lnb, w_fn, b_fn, fbig, cwt, swt), f2)

    kv = _memkv(mem.reshape(BATCH * MEM_LEN, D), row(ln_mem_g), row(ln_mem_b),
                w_mem_kv[0].astype(BF16))

    w_r = _pad_cols(w_router[0], LANES)
    wr_hi = w_r.astype(BF16)
    wr_lo = (w_r - wr_hi.astype(F32)).astype(BF16)
    b_r = jnp.concatenate([row(b_router[0]),
                           jnp.full((1, LANES - N_EXP), NEG_BIG, F32)], axis=1)
    h1, h1p, eidx, topw, cnt = _merge(
        x2, og, z, mq, proj, kv, lng, lnb,
        w_br_gla[0].astype(BF16), ccs, perm, w_br_fnet[0].astype(BF16),
        w_br_mem[0].astype(BF16), w_out[0].astype(BF16), row(b_out[0]),
        row(ln1_g[0]), row(ln1_b[0]), jnp.concatenate([wr_hi, wr_lo], axis=1), wr_hi, b_r)

    dest = _plan(eidx, cnt)
    counts = cnt[0, :N_EXP].astype(I32)
    dest_k = dest[:, :TOP_K]
    idx4 = dest_k.reshape(SC_WORKERS, SCD_NCH, SC_CH, TOP_K).transpose(0, 1, 3, 2)
    xs = _sc_dispatch(h1p, idx4.reshape(SC_WORKERS, SCD_NCH * TOP_K, SC_CH))
    ys = _experts(_work_items(counts), xs, w_gu[0], b_gu[0], w_down[0], b_down[0])
    out = h1
    for grp in range(COMB_GROUPS):
        dest_g = dest_k[grp * COMB_TG:(grp + 1) * COMB_TG].T.reshape(SC_WORKERS, SC_NCH, SC_CH)
        g4 = _sc_gather(ys, dest_g).reshape(TOP_K, COMB_TG, D // 2)
        out = _combine_dense(g4, out, topw, row(ln2_g[0]), row(ln2_b[0]), grp)
    return out.reshape(BATCH, SEQ, D)
```

```python
import functools
import math

import numpy as np
import jax
import jax.numpy as jnp
from jax import lax
from jax.experimental import pallas as pl
from jax.experimental.pallas import tpu as pltpu
from jax.experimental.pallas import tpu_sc as plsc

F32 = jnp.float32
BF16 = jnp.bfloat16
I32 = jnp.int32
U32 = jnp.uint32

D = 1024
BATCH = 4
SEQ = 4096
T = BATCH * SEQ
GLA_H = 4
GLA_DK = 128
GLA_DV = 256
GLA_LR = 16
GLA_TAU = 16.0
GLA_C = 64
FN_G = 4
FN_GW = 128
FN_W = 512
MEM_LEN = 256
MEM_H = 4
MEM_HD = 128
MQ_W = 512
N_EXP = 32
TOP_K = 4
D_FF = 1024
SW_LIMIT = 7.0
SW_ALPHA = 1.702
LN_EPS = 1e-5
RMS_EPS = 1e-6
DN_ALPHA = 2.0 ** 0.25
A_ROWS = T * TOP_K

FFT_N1 = 128
FFT_N2 = 32

LANES = 128
NEG_BIG = -1e30
MIB = 1024 * 1024


def _ln(x, g, b):
    mu = jnp.mean(x, axis=-1, keepdims=True)
    xc = x - mu
    var = jnp.mean(xc * xc, axis=-1, keepdims=True)
    return xc * lax.rsqrt(var + LN_EPS) * g + b


def _dot(a, b):
    return jnp.dot(a, b, preferred_element_type=F32)


def _dot_nt(a, b):
    return lax.dot_general(a, b, (((1,), (1,)), ((), ())), preferred_element_type=F32)


def _dot_tn(a, b):
    return lax.dot_general(a, b, (((0,), (0,)), ((), ())), preferred_element_type=F32)


def _interleave(*stages):
    live = list(stages)
    while live:
        for st in list(live):
            try:
                next(st)
            except StopIteration:
                live.remove(st)


def _split_bf16(a):
    hi = a.astype(BF16)
    return hi, (a - hi.astype(F32)).astype(BF16)


INPROJ_TM = 1024
INPROJ_TN = 3072


PROJ_W = 6 * 1024


def _inproj_kernel(x_ref, g_ref, b_ref, w_ref, bias_ref, wmq_ref, bmq_ref, wlr_ref, blr_ref,
                   proj_ref, mq_ref, lr_ref, hb_ref):
    @pl.when(pl.program_id(1) == 0)
    def _():
        hb = _ln(x_ref[...], g_ref[...], b_ref[...]).astype(BF16)
        hb_ref[...] = hb
        lr_ref[...] = _dot(hb, wlr_ref[...]) + blr_ref[...]
        mq_ref[...] = (_dot(hb, wmq_ref[...]) + bmq_ref[...]).astype(BF16)

    proj_ref[...] = (_dot(hb_ref[...], w_ref[...]) + bias_ref[...]).astype(BF16)


def _inproj(x2, ln_g, ln_b, w_main, b_main, w_mq, b_mq, w_lr, b_lr):
    tm, tn = INPROJ_TM, INPROJ_TN
    nj = PROJ_W // tn
    row = lambda i, j: (i, 0)
    const = lambda i, j: (0, 0)
    outs = (
        jax.ShapeDtypeStruct((T, PROJ_W), BF16),
        jax.ShapeDtypeStruct((T, MQ_W), BF16),
        jax.ShapeDtypeStruct((T, LANES), F32),
    )
    return pl.pallas_call(
        _inproj_kernel,
        grid=(T // tm, nj),
        in_specs=[
            pl.BlockSpec((tm, D), row),
            pl.BlockSpec((1, D), const),
            pl.BlockSpec((1, D), const),
            pl.BlockSpec((D, tn), lambda i, j: (0, j)),
            pl.BlockSpec((1, tn), lambda i, j: (0, j)),
            pl.BlockSpec((D, MQ_W), const),
            pl.BlockSpec((1, MQ_W), const),
            pl.BlockSpec((D, LANES), const),
            pl.BlockSpec((1, LANES), const),
        ],
        out_specs=[
            pl.BlockSpec((tm, tn), lambda i, j: (i, j)),
            pl.BlockSpec((tm, MQ_W), row),
            pl.BlockSpec((tm, LANES), row),
        ],
        out_shape=outs,
        scratch_shapes=[pltpu.VMEM((tm, D), BF16)],
        compiler_params=pltpu.CompilerParams(
            dimension_semantics=("arbitrary", "arbitrary"),
            vmem_limit_bytes=48 * MIB),
        name="ln_inproj",
    )(x2, ln_g, ln_b, w_main, b_main, w_mq, b_mq, w_lr, b_lr)


GLA_BULK = 256
GLA_FIN = 512
GLA_NCH = SEQ // GLA_C
GLA_CPB = GLA_BULK // GLA_C
GLA_PIPE = 4


def _gla_kernel(q_ref, k_ref, v_ref, r_ref, lr_ref, wd_ref, bd_ref, g_ref, cs_ref, o_ref,
                acc_ref, qin_ref, kin_ref, kst_ref, dec_ref, u_ref, stf_ref, stb_ref):
    C = GLA_C
    G = GLA_BULK
    DK = GLA_DK
    NG = SEQ // G
    scale = DK ** -0.5
    ii = lax.broadcasted_iota(I32, (G, G), 0)
    jj = lax.broadcasted_iota(I32, (G, G), 1)
    same = (ii // C) == (jj // C)
    lower = jnp.logical_and(same, ii >= jj)
    upper = jnp.logical_and(same, ii <= jj)
    is_fwd = lax.broadcasted_iota(I32, (G, 2 * DK), 1) < DK
    chunk_of_row = lax.broadcasted_iota(I32, (G, DK), 0) // C

    def stage_a(gi):
        rows = pl.ds(pl.multiple_of(gi * G, G), G)
        z = _dot(jnp.concatenate(_split_bf16(lr_ref[rows, :]), axis=1), wd_ref[...]) + bd_ref[...]
        yield
        la = -(jnp.maximum(-z, 0.0) + jnp.log(1.0 + jnp.exp(-jnp.abs(z)))) * (1.0 / GLA_TAU)
        la_hi, la_lo = _split_bf16(la)
        pre2 = _dot(cs_ref[...], jnp.concatenate([la_hi, la_lo], axis=1))
        yield
        pre = pre2[:, :2 * DK] + pre2[:, 2 * DK:]
        blast = jnp.concatenate(
            [jnp.broadcast_to(pre[ci * C + C - 1:ci * C + C, :], (C, 2 * DK))
             for ci in range(GLA_CPB)], axis=0)
        b = jnp.where(is_fwd, pre, blast - pre + la)
        qf32 = q_ref[rows, :].astype(F32)
        kf32 = k_ref[rows, :].astype(F32)
        q2 = jnp.concatenate([qf32, qf32], axis=1)
        k2 = jnp.concatenate([kf32, kf32], axis=1)
        qin_ref[rows, :] = (q2 * (scale * jnp.exp(b))).astype(BF16)
        kin_ref[rows, :] = (k2 * jnp.exp(-b)).astype(BF16)
        kst_ref[rows, :] = (k2 * jnp.exp(blast - b)).astype(BF16)
        dec = jnp.exp(blast)
        for ci in range(GLA_CPB):
            dec_ref[pl.ds(gi * GLA_CPB + ci, 1), :] = dec[ci * C:ci * C + 1, :]

    def stage_b(gi):
        rows = pl.ds(pl.multiple_of(gi * G, G), G)
        qi = qin_ref[rows, :]
        ki = kin_ref[rows, :]
        ks = kst_ref[rows, :]
        vb = v_ref[rows, :]
        att_f = _dot_nt(qi[:, :DK], ki[:, :DK])
        att_b = _dot_nt(qi[:, DK:], ki[:, DK:])
        yield
        att = jnp.where(lower, att_f, 0.0) + jnp.where(upper, att_b, 0.0)
        acc_ref[rows, :] = _dot(att.astype(BF16), vb)
        yield
        ksb = jnp.concatenate(
            [jnp.where(chunk_of_row == ci, ks[:, d * DK:(d + 1) * DK], jnp.zeros((G, DK), BF16))
             for d in range(2) for ci in range(GLA_CPB)], axis=1)
        u = _dot_tn(vb, ksb)
        for d in range(2):
            for ci in range(GLA_CPB):
                col = (d * GLA_CPB + ci) * DK
                u_ref[d, gi * GLA_CPB + ci] = u[:, col:col + DK]

    P = GLA_PIPE
    _interleave(*[stage_a(j) for j in range(P)])

    def bulk(i, carry):
        g = P * i
        _interleave(*[st for j in range(P) for st in (stage_b(g - P + j), stage_a(g + j))])
        return carry

    lax.fori_loop(1, NG // P, bulk, 0)
    _interleave(*[stage_b(NG - P + j) for j in range(P)])

    stf_ref[...] = jnp.zeros_like(stf_ref)
    stb_ref[...] = jnp.zeros_like(stb_ref)

    def one(n, d, st_ref):
        lanes = slice(d * DK, (d + 1) * DK)
        rows = pl.ds(pl.multiple_of(n * C, C), C)
        st = st_ref[...]
        acc_ref[rows, :] += _dot_nt(qin_ref[rows, lanes], st.astype(BF16))
        st_ref[...] = st * dec_ref[pl.ds(n, 1), :][:, lanes] + u_ref[d, n]

    def step(i, carry):
        one(i, 0, stf_ref)
        one(GLA_NCH - 1 - i, 1, stb_ref)
        return carry

    lax.fori_loop(0, GLA_NCH, step, 0, unroll=16)

    def fin(gi, carry):
        rows = pl.ds(pl.multiple_of(gi * GLA_FIN, GLA_FIN), GLA_FIN)
        o = acc_ref[rows, :]
        o = o * lax.rsqrt(jnp.mean(o * o, axis=-1, keepdims=True) + RMS_EPS) * g_ref[...]
        rg = r_ref[rows, :].astype(F32)
        o_ref[rows, :] = (o * (rg * jax.nn.sigmoid(rg))).astype(BF16)
        return carry

    lax.fori_loop(0, SEQ // GLA_FIN, fin, 0)


def _gla(proj, lr, wd, bd, g):
    i = np.arange(GLA_BULK)
    cs = ((i[:, None] // GLA_C) == (i[None, :] // GLA_C)) & (i[:, None] >= i[None, :])
    cs = jnp.asarray(cs, dtype=F32).astype(BF16)
    v_blk = 1024 // GLA_DV
    return pl.pallas_call(
        _gla_kernel,
        grid=(BATCH, GLA_H),
        in_specs=[
            pl.BlockSpec((SEQ, GLA_DK), lambda b, h: (b, h)),
            pl.BlockSpec((SEQ, GLA_DK), lambda b, h: (b, GLA_H + h)),
            pl.BlockSpec((SEQ, GLA_DV), lambda b, h: (b, v_blk + h)),
            pl.BlockSpec((SEQ, GLA_DV), lambda b, h: (b, 2 * v_blk + h)),
            pl.BlockSpec((SEQ, LANES), lambda b, h: (b, 0)),
            pl.BlockSpec((2 * LANES, 2 * GLA_DK), lambda b, h: (0, h)),
            pl.BlockSpec((1, 2 * GLA_DK), lambda b, h: (0, h)),
            pl.BlockSpec((1, GLA_DV), lambda b, h: (0, 0)),
            pl.BlockSpec((GLA_BULK, GLA_BULK), lambda b, h: (0, 0)),
        ],
        out_specs=pl.BlockSpec((SEQ, GLA_DV), lambda b, h: (b, h)),
        out_shape=jax.ShapeDtypeStruct((T, GLA_H * GLA_DV), BF16),
        scratch_shapes=[
            pltpu.VMEM((SEQ, GLA_DV), F32),
            pltpu.VMEM((SEQ, 2 * GLA_DK), BF16),
            pltpu.VMEM((SEQ, 2 * GLA_DK), BF16),
            pltpu.VMEM((SEQ, 2 * GLA_DK), BF16),
            pltpu.VMEM((GLA_NCH, 2 * GLA_DK), F32),
            pltpu.VMEM((2, GLA_NCH, GLA_DV, GLA_DK), F32),
            pltpu.VMEM((GLA_DV, GLA_DK), F32),
            pltpu.VMEM((GLA_DV, GLA_DK), F32),
        ],
        compiler_params=pltpu.CompilerParams(
            dimension_semantics=("arbitrary", "arbitrary"),
            vmem_limit_bytes=58 * MIB),
        name="gla",
    )(proj, proj, proj, proj, lr, wd, bd, g, cs)


FFT1_S = 16
FFT1_SUB = 2
FFT1_ROWS = FFT_N2 * FFT1_S
FFT2_KB = 8


def _fft1_kernel(x_ref, g_ref, b_ref, w_ref, bias_ref, fbig_ref, cw_ref, sw_ref, o_ref):
    sh = FFT1_S // FFT1_SUB
    rows = FFT_N2 * sh
    res = {}

    def sub(h):
        xv = x_ref[:, h * sh:(h + 1) * sh, :].reshape(rows, D)
        hb = _ln(xv, g_ref[...], b_ref[...]).astype(BF16)
        fn = (_dot(hb, w_ref[...]) + bias_ref[...]).astype(BF16)
        yield
        a = _dot(fbig_ref[...], fn)
        yield
        ar = a[:rows]
        ai = a[rows:]
        cw = jnp.concatenate([cw_ref[h]] * (FN_W // LANES), axis=1)
        sw = jnp.concatenate([sw_ref[h]] * (FN_W // LANES), axis=1)
        res[h] = ((ar * cw + ai * sw).reshape(FFT_N2, sh, FN_W),
                  (ai * cw - ar * sw).reshape(FFT_N2, sh, FN_W))

    _interleave(*[sub(h) for h in range(FFT1_SUB)])
    for ri in range(2):
        o_ref[ri] = jnp.concatenate([res[h][ri] for h in range(FFT1_SUB)], axis=1).astype(BF16)


def _fft1(x4, ln_g, ln_b, w_fn, b_fn, fbig, cwt, swt):
    s = FFT1_S
    const = lambda b, j: (0, 0)
    return pl.pallas_call(
        _fft1_kernel,
        grid=(BATCH, FFT_N1 // s),
        in_specs=[
            pl.BlockSpec((None, FFT_N2, s, D), lambda b, j: (b, 0, j, 0)),
            pl.BlockSpec((1, D), const),
            pl.BlockSpec((1, D), const),
            pl.BlockSpec((D, FN_W), const),
            pl.BlockSpec((1, FN_W), const),
            pl.BlockSpec((2 * FFT1_ROWS // FFT1_SUB, FFT1_ROWS // FFT1_SUB), const),
            pl.BlockSpec((FFT1_SUB, FFT1_ROWS // FFT1_SUB, LANES), lambda b, j: (j, 0, 0)),
            pl.BlockSpec((FFT1_SUB, FFT1_ROWS // FFT1_SUB, LANES), lambda b, j: (j, 0, 0)),
        ],
        out_specs=pl.BlockSpec((None, 2, FFT_N2, s, FN_W), lambda b, j: (b, 0, 0, j, 0)),
        out_shape=jax.ShapeDtypeStruct((BATCH, 2, FFT_N2, FFT_N1, FN_W), BF16),
        compiler_params=pltpu.CompilerParams(
            dimension_semantics=("arbitrary", "arbitrary"),
            vmem_limit_bytes=40 * MIB),
        name="fft_stage1",
    )(x4, ln_g, ln_b, w_fn, b_fn, fbig, cwt, swt)


def _fft2_kernel(d_ref, f2_ref, o_ref):
    f2 = f2_ref[...]
    for kk in range(FFT2_KB):
        z = _dot(f2, jnp.concatenate([d_ref[0, kk], d_ref[1, kk]], axis=0))
        o_ref[0, kk] = z[:FFT_N1].astype(BF16)
        o_ref[1, kk] = z[FFT_N1:].astype(BF16)


def _fft2(dmat, f2):
    kb = FFT2_KB
    blk = (None, 2, kb, FFT_N1, FN_W)
    return pl.pallas_call(
        _fft2_kernel,
        grid=(BATCH, FFT_N2 // kb),
        in_specs=[
            pl.BlockSpec(blk, lambda b, j: (b, 0, j, 0, 0)),
            pl.BlockSpec((2 * FFT_N1, 2 * FFT_N1), lambda b, j: (0, 0)),
        ],
        out_specs=pl.BlockSpec(blk, lambda b, j: (b, 0, j, 0, 0)),
        out_shape=jax.ShapeDtypeStruct((BATCH, 2, FFT_N2, FFT_N1, FN_W), BF16),
        compiler_params=pltpu.CompilerParams(
            dimension_semantics=("arbitrary", "arbitrary")),
        name="fft_stage2",
    )(dmat, f2)


def _dft_tables(merge_tm):
    s = FFT1_S // FFT1_SUB
    n2 = np.arange(FFT_N2, dtype=np.float64)
    n1 = np.arange(FFT_N1, dtype=np.float64)
    th = 2.0 * np.pi * np.outer(n2, n2) / FFT_N2
    f1 = np.stack([np.cos(th), -np.sin(th)]) / math.sqrt(SEQ)
    fbig = np.einsum("rkn,st->rksnt", f1, np.eye(s)).reshape(2 * FFT_N2 * s, FFT_N2 * s)
    tw = 2.0 * np.pi * np.outer(n2, n1) / SEQ
    tw = tw.reshape(FFT_N2, FFT_N1 // s, s).transpose(1, 0, 2).reshape(FFT_N1 // s, FFT_N2 * s)
    cwt = np.broadcast_to(np.cos(tw)[:, :, None], tw.shape + (LANES,))
    swt = np.broadcast_to(np.sin(tw)[:, :, None], tw.shape + (LANES,))
    th1 = 2.0 * np.pi * np.outer(n1, n1) / FFT_N1
    c1, s1 = np.cos(th1), np.sin(th1)
    f2 = np.block([[c1, s1], [-s1, c1]])
    cc = np.arange(FN_GW, dtype=np.float64)
    thc = 2.0 * np.pi * np.outer(cc, cc) / FN_GW
    ccs = np.concatenate([np.cos(thc), np.sin(thc)], axis=0) / math.sqrt(FN_GW)
    k1n = merge_tm // FFT_N2
    r = np.arange(merge_tm)
    perm = np.zeros((merge_tm, merge_tm))
    perm[r, (r % FFT_N2) * k1n + r // FFT_N2] = 1.0
    as32 = lambda a: jnp.asarray(np.ascontiguousarray(a), dtype=F32)
    return (as32(fbig).astype(BF16), as32(cwt), as32(swt), as32(f2).astype(BF16),
            as32(ccs).astype(BF16), as32(perm).astype(BF16))


def _memkv_kernel(m_ref, g_ref, b_ref, w_ref, o_ref):
    mn = _ln(m_ref[...], g_ref[...], b_ref[...]).astype(BF16)
    o_ref[...] = _dot(mn, w_ref[...]).astype(BF16)


def _memkv(mem2, g, b, w):
    return pl.pallas_call(
        _memkv_kernel,
        grid=(BATCH,),
        in_specs=[
            pl.BlockSpec((MEM_LEN, D), lambda i: (i, 0)),
            pl.BlockSpec((1, D), lambda i: (0, 0)),
            pl.BlockSpec((1, D), lambda i: (0, 0)),
            pl.BlockSpec((D, 2 * MQ_W), lambda i: (0, 0)),
        ],
        out_specs=pl.BlockSpec((MEM_LEN, 2 * MQ_W), lambda i: (i, 0)),
        out_shape=jax.ShapeDtypeStruct((BATCH * MEM_LEN, 2 * MQ_W), BF16),
        compiler_params=pltpu.CompilerParams(dimension_semantics=("arbitrary",)),
        name="mem_kv",
    )(mem2, g, b, w)


MERGE_TM = 512
MERGE_K1 = MERGE_TM // FFT_N2


def _pack_bf16_pair(v):
    n = v.shape[1] // 2
    bits = lax.bitcast_convert_type(v.astype(BF16).astype(F32), U32)
    return (bits[:, n:] & jnp.uint32(0xFFFF0000)) | (bits[:, :n] >> 16)


def _unpack_bf16_pair(p):
    lo = lax.bitcast_convert_type(p << 16, F32)
    hi = lax.bitcast_convert_type(p & jnp.uint32(0xFFFF0000), F32)
    return lo, hi


def _merge_kernel(x_ref, og_ref, zr_ref, zi_ref, mq_ref, gt_ref, kv_ref,
                  lng_ref, lnb_ref, wg_ref, ccs_ref, perm_ref, wf_ref, wm_ref, wo_ref, bo_ref,
                  l1g_ref, l1b_ref, wr2_ref, wrh_ref, br_ref,
                  h1_ref, h1p_ref, eidx_ref, topw_ref, cnt_ref):
    tm = MERGE_TM
    y = {}

    def branch_fnet():
        zr = zr_ref[...].reshape(tm, FN_W)
        zi = zi_ref[...].reshape(tm, FN_W)
        ys = []
        for g in range(FN_G):
            sl = slice(g * FN_GW, (g + 1) * FN_GW)
            ys.append(_dot(jnp.concatenate([zr[:, sl], zi[:, sl]], axis=1), ccs_ref[...]))
        yield
        yp = _dot(perm_ref[...], jnp.concatenate(ys, axis=1).astype(BF16))
        yield
        y["fnet"] = _dot(yp.astype(BF16), wf_ref[...])

    def branch_mem():
        heads = [slice(hd * MEM_HD, (hd + 1) * MEM_HD) for hd in range(MEM_H)]
        ss = [_dot_nt(mq_ref[:, sl], kv_ref[:, sl]) for sl in heads]
        yield
        oms = []
        for hd, s in enumerate(ss):
            s = s * (MEM_HD ** -0.5)
            s = s - jnp.max(s, axis=-1, keepdims=True)
            p = jnp.exp(s)
            p = p * (1.0 / jnp.sum(p, axis=-1, keepdims=True))
            oms.append(_dot(p.astype(BF16),
                            kv_ref[:, MQ_W + hd * MEM_HD:MQ_W + (hd + 1) * MEM_HD]))
        yield
        y["mem"] = _dot(jnp.concatenate(oms, axis=1).astype(BF16), wm_ref[...])

    def branch_gla():
        y["gla"] = _dot(og_ref[...], wg_ref[...])
        yield

    _interleave(branch_fnet(), branch_mem(), branch_gla())

    def gate(c):
        return 0.5 + 0.5 * jnp.tanh(0.5 * gt_ref[:, c * D:(c + 1) * D].astype(F32))

    merged = gate(0) * y["gla"] + gate(1) * y["fnet"] + gate(2) * y["mem"]
    mix = _dot(merged.astype(BF16), wo_ref[...]) + bo_ref[...]
    h = _ln(x_ref[...], lng_ref[...], lnb_ref[...])
    h1 = _ln(DN_ALPHA * h + mix, l1g_ref[...], l1b_ref[...])
    h1_ref[...] = h1
    h1p_ref[...] = _pack_bf16_pair(h1)

    h_hi, h_lo = _split_bf16(h1)
    d2 = _dot(h_hi, wr2_ref[...])
    l = d2[:, :LANES] + d2[:, LANES:] + _dot(h_lo, wrh_ref[...]) + br_ref[...]
    lane = lax.broadcasted_iota(I32, (tm, LANES), 1)
    vals, idxs = [], []
    for _ in range(TOP_K):
        m = jnp.max(l, axis=-1, keepdims=True)
        idx = jnp.min(jnp.where(l == m, lane, LANES), axis=-1, keepdims=True)
        vals.append(m)
        idxs.append(idx)
        l = jnp.where(lane == idx, -jnp.inf, l)
    es = [jnp.exp(v - vals[0]) for v in vals]
    den = es[0] + es[1] + es[2] + es[3]
    eo = jnp.zeros((tm, LANES), I32)
    wo = jnp.zeros((tm, LANES), F32)
    chosen = jnp.zeros((tm, LANES), F32)
    for k in range(TOP_K):
        eo = jnp.where(lane == k, idxs[k], eo)
        wo = jnp.where(lane == k, es[k] / den, wo)
        chosen = chosen + jnp.where(lane == idxs[k], 1.0, 0.0)
    eidx_ref[...] = eo
    topw_ref[...] = wo

    @pl.when(pl.program_id(0) == 0)
    def _():
        cnt_ref[...] = jnp.zeros_like(cnt_ref)

    cnt_ref[...] += jnp.broadcast_to(jnp.sum(chosen, axis=0, keepdims=True), cnt_ref.shape)


def _merge(x2, og, z, mq, gates, kv, lng, lnb, wg, ccs, perm, wf, wm, wo, bo, l1g, l1b,
           wr2, wrh, br):
    tm = MERGE_TM
    per_b = SEQ // tm
    row = lambda i: (i, 0)
    const = lambda i: (0, 0)
    zblk = (None, None, FFT_N2, MERGE_K1, FN_W)
    outs = (
        jax.ShapeDtypeStruct((T, D), F32),
        jax.ShapeDtypeStruct((T, D // 2), U32),
        jax.ShapeDtypeStruct((T, LANES), I32),
        jax.ShapeDtypeStruct((T, LANES), F32),
        jax.ShapeDtypeStruct((8, LANES), F32),
    )
    return pl.pallas_call(
        _merge_kernel,
        grid=(T // tm,),
        in_specs=[
            pl.BlockSpec((tm, D), row),
            pl.BlockSpec((tm, D), row),
            pl.BlockSpec(zblk, lambda i: (i // per_b, 0, 0, i % per_b, 0)),
            pl.BlockSpec(zblk, lambda i: (i // per_b, 1, 0, i % per_b, 0)),
            pl.BlockSpec((tm, MQ_W), row),
            pl.BlockSpec((tm, 3 * D), lambda i: (i, 1)),
            pl.BlockSpec((MEM_LEN, 2 * MQ_W), lambda i: (i // per_b, 0)),
            pl.BlockSpec((1, D), const), pl.BlockSpec((1, D), const),
            pl.BlockSpec((D, D), const),
            pl.BlockSpec((2 * FN_GW, FN_GW), const),
            pl.BlockSpec((tm, tm), const),
            pl.BlockSpec((FN_W, D), const),
            pl.BlockSpec((MQ_W, D), const),
            pl.BlockSpec((D, D), const),
            pl.BlockSpec((1, D), const),
            pl.BlockSpec((1, D), const), pl.BlockSpec((1, D), const),
            pl.BlockSpec((D, 2 * LANES), const),
            pl.BlockSpec((D, LANES), const),
            pl.BlockSpec((1, LANES), const),
        ],
        out_specs=[
            pl.BlockSpec((tm, D), row),
            pl.BlockSpec((tm, D // 2), row),
            pl.BlockSpec((tm, LANES), row),
            pl.BlockSpec((tm, LANES), row),
            pl.BlockSpec((8, LANES), const),
        ],
        out_shape=outs,
        compiler_params=pltpu.CompilerParams(
            dimension_semantics=("arbitrary",),
            vmem_limit_bytes=58 * MIB),
        name="merge_ln1_router",
    )(x2, og, z, z, mq, gates, kv, lng, lnb, wg, ccs, perm, wf, wm, wo, bo, l1g, l1b,
      wr2, wrh, br)


PLAN_TP = 1024


def _expert_onehots(e, lane):
    onehots = [lane == e[:, k:k + 1] for k in range(TOP_K)]
    mf = jnp.zeros(lane.shape, F32)
    for oh in onehots:
        mf = mf + jnp.where(oh, 1.0, 0.0)
    return onehots, mf


def _plan_kernel(e_ref, tot_ref, dest_ref, cnt_ref, off_ref):
    i = pl.program_id(0)
    tp = PLAN_TP
    lane = lax.broadcasted_iota(I32, (tp, LANES), 1)
    onehots, mf = _expert_onehots(e_ref[...], lane)

    @pl.when(i == 0)
    def _():
        tot = tot_ref[0:1, :]
        padded = jnp.floor((tot + (MOE_BM - 1)) * (1.0 / MOE_BM)) * MOE_BM
        lane1 = lax.broadcasted_iota(I32, (1, LANES), 1)
        inc = padded
        for s in (1, 2, 4, 8, 16, 32, 64):
            inc = inc + jnp.where(lane1 >= s, pltpu.roll(inc, s, 1), 0.0)
        off_ref[...] = inc - padded
        cnt_ref[...] = jnp.zeros_like(cnt_ref)

    ri = lax.broadcasted_iota(I32, (tp, tp), 0)
    ci = lax.broadcasted_iota(I32, (tp, tp), 1)
    ltri = jnp.where(ri > ci, 1.0, 0.0).astype(BF16)
    rank = _dot(ltri, mf.astype(BF16)) + cnt_ref[...] + off_ref[...]
    out = jnp.zeros((tp, LANES), I32)
    for k in range(TOP_K):
        dk = jnp.sum(jnp.where(onehots[k], rank, 0.0), axis=-1, keepdims=True)
        out = jnp.where(lane == k, dk.astype(I32), out)
    dest_ref[...] = out
    cnt_ref[...] += jnp.sum(mf, axis=0, keepdims=True)


def _plan(eidx, cnt):
    tp = PLAN_TP
    return pl.pallas_call(
        _plan_kernel,
        grid=(T // tp,),
        in_specs=[pl.BlockSpec((tp, LANES), lambda i: (i, 0)),
                  pl.BlockSpec((8, LANES), lambda i: (0, 0))],
        out_specs=pl.BlockSpec((tp, LANES), lambda i: (i, 0)),
        out_shape=jax.ShapeDtypeStruct((T, LANES), I32),
        scratch_shapes=[pltpu.VMEM((1, LANES), F32), pltpu.VMEM((1, LANES), F32)],
        compiler_params=pltpu.CompilerParams(dimension_semantics=("arbitrary",)),
        name="route_plan",
    )(eidx, cnt)


MOE_BM = 512
MOE_NW = A_ROWS // MOE_BM + N_EXP
XS_ROWS = MOE_NW * MOE_BM
MOE_FF_SLICES = 2


def _expert_kernel(we_ref, wb_ref, wv_ref, wfe_ref, wsl_ref, wnx_ref,
                   x_ref, wgu_hbm, bgu_ref, wdn_hbm, bdn_ref, o_ref,
                   wgu_f32, wdn_f32, wgu_bf, wdn_bf, sems):
    w = pl.program_id(0)
    e = we_ref[w]

    def weight_copies(expert, slot):
        return (pltpu.make_async_copy(wgu_hbm.at[expert], wgu_f32.at[slot], sems.at[slot, 0]),
                pltpu.make_async_copy(wdn_hbm.at[expert], wdn_f32.at[slot], sems.at[slot, 1]))

    @pl.when(w == 0)
    def _():
        for cp_ in weight_copies(e, 0):
            cp_.start()

    first = wfe_ref[w] == 1
    slot = wsl_ref[w]

    @pl.when(first)
    def _():
        for cp_ in weight_copies(e, slot):
            cp_.wait()
        nxt = wnx_ref[w]

        @pl.when(nxt >= 0)
        def _():
            for cp_ in weight_copies(nxt, 1 - slot):
                cp_.start(priority=1)

    hw = D_FF // MOE_FF_SLICES
    col_slices = [(slice(hf * hw, (hf + 1) * hw), slice(D_FF + hf * hw, D_FF + (hf + 1) * hw))
                  for hf in range(MOE_FF_SLICES)]

    def cast_weights():
        for gc, uc in col_slices:
            wgu_bf[:, gc] = wgu_f32[slot, :, gc].astype(BF16)
            wgu_bf[:, uc] = wgu_f32[slot, :, uc].astype(BF16)
            yield
        for hf in range(MOE_FF_SLICES):
            rs = slice(hf * hw, (hf + 1) * hw)
            wdn_bf[rs, :] = wdn_f32[slot, rs, :].astype(BF16)
            yield

    def ffn(rows):
        xlo, xhi = _unpack_bf16_pair(x_ref[rows, :])
        xb = jnp.concatenate([xlo.astype(BF16), xhi.astype(BF16)], axis=1)
        bgu = bgu_ref[pl.ds(e, 1), :]
        gus = []
        for gc, uc in col_slices:
            gus.append((_dot(xb, wgu_bf[:, gc]) + bgu[:, gc], _dot(xb, wgu_bf[:, uc]) + bgu[:, uc]))
            yield
        out = bdn_ref[pl.ds(e, 1), :]
        for hf, (g, u) in enumerate(gus):
            gate = jnp.minimum(g, SW_LIMIT)
            up = jnp.clip(u, -SW_LIMIT, SW_LIMIT)
            act = (up + 1.0) * (gate * jax.nn.sigmoid(SW_ALPHA * gate))
            out = out + _dot(act.astype(BF16), wdn_bf[hf * hw:(hf + 1) * hw, :])
            if hf + 1 < MOE_FF_SLICES:
                yield
        o_ref[rows, :] = _pack_bf16_pair(out)

    for code, rows in ((1, slice(None)), (2, slice(0, MOE_BM // 2))):
        @pl.when(jnp.logical_and(wv_ref[w] == code, first))
        def _(rows=rows):
            _interleave(cast_weights(), ffn(rows))

        @pl.when(jnp.logical_and(wv_ref[w] == code, jnp.logical_not(first)))
        def _(rows=rows):
            _interleave(ffn(rows))

    @pl.when(wv_ref[w] == 2)
    def _():
        o_ref[MOE_BM // 2:, :] = jnp.zeros((MOE_BM // 2, D // 2), U32)

    @pl.when(wv_ref[w] == 0)
    def _():
        o_ref[...] = jnp.zeros_like(o_ref)


def _experts(meta, xs, w_gu, b_gu, w_down, b_down):
    return pl.pallas_call(
        _expert_kernel,
        grid_spec=pltpu.PrefetchScalarGridSpec(
            num_scalar_prefetch=len(meta),
            grid=(MOE_NW,),
            in_specs=[
                pl.BlockSpec((MOE_BM, D // 2), lambda w, we, wb, *_: (wb[w], 0)),
                pl.BlockSpec(memory_space=pl.ANY),
                pl.BlockSpec((N_EXP, 2 * D_FF), lambda w, *_: (0, 0)),
                pl.BlockSpec(memory_space=pl.ANY),
                pl.BlockSpec((N_EXP, D), lambda w, *_: (0, 0)),
            ],
            out_specs=pl.BlockSpec((MOE_BM, D // 2), lambda w, *_: (w, 0)),
            scratch_shapes=[
                pltpu.VMEM((2, D, 2 * D_FF), F32),
                pltpu.VMEM((2, D_FF, D), F32),
                pltpu.VMEM((D, 2 * D_FF), BF16),
                pltpu.VMEM((D_FF, D), BF16),
                pltpu.SemaphoreType.DMA((2, 2)),
            ],
        ),
        out_shape=jax.ShapeDtypeStruct((XS_ROWS, D // 2), U32),
        compiler_params=pltpu.CompilerParams(
            dimension_semantics=("arbitrary",),
            vmem_limit_bytes=56 * MIB),
        name="moe_experts",
    )(*meta, xs, w_gu, b_gu, w_down, b_down)


def _work_items(counts):
    n_e = (counts + MOE_BM - 1) // MOE_BM
    item_end = jnp.cumsum(n_e)
    total = item_end[-1]
    w = jnp.arange(MOE_NW, dtype=I32)
    valid = w < total
    wc = jnp.minimum(w, total - 1)
    e_w = jnp.sum((item_end[None, :] <= wc[:, None]).astype(I32), axis=1)
    e_w = jnp.minimum(e_w, N_EXP - 1)
    rows_here = counts[e_w] - (wc - (item_end - n_e)[e_w]) * MOE_BM
    valid = jnp.where(valid, jnp.where(rows_here <= MOE_BM // 2, 2, 1), 0)
    prev_e = jnp.concatenate([jnp.full((1,), -1, I32), e_w[:-1]])
    fe = (e_w != prev_e).astype(I32)
    slot = (jnp.cumsum(fe) - 1) % 2
    first_at = jnp.where(fe == 1, w, MOE_NW)
    next_first = jnp.concatenate([lax.cummin(first_at, reverse=True)[1:],
                                  jnp.full((1,), MOE_NW, I32)])
    nxt = jnp.where(next_first < MOE_NW, e_w[jnp.minimum(next_first, MOE_NW - 1)], -1)
    return tuple(a.astype(I32) for a in (e_w, wc, valid, fe, slot, nxt))


COMB_TM = 512
SC_CORES = 2
SC_SUBCORES = 16
SC_WORKERS = SC_CORES * SC_SUBCORES
SC_CH = 64
COMB_GROUPS = 4
COMB_TG = T // COMB_GROUPS
SC_ROWS_PER_W = COMB_TG * TOP_K // SC_WORKERS
SC_NCH = SC_ROWS_PER_W // SC_CH


def _sc_gather(table, idx3):
    mesh = plsc.VectorSubcoreMesh(core_axis_name="c", subcore_axis_name="s")

    @functools.partial(
        pl.kernel, mesh=mesh,
        out_type=jax.ShapeDtypeStruct((COMB_TG * TOP_K, D // 2), U32),
        scratch_types=[
            pltpu.VMEM((SC_NCH, SC_CH), I32),
            pltpu.VMEM((2, SC_CH, D // 2), U32),
            pltpu.SemaphoreType.DMA((2,)),
            pltpu.SemaphoreType.DMA((2,)),
        ],
    )
    def k(table_hbm, idx_hbm, out_hbm, idx_v, rows_v, gsem, psem):
        wid = lax.axis_index("s") * SC_CORES + lax.axis_index("c")
        base = wid * SC_ROWS_PER_W
        pltpu.sync_copy(idx_hbm.at[wid], idx_v)

        def gather(j, b):
            return pltpu.make_async_copy(table_hbm.at[idx_v.at[j]], rows_v.at[b], gsem.at[b])

        def put(j, b):
            return pltpu.make_async_copy(rows_v.at[b], out_hbm.at[pl.ds(base + j * SC_CH, SC_CH)],
                                         psem.at[b])

        gather(0, 0).start()

        @pl.loop(0, SC_NCH, step=2)
        def _(j0):
            for b in range(2):
                j = j0 + b

                @pl.when(j + 1 < SC_NCH)
                def _():
                    @pl.when(j >= 1)
                    def _():
                        put(j - 1, 1 - b).wait()
                    gather(j + 1, 1 - b).start()

                gather(j, b).wait()
                put(j, b).start()

        put(SC_NCH - 2, 0).wait()
        put(SC_NCH - 1, 1).wait()

    return k(table, idx3)


SCD_TOK_PER_W = T // SC_WORKERS
SCD_NCH = SCD_TOK_PER_W // SC_CH


def _sc_dispatch(h1p, idx4):
    mesh = plsc.VectorSubcoreMesh(core_axis_name="c", subcore_axis_name="s")

    @functools.partial(
        pl.kernel, mesh=mesh,
        out_type=jax.ShapeDtypeStruct((XS_ROWS, D // 2), U32),
        scratch_types=[
            pltpu.VMEM((SCD_NCH * TOP_K, SC_CH), I32),
            pltpu.VMEM((2, SC_CH, D // 2), U32),
            pltpu.SemaphoreType.DMA((2,)),
            pltpu.SemaphoreType.DMA((2,)),
        ],
    )
    def k(h_hbm, idx_hbm, xs_hbm, idx_v, rows_v, gsem, psem):
        wid = lax.axis_index("s") * SC_CORES + lax.axis_index("c")
        base = wid * SCD_TOK_PER_W
        pltpu.sync_copy(idx_hbm.at[wid], idx_v)

        def get(c, b):
            return pltpu.make_async_copy(h_hbm.at[pl.ds(base + c * SC_CH, SC_CH)], rows_v.at[b],
                                         gsem.at[b])

        def puts(c, b):
            return [pltpu.make_async_copy(rows_v.at[b], xs_hbm.at[idx_v.at[c * TOP_K + kk]],
                                          psem.at[b]) for kk in range(TOP_K)]

        get(0, 0).start()

        @pl.loop(0, SCD_NCH, step=2)
        def _(c0):
            for b in range(2):
                c = c0 + b

                @pl.when(c + 1 < SCD_NCH)
                def _():
                    @pl.when(c >= 1)
                    def _():
                        for cp_ in puts(c - 1, 1 - b):
                            cp_.wait()
                    get(c + 1, 1 - b).start()

                get(c, b).wait()
                for cp_ in puts(c, b):
                    cp_.start()

        for cp_ in puts(SCD_NCH - 2, 0) + puts(SCD_NCH - 1, 1):
            cp_.wait()

    return k(h1p, idx4)


def _combine_dense_kernel(g_ref, h1_ref, tw_ref, lg_ref, lb_ref, o_ref):
    tw = tw_ref[...]
    ylo = jnp.zeros((COMB_TM, D // 2), F32)
    yhi = jnp.zeros((COMB_TM, D // 2), F32)
    for k in range(TOP_K):
        lo, hi = _unpack_bf16_pair(g_ref[k])
        wk = tw[:, k:k + 1]
        ylo = ylo + lo * wk
        yhi = yhi + hi * wk
    ff = jnp.concatenate([ylo, yhi], axis=1)
    o_ref[...] = _ln(DN_ALPHA * h1_ref[...] + ff, lg_ref[...], lb_ref[...])


def _combine_dense(g4, h1, topw, g, b, group):
    tm = COMB_TM
    t0 = group * (COMB_TG // tm)
    return pl.pallas_call(
        _combine_dense_kernel,
        grid=(COMB_TG // tm,),
        in_specs=[
            pl.BlockSpec((TOP_K, tm, D // 2), lambda i: (0, i, 0)),
            pl.BlockSpec((tm, D), lambda i: (t0 + i, 0)),
            pl.BlockSpec((tm, LANES), lambda i: (t0 + i, 0)),
            pl.BlockSpec((1, D), lambda i: (0, 0)),
            pl.BlockSpec((1, D), lambda i: (0, 0)),
        ],
        out_specs=pl.BlockSpec((tm, D), lambda i: (t0 + i, 0)),
        out_shape=jax.ShapeDtypeStruct((T, D), F32),
        input_output_aliases={1: 0},
        compiler_params=pltpu.CompilerParams(
            dimension_semantics=("arbitrary",),
            vmem_limit_bytes=40 * MIB),
        name="moe_combine_dense_ln2",
    )(g4, h1, topw, g, b)


def _pad_cols(a, n):
    return jnp.pad(a, ((0, 0), (0, n - a.shape[1])))


def kernel(x, mem, ln_in_g, ln_in_b, ln_mem_g, ln_mem_b, w_in, b_in, w_decay_f, b_decay_f,
           w_decay_b, b_decay_b, gla_norm_g, w_br_gla, w_br_fnet, w_br_mem, w_mem_kv, w_out,
           b_out, ln1_g, ln1_b, w_router, b_router, w_gu, b_gu, w_down, b_down, ln2_g, ln2_b):
    assert x.shape == (BATCH, SEQ, D) and w_in.shape[0] == 1
    row = lambda a: a.reshape(1, -1)
    x2 = x.reshape(T, D)
    w_in0, b_in0 = w_in[0], b_in[0]
    c_lr, c_fn, c_mq, c_gt = 3072, 3072 + 2 * GLA_LR, 3104 + FN_W, 3104 + FN_W + MQ_W
    w_main = jnp.concatenate([w_in0[:, :c_lr], w_in0[:, c_gt:]], axis=1).astype(BF16)
    b_main = row(jnp.concatenate([b_in0[:c_lr], b_in0[c_gt:]]))
    w_lr = _pad_cols(w_in0[:, c_lr:c_fn], LANES).astype(BF16)
    b_lr = _pad_cols(row(b_in0[c_lr:c_fn]), LANES)
    w_mq = w_in0[:, c_mq:c_gt].astype(BF16)
    b_mq = row(b_in0[c_mq:c_gt])
    w_fn = w_in0[:, c_fn:c_mq].astype(BF16)
    b_fn = row(b_in0[c_fn:c_mq])
    lng, lnb = row(ln_in_g), row(ln_in_b)

    proj, mq, lr = _inproj(x2, lng, lnb, w_main, b_main, w_mq, b_mq, w_lr, b_lr)

    zpad = jnp.zeros((LANES - 2 * GLA_LR, GLA_H * GLA_DK), F32)
    zlr = jnp.zeros((GLA_LR, GLA_H * GLA_DK), F32)
    wdf = jnp.concatenate([w_decay_f[0], zlr, zpad], axis=0).reshape(LANES, GLA_H, GLA_DK)
    wdb = jnp.concatenate([zlr, w_decay_b[0], zpad], axis=0).reshape(LANES, GLA_H, GLA_DK)
    wd = jnp.concatenate([wdf, wdb], axis=2).reshape(LANES, GLA_H * 2 * GLA_DK).astype(BF16)
    wd = jnp.concatenate([wd, wd], axis=0)
    bd = jnp.concatenate([b_decay_f[0].reshape(GLA_H, GLA_DK),
                          b_decay_b[0].reshape(GLA_H, GLA_DK)], axis=1).reshape(1, -1)
    og = _gla(proj, lr, wd, bd, row(gla_norm_g[0]))

    fbig, cwt, swt, f2, ccs, perm = _dft_tables(MERGE_TM)
    x4 = x.reshape(BATCH, FFT_N2, FFT_N1, D)
    z = _fft2(_fft1(x4, lng, lnb, w_fn, b_fn, fbig, cwt, swt), f2)

    kv = _memkv(mem.reshape(BATCH * MEM_LEN, D), row(ln_mem_g), row(ln_mem_b),
                w_mem_kv[0].astype(BF16))

    w_r = _pad_cols(w_router[0], LANES)
    wr_hi = w_r.astype(BF16)
    wr_lo = (w_r - wr_hi.astype(F32)).astype(BF16)
    b_r = jnp.concatenate([row(b_router[0]),
                           jnp.full((1, LANES - N_EXP), NEG_BIG, F32)], axis=1)
    h1, h1p, eidx, topw, cnt = _merge(
        x2, og, z, mq, proj, kv, lng, lnb,
        w_br_gla[0].astype(BF16), ccs, perm, w_br_fnet[0].astype(BF16),
        w_br_mem[0].astype(BF16), w_out[0].astype(BF16), row(b_out[0]),
        row(ln1_g[0]), row(ln1_b[0]), jnp.concatenate([wr_hi, wr_lo], axis=1), wr_hi, b_r)

    dest = _plan(eidx, cnt)
    counts = cnt[0, :N_EXP].astype(I32)
    dest_k = dest[:, :TOP_K]
    idx4 = dest_k.reshape(SC_WORKERS, SCD_NCH, SC_CH, TOP_K).transpose(0, 1, 3, 2)
    xs = _sc_dispatch(h1p, idx4.reshape(SC_WORKERS, SCD_NCH * TOP_K, SC_CH))
    ys = _experts(_work_items(counts), xs, w_gu[0], b_gu[0], w_down[0], b_down[0])
    out = h1
    for grp in range(COMB_GROUPS):
        dest_g = dest_k[grp * COMB_TG:(grp + 1) * COMB_TG].T.reshape(SC_WORKERS, SC_NCH, SC_CH)
        g4 = _sc_gather(ys, dest_g).reshape(TOP_K, COMB_TG, D // 2)
        out = _combine_dense(g4, out, topw, row(ln2_g[0]), row(ln2_b[0]), grp)
    return out.reshape(BATCH, SEQ, D)
```

```python
import functools
import math

import numpy as np
import jax
import jax.numpy as jnp
from jax import lax
from jax.experimental import pallas as pl
from jax.experimental.pallas import tpu as pltpu
from jax.experimental.pallas import tpu_sc as plsc

F32 = jnp.float32
BF16 = jnp.bfloat16
I32 = jnp.int32
U32 = jnp.uint32

D = 1024
BATCH = 4
SEQ = 4096
T = BATCH * SEQ
GLA_H = 4
GLA_DK = 128
GLA_DV = 256
GLA_LR = 16
GLA_TAU = 16.0
GLA_C = 64
FN_G = 4
FN_GW = 128
FN_W = 512
MEM_LEN = 256
MEM_H = 4
MEM_HD = 128
MQ_W = 512
N_EXP = 32
TOP_K = 4
D_FF = 1024
SW_LIMIT = 7.0
SW_ALPHA = 1.702
LN_EPS = 1e-5
RMS_EPS = 1e-6
DN_ALPHA = 2.0 ** 0.25
A_ROWS = T * TOP_K

FFT_N1 = 128
FFT_N2 = 32

LANES = 128
NEG_BIG = -1e30
MIB = 1024 * 1024


def _ln(x, g, b):
    mu = jnp.mean(x, axis=-1, keepdims=True)
    xc = x - mu
    var = jnp.mean(xc * xc, axis=-1, keepdims=True)
    return xc * lax.rsqrt(var + LN_EPS) * g + b


def _dot(a, b):
    return jnp.dot(a, b, preferred_element_type=F32)


def _dot_nt(a, b):
    return lax.dot_general(a, b, (((1,), (1,)), ((), ())), preferred_element_type=F32)


def _dot_tn(a, b):
    return lax.dot_general(a, b, (((0,), (0,)), ((), ())), preferred_element_type=F32)


def _interleave(*stages):
    live = list(stages)
    while live:
        for st in list(live):
            try:
                next(st)
            except StopIteration:
                live.remove(st)


def _split_bf16(a):
    hi = a.astype(BF16)
    return hi, (a - hi.astype(F32)).astype(BF16)


INPROJ_TM = 1024
INPROJ_TN = 3072


PROJ_W = 6 * 1024


def _inproj_kernel(x_ref, g_ref, b_ref, w_ref, bias_ref, wmq_ref, bmq_ref, wlr_ref, blr_ref,
                   proj_ref, mq_ref, lr_ref, hb_ref):
    @pl.when(pl.program_id(1) == 0)
    def _():
        hb = _ln(x_ref[...], g_ref[...], b_ref[...]).astype(BF16)
        hb_ref[...] = hb
        lr_ref[...] = _dot(hb, wlr_ref[...]) + blr_ref[...]
        mq_ref[...] = (_dot(hb, wmq_ref[...]) + bmq_ref[...]).astype(BF16)

    proj_ref[...] = (_dot(hb_ref[...], w_ref[...]) + bias_ref[...]).astype(BF16)


def _inproj(x2, ln_g, ln_b, w_main, b_main, w_mq, b_mq, w_lr, b_lr):
    tm, tn = INPROJ_TM, INPROJ_TN
    nj = PROJ_W // tn
    row = lambda i, j: (i, 0)
    const = lambda i, j: (0, 0)
    outs = (
        jax.ShapeDtypeStruct((T, PROJ_W), BF16),
        jax.ShapeDtypeStruct((T, MQ_W), BF16),
        jax.ShapeDtypeStruct((T, LANES), F32),
    )
    return pl.pallas_call(
        _inproj_kernel,
        grid=(T // tm, nj),
        in_specs=[
            pl.BlockSpec((tm, D), row),
            pl.BlockSpec((1, D), const),
            pl.BlockSpec((1, D), const),
            pl.BlockSpec((D, tn), lambda i, j: (0, j)),
            pl.BlockSpec((1, tn), lambda i, j: (0, j)),
            pl.BlockSpec((D, MQ_W), const),
            pl.BlockSpec((1, MQ_W), const),
            pl.BlockSpec((D, LANES), const),
            pl.BlockSpec((1, LANES), const),
        ],
        out_specs=[
            pl.BlockSpec((tm, tn), lambda i, j: (i, j)),
            pl.BlockSpec((tm, MQ_W), row),
            pl.BlockSpec((tm, LANES), row),
        ],
        out_shape=outs,
        scratch_shapes=[pltpu.VMEM((tm, D), BF16)],
        compiler_params=pltpu.CompilerParams(
            dimension_semantics=("arbitrary", "arbitrary"),
            vmem_limit_bytes=48 * MIB),
        name="ln_inproj",
    )(x2, ln_g, ln_b, w_main, b_main, w_mq, b_mq, w_lr, b_lr)


GLA_BULK = 256
GLA_FIN = 512
GLA_NCH = SEQ // GLA_C
GLA_CPB = GLA_BULK // GLA_C
GLA_PIPE = 4


def _gla_kernel(q_ref, k_ref, v_ref, r_ref, lr_ref, wd_ref, bd_ref, g_ref, cs_ref, o_ref,
                acc_ref, qin_ref, kin_ref, kst_ref, dec_ref, u_ref, stf_ref, stb_ref):
    C = GLA_C
    G = GLA_BULK
    DK = GLA_DK
    NG = SEQ // G
    scale = DK ** -0.5
    ii = lax.broadcasted_iota(I32, (G, G), 0)
    jj = lax.broadcasted_iota(I32, (G, G), 1)
    same = (ii // C) == (jj // C)
    lower = jnp.logical_and(same, ii >= jj)
    upper = jnp.logical_and(same, ii <= jj)
    is_fwd = lax.broadcasted_iota(I32, (G, 2 * DK), 1) < DK
    chunk_of_row = lax.broadcasted_iota(I32, (G, DK), 0) // C

    def stage_a(gi):
        rows = pl.ds(pl.multiple_of(gi * G, G), G)
        z = _dot(jnp.concatenate(_split_bf16(lr_ref[rows, :]), axis=1), wd_ref[...]) + bd_ref[...]
        yield
        la = -(jnp.maximum(-z, 0.0) + jnp.log(1.0 + jnp.exp(-jnp.abs(z)))) * (1.0 / GLA_TAU)
        la_hi, la_lo = _split_bf16(la)
        pre2 = _dot(cs_ref[...], jnp.concatenate([la_hi, la_lo], axis=1))
        yield
        pre = pre2[:, :2 * DK] + pre2[:, 2 * DK:]
        blast = jnp.concatenate(
            [jnp.broadcast_to(pre[ci * C + C - 1:ci * C + C, :], (C, 2 * DK))
             for ci in range(GLA_CPB)], axis=0)
        b = jnp.where(is_fwd, pre, blast - pre + la)
        qf32 = q_ref[rows, :].astype(F32)
        kf32 = k_ref[rows, :].astype(F32)
        q2 = jnp.concatenate([qf32, qf32], axis=1)
        k2 = jnp.concatenate([kf32, kf32], axis=1)
        qin_ref[rows, :] = (q2 * (scale * jnp.exp(b))).astype(BF16)
        kin_ref[rows, :] = (k2 * jnp.exp(-b)).astype(BF16)
        kst_ref[rows, :] = (k2 * jnp.exp(blast - b)).astype(BF16)
        dec = jnp.exp(blast)
        for ci in range(GLA_CPB):
            dec_ref[pl.ds(gi * GLA_CPB + ci, 1), :] = dec[ci * C:ci * C + 1, :]

    def stage_b(gi):
        rows = pl.ds(pl.multiple_of(gi * G, G), G)
        qi = qin_ref[rows, :]
        ki = kin_ref[rows, :]
        ks = kst_ref[rows, :]
        vb = v_ref[rows, :]
        att_f = _dot_nt(qi[:, :DK], ki[:, :DK])
        att_b = _dot_nt(qi[:, DK:], ki[:, DK:])
        yield
        att = jnp.where(lower, att_f, 0.0) + jnp.where(upper, att_b, 0.0)
        acc_ref[rows, :] = _dot(att.astype(BF16), vb)
        yield
        ksb = jnp.concatenate(
            [jnp.where(chunk_of_row == ci, ks[:, d * DK:(d + 1) * DK], jnp.zeros((G, DK), BF16))
             for d in range(2) for ci in range(GLA_CPB)], axis=1)
        u = _dot_tn(vb, ksb)
        for d in range(2):
            for ci in range(GLA_CPB):
                col = (d * GLA_CPB + ci) * DK
                u_ref[d, gi * GLA_CPB + ci] = u[:, col:col + DK]

    P = GLA_PIPE
    _interleave(*[stage_a(j) for j in range(P)])

    def bulk(i, carry):
        g = P * i
        _interleave(*[st for j in range(P) for st in (stage_b(g - P + j), stage_a(g + j))])
        return carry

    lax.fori_loop(1, NG // P, bulk, 0)
    _interleave(*[stage_b(NG - P + j) for j in range(P)])

    stf_ref[...] = jnp.zeros_like(stf_ref)
    stb_ref[...] = jnp.zeros_like(stb_ref)

    def one(n, d, st_ref):
        lanes = slice(d * DK, (d + 1) * DK)
        rows = pl.ds(pl.multiple_of(n * C, C), C)
        st = st_ref[...]
        acc_ref[rows, :] += _dot_nt(qin_ref[rows, lanes], st.astype(BF16))
        st_ref[...] = st * dec_ref[pl.ds(n, 1), :][:, lanes] + u_ref[d, n]

    def step(i, carry):
        one(i, 0, stf_ref)
        one(GLA_NCH - 1 - i, 1, stb_ref)
        return carry

    lax.fori_loop(0, GLA_NCH, step, 0, unroll=16)

    def fin(gi, carry):
        rows = pl.ds(pl.multiple_of(gi * GLA_FIN, GLA_FIN), GLA_FIN)
        o = acc_ref[rows, :]
        o = o * lax.rsqrt(jnp.mean(o * o, axis=-1, keepdims=True) + RMS_EPS) * g_ref[...]
        rg = r_ref[rows, :].astype(F32)
        o_ref[rows, :] = (o * (rg * jax.nn.sigmoid(rg))).astype(BF16)
        return carry

    lax.fori_loop(0, SEQ // GLA_FIN, fin, 0)


def _gla(proj, lr, wd, bd, g):
    i = np.arange(GLA_BULK)
    cs = ((i[:, None] // GLA_C) == (i[None, :] // GLA_C)) & (i[:, None] >= i[None, :])
    cs = jnp.asarray(cs, dtype=F32).astype(BF16)
    v_blk = 1024 // GLA_DV
    return pl.pallas_call(
        _gla_kernel,
        grid=(BATCH, GLA_H),
        in_specs=[
            pl.BlockSpec((SEQ, GLA_DK), lambda b, h: (b, h)),
            pl.BlockSpec((SEQ, GLA_DK), lambda b, h: (b, GLA_H + h)),
            pl.BlockSpec((SEQ, GLA_DV), lambda b, h: (b, v_blk + h)),
            pl.BlockSpec((SEQ, GLA_DV), lambda b, h: (b, 2 * v_blk + h)),
            pl.BlockSpec((SEQ, LANES), lambda b, h: (b, 0)),
            pl.BlockSpec((2 * LANES, 2 * GLA_DK), lambda b, h: (0, h)),
            pl.BlockSpec((1, 2 * GLA_DK), lambda b, h: (0, h)),
            pl.BlockSpec((1, GLA_DV), lambda b, h: (0, 0)),
            pl.BlockSpec((GLA_BULK, GLA_BULK), lambda b, h: (0, 0)),
        ],
        out_specs=pl.BlockSpec((SEQ, GLA_DV), lambda b, h: (b, h)),
        out_shape=jax.ShapeDtypeStruct((T, GLA_H * GLA_DV), BF16),
        scratch_shapes=[
            pltpu.VMEM((SEQ, GLA_DV), F32),
            pltpu.VMEM((SEQ, 2 * GLA_DK), BF16),
            pltpu.VMEM((SEQ, 2 * GLA_DK), BF16),
            pltpu.VMEM((SEQ, 2 * GLA_DK), BF16),
            pltpu.VMEM((GLA_NCH, 2 * GLA_DK), F32),
            pltpu.VMEM((2, GLA_NCH, GLA_DV, GLA_DK), F32),
            pltpu.VMEM((GLA_DV, GLA_DK), F32),
            pltpu.VMEM((GLA_DV, GLA_DK), F32),
        ],
        compiler_params=pltpu.CompilerParams(
            dimension_semantics=("arbitrary", "arbitrary"),
            vmem_limit_bytes=58 * MIB),
        name="gla",
    )(proj, proj, proj, proj, lr, wd, bd, g, cs)


FFT1_S = 32
FFT1_SUB = 4
FFT1_ROWS = FFT_N2 * FFT1_S
FFT2_KB = 8


def _fft1_kernel(x_ref, g_ref, b_ref, w_ref, bias_ref, fbig_ref, cw_ref, sw_ref, o_ref):
    sh = FFT1_S // FFT1_SUB
    rows = FFT_N2 * sh
    res = {}

    def sub(h):
        xv = x_ref[:, h * sh:(h + 1) * sh, :].reshape(rows, D)
        hb = _ln(xv, g_ref[...], b_ref[...]).astype(BF16)
        fn = (_dot(hb, w_ref[...]) + bias_ref[...]).astype(BF16)
        yield
        a = _dot(fbig_ref[...], fn)
        yield
        ar = a[:rows]
        ai = a[rows:]
        cw = jnp.concatenate([cw_ref[h]] * (FN_W // LANES), axis=1)
        sw = jnp.concatenate([sw_ref[h]] * (FN_W // LANES), axis=1)
        res[h] = ((ar * cw + ai * sw).reshape(FFT_N2, sh, FN_W),
                  (ai * cw - ar * sw).reshape(FFT_N2, sh, FN_W))

    _interleave(*[sub(h) for h in range(FFT1_SUB)])
    for ri in range(2):
        o_ref[ri] = jnp.concatenate([res[h][ri] for h in range(FFT1_SUB)], axis=1).astype(BF16)


def _fft1(x4, ln_g, ln_b, w_fn, b_fn, fbig, cwt, swt):
    s = FFT1_S
    const = lambda b, j: (0, 0)
    return pl.pallas_call(
        _fft1_kernel,
        grid=(BATCH, FFT_N1 // s),
        in_specs=[
            pl.BlockSpec((None, FFT_N2, s, D), lambda b, j: (b, 0, j, 0)),
            pl.BlockSpec((1, D), const),
            pl.BlockSpec((1, D), const),
            pl.BlockSpec((D, FN_W), const),
            pl.BlockSpec((1, FN_W), const),
            pl.BlockSpec((2 * FFT1_ROWS // FFT1_SUB, FFT1_ROWS // FFT1_SUB), const),
            pl.BlockSpec((FFT1_SUB, FFT1_ROWS // FFT1_SUB, LANES), lambda b, j: (j, 0, 0)),
            pl.BlockSpec((FFT1_SUB, FFT1_ROWS // FFT1_SUB, LANES), lambda b, j: (j, 0, 0)),
        ],
        out_specs=pl.BlockSpec((None, 2, FFT_N2, s, FN_W), lambda b, j: (b, 0, 0, j, 0)),
        out_shape=jax.ShapeDtypeStruct((BATCH, 2, FFT_N2, FFT_N1, FN_W), BF16),
        compiler_params=pltpu.CompilerParams(
            dimension_semantics=("arbitrary", "arbitrary"),
            vmem_limit_bytes=40 * MIB),
        name="fft_stage1",
    )(x4, ln_g, ln_b, w_fn, b_fn, fbig, cwt, swt)


def _fft2_kernel(d_ref, f2_ref, o_ref):
    f2 = f2_ref[...]
    for kk in range(FFT2_KB):
        z = _dot(f2, jnp.concatenate([d_ref[0, kk], d_ref[1, kk]], axis=0))
        o_ref[0, kk] = z[:FFT_N1].astype(BF16)
        o_ref[1, kk] = z[FFT_N1:].astype(BF16)


def _fft2(dmat, f2):
    kb = FFT2_KB
    blk = (None, 2, kb, FFT_N1, FN_W)
    return pl.pallas_call(
        _fft2_kernel,
        grid=(BATCH, FFT_N2 // kb),
        in_specs=[
            pl.BlockSpec(blk, lambda b, j: (b, 0, j, 0, 0)),
            pl.BlockSpec((2 * FFT_N1, 2 * FFT_N1), lambda b, j: (0, 0)),
        ],
        out_specs=pl.BlockSpec(blk, lambda b, j: (b, 0, j, 0, 0)),
        out_shape=jax.ShapeDtypeStruct((BATCH, 2, FFT_N2, FFT_N1, FN_W), BF16),
        compiler_params=pltpu.CompilerParams(
            dimension_semantics=("arbitrary", "arbitrary")),
        name="fft_stage2",
    )(dmat, f2)


def _dft_tables(merge_tm):
    s = FFT1_S // FFT1_SUB
    n2 = np.arange(FFT_N2, dtype=np.float64)
    n1 = np.arange(FFT_N1, dtype=np.float64)
    th = 2.0 * np.pi * np.outer(n2, n2) / FFT_N2
    f1 = np.stack([np.cos(th), -np.sin(th)]) / math.sqrt(SEQ)
    fbig = np.einsum("rkn,st->rksnt", f1, np.eye(s)).reshape(2 * FFT_N2 * s, FFT_N2 * s)
    tw = 2.0 * np.pi * np.outer(n2, n1) / SEQ
    tw = tw.reshape(FFT_N2, FFT_N1 // s, s).transpose(1, 0, 2).reshape(FFT_N1 // s, FFT_N2 * s)
    cwt = np.broadcast_to(np.cos(tw)[:, :, None], tw.shape + (LANES,))
    swt = np.broadcast_to(np.sin(tw)[:, :, None], tw.shape + (LANES,))
    th1 = 2.0 * np.pi * np.outer(n1, n1) / FFT_N1
    c1, s1 = np.cos(th1), np.sin(th1)
    f2 = np.block([[c1, s1], [-s1, c1]])
    cc = np.arange(FN_GW, dtype=np.float64)
    thc = 2.0 * np.pi * np.outer(cc, cc) / FN_GW
    ccs = np.concatenate([np.cos(thc), np.sin(thc)], axis=0) / math.sqrt(FN_GW)
    k1n = merge_tm // FFT_N2
    r = np.arange(merge_tm)
    perm = np.zeros((merge_tm, merge_tm))
    perm[r, (r % FFT_N2) * k1n + r // FFT_N2] = 1.0
    as32 = lambda a: jnp.asarray(np.ascontiguousarray(a), dtype=F32)
    return (as32(fbig).astype(BF16), as32(cwt), as32(swt), as32(f2).astype(BF16),
            as32(ccs).astype(BF16), as32(perm).astype(BF16))


def _memkv_kernel(m_ref, g_ref, b_ref, w_ref, o_ref):
    mn = _ln(m_ref[...], g_ref[...], b_ref[...]).astype(BF16)
    o_ref[...] = _dot(mn, w_ref[...]).astype(BF16)


def _memkv(mem2, g, b, w):
    return pl.pallas_call(
        _memkv_kernel,
        grid=(BATCH,),
        in_specs=[
            pl.BlockSpec((MEM_LEN, D), lambda i: (i, 0)),
            pl.BlockSpec((1, D), lambda i: (0, 0)),
            pl.BlockSpec((1, D), lambda i: (0, 0)),
            pl.BlockSpec((D, 2 * MQ_W), lambda i: (0, 0)),
        ],
        out_specs=pl.BlockSpec((MEM_LEN, 2 * MQ_W), lambda i: (i, 0)),
        out_shape=jax.ShapeDtypeStruct((BATCH * MEM_LEN, 2 * MQ_W), BF16),
        compiler_params=pltpu.CompilerParams(dimension_semantics=("arbitrary",)),
        name="mem_kv",
    )(mem2, g, b, w)


MERGE_TM = 512
MERGE_K1 = MERGE_TM // FFT_N2


def _pack_bf16_pair(v):
    n = v.shape[1] // 2
    bits = lax.bitcast_convert_type(v.astype(BF16).astype(F32), U32)
    return (bits[:, n:] & jnp.uint32(0xFFFF0000)) | (bits[:, :n] >> 16)


def _unpack_bf16_pair(p):
    lo = lax.bitcast_convert_type(p << 16, F32)
    hi = lax.bitcast_convert_type(p & jnp.uint32(0xFFFF0000), F32)
    return lo, hi


def _merge_kernel(x_ref, og_ref, zr_ref, zi_ref, mq_ref, gt_ref, kv_ref,
                  lng_ref, lnb_ref, wg_ref, ccs_ref, perm_ref, wf_ref, wm_ref, wo_ref, bo_ref,
                  l1g_ref, l1b_ref, wr2_ref, wrh_ref, br_ref,
                  h1_ref, h1p_ref, eidx_ref, topw_ref, cnt_ref):
    tm = MERGE_TM
    y = {}

    def branch_fnet():
        zr = zr_ref[...].reshape(tm, FN_W)
        zi = zi_ref[...].reshape(tm, FN_W)
        ys = []
        for g in range(FN_G):
            sl = slice(g * FN_GW, (g + 1) * FN_GW)
            ys.append(_dot(jnp.concatenate([zr[:, sl], zi[:, sl]], axis=1), ccs_ref[...]))
        yield
        yp = _dot(perm_ref[...], jnp.concatenate(ys, axis=1).astype(BF16))
        yield
        y["fnet"] = _dot(yp.astype(BF16), wf_ref[...])

    def branch_mem():
        heads = [slice(hd * MEM_HD, (hd + 1) * MEM_HD) for hd in range(MEM_H)]
        ss = [_dot_nt(mq_ref[:, sl], kv_ref[:, sl]) for sl in heads]
        yield
        oms = []
        for hd, s in enumerate(ss):
            s = s * (MEM_HD ** -0.5)
            s = s - jnp.max(s, axis=-1, keepdims=True)
            p = jnp.exp(s)
            p = p * (1.0 / jnp.sum(p, axis=-1, keepdims=True))
            oms.append(_dot(p.astype(BF16),
                            kv_ref[:, MQ_W + hd * MEM_HD:MQ_W + (hd + 1) * MEM_HD]))
        yield
        y["mem"] = _dot(jnp.concatenate(oms, axis=1).astype(BF16), wm_ref[...])

    def branch_gla():
        y["gla"] = _dot(og_ref[...], wg_ref[...])
        yield

    _interleave(branch_fnet(), branch_mem(), branch_gla())

    def gate(c):
        return 0.5 + 0.5 * jnp.tanh(0.5 * gt_ref[:, c * D:(c + 1) * D].astype(F32))

    merged = gate(0) * y["gla"] + gate(1) * y["fnet"] + gate(2) * y["mem"]
    mix = _dot(merged.astype(BF16), wo_ref[...]) + bo_ref[...]
    h = _ln(x_ref[...], lng_ref[...], lnb_ref[...])
    h1 = _ln(DN_ALPHA * h + mix, l1g_ref[...], l1b_ref[...])
    h1_ref[...] = h1
    h1p_ref[...] = _pack_bf16_pair(h1)

    h_hi, h_lo = _split_bf16(h1)
    d2 = _dot(h_hi, wr2_ref[...])
    l = d2[:, :LANES] + d2[:, LANES:] + _dot(h_lo, wrh_ref[...]) + br_ref[...]
    lane = lax.broadcasted_iota(I32, (tm, LANES), 1)
    vals, idxs = [], []
    for _ in range(TOP_K):
        m = jnp.max(l, axis=-1, keepdims=True)
        idx = jnp.min(jnp.where(l == m, lane, LANES), axis=-1, keepdims=True)
        vals.append(m)
        idxs.append(idx)
        l = jnp.where(lane == idx, -jnp.inf, l)
    es = [jnp.exp(v - vals[0]) for v in vals]
    den = es[0] + es[1] + es[2] + es[3]
    eo = jnp.zeros((tm, LANES), I32)
    wo = jnp.zeros((tm, LANES), F32)
    chosen = jnp.zeros((tm, LANES), F32)
    for k in range(TOP_K):
        eo = jnp.where(lane == k, idxs[k], eo)
        wo = jnp.where(lane == k, es[k] / den, wo)
        chosen = chosen + jnp.where(lane == idxs[k], 1.0, 0.0)
    eidx_ref[...] = eo
    topw_ref[...] = wo

    @pl.when(pl.program_id(0) == 0)
    def _():
        cnt_ref[...] = jnp.zeros_like(cnt_ref)

    cnt_ref[...] += jnp.broadcast_to(jnp.sum(chosen, axis=0, keepdims=True), cnt_ref.shape)


def _merge(x2, og, z, mq, gates, kv, lng, lnb, wg, ccs, perm, wf, wm, wo, bo, l1g, l1b,
           wr2, wrh, br):
    tm = MERGE_TM
    per_b = SEQ // tm
    row = lambda i: (i, 0)
    const = lambda i: (0, 0)
    zblk = (None, None, FFT_N2, MERGE_K1, FN_W)
    outs = (
        jax.ShapeDtypeStruct((T, D), F32),
        jax.ShapeDtypeStruct((T, D // 2), U32),
        jax.ShapeDtypeStruct((T, LANES), I32),
        jax.ShapeDtypeStruct((T, LANES), F32),
        jax.ShapeDtypeStruct((8, LANES), F32),
    )
    return pl.pallas_call(
        _merge_kernel,
        grid=(T // tm,),
        in_specs=[
            pl.BlockSpec((tm, D), row),
            pl.BlockSpec((tm, D), row),
            pl.BlockSpec(zblk, lambda i: (i // per_b, 0, 0, i % per_b, 0)),
            pl.BlockSpec(zblk, lambda i: (i // per_b, 1, 0, i % per_b, 0)),
            pl.BlockSpec((tm, MQ_W), row),
            pl.BlockSpec((tm, 3 * D), lambda i: (i, 1)),
            pl.BlockSpec((MEM_LEN, 2 * MQ_W), lambda i: (i // per_b, 0)),
            pl.BlockSpec((1, D), const), pl.BlockSpec((1, D), const),
            pl.BlockSpec((D, D), const),
            pl.BlockSpec((2 * FN_GW, FN_GW), const),
            pl.BlockSpec((tm, tm), const),
            pl.BlockSpec((FN_W, D), const),
            pl.BlockSpec((MQ_W, D), const),
            pl.BlockSpec((D, D), const),
            pl.BlockSpec((1, D), const),
            pl.BlockSpec((1, D), const), pl.BlockSpec((1, D), const),
            pl.BlockSpec((D, 2 * LANES), const),
            pl.BlockSpec((D, LANES), const),
            pl.BlockSpec((1, LANES), const),
        ],
        out_specs=[
            pl.BlockSpec((tm, D), row),
            pl.BlockSpec((tm, D // 2), row),
            pl.BlockSpec((tm, LANES), row),
            pl.BlockSpec((tm, LANES), row),
            pl.BlockSpec((8, LANES), const),
        ],
        out_shape=outs,
        compiler_params=pltpu.CompilerParams(
            dimension_semantics=("arbitrary",),
            vmem_limit_bytes=58 * MIB),
        name="merge_ln1_router",
    )(x2, og, z, z, mq, gates, kv, lng, lnb, wg, ccs, perm, wf, wm, wo, bo, l1g, l1b,
      wr2, wrh, br)


PLAN_TP = 1024


def _expert_onehots(e, lane):
    onehots = [lane == e[:, k:k + 1] for k in range(TOP_K)]
    mf = jnp.zeros(lane.shape, F32)
    for oh in onehots:
        mf = mf + jnp.where(oh, 1.0, 0.0)
    return onehots, mf


def _plan_kernel(e_ref, tot_ref, dest_ref, cnt_ref, off_ref):
    i = pl.program_id(0)
    tp = PLAN_TP
    lane = lax.broadcasted_iota(I32, (tp, LANES), 1)
    onehots, mf = _expert_onehots(e_ref[...], lane)

    @pl.when(i == 0)
    def _():
        tot = tot_ref[0:1, :]
        padded = jnp.floor((tot + (MOE_BM - 1)) * (1.0 / MOE_BM)) * MOE_BM
        lane1 = lax.broadcasted_iota(I32, (1, LANES), 1)
        inc = padded
        for s in (1, 2, 4, 8, 16, 32, 64):
            inc = inc + jnp.where(lane1 >= s, pltpu.roll(inc, s, 1), 0.0)
        off_ref[...] = inc - padded
        cnt_ref[...] = jnp.zeros_like(cnt_ref)

    ri = lax.broadcasted_iota(I32, (tp, tp), 0)
    ci = lax.broadcasted_iota(I32, (tp, tp), 1)
    ltri = jnp.where(ri > ci, 1.0, 0.0).astype(BF16)
    rank = _dot(ltri, mf.astype(BF16)) + cnt_ref[...] + off_ref[...]
    out = jnp.zeros((tp, LANES), I32)
    for k in range(TOP_K):
        dk = jnp.sum(jnp.where(onehots[k], rank, 0.0), axis=-1, keepdims=True)
        out = jnp.where(lane == k, dk.astype(I32), out)
    dest_ref[...] = out
    cnt_ref[...] += jnp.sum(mf, axis=0, keepdims=True)


def _plan(eidx, cnt):
    tp = PLAN_TP
    return pl.pallas_call(
        _plan_kernel,
        grid=(T // tp,),
        in_specs=[pl.BlockSpec((tp, LANES), lambda i: (i, 0)),
                  pl.BlockSpec((8, LANES), lambda i: (0, 0))],
        out_specs=pl.BlockSpec((tp, LANES), lambda i: (i, 0)),
        out_shape=jax.ShapeDtypeStruct((T, LANES), I32),
        scratch_shapes=[pltpu.VMEM((1, LANES), F32), pltpu.VMEM((1, LANES), F32)],
        compiler_params=pltpu.CompilerParams(dimension_semantics=("arbitrary",)),
        name="route_plan",
    )(eidx, cnt)


MOE_BM = 512
MOE_NW = A_ROWS // MOE_BM + N_EXP
XS_ROWS = MOE_NW * MOE_BM
MOE_FF_SLICES = 2


def _expert_kernel(we_ref, wb_ref, wv_ref, wfe_ref, wsl_ref, wnx_ref,
                   x_ref, wgu_hbm, bgu_ref, wdn_hbm, bdn_ref, o_ref,
                   wgu_f32, wdn_f32, wgu_bf, wdn_bf, sems):
    w = pl.program_id(0)
    e = we_ref[w]

    def weight_copies(expert, slot):
        return (pltpu.make_async_copy(wgu_hbm.at[expert], wgu_f32.at[slot], sems.at[slot, 0]),
                pltpu.make_async_copy(wdn_hbm.at[expert], wdn_f32.at[slot], sems.at[slot, 1]))

    @pl.when(w == 0)
    def _():
        for cp_ in weight_copies(e, 0):
            cp_.start()

    first = wfe_ref[w] == 1
    slot = wsl_ref[w]

    @pl.when(first)
    def _():
        for cp_ in weight_copies(e, slot):
            cp_.wait()
        nxt = wnx_ref[w]

        @pl.when(nxt >= 0)
        def _():
            for cp_ in weight_copies(nxt, 1 - slot):
                cp_.start()

    hw = D_FF // MOE_FF_SLICES
    col_slices = [(slice(hf * hw, (hf + 1) * hw), slice(D_FF + hf * hw, D_FF + (hf + 1) * hw))
                  for hf in range(MOE_FF_SLICES)]

    def cast_weights():
        for gc, uc in col_slices:
            wgu_bf[:, gc] = wgu_f32[slot, :, gc].astype(BF16)
            wgu_bf[:, uc] = wgu_f32[slot, :, uc].astype(BF16)
            yield
        for hf in range(MOE_FF_SLICES):
            rs = slice(hf * hw, (hf + 1) * hw)
            wdn_bf[rs, :] = wdn_f32[slot, rs, :].astype(BF16)
            yield

    def ffn(rows):
        xlo, xhi = _unpack_bf16_pair(x_ref[rows, :])
        xb = jnp.concatenate([xlo.astype(BF16), xhi.astype(BF16)], axis=1)
        bgu = bgu_ref[pl.ds(e, 1), :]
        gus = []
        for gc, uc in col_slices:
            gus.append((_dot(xb, wgu_bf[:, gc]) + bgu[:, gc], _dot(xb, wgu_bf[:, uc]) + bgu[:, uc]))
            yield
        out = bdn_ref[pl.ds(e, 1), :]
        for hf, (g, u) in enumerate(gus):
            gate = jnp.minimum(g, SW_LIMIT)
            up = jnp.clip(u, -SW_LIMIT, SW_LIMIT)
            act = (up + 1.0) * (gate * jax.nn.sigmoid(SW_ALPHA * gate))
            out = out + _dot(act.astype(BF16), wdn_bf[hf * hw:(hf + 1) * hw, :])
            if hf + 1 < MOE_FF_SLICES:
                yield
        o_ref[rows, :] = _pack_bf16_pair(out)

    for code, rows in ((1, slice(None)), (2, slice(0, MOE_BM // 2))):
        @pl.when(jnp.logical_and(wv_ref[w] == code, first))
        def _(rows=rows):
            _interleave(cast_weights(), ffn(rows))

        @pl.when(jnp.logical_and(wv_ref[w] == code, jnp.logical_not(first)))
        def _(rows=rows):
            _interleave(ffn(rows))

    @pl.when(wv_ref[w] == 2)
    def _():
        o_ref[MOE_BM // 2:, :] = jnp.zeros((MOE_BM // 2, D // 2), U32)

    @pl.when(wv_ref[w] == 0)
    def _():
        o_ref[...] = jnp.zeros_like(o_ref)


def _experts(meta, xs, w_gu, b_gu, w_down, b_down):
    return pl.pallas_call(
        _expert_kernel,
        grid_spec=pltpu.PrefetchScalarGridSpec(
            num_scalar_prefetch=len(meta),
            grid=(MOE_NW,),
            in_specs=[
                pl.BlockSpec((MOE_BM, D // 2), lambda w, we, wb, *_: (wb[w], 0)),
                pl.BlockSpec(memory_space=pl.ANY),
                pl.BlockSpec((N_EXP, 2 * D_FF), lambda w, *_: (0, 0)),
                pl.BlockSpec(memory_space=pl.ANY),
                pl.BlockSpec((N_EXP, D), lambda w, *_: (0, 0)),
            ],
            out_specs=pl.BlockSpec((MOE_BM, D // 2), lambda w, *_: (w, 0)),
            scratch_shapes=[
                pltpu.VMEM((2, D, 2 * D_FF), F32),
                pltpu.VMEM((2, D_FF, D), F32),
                pltpu.VMEM((D, 2 * D_FF), BF16),
                pltpu.VMEM((D_FF, D), BF16),
                pltpu.SemaphoreType.DMA((2, 2)),
            ],
        ),
        out_shape=jax.ShapeDtypeStruct((XS_ROWS, D // 2), U32),
        compiler_params=pltpu.CompilerParams(
            dimension_semantics=("arbitrary",),
            vmem_limit_bytes=56 * MIB),
        name="moe_experts",
    )(*meta, xs, w_gu, b_gu, w_down, b_down)


def _work_items(counts):
    n_e = (counts + MOE_BM - 1) // MOE_BM
    item_end = jnp.cumsum(n_e)
    total = item_end[-1]
    w = jnp.arange(MOE_NW, dtype=I32)
    valid = w < total
    wc = jnp.minimum(w, total - 1)
    e_w = jnp.sum((item_end[None, :] <= wc[:, None]).astype(I32), axis=1)
    e_w = jnp.minimum(e_w, N_EXP - 1)
    rows_here = counts[e_w] - (wc - (item_end - n_e)[e_w]) * MOE_BM
    valid = jnp.where(valid, jnp.where(rows_here <= MOE_BM // 2, 2, 1), 0)
    prev_e = jnp.concatenate([jnp.full((1,), -1, I32), e_w[:-1]])
    fe = (e_w != prev_e).astype(I32)
    slot = (jnp.cumsum(fe) - 1) % 2
    first_at = jnp.where(fe == 1, w, MOE_NW)
    next_first = jnp.concatenate([lax.cummin(first_at, reverse=True)[1:],
                                  jnp.full((1,), MOE_NW, I32)])
    nxt = jnp.where(next_first < MOE_NW, e_w[jnp.minimum(next_first, MOE_NW - 1)], -1)
    return tuple(a.astype(I32) for a in (e_w, wc, valid, fe, slot, nxt))


COMB_TM = 512
SC_CORES = 2
SC_SUBCORES = 16
SC_WORKERS = SC_CORES * SC_SUBCORES
SC_CH = 64
COMB_GROUPS = 4
COMB_TG = T // COMB_GROUPS
SC_ROWS_PER_W = COMB_TG * TOP_K // SC_WORKERS
SC_NCH = SC_ROWS_PER_W // SC_CH


def _sc_gather(table, idx3):
    mesh = plsc.VectorSubcoreMesh(core_axis_name="c", subcore_axis_name="s")

    @functools.partial(
        pl.kernel, mesh=mesh,
        out_type=jax.ShapeDtypeStruct((COMB_TG * TOP_K, D // 2), U32),
        scratch_types=[
            pltpu.VMEM((SC_NCH, SC_CH), I32),
            pltpu.VMEM((2, SC_CH, D // 2), U32),
            pltpu.SemaphoreType.DMA((2,)),
            pltpu.SemaphoreType.DMA((2,)),
        ],
    )
    def k(table_hbm, idx_hbm, out_hbm, idx_v, rows_v, gsem, psem):
        wid = lax.axis_index("s") * SC_CORES + lax.axis_index("c")
        base = wid * SC_ROWS_PER_W
        pltpu.sync_copy(idx_hbm.at[wid], idx_v)

        def gather(j, b):
            return pltpu.make_async_copy(table_hbm.at[idx_v.at[j]], rows_v.at[b], gsem.at[b])

        def put(j, b):
            return pltpu.make_async_copy(rows_v.at[b], out_hbm.at[pl.ds(base + j * SC_CH, SC_CH)],
                                         psem.at[b])

        gather(0, 0).start()

        @pl.loop(0, SC_NCH, step=2)
        def _(j0):
            for b in range(2):
                j = j0 + b

                @pl.when(j + 1 < SC_NCH)
                def _():
                    @pl.when(j >= 1)
                    def _():
                        put(j - 1, 1 - b).wait()
                    gather(j + 1, 1 - b).start()

                gather(j, b).wait()
                put(j, b).start()

        put(SC_NCH - 2, 0).wait()
        put(SC_NCH - 1, 1).wait()

    return k(table, idx3)


SCD_TOK_PER_W = T // SC_WORKERS
SCD_NCH = SCD_TOK_PER_W // SC_CH


def _sc_dispatch(h1p, idx4):
    mesh = plsc.VectorSubcoreMesh(core_axis_name="c", subcore_axis_name="s")

    @functools.partial(
        pl.kernel, mesh=mesh,
        out_type=jax.ShapeDtypeStruct((XS_ROWS, D // 2), U32),
        scratch_types=[
            pltpu.VMEM((SCD_NCH * TOP_K, SC_CH), I32),
            pltpu.VMEM((2, SC_CH, D // 2), U32),
            pltpu.SemaphoreType.DMA((2,)),
            pltpu.SemaphoreType.DMA((2,)),
        ],
    )
    def k(h_hbm, idx_hbm, xs_hbm, idx_v, rows_v, gsem, psem):
        wid = lax.axis_index("s") * SC_CORES + lax.axis_index("c")
        base = wid * SCD_TOK_PER_W
        pltpu.sync_copy(idx_hbm.at[wid], idx_v)

        def get(c, b):
            return pltpu.make_async_copy(h_hbm.at[pl.ds(base + c * SC_CH, SC_CH)], rows_v.at[b],
                                         gsem.at[b])

        def puts(c, b):
            return [pltpu.make_async_copy(rows_v.at[b], xs_hbm.at[idx_v.at[c * TOP_K + kk]],
                                          psem.at[b]) for kk in range(TOP_K)]

        get(0, 0).start()

        @pl.loop(0, SCD_NCH, step=2)
        def _(c0):
            for b in range(2):
                c = c0 + b

                @pl.when(c + 1 < SCD_NCH)
                def _():
                    @pl.when(c >= 1)
                    def _():
                        for cp_ in puts(c - 1, 1 - b):
                            cp_.wait()
                    get(c + 1, 1 - b).start()

                get(c, b).wait()
                for cp_ in puts(c, b):
                    cp_.start()

        for cp_ in puts(SCD_NCH - 2, 0) + puts(SCD_NCH - 1, 1):
            cp_.wait()

    return k(h1p, idx4)


def _combine_dense_kernel(g_ref, h1_ref, tw_ref, lg_ref, lb_ref, o_ref):
    tw = tw_ref[...]
    ylo = jnp.zeros((COMB_TM, D // 2), F32)
    yhi = jnp.zeros((COMB_TM, D // 2), F32)
    for k in range(TOP_K):
        lo, hi = _unpack_bf16_pair(g_ref[k])
        wk = tw[:, k:k + 1]
        ylo = ylo + lo * wk
        yhi = yhi + hi * wk
    ff = jnp.concatenate([ylo, yhi], axis=1)
    o_ref[...] = _ln(DN_ALPHA * h1_ref[...] + ff, lg_ref[...], lb_ref[...])


def _combine_dense(g4, h1, topw, g, b, group):
    tm = COMB_TM
    t0 = group * (COMB_TG // tm)
    return pl.pallas_call(
        _combine_dense_kernel,
        grid=(COMB_TG // tm,),
        in_specs=[
            pl.BlockSpec((TOP_K, tm, D // 2), lambda i: (0, i, 0)),
            pl.BlockSpec((tm, D), lambda i: (t0 + i, 0)),
            pl.BlockSpec((tm, LANES), lambda i: (t0 + i, 0)),
            pl.BlockSpec((1, D), lambda i: (0, 0)),
            pl.BlockSpec((1, D), lambda i: (0, 0)),
        ],
        out_specs=pl.BlockSpec((tm, D), lambda i: (t0 + i, 0)),
        out_shape=jax.ShapeDtypeStruct((T, D), F32),
        input_output_aliases={1: 0},
        compiler_params=pltpu.CompilerParams(
            dimension_semantics=("arbitrary",),
            vmem_limit_bytes=40 * MIB),
        name="moe_combine_dense_ln2",
    )(g4, h1, topw, g, b)


def _pad_cols(a, n):
    return jnp.pad(a, ((0, 0), (0, n - a.shape[1])))


def kernel(x, mem, ln_in_g, ln_in_b, ln_mem_g, ln_mem_b, w_in, b_in, w_decay_f, b_decay_f,
           w_decay_b, b_decay_b, gla_norm_g, w_br_gla, w_br_fnet, w_br_mem, w_mem_kv, w_out,
           b_out, ln1_g, ln1_b, w_router, b_router, w_gu, b_gu, w_down, b_down, ln2_g, ln2_b):
    assert x.shape == (BATCH, SEQ, D) and w_in.shape[0] == 1
    row = lambda a: a.reshape(1, -1)
    x2 = x.reshape(T, D)
    w_in0, b_in0 = w_in[0], b_in[0]
    c_lr, c_fn, c_mq, c_gt = 3072, 3072 + 2 * GLA_LR, 3104 + FN_W, 3104 + FN_W + MQ_W
    w_main = jnp.concatenate([w_in0[:, :c_lr], w_in0[:, c_gt:]], axis=1).astype(BF16)
    b_main = row(jnp.concatenate([b_in0[:c_lr], b_in0[c_gt:]]))
    w_lr = _pad_cols(w_in0[:, c_lr:c_fn], LANES).astype(BF16)
    b_lr = _pad_cols(row(b_in0[c_lr:c_fn]), LANES)
    w_mq = w_in0[:, c_mq:c_gt].astype(BF16)
    b_mq = row(b_in0[c_mq:c_gt])
    w_fn = w_in0[:, c_fn:c_mq].astype(BF16)
    b_fn = row(b_in0[c_fn:c_mq])
    lng, lnb = row(ln_in_g), row(ln_in_b)

    proj, mq, lr = _inproj(x2, lng, lnb, w_main, b_main, w_mq, b_mq, w_lr, b_lr)

    zpad = jnp.zeros((LANES - 2 * GLA_LR, GLA_H * GLA_DK), F32)
    zlr = jnp.zeros((GLA_LR, GLA_H * GLA_DK), F32)
    wdf = jnp.concatenate([w_decay_f[0], zlr, zpad], axis=0).reshape(LANES, GLA_H, GLA_DK)
    wdb = jnp.concatenate([zlr, w_decay_b[0], zpad], axis=0).reshape(LANES, GLA_H, GLA_DK)
    wd = jnp.concatenate([wdf, wdb], axis=2).reshape(LANES, GLA_H * 2 * GLA_DK).astype(BF16)
    wd = jnp.concatenate([wd, wd], axis=0)
    bd = jnp.concatenate([b_decay_f[0].reshape(GLA_H, GLA_DK),
                          b_decay_b[0].reshape(GLA_H, GLA_DK)], axis=1).reshape(1, -1)
    og = _gla(proj, lr, wd, bd, row(gla_norm_g[0]))

    fbig, cwt, swt, f2, ccs, perm = _dft_tables(MERGE_TM)
    x4 = x.reshape(BATCH, FFT_N2, FFT_N1, D)
    z = _fft2(_fft1(x4, lng, lnb, w_fn, b_fn, fbig, cwt, swt), f2)

    kv = _memkv(mem.reshape(BATCH * MEM_LEN, D), row(ln_mem_g), row(ln_mem_b),
                w_mem_kv[0].astype(BF16))

    w_r = _pad_cols(w_router[0], LANES)
    wr_hi = w_r.astype(BF16)
    wr_lo = (w_r - wr_hi.astype(F32)).astype(BF16)
    b_r = jnp.concatenate([row(b_router[0]),
                           jnp.full((1, LANES - N_EXP), NEG_BIG, F32)], axis=1)
    h1, h1p, eidx, topw, cnt = _merge(
        x2, og, z, mq, proj, kv, lng, lnb,
        w_br_gla[0].astype(BF16), ccs, perm, w_br_fnet[0].astype(BF16),
        w_br_mem[0].astype(BF16), w_out[0].astype(BF16), row(b_out[0]),
        row(ln1_g[0]), row(ln1_b[0]), jnp.concatenate([wr_hi, wr_lo], axis=1), wr_hi, b_r)

    dest = _plan(eidx, cnt)
    counts = cnt[0, :N_EXP].astype(I32)
    dest_k = dest[:, :TOP_K]
    idx4 = dest_k.reshape(SC_WORKERS, SCD_NCH, SC_CH, TOP_K).transpose(0, 1, 3, 2)
    xs = _sc_dispatch(h1p, idx4.reshape(SC_WORKERS, SCD_NCH * TOP_K, SC_CH))
    ys = _experts(_work_items(counts), xs, w_gu[0], b_gu[0], w_down[0], b_down[0])
    out = h1
    for grp in range(COMB_GROUPS):
        dest_g = dest_k[grp * COMB_TG:(grp + 1) * COMB_TG].T.reshape(SC_WORKERS, SC_NCH, SC_CH)
        g4 = _sc_gather(ys, dest_g).reshape(TOP_K, COMB_TG, D // 2)
        out = _combine_dense(g4, out, topw, row(ln2_g[0]), row(ln2_b[0]), grp)
    return out.reshape(BATCH, SEQ, D)
```

```python
import functools
import math

import numpy as np
import jax
import jax.numpy as jnp
from jax import lax
from jax.experimental import pallas as pl
from jax.experimental.pallas import tpu as pltpu
from jax.experimental.pallas import tpu_sc as plsc

F32 = jnp.float32
BF16 = jnp.bfloat16
I32 = jnp.int32
U32 = jnp.uint32

D = 1024
BATCH = 4
SEQ = 4096
T = BATCH * SEQ
GLA_H = 4
GLA_DK = 128
GLA_DV = 256
GLA_LR = 16
GLA_TAU = 16.0
GLA_C = 64
FN_G = 4
FN_GW = 128
FN_W = 512
MEM_LEN = 256
MEM_H = 4
MEM_HD = 128
MQ_W = 512
N_EXP = 32
TOP_K = 4
D_FF = 1024
SW_LIMIT = 7.0
SW_ALPHA = 1.702
LN_EPS = 1e-5
RMS_EPS = 1e-6
DN_ALPHA = 2.0 ** 0.25
A_ROWS = T * TOP_K

FFT_N1 = 128
FFT_N2 = 32

LANES = 128
NEG_BIG = -1e30
MIB = 1024 * 1024


def _ln(x, g, b):
    mu = jnp.mean(x, axis=-1, keepdims=True)
    xc = x - mu
    var = jnp.mean(xc * xc, axis=-1, keepdims=True)
    return xc * lax.rsqrt(var + LN_EPS) * g + b


def _dot(a, b):
    return jnp.dot(a, b, preferred_element_type=F32)


def _dot_nt(a, b):
    return lax.dot_general(a, b, (((1,), (1,)), ((), ())), preferred_element_type=F32)


def _dot_tn(a, b):
    return lax.dot_general(a, b, (((0,), (0,)), ((), ())), preferred_element_type=F32)


def _interleave(*stages):
    live = list(stages)
    while live:
        for st in list(live):
            try:
                next(st)
            except StopIteration:
                live.remove(st)


def _split_bf16(a):
    hi = a.astype(BF16)
    return hi, (a - hi.astype(F32)).astype(BF16)


INPROJ_TM = 1024
INPROJ_TN = 3072


PROJ_W = 6 * 1024


def _inproj_kernel(x_ref, g_ref, b_ref, w_ref, bias_ref, wmq_ref, bmq_ref, wlr_ref, blr_ref,
                   proj_ref, mq_ref, lr_ref, hb_ref):
    @pl.when(pl.program_id(1) == 0)
    def _():
        hb = _ln(x_ref[...], g_ref[...], b_ref[...]).astype(BF16)
        hb_ref[...] = hb
        lr_ref[...] = _dot(hb, wlr_ref[...]) + blr_ref[...]
        mq_ref[...] = (_dot(hb, wmq_ref[...]) + bmq_ref[...]).astype(BF16)

    proj_ref[...] = (_dot(hb_ref[...], w_ref[...]) + bias_ref[...]).astype(BF16)


def _inproj(x2, ln_g, ln_b, w_main, b_main, w_mq, b_mq, w_lr, b_lr):
    tm, tn = INPROJ_TM, INPROJ_TN
    nj = PROJ_W // tn
    row = lambda i, j: (i, 0)
    const = lambda i, j: (0, 0)
    outs = (
        jax.ShapeDtypeStruct((T, PROJ_W), BF16),
        jax.ShapeDtypeStruct((T, MQ_W), BF16),
        jax.ShapeDtypeStruct((T, LANES), F32),
    )
    return pl.pallas_call(
        _inproj_kernel,
        grid=(T // tm, nj),
        in_specs=[
            pl.BlockSpec((tm, D), row),
            pl.BlockSpec((1, D), const),
            pl.BlockSpec((1, D), const),
            pl.BlockSpec((D, tn), lambda i, j: (0, j)),
            pl.BlockSpec((1, tn), lambda i, j: (0, j)),
            pl.BlockSpec((D, MQ_W), const),
            pl.BlockSpec((1, MQ_W), const),
            pl.BlockSpec((D, LANES), const),
            pl.BlockSpec((1, LANES), const),
        ],
        out_specs=[
            pl.BlockSpec((tm, tn), lambda i, j: (i, j)),
            pl.BlockSpec((tm, MQ_W), row),
            pl.BlockSpec((tm, LANES), row),
        ],
        out_shape=outs,
        scratch_shapes=[pltpu.VMEM((tm, D), BF16)],
        compiler_params=pltpu.CompilerParams(
            dimension_semantics=("arbitrary", "arbitrary"),
            vmem_limit_bytes=48 * MIB),
        name="ln_inproj",
    )(x2, ln_g, ln_b, w_main, b_main, w_mq, b_mq, w_lr, b_lr)


GLA_BULK = 256
GLA_FIN = 512
GLA_NCH = SEQ // GLA_C
GLA_CPB = GLA_BULK // GLA_C
GLA_PIPE = 4


def _gla_kernel(q_ref, k_ref, v_ref, r_ref, lr_ref, wd_ref, bd_ref, g_ref, cs_ref, o_ref,
                acc_ref, qin_ref, kin_ref, kst_ref, dec_ref, u_ref, stf_ref, stb_ref):
    C = GLA_C
    G = GLA_BULK
    DK = GLA_DK
    NG = SEQ // G
    scale = DK ** -0.5
    ii = lax.broadcasted_iota(I32, (G, G), 0)
    jj = lax.broadcasted_iota(I32, (G, G), 1)
    same = (ii // C) == (jj // C)
    lower = jnp.logical_and(same, ii >= jj)
    upper = jnp.logical_and(same, ii <= jj)
    is_fwd = lax.broadcasted_iota(I32, (G, 2 * DK), 1) < DK
    chunk_of_row = lax.broadcasted_iota(I32, (G, DK), 0) // C

    def stage_a(gi):
        rows = pl.ds(pl.multiple_of(gi * G, G), G)
        z = _dot(jnp.concatenate(_split_bf16(lr_ref[rows, :]), axis=1), wd_ref[...]) + bd_ref[...]
        yield
        la = -(jnp.maximum(-z, 0.0) + jnp.log(1.0 + jnp.exp(-jnp.abs(z)))) * (1.0 / GLA_TAU)
        la_hi, la_lo = _split_bf16(la)
        pre2 = _dot(cs_ref[...], jnp.concatenate([la_hi, la_lo], axis=1))
        yield
        pre = pre2[:, :2 * DK] + pre2[:, 2 * DK:]
        blast = jnp.concatenate(
            [jnp.broadcast_to(pre[ci * C + C - 1:ci * C + C, :], (C, 2 * DK))
             for ci in range(GLA_CPB)], axis=0)
        b = jnp.where(is_fwd, pre, blast - pre + la)
        qf32 = q_ref[rows, :].astype(F32)
        kf32 = k_ref[rows, :].astype(F32)
        q2 = jnp.concatenate([qf32, qf32], axis=1)
        k2 = jnp.concatenate([kf32, kf32], axis=1)
        qin_ref[rows, :] = (q2 * (scale * jnp.exp(b))).astype(BF16)
        kin_ref[rows, :] = (k2 * jnp.exp(-b)).astype(BF16)
        kst_ref[rows, :] = (k2 * jnp.exp(blast - b)).astype(BF16)
        dec = jnp.exp(blast)
        for ci in range(GLA_CPB):
            dec_ref[pl.ds(gi * GLA_CPB + ci, 1), :] = dec[ci * C:ci * C + 1, :]

    def stage_b(gi):
        rows = pl.ds(pl.multiple_of(gi * G, G), G)
        qi = qin_ref[rows, :]
        ki = kin_ref[rows, :]
        ks = kst_ref[rows, :]
        vb = v_ref[rows, :]
        att_f = _dot_nt(qi[:, :DK], ki[:, :DK])
        att_b = _dot_nt(qi[:, DK:], ki[:, DK:])
        yield
        att = jnp.where(lower, att_f, 0.0) + jnp.where(upper, att_b, 0.0)
        acc_ref[rows, :] = _dot(att.astype(BF16), vb)
        yield
        ksb = jnp.concatenate(
            [jnp.where(chunk_of_row == ci, ks[:, d * DK:(d + 1) * DK], jnp.zeros((G, DK), BF16))
             for d in range(2) for ci in range(GLA_CPB)], axis=1)
        u = _dot_tn(vb, ksb)
        for d in range(2):
            for ci in range(GLA_CPB):
                col = (d * GLA_CPB + ci) * DK
                u_ref[d, gi * GLA_CPB + ci] = u[:, col:col + DK]

    P = GLA_PIPE
    _interleave(*[stage_a(j) for j in range(P)])

    def bulk(i, carry):
        g = P * i
        _interleave(*[st for j in range(P) for st in (stage_b(g - P + j), stage_a(g + j))])
        return carry

    lax.fori_loop(1, NG // P, bulk, 0)
    _interleave(*[stage_b(NG - P + j) for j in range(P)])

    stf_ref[...] = jnp.zeros_like(stf_ref)
    stb_ref[...] = jnp.zeros_like(stb_ref)

    def one(n, d, st_ref):
        lanes = slice(d * DK, (d + 1) * DK)
        rows = pl.ds(pl.multiple_of(n * C, C), C)
        st = st_ref[...]
        acc_ref[rows, :] += _dot_nt(qin_ref[rows, lanes], st.astype(BF16))
        st_ref[...] = st * dec_ref[pl.ds(n, 1), :][:, lanes] + u_ref[d, n]

    def step(i, carry):
        one(i, 0, stf_ref)
        one(GLA_NCH - 1 - i, 1, stb_ref)
        return carry

    lax.fori_loop(0, GLA_NCH, step, 0, unroll=16)

    def fin(gi, carry):
        rows = pl.ds(pl.multiple_of(gi * GLA_FIN, GLA_FIN), GLA_FIN)
        o = acc_ref[rows, :]
        o = o * lax.rsqrt(jnp.mean(o * o, axis=-1, keepdims=True) + RMS_EPS) * g_ref[...]
        rg = r_ref[rows, :].astype(F32)
        o_ref[rows, :] = (o * (rg * jax.nn.sigmoid(rg))).astype(BF16)
        return carry

    lax.fori_loop(0, SEQ // GLA_FIN, fin, 0)


def _gla(proj, lr, wd, bd, g):
    i = np.arange(GLA_BULK)
    cs = ((i[:, None] // GLA_C) == (i[None, :] // GLA_C)) & (i[:, None] >= i[None, :])
    cs = jnp.asarray(cs, dtype=F32).astype(BF16)
    v_blk = 1024 // GLA_DV
    return pl.pallas_call(
        _gla_kernel,
        grid=(BATCH, GLA_H),
        in_specs=[
            pl.BlockSpec((SEQ, GLA_DK), lambda b, h: (b, h)),
            pl.BlockSpec((SEQ, GLA_DK), lambda b, h: (b, GLA_H + h)),
            pl.BlockSpec((SEQ, GLA_DV), lambda b, h: (b, v_blk + h)),
            pl.BlockSpec((SEQ, GLA_DV), lambda b, h: (b, 2 * v_blk + h)),
            pl.BlockSpec((SEQ, LANES), lambda b, h: (b, 0)),
            pl.BlockSpec((2 * LANES, 2 * GLA_DK), lambda b, h: (0, h)),
            pl.BlockSpec((1, 2 * GLA_DK), lambda b, h: (0, h)),
            pl.BlockSpec((1, GLA_DV), lambda b, h: (0, 0)),
            pl.BlockSpec((GLA_BULK, GLA_BULK), lambda b, h: (0, 0)),
        ],
        out_specs=pl.BlockSpec((SEQ, GLA_DV), lambda b, h: (b, h)),
        out_shape=jax.ShapeDtypeStruct((T, GLA_H * GLA_DV), BF16),
        scratch_shapes=[
            pltpu.VMEM((SEQ, GLA_DV), F32),
            pltpu.VMEM((SEQ, 2 * GLA_DK), BF16),
            pltpu.VMEM((SEQ, 2 * GLA_DK), BF16),
            pltpu.VMEM((SEQ, 2 * GLA_DK), BF16),
            pltpu.VMEM((GLA_NCH, 2 * GLA_DK), F32),
            pltpu.VMEM((2, GLA_NCH, GLA_DV, GLA_DK), F32),
            pltpu.VMEM((GLA_DV, GLA_DK), F32),
            pltpu.VMEM((GLA_DV, GLA_DK), F32),
        ],
        compiler_params=pltpu.CompilerParams(
            dimension_semantics=("arbitrary", "arbitrary"),
            vmem_limit_bytes=58 * MIB),
        name="gla",
    )(proj, proj, proj, proj, lr, wd, bd, g, cs)


FFT1_S = 64
FFT1_SUB = 8
FFT1_ROWS = FFT_N2 * FFT1_S
FFT2_KB = 8


def _fft1_kernel(x_ref, g_ref, b_ref, w_ref, bias_ref, fbig_ref, cw_ref, sw_ref, o_ref):
    sh = FFT1_S // FFT1_SUB
    rows = FFT_N2 * sh
    res = {}

    def sub(h):
        xv = x_ref[:, h * sh:(h + 1) * sh, :].reshape(rows, D)
        hb = _ln(xv, g_ref[...], b_ref[...]).astype(BF16)
        fn = (_dot(hb, w_ref[...]) + bias_ref[...]).astype(BF16)
        yield
        a = _dot(fbig_ref[...], fn)
        yield
        ar = a[:rows]
        ai = a[rows:]
        cw = jnp.concatenate([cw_ref[h]] * (FN_W // LANES), axis=1)
        sw = jnp.concatenate([sw_ref[h]] * (FN_W // LANES), axis=1)
        res[h] = ((ar * cw + ai * sw).reshape(FFT_N2, sh, FN_W),
                  (ai * cw - ar * sw).reshape(FFT_N2, sh, FN_W))

    _interleave(*[sub(h) for h in range(FFT1_SUB)])
    for ri in range(2):
        o_ref[ri] = jnp.concatenate([res[h][ri] for h in range(FFT1_SUB)], axis=1).astype(BF16)


def _fft1(x4, ln_g, ln_b, w_fn, b_fn, fbig, cwt, swt):
    s = FFT1_S
    const = lambda b, j: (0, 0)
    return pl.pallas_call(
        _fft1_kernel,
        grid=(BATCH, FFT_N1 // s),
        in_specs=[
            pl.BlockSpec((None, FFT_N2, s, D), lambda b, j: (b, 0, j, 0)),
            pl.BlockSpec((1, D), const),
            pl.BlockSpec((1, D), const),
            pl.BlockSpec((D, FN_W), const),
            pl.BlockSpec((1, FN_W), const),
            pl.BlockSpec((2 * FFT1_ROWS // FFT1_SUB, FFT1_ROWS // FFT1_SUB), const),
            pl.BlockSpec((FFT1_SUB, FFT1_ROWS // FFT1_SUB, LANES), lambda b, j: (j, 0, 0)),
            pl.BlockSpec((FFT1_SUB, FFT1_ROWS // FFT1_SUB, LANES), lambda b, j: (j, 0, 0)),
        ],
        out_specs=pl.BlockSpec((None, 2, FFT_N2, s, FN_W), lambda b, j: (b, 0, 0, j, 0)),
        out_shape=jax.ShapeDtypeStruct((BATCH, 2, FFT_N2, FFT_N1, FN_W), BF16),
        compiler_params=pltpu.CompilerParams(
            dimension_semantics=("arbitrary", "arbitrary"),
            vmem_limit_bytes=56 * MIB),
        name="fft_stage1",
    )(x4, ln_g, ln_b, w_fn, b_fn, fbig, cwt, swt)


def _fft2_kernel(d_ref, f2_ref, o_ref):
    f2 = f2_ref[...]
    for kk in range(FFT2_KB):
        z = _dot(f2, jnp.concatenate([d_ref[0, kk], d_ref[1, kk]], axis=0))
        o_ref[0, kk] = z[:FFT_N1].astype(BF16)
        o_ref[1, kk] = z[FFT_N1:].astype(BF16)


def _fft2(dmat, f2):
    kb = FFT2_KB
    blk = (None, 2, kb, FFT_N1, FN_W)
    return pl.pallas_call(
        _fft2_kernel,
        grid=(BATCH, FFT_N2 // kb),
        in_specs=[
            pl.BlockSpec(blk, lambda b, j: (b, 0, j, 0, 0)),
            pl.BlockSpec((2 * FFT_N1, 2 * FFT_N1), lambda b, j: (0, 0)),
        ],
        out_specs=pl.BlockSpec(blk, lambda b, j: (b, 0, j, 0, 0)),
        out_shape=jax.ShapeDtypeStruct((BATCH, 2, FFT_N2, FFT_N1, FN_W), BF16),
        compiler_params=pltpu.CompilerParams(
            dimension_semantics=("arbitrary", "arbitrary")),
        name="fft_stage2",
    )(dmat, f2)


def _dft_tables(merge_tm):
    s = FFT1_S // FFT1_SUB
    n2 = np.arange(FFT_N2, dtype=np.float64)
    n1 = np.arange(FFT_N1, dtype=np.float64)
    th = 2.0 * np.pi * np.outer(n2, n2) / FFT_N2
    f1 = np.stack([np.cos(th), -np.sin(th)]) / math.sqrt(SEQ)
    fbig = np.einsum("rkn,st->rksnt", f1, np.eye(s)).reshape(2 * FFT_N2 * s, FFT_N2 * s)
    tw = 2.0 * np.pi * np.outer(n2, n1) / SEQ
    tw = tw.reshape(FFT_N2, FFT_N1 // s, s).transpose(1, 0, 2).reshape(FFT_N1 // s, FFT_N2 * s)
    cwt = np.broadcast_to(np.cos(tw)[:, :, None], tw.shape + (LANES,))
    swt = np.broadcast_to(np.sin(tw)[:, :, None], tw.shape + (LANES,))
    th1 = 2.0 * np.pi * np.outer(n1, n1) / FFT_N1
    c1, s1 = np.cos(th1), np.sin(th1)
    f2 = np.block([[c1, s1], [-s1, c1]])
    cc = np.arange(FN_GW, dtype=np.float64)
    thc = 2.0 * np.pi * np.outer(cc, cc) / FN_GW
    ccs = np.concatenate([np.cos(thc), np.sin(thc)], axis=0) / math.sqrt(FN_GW)
    k1n = merge_tm // FFT_N2
    r = np.arange(merge_tm)
    perm = np.zeros((merge_tm, merge_tm))
    perm[r, (r % FFT_N2) * k1n + r // FFT_N2] = 1.0
    as32 = lambda a: jnp.asarray(np.ascontiguousarray(a), dtype=F32)
    return (as32(fbig).astype(BF16), as32(cwt), as32(swt), as32(f2).astype(BF16),
            as32(ccs).astype(BF16), as32(perm).astype(BF16))


def _memkv_kernel(m_ref, g_ref, b_ref, w_ref, o_ref):
    mn = _ln(m_ref[...], g_ref[...], b_ref[...]).astype(BF16)
    o_ref[...] = _dot(mn, w_ref[...]).astype(BF16)


def _memkv(mem2, g, b, w):
    return pl.pallas_call(
        _memkv_kernel,
        grid=(BATCH,),
        in_specs=[
            pl.BlockSpec((MEM_LEN, D), lambda i: (i, 0)),
            pl.BlockSpec((1, D), lambda i: (0, 0)),
            pl.BlockSpec((1, D), lambda i: (0, 0)),
            pl.BlockSpec((D, 2 * MQ_W), lambda i: (0, 0)),
        ],
        out_specs=pl.BlockSpec((MEM_LEN, 2 * MQ_W), lambda i: (i, 0)),
        out_shape=jax.ShapeDtypeStruct((BATCH * MEM_LEN, 2 * MQ_W), BF16),
        compiler_params=pltpu.CompilerParams(dimension_semantics=("arbitrary",)),
        name="mem_kv",
    )(mem2, g, b, w)


MERGE_TM = 512
MERGE_K1 = MERGE_TM // FFT_N2


def _pack_bf16_pair(v):
    n = v.shape[1] // 2
    bits = lax.bitcast_convert_type(v.astype(BF16).astype(F32), U32)
    return (bits[:, n:] & jnp.uint32(0xFFFF0000)) | (bits[:, :n] >> 16)


def _unpack_bf16_pair(p):
    lo = lax.bitcast_convert_type(p << 16, F32)
    hi = lax.bitcast_convert_type(p & jnp.uint32(0xFFFF0000), F32)
    return lo, hi


def _merge_kernel(x_ref, og_ref, zr_ref, zi_ref, mq_ref, gt_ref, kv_ref,
                  lng_ref, lnb_ref, wg_ref, ccs_ref, perm_ref, wf_ref, wm_ref, wo_ref, bo_ref,
                  l1g_ref, l1b_ref, wr2_ref, wrh_ref, br_ref,
                  h1_ref, h1p_ref, eidx_ref, topw_ref, cnt_ref):
    tm = MERGE_TM
    y = {}

    def branch_fnet():
        zr = zr_ref[...].reshape(tm, FN_W)
        zi = zi_ref[...].reshape(tm, FN_W)
        ys = []
        for g in range(FN_G):
            sl = slice(g * FN_GW, (g + 1) * FN_GW)
            ys.append(_dot(jnp.concatenate([zr[:, sl], zi[:, sl]], axis=1), ccs_ref[...]))
        yield
        yp = _dot(perm_ref[...], jnp.concatenate(ys, axis=1).astype(BF16))
        yield
        y["fnet"] = _dot(yp.astype(BF16), wf_ref[...])

    def branch_mem():
        heads = [slice(hd * MEM_HD, (hd + 1) * MEM_HD) for hd in range(MEM_H)]
        ss = [_dot_nt(mq_ref[:, sl], kv_ref[:, sl]) for sl in heads]
        yield
        oms = []
        for hd, s in enumerate(ss):
            s = s * (MEM_HD ** -0.5)
            s = s - jnp.max(s, axis=-1, keepdims=True)
            p = jnp.exp(s)
            p = p * (1.0 / jnp.sum(p, axis=-1, keepdims=True))
            oms.append(_dot(p.astype(BF16),
                            kv_ref[:, MQ_W + hd * MEM_HD:MQ_W + (hd + 1) * MEM_HD]))
        yield
        y["mem"] = _dot(jnp.concatenate(oms, axis=1).astype(BF16), wm_ref[...])

    def branch_gla():
        y["gla"] = _dot(og_ref[...], wg_ref[...])
        yield

    _interleave(branch_fnet(), branch_mem(), branch_gla())

    def gate(c):
        return 0.5 + 0.5 * jnp.tanh(0.5 * gt_ref[:, c * D:(c + 1) * D].astype(F32))

    merged = gate(0) * y["gla"] + gate(1) * y["fnet"] + gate(2) * y["mem"]
    mix = _dot(merged.astype(BF16), wo_ref[...]) + bo_ref[...]
    h = _ln(x_ref[...], lng_ref[...], lnb_ref[...])
    h1 = _ln(DN_ALPHA * h + mix, l1g_ref[...], l1b_ref[...])
    h1_ref[...] = h1
    h1p_ref[...] = _pack_bf16_pair(h1)

    h_hi, h_lo = _split_bf16(h1)
    d2 = _dot(h_hi, wr2_ref[...])
    l = d2[:, :LANES] + d2[:, LANES:] + _dot(h_lo, wrh_ref[...]) + br_ref[...]
    lane = lax.broadcasted_iota(I32, (tm, LANES), 1)
    vals, idxs = [], []
    for _ in range(TOP_K):
        m = jnp.max(l, axis=-1, keepdims=True)
        idx = jnp.min(jnp.where(l == m, lane, LANES), axis=-1, keepdims=True)
        vals.append(m)
        idxs.append(idx)
        l = jnp.where(lane == idx, -jnp.inf, l)
    es = [jnp.exp(v - vals[0]) for v in vals]
    den = es[0] + es[1] + es[2] + es[3]
    eo = jnp.zeros((tm, LANES), I32)
    wo = jnp.zeros((tm, LANES), F32)
    chosen = jnp.zeros((tm, LANES), F32)
    for k in range(TOP_K):
        eo = jnp.where(lane == k, idxs[k], eo)
        wo = jnp.where(lane == k, es[k] / den, wo)
        chosen = chosen + jnp.where(lane == idxs[k], 1.0, 0.0)
    eidx_ref[...] = eo
    topw_ref[...] = wo

    @pl.when(pl.program_id(0) == 0)
    def _():
        cnt_ref[...] = jnp.zeros_like(cnt_ref)

    cnt_ref[...] += jnp.broadcast_to(jnp.sum(chosen, axis=0, keepdims=True), cnt_ref.shape)


def _merge(x2, og, z, mq, gates, kv, lng, lnb, wg, ccs, perm, wf, wm, wo, bo, l1g, l1b,
           wr2, wrh, br):
    tm = MERGE_TM
    per_b = SEQ // tm
    row = lambda i: (i, 0)
    const = lambda i: (0, 0)
    zblk = (None, None, FFT_N2, MERGE_K1, FN_W)
    outs = (
        jax.ShapeDtypeStruct((T, D), F32),
        jax.ShapeDtypeStruct((T, D // 2), U32),
        jax.ShapeDtypeStruct((T, LANES), I32),
        jax.ShapeDtypeStruct((T, LANES), F32),
        jax.ShapeDtypeStruct((8, LANES), F32),
    )
    return pl.pallas_call(
        _merge_kernel,
        grid=(T // tm,),
        in_specs=[
            pl.BlockSpec((tm, D), row),
            pl.BlockSpec((tm, D), row),
            pl.BlockSpec(zblk, lambda i: (i // per_b, 0, 0, i % per_b, 0)),
            pl.BlockSpec(zblk, lambda i: (i // per_b, 1, 0, i % per_b, 0)),
            pl.BlockSpec((tm, MQ_W), row),
            pl.BlockSpec((tm, 3 * D), lambda i: (i, 1)),
            pl.BlockSpec((MEM_LEN, 2 * MQ_W), lambda i: (i // per_b, 0)),
            pl.BlockSpec((1, D), const), pl.BlockSpec((1, D), const),
            pl.BlockSpec((D, D), const),
            pl.BlockSpec((2 * FN_GW, FN_GW), const),
            pl.BlockSpec((tm, tm), const),
            pl.BlockSpec((FN_W, D), const),
            pl.BlockSpec((MQ_W, D), const),
            pl.BlockSpec((D, D), const),
            pl.BlockSpec((1, D), const),
            pl.BlockSpec((1, D), const), pl.BlockSpec((1, D), const),
            pl.BlockSpec((D, 2 * LANES), const),
            pl.BlockSpec((D, LANES), const),
            pl.BlockSpec((1, LANES), const),
        ],
        out_specs=[
            pl.BlockSpec((tm, D), row),
            pl.BlockSpec((tm, D // 2), row),
            pl.BlockSpec((tm, LANES), row),
            pl.BlockSpec((tm, LANES), row),
            pl.BlockSpec((8, LANES), const),
        ],
        out_shape=outs,
        compiler_params=pltpu.CompilerParams(
            dimension_semantics=("arbitrary",),
            vmem_limit_bytes=58 * MIB),
        name="merge_ln1_router",
    )(x2, og, z, z, mq, gates, kv, lng, lnb, wg, ccs, perm, wf, wm, wo, bo, l1g, l1b,
      wr2, wrh, br)


PLAN_TP = 1024


def _expert_onehots(e, lane):
    onehots = [lane == e[:, k:k + 1] for k in range(TOP_K)]
    mf = jnp.zeros(lane.shape, F32)
    for oh in onehots:
        mf = mf + jnp.where(oh, 1.0, 0.0)
    return onehots, mf


def _plan_kernel(e_ref, tot_ref, dest_ref, cnt_ref, off_ref):
    i = pl.program_id(0)
    tp = PLAN_TP
    lane = lax.broadcasted_iota(I32, (tp, LANES), 1)
    onehots, mf = _expert_onehots(e_ref[...], lane)

    @pl.when(i == 0)
    def _():
        tot = tot_ref[0:1, :]
        padded = jnp.floor((tot + (MOE_BM - 1)) * (1.0 / MOE_BM)) * MOE_BM
        lane1 = lax.broadcasted_iota(I32, (1, LANES), 1)
        inc = padded
        for s in (1, 2, 4, 8, 16, 32, 64):
            inc = inc + jnp.where(lane1 >= s, pltpu.roll(inc, s, 1), 0.0)
        off_ref[...] = inc - padded
        cnt_ref[...] = jnp.zeros_like(cnt_ref)

    ri = lax.broadcasted_iota(I32, (tp, tp), 0)
    ci = lax.broadcasted_iota(I32, (tp, tp), 1)
    ltri = jnp.where(ri > ci, 1.0, 0.0).astype(BF16)
    rank = _dot(ltri, mf.astype(BF16)) + cnt_ref[...] + off_ref[...]
    out = jnp.zeros((tp, LANES), I32)
    for k in range(TOP_K):
        dk = jnp.sum(jnp.where(onehots[k], rank, 0.0), axis=-1, keepdims=True)
        out = jnp.where(lane == k, dk.astype(I32), out)
    dest_ref[...] = out
    cnt_ref[...] += jnp.sum(mf, axis=0, keepdims=True)


def _plan(eidx, cnt):
    tp = PLAN_TP
    return pl.pallas_call(
        _plan_kernel,
        grid=(T // tp,),
        in_specs=[pl.BlockSpec((tp, LANES), lambda i: (i, 0)),
                  pl.BlockSpec((8, LANES), lambda i: (0, 0))],
        out_specs=pl.BlockSpec((tp, LANES), lambda i: (i, 0)),
        out_shape=jax.ShapeDtypeStruct((T, LANES), I32),
        scratch_shapes=[pltpu.VMEM((1, LANES), F32), pltpu.VMEM((1, LANES), F32)],
        compiler_params=pltpu.CompilerParams(dimension_semantics=("arbitrary",)),
        name="route_plan",
    )(eidx, cnt)


MOE_BM = 512
MOE_NW = A_ROWS // MOE_BM + N_EXP
XS_ROWS = MOE_NW * MOE_BM
MOE_FF_SLICES = 2


def _expert_kernel(we_ref, wb_ref, wv_ref, wfe_ref, wsl_ref, wnx_ref,
                   x_ref, wgu_hbm, bgu_ref, wdn_hbm, bdn_ref, o_ref,
                   wgu_f32, wdn_f32, wgu_bf, wdn_bf, sems):
    w = pl.program_id(0)
    e = we_ref[w]

    def weight_copies(expert, slot):
        return (pltpu.make_async_copy(wgu_hbm.at[expert], wgu_f32.at[slot], sems.at[slot, 0]),
                pltpu.make_async_copy(wdn_hbm.at[expert], wdn_f32.at[slot], sems.at[slot, 1]))

    @pl.when(w == 0)
    def _():
        for cp_ in weight_copies(e, 0):
            cp_.start()

    first = wfe_ref[w] == 1
    slot = wsl_ref[w]

    @pl.when(first)
    def _():
        for cp_ in weight_copies(e, slot):
            cp_.wait()
        nxt = wnx_ref[w]

        @pl.when(nxt >= 0)
        def _():
            for cp_ in weight_copies(nxt, 1 - slot):
                cp_.start()

    hw = D_FF // MOE_FF_SLICES
    col_slices = [(slice(hf * hw, (hf + 1) * hw), slice(D_FF + hf * hw, D_FF + (hf + 1) * hw))
                  for hf in range(MOE_FF_SLICES)]

    def cast_weights():
        for gc, uc in col_slices:
            wgu_bf[:, gc] = wgu_f32[slot, :, gc].astype(BF16)
            wgu_bf[:, uc] = wgu_f32[slot, :, uc].astype(BF16)
            yield
        for hf in range(MOE_FF_SLICES):
            rs = slice(hf * hw, (hf + 1) * hw)
            wdn_bf[rs, :] = wdn_f32[slot, rs, :].astype(BF16)
            yield

    def ffn(rows):
        xlo, xhi = _unpack_bf16_pair(x_ref[rows, :])
        xb = jnp.concatenate([xlo.astype(BF16), xhi.astype(BF16)], axis=1)
        bgu = bgu_ref[pl.ds(e, 1), :]
        gus = []
        for gc, uc in col_slices:
            gus.append((_dot(xb, wgu_bf[:, gc]) + bgu[:, gc], _dot(xb, wgu_bf[:, uc]) + bgu[:, uc]))
            yield
        out = bdn_ref[pl.ds(e, 1), :]
        for hf, (g, u) in enumerate(gus):
            gate = jnp.minimum(g, SW_LIMIT)
            up = jnp.clip(u, -SW_LIMIT, SW_LIMIT)
            act = (up + 1.0) * (gate * jax.nn.sigmoid(SW_ALPHA * gate))
            out = out + _dot(act.astype(BF16), wdn_bf[hf * hw:(hf + 1) * hw, :])
            if hf + 1 < MOE_FF_SLICES:
                yield
        o_ref[rows, :] = _pack_bf16_pair(out)

    for code, rows in ((1, slice(None)), (2, slice(0, MOE_BM // 2))):
        @pl.when(jnp.logical_and(wv_ref[w] == code, first))
        def _(rows=rows):
            _interleave(cast_weights(), ffn(rows))

        @pl.when(jnp.logical_and(wv_ref[w] == code, jnp.logical_not(first)))
        def _(rows=rows):
            _interleave(ffn(rows))

    @pl.when(wv_ref[w] == 2)
    def _():
        o_ref[MOE_BM // 2:, :] = jnp.zeros((MOE_BM // 2, D // 2), U32)

    @pl.when(wv_ref[w] == 0)
    def _():
        o_ref[...] = jnp.zeros_like(o_ref)


def _experts(meta, xs, w_gu, b_gu, w_down, b_down):
    return pl.pallas_call(
        _expert_kernel,
        grid_spec=pltpu.PrefetchScalarGridSpec(
            num_scalar_prefetch=len(meta),
            grid=(MOE_NW,),
            in_specs=[
                pl.BlockSpec((MOE_BM, D // 2), lambda w, we, wb, *_: (wb[w], 0)),
                pl.BlockSpec(memory_space=pl.ANY),
                pl.BlockSpec((N_EXP, 2 * D_FF), lambda w, *_: (0, 0)),
                pl.BlockSpec(memory_space=pl.ANY),
                pl.BlockSpec((N_EXP, D), lambda w, *_: (0, 0)),
            ],
            out_specs=pl.BlockSpec((MOE_BM, D // 2), lambda w, *_: (w, 0)),
            scratch_shapes=[
                pltpu.VMEM((2, D, 2 * D_FF), F32),
                pltpu.VMEM((2, D_FF, D), F32),
                pltpu.VMEM((D, 2 * D_FF), BF16),
                pltpu.VMEM((D_FF, D), BF16),
                pltpu.SemaphoreType.DMA((2, 2)),
            ],
        ),
        out_shape=jax.ShapeDtypeStruct((XS_ROWS, D // 2), U32),
        compiler_params=pltpu.CompilerParams(
            dimension_semantics=("arbitrary",),
            vmem_limit_bytes=56 * MIB),
        name="moe_experts",
    )(*meta, xs, w_gu, b_gu, w_down, b_down)


def _work_items(counts):
    n_e = (counts + MOE_BM - 1) // MOE_BM
    item_end = jnp.cumsum(n_e)
    total = item_end[-1]
    w = jnp.arange(MOE_NW, dtype=I32)
    valid = w < total
    wc = jnp.minimum(w, total - 1)
    e_w = jnp.sum((item_end[None, :] <= wc[:, None]).astype(I32), axis=1)
    e_w = jnp.minimum(e_w, N_EXP - 1)
    rows_here = counts[e_w] - (wc - (item_end - n_e)[e_w]) * MOE_BM
    valid = jnp.where(valid, jnp.where(rows_here <= MOE_BM // 2, 2, 1), 0)
    prev_e = jnp.concatenate([jnp.full((1,), -1, I32), e_w[:-1]])
    fe = (e_w != prev_e).astype(I32)
    slot = (jnp.cumsum(fe) - 1) % 2
    first_at = jnp.where(fe == 1, w, MOE_NW)
    next_first = jnp.concatenate([lax.cummin(first_at, reverse=True)[1:],
                                  jnp.full((1,), MOE_NW, I32)])
    nxt = jnp.where(next_first < MOE_NW, e_w[jnp.minimum(next_first, MOE_NW - 1)], -1)
    return tuple(a.astype(I32) for a in (e_w, wc, valid, fe, slot, nxt))


COMB_TM = 512
SC_CORES = 2
SC_SUBCORES = 16
SC_WORKERS = SC_CORES * SC_SUBCORES
SC_CH = 64
COMB_GROUPS = 4
COMB_TG = T // COMB_GROUPS
SC_ROWS_PER_W = COMB_TG * TOP_K // SC_WORKERS
SC_NCH = SC_ROWS_PER_W // SC_CH


def _sc_gather(table, idx3):
    mesh = plsc.VectorSubcoreMesh(core_axis_name="c", subcore_axis_name="s")

    @functools.partial(
        pl.kernel, mesh=mesh,
        out_type=jax.ShapeDtypeStruct((COMB_TG * TOP_K, D // 2), U32),
        scratch_types=[
            pltpu.VMEM((SC_NCH, SC_CH), I32),
            pltpu.VMEM((2, SC_CH, D // 2), U32),
            pltpu.SemaphoreType.DMA((2,)),
            pltpu.SemaphoreType.DMA((2,)),
        ],
    )
    def k(table_hbm, idx_hbm, out_hbm, idx_v, rows_v, gsem, psem):
        wid = lax.axis_index("s") * SC_CORES + lax.axis_index("c")
        base = wid * SC_ROWS_PER_W
        pltpu.sync_copy(idx_hbm.at[wid], idx_v)

        def gather(j, b):
            return pltpu.make_async_copy(table_hbm.at[idx_v.at[j]], rows_v.at[b], gsem.at[b])

        def put(j, b):
            return pltpu.make_async_copy(rows_v.at[b], out_hbm.at[pl.ds(base + j * SC_CH, SC_CH)],
                                         psem.at[b])

        gather(0, 0).start()

        @pl.loop(0, SC_NCH, step=2)
        def _(j0):
            for b in range(2):
                j = j0 + b

                @pl.when(j + 1 < SC_NCH)
                def _():
                    @pl.when(j >= 1)
                    def _():
                        put(j - 1, 1 - b).wait()
                    gather(j + 1, 1 - b).start()

                gather(j, b).wait()
                put(j, b).start()

        put(SC_NCH - 2, 0).wait()
        put(SC_NCH - 1, 1).wait()

    return k(table, idx3)


SCD_TOK_PER_W = T // SC_WORKERS
SCD_NCH = SCD_TOK_PER_W // SC_CH


def _sc_dispatch(h1p, idx4):
    mesh = plsc.VectorSubcoreMesh(core_axis_name="c", subcore_axis_name="s")

    @functools.partial(
        pl.kernel, mesh=mesh,
        out_type=jax.ShapeDtypeStruct((XS_ROWS, D // 2), U32),
        scratch_types=[
            pltpu.VMEM((SCD_NCH * TOP_K, SC_CH), I32),
            pltpu.VMEM((2, SC_CH, D // 2), U32),
            pltpu.SemaphoreType.DMA((2,)),
            pltpu.SemaphoreType.DMA((2,)),
        ],
    )
    def k(h_hbm, idx_hbm, xs_hbm, idx_v, rows_v, gsem, psem):
        wid = lax.axis_index("s") * SC_CORES + lax.axis_index("c")
        base = wid * SCD_TOK_PER_W
        pltpu.sync_copy(idx_hbm.at[wid], idx_v)

        def get(c, b):
            return pltpu.make_async_copy(h_hbm.at[pl.ds(base + c * SC_CH, SC_CH)], rows_v.at[b],
                                         gsem.at[b])

        def puts(c, b):
            return [pltpu.make_async_copy(rows_v.at[b], xs_hbm.at[idx_v.at[c * TOP_K + kk]],
                                          psem.at[b]) for kk in range(TOP_K)]

        get(0, 0).start()

        @pl.loop(0, SCD_NCH, step=2)
        def _(c0):
            for b in range(2):
                c = c0 + b

                @pl.when(c + 1 < SCD_NCH)
                def _():
                    @pl.when(c >= 1)
                    def _():
                        for cp_ in puts(c - 1, 1 - b):
                            cp_.wait()
                    get(c + 1, 1 - b).start()

                get(c, b).wait()
                for cp_ in puts(c, b):
                    cp_.start()

        for cp_ in puts(SCD_NCH - 2, 0) + puts(SCD_NCH - 1, 1):
            cp_.wait()

    return k(h1p, idx4)


def _combine_dense_kernel(g_ref, h1_ref, tw_ref, lg_ref, lb_ref, o_ref):
    tw = tw_ref[...]
    ylo = jnp.zeros((COMB_TM, D // 2), F32)
    yhi = jnp.zeros((COMB_TM, D // 2), F32)
    for k in range(TOP_K):
        lo, hi = _unpack_bf16_pair(g_ref[k])
        wk = tw[:, k:k + 1]
        ylo = ylo + lo * wk
        yhi = yhi + hi * wk
    ff = jnp.concatenate([ylo, yhi], axis=1)
    o_ref[...] = _ln(DN_ALPHA * h1_ref[...] + ff, lg_ref[...], lb_ref[...])


def _combine_dense(g4, h1, topw, g, b, group):
    tm = COMB_TM
    t0 = group * (COMB_TG // tm)
    return pl.pallas_call(
        _combine_dense_kernel,
        grid=(COMB_TG // tm,),
        in_specs=[
            pl.BlockSpec((TOP_K, tm, D // 2), lambda i: (0, i, 0)),
            pl.BlockSpec((tm, D), lambda i: (t0 + i, 0)),
            pl.BlockSpec((tm, LANES), lambda i: (t0 + i, 0)),
            pl.BlockSpec((1, D), lambda i: (0, 0)),
            pl.BlockSpec((1, D), lambda i: (0, 0)),
        ],
        out_specs=pl.BlockSpec((tm, D), lambda i: (t0 + i, 0)),
        out_shape=jax.ShapeDtypeStruct((T, D), F32),
        input_output_aliases={1: 0},
        compiler_params=pltpu.CompilerParams(
            dimension_semantics=("arbitrary",),
            vmem_limit_bytes=40 * MIB),
        name="moe_combine_dense_ln2",
    )(g4, h1, topw, g, b)


def _pad_cols(a, n):
    return jnp.pad(a, ((0, 0), (0, n - a.shape[1])))


def kernel(x, mem, ln_in_g, ln_in_b, ln_mem_g, ln_mem_b, w_in, b_in, w_decay_f, b_decay_f,
           w_decay_b, b_decay_b, gla_norm_g, w_br_gla, w_br_fnet, w_br_mem, w_mem_kv, w_out,
           b_out, ln1_g, ln1_b, w_router, b_router, w_gu, b_gu, w_down, b_down, ln2_g, ln2_b):
    assert x.shape == (BATCH, SEQ, D) and w_in.shape[0] == 1
    row = lambda a: a.reshape(1, -1)
    x2 = x.reshape(T, D)
    w_in0, b_in0 = w_in[0], b_in[0]
    c_lr, c_fn, c_mq, c_gt = 3072, 3072 + 2 * GLA_LR, 3104 + FN_W, 3104 + FN_W + MQ_W
    w_main = jnp.concatenate([w_in0[:, :c_lr], w_in0[:, c_gt:]], axis=1).astype(BF16)
    b_main = row(jnp.concatenate([b_in0[:c_lr], b_in0[c_gt:]]))
    w_lr = _pad_cols(w_in0[:, c_lr:c_fn], LANES).astype(BF16)
    b_lr = _pad_cols(row(b_in0[c_lr:c_fn]), LANES)
    w_mq = w_in0[:, c_mq:c_gt].astype(BF16)
    b_mq = row(b_in0[c_mq:c_gt])
    w_fn = w_in0[:, c_fn:c_mq].astype(BF16)
    b_fn = row(b_in0[c_fn:c_mq])
    lng, lnb = row(ln_in_g), row(ln_in_b)

    proj, mq, lr = _inproj(x2, lng, lnb, w_main, b_main, w_mq, b_mq, w_lr, b_lr)

    zpad = jnp.zeros((LANES - 2 * GLA_LR, GLA_H * GLA_DK), F32)
    zlr = jnp.zeros((GLA_LR, GLA_H * GLA_DK), F32)
    wdf = jnp.concatenate([w_decay_f[0], zlr, zpad], axis=0).reshape(LANES, GLA_H, GLA_DK)
    wdb = jnp.concatenate([zlr, w_decay_b[0], zpad], axis=0).reshape(LANES, GLA_H, GLA_DK)
    wd = jnp.concatenate([wdf, wdb], axis=2).reshape(LANES, GLA_H * 2 * GLA_DK).astype(BF16)
    wd = jnp.concatenate([wd, wd], axis=0)
    bd = jnp.concatenate([b_decay_f[0].reshape(GLA_H, GLA_DK),
                          b_decay_b[0].reshape(GLA_H, GLA_DK)], axis=1).reshape(1, -1)
    og = _gla(proj, lr, wd, bd, row(gla_norm_g[0]))

    fbig, cwt, swt, f2, ccs, perm = _dft_tables(MERGE_TM)
    x4 = x.reshape(BATCH, FFT_N2, FFT_N1, D)
    z = _fft2(_fft1(x4, lng, lnb, w_fn, b_fn, fbig, cwt, swt), f2)

    kv = _memkv(mem.reshape(BATCH * MEM_LEN, D), row(ln_mem_g), row(ln_mem_b),
                w_mem_kv[0].astype(BF16))

    w_r = _pad_cols(w_router[0], LANES)
    wr_hi = w_r.astype(BF16)
    wr_lo = (w_r - wr_hi.astype(F32)).astype(BF16)
    b_r = jnp.concatenate([row(b_router[0]),
                           jnp.full((1, LANES - N_EXP), NEG_BIG, F32)], axis=1)
    h1, h1p, eidx, topw, cnt = _merge(
        x2, og, z, mq, proj, kv, lng, lnb,
        w_br_gla[0].astype(BF16), ccs, perm, w_br_fnet[0].astype(BF16),
        w_br_mem[0].astype(BF16), w_out[0].astype(BF16), row(b_out[0]),
        row(ln1_g[0]), row(ln1_b[0]), jnp.concatenate([wr_hi, wr_lo], axis=1), wr_hi, b_r)

    dest = _plan(eidx, cnt)
    counts = cnt[0, :N_EXP].astype(I32)
    dest_k = dest[:, :TOP_K]
    idx4 = dest_k.reshape(SC_WORKERS, SCD_NCH, SC_CH, TOP_K).transpose(0, 1, 3, 2)
    xs = _sc_dispatch(h1p, idx4.reshape(SC_WORKERS, SCD_NCH * TOP_K, SC_CH))
    ys = _experts(_work_items(counts), xs, w_gu[0], b_gu[0], w_down[0], b_down[0])
    out = h1
    for grp in range(COMB_GROUPS):
        dest_g = dest_k[grp * COMB_TG:(grp + 1) * COMB_TG].T.reshape(SC_WORKERS, SC_NCH, SC_CH)
        g4 = _sc_gather(ys, dest_g).reshape(TOP_K, COMB_TG, D // 2)
        out = _combine_dense(g4, out, topw, row(ln2_g[0]), row(ln2_b[0]), grp)
    return out.reshape(BATCH, SEQ, D)
```

```python
import functools
import math

import numpy as np
import jax
import jax.numpy as jnp
from jax import lax
from jax.experimental import pallas as pl
from jax.experimental.pallas import tpu as pltpu
from jax.experimental.pallas import tpu_sc as plsc

F32 = jnp.float32
BF16 = jnp.bfloat16
I32 = jnp.int32
U32 = jnp.uint32

D = 1024
BATCH = 4
SEQ = 4096
T = BATCH * SEQ
GLA_H = 4
GLA_DK = 128
GLA_DV = 256
GLA_LR = 16
GLA_TAU = 16.0
GLA_C = 64
FN_G = 4
FN_GW = 128
FN_W = 512
MEM_LEN = 256
MEM_H = 4
MEM_HD = 128
MQ_W = 512
N_EXP = 32
TOP_K = 4
D_FF = 1024
SW_LIMIT = 7.0
SW_ALPHA = 1.702
LN_EPS = 1e-5
RMS_EPS = 1e-6
DN_ALPHA = 2.0 ** 0.25
A_ROWS = T * TOP_K

FFT_N1 = 128
FFT_N2 = 32

LANES = 128
NEG_BIG = -1e30
MIB = 1024 * 1024


def _ln(x, g, b):
    mu = jnp.mean(x, axis=-1, keepdims=True)
    xc = x - mu
    var = jnp.mean(xc * xc, axis=-1, keepdims=True)
    return xc * lax.rsqrt(var + LN_EPS) * g + b


def _dot(a, b):
    return jnp.dot(a, b, preferred_element_type=F32)


def _dot_nt(a, b):
    return lax.dot_general(a, b, (((1,), (1,)), ((), ())), preferred_element_type=F32)


def _dot_tn(a, b):
    return lax.dot_general(a, b, (((0,), (0,)), ((), ())), preferred_element_type=F32)


def _interleave(*stages):
    live = list(stages)
    while live:
        for st in list(live):
            try:
                next(st)
            except StopIteration:
                live.remove(st)


def _split_bf16(a):
    hi = a.astype(BF16)
    return hi, (a - hi.astype(F32)).astype(BF16)


INPROJ_TM = 1024
INPROJ_TN = 3328


PROJ_MQ = 6 * 1024
PROJ_W = PROJ_MQ + MQ_W


def _inproj_kernel(x_ref, g_ref, b_ref, w_ref, bias_ref, wlr_ref, blr_ref,
                   proj_ref, lr_ref, hb_ref):
    @pl.when(pl.program_id(1) == 0)
    def _():
        hb = _ln(x_ref[...], g_ref[...], b_ref[...]).astype(BF16)
        hb_ref[...] = hb
        lr_ref[...] = _dot(hb, wlr_ref[...]) + blr_ref[...]

    proj_ref[...] = (_dot(hb_ref[...], w_ref[...]) + bias_ref[...]).astype(BF16)


def _inproj(x2, ln_g, ln_b, w_main, b_main, w_lr, b_lr):
    tm, tn = INPROJ_TM, INPROJ_TN
    nj = PROJ_W // tn
    row = lambda i, j: (i, 0)
    const = lambda i, j: (0, 0)
    outs = (
        jax.ShapeDtypeStruct((T, PROJ_W), BF16),
        jax.ShapeDtypeStruct((T, LANES), F32),
    )
    return pl.pallas_call(
        _inproj_kernel,
        grid=(T // tm, nj),
        in_specs=[
            pl.BlockSpec((tm, D), row),
            pl.BlockSpec((1, D), const),
            pl.BlockSpec((1, D), const),
            pl.BlockSpec((D, tn), lambda i, j: (0, j)),
            pl.BlockSpec((1, tn), lambda i, j: (0, j)),
            pl.BlockSpec((D, LANES), const),
            pl.BlockSpec((1, LANES), const),
        ],
        out_specs=[
            pl.BlockSpec((tm, tn), lambda i, j: (i, j)),
            pl.BlockSpec((tm, LANES), row),
        ],
        out_shape=outs,
        scratch_shapes=[pltpu.VMEM((tm, D), BF16)],
        compiler_params=pltpu.CompilerParams(
            dimension_semantics=("arbitrary", "arbitrary"),
            vmem_limit_bytes=48 * MIB),
        name="ln_inproj",
    )(x2, ln_g, ln_b, w_main, b_main, w_lr, b_lr)


GLA_BULK = 256
GLA_FIN = 512
GLA_NCH = SEQ // GLA_C
GLA_CPB = GLA_BULK // GLA_C
GLA_PIPE = 4


def _gla_kernel(q_ref, k_ref, v_ref, r_ref, lr_ref, wd_ref, bd_ref, g_ref, cs_ref, o_ref,
                acc_ref, qin_ref, kin_ref, kst_ref, dec_ref, u_ref, stf_ref, stb_ref):
    C = GLA_C
    G = GLA_BULK
    DK = GLA_DK
    NG = SEQ // G
    scale = DK ** -0.5
    ii = lax.broadcasted_iota(I32, (G, G), 0)
    jj = lax.broadcasted_iota(I32, (G, G), 1)
    same = (ii // C) == (jj // C)
    lower = jnp.logical_and(same, ii >= jj)
    upper = jnp.logical_and(same, ii <= jj)
    is_fwd = lax.broadcasted_iota(I32, (G, 2 * DK), 1) < DK
    chunk_of_row = lax.broadcasted_iota(I32, (G, DK), 0) // C

    def stage_a(gi):
        rows = pl.ds(pl.multiple_of(gi * G, G), G)
        z = _dot(jnp.concatenate(_split_bf16(lr_ref[rows, :]), axis=1), wd_ref[...]) + bd_ref[...]
        yield
        la = -(jnp.maximum(-z, 0.0) + jnp.log(1.0 + jnp.exp(-jnp.abs(z)))) * (1.0 / GLA_TAU)
        la_hi, la_lo = _split_bf16(la)
        pre2 = _dot(cs_ref[...], jnp.concatenate([la_hi, la_lo], axis=1))
        yield
        pre = pre2[:, :2 * DK] + pre2[:, 2 * DK:]
        blast = jnp.concatenate(
            [jnp.broadcast_to(pre[ci * C + C - 1:ci * C + C, :], (C, 2 * DK))
             for ci in range(GLA_CPB)], axis=0)
        b = jnp.where(is_fwd, pre, blast - pre + la)
        qf32 = q_ref[rows, :].astype(F32)
        kf32 = k_ref[rows, :].astype(F32)
        q2 = jnp.concatenate([qf32, qf32], axis=1)
        k2 = jnp.concatenate([kf32, kf32], axis=1)
        qin_ref[rows, :] = (q2 * (scale * jnp.exp(b))).astype(BF16)
        kin_ref[rows, :] = (k2 * jnp.exp(-b)).astype(BF16)
        kst_ref[rows, :] = (k2 * jnp.exp(blast - b)).astype(BF16)
        dec = jnp.exp(blast)
        for ci in range(GLA_CPB):
            dec_ref[pl.ds(gi * GLA_CPB + ci, 1), :] = dec[ci * C:ci * C + 1, :]

    def stage_b(gi):
        rows = pl.ds(pl.multiple_of(gi * G, G), G)
        qi = qin_ref[rows, :]
        ki = kin_ref[rows, :]
        ks = kst_ref[rows, :]
        vb = v_ref[rows, :]
        att_f = _dot_nt(qi[:, :DK], ki[:, :DK])
        att_b = _dot_nt(qi[:, DK:], ki[:, DK:])
        yield
        att = jnp.where(lower, att_f, 0.0) + jnp.where(upper, att_b, 0.0)
        acc_ref[rows, :] = _dot(att.astype(BF16), vb)
        yield
        ksb = jnp.concatenate(
            [jnp.where(chunk_of_row == ci, ks[:, d * DK:(d + 1) * DK], jnp.zeros((G, DK), BF16))
             for d in range(2) for ci in range(GLA_CPB)], axis=1)
        u = _dot_tn(vb, ksb)
        for d in range(2):
            for ci in range(GLA_CPB):
                col = (d * GLA_CPB + ci) * DK
                u_ref[d, gi * GLA_CPB + ci] = u[:, col:col + DK]

    P = GLA_PIPE
    _interleave(*[stage_a(j) for j in range(P)])

    def bulk(i, carry):
        g = P * i
        _interleave(*[st for j in range(P) for st in (stage_b(g - P + j), stage_a(g + j))])
        return carry

    lax.fori_loop(1, NG // P, bulk, 0)
    _interleave(*[stage_b(NG - P + j) for j in range(P)])

    stf_ref[...] = jnp.zeros_like(stf_ref)
    stb_ref[...] = jnp.zeros_like(stb_ref)

    def one(n, d, st_ref):
        lanes = slice(d * DK, (d + 1) * DK)
        rows = pl.ds(pl.multiple_of(n * C, C), C)
        st = st_ref[...]
        acc_ref[rows, :] += _dot_nt(qin_ref[rows, lanes], st.astype(BF16))
        st_ref[...] = st * dec_ref[pl.ds(n, 1), :][:, lanes] + u_ref[d, n]

    def step(i, carry):
        one(i, 0, stf_ref)
        one(GLA_NCH - 1 - i, 1, stb_ref)
        return carry

    lax.fori_loop(0, GLA_NCH, step, 0, unroll=16)

    def fin(gi, carry):
        rows = pl.ds(pl.multiple_of(gi * GLA_FIN, GLA_FIN), GLA_FIN)
        o = acc_ref[rows, :]
        o = o * lax.rsqrt(jnp.mean(o * o, axis=-1, keepdims=True) + RMS_EPS) * g_ref[...]
        rg = r_ref[rows, :].astype(F32)
        o_ref[rows, :] = (o * (rg * jax.nn.sigmoid(rg))).astype(BF16)
        return carry

    lax.fori_loop(0, SEQ // GLA_FIN, fin, 0)


def _gla(proj, lr, wd, bd, g):
    i = np.arange(GLA_BULK)
    cs = ((i[:, None] // GLA_C) == (i[None, :] // GLA_C)) & (i[:, None] >= i[None, :])
    cs = jnp.asarray(cs, dtype=F32).astype(BF16)
    v_blk = 1024 // GLA_DV
    return pl.pallas_call(
        _gla_kernel,
        grid=(BATCH, GLA_H),
        in_specs=[
            pl.BlockSpec((SEQ, GLA_DK), lambda b, h: (b, h)),
            pl.BlockSpec((SEQ, GLA_DK), lambda b, h: (b, GLA_H + h)),
            pl.BlockSpec((SEQ, GLA_DV), lambda b, h: (b, v_blk + h)),
            pl.BlockSpec((SEQ, GLA_DV), lambda b, h: (b, 2 * v_blk + h)),
            pl.BlockSpec((SEQ, LANES), lambda b, h: (b, 0)),
            pl.BlockSpec((2 * LANES, 2 * GLA_DK), lambda b, h: (0, h)),
            pl.BlockSpec((1, 2 * GLA_DK), lambda b, h: (0, h)),
            pl.BlockSpec((1, GLA_DV), lambda b, h: (0, 0)),
            pl.BlockSpec((GLA_BULK, GLA_BULK), lambda b, h: (0, 0)),
        ],
        out_specs=pl.BlockSpec((SEQ, GLA_DV), lambda b, h: (b, h)),
        out_shape=jax.ShapeDtypeStruct((T, GLA_H * GLA_DV), BF16),
        scratch_shapes=[
            pltpu.VMEM((SEQ, GLA_DV), F32),
            pltpu.VMEM((SEQ, 2 * GLA_DK), BF16),
            pltpu.VMEM((SEQ, 2 * GLA_DK), BF16),
            pltpu.VMEM((SEQ, 2 * GLA_DK), BF16),
            pltpu.VMEM((GLA_NCH, 2 * GLA_DK), F32),
            pltpu.VMEM((2, GLA_NCH, GLA_DV, GLA_DK), F32),
            pltpu.VMEM((GLA_DV, GLA_DK), F32),
            pltpu.VMEM((GLA_DV, GLA_DK), F32),
        ],
        compiler_params=pltpu.CompilerParams(
            dimension_semantics=("arbitrary", "arbitrary"),
            vmem_limit_bytes=58 * MIB),
        name="gla",
    )(proj, proj, proj, proj, lr, wd, bd, g, cs)


FFT1_S = 32
FFT1_SUB = 4
FFT1_ROWS = FFT_N2 * FFT1_S
FFT2_KB = 8


def _fft1_kernel(x_ref, g_ref, b_ref, w_ref, bias_ref, fbig_ref, cw_ref, sw_ref, o_ref):
    sh = FFT1_S // FFT1_SUB
    rows = FFT_N2 * sh
    res = {}

    def sub(h):
        xv = x_ref[:, h * sh:(h + 1) * sh, :].reshape(rows, D)
        hb = _ln(xv, g_ref[...], b_ref[...]).astype(BF16)
        fn = (_dot(hb, w_ref[...]) + bias_ref[...]).astype(BF16)
        yield
        a = _dot(fbig_ref[...], fn)
        yield
        ar = a[:rows]
        ai = a[rows:]
        cw = jnp.concatenate([cw_ref[h]] * (FN_W // LANES), axis=1)
        sw = jnp.concatenate([sw_ref[h]] * (FN_W // LANES), axis=1)
        res[h] = ((ar * cw + ai * sw).reshape(FFT_N2, sh, FN_W),
                  (ai * cw - ar * sw).reshape(FFT_N2, sh, FN_W))

    _interleave(*[sub(h) for h in range(FFT1_SUB)])
    for ri in range(2):
        o_ref[ri] = jnp.concatenate([res[h][ri] for h in range(FFT1_SUB)], axis=1).astype(BF16)


def _fft1(x4, ln_g, ln_b, w_fn, b_fn, fbig, cwt, swt):
    s = FFT1_S
    const = lambda b, j: (0, 0)
    return pl.pallas_call(
        _fft1_kernel,
        grid=(BATCH, FFT_N1 // s),
        in_specs=[
            pl.BlockSpec((None, FFT_N2, s, D), lambda b, j: (b, 0, j, 0)),
            pl.BlockSpec((1, D), const),
            pl.BlockSpec((1, D), const),
            pl.BlockSpec((D, FN_W), const),
            pl.BlockSpec((1, FN_W), const),
            pl.BlockSpec((2 * FFT1_ROWS // FFT1_SUB, FFT1_ROWS // FFT1_SUB), const),
            pl.BlockSpec((FFT1_SUB, FFT1_ROWS // FFT1_SUB, LANES), lambda b, j: (j, 0, 0)),
            pl.BlockSpec((FFT1_SUB, FFT1_ROWS // FFT1_SUB, LANES), lambda b, j: (j, 0, 0)),
        ],
        out_specs=pl.BlockSpec((None, 2, FFT_N2, s, FN_W), lambda b, j: (b, 0, 0, j, 0)),
        out_shape=jax.ShapeDtypeStruct((BATCH, 2, FFT_N2, FFT_N1, FN_W), BF16),
        compiler_params=pltpu.CompilerParams(
            dimension_semantics=("arbitrary", "arbitrary"),
            vmem_limit_bytes=40 * MIB),
        name="fft_stage1",
    )(x4, ln_g, ln_b, w_fn, b_fn, fbig, cwt, swt)


def _fft2_kernel(d_ref, f2_ref, o_ref):
    f2 = f2_ref[...]
    for kk in range(FFT2_KB):
        z = _dot(f2, jnp.concatenate([d_ref[0, kk], d_ref[1, kk]], axis=0))
        o_ref[0, kk] = z[:FFT_N1].astype(BF16)
        o_ref[1, kk] = z[FFT_N1:].astype(BF16)


def _fft2(dmat, f2):
    kb = FFT2_KB
    blk = (None, 2, kb, FFT_N1, FN_W)
    return pl.pallas_call(
        _fft2_kernel,
        grid=(BATCH, FFT_N2 // kb),
        in_specs=[
            pl.BlockSpec(blk, lambda b, j: (b, 0, j, 0, 0)),
            pl.BlockSpec((2 * FFT_N1, 2 * FFT_N1), lambda b, j: (0, 0)),
        ],
        out_specs=pl.BlockSpec(blk, lambda b, j: (b, 0, j, 0, 0)),
        out_shape=jax.ShapeDtypeStruct((BATCH, 2, FFT_N2, FFT_N1, FN_W), BF16),
        compiler_params=pltpu.CompilerParams(
            dimension_semantics=("arbitrary", "arbitrary")),
        name="fft_stage2",
    )(dmat, f2)


def _dft_tables(merge_tm):
    s = FFT1_S // FFT1_SUB
    n2 = np.arange(FFT_N2, dtype=np.float64)
    n1 = np.arange(FFT_N1, dtype=np.float64)
    th = 2.0 * np.pi * np.outer(n2, n2) / FFT_N2
    f1 = np.stack([np.cos(th), -np.sin(th)]) / math.sqrt(SEQ)
    fbig = np.einsum("rkn,st->rksnt", f1, np.eye(s)).reshape(2 * FFT_N2 * s, FFT_N2 * s)
    tw = 2.0 * np.pi * np.outer(n2, n1) / SEQ
    tw = tw.reshape(FFT_N2, FFT_N1 // s, s).transpose(1, 0, 2).reshape(FFT_N1 // s, FFT_N2 * s)
    cwt = np.broadcast_to(np.cos(tw)[:, :, None], tw.shape + (LANES,))
    swt = np.broadcast_to(np.sin(tw)[:, :, None], tw.shape + (LANES,))
    th1 = 2.0 * np.pi * np.outer(n1, n1) / FFT_N1
    c1, s1 = np.cos(th1), np.sin(th1)
    f2 = np.block([[c1, s1], [-s1, c1]])
    cc = np.arange(FN_GW, dtype=np.float64)
    thc = 2.0 * np.pi * np.outer(cc, cc) / FN_GW
    ccs = np.concatenate([np.cos(thc), np.sin(thc)], axis=0) / math.sqrt(FN_GW)
    k1n = merge_tm // FFT_N2
    r = np.arange(merge_tm)
    perm = np.zeros((merge_tm, merge_tm))
    perm[r, (r % FFT_N2) * k1n + r // FFT_N2] = 1.0
    as32 = lambda a: jnp.asarray(np.ascontiguousarray(a), dtype=F32)
    return (as32(fbig).astype(BF16), as32(cwt), as32(swt), as32(f2).astype(BF16),
            as32(ccs).astype(BF16), as32(perm).astype(BF16))


def _memkv_kernel(m_ref, g_ref, b_ref, w_ref, o_ref):
    mn = _ln(m_ref[...], g_ref[...], b_ref[...]).astype(BF16)
    o_ref[...] = _dot(mn, w_ref[...]).astype(BF16)


def _memkv(mem2, g, b, w):
    return pl.pallas_call(
        _memkv_kernel,
        grid=(BATCH,),
        in_specs=[
            pl.BlockSpec((MEM_LEN, D), lambda i: (i, 0)),
            pl.BlockSpec((1, D), lambda i: (0, 0)),
            pl.BlockSpec((1, D), lambda i: (0, 0)),
            pl.BlockSpec((D, 2 * MQ_W), lambda i: (0, 0)),
        ],
        out_specs=pl.BlockSpec((MEM_LEN, 2 * MQ_W), lambda i: (i, 0)),
        out_shape=jax.ShapeDtypeStruct((BATCH * MEM_LEN, 2 * MQ_W), BF16),
        compiler_params=pltpu.CompilerParams(dimension_semantics=("arbitrary",)),
        name="mem_kv",
    )(mem2, g, b, w)


MERGE_TM = 512
MERGE_K1 = MERGE_TM // FFT_N2


def _pack_bf16_pair(v):
    n = v.shape[1] // 2
    bits = lax.bitcast_convert_type(v.astype(BF16).astype(F32), U32)
    return (bits[:, n:] & jnp.uint32(0xFFFF0000)) | (bits[:, :n] >> 16)


def _unpack_bf16_pair(p):
    lo = lax.bitcast_convert_type(p << 16, F32)
    hi = lax.bitcast_convert_type(p & jnp.uint32(0xFFFF0000), F32)
    return lo, hi


def _merge_kernel(x_ref, og_ref, zr_ref, zi_ref, mq_ref, gt_ref, kv_ref,
                  lng_ref, lnb_ref, wg_ref, ccs_ref, perm_ref, wf_ref, wm_ref, wo_ref, bo_ref,
                  l1g_ref, l1b_ref, wr2_ref, wrh_ref, br_ref,
                  h1_ref, h1p_ref, eidx_ref, topw_ref, cnt_ref):
    tm = MERGE_TM
    y = {}

    def branch_fnet():
        zr = zr_ref[...].reshape(tm, FN_W)
        zi = zi_ref[...].reshape(tm, FN_W)
        ys = []
        for g in range(FN_G):
            sl = slice(g * FN_GW, (g + 1) * FN_GW)
            ys.append(_dot(jnp.concatenate([zr[:, sl], zi[:, sl]], axis=1), ccs_ref[...]))
        yield
        yp = _dot(perm_ref[...], jnp.concatenate(ys, axis=1).astype(BF16))
        yield
        y["fnet"] = _dot(yp.astype(BF16), wf_ref[...])

    def branch_mem():
        heads = [slice(hd * MEM_HD, (hd + 1) * MEM_HD) for hd in range(MEM_H)]
        ss = [_dot_nt(mq_ref[:, sl], kv_ref[:, sl]) for sl in heads]
        yield
        oms = []
        for hd, s in enumerate(ss):
            s = s * (MEM_HD ** -0.5)
            s = s - jnp.max(s, axis=-1, keepdims=True)
            p = jnp.exp(s)
            p = p * (1.0 / jnp.sum(p, axis=-1, keepdims=True))
            oms.append(_dot(p.astype(BF16),
                            kv_ref[:, MQ_W + hd * MEM_HD:MQ_W + (hd + 1) * MEM_HD]))
        yield
        y["mem"] = _dot(jnp.concatenate(oms, axis=1).astype(BF16), wm_ref[...])

    def branch_gla():
        y["gla"] = _dot(og_ref[...], wg_ref[...])
        yield

    _interleave(branch_fnet(), branch_mem(), branch_gla())

    def gate(c):
        return 0.5 + 0.5 * jnp.tanh(0.5 * gt_ref[:, c * D:(c + 1) * D].astype(F32))

    merged = gate(0) * y["gla"] + gate(1) * y["fnet"] + gate(2) * y["mem"]
    mix = _dot(merged.astype(BF16), wo_ref[...]) + bo_ref[...]
    h = _ln(x_ref[...], lng_ref[...], lnb_ref[...])
    h1 = _ln(DN_ALPHA * h + mix, l1g_ref[...], l1b_ref[...])
    h1_ref[...] = h1
    h1p_ref[...] = _pack_bf16_pair(h1)

    h_hi, h_lo = _split_bf16(h1)
    d2 = _dot(h_hi, wr2_ref[...])
    l = d2[:, :LANES] + d2[:, LANES:] + _dot(h_lo, wrh_ref[...]) + br_ref[...]
    lane = lax.broadcasted_iota(I32, (tm, LANES), 1)
    vals, idxs = [], []
    for _ in range(TOP_K):
        m = jnp.max(l, axis=-1, keepdims=True)
        idx = jnp.min(jnp.where(l == m, lane, LANES), axis=-1, keepdims=True)
        vals.append(m)
        idxs.append(idx)
        l = jnp.where(lane == idx, -jnp.inf, l)
    es = [jnp.exp(v - vals[0]) for v in vals]
    den = es[0] + es[1] + es[2] + es[3]
    eo = jnp.zeros((tm, LANES), I32)
    wo = jnp.zeros((tm, LANES), F32)
    chosen = jnp.zeros((tm, LANES), F32)
    for k in range(TOP_K):
        eo = jnp.where(lane == k, idxs[k], eo)
        wo = jnp.where(lane == k, es[k] / den, wo)
        chosen = chosen + jnp.where(lane == idxs[k], 1.0, 0.0)
    eidx_ref[...] = eo
    topw_ref[...] = wo

    @pl.when(pl.program_id(0) == 0)
    def _():
        cnt_ref[...] = jnp.zeros_like(cnt_ref)

    cnt_ref[...] += jnp.broadcast_to(jnp.sum(chosen, axis=0, keepdims=True), cnt_ref.shape)


def _merge(x2, og, z, mq, gates, kv, lng, lnb, wg, ccs, perm, wf, wm, wo, bo, l1g, l1b,
           wr2, wrh, br):
    tm = MERGE_TM
    per_b = SEQ // tm
    row = lambda i: (i, 0)
    const = lambda i: (0, 0)
    zblk = (None, None, FFT_N2, MERGE_K1, FN_W)
    outs = (
        jax.ShapeDtypeStruct((T, D), F32),
        jax.ShapeDtypeStruct((T, D // 2), U32),
        jax.ShapeDtypeStruct((T, LANES), I32),
        jax.ShapeDtypeStruct((T, LANES), F32),
        jax.ShapeDtypeStruct((8, LANES), F32),
    )
    return pl.pallas_call(
        _merge_kernel,
        grid=(T // tm,),
        in_specs=[
            pl.BlockSpec((tm, D), row),
            pl.BlockSpec((tm, D), row),
            pl.BlockSpec(zblk, lambda i: (i // per_b, 0, 0, i % per_b, 0)),
            pl.BlockSpec(zblk, lambda i: (i // per_b, 1, 0, i % per_b, 0)),
            pl.BlockSpec((tm, MQ_W), lambda i: (i, PROJ_MQ // MQ_W)),
            pl.BlockSpec((tm, 3 * D), lambda i: (i, 1)),
            pl.BlockSpec((MEM_LEN, 2 * MQ_W), lambda i: (i // per_b, 0)),
            pl.BlockSpec((1, D), const), pl.BlockSpec((1, D), const),
            pl.BlockSpec((D, D), const),
            pl.BlockSpec((2 * FN_GW, FN_GW), const),
            pl.BlockSpec((tm, tm), const),
            pl.BlockSpec((FN_W, D), const),
            pl.BlockSpec((MQ_W, D), const),
            pl.BlockSpec((D, D), const),
            pl.BlockSpec((1, D), const),
            pl.BlockSpec((1, D), const), pl.BlockSpec((1, D), const),
            pl.BlockSpec((D, 2 * LANES), const),
            pl.BlockSpec((D, LANES), const),
            pl.BlockSpec((1, LANES), const),
        ],
        out_specs=[
            pl.BlockSpec((tm, D), row),
            pl.BlockSpec((tm, D // 2), row),
            pl.BlockSpec((tm, LANES), row),
            pl.BlockSpec((tm, LANES), row),
            pl.BlockSpec((8, LANES), const),
        ],
        out_shape=outs,
        compiler_params=pltpu.CompilerParams(
            dimension_semantics=("arbitrary",),
            vmem_limit_bytes=58 * MIB),
        name="merge_ln1_router",
    )(x2, og, z, z, mq, gates, kv, lng, lnb, wg, ccs, perm, wf, wm, wo, bo, l1g, l1b,
      wr2, wrh, br)


PLAN_TP = 1024


def _expert_onehots(e, lane):
    onehots = [lane == e[:, k:k + 1] for k in range(TOP_K)]
    mf = jnp.zeros(lane.shape, F32)
    for oh in onehots:
        mf = mf + jnp.where(oh, 1.0, 0.0)
    return onehots, mf


def _plan_kernel(e_ref, tot_ref, dest_ref, cnt_ref, off_ref):
    i = pl.program_id(0)
    tp = PLAN_TP
    lane = lax.broadcasted_iota(I32, (tp, LANES), 1)
    onehots, mf = _expert_onehots(e_ref[...], lane)

    @pl.when(i == 0)
    def _():
        tot = tot_ref[0:1, :]
        padded = jnp.floor((tot + (MOE_BM - 1)) * (1.0 / MOE_BM)) * MOE_BM
        lane1 = lax.broadcasted_iota(I32, (1, LANES), 1)
        inc = padded
        for s in (1, 2, 4, 8, 16, 32, 64):
            inc = inc + jnp.where(lane1 >= s, pltpu.roll(inc, s, 1), 0.0)
        off_ref[...] = inc - padded
        cnt_ref[...] = jnp.zeros_like(cnt_ref)

    ri = lax.broadcasted_iota(I32, (tp, tp), 0)
    ci = lax.broadcasted_iota(I32, (tp, tp), 1)
    ltri = jnp.where(ri > ci, 1.0, 0.0).astype(BF16)
    rank = _dot(ltri, mf.astype(BF16)) + cnt_ref[...] + off_ref[...]
    out = jnp.zeros((tp, LANES), I32)
    for k in range(TOP_K):
        dk = jnp.sum(jnp.where(onehots[k], rank, 0.0), axis=-1, keepdims=True)
        out = jnp.where(lane == k, dk.astype(I32), out)
    dest_ref[...] = out
    cnt_ref[...] += jnp.sum(mf, axis=0, keepdims=True)


def _plan(eidx, cnt):
    tp = PLAN_TP
    return pl.pallas_call(
        _plan_kernel,
        grid=(T // tp,),
        in_specs=[pl.BlockSpec((tp, LANES), lambda i: (i, 0)),
                  pl.BlockSpec((8, LANES), lambda i: (0, 0))],
        out_specs=pl.BlockSpec((tp, LANES), lambda i: (i, 0)),
        out_shape=jax.ShapeDtypeStruct((T, LANES), I32),
        scratch_shapes=[pltpu.VMEM((1, LANES), F32), pltpu.VMEM((1, LANES), F32)],
        compiler_params=pltpu.CompilerParams(dimension_semantics=("arbitrary",)),
        name="route_plan",
    )(eidx, cnt)


MOE_BM = 512
MOE_NW = A_ROWS // MOE_BM + N_EXP
XS_ROWS = MOE_NW * MOE_BM
MOE_FF_SLICES = 2


def _expert_kernel(we_ref, wb_ref, wv_ref, wfe_ref, wsl_ref, wnx_ref,
                   x_ref, wgu_hbm, bgu_ref, wdn_hbm, bdn_ref, o_ref,
                   wgu_f32, wdn_f32, wgu_bf, wdn_bf, sems):
    w = pl.program_id(0)
    e = we_ref[w]

    def weight_copies(expert, slot):
        return (pltpu.make_async_copy(wgu_hbm.at[expert], wgu_f32.at[slot], sems.at[slot, 0]),
                pltpu.make_async_copy(wdn_hbm.at[expert], wdn_f32.at[slot], sems.at[slot, 1]))

    @pl.when(w == 0)
    def _():
        for cp_ in weight_copies(e, 0):
            cp_.start()

    first = wfe_ref[w] == 1
    slot = wsl_ref[w]

    @pl.when(first)
    def _():
        for cp_ in weight_copies(e, slot):
            cp_.wait()
        nxt = wnx_ref[w]

        @pl.when(nxt >= 0)
        def _():
            for cp_ in weight_copies(nxt, 1 - slot):
                cp_.start()

    hw = D_FF // MOE_FF_SLICES
    col_slices = [(slice(hf * hw, (hf + 1) * hw), slice(D_FF + hf * hw, D_FF + (hf + 1) * hw))
                  for hf in range(MOE_FF_SLICES)]

    def cast_weights():
        for gc, uc in col_slices:
            wgu_bf[:, gc] = wgu_f32[slot, :, gc].astype(BF16)
            wgu_bf[:, uc] = wgu_f32[slot, :, uc].astype(BF16)
            yield
        for hf in range(MOE_FF_SLICES):
            rs = slice(hf * hw, (hf + 1) * hw)
            wdn_bf[rs, :] = wdn_f32[slot, rs, :].astype(BF16)
            yield

    def ffn(rows):
        xlo, xhi = _unpack_bf16_pair(x_ref[rows, :])
        xb = jnp.concatenate([xlo.astype(BF16), xhi.astype(BF16)], axis=1)
        bgu = bgu_ref[pl.ds(e, 1), :]
        gus = []
        for gc, uc in col_slices:
            gus.append((_dot(xb, wgu_bf[:, gc]) + bgu[:, gc], _dot(xb, wgu_bf[:, uc]) + bgu[:, uc]))
            yield
        out = bdn_ref[pl.ds(e, 1), :]
        for hf, (g, u) in enumerate(gus):
            gate = jnp.minimum(g, SW_LIMIT)
            up = jnp.clip(u, -SW_LIMIT, SW_LIMIT)
            act = (up + 1.0) * (gate * jax.nn.sigmoid(SW_ALPHA * gate))
            out = out + _dot(act.astype(BF16), wdn_bf[hf * hw:(hf + 1) * hw, :])
            if hf + 1 < MOE_FF_SLICES:
                yield
        o_ref[rows, :] = _pack_bf16_pair(out)

    for code, rows in ((1, slice(None)), (2, slice(0, MOE_BM // 2))):
        @pl.when(jnp.logical_and(wv_ref[w] == code, first))
        def _(rows=rows):
            _interleave(cast_weights(), ffn(rows))

        @pl.when(jnp.logical_and(wv_ref[w] == code, jnp.logical_not(first)))
        def _(rows=rows):
            _interleave(ffn(rows))

    @pl.when(wv_ref[w] == 2)
    def _():
        o_ref[MOE_BM // 2:, :] = jnp.zeros((MOE_BM // 2, D // 2), U32)

    @pl.when(wv_ref[w] == 0)
    def _():
        o_ref[...] = jnp.zeros_like(o_ref)


def _experts(meta, xs, w_gu, b_gu, w_down, b_down):
    return pl.pallas_call(
        _expert_kernel,
        grid_spec=pltpu.PrefetchScalarGridSpec(
            num_scalar_prefetch=len(meta),
            grid=(MOE_NW,),
            in_specs=[
                pl.BlockSpec((MOE_BM, D // 2), lambda w, we, wb, *_: (wb[w], 0)),
                pl.BlockSpec(memory_space=pl.ANY),
                pl.BlockSpec((N_EXP, 2 * D_FF), lambda w, *_: (0, 0)),
                pl.BlockSpec(memory_space=pl.ANY),
                pl.BlockSpec((N_EXP, D), lambda w, *_: (0, 0)),
            ],
            out_specs=pl.BlockSpec((MOE_BM, D // 2), lambda w, *_: (w, 0)),
            scratch_shapes=[
                pltpu.VMEM((2, D, 2 * D_FF), F32),
                pltpu.VMEM((2, D_FF, D), F32),
                pltpu.VMEM((D, 2 * D_FF), BF16),
                pltpu.VMEM((D_FF, D), BF16),
                pltpu.SemaphoreType.DMA((2, 2)),
            ],
        ),
        out_shape=jax.ShapeDtypeStruct((XS_ROWS, D // 2), U32),
        compiler_params=pltpu.CompilerParams(
            dimension_semantics=("arbitrary",),
            vmem_limit_bytes=56 * MIB),
        name="moe_experts",
    )(*meta, xs, w_gu, b_gu, w_down, b_down)


def _work_items(counts):
    n_e = (counts + MOE_BM - 1) // MOE_BM
    item_end = jnp.cumsum(n_e)
    total = item_end[-1]
    w = jnp.arange(MOE_NW, dtype=I32)
    valid = w < total
    wc = jnp.minimum(w, total - 1)
    e_w = jnp.sum((item_end[None, :] <= wc[:, None]).astype(I32), axis=1)
    e_w = jnp.minimum(e_w, N_EXP - 1)
    rows_here = counts[e_w] - (wc - (item_end - n_e)[e_w]) * MOE_BM
    valid = jnp.where(valid, jnp.where(rows_here <= MOE_BM // 2, 2, 1), 0)
    prev_e = jnp.concatenate([jnp.full((1,), -1, I32), e_w[:-1]])
    fe = (e_w != prev_e).astype(I32)
    slot = (jnp.cumsum(fe) - 1) % 2
    first_at = jnp.where(fe == 1, w, MOE_NW)
    next_first = jnp.concatenate([lax.cummin(first_at, reverse=True)[1:],
                                  jnp.full((1,), MOE_NW, I32)])
    nxt = jnp.where(next_first < MOE_NW, e_w[jnp.minimum(next_first, MOE_NW - 1)], -1)
    return tuple(a.astype(I32) for a in (e_w, wc, valid, fe, slot, nxt))


COMB_TM = 512
SC_CORES = 2
SC_SUBCORES = 16
SC_WORKERS = SC_CORES * SC_SUBCORES
SC_CH = 64
COMB_GROUPS = 4
COMB_TG = T // COMB_GROUPS
SC_ROWS_PER_W = COMB_TG * TOP_K // SC_WORKERS
SC_NCH = SC_ROWS_PER_W // SC_CH


def _sc_gather(table, idx3):
    mesh = plsc.VectorSubcoreMesh(core_axis_name="c", subcore_axis_name="s")

    @functools.partial(
        pl.kernel, mesh=mesh,
        out_type=jax.ShapeDtypeStruct((COMB_TG * TOP_K, D // 2), U32),
        scratch_types=[
            pltpu.VMEM((SC_NCH, SC_CH), I32),
            pltpu.VMEM((2, SC_CH, D // 2), U32),
            pltpu.SemaphoreType.DMA((2,)),
            pltpu.SemaphoreType.DMA((2,)),
        ],
    )
    def k(table_hbm, idx_hbm, out_hbm, idx_v, rows_v, gsem, psem):
        wid = lax.axis_index("s") * SC_CORES + lax.axis_index("c")
        base = wid * SC_ROWS_PER_W
        pltpu.sync_copy(idx_hbm.at[wid], idx_v)

        def gather(j, b):
            return pltpu.make_async_copy(table_hbm.at[idx_v.at[j]], rows_v.at[b], gsem.at[b])

        def put(j, b):
            return pltpu.make_async_copy(rows_v.at[b], out_hbm.at[pl.ds(base + j * SC_CH, SC_CH)],
                                         psem.at[b])

        gather(0, 0).start()

        @pl.loop(0, SC_NCH, step=2)
        def _(j0):
            for b in range(2):
                j = j0 + b

                @pl.when(j + 1 < SC_NCH)
                def _():
                    @pl.when(j >= 1)
                    def _():
                        put(j - 1, 1 - b).wait()
                    gather(j + 1, 1 - b).start()

                gather(j, b).wait()
                put(j, b).start()

        put(SC_NCH - 2, 0).wait()
        put(SC_NCH - 1, 1).wait()

    return k(table, idx3)


SCD_TOK_PER_W = T // SC_WORKERS
SCD_NCH = SCD_TOK_PER_W // SC_CH


def _sc_dispatch(h1p, idx4):
    mesh = plsc.VectorSubcoreMesh(core_axis_name="c", subcore_axis_name="s")

    @functools.partial(
        pl.kernel, mesh=mesh,
        out_type=jax.ShapeDtypeStruct((XS_ROWS, D // 2), U32),
        scratch_types=[
            pltpu.VMEM((SCD_NCH * TOP_K, SC_CH), I32),
            pltpu.VMEM((2, SC_CH, D // 2), U32),
            pltpu.SemaphoreType.DMA((2,)),
            pltpu.SemaphoreType.DMA((2,)),
        ],
    )
    def k(h_hbm, idx_hbm, xs_hbm, idx_v, rows_v, gsem, psem):
        wid = lax.axis_index("s") * SC_CORES + lax.axis_index("c")
        base = wid * SCD_TOK_PER_W
        pltpu.sync_copy(idx_hbm.at[wid], idx_v)

        def get(c, b):
            return pltpu.make_async_copy(h_hbm.at[pl.ds(base + c * SC_CH, SC_CH)], rows_v.at[b],
                                         gsem.at[b])

        def puts(c, b):
            return [pltpu.make_async_copy(rows_v.at[b], xs_hbm.at[idx_v.at[c * TOP_K + kk]],
                                          psem.at[b]) for kk in range(TOP_K)]

        get(0, 0).start()

        @pl.loop(0, SCD_NCH, step=2)
        def _(c0):
            for b in range(2):
                c = c0 + b

                @pl.when(c + 1 < SCD_NCH)
                def _():
                    @pl.when(c >= 1)
                    def _():
                        for cp_ in puts(c - 1, 1 - b):
                            cp_.wait()
                    get(c + 1, 1 - b).start()

                get(c, b).wait()
                for cp_ in puts(c, b):
                    cp_.start()

        for cp_ in puts(SCD_NCH - 2, 0) + puts(SCD_NCH - 1, 1):
            cp_.wait()

    return k(h1p, idx4)


def _combine_dense_kernel(g_ref, h1_ref, tw_ref, lg_ref, lb_ref, o_ref):
    tw = tw_ref[...]
    ylo = jnp.zeros((COMB_TM, D // 2), F32)
    yhi = jnp.zeros((COMB_TM, D // 2), F32)
    for k in range(TOP_K):
        lo, hi = _unpack_bf16_pair(g_ref[k])
        wk = tw[:, k:k + 1]
        ylo = ylo + lo * wk
        yhi = yhi + hi * wk
    ff = jnp.concatenate([ylo, yhi], axis=1)
    o_ref[...] = _ln(DN_ALPHA * h1_ref[...] + ff, lg_ref[...], lb_ref[...])


def _combine_dense(g4, h1, topw, g, b, group):
    tm = COMB_TM
    t0 = group * (COMB_TG // tm)
    return pl.pallas_call(
        _combine_dense_kernel,
        grid=(COMB_TG // tm,),
        in_specs=[
            pl.BlockSpec((TOP_K, tm, D // 2), lambda i: (0, i, 0)),
            pl.BlockSpec((tm, D), lambda i: (t0 + i, 0)),
            pl.BlockSpec((tm, LANES), lambda i: (t0 + i, 0)),
            pl.BlockSpec((1, D), lambda i: (0, 0)),
            pl.BlockSpec((1, D), lambda i: (0, 0)),
        ],
        out_specs=pl.BlockSpec((tm, D), lambda i: (t0 + i, 0)),
        out_shape=jax.ShapeDtypeStruct((T, D), F32),
        input_output_aliases={1: 0},
        compiler_params=pltpu.CompilerParams(
            dimension_semantics=("arbitrary",),
            vmem_limit_bytes=40 * MIB),
        name="moe_combine_dense_ln2",
    )(g4, h1, topw, g, b)


def _pad_cols(a, n):
    return jnp.pad(a, ((0, 0), (0, n - a.shape[1])))


def kernel(x, mem, ln_in_g, ln_in_b, ln_mem_g, ln_mem_b, w_in, b_in, w_decay_f, b_decay_f,
           w_decay_b, b_decay_b, gla_norm_g, w_br_gla, w_br_fnet, w_br_mem, w_mem_kv, w_out,
           b_out, ln1_g, ln1_b, w_router, b_router, w_gu, b_gu, w_down, b_down, ln2_g, ln2_b):
    assert x.shape == (BATCH, SEQ, D) and w_in.shape[0] == 1
    row = lambda a: a.reshape(1, -1)
    x2 = x.reshape(T, D)
    w_in0, b_in0 = w_in[0], b_in[0]
    c_lr, c_fn, c_mq, c_gt = 3072, 3072 + 2 * GLA_LR, 3104 + FN_W, 3104 + FN_W + MQ_W
    w_main = jnp.concatenate([w_in0[:, :c_lr], w_in0[:, c_gt:], w_in0[:, c_mq:c_gt]],
                             axis=1).astype(BF16)
    b_main = row(jnp.concatenate([b_in0[:c_lr], b_in0[c_gt:], b_in0[c_mq:c_gt]]))
    w_lr = _pad_cols(w_in0[:, c_lr:c_fn], LANES).astype(BF16)
    b_lr = _pad_cols(row(b_in0[c_lr:c_fn]), LANES)
    w_fn = w_in0[:, c_fn:c_mq].astype(BF16)
    b_fn = row(b_in0[c_fn:c_mq])
    lng, lnb = row(ln_in_g), row(ln_in_b)

    proj, lr = _inproj(x2, lng, lnb, w_main, b_main, w_lr, b_lr)

    zpad = jnp.zeros((LANES - 2 * GLA_LR, GLA_H * GLA_DK), F32)
    zlr = jnp.zeros((GLA_LR, GLA_H * GLA_DK), F32)
    wdf = jnp.concatenate([w_decay_f[0], zlr, zpad], axis=0).reshape(LANES, GLA_H, GLA_DK)
    wdb = jnp.concatenate([zlr, w_decay_b[0], zpad], axis=0).reshape(LANES, GLA_H, GLA_DK)
    wd = jnp.concatenate([wdf, wdb], axis=2).reshape(LANES, GLA_H * 2 * GLA_DK).astype(BF16)
    wd = jnp.concatenate([wd, wd], axis=0)
    bd = jnp.concatenate([b_decay_f[0].reshape(GLA_H, GLA_DK),
                          b_decay_b[0].reshape(GLA_H, GLA_DK)], axis=1).reshape(1, -1)
    og = _gla(proj, lr, wd, bd, row(gla_norm_g[0]))

    fbig, cwt, swt, f2, ccs, perm = _dft_tables(MERGE_TM)
    x4 = x.reshape(BATCH, FFT_N2, FFT_N1, D)
    z = _fft2(_fft1(x4, lng, lnb, w_fn, b_fn, fbig, cwt, swt), f2)

    kv = _memkv(mem.reshape(BATCH * MEM_LEN, D), row(ln_mem_g), row(ln_mem_b),
                w_mem_kv[0].astype(BF16))

    w_r = _pad_cols(w_router[0], LANES)
    wr_hi = w_r.astype(BF16)
    wr_lo = (w_r - wr_hi.astype(F32)).astype(BF16)
    b_r = jnp.concatenate([row(b_router[0]),
                           jnp.full((1, LANES - N_EXP), NEG_BIG, F32)], axis=1)
    h1, h1p, eidx, topw, cnt = _merge(
        x2, og, z, proj, proj, kv, lng, lnb,
        w_br_gla[0].astype(BF16), ccs, perm, w_br_fnet[0].astype(BF16),
        w_br_mem[0].astype(BF16), w_out[0].astype(BF16), row(b_out[0]),
        row(ln1_g[0]), row(ln1_b[0]), jnp.concatenate([wr_hi, wr_lo], axis=1), wr_hi, b_r)

    dest = _plan(eidx, cnt)
    counts = cnt[0, :N_EXP].astype(I32)
    dest_k = dest[:, :TOP_K]
    idx4 = dest_k.reshape(SC_WORKERS, SCD_NCH, SC_CH, TOP_K).transpose(0, 1, 3, 2)
    xs = _sc_dispatch(h1p, idx4.reshape(SC_WORKERS, SCD_NCH * TOP_K, SC_CH))
    ys = _experts(_work_items(counts), xs, w_gu[0], b_gu[0], w_down[0], b_down[0])
    out = h1
    for grp in range(COMB_GROUPS):
        dest_g = dest_k[grp * COMB_TG:(grp + 1) * COMB_TG].T.reshape(SC_WORKERS, SC_NCH, SC_CH)
        g4 = _sc_gather(ys, dest_g).reshape(TOP_K, COMB_TG, D // 2)
        out = _combine_dense(g4, out, topw, row(ln2_g[0]), row(ln2_b[0]), grp)
    return out.reshape(BATCH, SEQ, D)
```
